```python
import math
import jax, jax.numpy as jnp
from jax import lax
import numpy as np

D_MODEL = 1024
BATCH = 8
SEQ = 2048
DEPTH = 1
DEC_BATCH = 128
DEC_SEQ = 4
PAST_LEN = 16384
PAGE_SIZE = 128

RET_WIDTH = D_MODEL // 2
RET_HEADS = 4
RET_HEAD_DIM = RET_WIDTH // RET_HEADS
RET_CHUNK = 128
ROPE_BASE = 10000.0
SSM_WIDTH = D_MODEL - RET_WIDTH
SSM_GROUP = 16
SSM_GROUPS = SSM_WIDTH // SSM_GROUP
SSM_STATE = 64
DT_MIN = 1e-3
DT_MAX = 1e-1
MIX_WIDTH = RET_WIDTH + SSM_WIDTH
IN_WIDTH = 4 * RET_WIDTH + SSM_WIDTH
N_MOD = 6
N_EXPERTS = 32
TOP_K = 4
D_FF = D_MODEL
SWIGLU_LIMIT = 7.0
SWIGLU_ALPHA = 1.702
NORM_EPS = 1e-6

kernel_name = 'hybrid_retention_s5_moe_adaln_step'


def _rms_norm(x, w):
    xf = x.astype(jnp.float32)
    y = xf * lax.rsqrt(jnp.mean(xf * xf, axis=-1, keepdims=True) + NORM_EPS)
    return (y * w.astype(jnp.float32)).astype(x.dtype)


def _rotary(x, pos):
    half = x.shape[-1] // 2
    inv_freq = ROPE_BASE ** (-jnp.arange(half, dtype=jnp.float32) / half)
    ang = pos[:, None] * inv_freq[None, :]
    cos = jnp.cos(ang)[None, :, None, :]
    sin = jnp.sin(ang)[None, :, None, :]
    x1, x2 = x[..., :half], x[..., half:]
    return jnp.concatenate([x1 * cos - x2 * sin, x2 * cos + x1 * sin], axis=-1)


def _retention(q, k, v, s0):
    bt, seq_len, n_heads, _ = q.shape
    chunk = math.gcd(seq_len, RET_CHUNK)
    n_chunks = seq_len // chunk
    log_gamma = jnp.log1p(-jnp.exp2(-5.0 - jnp.arange(n_heads, dtype=jnp.float32)))
    idx = jnp.arange(chunk, dtype=jnp.float32)
    rel = idx[:, None] - idx[None, :]
    causal = rel >= 0
    decay_in = jnp.where(causal[None], jnp.exp(jnp.where(causal, rel, 0.0)[None] * log_gamma[:, None, None]), 0.0)
    cross_decay = jnp.exp((idx[:, None] + 1.0) * log_gamma[None, :])
    state_decay = jnp.exp((chunk - 1.0 - idx)[:, None] * log_gamma[None, :])
    chunk_decay = jnp.exp(chunk * log_gamma)

    def to_chunks(t):
        return jnp.moveaxis(t.reshape(bt, n_chunks, chunk, n_heads, t.shape[-1]), 1, 0)

    def step(state, qkv):
        qc, kc, vc = qkv
        scores = jnp.einsum('bihd,bjhd->bhij', qc, kc) * decay_in[None]
        inner = jnp.einsum('bhij,bjhe->bihe', scores, vc)
        cross = jnp.einsum('bihd,bhde->bihe', qc, state) * cross_decay[None, :, :, None]
        new_state = state * chunk_decay[None, :, None, None] + jnp.einsum('bjhd,bjhe,jh->bhde', kc, vc, state_decay)
        return new_state, inner + cross

    s_final, out = lax.scan(step, s0, (to_chunks(q), to_chunks(k), to_chunks(v)))
    out = jnp.moveaxis(out, 0, 1).reshape(bt, seq_len, n_heads, v.shape[-1])
    return out, s_final


def _s5(u, h0_re, h0_im, lam_re, lam_im, log_dt, b_re, b_im, c_re, c_im, d_skip):
    f32 = jnp.float32
    lam_re, lam_im = lam_re.astype(f32), lam_im.astype(f32)
    b_re, b_im, c_re, c_im = b_re.astype(f32), b_im.astype(f32), c_re.astype(f32), c_im.astype(f32)
    dt = jnp.exp(log_dt.astype(f32))[:, None]
    mag = jnp.exp(lam_re * dt)
    ang = lam_im * dt
    lb_re, lb_im = mag * jnp.cos(ang), mag * jnp.sin(ang)
    den = lam_re * lam_re + lam_im * lam_im
    f_re = ((lb_re - 1.0) * lam_re + lb_im * lam_im) / den
    f_im = (lb_im * lam_re - (lb_re - 1.0) * lam_im) / den
    bb_re = f_re[..., None] * b_re - f_im[..., None] * b_im
    bb_im = f_re[..., None] * b_im + f_im[..., None] * b_re
    bu_re = jnp.einsum('blgc,gpc->blgp', u, bb_re)
    bu_im = jnp.einsum('blgc,gpc->blgp', u, bb_im)
    bu_re = bu_re.at[:, 0].add(lb_re * h0_re - lb_im * h0_im)
    bu_im = bu_im.at[:, 0].add(lb_re * h0_im + lb_im * h0_re)
    a_re = jnp.broadcast_to(lb_re, bu_re.shape)
    a_im = jnp.broadcast_to(lb_im, bu_im.shape)

    def combine(e1, e2):
        ar1, ai1, br1, bi1 = e1
        ar2, ai2, br2, bi2 = e2
        return (ar2 * ar1 - ai2 * ai1, ar2 * ai1 + ai2 * ar1,
                ar2 * br1 - ai2 * bi1 + br2, ar2 * bi1 + ai2 * br1 + bi2)

    _, _, h_re, h_im = lax.associative_scan(combine, (a_re, a_im, bu_re, bu_im), axis=1)
    y = (jnp.einsum('blgp,gcp->blgc', h_re, c_re) - jnp.einsum('blgp,gcp->blgc', h_im, c_im)
         + d_skip.astype(f32) * u)
    return y, h_re[:, -1], h_im[:, -1]


def _moe(h, w_router, b_router, w1, b1, w2, b2):
    logits = (h @ w_router + b_router).astype(jnp.float32)
    top_val, top_idx = lax.top_k(logits, TOP_K)
    top_w = jax.nn.softmax(top_val, axis=-1)
    gate = jnp.einsum('tk,tke->te', top_w, jax.nn.one_hot(top_idx, N_EXPERTS, dtype=jnp.float32))
    out = jnp.zeros(h.shape, jnp.float32)
    for e in range(N_EXPERTS):
        hu = h @ w1[e] + b1[e]
        x_glu = jnp.minimum(hu[:, :D_FF], SWIGLU_LIMIT)
        x_lin = jnp.clip(hu[:, D_FF:], -SWIGLU_LIMIT, SWIGLU_LIMIT)
        act = x_glu * jax.nn.sigmoid(SWIGLU_ALPHA * x_glu) * (x_lin + 1.0)
        out = out + gate[:, e:e + 1] * (act @ w2[e] + b2[e]).astype(jnp.float32)
    return out.astype(h.dtype)


def _layer(x, c, pos, s_ret, s_re, s_im, lp):
    bt, seq_len, _ = x.shape
    f32 = jnp.float32
    mod = jax.nn.silu(c) @ lp['w_ada'] + lp['b_ada']
    sh1, sc1, g1, sh2, sc2, g2 = jnp.split(mod[:, None, :], N_MOD, axis=-1)
    h = _rms_norm(x, lp['norm1_w']) * (1.0 + sc1) + sh1
    z = h @ lp['w_in']
    q, k, v, g, u = jnp.split(z, [RET_WIDTH, 2 * RET_WIDTH, 3 * RET_WIDTH, 4 * RET_WIDTH], axis=-1)

    def heads(t):
        return t.reshape(bt, seq_len, RET_HEADS, RET_HEAD_DIM).astype(f32)

    qh = _rotary(heads(q), pos)
    kh = _rotary(heads(k), pos) * (RET_HEAD_DIM ** -0.5)
    o, s_ret_new = _retention(qh, kh, heads(v), s_ret.astype(f32))
    o = o * lax.rsqrt(jnp.mean(o * o, axis=-1, keepdims=True) + NORM_EPS)
    o = o.reshape(bt, seq_len, RET_WIDTH) * lp['ret_norm_w'].astype(f32) * jax.nn.silu(g.astype(f32))

    uf = u.astype(f32).reshape(bt, seq_len, SSM_GROUPS, SSM_GROUP)
    y, h_re, h_im = _s5(uf, s_re.astype(f32), s_im.astype(f32), lp['s5_lam_re'], lp['s5_lam_im'],
                        lp['s5_log_dt'], lp['s5_b_re'], lp['s5_b_im'], lp['s5_c_re'], lp['s5_c_im'], lp['s5_d'])
    y = jax.nn.gelu(y.reshape(bt, seq_len, SSM_WIDTH))
    y = y * jax.nn.sigmoid(y @ lp['w_glu'].astype(f32) + lp['b_glu'].astype(f32))
    y = _rms_norm(y, lp['s5_norm_w'])

    mix = jnp.concatenate([o, y], axis=-1).astype(x.dtype) @ lp['w_out']
    x = x + g1 * mix
    h2 = _rms_norm(x, lp['norm2_w']) * (1.0 + sc2) + sh2
    ff = _moe(h2.reshape(bt * seq_len, D_MODEL), lp['w_router'], lp['b_router'],
              lp['w1'], lp['b1'], lp['w2'], lp['b2']).reshape(bt, seq_len, D_MODEL)
    x = x + g2 * ff
    return x, s_ret_new, h_re, h_im


def _trunk(x, c, pos, s_ret, s_re, s_im, p, final_w):
    ret_states, re_states, im_states = [], [], []
    for l in range(DEPTH):
        lp = {name: arr[l] for name, arr in p.items()}
        x, sr, hr, hi = _layer(x, c, pos, s_ret[l], s_re[l], s_im[l], lp)
        ret_states.append(sr)
        re_states.append(hr)
        im_states.append(hi)
    return _rms_norm(x, final_w), jnp.stack(ret_states), jnp.stack(re_states), jnp.stack(im_states)


def setup_inputs(seed: int = 0) -> dict:
    key = jax.random.key(seed)
    ks = jax.random.split(key, 40)
    f32 = jnp.float32

    def nrm(k, shape, scale):
        return jax.random.normal(k, shape, f32) * scale

    def gain(k, shape):
        return 1.0 + 0.02 * jax.random.normal(k, shape, f32)

    n_idx = jnp.arange(SSM_STATE, dtype=f32)
    return {
        'x_prompt': nrm(ks[0], (BATCH, SEQ, D_MODEL), 1.0),
        'x_sample': nrm(ks[1], (DEC_BATCH, DEC_SEQ, D_MODEL), 1.0),
        'c_prompt': nrm(ks[2], (BATCH, D_MODEL), 1.0),
        'c_sample': nrm(ks[3], (DEC_BATCH, D_MODEL), 1.0),
        'state_ret': nrm(ks[4], (DEPTH, DEC_BATCH, RET_HEADS, RET_HEAD_DIM, RET_HEAD_DIM), 0.5),
        'state_s5_re': nrm(ks[5], (DEPTH, DEC_BATCH, SSM_GROUPS, SSM_STATE), 0.1),
        'state_s5_im': nrm(ks[6], (DEPTH, DEC_BATCH, SSM_GROUPS, SSM_STATE), 0.1),
        'norm1_w': gain(ks[7], (DEPTH, D_MODEL)),
        'norm2_w': gain(ks[8], (DEPTH, D_MODEL)),
        'w_ada': nrm(ks[9], (DEPTH, D_MODEL, N_MOD * D_MODEL), 0.5 * D_MODEL ** -0.5),
        'b_ada': nrm(ks[10], (DEPTH, N_MOD * D_MODEL), 0.01),
        'w_in': nrm(ks[11], (DEPTH, D_MODEL, IN_WIDTH), D_MODEL ** -0.5),
        'ret_norm_w': gain(ks[12], (DEPTH, RET_WIDTH)),
        's5_lam_re': -0.5 + nrm(ks[13], (DEPTH, SSM_GROUPS, SSM_STATE), 0.01),
        's5_lam_im': math.pi * n_idx + nrm(ks[14], (DEPTH, SSM_GROUPS, SSM_STATE), 0.01),
        's5_log_dt': jax.random.uniform(ks[15], (DEPTH, SSM_GROUPS), f32, math.log(DT_MIN), math.log(DT_MAX)),
        's5_b_re': nrm(ks[16], (DEPTH, SSM_GROUPS, SSM_STATE, SSM_GROUP), (2 * SSM_GROUP) ** -0.5),
        's5_b_im': nrm(ks[17], (DEPTH, SSM_GROUPS, SSM_STATE, SSM_GROUP), (2 * SSM_GROUP) ** -0.5),
        's5_c_re': nrm(ks[18], (DEPTH, SSM_GROUPS, SSM_GROUP, SSM_STATE), SSM_STATE ** -0.5),
        's5_c_im': nrm(ks[19], (DEPTH, SSM_GROUPS, SSM_GROUP, SSM_STATE), SSM_STATE ** -0.5),
        's5_d': nrm(ks[20], (DEPTH, SSM_GROUPS, SSM_GROUP), 0.5),
        'w_glu': nrm(ks[21], (DEPTH, SSM_WIDTH, SSM_WIDTH), SSM_WIDTH ** -0.5),
        'b_glu': nrm(ks[22], (DEPTH, SSM_WIDTH), 0.01),
        's5_norm_w': gain(ks[23], (DEPTH, SSM_WIDTH)),
        'w_out': nrm(ks[24], (DEPTH, MIX_WIDTH, D_MODEL), MIX_WIDTH ** -0.5),
        'w_router': nrm(ks[25], (DEPTH, D_MODEL, N_EXPERTS), D_MODEL ** -0.5),
        'b_router': nrm(ks[26], (DEPTH, N_EXPERTS), 0.01),
        'w1': nrm(ks[27], (DEPTH, N_EXPERTS, D_MODEL, 2 * D_FF), D_MODEL ** -0.5),
        'b1': nrm(ks[28], (DEPTH, N_EXPERTS, 2 * D_FF), 0.01),
        'w2': nrm(ks[29], (DEPTH, N_EXPERTS, D_FF, D_MODEL), D_FF ** -0.5),
        'b2': nrm(ks[30], (DEPTH, N_EXPERTS, D_MODEL), 0.01),
        'final_w': gain(ks[31], (D_MODEL,)),
    }


def reference(x_prompt, x_sample, c_prompt, c_sample, state_ret, state_s5_re, state_s5_im,
              norm1_w, norm2_w, w_ada, b_ada, w_in, ret_norm_w, s5_lam_re, s5_lam_im, s5_log_dt,
              s5_b_re, s5_b_im, s5_c_re, s5_c_im, s5_d, w_glu, b_glu, s5_norm_w, w_out,
              w_router, b_router, w1, b1, w2, b2, final_w):
    p = {
        'norm1_w': norm1_w, 'norm2_w': norm2_w, 'w_ada': w_ada, 'b_ada': b_ada, 'w_in': w_in,
        'ret_norm_w': ret_norm_w, 's5_lam_re': s5_lam_re, 's5_lam_im': s5_lam_im, 's5_log_dt': s5_log_dt,
        's5_b_re': s5_b_re, 's5_b_im': s5_b_im, 's5_c_re': s5_c_re, 's5_c_im': s5_c_im, 's5_d': s5_d,
        'w_glu': w_glu, 'b_glu': b_glu, 's5_norm_w': s5_norm_w, 'w_out': w_out,
        'w_router': w_router, 'b_router': b_router, 'w1': w1, 'b1': b1, 'w2': w2, 'b2': b2,
    }
    bp, lp_len = x_prompt.shape[0], x_prompt.shape[1]
    ls = x_sample.shape[1]
    pos_prompt = jnp.arange(lp_len, dtype=jnp.float32)
    pos_sample = PAST_LEN + jnp.arange(ls, dtype=jnp.float32)
    zero_ret = jnp.zeros((DEPTH, bp, RET_HEADS, RET_HEAD_DIM, RET_HEAD_DIM), jnp.float32)
    zero_s5 = jnp.zeros((DEPTH, bp, SSM_GROUPS, SSM_STATE), jnp.float32)
    y_prompt, ret_p, s5re_p, s5im_p = _trunk(x_prompt, c_prompt, pos_prompt, zero_ret, zero_s5, zero_s5, p, final_w)
    y_sample, ret_s, s5re_s, s5im_s = _trunk(x_sample, c_sample, pos_sample, state_ret, state_s5_re, state_s5_im, p, final_w)
    return (y_prompt, y_sample, ret_p, s5re_p, s5im_p, ret_s, s5re_s, s5im_s)
```

```python
import functools
import math

import jax
import jax.numpy as jnp
from jax import lax
from jax.experimental import pallas as pl
from jax.experimental.pallas import tpu as pltpu

F32 = jnp.float32
BF16 = jnp.bfloat16
HIGHEST = lax.Precision.HIGHEST

D_MODEL = 1024
PAST_LEN = 16384
RET_WIDTH = 512
RET_HEADS = 4
HEAD_DIM = 128
ROPE_BASE = 10000.0
SSM_WIDTH = 512
SSM_GROUP = 16
SSM_GROUPS = 32
SSM_STATE = 64
SSM_CH = SSM_GROUPS * SSM_STATE
IN_WIDTH = 4 * RET_WIDTH + SSM_WIDTH
N_EXPERTS = 32
TOP_K = 4
D_FF = 1024
SWIGLU_LIMIT = 7.0
SWIGLU_ALPHA = 1.702
NORM_EPS = 1e-6

LANES = 128
SUBLANES = 8
VMEM_LIMIT = 56 * 1024 * 1024

SEQ_PER_BLOCK = 8
PROMPT_CHUNK = 64
S5_BLOCK_GROUPS = 8
N_S5_BLOCKS = SSM_GROUPS // S5_BLOCK_GROUPS
S5_BLOCK_IN = S5_BLOCK_GROUPS * SSM_GROUP
S5_BLOCK_CH = S5_BLOCK_GROUPS * SSM_STATE
ROUTER_TILE = 512
SLOT_TILE = 256
DMA_ROWS_PER_STEP = 2048
COMBINE_TILE = 256


def _silu(x):
    return x * jax.nn.sigmoid(x)


def _ada_kernel(c_ref, w_ref, b_ref, o_ref):
    s = _silu(c_ref[...])
    o_ref[...] = jnp.dot(s, w_ref[...], precision=HIGHEST, preferred_element_type=F32) + b_ref[...]


def _ada(c_all, w_ada, b_ada):
    n_rows, n_out = c_all.shape[0], w_ada.shape[1]
    tn = 1536
    return pl.pallas_call(
        _ada_kernel,
        grid=(n_out // tn,),
        in_specs=[
            pl.BlockSpec((n_rows, D_MODEL), lambda j: (0, 0)),
            pl.BlockSpec((D_MODEL, tn), lambda j: (0, j)),
            pl.BlockSpec((1, tn), lambda j: (0, j)),
        ],
        out_specs=pl.BlockSpec((n_rows, tn), lambda j: (0, j)),
        out_shape=jax.ShapeDtypeStruct((n_rows, n_out), F32),
        compiler_params=pltpu.CompilerParams(dimension_semantics=("arbitrary",), vmem_limit_bytes=VMEM_LIMIT),
        name="ada",
    )(c_all, w_ada, b_ada.reshape(1, n_out))


def _s5prep_kernel(lre_ref, lim_ref, ldt_ref, bre_ref, bim_ref, lbr_ref, lbi_ref, bbr_ref, bbi_ref):
    lam_re, lam_im = lre_ref[...], lim_ref[...]
    dt = jnp.exp(ldt_ref[...])
    mag = jnp.exp(lam_re * dt)
    ang = lam_im * dt
    lb_re, lb_im = mag * jnp.cos(ang), mag * jnp.sin(ang)
    den = lam_re * lam_re + lam_im * lam_im
    f_re = ((lb_re - 1.0) * lam_re + lb_im * lam_im) / den
    f_im = (lb_im * lam_re - (lb_re - 1.0) * lam_im) / den
    lbr_ref[...] = lb_re
    lbi_ref[...] = lb_im
    b_re, b_im = bre_ref[...], bim_ref[...]
    bbr_ref[...] = f_re[:, None, :] * b_re - f_im[:, None, :] * b_im
    bbi_ref[...] = f_re[:, None, :] * b_im + f_im[:, None, :] * b_re


def _s5prep(lam_re, lam_im, log_dt, b_re, b_im):
    g, p = lam_re.shape
    bt_re = jnp.transpose(b_re, (0, 2, 1))
    bt_im = jnp.transpose(b_im, (0, 2, 1))
    return pl.pallas_call(
        _s5prep_kernel,
        out_shape=(
            jax.ShapeDtypeStruct((g, p), F32), jax.ShapeDtypeStruct((g, p), F32),
            jax.ShapeDtypeStruct((g, SSM_GROUP, p), F32), jax.ShapeDtypeStruct((g, SSM_GROUP, p), F32),
        ),
        name="s5prep",
    )(lam_re, lam_im, log_dt.reshape(g, 1), bt_re, bt_im)


def _block_diag(blocks):
    _, r, c = blocks.shape
    b4 = blocks.reshape(N_S5_BLOCKS, S5_BLOCK_GROUPS, r, c)
    eye = jnp.eye(S5_BLOCK_GROUPS, dtype=blocks.dtype)
    out = b4[:, :, :, None, :] * eye[None, :, None, :, None]
    return out.reshape(N_S5_BLOCKS, S5_BLOCK_GROUPS * r, S5_BLOCK_GROUPS * c)


def _mixer_kernel(x_ref, mod_ref, n1w_ref, win_ref, cos_ref, sin_ref, dmask_ref, cdec_ref, sdec_ref,
                  rnw_ref, bmat_ref, cre_ref, cim_ref, lbr_ref, lbi_ref, dsk_ref, wglu_ref, bglu_ref,
                  snw_ref, wout_ref, sret0_ref, sre0_ref, sim0_ref,
                  x1_ref, sret_ref, sre_ref, sim_ref,
                  hb_ref, z_ref, zu_ref, oy_ref, utb_ref, bur_ref, bui_ref, ytb_ref, yb_ref,
                  *, n_seq, chunk, tile_rows, carry, chunk_decay):
    rows = n_seq * chunk
    seq_per_tile = tile_rows // chunk
    n_tiles = rows // tile_rows
    per_row_mod = mod_ref.shape[0] == rows

    def load_states():
        sret_ref[...] = sret0_ref[...]
        sre_ref[...] = sre0_ref[...]
        sim_ref[...] = sim0_ref[...]

    if carry:
        pl.when(pl.program_id(0) == 0)(load_states)
    else:
        load_states()

    n1w = n1w_ref[...]
    mod_rows = rows if per_row_mod else chunk
    for i in range(rows // mod_rows):
        r0 = i * mod_rows
        xb = _load_rows(x_ref, r0, mod_rows, chunk)
        if per_row_mod:
            sh = mod_ref[pl.ds(r0, mod_rows), pl.ds(0, D_MODEL)]
            sc = mod_ref[pl.ds(r0, mod_rows), pl.ds(D_MODEL, D_MODEL)]
        else:
            sh = mod_ref[pl.ds(i, 1), pl.ds(0, D_MODEL)]
            sc = mod_ref[pl.ds(i, 1), pl.ds(D_MODEL, D_MODEL)]
        ms = jnp.mean(xb * xb, axis=-1, keepdims=True)
        hn = xb * lax.rsqrt(ms + NORM_EPS) * n1w
        hb_ref[pl.ds(r0, mod_rows), :] = (hn * (1.0 + sc) + sh).astype(BF16)
    ret_w = 4 * RET_WIDTH
    z_ref[...] = jnp.dot(hb_ref[...], win_ref[:, pl.ds(0, ret_w)], preferred_element_type=F32)
    zu = jnp.dot(hb_ref[...], win_ref[:, pl.ds(ret_w, SSM_WIDTH)], preferred_element_type=F32)
    for c in range(SSM_WIDTH // LANES):
        zu_ref[c] = zu[:, c * LANES:(c + 1) * LANES]

    cos = cos_ref[...]
    sin = sin_ref[...]
    scale = HEAD_DIM ** -0.5
    if seq_per_tile > 1:
        row_id = lax.broadcasted_iota(jnp.int32, (tile_rows, HEAD_DIM), 0)

    def rope(t):
        return t * cos + pltpu.roll(t, HEAD_DIM // 2, 1) * sin

    def ret_tile(ti, c):
        r0 = pl.multiple_of(ti * tile_rows, tile_rows)
        for h in range(RET_HEADS):
            c0 = h * HEAD_DIM
            q = rope(z_ref[pl.ds(r0, tile_rows), pl.ds(c0, HEAD_DIM)])
            k = rope(z_ref[pl.ds(r0, tile_rows), pl.ds(RET_WIDTH + c0, HEAD_DIM)]) * scale
            v = z_ref[pl.ds(r0, tile_rows), pl.ds(2 * RET_WIDTH + c0, HEAD_DIM)]
            g = z_ref[pl.ds(r0, tile_rows), pl.ds(3 * RET_WIDTH + c0, HEAD_DIM)]
            kd = k * sdec_ref[h]
            if tile_rows < HEAD_DIM:
                pad = jnp.zeros((HEAD_DIM - tile_rows, HEAD_DIM), F32)
                k, v, kd = (jnp.concatenate([t, pad], axis=0) for t in (k, v, kd))
                if seq_per_tile > 1:
                    row_kv = lax.broadcasted_iota(jnp.int32, (HEAD_DIM, HEAD_DIM), 0)
            elif seq_per_tile > 1:
                row_kv = row_id
            qb, kb, vb = q.astype(BF16), k.astype(BF16), v.astype(BF16)
            s = lax.dot_general(qb, kb, (((1,), (1,)), ((), ())), preferred_element_type=F32) * dmask_ref[h]
            o = jnp.dot(s.astype(BF16), vb, preferred_element_type=F32)
            cross = None
            for si in range(seq_per_tile):
                sidx = ti * seq_per_tile + si
                st = sret_ref[sidx, h]
                cr = jnp.dot(qb, st.astype(BF16), preferred_element_type=F32)
                if seq_per_tile > 1:
                    in_seq = (row_id >= si * chunk) & (row_id < (si + 1) * chunk)
                    cross = jnp.where(in_seq, cr, 0.0 if cross is None else cross)
                    kds = jnp.where((row_kv >= si * chunk) & (row_kv < (si + 1) * chunk), kd, 0.0)
                else:
                    cross, kds = cr, kd
                upd = lax.dot_general(kds.astype(BF16), vb, (((0,), (0,)), ((), ())), preferred_element_type=F32)
                sret_ref[sidx, h] = st * chunk_decay[h] + upd
            o = o + cross * cdec_ref[h]
            o = o * lax.rsqrt(jnp.mean(o * o, axis=-1, keepdims=True) + NORM_EPS)
            o = o * rnw_ref[:, pl.ds(c0, HEAD_DIM)] * _silu(g)
            oy_ref[pl.ds(r0, tile_rows), pl.ds(c0, HEAD_DIM)] = o
        return c

    lax.fori_loop(0, n_tiles, ret_tile, 0)

    for t in range(chunk):
        for c in range(SSM_WIDTH // LANES):
            utb_ref[pl.ds(t * n_seq, n_seq), pl.ds(c * LANES, LANES)] = zu_ref[c, pl.ds(t, n_seq, stride=chunk), :]
    for blk in range(N_S5_BLOCKS):
        ub = utb_ref[:, pl.ds(blk * S5_BLOCK_IN, S5_BLOCK_IN)].astype(BF16)
        bu = jnp.dot(ub, bmat_ref[blk], preferred_element_type=F32)
        bur_ref[:, pl.ds(blk * S5_BLOCK_CH, S5_BLOCK_CH)] = bu[:, :S5_BLOCK_CH]
        bui_ref[:, pl.ds(blk * S5_BLOCK_CH, S5_BLOCK_CH)] = bu[:, S5_BLOCK_CH:]

    scan_w = 2 * S5_BLOCK_CH
    for p in range(SSM_CH // scan_w):
        cols = pl.ds(p * scan_w, scan_w)
        lbr = jnp.broadcast_to(lbr_ref[:, cols], (n_seq, scan_w))
        lbi = jnp.broadcast_to(lbi_ref[:, cols], (n_seq, scan_w))

        def scan_step(t, hc, cols=cols, lbr=lbr, lbi=lbi):
            hr, hi = hc
            r0 = pl.multiple_of(t * n_seq, n_seq)
            nr = lbr * hr - lbi * hi + bur_ref[pl.ds(r0, n_seq), cols]
            ni = lbr * hi + lbi * hr + bui_ref[pl.ds(r0, n_seq), cols]
            bur_ref[pl.ds(r0, n_seq), cols] = nr
            bui_ref[pl.ds(r0, n_seq), cols] = ni
            return nr, ni

        h0 = (sre_ref[:, cols], sim_ref[:, cols])
        if chunk <= 8:
            hc = h0
            for t in range(chunk):
                hc = scan_step(t, hc)
        else:
            unroll = 4

            def scan_group(tg, hc):
                for j in range(unroll):
                    hc = scan_step(tg * unroll + j, hc)
                return hc

            hc = lax.fori_loop(0, chunk // unroll, scan_group, h0)
        sre_ref[:, cols] = hc[0]
        sim_ref[:, cols] = hc[1]

    for blk in range(N_S5_BLOCKS):
        cols = pl.ds(blk * S5_BLOCK_CH, S5_BLOCK_CH)
        yb = jnp.dot(bur_ref[:, cols].astype(BF16), cre_ref[blk], preferred_element_type=F32)
        yb = yb + jnp.dot(bui_ref[:, cols].astype(BF16), cim_ref[blk], preferred_element_type=F32)
        ucols = pl.ds(blk * S5_BLOCK_IN, S5_BLOCK_IN)
        ytb_ref[:, ucols] = yb + dsk_ref[:, ucols] * utb_ref[:, ucols]
    for t in range(chunk):
        for c in range(SSM_WIDTH // LANES):
            yb_ref[c, pl.ds(t, n_seq, stride=chunk), :] = ytb_ref[pl.ds(t * n_seq, n_seq), pl.ds(c * LANES, LANES)]

    y = jnp.concatenate([yb_ref[c] for c in range(SSM_WIDTH // LANES)], axis=1)
    y = jax.nn.gelu(y, approximate=True)
    gate = jnp.dot(y.astype(BF16), wglu_ref[...], preferred_element_type=F32) + bglu_ref[...]
    y = y * jax.nn.sigmoid(gate)
    y = y * lax.rsqrt(jnp.mean(y * y, axis=-1, keepdims=True) + NORM_EPS) * snw_ref[...]
    oy_ref[:, pl.ds(RET_WIDTH, SSM_WIDTH)] = y

    mix = jnp.dot(oy_ref[...].astype(BF16), wout_ref[...], preferred_element_type=F32)
    for i in range(rows // mod_rows):
        r0 = i * mod_rows
        if per_row_mod:
            g1 = mod_ref[pl.ds(r0, mod_rows), pl.ds(2 * D_MODEL, D_MODEL)]
        else:
            g1 = mod_ref[pl.ds(i, 1), pl.ds(2 * D_MODEL, D_MODEL)]
        _store_rows(x1_ref, r0, mod_rows, chunk,
                    _load_rows(x_ref, r0, mod_rows, chunk) + g1 * mix[r0:r0 + mod_rows])


def _load_rows(ref, r0, n, chunk):
    if len(ref.shape) == 2:
        return ref[pl.ds(r0, n), :]
    assert n == chunk and r0 % chunk == 0
    return ref[r0 // chunk]


def _store_rows(ref, r0, n, chunk, val):
    if len(ref.shape) == 2:
        ref[pl.ds(r0, n), :] = val
    else:
        assert n == chunk and r0 % chunk == 0
        ref[r0 // chunk] = val


def _const_spec(shape):
    nd = len(shape)
    return pl.BlockSpec(shape, lambda j, _n=nd: (0,) * _n)


def _decay_tables(chunk, tile_rows):
    log_gamma = jnp.log1p(-jnp.exp2(-5.0 - jnp.arange(RET_HEADS, dtype=F32)))
    r = jnp.arange(tile_rows)
    seq, loc = r // chunk, (r % chunk).astype(F32)
    rel = loc[:, None] - loc[None, :]
    ok = (seq[:, None] == seq[None, :]) & (rel >= 0)
    dmask = jnp.where(ok[None], jnp.exp(jnp.where(ok, rel, 0.0)[None] * log_gamma[:, None, None]), 0.0)
    if tile_rows < HEAD_DIM:
        dmask = jnp.pad(dmask, ((0, 0), (0, 0), (0, HEAD_DIM - tile_rows)))
    cdec = jnp.exp((loc[None, :] + 1.0) * log_gamma[:, None])
    sdec = jnp.exp((chunk - 1.0 - loc)[None, :] * log_gamma[:, None])
    bcast = lambda t: jnp.broadcast_to(t[:, :, None], (RET_HEADS, tile_rows, HEAD_DIM))
    return dmask, bcast(cdec), bcast(sdec), log_gamma


def _rope_tables(pos):
    half = HEAD_DIM // 2
    inv_freq = ROPE_BASE ** (-jnp.arange(half, dtype=F32) / half)
    ang = pos[:, None] * inv_freq[None, :]
    cos, sin = jnp.cos(ang), jnp.sin(ang)
    return jnp.concatenate([cos, cos], axis=-1), jnp.concatenate([-sin, sin], axis=-1)


def _mixer(x, mod, pos, states, wts, *, prompt):
    n_seq = SEQ_PER_BLOCK
    if prompt:
        n_total, seq_len, _ = x.shape
        assert n_total == n_seq
        chunk, tile_rows, n_steps = PROMPT_CHUNK, PROMPT_CHUNK, seq_len // PROMPT_CHUNK
        x_spec = pl.BlockSpec((n_seq, chunk, D_MODEL), lambda j: (0, j, 0))
        mod_spec = _const_spec(mod.shape)
        tab_spec = pl.BlockSpec((chunk, HEAD_DIM), lambda j: (j, 0))
        seq_map = lambda j: 0
    else:
        chunk = pos.shape[0]
        tile_rows = SUBLANES
        n_total = x.shape[0] // chunk
        n_steps = n_total // n_seq
        x_spec = pl.BlockSpec((n_seq * chunk, D_MODEL), lambda j: (j, 0))
        mod_spec = pl.BlockSpec((n_seq * chunk, mod.shape[1]), lambda j: (j, 0))
        tab_spec = _const_spec((tile_rows, HEAD_DIM))
        seq_map = lambda j: j
    rows = n_seq * chunk
    cos, sin = _rope_tables(pos)
    if not prompt:
        reps = tile_rows // chunk
        cos, sin = jnp.tile(cos, (reps, 1)), jnp.tile(sin, (reps, 1))
    dmask, cdec, sdec, log_gamma = _decay_tables(chunk, tile_rows)
    chunk_decay = tuple(float(math.exp(chunk * math.log1p(-2.0 ** (-5.0 - h)))) for h in range(RET_HEADS))
    del log_gamma
    sret0, sre0, sim0 = states

    st_ret_spec = pl.BlockSpec((n_seq, RET_HEADS, HEAD_DIM, HEAD_DIM), lambda j: (seq_map(j), 0, 0, 0))
    st_s5_spec = pl.BlockSpec((n_seq, SSM_CH), lambda j: (seq_map(j), 0))
    consts = [dmask, cdec, sdec, wts["rnw"], wts["bmat"], wts["cre"], wts["cim"], wts["lbr"], wts["lbi"],
              wts["dsk"], wts["w_glu"], wts["b_glu"], wts["snw"], wts["w_out"]]
    args = [x, mod, wts["n1w"], wts["w_in"], cos, sin] + consts + [sret0, sre0, sim0]
    in_specs = ([x_spec, mod_spec, _const_spec(wts["n1w"].shape), _const_spec(wts["w_in"].shape), tab_spec, tab_spec]
                + [_const_spec(a.shape) for a in consts] + [st_ret_spec, st_s5_spec, st_s5_spec])

    kern = functools.partial(_mixer_kernel, n_seq=n_seq, chunk=chunk, tile_rows=tile_rows, carry=prompt,
                             chunk_decay=chunk_decay)
    out_shape = (
        jax.ShapeDtypeStruct(x.shape, F32),
        jax.ShapeDtypeStruct((n_total, RET_HEADS, HEAD_DIM, HEAD_DIM), F32),
        jax.ShapeDtypeStruct((n_total, SSM_CH), F32),
        jax.ShapeDtypeStruct((n_total, SSM_CH), F32),
    )
    scratch = [
        pltpu.VMEM((rows, D_MODEL), BF16),
        pltpu.VMEM((rows, 4 * RET_WIDTH), F32),
        pltpu.VMEM((SSM_WIDTH // LANES, rows, LANES), F32),
        pltpu.VMEM((rows, D_MODEL), F32),
        pltpu.VMEM((rows, SSM_WIDTH), F32),
        pltpu.VMEM((rows, SSM_CH), F32),
        pltpu.VMEM((rows, SSM_CH), F32),
        pltpu.VMEM((rows, SSM_WIDTH), F32),
        pltpu.VMEM((SSM_WIDTH // LANES, rows, LANES), F32),
    ]
    return pl.pallas_call(
        kern,
        grid=(n_steps,),
        in_specs=in_specs,
        out_specs=(x_spec, st_ret_spec, st_s5_spec, st_s5_spec),
        out_shape=out_shape,
        scratch_shapes=scratch,
        compiler_params=pltpu.CompilerParams(dimension_semantics=("arbitrary",), vmem_limit_bytes=VMEM_LIMIT),
        name="mixer_prompt" if prompt else "mixer_sample",
    )(*args)


def _route_tile(x, sh, sc, n2w_ref, wr_ref, br_ref, h2_ref, topi_ref, topw_ref):
    ms = jnp.mean(x * x, axis=-1, keepdims=True)
    h2 = x * lax.rsqrt(ms + NORM_EPS) * n2w_ref[...] * (1.0 + sc) + sh
    h2_ref[...] = h2
    logits = jnp.dot(h2, wr_ref[...], precision=HIGHEST, preferred_element_type=F32) + br_ref[...]
    lane = lax.broadcasted_iota(jnp.int32, logits.shape, 1)
    work = logits
    vals, idxs = [], []
    for _ in range(TOP_K):
        m = jnp.max(work, axis=-1, keepdims=True)
        idx = jnp.min(jnp.where(work == m, lane, LANES), axis=-1, keepdims=True)
        vals.append(m)
        idxs.append(idx)
        work = jnp.where(lane == idx, -jnp.inf, work)
    exps = [jnp.exp(v - vals[0]) for v in vals]
    tot = exps[0] + exps[1] + exps[2] + exps[3]
    topi = jnp.zeros(logits.shape, jnp.int32)
    topw = jnp.zeros(logits.shape, F32)
    for k in range(TOP_K):
        topi = jnp.where(lane == k, idxs[k], topi)
        topw = jnp.where(lane == k, exps[k] / tot, topw)
    topi_ref[...] = topi
    topw_ref[...] = topw


def _router_kernel(xp_ref, modp_ref, xs_ref, mods_ref, n2w_ref, wr_ref, br_ref, h2_ref, topi_ref, topw_ref,
                   *, n_prompt_tiles):
    i = pl.program_id(0)
    consts_outs = (n2w_ref, wr_ref, br_ref, h2_ref, topi_ref, topw_ref)

    @pl.when(i < n_prompt_tiles)
    def _():
        _route_tile(xp_ref[...], modp_ref[0, :, pl.ds(0, D_MODEL)], modp_ref[0, :, pl.ds(D_MODEL, D_MODEL)],
                    *consts_outs)

    @pl.when(i >= n_prompt_tiles)
    def _():
        _route_tile(xs_ref[...], mods_ref[:, pl.ds(0, D_MODEL)], mods_ref[:, pl.ds(D_MODEL, D_MODEL)],
                    *consts_outs)


def _router(xp_rows, mod_p, seq_len, xs_rows, mod_s, n2w, wr_pad, br_pad):
    tile = ROUTER_TILE
    n_p, n_s = xp_rows.shape[0], xs_rows.shape[0]
    tp, ts = n_p // tile, n_s // tile
    n_total = n_p + n_s
    mod_p = mod_p.reshape(mod_p.shape[0], 1, mod_p.shape[1])
    clamp_p = lambda i: jnp.minimum(i, tp - 1)
    clamp_s = lambda i: jnp.maximum(i - tp, 0)
    return pl.pallas_call(
        functools.partial(_router_kernel, n_prompt_tiles=tp),
        grid=(tp + ts,),
        in_specs=[pl.BlockSpec((tile, D_MODEL), lambda i: (clamp_p(i), 0)),
                  pl.BlockSpec((1, 1, mod_p.shape[2]), lambda i: ((clamp_p(i) * tile) // seq_len, 0, 0)),
                  pl.BlockSpec((tile, D_MODEL), lambda i: (clamp_s(i), 0)),
                  pl.BlockSpec((tile, mod_s.shape[1]), lambda i: (clamp_s(i), 0)),
                  _const_spec(n2w.shape), _const_spec(wr_pad.shape), _const_spec(br_pad.shape)],
        out_specs=(pl.BlockSpec((tile, D_MODEL), lambda i: (i, 0)),
                   pl.BlockSpec((tile, LANES), lambda i: (i, 0)),
                   pl.BlockSpec((tile, LANES), lambda i: (i, 0))),
        out_shape=(jax.ShapeDtypeStruct((n_total, D_MODEL), F32),
                   jax.ShapeDtypeStruct((n_total, LANES), jnp.int32),
                   jax.ShapeDtypeStruct((n_total, LANES), F32)),
        compiler_params=pltpu.CompilerParams(dimension_semantics=("arbitrary",), vmem_limit_bytes=VMEM_LIMIT),
        name="router",
    )(xp_rows, mod_p, xs_rows, mod_s, n2w, wr_pad, br_pad)


def _row_copy(src_ref, dst_ref, src_row, dst_row, sem):
    return pltpu.make_async_copy(src_ref.at[pl.ds(src_row, 1)], dst_ref.at[pl.ds(dst_row, 1)], sem)


def _gather_rows_kernel(idx_ref, src_ref, dst_ref, sem):
    base = pl.program_id(0) * DMA_ROWS_PER_STEP

    def issue(j, c):
        _row_copy(src_ref, dst_ref, idx_ref[0, 0, j], base + j, sem).start()
        return c

    lax.fori_loop(0, DMA_ROWS_PER_STEP, issue, 0)

    def drain(j, c):
        _row_copy(src_ref, dst_ref, idx_ref[0, 0, j], base + j, sem).wait()
        return c

    lax.fori_loop(0, DMA_ROWS_PER_STEP, drain, 0)


def _scatter_rows_kernel(idx_ref, src_ref, dst_ref, sem):
    base = pl.program_id(0) * DMA_ROWS_PER_STEP

    def issue(j, c):
        d = idx_ref[0, 0, j]

        @pl.when(d >= 0)
        def _():
            _row_copy(src_ref, dst_ref, base + j, d, sem).start()

        return c

    lax.fori_loop(0, DMA_ROWS_PER_STEP, issue, 0)

    def drain(j, c):
        d = idx_ref[0, 0, j]

        @pl.when(d >= 0)
        def _():
            _row_copy(src_ref, dst_ref, base + j, d, sem).wait()

        return c

    lax.fori_loop(0, DMA_ROWS_PER_STEP, drain, 0)


def _move_rows(kernel, idx, src, n_out, name):
    n = idx.shape[0]
    steps = n // DMA_ROWS_PER_STEP
    return pl.pallas_call(
        kernel,
        grid=(steps,),
        in_specs=[pl.BlockSpec((1, 1, DMA_ROWS_PER_STEP), lambda i: (i, 0, 0), memory_space=pltpu.SMEM),
                  pl.BlockSpec(memory_space=pl.ANY)],
        out_specs=pl.BlockSpec(memory_space=pl.ANY),
        out_shape=jax.ShapeDtypeStruct((n_out, src.shape[1]), src.dtype),
        scratch_shapes=[pltpu.SemaphoreType.DMA(())],
        compiler_params=pltpu.CompilerParams(dimension_semantics=("arbitrary",)),
        name=name,
    )(idx.reshape(steps, 1, DMA_ROWS_PER_STEP), src)


def _experts_kernel(te_ref, nv_ref, xs_ref, w1_ref, b1_ref, w2_ref, b2_ref, ys_ref, w1b_ref, w2b_ref):
    i = pl.program_id(0)
    prev = te_ref[jnp.maximum(i - 1, 0)]
    new_expert = (i == 0) | (te_ref[i] != prev)

    @pl.when(new_expert)
    def _():
        w1b_ref[...] = w1_ref[0].astype(BF16)
        w2b_ref[...] = w2_ref[0].astype(BF16)

    @pl.when(i < nv_ref[0])
    def _():
        x = xs_ref[...].astype(BF16)
        hu = jnp.dot(x, w1b_ref[...], preferred_element_type=F32) + b1_ref[0]
        x_glu = jnp.minimum(hu[:, :D_FF], SWIGLU_LIMIT)
        x_lin = jnp.clip(hu[:, D_FF:], -SWIGLU_LIMIT, SWIGLU_LIMIT)
        act = x_glu * jax.nn.sigmoid(SWIGLU_ALPHA * x_glu) * (x_lin + 1.0)
        ys_ref[...] = jnp.dot(act.astype(BF16), w2b_ref[...], preferred_element_type=F32) + b2_ref[0]

    @pl.when(i >= nv_ref[0])
    def _():
        ys_ref[...] = jnp.zeros(ys_ref.shape, F32)


def _experts(tile_expert, n_valid, xs, w1, b1, w2, b2):
    n_slots = xs.shape[0]
    n_tiles = n_slots // SLOT_TILE
    grid_spec = pltpu.PrefetchScalarGridSpec(
        num_scalar_prefetch=2,
        grid=(n_tiles,),
        in_specs=[
            pl.BlockSpec((SLOT_TILE, D_MODEL), lambda i, te, nv: (i, 0)),
            pl.BlockSpec((1, D_MODEL, 2 * D_FF), lambda i, te, nv: (te[i], 0, 0)),
            pl.BlockSpec((1, 1, 2 * D_FF), lambda i, te, nv: (te[i], 0, 0)),
            pl.BlockSpec((1, D_FF, D_MODEL), lambda i, te, nv: (te[i], 0, 0)),
            pl.BlockSpec((1, 1, D_MODEL), lambda i, te, nv: (te[i], 0, 0)),
        ],
        out_specs=pl.BlockSpec((SLOT_TILE, D_MODEL), lambda i, te, nv: (i, 0)),
        scratch_shapes=[pltpu.VMEM((D_MODEL, 2 * D_FF), BF16), pltpu.VMEM((D_FF, D_MODEL), BF16)],
    )
    return pl.pallas_call(
        _experts_kernel,
        grid_spec=grid_spec,
        out_shape=jax.ShapeDtypeStruct((n_slots, D_MODEL), F32),
        compiler_params=pltpu.CompilerParams(dimension_semantics=("arbitrary",), vmem_limit_bytes=VMEM_LIMIT),
        name="experts",
    )(tile_expert, n_valid, xs, w1, b1.reshape(N_EXPERTS, 1, 2 * D_FF), w2, b2.reshape(N_EXPERTS, 1, D_MODEL))


def _combine_kernel(y4_ref, x1_ref, topw_ref, g2_ref, fw_ref, o_ref, *, per_row_mod):
    w = topw_ref[...]
    ff = w[:, 0:1] * y4_ref[:, pl.ds(0, D_MODEL)]
    for k in range(1, TOP_K):
        ff = ff + w[:, k:k + 1] * y4_ref[:, pl.ds(k * D_MODEL, D_MODEL)]
    g2 = g2_ref[...] if per_row_mod else g2_ref[0]
    x = x1_ref[...] + g2 * ff
    ms = jnp.mean(x * x, axis=-1, keepdims=True)
    o_ref[...] = x * lax.rsqrt(ms + NORM_EPS) * fw_ref[...]


def _combine(y4, x1_rows, topw, g2, fw, row_offset, rows_per_mod):
    n = x1_rows.shape[0]
    tile = COMBINE_TILE
    off = row_offset // tile
    per_row = rows_per_mod == 1
    if per_row:
        g2_spec = pl.BlockSpec((tile, D_MODEL), lambda i: (i, 0))
    else:
        g2 = g2.reshape(g2.shape[0], 1, D_MODEL)
        g2_spec = pl.BlockSpec((1, 1, D_MODEL), lambda i: ((i * tile) // rows_per_mod, 0, 0))
    return pl.pallas_call(
        functools.partial(_combine_kernel, per_row_mod=per_row),
        grid=(n // tile,),
        in_specs=[pl.BlockSpec((tile, TOP_K * D_MODEL), lambda i: (i + off, 0)),
                  pl.BlockSpec((tile, D_MODEL), lambda i: (i, 0)),
                  pl.BlockSpec((tile, LANES), lambda i: (i + off, 0)),
                  g2_spec, _const_spec(fw.shape)],
        out_specs=pl.BlockSpec((tile, D_MODEL), lambda i: (i, 0)),
        out_shape=jax.ShapeDtypeStruct((n, D_MODEL), F32),
        compiler_params=pltpu.CompilerParams(dimension_semantics=("arbitrary",), vmem_limit_bytes=VMEM_LIMIT),
        name="combine",
    )(y4, x1_rows, topw, g2, fw)


def _routing_tables(top_idx, n_slots):
    n_assign = top_idx.shape[0] * TOP_K
    e_flat = top_idx.reshape(n_assign)
    order = jnp.argsort(e_flat, stable=True).astype(jnp.int32)
    counts = jnp.sum((e_flat[:, None] == jnp.arange(N_EXPERTS, dtype=jnp.int32)[None, :]).astype(jnp.int32), axis=0)
    padded = ((counts + SLOT_TILE - 1) // SLOT_TILE) * SLOT_TILE
    pend = jnp.cumsum(padded)
    poff = pend - padded
    uoff = jnp.cumsum(counts) - counts
    slot = jnp.arange(n_slots, dtype=jnp.int32)
    e_slot = jnp.searchsorted(pend, slot, side="right").astype(jnp.int32)
    e_clip = jnp.minimum(e_slot, N_EXPERTS - 1)
    rank = slot - poff[e_clip]
    valid = (e_slot < N_EXPERTS) & (rank < counts[e_clip])
    assign = order[jnp.clip(uoff[e_clip] + rank, 0, n_assign - 1)]
    slot_token = jnp.where(valid, assign // TOP_K, 0).astype(jnp.int32)
    slot_assign = jnp.where(valid, assign, -1).astype(jnp.int32)
    n_tiles = n_slots // SLOT_TILE
    n_valid = (pend[-1] // SLOT_TILE).astype(jnp.int32)
    tile_e = jnp.searchsorted(pend, jnp.arange(n_tiles, dtype=jnp.int32) * SLOT_TILE, side="right").astype(jnp.int32)
    last_e = tile_e[jnp.maximum(n_valid - 1, 0)]
    tile_e = jnp.where(jnp.arange(n_tiles) < n_valid, tile_e, last_e).astype(jnp.int32)
    return slot_token, slot_assign, tile_e, n_valid.reshape(1)


def _round_up(n, m):
    return ((n + m - 1) // m) * m


def kernel(x_prompt, x_sample, c_prompt, c_sample, state_ret, state_s5_re, state_s5_im, norm1_w, norm2_w, w_ada, b_ada, w_in, ret_norm_w, s5_lam_re, s5_lam_im, s5_log_dt, s5_b_re, s5_b_im, s5_c_re, s5_c_im, s5_d, w_glu, b_glu, s5_norm_w, w_out, w_router, b_router, w1, b1, w2, b2, final_w):
    bp, lp, _ = x_prompt.shape
    bs, ls, _ = x_sample.shape
    assert norm1_w.shape[0] == 1, "single-layer model"
    n_p, n_s = bp * lp, bs * ls
    n_tok = n_p + n_s

    mod = _ada(jnp.concatenate([c_prompt, c_sample], axis=0), w_ada[0], b_ada[0])
    mod_p, mod_s = mod[:bp], jnp.repeat(mod[bp:], ls, axis=0)
    d3 = 3 * D_MODEL

    lbr, lbi, bbr, bbi = _s5prep(s5_lam_re[0], s5_lam_im[0], s5_log_dt[0], s5_b_re[0], s5_b_im[0])
    bmat = jnp.concatenate([_block_diag(bbr), _block_diag(bbi)], axis=-1).astype(BF16)
    cre = _block_diag(jnp.transpose(s5_c_re[0], (0, 2, 1))).astype(BF16)
    cim = _block_diag(jnp.transpose(-s5_c_im[0], (0, 2, 1))).astype(BF16)
    wts = dict(
        n1w=norm1_w, w_in=w_in[0].astype(BF16), rnw=ret_norm_w, bmat=bmat, cre=cre, cim=cim,
        lbr=lbr.reshape(1, SSM_CH), lbi=lbi.reshape(1, SSM_CH), dsk=s5_d[0].reshape(1, SSM_WIDTH),
        w_glu=w_glu[0].astype(BF16), b_glu=b_glu, snw=s5_norm_w, w_out=w_out[0].astype(BF16),
    )

    zero_states = (jnp.zeros((bp, RET_HEADS, HEAD_DIM, HEAD_DIM), F32), jnp.zeros((bp, SSM_CH), F32),
                   jnp.zeros((bp, SSM_CH), F32))
    x1_p, ret_p, re_p, im_p = _mixer(x_prompt, mod_p[:, :d3], jnp.arange(lp, dtype=F32), zero_states, wts,
                                     prompt=True)
    sample_states = (state_ret[0], state_s5_re[0].reshape(bs, SSM_CH), state_s5_im[0].reshape(bs, SSM_CH))
    x1_s, ret_s, re_s, im_s = _mixer(x_sample.reshape(n_s, D_MODEL), mod_s[:, :d3],
                                     PAST_LEN + jnp.arange(ls, dtype=F32), sample_states, wts, prompt=False)

    wr_pad = jnp.pad(w_router[0], ((0, 0), (0, LANES - N_EXPERTS)))
    br_pad = jnp.pad(b_router, ((0, 0), (0, LANES - N_EXPERTS)), constant_values=-1e30)
    x1_p_rows = x1_p.reshape(n_p, D_MODEL)
    h2, topi, topw = _router(x1_p_rows, mod_p[:, d3:d3 + 2 * D_MODEL], lp, x1_s, mod_s[:, d3:d3 + 2 * D_MODEL],
                             norm2_w, wr_pad, br_pad)

    n_assign = n_tok * TOP_K
    n_slots = _round_up(_round_up(n_assign, SLOT_TILE) + N_EXPERTS * SLOT_TILE, DMA_ROWS_PER_STEP)
    slot_token, slot_assign, tile_e, n_valid = _routing_tables(topi[:, :TOP_K], n_slots)
    xs = _move_rows(_gather_rows_kernel, slot_token, h2, n_slots, "dispatch")
    ys = _experts(tile_e, n_valid, xs, w1[0], b1[0], w2[0], b2[0])
    y4 = _move_rows(_scatter_rows_kernel, slot_assign, ys, n_assign, "undispatch")
    y4 = y4.reshape(n_tok, TOP_K * D_MODEL)

    g2_p, g2_s = mod_p[:, 5 * D_MODEL:], mod_s[:, 5 * D_MODEL:]
    fw = final_w.reshape(1, D_MODEL)
    y_p = _combine(y4, x1_p_rows, topw, g2_p, fw, 0, lp)
    y_s = _combine(y4, x1_s, topw, g2_s, fw, n_p, 1)

    g, p = SSM_GROUPS, SSM_STATE
    return (y_p.reshape(bp, lp, D_MODEL), y_s.reshape(bs, ls, D_MODEL),
            ret_p[None], re_p.reshape(1, bp, g, p), im_p.reshape(1, bp, g, p),
            ret_s[None], re_s.reshape(1, bs, g, p), im_s.reshape(1, bs, g, p))
```

```python
import functools
import math

import jax
import jax.numpy as jnp
from jax import lax
from jax.experimental import pallas as pl
from jax.experimental.pallas import tpu as pltpu
from jax.experimental.pallas import tpu_sc as plsc

F32 = jnp.float32
BF16 = jnp.bfloat16
HIGHEST = lax.Precision.HIGHEST

D_MODEL = 1024
PAST_LEN = 16384
RET_WIDTH = 512
RET_HEADS = 4
HEAD_DIM = 128
ROPE_BASE = 10000.0
SSM_WIDTH = 512
SSM_GROUP = 16
SSM_GROUPS = 32
SSM_STATE = 64
SSM_CH = SSM_GROUPS * SSM_STATE
IN_WIDTH = 4 * RET_WIDTH + SSM_WIDTH
N_EXPERTS = 32
TOP_K = 4
D_FF = 1024
SWIGLU_LIMIT = 7.0
SWIGLU_ALPHA = 1.702
NORM_EPS = 1e-6

LANES = 128
SUBLANES = 8
VMEM_LIMIT = 56 * 1024 * 1024

SEQ_PER_BLOCK = 8
PROMPT_CHUNK = 64
S5_BLOCK_GROUPS = 8
N_S5_BLOCKS = SSM_GROUPS // S5_BLOCK_GROUPS
S5_BLOCK_IN = S5_BLOCK_GROUPS * SSM_GROUP
S5_BLOCK_CH = S5_BLOCK_GROUPS * SSM_STATE
ROUTER_TILE = 512
SLOT_TILE = 256
COMBINE_TILE = 256
SC_GATHER_WINDOW = 128
SC_WORKERS = 32


def _silu(x):
    return x * jax.nn.sigmoid(x)


def _ada_kernel(c_ref, w_ref, b_ref, o_ref):
    s = _silu(c_ref[...])
    o_ref[...] = jnp.dot(s, w_ref[...], precision=HIGHEST, preferred_element_type=F32) + b_ref[...]


def _ada(c_all, w_ada, b_ada):
    n_rows, n_out = c_all.shape[0], w_ada.shape[1]
    tn = 1536
    return pl.pallas_call(
        _ada_kernel,
        grid=(n_out // tn,),
        in_specs=[
            pl.BlockSpec((n_rows, D_MODEL), lambda j: (0, 0)),
            pl.BlockSpec((D_MODEL, tn), lambda j: (0, j)),
            pl.BlockSpec((1, tn), lambda j: (0, j)),
        ],
        out_specs=pl.BlockSpec((n_rows, tn), lambda j: (0, j)),
        out_shape=jax.ShapeDtypeStruct((n_rows, n_out), F32),
        compiler_params=pltpu.CompilerParams(dimension_semantics=("arbitrary",), vmem_limit_bytes=VMEM_LIMIT),
        name="ada",
    )(c_all, w_ada, b_ada.reshape(1, n_out))


def _s5prep_kernel(lre_ref, lim_ref, ldt_ref, bre_ref, bim_ref, lbr_ref, lbi_ref, bbr_ref, bbi_ref):
    lam_re, lam_im = lre_ref[...], lim_ref[...]
    dt = jnp.exp(ldt_ref[...])
    mag = jnp.exp(lam_re * dt)
    ang = lam_im * dt
    lb_re, lb_im = mag * jnp.cos(ang), mag * jnp.sin(ang)
    den = lam_re * lam_re + lam_im * lam_im
    f_re = ((lb_re - 1.0) * lam_re + lb_im * lam_im) / den
    f_im = (lb_im * lam_re - (lb_re - 1.0) * lam_im) / den
    lbr_ref[...] = lb_re
    lbi_ref[...] = lb_im
    b_re, b_im = bre_ref[...], bim_ref[...]
    bbr_ref[...] = f_re[:, None, :] * b_re - f_im[:, None, :] * b_im
    bbi_ref[...] = f_re[:, None, :] * b_im + f_im[:, None, :] * b_re


def _s5prep(lam_re, lam_im, log_dt, b_re, b_im):
    g, p = lam_re.shape
    bt_re = jnp.transpose(b_re, (0, 2, 1))
    bt_im = jnp.transpose(b_im, (0, 2, 1))
    return pl.pallas_call(
        _s5prep_kernel,
        out_shape=(
            jax.ShapeDtypeStruct((g, p), F32), jax.ShapeDtypeStruct((g, p), F32),
            jax.ShapeDtypeStruct((g, SSM_GROUP, p), F32), jax.ShapeDtypeStruct((g, SSM_GROUP, p), F32),
        ),
        name="s5prep",
    )(lam_re, lam_im, log_dt.reshape(g, 1), bt_re, bt_im)


def _block_diag(blocks):
    _, r, c = blocks.shape
    b4 = blocks.reshape(N_S5_BLOCKS, S5_BLOCK_GROUPS, r, c)
    eye = jnp.eye(S5_BLOCK_GROUPS, dtype=blocks.dtype)
    out = b4[:, :, :, None, :] * eye[None, :, None, :, None]
    return out.reshape(N_S5_BLOCKS, S5_BLOCK_GROUPS * r, S5_BLOCK_GROUPS * c)


def _mixer_kernel(x_ref, mod_ref, n1w_ref, win_ref, cos_ref, sin_ref, dmask_ref, cdec_ref, sdec_ref,
                  rnw_ref, bmat_ref, cre_ref, cim_ref, lbr_ref, lbi_ref, dsk_ref, wglu_ref, bglu_ref,
                  snw_ref, wout_ref, sret0_ref, sre0_ref, sim0_ref,
                  x1_ref, sret_ref, sre_ref, sim_ref,
                  hb_ref, z_ref, zu_ref, oy_ref, utb_ref, bur_ref, bui_ref, ytb_ref, yb_ref,
                  *, n_seq, chunk, tile_rows, carry, chunk_decay):
    rows = n_seq * chunk
    seq_per_tile = tile_rows // chunk
    n_tiles = rows // tile_rows
    per_row_mod = mod_ref.shape[0] == rows

    def load_states():
        sret_ref[...] = sret0_ref[...]
        sre_ref[...] = sre0_ref[...]
        sim_ref[...] = sim0_ref[...]

    if carry:
        pl.when(pl.program_id(0) == 0)(load_states)
    else:
        load_states()

    n1w = n1w_ref[...]
    mod_rows = rows if per_row_mod else chunk
    for i in range(rows // mod_rows):
        r0 = i * mod_rows
        xb = _load_rows(x_ref, r0, mod_rows, chunk)
        if per_row_mod:
            sh = mod_ref[pl.ds(r0, mod_rows), pl.ds(0, D_MODEL)]
            sc = mod_ref[pl.ds(r0, mod_rows), pl.ds(D_MODEL, D_MODEL)]
        else:
            sh = mod_ref[pl.ds(i, 1), pl.ds(0, D_MODEL)]
            sc = mod_ref[pl.ds(i, 1), pl.ds(D_MODEL, D_MODEL)]
        ms = jnp.mean(xb * xb, axis=-1, keepdims=True)
        hn = xb * lax.rsqrt(ms + NORM_EPS) * n1w
        hb_ref[pl.ds(r0, mod_rows), :] = (hn * (1.0 + sc) + sh).astype(BF16)
    ret_w = 4 * RET_WIDTH
    z_ref[...] = jnp.dot(hb_ref[...], win_ref[:, pl.ds(0, ret_w)], preferred_element_type=F32)
    zu = jnp.dot(hb_ref[...], win_ref[:, pl.ds(ret_w, SSM_WIDTH)], preferred_element_type=F32)
    for c in range(SSM_WIDTH // LANES):
        zu_ref[c] = zu[:, c * LANES:(c + 1) * LANES]

    cos = cos_ref[...]
    sin = sin_ref[...]
    scale = HEAD_DIM ** -0.5
    if seq_per_tile > 1:
        row_id = lax.broadcasted_iota(jnp.int32, (tile_rows, HEAD_DIM), 0)

    def rope(t):
        return t * cos + pltpu.roll(t, HEAD_DIM // 2, 1) * sin

    def ret_tile(ti, c):
        r0 = pl.multiple_of(ti * tile_rows, tile_rows)
        for h in range(RET_HEADS):
            c0 = h * HEAD_DIM
            q = rope(z_ref[pl.ds(r0, tile_rows), pl.ds(c0, HEAD_DIM)])
            k = rope(z_ref[pl.ds(r0, tile_rows), pl.ds(RET_WIDTH + c0, HEAD_DIM)]) * scale
            v = z_ref[pl.ds(r0, tile_rows), pl.ds(2 * RET_WIDTH + c0, HEAD_DIM)]
            g = z_ref[pl.ds(r0, tile_rows), pl.ds(3 * RET_WIDTH + c0, HEAD_DIM)]
            kd = k * sdec_ref[h]
            if tile_rows < HEAD_DIM:
                pad = jnp.zeros((HEAD_DIM - tile_rows, HEAD_DIM), F32)
                k, v, kd = (jnp.concatenate([t, pad], axis=0) for t in (k, v, kd))
                if seq_per_tile > 1:
                    row_kv = lax.broadcasted_iota(jnp.int32, (HEAD_DIM, HEAD_DIM), 0)
            elif seq_per_tile > 1:
                row_kv = row_id
            qb, kb, vb = q.astype(BF16), k.astype(BF16), v.astype(BF16)
            s = lax.dot_general(qb, kb, (((1,), (1,)), ((), ())), preferred_element_type=F32) * dmask_ref[h]
            o = jnp.dot(s.astype(BF16), vb, preferred_element_type=F32)
            cross = None
            for si in range(seq_per_tile):
                sidx = ti * seq_per_tile + si
                st = sret_ref[sidx, h]
                cr = jnp.dot(qb, st.astype(BF16), preferred_element_type=F32)
                if seq_per_tile > 1:
                    in_seq = (row_id >= si * chunk) & (row_id < (si + 1) * chunk)
                    cross = jnp.where(in_seq, cr, 0.0 if cross is None else cross)
                    kds = jnp.where((row_kv >= si * chunk) & (row_kv < (si + 1) * chunk), kd, 0.0)
                else:
                    cross, kds = cr, kd
                upd = lax.dot_general(kds.astype(BF16), vb, (((0,), (0,)), ((), ())), preferred_element_type=F32)
                sret_ref[sidx, h] = st * chunk_decay[h] + upd
            o = o + cross * cdec_ref[h]
            o = o * lax.rsqrt(jnp.mean(o * o, axis=-1, keepdims=True) + NORM_EPS)
            o = o * rnw_ref[:, pl.ds(c0, HEAD_DIM)] * _silu(g)
            oy_ref[pl.ds(r0, tile_rows), pl.ds(c0, HEAD_DIM)] = o
        return c

    lax.fori_loop(0, n_tiles, ret_tile, 0)

    for t in range(chunk):
        for c in range(SSM_WIDTH // LANES):
            utb_ref[pl.ds(t * n_seq, n_seq), pl.ds(c * LANES, LANES)] = zu_ref[c, pl.ds(t, n_seq, stride=chunk), :]
    for blk in range(N_S5_BLOCKS):
        ub = utb_ref[:, pl.ds(blk * S5_BLOCK_IN, S5_BLOCK_IN)].astype(BF16)
        bu = jnp.dot(ub, bmat_ref[blk], preferred_element_type=F32)
        bur_ref[:, pl.ds(blk * S5_BLOCK_CH, S5_BLOCK_CH)] = bu[:, :S5_BLOCK_CH]
        bui_ref[:, pl.ds(blk * S5_BLOCK_CH, S5_BLOCK_CH)] = bu[:, S5_BLOCK_CH:]

    scan_w = 2 * S5_BLOCK_CH
    for p in range(SSM_CH // scan_w):
        cols = pl.ds(p * scan_w, scan_w)
        lbr = jnp.broadcast_to(lbr_ref[:, cols], (n_seq, scan_w))
        lbi = jnp.broadcast_to(lbi_ref[:, cols], (n_seq, scan_w))

        def scan_step(t, hc, cols=cols, lbr=lbr, lbi=lbi):
            hr, hi = hc
            r0 = pl.multiple_of(t * n_seq, n_seq)
            nr = lbr * hr - lbi * hi + bur_ref[pl.ds(r0, n_seq), cols]
            ni = lbr * hi + lbi * hr + bui_ref[pl.ds(r0, n_seq), cols]
            bur_ref[pl.ds(r0, n_seq), cols] = nr
            bui_ref[pl.ds(r0, n_seq), cols] = ni
            return nr, ni

        h0 = (sre_ref[:, cols], sim_ref[:, cols])
        if chunk <= 8:
            hc = h0
            for t in range(chunk):
                hc = scan_step(t, hc)
        else:
            unroll = 4

            def scan_group(tg, hc):
                for j in range(unroll):
                    hc = scan_step(tg * unroll + j, hc)
                return hc

            hc = lax.fori_loop(0, chunk // unroll, scan_group, h0)
        sre_ref[:, cols] = hc[0]
        sim_ref[:, cols] = hc[1]

    for blk in range(N_S5_BLOCKS):
        cols = pl.ds(blk * S5_BLOCK_CH, S5_BLOCK_CH)
        yb = jnp.dot(bur_ref[:, cols].astype(BF16), cre_ref[blk], preferred_element_type=F32)
        yb = yb + jnp.dot(bui_ref[:, cols].astype(BF16), cim_ref[blk], preferred_element_type=F32)
        ucols = pl.ds(blk * S5_BLOCK_IN, S5_BLOCK_IN)
        ytb_ref[:, ucols] = yb + dsk_ref[:, ucols] * utb_ref[:, ucols]
    for t in range(chunk):
        for c in range(SSM_WIDTH // LANES):
            yb_ref[c, pl.ds(t, n_seq, stride=chunk), :] = ytb_ref[pl.ds(t * n_seq, n_seq), pl.ds(c * LANES, LANES)]

    y = jnp.concatenate([yb_ref[c] for c in range(SSM_WIDTH // LANES)], axis=1)
    y = jax.nn.gelu(y, approximate=True)
    gate = jnp.dot(y.astype(BF16), wglu_ref[...], preferred_element_type=F32) + bglu_ref[...]
    y = y * jax.nn.sigmoid(gate)
    y = y * lax.rsqrt(jnp.mean(y * y, axis=-1, keepdims=True) + NORM_EPS) * snw_ref[...]
    oy_ref[:, pl.ds(RET_WIDTH, SSM_WIDTH)] = y

    mix = jnp.dot(oy_ref[...].astype(BF16), wout_ref[...], preferred_element_type=F32)
    for i in range(rows // mod_rows):
        r0 = i * mod_rows
        if per_row_mod:
            g1 = mod_ref[pl.ds(r0, mod_rows), pl.ds(2 * D_MODEL, D_MODEL)]
        else:
            g1 = mod_ref[pl.ds(i, 1), pl.ds(2 * D_MODEL, D_MODEL)]
        _store_rows(x1_ref, r0, mod_rows, chunk,
                    _load_rows(x_ref, r0, mod_rows, chunk) + g1 * mix[r0:r0 + mod_rows])


def _load_rows(ref, r0, n, chunk):
    if len(ref.shape) == 2:
        return ref[pl.ds(r0, n), :]
    assert n == chunk and r0 % chunk == 0
    return ref[r0 // chunk]


def _store_rows(ref, r0, n, chunk, val):
    if len(ref.shape) == 2:
        ref[pl.ds(r0, n), :] = val
    else:
        assert n == chunk and r0 % chunk == 0
        ref[r0 // chunk] = val


def _const_spec(shape):
    nd = len(shape)
    return pl.BlockSpec(shape, lambda j, _n=nd: (0,) * _n)


def _decay_tables(chunk, tile_rows):
    log_gamma = jnp.log1p(-jnp.exp2(-5.0 - jnp.arange(RET_HEADS, dtype=F32)))
    r = jnp.arange(tile_rows)
    seq, loc = r // chunk, (r % chunk).astype(F32)
    rel = loc[:, None] - loc[None, :]
    ok = (seq[:, None] == seq[None, :]) & (rel >= 0)
    dmask = jnp.where(ok[None], jnp.exp(jnp.where(ok, rel, 0.0)[None] * log_gamma[:, None, None]), 0.0)
    if tile_rows < HEAD_DIM:
        dmask = jnp.pad(dmask, ((0, 0), (0, 0), (0, HEAD_DIM - tile_rows)))
    cdec = jnp.exp((loc[None, :] + 1.0) * log_gamma[:, None])
    sdec = jnp.exp((chunk - 1.0 - loc)[None, :] * log_gamma[:, None])
    bcast = lambda t: jnp.broadcast_to(t[:, :, None], (RET_HEADS, tile_rows, HEAD_DIM))
    return dmask, bcast(cdec), bcast(sdec), log_gamma


def _rope_tables(pos):
    half = HEAD_DIM // 2
    inv_freq = ROPE_BASE ** (-jnp.arange(half, dtype=F32) / half)
    ang = pos[:, None] * inv_freq[None, :]
    cos, sin = jnp.cos(ang), jnp.sin(ang)
    return jnp.concatenate([cos, cos], axis=-1), jnp.concatenate([-sin, sin], axis=-1)


def _mixer(x, mod, pos, states, wts, *, prompt):
    n_seq = SEQ_PER_BLOCK
    if prompt:
        n_total, seq_len, _ = x.shape
        assert n_total == n_seq
        chunk, tile_rows, n_steps = PROMPT_CHUNK, PROMPT_CHUNK, seq_len // PROMPT_CHUNK
        x_spec = pl.BlockSpec((n_seq, chunk, D_MODEL), lambda j: (0, j, 0))
        mod_spec = _const_spec(mod.shape)
        tab_spec = pl.BlockSpec((chunk, HEAD_DIM), lambda j: (j, 0))
        seq_map = lambda j: 0
    else:
        chunk = pos.shape[0]
        tile_rows = SUBLANES
        n_total = x.shape[0] // chunk
        n_steps = n_total // n_seq
        x_spec = pl.BlockSpec((n_seq * chunk, D_MODEL), lambda j: (j, 0))
        mod_spec = pl.BlockSpec((n_seq * chunk, mod.shape[1]), lambda j: (j, 0))
        tab_spec = _const_spec((tile_rows, HEAD_DIM))
        seq_map = lambda j: j
    rows = n_seq * chunk
    cos, sin = _rope_tables(pos)
    if not prompt:
        reps = tile_rows // chunk
        cos, sin = jnp.tile(cos, (reps, 1)), jnp.tile(sin, (reps, 1))
    dmask, cdec, sdec, log_gamma = _decay_tables(chunk, tile_rows)
    chunk_decay = tuple(float(math.exp(chunk * math.log1p(-2.0 ** (-5.0 - h)))) for h in range(RET_HEADS))
    del log_gamma
    sret0, sre0, sim0 = states

    st_ret_spec = pl.BlockSpec((n_seq, RET_HEADS, HEAD_DIM, HEAD_DIM), lambda j: (seq_map(j), 0, 0, 0))
    st_s5_spec = pl.BlockSpec((n_seq, SSM_CH), lambda j: (seq_map(j), 0))
    consts = [dmask, cdec, sdec, wts["rnw"], wts["bmat"], wts["cre"], wts["cim"], wts["lbr"], wts["lbi"],
              wts["dsk"], wts["w_glu"], wts["b_glu"], wts["snw"], wts["w_out"]]
    args = [x, mod, wts["n1w"], wts["w_in"], cos, sin] + consts + [sret0, sre0, sim0]
    in_specs = ([x_spec, mod_spec, _const_spec(wts["n1w"].shape), _const_spec(wts["w_in"].shape), tab_spec, tab_spec]
                + [_const_spec(a.shape) for a in consts] + [st_ret_spec, st_s5_spec, st_s5_spec])

    kern = functools.partial(_mixer_kernel, n_seq=n_seq, chunk=chunk, tile_rows=tile_rows, carry=prompt,
                             chunk_decay=chunk_decay)
    out_shape = (
        jax.ShapeDtypeStruct(x.shape, F32),
        jax.ShapeDtypeStruct((n_total, RET_HEADS, HEAD_DIM, HEAD_DIM), F32),
        jax.ShapeDtypeStruct((n_total, SSM_CH), F32),
        jax.ShapeDtypeStruct((n_total, SSM_CH), F32),
    )
    scratch = [
        pltpu.VMEM((rows, D_MODEL), BF16),
        pltpu.VMEM((rows, 4 * RET_WIDTH), F32),
        pltpu.VMEM((SSM_WIDTH // LANES, rows, LANES), F32),
        pltpu.VMEM((rows, D_MODEL), F32),
        pltpu.VMEM((rows, SSM_WIDTH), F32),
        pltpu.VMEM((rows, SSM_CH), F32),
        pltpu.VMEM((rows, SSM_CH), F32),
        pltpu.VMEM((rows, SSM_WIDTH), F32),
        pltpu.VMEM((SSM_WIDTH // LANES, rows, LANES), F32),
    ]
    return pl.pallas_call(
        kern,
        grid=(n_steps,),
        in_specs=in_specs,
        out_specs=(x_spec, st_ret_spec, st_s5_spec, st_s5_spec),
        out_shape=out_shape,
        scratch_shapes=scratch,
        compiler_params=pltpu.CompilerParams(dimension_semantics=("arbitrary",), vmem_limit_bytes=VMEM_LIMIT),
        name="mixer_prompt" if prompt else "mixer_sample",
    )(*args)


PACK_ROWS = D_MODEL // (2 * LANES)


def _store_packed(ref, x):
    n, half = x.shape[0], D_MODEL // 2
    bits = lax.bitcast_convert_type(x.astype(BF16).astype(F32), jnp.uint32)
    words = bits[:, :half] | (bits[:, half:] >> 16)
    for c in range(PACK_ROWS):
        ref[pl.ds(c, n, stride=PACK_ROWS), :] = words[:, c * LANES:(c + 1) * LANES]


def _load_packed(ref, n, first_row=0, row_stride=PACK_ROWS):
    hi, lo = [], []
    for c in range(PACK_ROWS):
        w = ref[pl.ds(first_row + c, n, stride=row_stride), :]
        hi.append(lax.bitcast_convert_type(w & jnp.uint32(0xFFFF0000), F32))
        lo.append(lax.bitcast_convert_type(w << 16, F32))
    return jnp.concatenate(hi + lo, axis=1)


def _route_tile(x, sh, sc, n2w_ref, wr_ref, br_ref, h2_ref, topi_ref, topw_ref):
    ms = jnp.mean(x * x, axis=-1, keepdims=True)
    h2 = x * lax.rsqrt(ms + NORM_EPS) * n2w_ref[...] * (1.0 + sc) + sh
    _store_packed(h2_ref, h2)
    logits = jnp.dot(h2, wr_ref[...], precision=HIGHEST, preferred_element_type=F32) + br_ref[...]
    lane = lax.broadcasted_iota(jnp.int32, logits.shape, 1)
    work = logits
    vals, idxs = [], []
    for _ in range(TOP_K):
        m = jnp.max(work, axis=-1, keepdims=True)
        idx = jnp.min(jnp.where(work == m, lane, LANES), axis=-1, keepdims=True)
        vals.append(m)
        idxs.append(idx)
        work = jnp.where(lane == idx, -jnp.inf, work)
    exps = [jnp.exp(v - vals[0]) for v in vals]
    tot = exps[0] + exps[1] + exps[2] + exps[3]
    topi = jnp.zeros(logits.shape, jnp.int32)
    topw = jnp.zeros(logits.shape, F32)
    for k in range(TOP_K):
        topi = jnp.where(lane == k, idxs[k], topi)
        topw = jnp.where(lane == k, exps[k] / tot, topw)
    topi_ref[...] = topi
    topw_ref[...] = topw


def _router_kernel(xp_ref, modp_ref, xs_ref, mods_ref, n2w_ref, wr_ref, br_ref, h2_ref, topi_ref, topw_ref,
                   *, n_prompt_tiles):
    i = pl.program_id(0)
    consts_outs = (n2w_ref, wr_ref, br_ref, h2_ref, topi_ref, topw_ref)

    @pl.when(i < n_prompt_tiles)
    def _():
        _route_tile(xp_ref[...], modp_ref[0, :, pl.ds(0, D_MODEL)], modp_ref[0, :, pl.ds(D_MODEL, D_MODEL)],
                    *consts_outs)

    @pl.when(i >= n_prompt_tiles)
    def _():
        _route_tile(xs_ref[...], mods_ref[:, pl.ds(0, D_MODEL)], mods_ref[:, pl.ds(D_MODEL, D_MODEL)],
                    *consts_outs)


def _router(xp_rows, mod_p, seq_len, xs_rows, mod_s, n2w, wr_pad, br_pad):
    tile = ROUTER_TILE
    n_p, n_s = xp_rows.shape[0], xs_rows.shape[0]
    tp, ts = n_p // tile, n_s // tile
    n_total = n_p + n_s
    mod_p = mod_p.reshape(mod_p.shape[0], 1, mod_p.shape[1])
    clamp_p = lambda i: jnp.minimum(i, tp - 1)
    clamp_s = lambda i: jnp.maximum(i - tp, 0)
    return pl.pallas_call(
        functools.partial(_router_kernel, n_prompt_tiles=tp),
        grid=(tp + ts,),
        in_specs=[pl.BlockSpec((tile, D_MODEL), lambda i: (clamp_p(i), 0)),
                  pl.BlockSpec((1, 1, mod_p.shape[2]), lambda i: ((clamp_p(i) * tile) // seq_len, 0, 0)),
                  pl.BlockSpec((tile, D_MODEL), lambda i: (clamp_s(i), 0)),
                  pl.BlockSpec((tile, mod_s.shape[1]), lambda i: (clamp_s(i), 0)),
                  _const_spec(n2w.shape), _const_spec(wr_pad.shape), _const_spec(br_pad.shape)],
        out_specs=(pl.BlockSpec((tile * PACK_ROWS, LANES), lambda i: (i, 0)),
                   pl.BlockSpec((tile, LANES), lambda i: (i, 0)),
                   pl.BlockSpec((tile, LANES), lambda i: (i, 0))),
        out_shape=(jax.ShapeDtypeStruct((n_total * PACK_ROWS, LANES), jnp.uint32),
                   jax.ShapeDtypeStruct((n_total, LANES), jnp.int32),
                   jax.ShapeDtypeStruct((n_total, LANES), F32)),
        compiler_params=pltpu.CompilerParams(dimension_semantics=("arbitrary",), vmem_limit_bytes=VMEM_LIMIT),
        name="router",
    )(xp_rows, mod_p, xs_rows, mod_s, n2w, wr_pad, br_pad)


def _gather_rows(table, idx):
    n = idx.shape[0]
    steps = n // SC_GATHER_WINDOW
    assert n % SC_GATHER_WINDOW == 0 and steps % SC_WORKERS == 0
    mesh = plsc.VectorSubcoreMesh(core_axis_name="c", subcore_axis_name="s")

    @functools.partial(pl.kernel, out_type=jax.ShapeDtypeStruct((n, table.shape[1]), table.dtype), mesh=mesh,
                       scratch_types=[])
    def gather_kernel(table_hbm, idx_hbm, out_hbm):
        def body(idx_vmem, out_vmem):
            pltpu.sync_copy(table_hbm.at[idx_vmem.at[0]], out_vmem)

        pltpu.emit_pipeline(
            body,
            grid=(steps,),
            in_specs=[pl.BlockSpec((1, SC_GATHER_WINDOW), lambda i: (0, i))],
            out_specs=[pl.BlockSpec((SC_GATHER_WINDOW, table.shape[1]), lambda i: (i, 0))],
            core_axis_name=("c", "s"),
            dimension_semantics=(pltpu.PARALLEL,),
        )(idx_hbm, out_hbm)

    return gather_kernel(table, idx.reshape(1, n))


def _packed_row_indices(rows):
    return (rows[:, None] * PACK_ROWS + jnp.arange(PACK_ROWS, dtype=jnp.int32)[None, :]).reshape(-1)


def _experts_kernel(te_ref, nv_ref, xs_ref, w1_ref, b1_ref, w2_ref, b2_ref, ys_ref, w1b_ref, w2b_ref):
    i = pl.program_id(0)
    prev = te_ref[jnp.maximum(i - 1, 0)]
    new_expert = (i == 0) | (te_ref[i] != prev)

    @pl.when(new_expert)
    def _():
        w1b_ref[...] = w1_ref[0].astype(BF16)
        w2b_ref[...] = w2_ref[0].astype(BF16)

    @pl.when(i < nv_ref[0])
    def _():
        x = _load_packed(xs_ref, SLOT_TILE).astype(BF16)
        hu = jnp.dot(x, w1b_ref[...], preferred_element_type=F32) + b1_ref[0]
        x_glu = jnp.minimum(hu[:, :D_FF], SWIGLU_LIMIT)
        x_lin = jnp.clip(hu[:, D_FF:], -SWIGLU_LIMIT, SWIGLU_LIMIT)
        act = x_glu * jax.nn.sigmoid(SWIGLU_ALPHA * x_glu) * (x_lin + 1.0)
        _store_packed(ys_ref, jnp.dot(act.astype(BF16), w2b_ref[...], preferred_element_type=F32) + b2_ref[0])

    @pl.when(i >= nv_ref[0])
    def _():
        ys_ref[...] = jnp.zeros(ys_ref.shape, jnp.uint32)


def _experts(tile_expert, n_valid, xs, w1, b1, w2, b2):
    n_slots = xs.shape[0] // PACK_ROWS
    n_tiles = n_slots // SLOT_TILE
    grid_spec = pltpu.PrefetchScalarGridSpec(
        num_scalar_prefetch=2,
        grid=(n_tiles,),
        in_specs=[
            pl.BlockSpec((SLOT_TILE * PACK_ROWS, LANES), lambda i, te, nv: (i, 0)),
            pl.BlockSpec((1, D_MODEL, 2 * D_FF), lambda i, te, nv: (te[i], 0, 0)),
            pl.BlockSpec((1, 1, 2 * D_FF), lambda i, te, nv: (te[i], 0, 0)),
            pl.BlockSpec((1, D_FF, D_MODEL), lambda i, te, nv: (te[i], 0, 0)),
            pl.BlockSpec((1, 1, D_MODEL), lambda i, te, nv: (te[i], 0, 0)),
        ],
        out_specs=pl.BlockSpec((SLOT_TILE * PACK_ROWS, LANES), lambda i, te, nv: (i, 0)),
        scratch_shapes=[pltpu.VMEM((D_MODEL, 2 * D_FF), BF16), pltpu.VMEM((D_FF, D_MODEL), BF16)],
    )
    return pl.pallas_call(
        _experts_kernel,
        grid_spec=grid_spec,
        out_shape=jax.ShapeDtypeStruct((n_slots * PACK_ROWS, LANES), jnp.uint32),
        compiler_params=pltpu.CompilerParams(dimension_semantics=("arbitrary",), vmem_limit_bytes=VMEM_LIMIT),
        name="experts",
    )(tile_expert, n_valid, xs, w1, b1.reshape(N_EXPERTS, 1, 2 * D_FF), w2, b2.reshape(N_EXPERTS, 1, D_MODEL))


def _combine_kernel(y4_ref, x1_ref, topw_ref, g2_ref, fw_ref, o_ref, *, per_row_mod):
    w = topw_ref[...]
    n = w.shape[0]
    ff = None
    for k in range(TOP_K):
        yk = w[:, k:k + 1] * _load_packed(y4_ref, n, first_row=k * PACK_ROWS, row_stride=TOP_K * PACK_ROWS)
        ff = yk if ff is None else ff + yk
    g2 = g2_ref[...] if per_row_mod else g2_ref[0]
    x = x1_ref[...] + g2 * ff
    ms = jnp.mean(x * x, axis=-1, keepdims=True)
    o_ref[...] = x * lax.rsqrt(ms + NORM_EPS) * fw_ref[...]


def _combine(y4, x1_rows, topw, g2, fw, row_offset, rows_per_mod):
    n = x1_rows.shape[0]
    tile = COMBINE_TILE
    off = row_offset // tile
    per_row = rows_per_mod == 1
    if per_row:
        g2_spec = pl.BlockSpec((tile, D_MODEL), lambda i: (i, 0))
    else:
        g2 = g2.reshape(g2.shape[0], 1, D_MODEL)
        g2_spec = pl.BlockSpec((1, 1, D_MODEL), lambda i: ((i * tile) // rows_per_mod, 0, 0))
    return pl.pallas_call(
        functools.partial(_combine_kernel, per_row_mod=per_row),
        grid=(n // tile,),
        in_specs=[pl.BlockSpec((tile * TOP_K * PACK_ROWS, LANES), lambda i: (i + off, 0)),
                  pl.BlockSpec((tile, D_MODEL), lambda i: (i, 0)),
                  pl.BlockSpec((tile, LANES), lambda i: (i + off, 0)),
                  g2_spec, _const_spec(fw.shape)],
        out_specs=pl.BlockSpec((tile, D_MODEL), lambda i: (i, 0)),
        out_shape=jax.ShapeDtypeStruct((n, D_MODEL), F32),
        compiler_params=pltpu.CompilerParams(dimension_semantics=("arbitrary",), vmem_limit_bytes=VMEM_LIMIT),
        name="combine",
    )(y4, x1_rows, topw, g2, fw)


def _routing_tables(top_idx, n_slots):
    n_assign = top_idx.shape[0] * TOP_K
    e_flat = top_idx.reshape(n_assign)
    onehot = (e_flat[:, None] == jnp.arange(N_EXPERTS, dtype=jnp.int32)[None, :]).astype(jnp.int32)
    running = jnp.cumsum(onehot, axis=0)
    counts = running[-1]
    padded = ((counts + SLOT_TILE - 1) // SLOT_TILE) * SLOT_TILE
    pend = jnp.cumsum(padded)
    poff = pend - padded
    slot = jnp.sum(onehot * (running - 1 + poff[None, :]), axis=1).astype(jnp.int32)
    token = jnp.arange(n_assign, dtype=jnp.int32) // TOP_K
    slot_token = jnp.zeros((n_slots,), jnp.int32).at[slot].set(token, unique_indices=True)
    n_tiles = n_slots // SLOT_TILE
    n_valid = (pend[-1] // SLOT_TILE).astype(jnp.int32)
    tile_row = jnp.minimum(jnp.arange(n_tiles, dtype=jnp.int32), n_valid - 1) * SLOT_TILE
    tile_e = jnp.sum((pend[None, :] <= tile_row[:, None]).astype(jnp.int32), axis=1).astype(jnp.int32)
    return slot_token, slot, tile_e, n_valid.reshape(1)


def _round_up(n, m):
    return ((n + m - 1) // m) * m


def kernel(x_prompt, x_sample, c_prompt, c_sample, state_ret, state_s5_re, state_s5_im, norm1_w, norm2_w, w_ada, b_ada, w_in, ret_norm_w, s5_lam_re, s5_lam_im, s5_log_dt, s5_b_re, s5_b_im, s5_c_re, s5_c_im, s5_d, w_glu, b_glu, s5_norm_w, w_out, w_router, b_router, w1, b1, w2, b2, final_w):
    bp, lp, _ = x_prompt.shape
    bs, ls, _ = x_sample.shape
    assert norm1_w.shape[0] == 1, "single-layer model"
    n_p, n_s = bp * lp, bs * ls
    n_tok = n_p + n_s

    mod = _ada(jnp.concatenate([c_prompt, c_sample], axis=0), w_ada[0], b_ada[0])
    mod_p, mod_s = mod[:bp], jnp.repeat(mod[bp:], ls, axis=0)
    d3 = 3 * D_MODEL

    lbr, lbi, bbr, bbi = _s5prep(s5_lam_re[0], s5_lam_im[0], s5_log_dt[0], s5_b_re[0], s5_b_im[0])
    bmat = jnp.concatenate([_block_diag(bbr), _block_diag(bbi)], axis=-1).astype(BF16)
    cre = _block_diag(jnp.transpose(s5_c_re[0], (0, 2, 1))).astype(BF16)
    cim = _block_diag(jnp.transpose(-s5_c_im[0], (0, 2, 1))).astype(BF16)
    wts = dict(
        n1w=norm1_w, w_in=w_in[0].astype(BF16), rnw=ret_norm_w, bmat=bmat, cre=cre, cim=cim,
        lbr=lbr.reshape(1, SSM_CH), lbi=lbi.reshape(1, SSM_CH), dsk=s5_d[0].reshape(1, SSM_WIDTH),
        w_glu=w_glu[0].astype(BF16), b_glu=b_glu, snw=s5_norm_w, w_out=w_out[0].astype(BF16),
    )

    zero_states = (jnp.zeros((bp, RET_HEADS, HEAD_DIM, HEAD_DIM), F32), jnp.zeros((bp, SSM_CH), F32),
                   jnp.zeros((bp, SSM_CH), F32))
    x1_p, ret_p, re_p, im_p = _mixer(x_prompt, mod_p[:, :d3], jnp.arange(lp, dtype=F32), zero_states, wts,
                                     prompt=True)
    sample_states = (state_ret[0], state_s5_re[0].reshape(bs, SSM_CH), state_s5_im[0].reshape(bs, SSM_CH))
    x1_s, ret_s, re_s, im_s = _mixer(x_sample.reshape(n_s, D_MODEL), mod_s[:, :d3],
                                     PAST_LEN + jnp.arange(ls, dtype=F32), sample_states, wts, prompt=False)

    wr_pad = jnp.pad(w_router[0], ((0, 0), (0, LANES - N_EXPERTS)))
    br_pad = jnp.pad(b_router, ((0, 0), (0, LANES - N_EXPERTS)), constant_values=-1e30)
    x1_p_rows = x1_p.reshape(n_p, D_MODEL)
    h2, topi, topw = _router(x1_p_rows, mod_p[:, d3:d3 + 2 * D_MODEL], lp, x1_s, mod_s[:, d3:d3 + 2 * D_MODEL],
                             norm2_w, wr_pad, br_pad)

    n_assign = n_tok * TOP_K
    gather_quantum = SC_GATHER_WINDOW * SC_WORKERS // PACK_ROWS
    assert n_assign % gather_quantum == 0
    n_slots = _round_up(_round_up(n_assign, SLOT_TILE) + N_EXPERTS * SLOT_TILE, gather_quantum)
    slot_token, slot, tile_e, n_valid = _routing_tables(topi[:, :TOP_K], n_slots)
    xs = _gather_rows(h2, _packed_row_indices(slot_token))
    ys = _experts(tile_e, n_valid, xs, w1[0], b1[0], w2[0], b2[0])
    y4 = _gather_rows(ys, _packed_row_indices(slot))

    g2_p, g2_s = mod_p[:, 5 * D_MODEL:], mod_s[:, 5 * D_MODEL:]
    fw = final_w.reshape(1, D_MODEL)
    y_p = _combine(y4, x1_p_rows, topw, g2_p, fw, 0, lp)
    y_s = _combine(y4, x1_s, topw, g2_s, fw, n_p, 1)

    g, p = SSM_GROUPS, SSM_STATE
    return (y_p.reshape(bp, lp, D_MODEL), y_s.reshape(bs, ls, D_MODEL),
            ret_p[None], re_p.reshape(1, bp, g, p), im_p.reshape(1, bp, g, p),
            ret_s[None], re_s.reshape(1, bs, g, p), im_s.reshape(1, bs, g, p))
```

```python
import functools
import math

import jax
import jax.numpy as jnp
from jax import lax
from jax.experimental import pallas as pl
from jax.experimental.pallas import tpu as pltpu
from jax.experimental.pallas import tpu_sc as plsc

F32 = jnp.float32
BF16 = jnp.bfloat16
HIGHEST = lax.Precision.HIGHEST

D_MODEL = 1024
PAST_LEN = 16384
RET_WIDTH = 512
RET_HEADS = 4
HEAD_DIM = 128
ROPE_BASE = 10000.0
SSM_WIDTH = 512
SSM_GROUP = 16
SSM_GROUPS = 32
SSM_STATE = 64
SSM_CH = SSM_GROUPS * SSM_STATE
IN_WIDTH = 4 * RET_WIDTH + SSM_WIDTH
N_EXPERTS = 32
TOP_K = 4
D_FF = 1024
SWIGLU_LIMIT = 7.0
SWIGLU_ALPHA = 1.702
NORM_EPS = 1e-6

LANES = 128
SUBLANES = 8
VMEM_LIMIT = 56 * 1024 * 1024

SEQ_PER_BLOCK = 8
PROMPT_CHUNK = 64
S5_BLOCK_GROUPS = 8
N_S5_BLOCKS = SSM_GROUPS // S5_BLOCK_GROUPS
S5_BLOCK_IN = S5_BLOCK_GROUPS * SSM_GROUP
S5_BLOCK_CH = S5_BLOCK_GROUPS * SSM_STATE
ROUTER_TILE = 512
SLOT_TILE = 256
COMBINE_TILE = 256
SC_GATHER_WINDOW = 128
SC_WORKERS = 32


def _silu(x):
    return x * jax.nn.sigmoid(x)


def _ada_kernel(c_ref, w_ref, b_ref, o_ref):
    s = _silu(c_ref[...])
    o_ref[...] = jnp.dot(s, w_ref[...], precision=HIGHEST, preferred_element_type=F32) + b_ref[...]


def _ada(c_all, w_ada, b_ada):
    n_rows, n_out = c_all.shape[0], w_ada.shape[1]
    tn = 1536
    return pl.pallas_call(
        _ada_kernel,
        grid=(n_out // tn,),
        in_specs=[
            pl.BlockSpec((n_rows, D_MODEL), lambda j: (0, 0)),
            pl.BlockSpec((D_MODEL, tn), lambda j: (0, j)),
            pl.BlockSpec((1, tn), lambda j: (0, j)),
        ],
        out_specs=pl.BlockSpec((n_rows, tn), lambda j: (0, j)),
        out_shape=jax.ShapeDtypeStruct((n_rows, n_out), F32),
        compiler_params=pltpu.CompilerParams(dimension_semantics=("arbitrary",), vmem_limit_bytes=VMEM_LIMIT),
        name="ada",
    )(c_all, w_ada, b_ada.reshape(1, n_out))


def _s5prep_kernel(lre_ref, lim_ref, ldt_ref, bre_ref, bim_ref, lbr_ref, lbi_ref, bbr_ref, bbi_ref):
    lam_re, lam_im = lre_ref[...], lim_ref[...]
    dt = jnp.exp(ldt_ref[...])
    mag = jnp.exp(lam_re * dt)
    ang = lam_im * dt
    lb_re, lb_im = mag * jnp.cos(ang), mag * jnp.sin(ang)
    den = lam_re * lam_re + lam_im * lam_im
    f_re = ((lb_re - 1.0) * lam_re + lb_im * lam_im) / den
    f_im = (lb_im * lam_re - (lb_re - 1.0) * lam_im) / den
    lbr_ref[...] = lb_re
    lbi_ref[...] = lb_im
    b_re, b_im = bre_ref[...], bim_ref[...]
    bbr_ref[...] = f_re[:, None, :] * b_re - f_im[:, None, :] * b_im
    bbi_ref[...] = f_re[:, None, :] * b_im + f_im[:, None, :] * b_re


def _s5prep(lam_re, lam_im, log_dt, b_re, b_im):
    g, p = lam_re.shape
    bt_re = jnp.transpose(b_re, (0, 2, 1))
    bt_im = jnp.transpose(b_im, (0, 2, 1))
    return pl.pallas_call(
        _s5prep_kernel,
        out_shape=(
            jax.ShapeDtypeStruct((g, p), F32), jax.ShapeDtypeStruct((g, p), F32),
            jax.ShapeDtypeStruct((g, SSM_GROUP, p), F32), jax.ShapeDtypeStruct((g, SSM_GROUP, p), F32),
        ),
        name="s5prep",
    )(lam_re, lam_im, log_dt.reshape(g, 1), bt_re, bt_im)


def _block_diag(blocks):
    _, r, c = blocks.shape
    b4 = blocks.reshape(N_S5_BLOCKS, S5_BLOCK_GROUPS, r, c)
    eye = jnp.eye(S5_BLOCK_GROUPS, dtype=blocks.dtype)
    out = b4[:, :, :, None, :] * eye[None, :, None, :, None]
    return out.reshape(N_S5_BLOCKS, S5_BLOCK_GROUPS * r, S5_BLOCK_GROUPS * c)


def _mixer_kernel(x_ref, mod_ref, n1w_ref, win_ref, cos_ref, sin_ref, dmask_ref, cdec_ref, sdec_ref,
                  rnw_ref, bmat_ref, cre_ref, cim_ref, lbr_ref, lbi_ref, dsk_ref, wglu_ref, bglu_ref,
                  snw_ref, wout_ref, sret0_ref, sre0_ref, sim0_ref,
                  x1_ref, sret_ref, sre_ref, sim_ref,
                  hb_ref, z_ref, zu_ref, oy_ref, utb_ref, bur_ref, bui_ref, ytb_ref, yb_ref,
                  *, n_seq, chunk, tile_rows, carry, chunk_decay):
    rows = n_seq * chunk
    seq_per_tile = tile_rows // chunk
    n_tiles = rows // tile_rows
    per_row_mod = mod_ref.shape[0] == rows

    def load_states():
        sret_ref[...] = sret0_ref[...]
        sre_ref[...] = sre0_ref[...]
        sim_ref[...] = sim0_ref[...]

    if carry:
        pl.when(pl.program_id(0) == 0)(load_states)
    else:
        load_states()

    n1w = n1w_ref[...]
    mod_rows = rows if per_row_mod else chunk
    for i in range(rows // mod_rows):
        r0 = i * mod_rows
        xb = _load_rows(x_ref, r0, mod_rows, chunk)
        if per_row_mod:
            sh = mod_ref[pl.ds(r0, mod_rows), pl.ds(0, D_MODEL)]
            sc = mod_ref[pl.ds(r0, mod_rows), pl.ds(D_MODEL, D_MODEL)]
        else:
            sh = mod_ref[pl.ds(i, 1), pl.ds(0, D_MODEL)]
            sc = mod_ref[pl.ds(i, 1), pl.ds(D_MODEL, D_MODEL)]
        ms = jnp.mean(xb * xb, axis=-1, keepdims=True)
        hn = xb * lax.rsqrt(ms + NORM_EPS) * n1w
        hb_ref[pl.ds(r0, mod_rows), :] = (hn * (1.0 + sc) + sh).astype(BF16)
    ret_w = 4 * RET_WIDTH
    z_ref[...] = jnp.dot(hb_ref[...], win_ref[:, pl.ds(0, ret_w)], preferred_element_type=F32)
    zu = jnp.dot(hb_ref[...], win_ref[:, pl.ds(ret_w, SSM_WIDTH)], preferred_element_type=F32)
    for c in range(SSM_WIDTH // LANES):
        zu_ref[c] = zu[:, c * LANES:(c + 1) * LANES]

    cos = cos_ref[...]
    sin = sin_ref[...]
    scale = HEAD_DIM ** -0.5
    if seq_per_tile > 1:
        row_id = lax.broadcasted_iota(jnp.int32, (tile_rows, HEAD_DIM), 0)

    def rope(t):
        return t * cos + pltpu.roll(t, HEAD_DIM // 2, 1) * sin

    def ret_tile(ti, c):
        r0 = pl.multiple_of(ti * tile_rows, tile_rows)
        for h in range(RET_HEADS):
            c0 = h * HEAD_DIM
            q = rope(z_ref[pl.ds(r0, tile_rows), pl.ds(c0, HEAD_DIM)])
            k = rope(z_ref[pl.ds(r0, tile_rows), pl.ds(RET_WIDTH + c0, HEAD_DIM)]) * scale
            v = z_ref[pl.ds(r0, tile_rows), pl.ds(2 * RET_WIDTH + c0, HEAD_DIM)]
            g = z_ref[pl.ds(r0, tile_rows), pl.ds(3 * RET_WIDTH + c0, HEAD_DIM)]
            kd = k * sdec_ref[h]
            if tile_rows < HEAD_DIM:
                pad = jnp.zeros((HEAD_DIM - tile_rows, HEAD_DIM), F32)
                k, v, kd = (jnp.concatenate([t, pad], axis=0) for t in (k, v, kd))
                if seq_per_tile > 1:
                    row_kv = lax.broadcasted_iota(jnp.int32, (HEAD_DIM, HEAD_DIM), 0)
            elif seq_per_tile > 1:
                row_kv = row_id
            qb, kb, vb = q.astype(BF16), k.astype(BF16), v.astype(BF16)
            s = lax.dot_general(qb, kb, (((1,), (1,)), ((), ())), preferred_element_type=F32) * dmask_ref[h]
            o = jnp.dot(s.astype(BF16), vb, preferred_element_type=F32)
            cross = None
            for si in range(seq_per_tile):
                sidx = ti * seq_per_tile + si
                st = sret_ref[sidx, h]
                cr = jnp.dot(qb, st.astype(BF16), preferred_element_type=F32)
                if seq_per_tile > 1:
                    in_seq = (row_id >= si * chunk) & (row_id < (si + 1) * chunk)
                    cross = jnp.where(in_seq, cr, 0.0 if cross is None else cross)
                    kds = jnp.where((row_kv >= si * chunk) & (row_kv < (si + 1) * chunk), kd, 0.0)
                else:
                    cross, kds = cr, kd
                upd = lax.dot_general(kds.astype(BF16), vb, (((0,), (0,)), ((), ())), preferred_element_type=F32)
                sret_ref[sidx, h] = st * chunk_decay[h] + upd
            o = o + cross * cdec_ref[h]
            o = o * lax.rsqrt(jnp.mean(o * o, axis=-1, keepdims=True) + NORM_EPS)
            o = o * rnw_ref[:, pl.ds(c0, HEAD_DIM)] * _silu(g)
            oy_ref[pl.ds(r0, tile_rows), pl.ds(c0, HEAD_DIM)] = o
        return c

    lax.fori_loop(0, n_tiles, ret_tile, 0)

    for t in range(chunk):
        for c in range(SSM_WIDTH // LANES):
            utb_ref[pl.ds(t * n_seq, n_seq), pl.ds(c * LANES, LANES)] = zu_ref[c, pl.ds(t, n_seq, stride=chunk), :]
    for blk in range(N_S5_BLOCKS):
        ub = utb_ref[:, pl.ds(blk * S5_BLOCK_IN, S5_BLOCK_IN)].astype(BF16)
        bu = jnp.dot(ub, bmat_ref[blk], preferred_element_type=F32)
        bur_ref[:, pl.ds(blk * S5_BLOCK_CH, S5_BLOCK_CH)] = bu[:, :S5_BLOCK_CH]
        bui_ref[:, pl.ds(blk * S5_BLOCK_CH, S5_BLOCK_CH)] = bu[:, S5_BLOCK_CH:]

    scan_w = 2 * S5_BLOCK_CH
    for p in range(SSM_CH // scan_w):
        cols = pl.ds(p * scan_w, scan_w)
        lbr = jnp.broadcast_to(lbr_ref[:, cols], (n_seq, scan_w))
        lbi = jnp.broadcast_to(lbi_ref[:, cols], (n_seq, scan_w))

        def scan_step(t, hc, cols=cols, lbr=lbr, lbi=lbi):
            hr, hi = hc
            r0 = pl.multiple_of(t * n_seq, n_seq)
            nr = lbr * hr - lbi * hi + bur_ref[pl.ds(r0, n_seq), cols]
            ni = lbr * hi + lbi * hr + bui_ref[pl.ds(r0, n_seq), cols]
            bur_ref[pl.ds(r0, n_seq), cols] = nr
            bui_ref[pl.ds(r0, n_seq), cols] = ni
            return nr, ni

        h0 = (sre_ref[:, cols], sim_ref[:, cols])
        if chunk <= 8:
            hc = h0
            for t in range(chunk):
                hc = scan_step(t, hc)
        else:
            unroll = 4

            def scan_group(tg, hc):
                for j in range(unroll):
                    hc = scan_step(tg * unroll + j, hc)
                return hc

            hc = lax.fori_loop(0, chunk // unroll, scan_group, h0)
        sre_ref[:, cols] = hc[0]
        sim_ref[:, cols] = hc[1]

    for blk in range(N_S5_BLOCKS):
        cols = pl.ds(blk * S5_BLOCK_CH, S5_BLOCK_CH)
        yb = jnp.dot(bur_ref[:, cols].astype(BF16), cre_ref[blk], preferred_element_type=F32)
        yb = yb + jnp.dot(bui_ref[:, cols].astype(BF16), cim_ref[blk], preferred_element_type=F32)
        ucols = pl.ds(blk * S5_BLOCK_IN, S5_BLOCK_IN)
        ytb_ref[:, ucols] = yb + dsk_ref[:, ucols] * utb_ref[:, ucols]
    for t in range(chunk):
        for c in range(SSM_WIDTH // LANES):
            yb_ref[c, pl.ds(t, n_seq, stride=chunk), :] = ytb_ref[pl.ds(t * n_seq, n_seq), pl.ds(c * LANES, LANES)]

    y = jnp.concatenate([yb_ref[c] for c in range(SSM_WIDTH // LANES)], axis=1)
    y = jax.nn.gelu(y, approximate=True)
    gate = jnp.dot(y.astype(BF16), wglu_ref[...], preferred_element_type=F32) + bglu_ref[...]
    y = y * jax.nn.sigmoid(gate)
    y = y * lax.rsqrt(jnp.mean(y * y, axis=-1, keepdims=True) + NORM_EPS) * snw_ref[...]
    oy_ref[:, pl.ds(RET_WIDTH, SSM_WIDTH)] = y

    mix = jnp.dot(oy_ref[...].astype(BF16), wout_ref[...], preferred_element_type=F32)
    for i in range(rows // mod_rows):
        r0 = i * mod_rows
        if per_row_mod:
            g1 = mod_ref[pl.ds(r0, mod_rows), pl.ds(2 * D_MODEL, D_MODEL)]
        else:
            g1 = mod_ref[pl.ds(i, 1), pl.ds(2 * D_MODEL, D_MODEL)]
        _store_rows(x1_ref, r0, mod_rows, chunk,
                    _load_rows(x_ref, r0, mod_rows, chunk) + g1 * mix[r0:r0 + mod_rows])


def _load_rows(ref, r0, n, chunk):
    if len(ref.shape) == 2:
        return ref[pl.ds(r0, n), :]
    assert n == chunk and r0 % chunk == 0
    return ref[r0 // chunk]


def _store_rows(ref, r0, n, chunk, val):
    if len(ref.shape) == 2:
        ref[pl.ds(r0, n), :] = val
    else:
        assert n == chunk and r0 % chunk == 0
        ref[r0 // chunk] = val


def _const_spec(shape):
    nd = len(shape)
    return pl.BlockSpec(shape, lambda j, _n=nd: (0,) * _n)


def _decay_tables(chunk, tile_rows):
    log_gamma = jnp.log1p(-jnp.exp2(-5.0 - jnp.arange(RET_HEADS, dtype=F32)))
    r = jnp.arange(tile_rows)
    seq, loc = r // chunk, (r % chunk).astype(F32)
    rel = loc[:, None] - loc[None, :]
    ok = (seq[:, None] == seq[None, :]) & (rel >= 0)
    dmask = jnp.where(ok[None], jnp.exp(jnp.where(ok, rel, 0.0)[None] * log_gamma[:, None, None]), 0.0)
    if tile_rows < HEAD_DIM:
        dmask = jnp.pad(dmask, ((0, 0), (0, 0), (0, HEAD_DIM - tile_rows)))
    cdec = jnp.exp((loc[None, :] + 1.0) * log_gamma[:, None])
    sdec = jnp.exp((chunk - 1.0 - loc)[None, :] * log_gamma[:, None])
    bcast = lambda t: jnp.broadcast_to(t[:, :, None], (RET_HEADS, tile_rows, HEAD_DIM))
    return dmask, bcast(cdec), bcast(sdec), log_gamma


def _rope_tables(pos):
    half = HEAD_DIM // 2
    inv_freq = ROPE_BASE ** (-jnp.arange(half, dtype=F32) / half)
    ang = pos[:, None] * inv_freq[None, :]
    cos, sin = jnp.cos(ang), jnp.sin(ang)
    return jnp.concatenate([cos, cos], axis=-1), jnp.concatenate([-sin, sin], axis=-1)


def _mixer(x, mod, pos, states, wts, *, prompt):
    n_seq = SEQ_PER_BLOCK
    if prompt:
        n_total, seq_len, _ = x.shape
        assert n_total == n_seq
        chunk, tile_rows, n_steps = PROMPT_CHUNK, PROMPT_CHUNK, seq_len // PROMPT_CHUNK
        x_spec = pl.BlockSpec((n_seq, chunk, D_MODEL), lambda j: (0, j, 0))
        mod_spec = _const_spec(mod.shape)
        tab_spec = pl.BlockSpec((chunk, HEAD_DIM), lambda j: (j, 0))
        seq_map = lambda j: 0
    else:
        chunk = pos.shape[0]
        tile_rows = SUBLANES
        n_total = x.shape[0] // chunk
        n_steps = n_total // n_seq
        x_spec = pl.BlockSpec((n_seq * chunk, D_MODEL), lambda j: (j, 0))
        mod_spec = pl.BlockSpec((n_seq * chunk, mod.shape[1]), lambda j: (j, 0))
        tab_spec = _const_spec((tile_rows, HEAD_DIM))
        seq_map = lambda j: j
    rows = n_seq * chunk
    cos, sin = _rope_tables(pos)
    if not prompt:
        reps = tile_rows // chunk
        cos, sin = jnp.tile(cos, (reps, 1)), jnp.tile(sin, (reps, 1))
    dmask, cdec, sdec, log_gamma = _decay_tables(chunk, tile_rows)
    chunk_decay = tuple(float(math.exp(chunk * math.log1p(-2.0 ** (-5.0 - h)))) for h in range(RET_HEADS))
    del log_gamma
    sret0, sre0, sim0 = states

    st_ret_spec = pl.BlockSpec((n_seq, RET_HEADS, HEAD_DIM, HEAD_DIM), lambda j: (seq_map(j), 0, 0, 0))
    st_s5_spec = pl.BlockSpec((n_seq, SSM_CH), lambda j: (seq_map(j), 0))
    consts = [dmask, cdec, sdec, wts["rnw"], wts["bmat"], wts["cre"], wts["cim"], wts["lbr"], wts["lbi"],
              wts["dsk"], wts["w_glu"], wts["b_glu"], wts["snw"], wts["w_out"]]
    args = [x, mod, wts["n1w"], wts["w_in"], cos, sin] + consts + [sret0, sre0, sim0]
    in_specs = ([x_spec, mod_spec, _const_spec(wts["n1w"].shape), _const_spec(wts["w_in"].shape), tab_spec, tab_spec]
                + [_const_spec(a.shape) for a in consts] + [st_ret_spec, st_s5_spec, st_s5_spec])

    kern = functools.partial(_mixer_kernel, n_seq=n_seq, chunk=chunk, tile_rows=tile_rows, carry=prompt,
                             chunk_decay=chunk_decay)
    out_shape = (
        jax.ShapeDtypeStruct(x.shape, F32),
        jax.ShapeDtypeStruct((n_total, RET_HEADS, HEAD_DIM, HEAD_DIM), F32),
        jax.ShapeDtypeStruct((n_total, SSM_CH), F32),
        jax.ShapeDtypeStruct((n_total, SSM_CH), F32),
    )
    scratch = [
        pltpu.VMEM((rows, D_MODEL), BF16),
        pltpu.VMEM((rows, 4 * RET_WIDTH), F32),
        pltpu.VMEM((SSM_WIDTH // LANES, rows, LANES), F32),
        pltpu.VMEM((rows, D_MODEL), F32),
        pltpu.VMEM((rows, SSM_WIDTH), F32),
        pltpu.VMEM((rows, SSM_CH), F32),
        pltpu.VMEM((rows, SSM_CH), F32),
        pltpu.VMEM((rows, SSM_WIDTH), F32),
        pltpu.VMEM((SSM_WIDTH // LANES, rows, LANES), F32),
    ]
    return pl.pallas_call(
        kern,
        grid=(n_steps,),
        in_specs=in_specs,
        out_specs=(x_spec, st_ret_spec, st_s5_spec, st_s5_spec),
        out_shape=out_shape,
        scratch_shapes=scratch,
        compiler_params=pltpu.CompilerParams(dimension_semantics=("arbitrary",), vmem_limit_bytes=VMEM_LIMIT),
        name="mixer_prompt" if prompt else "mixer_sample",
    )(*args)


PACK_ROWS = D_MODEL // (2 * LANES)


def _store_packed(ref, x):
    half = D_MODEL // 2
    bits = lax.bitcast_convert_type(x.astype(BF16).astype(F32), jnp.uint32)
    words = bits[:, :half] | (bits[:, half:] >> 16)
    for c in range(PACK_ROWS):
        ref[c] = words[:, c * LANES:(c + 1) * LANES]


def _load_packed(ref, n, first_row=0, row_stride=1):
    hi, lo = [], []
    for c in range(PACK_ROWS):
        w = ref[c] if row_stride == 1 else ref[c, pl.ds(first_row, n, stride=row_stride), :]
        hi.append(lax.bitcast_convert_type(w & jnp.uint32(0xFFFF0000), F32))
        lo.append(lax.bitcast_convert_type(w << 16, F32))
    return jnp.concatenate(hi + lo, axis=1)


def _route_tile(x, sh, sc, n2w_ref, wr_ref, br_ref, h2_ref, topi_ref, topw_ref):
    ms = jnp.mean(x * x, axis=-1, keepdims=True)
    h2 = x * lax.rsqrt(ms + NORM_EPS) * n2w_ref[...] * (1.0 + sc) + sh
    _store_packed(h2_ref, h2)
    logits = jnp.dot(h2, wr_ref[...], precision=HIGHEST, preferred_element_type=F32) + br_ref[...]
    lane = lax.broadcasted_iota(jnp.int32, logits.shape, 1)
    work = logits
    vals, idxs = [], []
    for _ in range(TOP_K):
        m = jnp.max(work, axis=-1, keepdims=True)
        idx = jnp.min(jnp.where(work == m, lane, LANES), axis=-1, keepdims=True)
        vals.append(m)
        idxs.append(idx)
        work = jnp.where(lane == idx, -jnp.inf, work)
    exps = [jnp.exp(v - vals[0]) for v in vals]
    tot = exps[0] + exps[1] + exps[2] + exps[3]
    topi = jnp.zeros(logits.shape, jnp.int32)
    topw = jnp.zeros(logits.shape, F32)
    for k in range(TOP_K):
        topi = jnp.where(lane == k, idxs[k], topi)
        topw = jnp.where(lane == k, exps[k] / tot, topw)
    topi_ref[...] = topi
    topw_ref[...] = topw


def _router_kernel(xp_ref, modp_ref, xs_ref, mods_ref, n2w_ref, wr_ref, br_ref, h2_ref, topi_ref, topw_ref,
                   *, n_prompt_tiles):
    i = pl.program_id(0)
    consts_outs = (n2w_ref, wr_ref, br_ref, h2_ref, topi_ref, topw_ref)

    @pl.when(i < n_prompt_tiles)
    def _():
        _route_tile(xp_ref[...], modp_ref[0, :, pl.ds(0, D_MODEL)], modp_ref[0, :, pl.ds(D_MODEL, D_MODEL)],
                    *consts_outs)

    @pl.when(i >= n_prompt_tiles)
    def _():
        _route_tile(xs_ref[...], mods_ref[:, pl.ds(0, D_MODEL)], mods_ref[:, pl.ds(D_MODEL, D_MODEL)],
                    *consts_outs)


def _router(xp_rows, mod_p, seq_len, xs_rows, mod_s, n2w, wr_pad, br_pad):
    tile = ROUTER_TILE
    n_p, n_s = xp_rows.shape[0], xs_rows.shape[0]
    tp, ts = n_p // tile, n_s // tile
    n_total = n_p + n_s
    mod_p = mod_p.reshape(mod_p.shape[0], 1, mod_p.shape[1])
    clamp_p = lambda i: jnp.minimum(i, tp - 1)
    clamp_s = lambda i: jnp.maximum(i - tp, 0)
    return pl.pallas_call(
        functools.partial(_router_kernel, n_prompt_tiles=tp),
        grid=(tp + ts,),
        in_specs=[pl.BlockSpec((tile, D_MODEL), lambda i: (clamp_p(i), 0)),
                  pl.BlockSpec((1, 1, mod_p.shape[2]), lambda i: ((clamp_p(i) * tile) // seq_len, 0, 0)),
                  pl.BlockSpec((tile, D_MODEL), lambda i: (clamp_s(i), 0)),
                  pl.BlockSpec((tile, mod_s.shape[1]), lambda i: (clamp_s(i), 0)),
                  _const_spec(n2w.shape), _const_spec(wr_pad.shape), _const_spec(br_pad.shape)],
        out_specs=(pl.BlockSpec((PACK_ROWS, tile, LANES), lambda i: (0, i, 0)),
                   pl.BlockSpec((tile, LANES), lambda i: (i, 0)),
                   pl.BlockSpec((tile, LANES), lambda i: (i, 0))),
        out_shape=(jax.ShapeDtypeStruct((PACK_ROWS, n_total, LANES), jnp.uint32),
                   jax.ShapeDtypeStruct((n_total, LANES), jnp.int32),
                   jax.ShapeDtypeStruct((n_total, LANES), F32)),
        compiler_params=pltpu.CompilerParams(dimension_semantics=("arbitrary",), vmem_limit_bytes=VMEM_LIMIT),
        name="router",
    )(xp_rows, mod_p, xs_rows, mod_s, n2w, wr_pad, br_pad)


def _gather_rows(table, idx):
    n = idx.shape[0]
    steps = n // SC_GATHER_WINDOW
    assert n % SC_GATHER_WINDOW == 0 and steps % SC_WORKERS == 0
    mesh = plsc.VectorSubcoreMesh(core_axis_name="c", subcore_axis_name="s")

    @functools.partial(pl.kernel, out_type=jax.ShapeDtypeStruct((n, table.shape[1]), table.dtype), mesh=mesh,
                       scratch_types=[])
    def gather_kernel(table_hbm, idx_hbm, out_hbm):
        def body(idx_vmem, out_vmem):
            pltpu.sync_copy(table_hbm.at[idx_vmem.at[0]], out_vmem)

        pltpu.emit_pipeline(
            body,
            grid=(steps,),
            in_specs=[pl.BlockSpec((1, SC_GATHER_WINDOW), lambda i: (0, i))],
            out_specs=[pl.BlockSpec((SC_GATHER_WINDOW, table.shape[1]), lambda i: (i, 0))],
            core_axis_name=("c", "s"),
            dimension_semantics=(pltpu.PARALLEL,),
        )(idx_hbm, out_hbm)

    return gather_kernel(table, idx.reshape(1, n))


def _gather_packed(table, rows):
    planes, n_table, lanes = table.shape
    idx = jnp.concatenate([rows + c * n_table for c in range(planes)])
    out = _gather_rows(table.reshape(planes * n_table, lanes), idx)
    return out.reshape(planes, rows.shape[0], lanes)


def _experts_kernel(te_ref, nv_ref, xs_ref, w1_ref, b1_ref, w2_ref, b2_ref, ys_ref, w1b_ref, w2b_ref):
    i = pl.program_id(0)
    prev = te_ref[jnp.maximum(i - 1, 0)]
    new_expert = (i == 0) | (te_ref[i] != prev)

    @pl.when(new_expert)
    def _():
        w1b_ref[...] = w1_ref[0].astype(BF16)
        w2b_ref[...] = w2_ref[0].astype(BF16)

    @pl.when(i < nv_ref[0])
    def _():
        x = _load_packed(xs_ref, SLOT_TILE).astype(BF16)
        hu = jnp.dot(x, w1b_ref[...], preferred_element_type=F32) + b1_ref[0]
        x_glu = jnp.minimum(hu[:, :D_FF], SWIGLU_LIMIT)
        x_lin = jnp.clip(hu[:, D_FF:], -SWIGLU_LIMIT, SWIGLU_LIMIT)
        act = x_glu * jax.nn.sigmoid(SWIGLU_ALPHA * x_glu) * (x_lin + 1.0)
        _store_packed(ys_ref, jnp.dot(act.astype(BF16), w2b_ref[...], preferred_element_type=F32) + b2_ref[0])

    @pl.when(i >= nv_ref[0])
    def _():
        ys_ref[...] = jnp.zeros(ys_ref.shape, jnp.uint32)


def _experts(tile_expert, n_valid, xs, w1, b1, w2, b2):
    n_slots = xs.shape[1]
    n_tiles = n_slots // SLOT_TILE
    grid_spec = pltpu.PrefetchScalarGridSpec(
        num_scalar_prefetch=2,
        grid=(n_tiles,),
        in_specs=[
            pl.BlockSpec((PACK_ROWS, SLOT_TILE, LANES), lambda i, te, nv: (0, i, 0)),
            pl.BlockSpec((1, D_MODEL, 2 * D_FF), lambda i, te, nv: (te[i], 0, 0)),
            pl.BlockSpec((1, 1, 2 * D_FF), lambda i, te, nv: (te[i], 0, 0)),
            pl.BlockSpec((1, D_FF, D_MODEL), lambda i, te, nv: (te[i], 0, 0)),
            pl.BlockSpec((1, 1, D_MODEL), lambda i, te, nv: (te[i], 0, 0)),
        ],
        out_specs=pl.BlockSpec((PACK_ROWS, SLOT_TILE, LANES), lambda i, te, nv: (0, i, 0)),
        scratch_shapes=[pltpu.VMEM((D_MODEL, 2 * D_FF), BF16), pltpu.VMEM((D_FF, D_MODEL), BF16)],
    )
    return pl.pallas_call(
        _experts_kernel,
        grid_spec=grid_spec,
        out_shape=jax.ShapeDtypeStruct((PACK_ROWS, n_slots, LANES), jnp.uint32),
        compiler_params=pltpu.CompilerParams(dimension_semantics=("arbitrary",), vmem_limit_bytes=VMEM_LIMIT),
        name="experts",
    )(tile_expert, n_valid, xs, w1, b1.reshape(N_EXPERTS, 1, 2 * D_FF), w2, b2.reshape(N_EXPERTS, 1, D_MODEL))


def _combine_kernel(y4_ref, x1_ref, topw_ref, g2_ref, fw_ref, o_ref, *, per_row_mod):
    w = topw_ref[...]
    n = w.shape[0]
    ff = None
    for k in range(TOP_K):
        yk = w[:, k:k + 1] * _load_packed(y4_ref, n, first_row=k, row_stride=TOP_K)
        ff = yk if ff is None else ff + yk
    g2 = g2_ref[...] if per_row_mod else g2_ref[0]
    x = x1_ref[...] + g2 * ff
    ms = jnp.mean(x * x, axis=-1, keepdims=True)
    o_ref[...] = x * lax.rsqrt(ms + NORM_EPS) * fw_ref[...]


def _combine(y4, x1_rows, topw, g2, fw, row_offset, rows_per_mod):
    n = x1_rows.shape[0]
    tile = COMBINE_TILE
    off = row_offset // tile
    per_row = rows_per_mod == 1
    if per_row:
        g2_spec = pl.BlockSpec((tile, D_MODEL), lambda i: (i, 0))
    else:
        g2 = g2.reshape(g2.shape[0], 1, D_MODEL)
        g2_spec = pl.BlockSpec((1, 1, D_MODEL), lambda i: ((i * tile) // rows_per_mod, 0, 0))
    return pl.pallas_call(
        functools.partial(_combine_kernel, per_row_mod=per_row),
        grid=(n // tile,),
        in_specs=[pl.BlockSpec((PACK_ROWS, tile * TOP_K, LANES), lambda i: (0, i + off, 0)),
                  pl.BlockSpec((tile, D_MODEL), lambda i: (i, 0)),
                  pl.BlockSpec((tile, LANES), lambda i: (i + off, 0)),
                  g2_spec, _const_spec(fw.shape)],
        out_specs=pl.BlockSpec((tile, D_MODEL), lambda i: (i, 0)),
        out_shape=jax.ShapeDtypeStruct((n, D_MODEL), F32),
        compiler_params=pltpu.CompilerParams(dimension_semantics=("arbitrary",), vmem_limit_bytes=VMEM_LIMIT),
        name="combine",
    )(y4, x1_rows, topw, g2, fw)


def _routing_tables(top_idx, n_slots):
    n_assign = top_idx.shape[0] * TOP_K
    e_flat = top_idx.reshape(n_assign)
    onehot = (e_flat[:, None] == jnp.arange(N_EXPERTS, dtype=jnp.int32)[None, :]).astype(jnp.int32)
    running = jnp.cumsum(onehot, axis=0)
    counts = running[-1]
    padded = ((counts + SLOT_TILE - 1) // SLOT_TILE) * SLOT_TILE
    pend = jnp.cumsum(padded)
    poff = pend - padded
    slot = jnp.sum(onehot * (running - 1 + poff[None, :]), axis=1).astype(jnp.int32)
    token = jnp.arange(n_assign, dtype=jnp.int32) // TOP_K
    filler = jnp.arange(n_slots, dtype=jnp.int32) % top_idx.shape[0]
    slot_token = filler.at[slot].set(token, unique_indices=True)
    n_tiles = n_slots // SLOT_TILE
    n_valid = (pend[-1] // SLOT_TILE).astype(jnp.int32)
    tile_row = jnp.minimum(jnp.arange(n_tiles, dtype=jnp.int32), n_valid - 1) * SLOT_TILE
    tile_e = jnp.sum((pend[None, :] <= tile_row[:, None]).astype(jnp.int32), axis=1).astype(jnp.int32)
    return slot_token, slot, tile_e, n_valid.reshape(1)


def _round_up(n, m):
    return ((n + m - 1) // m) * m


def kernel(x_prompt, x_sample, c_prompt, c_sample, state_ret, state_s5_re, state_s5_im, norm1_w, norm2_w, w_ada, b_ada, w_in, ret_norm_w, s5_lam_re, s5_lam_im, s5_log_dt, s5_b_re, s5_b_im, s5_c_re, s5_c_im, s5_d, w_glu, b_glu, s5_norm_w, w_out, w_router, b_router, w1, b1, w2, b2, final_w):
    bp, lp, _ = x_prompt.shape
    bs, ls, _ = x_sample.shape
    assert norm1_w.shape[0] == 1, "single-layer model"
    n_p, n_s = bp * lp, bs * ls
    n_tok = n_p + n_s

    mod = _ada(jnp.concatenate([c_prompt, c_sample], axis=0), w_ada[0], b_ada[0])
    mod_p, mod_s = mod[:bp], jnp.repeat(mod[bp:], ls, axis=0)
    d3 = 3 * D_MODEL

    lbr, lbi, bbr, bbi = _s5prep(s5_lam_re[0], s5_lam_im[0], s5_log_dt[0], s5_b_re[0], s5_b_im[0])
    bmat = jnp.concatenate([_block_diag(bbr), _block_diag(bbi)], axis=-1).astype(BF16)
    cre = _block_diag(jnp.transpose(s5_c_re[0], (0, 2, 1))).astype(BF16)
    cim = _block_diag(jnp.transpose(-s5_c_im[0], (0, 2, 1))).astype(BF16)
    wts = dict(
        n1w=norm1_w, w_in=w_in[0].astype(BF16), rnw=ret_norm_w, bmat=bmat, cre=cre, cim=cim,
        lbr=lbr.reshape(1, SSM_CH), lbi=lbi.reshape(1, SSM_CH), dsk=s5_d[0].reshape(1, SSM_WIDTH),
        w_glu=w_glu[0].astype(BF16), b_glu=b_glu, snw=s5_norm_w, w_out=w_out[0].astype(BF16),
    )

    zero_states = (jnp.zeros((bp, RET_HEADS, HEAD_DIM, HEAD_DIM), F32), jnp.zeros((bp, SSM_CH), F32),
                   jnp.zeros((bp, SSM_CH), F32))
    x1_p, ret_p, re_p, im_p = _mixer(x_prompt, mod_p[:, :d3], jnp.arange(lp, dtype=F32), zero_states, wts,
                                     prompt=True)
    sample_states = (state_ret[0], state_s5_re[0].reshape(bs, SSM_CH), state_s5_im[0].reshape(bs, SSM_CH))
    x1_s, ret_s, re_s, im_s = _mixer(x_sample.reshape(n_s, D_MODEL), mod_s[:, :d3],
                                     PAST_LEN + jnp.arange(ls, dtype=F32), sample_states, wts, prompt=False)

    wr_pad = jnp.pad(w_router[0], ((0, 0), (0, LANES - N_EXPERTS)))
    br_pad = jnp.pad(b_router, ((0, 0), (0, LANES - N_EXPERTS)), constant_values=-1e30)
    x1_p_rows = x1_p.reshape(n_p, D_MODEL)
    h2, topi, topw = _router(x1_p_rows, mod_p[:, d3:d3 + 2 * D_MODEL], lp, x1_s, mod_s[:, d3:d3 + 2 * D_MODEL],
                             norm2_w, wr_pad, br_pad)

    n_assign = n_tok * TOP_K
    gather_quantum = SC_GATHER_WINDOW * SC_WORKERS // PACK_ROWS
    assert n_assign % gather_quantum == 0
    n_slots = _round_up(_round_up(n_assign, SLOT_TILE) + N_EXPERTS * SLOT_TILE, gather_quantum)
    slot_token, slot, tile_e, n_valid = _routing_tables(topi[:, :TOP_K], n_slots)
    xs = _gather_packed(h2, slot_token)
    ys = _experts(tile_e, n_valid, xs, w1[0], b1[0], w2[0], b2[0])
    y4 = _gather_packed(ys, slot)

    g2_p, g2_s = mod_p[:, 5 * D_MODEL:], mod_s[:, 5 * D_MODEL:]
    fw = final_w.reshape(1, D_MODEL)
    y_p = _combine(y4, x1_p_rows, topw, g2_p, fw, 0, lp)
    y_s = _combine(y4, x1_s, topw, g2_s, fw, n_p, 1)

    g, p = SSM_GROUPS, SSM_STATE
    return (y_p.reshape(bp, lp, D_MODEL), y_s.reshape(bs, ls, D_MODEL),
            ret_p[None], re_p.reshape(1, bp, g, p), im_p.reshape(1, bp, g, p),
            ret_s[None], re_s.reshape(1, bs, g, p), im_s.reshape(1, bs, g, p))
```

```python
import functools
import math

import jax
import jax.numpy as jnp
from jax import lax
from jax.experimental import pallas as pl
from jax.experimental.pallas import tpu as pltpu
from jax.experimental.pallas import tpu_sc as plsc

F32 = jnp.float32
BF16 = jnp.bfloat16
HIGHEST = lax.Precision.HIGHEST

D_MODEL = 1024
PAST_LEN = 16384
RET_WIDTH = 512
RET_HEADS = 4
HEAD_DIM = 128
ROPE_BASE = 10000.0
SSM_WIDTH = 512
SSM_GROUP = 16
SSM_GROUPS = 32
SSM_STATE = 64
SSM_CH = SSM_GROUPS * SSM_STATE
IN_WIDTH = 4 * RET_WIDTH + SSM_WIDTH
N_EXPERTS = 32
TOP_K = 4
D_FF = 1024
SWIGLU_LIMIT = 7.0
SWIGLU_ALPHA = 1.702
NORM_EPS = 1e-6

LANES = 128
SUBLANES = 8
VMEM_LIMIT = 56 * 1024 * 1024

SEQ_PER_BLOCK = 8
PROMPT_CHUNK = 64
S5_BLOCK_GROUPS = 8
N_S5_BLOCKS = SSM_GROUPS // S5_BLOCK_GROUPS
S5_BLOCK_IN = S5_BLOCK_GROUPS * SSM_GROUP
S5_BLOCK_CH = S5_BLOCK_GROUPS * SSM_STATE
ROUTER_TILE = 512
SLOT_TILE = 256
COMBINE_TILE = 256
SC_GATHER_WINDOW = 128
SC_WORKERS = 32


def _silu(x):
    return x * jax.nn.sigmoid(x)


def _ada_kernel(c_ref, w_ref, b_ref, o_ref):
    s = _silu(c_ref[...])
    o_ref[...] = jnp.dot(s, w_ref[...], precision=HIGHEST, preferred_element_type=F32) + b_ref[...]


def _ada(c_all, w_ada, b_ada):
    n_rows, n_out = c_all.shape[0], w_ada.shape[1]
    tn = 1536
    return pl.pallas_call(
        _ada_kernel,
        grid=(n_out // tn,),
        in_specs=[
            pl.BlockSpec((n_rows, D_MODEL), lambda j: (0, 0)),
            pl.BlockSpec((D_MODEL, tn), lambda j: (0, j)),
            pl.BlockSpec((1, tn), lambda j: (0, j)),
        ],
        out_specs=pl.BlockSpec((n_rows, tn), lambda j: (0, j)),
        out_shape=jax.ShapeDtypeStruct((n_rows, n_out), F32),
        compiler_params=pltpu.CompilerParams(dimension_semantics=("arbitrary",), vmem_limit_bytes=VMEM_LIMIT),
        name="ada",
    )(c_all, w_ada, b_ada.reshape(1, n_out))


def _s5prep_kernel(lre_ref, lim_ref, ldt_ref, bre_ref, bim_ref, lbr_ref, lbi_ref, bbr_ref, bbi_ref):
    lam_re, lam_im = lre_ref[...], lim_ref[...]
    dt = jnp.exp(ldt_ref[...])
    mag = jnp.exp(lam_re * dt)
    ang = lam_im * dt
    lb_re, lb_im = mag * jnp.cos(ang), mag * jnp.sin(ang)
    den = lam_re * lam_re + lam_im * lam_im
    f_re = ((lb_re - 1.0) * lam_re + lb_im * lam_im) / den
    f_im = (lb_im * lam_re - (lb_re - 1.0) * lam_im) / den
    lbr_ref[...] = lb_re
    lbi_ref[...] = lb_im
    b_re, b_im = bre_ref[...], bim_ref[...]
    bbr_ref[...] = f_re[:, None, :] * b_re - f_im[:, None, :] * b_im
    bbi_ref[...] = f_re[:, None, :] * b_im + f_im[:, None, :] * b_re


def _s5prep(lam_re, lam_im, log_dt, b_re, b_im):
    g, p = lam_re.shape
    bt_re = jnp.transpose(b_re, (0, 2, 1))
    bt_im = jnp.transpose(b_im, (0, 2, 1))
    return pl.pallas_call(
        _s5prep_kernel,
        out_shape=(
            jax.ShapeDtypeStruct((g, p), F32), jax.ShapeDtypeStruct((g, p), F32),
            jax.ShapeDtypeStruct((g, SSM_GROUP, p), F32), jax.ShapeDtypeStruct((g, SSM_GROUP, p), F32),
        ),
        name="s5prep",
    )(lam_re, lam_im, log_dt.reshape(g, 1), bt_re, bt_im)


def _block_diag(blocks):
    _, r, c = blocks.shape
    b4 = blocks.reshape(N_S5_BLOCKS, S5_BLOCK_GROUPS, r, c)
    eye = jnp.eye(S5_BLOCK_GROUPS, dtype=blocks.dtype)
    out = b4[:, :, :, None, :] * eye[None, :, None, :, None]
    return out.reshape(N_S5_BLOCKS, S5_BLOCK_GROUPS * r, S5_BLOCK_GROUPS * c)


def _mixer_kernel(x_ref, mod_ref, n1w_ref, win_ref, cos_ref, sin_ref, dmask_ref, cdec_ref, sdec_ref,
                  rnw_ref, bmat_ref, cre_ref, cim_ref, lbr_ref, lbi_ref, dsk_ref, wglu_ref, bglu_ref,
                  snw_ref, wout_ref, sret0_ref, sre0_ref, sim0_ref,
                  x1_ref, sret_ref, sre_ref, sim_ref,
                  hb_ref, z_ref, zu_ref, oy_ref, utb_ref, bur_ref, bui_ref, ytb_ref, yb_ref,
                  *, n_seq, chunk, tile_rows, carry, chunk_decay):
    rows = n_seq * chunk
    seq_per_tile = tile_rows // chunk
    n_tiles = rows // tile_rows
    per_row_mod = mod_ref.shape[0] == rows

    def load_states():
        sret_ref[...] = sret0_ref[...]
        sre_ref[...] = sre0_ref[...]
        sim_ref[...] = sim0_ref[...]

    if carry:
        pl.when(pl.program_id(0) == 0)(load_states)
    else:
        load_states()

    n1w = n1w_ref[...]
    mod_rows = rows if per_row_mod else chunk
    for i in range(rows // mod_rows):
        r0 = i * mod_rows
        xb = _load_rows(x_ref, r0, mod_rows, chunk)
        if per_row_mod:
            sh = mod_ref[pl.ds(r0, mod_rows), pl.ds(0, D_MODEL)]
            sc = mod_ref[pl.ds(r0, mod_rows), pl.ds(D_MODEL, D_MODEL)]
        else:
            sh = mod_ref[pl.ds(i, 1), pl.ds(0, D_MODEL)]
            sc = mod_ref[pl.ds(i, 1), pl.ds(D_MODEL, D_MODEL)]
        ms = jnp.mean(xb * xb, axis=-1, keepdims=True)
        hn = xb * lax.rsqrt(ms + NORM_EPS) * n1w
        hb_ref[pl.ds(r0, mod_rows), :] = (hn * (1.0 + sc) + sh).astype(BF16)
    ret_w = 4 * RET_WIDTH
    z_ref[...] = jnp.dot(hb_ref[...], win_ref[:, pl.ds(0, ret_w)], preferred_element_type=F32)
    zu = jnp.dot(hb_ref[...], win_ref[:, pl.ds(ret_w, SSM_WIDTH)], preferred_element_type=F32)
    for c in range(SSM_WIDTH // LANES):
        zu_ref[c] = zu[:, c * LANES:(c + 1) * LANES]

    cos = cos_ref[...]
    sin = sin_ref[...]
    scale = HEAD_DIM ** -0.5
    if seq_per_tile > 1:
        row_id = lax.broadcasted_iota(jnp.int32, (tile_rows, HEAD_DIM), 0)

    def rope(t):
        return t * cos + pltpu.roll(t, HEAD_DIM // 2, 1) * sin

    def ret_tile(ti, c):
        r0 = pl.multiple_of(ti * tile_rows, tile_rows)
        for h in range(RET_HEADS):
            c0 = h * HEAD_DIM
            q = rope(z_ref[pl.ds(r0, tile_rows), pl.ds(c0, HEAD_DIM)])
            k = rope(z_ref[pl.ds(r0, tile_rows), pl.ds(RET_WIDTH + c0, HEAD_DIM)]) * scale
            v = z_ref[pl.ds(r0, tile_rows), pl.ds(2 * RET_WIDTH + c0, HEAD_DIM)]
            g = z_ref[pl.ds(r0, tile_rows), pl.ds(3 * RET_WIDTH + c0, HEAD_DIM)]
            kd = k * sdec_ref[h]
            if tile_rows < HEAD_DIM:
                pad = jnp.zeros((HEAD_DIM - tile_rows, HEAD_DIM), F32)
                k, v, kd = (jnp.concatenate([t, pad], axis=0) for t in (k, v, kd))
                if seq_per_tile > 1:
                    row_kv = lax.broadcasted_iota(jnp.int32, (HEAD_DIM, HEAD_DIM), 0)
            elif seq_per_tile > 1:
                row_kv = row_id
            qb, kb, vb = q.astype(BF16), k.astype(BF16), v.astype(BF16)
            s = lax.dot_general(qb, kb, (((1,), (1,)), ((), ())), preferred_element_type=F32) * dmask_ref[h]
            o = jnp.dot(s.astype(BF16), vb, preferred_element_type=F32)
            cross = None
            for si in range(seq_per_tile):
                sidx = ti * seq_per_tile + si
                st = sret_ref[sidx, h]
                cr = jnp.dot(qb, st.astype(BF16), preferred_element_type=F32)
                if seq_per_tile > 1:
                    in_seq = (row_id >= si * chunk) & (row_id < (si + 1) * chunk)
                    cross = jnp.where(in_seq, cr, 0.0 if cross is None else cross)
                    kds = jnp.where((row_kv >= si * chunk) & (row_kv < (si + 1) * chunk), kd, 0.0)
                else:
                    cross, kds = cr, kd
                upd = lax.dot_general(kds.astype(BF16), vb, (((0,), (0,)), ((), ())), preferred_element_type=F32)
                sret_ref[sidx, h] = st * chunk_decay[h] + upd
            o = o + cross * cdec_ref[h]
            o = o * lax.rsqrt(jnp.mean(o * o, axis=-1, keepdims=True) + NORM_EPS)
            o = o * rnw_ref[:, pl.ds(c0, HEAD_DIM)] * _silu(g)
            oy_ref[pl.ds(r0, tile_rows), pl.ds(c0, HEAD_DIM)] = o
        return c

    lax.fori_loop(0, n_tiles, ret_tile, 0)

    for t in range(chunk):
        for c in range(SSM_WIDTH // LANES):
            utb_ref[pl.ds(t * n_seq, n_seq), pl.ds(c * LANES, LANES)] = zu_ref[c, pl.ds(t, n_seq, stride=chunk), :]
    for blk in range(N_S5_BLOCKS):
        ub = utb_ref[:, pl.ds(blk * S5_BLOCK_IN, S5_BLOCK_IN)].astype(BF16)
        bu = jnp.dot(ub, bmat_ref[blk], preferred_element_type=F32)
        bur_ref[:, pl.ds(blk * S5_BLOCK_CH, S5_BLOCK_CH)] = bu[:, :S5_BLOCK_CH]
        bui_ref[:, pl.ds(blk * S5_BLOCK_CH, S5_BLOCK_CH)] = bu[:, S5_BLOCK_CH:]

    scan_w = 2 * S5_BLOCK_CH
    for p in range(SSM_CH // scan_w):
        cols = pl.ds(p * scan_w, scan_w)
        lbr = jnp.broadcast_to(lbr_ref[:, cols], (n_seq, scan_w))
        lbi = jnp.broadcast_to(lbi_ref[:, cols], (n_seq, scan_w))

        def scan_step(t, hc, cols=cols, lbr=lbr, lbi=lbi):
            hr, hi = hc
            r0 = pl.multiple_of(t * n_seq, n_seq)
            nr = lbr * hr - lbi * hi + bur_ref[pl.ds(r0, n_seq), cols]
            ni = lbr * hi + lbi * hr + bui_ref[pl.ds(r0, n_seq), cols]
            bur_ref[pl.ds(r0, n_seq), cols] = nr
            bui_ref[pl.ds(r0, n_seq), cols] = ni
            return nr, ni

        h0 = (sre_ref[:, cols], sim_ref[:, cols])
        if chunk <= 8:
            hc = h0
            for t in range(chunk):
                hc = scan_step(t, hc)
        else:
            unroll = 4

            def scan_group(tg, hc):
                for j in range(unroll):
                    hc = scan_step(tg * unroll + j, hc)
                return hc

            hc = lax.fori_loop(0, chunk // unroll, scan_group, h0)
        sre_ref[:, cols] = hc[0]
        sim_ref[:, cols] = hc[1]

    for blk in range(N_S5_BLOCKS):
        cols = pl.ds(blk * S5_BLOCK_CH, S5_BLOCK_CH)
        yb = jnp.dot(bur_ref[:, cols].astype(BF16), cre_ref[blk], preferred_element_type=F32)
        yb = yb + jnp.dot(bui_ref[:, cols].astype(BF16), cim_ref[blk], preferred_element_type=F32)
        ucols = pl.ds(blk * S5_BLOCK_IN, S5_BLOCK_IN)
        ytb_ref[:, ucols] = yb + dsk_ref[:, ucols] * utb_ref[:, ucols]
    for t in range(chunk):
        for c in range(SSM_WIDTH // LANES):
            yb_ref[c, pl.ds(t, n_seq, stride=chunk), :] = ytb_ref[pl.ds(t * n_seq, n_seq), pl.ds(c * LANES, LANES)]

    y = jnp.concatenate([yb_ref[c] for c in range(SSM_WIDTH // LANES)], axis=1)
    y = jax.nn.gelu(y, approximate=True)
    gate = jnp.dot(y.astype(BF16), wglu_ref[...], preferred_element_type=F32) + bglu_ref[...]
    y = y * jax.nn.sigmoid(gate)
    y = y * lax.rsqrt(jnp.mean(y * y, axis=-1, keepdims=True) + NORM_EPS) * snw_ref[...]
    oy_ref[:, pl.ds(RET_WIDTH, SSM_WIDTH)] = y

    mix = jnp.dot(oy_ref[...].astype(BF16), wout_ref[...], preferred_element_type=F32)
    for i in range(rows // mod_rows):
        r0 = i * mod_rows
        if per_row_mod:
            g1 = mod_ref[pl.ds(r0, mod_rows), pl.ds(2 * D_MODEL, D_MODEL)]
        else:
            g1 = mod_ref[pl.ds(i, 1), pl.ds(2 * D_MODEL, D_MODEL)]
        _store_rows(x1_ref, r0, mod_rows, chunk,
                    _load_rows(x_ref, r0, mod_rows, chunk) + g1 * mix[r0:r0 + mod_rows])


def _load_rows(ref, r0, n, chunk):
    if len(ref.shape) == 2:
        return ref[pl.ds(r0, n), :]
    assert n == chunk and r0 % chunk == 0
    return ref[r0 // chunk]


def _store_rows(ref, r0, n, chunk, val):
    if len(ref.shape) == 2:
        ref[pl.ds(r0, n), :] = val
    else:
        assert n == chunk and r0 % chunk == 0
        ref[r0 // chunk] = val


def _const_spec(shape):
    nd = len(shape)
    return pl.BlockSpec(shape, lambda j, _n=nd: (0,) * _n)


def _decay_tables(chunk, tile_rows):
    log_gamma = jnp.log1p(-jnp.exp2(-5.0 - jnp.arange(RET_HEADS, dtype=F32)))
    r = jnp.arange(tile_rows)
    seq, loc = r // chunk, (r % chunk).astype(F32)
    rel = loc[:, None] - loc[None, :]
    ok = (seq[:, None] == seq[None, :]) & (rel >= 0)
    dmask = jnp.where(ok[None], jnp.exp(jnp.where(ok, rel, 0.0)[None] * log_gamma[:, None, None]), 0.0)
    if tile_rows < HEAD_DIM:
        dmask = jnp.pad(dmask, ((0, 0), (0, 0), (0, HEAD_DIM - tile_rows)))
    cdec = jnp.exp((loc[None, :] + 1.0) * log_gamma[:, None])
    sdec = jnp.exp((chunk - 1.0 - loc)[None, :] * log_gamma[:, None])
    bcast = lambda t: jnp.broadcast_to(t[:, :, None], (RET_HEADS, tile_rows, HEAD_DIM))
    return dmask, bcast(cdec), bcast(sdec), log_gamma


def _rope_tables(pos):
    half = HEAD_DIM // 2
    inv_freq = ROPE_BASE ** (-jnp.arange(half, dtype=F32) / half)
    ang = pos[:, None] * inv_freq[None, :]
    cos, sin = jnp.cos(ang), jnp.sin(ang)
    return jnp.concatenate([cos, cos], axis=-1), jnp.concatenate([-sin, sin], axis=-1)


def _mixer(x, mod, pos, states, wts, *, prompt):
    n_seq = SEQ_PER_BLOCK
    if prompt:
        n_total, seq_len, _ = x.shape
        assert n_total == n_seq
        chunk, tile_rows, n_steps = PROMPT_CHUNK, PROMPT_CHUNK, seq_len // PROMPT_CHUNK
        x_spec = pl.BlockSpec((n_seq, chunk, D_MODEL), lambda j: (0, j, 0))
        mod_spec = _const_spec(mod.shape)
        tab_spec = pl.BlockSpec((chunk, HEAD_DIM), lambda j: (j, 0))
        seq_map = lambda j: 0
    else:
        chunk = pos.shape[0]
        tile_rows = SUBLANES
        n_total = x.shape[0] // chunk
        n_steps = n_total // n_seq
        x_spec = pl.BlockSpec((n_seq * chunk, D_MODEL), lambda j: (j, 0))
        mod_spec = pl.BlockSpec((n_seq * chunk, mod.shape[1]), lambda j: (j, 0))
        tab_spec = _const_spec((tile_rows, HEAD_DIM))
        seq_map = lambda j: j
    rows = n_seq * chunk
    cos, sin = _rope_tables(pos)
    if not prompt:
        reps = tile_rows // chunk
        cos, sin = jnp.tile(cos, (reps, 1)), jnp.tile(sin, (reps, 1))
    dmask, cdec, sdec, log_gamma = _decay_tables(chunk, tile_rows)
    chunk_decay = tuple(float(math.exp(chunk * math.log1p(-2.0 ** (-5.0 - h)))) for h in range(RET_HEADS))
    del log_gamma
    sret0, sre0, sim0 = states

    st_ret_spec = pl.BlockSpec((n_seq, RET_HEADS, HEAD_DIM, HEAD_DIM), lambda j: (seq_map(j), 0, 0, 0))
    st_s5_spec = pl.BlockSpec((n_seq, SSM_CH), lambda j: (seq_map(j), 0))
    consts = [dmask, cdec, sdec, wts["rnw"], wts["bmat"], wts["cre"], wts["cim"], wts["lbr"], wts["lbi"],
              wts["dsk"], wts["w_glu"], wts["b_glu"], wts["snw"], wts["w_out"]]
    args = [x, mod, wts["n1w"], wts["w_in"], cos, sin] + consts + [sret0, sre0, sim0]
    in_specs = ([x_spec, mod_spec, _const_spec(wts["n1w"].shape), _const_spec(wts["w_in"].shape), tab_spec, tab_spec]
                + [_const_spec(a.shape) for a in consts] + [st_ret_spec, st_s5_spec, st_s5_spec])

    kern = functools.partial(_mixer_kernel, n_seq=n_seq, chunk=chunk, tile_rows=tile_rows, carry=prompt,
                             chunk_decay=chunk_decay)
    out_shape = (
        jax.ShapeDtypeStruct(x.shape, F32),
        jax.ShapeDtypeStruct((n_total, RET_HEADS, HEAD_DIM, HEAD_DIM), F32),
        jax.ShapeDtypeStruct((n_total, SSM_CH), F32),
        jax.ShapeDtypeStruct((n_total, SSM_CH), F32),
    )
    scratch = [
        pltpu.VMEM((rows, D_MODEL), BF16),
        pltpu.VMEM((rows, 4 * RET_WIDTH), F32),
        pltpu.VMEM((SSM_WIDTH // LANES, rows, LANES), F32),
        pltpu.VMEM((rows, D_MODEL), F32),
        pltpu.VMEM((rows, SSM_WIDTH), F32),
        pltpu.VMEM((rows, SSM_CH), F32),
        pltpu.VMEM((rows, SSM_CH), F32),
        pltpu.VMEM((rows, SSM_WIDTH), F32),
        pltpu.VMEM((SSM_WIDTH // LANES, rows, LANES), F32),
    ]
    return pl.pallas_call(
        kern,
        grid=(n_steps,),
        in_specs=in_specs,
        out_specs=(x_spec, st_ret_spec, st_s5_spec, st_s5_spec),
        out_shape=out_shape,
        scratch_shapes=scratch,
        compiler_params=pltpu.CompilerParams(dimension_semantics=("arbitrary",), vmem_limit_bytes=VMEM_LIMIT),
        name="mixer_prompt" if prompt else "mixer_sample",
    )(*args)


PACK_ROWS = D_MODEL // (2 * LANES)


def _store_packed(ref, x):
    half = D_MODEL // 2
    bits = lax.bitcast_convert_type(x.astype(BF16).astype(F32), jnp.uint32)
    words = bits[:, :half] | (bits[:, half:] >> 16)
    for c in range(PACK_ROWS):
        ref[c] = words[:, c * LANES:(c + 1) * LANES]


def _load_packed(ref, n, first_row=0, row_stride=1):
    hi, lo = [], []
    for c in range(PACK_ROWS):
        w = ref[c] if row_stride == 1 else ref[c, pl.ds(first_row, n, stride=row_stride), :]
        hi.append(lax.bitcast_convert_type(w & jnp.uint32(0xFFFF0000), F32))
        lo.append(lax.bitcast_convert_type(w << 16, F32))
    return jnp.concatenate(hi + lo, axis=1)


def _route_tile(x, sh, sc, n2w_ref, wr_ref, br_ref, h2_ref, topi_ref, topw_ref):
    ms = jnp.mean(x * x, axis=-1, keepdims=True)
    h2 = x * lax.rsqrt(ms + NORM_EPS) * n2w_ref[...] * (1.0 + sc) + sh
    _store_packed(h2_ref, h2)
    logits = jnp.dot(h2, wr_ref[...], precision=HIGHEST, preferred_element_type=F32) + br_ref[...]
    lane = lax.broadcasted_iota(jnp.int32, logits.shape, 1)
    work = logits
    vals, idxs = [], []
    for _ in range(TOP_K):
        m = jnp.max(work, axis=-1, keepdims=True)
        idx = jnp.min(jnp.where(work == m, lane, LANES), axis=-1, keepdims=True)
        vals.append(m)
        idxs.append(idx)
        work = jnp.where(lane == idx, -jnp.inf, work)
    exps = [jnp.exp(v - vals[0]) for v in vals]
    tot = exps[0] + exps[1] + exps[2] + exps[3]
    topi = jnp.zeros(logits.shape, jnp.int32)
    topw = jnp.zeros(logits.shape, F32)
    for k in range(TOP_K):
        topi = jnp.where(lane == k, idxs[k], topi)
        topw = jnp.where(lane == k, exps[k] / tot, topw)
    topi_ref[...] = topi
    topw_ref[...] = topw


def _router_kernel(xp_ref, modp_ref, xs_ref, mods_ref, n2w_ref, wr_ref, br_ref, h2_ref, topi_ref, topw_ref,
                   *, n_prompt_tiles):
    i = pl.program_id(0)
    consts_outs = (n2w_ref, wr_ref, br_ref, h2_ref, topi_ref, topw_ref)

    @pl.when(i < n_prompt_tiles)
    def _():
        _route_tile(xp_ref[...], modp_ref[0, :, pl.ds(0, D_MODEL)], modp_ref[0, :, pl.ds(D_MODEL, D_MODEL)],
                    *consts_outs)

    @pl.when(i >= n_prompt_tiles)
    def _():
        _route_tile(xs_ref[...], mods_ref[:, pl.ds(0, D_MODEL)], mods_ref[:, pl.ds(D_MODEL, D_MODEL)],
                    *consts_outs)


def _router(xp_rows, mod_p, seq_len, xs_rows, mod_s, n2w, wr_pad, br_pad):
    tile = ROUTER_TILE
    n_p, n_s = xp_rows.shape[0], xs_rows.shape[0]
    tp, ts = n_p // tile, n_s // tile
    n_total = n_p + n_s
    mod_p = mod_p.reshape(mod_p.shape[0], 1, mod_p.shape[1])
    clamp_p = lambda i: jnp.minimum(i, tp - 1)
    clamp_s = lambda i: jnp.maximum(i - tp, 0)
    return pl.pallas_call(
        functools.partial(_router_kernel, n_prompt_tiles=tp),
        grid=(tp + ts,),
        in_specs=[pl.BlockSpec((tile, D_MODEL), lambda i: (clamp_p(i), 0)),
                  pl.BlockSpec((1, 1, mod_p.shape[2]), lambda i: ((clamp_p(i) * tile) // seq_len, 0, 0)),
                  pl.BlockSpec((tile, D_MODEL), lambda i: (clamp_s(i), 0)),
                  pl.BlockSpec((tile, mod_s.shape[1]), lambda i: (clamp_s(i), 0)),
                  _const_spec(n2w.shape), _const_spec(wr_pad.shape), _const_spec(br_pad.shape)],
        out_specs=(pl.BlockSpec((PACK_ROWS, tile, LANES), lambda i: (0, i, 0)),
                   pl.BlockSpec((tile, LANES), lambda i: (i, 0)),
                   pl.BlockSpec((tile, LANES), lambda i: (i, 0))),
        out_shape=(jax.ShapeDtypeStruct((PACK_ROWS, n_total, LANES), jnp.uint32),
                   jax.ShapeDtypeStruct((n_total, LANES), jnp.int32),
                   jax.ShapeDtypeStruct((n_total, LANES), F32)),
        compiler_params=pltpu.CompilerParams(dimension_semantics=("arbitrary",), vmem_limit_bytes=VMEM_LIMIT),
        name="router",
    )(xp_rows, mod_p, xs_rows, mod_s, n2w, wr_pad, br_pad)


def _gather_rows(table, idx):
    n = idx.shape[0]
    steps = n // SC_GATHER_WINDOW
    assert n % SC_GATHER_WINDOW == 0 and steps % SC_WORKERS == 0
    mesh = plsc.VectorSubcoreMesh(core_axis_name="c", subcore_axis_name="s")

    @functools.partial(pl.kernel, out_type=jax.ShapeDtypeStruct((n, table.shape[1]), table.dtype), mesh=mesh,
                       scratch_types=[])
    def gather_kernel(table_hbm, idx_hbm, out_hbm):
        def body(idx_vmem, out_vmem):
            pltpu.sync_copy(table_hbm.at[idx_vmem.at[0]], out_vmem)

        pltpu.emit_pipeline(
            body,
            grid=(steps,),
            in_specs=[pl.BlockSpec((1, SC_GATHER_WINDOW), lambda i: (0, i))],
            out_specs=[pl.BlockSpec((SC_GATHER_WINDOW, table.shape[1]), lambda i: (i, 0))],
            core_axis_name=("c", "s"),
            dimension_semantics=(pltpu.PARALLEL,),
        )(idx_hbm, out_hbm)

    return gather_kernel(table, idx.reshape(1, n))


def _scatter_rows(table, idx, src_block, n_out):
    n = idx.shape[0]
    steps = n // SC_GATHER_WINDOW
    assert n % SC_GATHER_WINDOW == 0 and steps % SC_WORKERS == 0
    mesh = plsc.VectorSubcoreMesh(core_axis_name="c", subcore_axis_name="s")

    @functools.partial(pl.kernel, out_type=jax.ShapeDtypeStruct((n_out, table.shape[1]), table.dtype), mesh=mesh,
                       scratch_types=[])
    def scatter_kernel(table_hbm, idx_hbm, out_hbm):
        def body(rows_vmem, idx_vmem):
            pltpu.sync_copy(rows_vmem, out_hbm.at[idx_vmem.at[0]])

        pltpu.emit_pipeline(
            body,
            grid=(steps,),
            in_specs=[pl.BlockSpec((SC_GATHER_WINDOW, table.shape[1]), lambda g: (src_block(g), 0)),
                      pl.BlockSpec((1, SC_GATHER_WINDOW), lambda g: (0, g))],
            out_specs=[],
            core_axis_name=("c", "s"),
            dimension_semantics=(pltpu.PARALLEL,),
        )(table_hbm, idx_hbm)

    return scatter_kernel(table, idx.reshape(1, n))


def _dispatch_packed(table, slot, n_slots):
    planes, n_tok, lanes = table.shape
    blocks = n_tok // SC_GATHER_WINDOW
    slot_kt = slot.reshape(n_tok, TOP_K).T
    idx = slot_kt[None, :, :] + (jnp.arange(planes, dtype=jnp.int32) * n_slots)[:, None, None]
    src_block = lambda g: (g // (TOP_K * blocks)) * blocks + g % blocks
    out = _scatter_rows(table.reshape(planes * n_tok, lanes), idx.reshape(-1), src_block, planes * n_slots)
    return out.reshape(planes, n_slots, lanes)


def _gather_packed(table, rows):
    planes, n_table, lanes = table.shape
    idx = jnp.concatenate([rows + c * n_table for c in range(planes)])
    out = _gather_rows(table.reshape(planes * n_table, lanes), idx)
    return out.reshape(planes, rows.shape[0], lanes)


def _experts_kernel(te_ref, tr_ref, nv_ref, xs_ref, w1_ref, b1_ref, w2_ref, b2_ref, ys_ref, w1b_ref, w2b_ref):
    i = pl.program_id(0)
    prev = te_ref[jnp.maximum(i - 1, 0)]
    new_expert = (i == 0) | (te_ref[i] != prev)

    @pl.when(new_expert)
    def _():
        w1b_ref[...] = w1_ref[0].astype(BF16)
        w2b_ref[...] = w2_ref[0].astype(BF16)

    @pl.when(i < nv_ref[0])
    def _():
        row = lax.broadcasted_iota(jnp.int32, (SLOT_TILE, D_MODEL), 0)
        x = jnp.where(row < tr_ref[i], _load_packed(xs_ref, SLOT_TILE), 0.0).astype(BF16)
        hu = jnp.dot(x, w1b_ref[...], preferred_element_type=F32) + b1_ref[0]
        x_glu = jnp.minimum(hu[:, :D_FF], SWIGLU_LIMIT)
        x_lin = jnp.clip(hu[:, D_FF:], -SWIGLU_LIMIT, SWIGLU_LIMIT)
        act = x_glu * jax.nn.sigmoid(SWIGLU_ALPHA * x_glu) * (x_lin + 1.0)
        _store_packed(ys_ref, jnp.dot(act.astype(BF16), w2b_ref[...], preferred_element_type=F32) + b2_ref[0])

    @pl.when(i >= nv_ref[0])
    def _():
        ys_ref[...] = jnp.zeros(ys_ref.shape, jnp.uint32)


def _experts(tile_expert, tile_rows, n_valid, xs, w1, b1, w2, b2):
    n_slots = xs.shape[1]
    n_tiles = n_slots // SLOT_TILE
    grid_spec = pltpu.PrefetchScalarGridSpec(
        num_scalar_prefetch=3,
        grid=(n_tiles,),
        in_specs=[
            pl.BlockSpec((PACK_ROWS, SLOT_TILE, LANES), lambda i, te, tr, nv: (0, i, 0)),
            pl.BlockSpec((1, D_MODEL, 2 * D_FF), lambda i, te, tr, nv: (te[i], 0, 0)),
            pl.BlockSpec((1, 1, 2 * D_FF), lambda i, te, tr, nv: (te[i], 0, 0)),
            pl.BlockSpec((1, D_FF, D_MODEL), lambda i, te, tr, nv: (te[i], 0, 0)),
            pl.BlockSpec((1, 1, D_MODEL), lambda i, te, tr, nv: (te[i], 0, 0)),
        ],
        out_specs=pl.BlockSpec((PACK_ROWS, SLOT_TILE, LANES), lambda i, te, tr, nv: (0, i, 0)),
        scratch_shapes=[pltpu.VMEM((D_MODEL, 2 * D_FF), BF16), pltpu.VMEM((D_FF, D_MODEL), BF16)],
    )
    return pl.pallas_call(
        _experts_kernel,
        grid_spec=grid_spec,
        out_shape=jax.ShapeDtypeStruct((PACK_ROWS, n_slots, LANES), jnp.uint32),
        compiler_params=pltpu.CompilerParams(dimension_semantics=("arbitrary",), vmem_limit_bytes=VMEM_LIMIT),
        name="experts",
    )(tile_expert, tile_rows, n_valid, xs, w1, b1.reshape(N_EXPERTS, 1, 2 * D_FF), w2,
      b2.reshape(N_EXPERTS, 1, D_MODEL))


def _combine_kernel(y4_ref, x1_ref, topw_ref, g2_ref, fw_ref, o_ref, *, per_row_mod):
    w = topw_ref[...]
    n = w.shape[0]
    ff = None
    for k in range(TOP_K):
        yk = w[:, k:k + 1] * _load_packed(y4_ref, n, first_row=k, row_stride=TOP_K)
        ff = yk if ff is None else ff + yk
    g2 = g2_ref[...] if per_row_mod else g2_ref[0]
    x = x1_ref[...] + g2 * ff
    ms = jnp.mean(x * x, axis=-1, keepdims=True)
    o_ref[...] = x * lax.rsqrt(ms + NORM_EPS) * fw_ref[...]


def _combine(y4, x1_rows, topw, g2, fw, row_offset, rows_per_mod):
    n = x1_rows.shape[0]
    tile = COMBINE_TILE
    off = row_offset // tile
    per_row = rows_per_mod == 1
    if per_row:
        g2_spec = pl.BlockSpec((tile, D_MODEL), lambda i: (i, 0))
    else:
        g2 = g2.reshape(g2.shape[0], 1, D_MODEL)
        g2_spec = pl.BlockSpec((1, 1, D_MODEL), lambda i: ((i * tile) // rows_per_mod, 0, 0))
    return pl.pallas_call(
        functools.partial(_combine_kernel, per_row_mod=per_row),
        grid=(n // tile,),
        in_specs=[pl.BlockSpec((PACK_ROWS, tile * TOP_K, LANES), lambda i: (0, i + off, 0)),
                  pl.BlockSpec((tile, D_MODEL), lambda i: (i, 0)),
                  pl.BlockSpec((tile, LANES), lambda i: (i + off, 0)),
                  g2_spec, _const_spec(fw.shape)],
        out_specs=pl.BlockSpec((tile, D_MODEL), lambda i: (i, 0)),
        out_shape=jax.ShapeDtypeStruct((n, D_MODEL), F32),
        compiler_params=pltpu.CompilerParams(dimension_semantics=("arbitrary",), vmem_limit_bytes=VMEM_LIMIT),
        name="combine",
    )(y4, x1_rows, topw, g2, fw)


def _routing_tables(top_idx, n_slots):
    n_assign = top_idx.shape[0] * TOP_K
    e_flat = top_idx.reshape(n_assign)
    onehot = (e_flat[:, None] == jnp.arange(N_EXPERTS, dtype=jnp.int32)[None, :]).astype(jnp.int32)
    running = jnp.cumsum(onehot, axis=0)
    counts = running[-1]
    padded = ((counts + SLOT_TILE - 1) // SLOT_TILE) * SLOT_TILE
    pend = jnp.cumsum(padded)
    poff = pend - padded
    slot = jnp.sum(onehot * (running - 1 + poff[None, :]), axis=1).astype(jnp.int32)
    n_tiles = n_slots // SLOT_TILE
    n_valid = (pend[-1] // SLOT_TILE).astype(jnp.int32)
    tile_row = jnp.minimum(jnp.arange(n_tiles, dtype=jnp.int32), n_valid - 1) * SLOT_TILE
    in_later = (pend[None, :] <= tile_row[:, None]).astype(jnp.int32)
    tile_e = jnp.sum(in_later, axis=1).astype(jnp.int32)
    is_e = (jnp.arange(N_EXPERTS, dtype=jnp.int32)[None, :] == tile_e[:, None]).astype(jnp.int32)
    used_end = jnp.sum(is_e * (poff + counts)[None, :], axis=1)
    tile_rows = jnp.clip(used_end - tile_row, 0, SLOT_TILE).astype(jnp.int32)
    return slot, tile_e, tile_rows, n_valid.reshape(1)


def _round_up(n, m):
    return ((n + m - 1) // m) * m


def kernel(x_prompt, x_sample, c_prompt, c_sample, state_ret, state_s5_re, state_s5_im, norm1_w, norm2_w, w_ada, b_ada, w_in, ret_norm_w, s5_lam_re, s5_lam_im, s5_log_dt, s5_b_re, s5_b_im, s5_c_re, s5_c_im, s5_d, w_glu, b_glu, s5_norm_w, w_out, w_router, b_router, w1, b1, w2, b2, final_w):
    bp, lp, _ = x_prompt.shape
    bs, ls, _ = x_sample.shape
    assert norm1_w.shape[0] == 1, "single-layer model"
    n_p, n_s = bp * lp, bs * ls
    n_tok = n_p + n_s

    mod = _ada(jnp.concatenate([c_prompt, c_sample], axis=0), w_ada[0], b_ada[0])
    mod_p, mod_s = mod[:bp], jnp.repeat(mod[bp:], ls, axis=0)
    d3 = 3 * D_MODEL

    lbr, lbi, bbr, bbi = _s5prep(s5_lam_re[0], s5_lam_im[0], s5_log_dt[0], s5_b_re[0], s5_b_im[0])
    bmat = jnp.concatenate([_block_diag(bbr), _block_diag(bbi)], axis=-1).astype(BF16)
    cre = _block_diag(jnp.transpose(s5_c_re[0], (0, 2, 1))).astype(BF16)
    cim = _block_diag(jnp.transpose(-s5_c_im[0], (0, 2, 1))).astype(BF16)
    wts = dict(
        n1w=norm1_w, w_in=w_in[0].astype(BF16), rnw=ret_norm_w, bmat=bmat, cre=cre, cim=cim,
        lbr=lbr.reshape(1, SSM_CH), lbi=lbi.reshape(1, SSM_CH), dsk=s5_d[0].reshape(1, SSM_WIDTH),
        w_glu=w_glu[0].astype(BF16), b_glu=b_glu, snw=s5_norm_w, w_out=w_out[0].astype(BF16),
    )

    zero_states = (jnp.zeros((bp, RET_HEADS, HEAD_DIM, HEAD_DIM), F32), jnp.zeros((bp, SSM_CH), F32),
                   jnp.zeros((bp, SSM_CH), F32))
    x1_p, ret_p, re_p, im_p = _mixer(x_prompt, mod_p[:, :d3], jnp.arange(lp, dtype=F32), zero_states, wts,
                                     prompt=True)
    sample_states = (state_ret[0], state_s5_re[0].reshape(bs, SSM_CH), state_s5_im[0].reshape(bs, SSM_CH))
    x1_s, ret_s, re_s, im_s = _mixer(x_sample.reshape(n_s, D_MODEL), mod_s[:, :d3],
                                     PAST_LEN + jnp.arange(ls, dtype=F32), sample_states, wts, prompt=False)

    wr_pad = jnp.pad(w_router[0], ((0, 0), (0, LANES - N_EXPERTS)))
    br_pad = jnp.pad(b_router, ((0, 0), (0, LANES - N_EXPERTS)), constant_values=-1e30)
    x1_p_rows = x1_p.reshape(n_p, D_MODEL)
    h2, topi, topw = _router(x1_p_rows, mod_p[:, d3:d3 + 2 * D_MODEL], lp, x1_s, mod_s[:, d3:d3 + 2 * D_MODEL],
                             norm2_w, wr_pad, br_pad)

    n_assign = n_tok * TOP_K
    gather_quantum = SC_GATHER_WINDOW * SC_WORKERS // PACK_ROWS
    assert n_assign % gather_quantum == 0
    n_slots = _round_up(_round_up(n_assign, SLOT_TILE) + N_EXPERTS * SLOT_TILE, gather_quantum)
    slot, tile_e, tile_rows, n_valid = _routing_tables(topi[:, :TOP_K], n_slots)
    xs = _dispatch_packed(h2, slot, n_slots)
    ys = _experts(tile_e, tile_rows, n_valid, xs, w1[0], b1[0], w2[0], b2[0])
    y4 = _gather_packed(ys, slot)

    g2_p, g2_s = mod_p[:, 5 * D_MODEL:], mod_s[:, 5 * D_MODEL:]
    fw = final_w.reshape(1, D_MODEL)
    y_p = _combine(y4, x1_p_rows, topw, g2_p, fw, 0, lp)
    y_s = _combine(y4, x1_s, topw, g2_s, fw, n_p, 1)

    g, p = SSM_GROUPS, SSM_STATE
    return (y_p.reshape(bp, lp, D_MODEL), y_s.reshape(bs, ls, D_MODEL),
            ret_p[None], re_p.reshape(1, bp, g, p), im_p.reshape(1, bp, g, p),
            ret_s[None], re_s.reshape(1, bs, g, p), im_s.reshape(1, bs, g, p))
```

```python
import functools
import math

import jax
import jax.numpy as jnp
from jax import lax
from jax.experimental import pallas as pl
from jax.experimental.pallas import tpu as pltpu
from jax.experimental.pallas import tpu_sc as plsc

F32 = jnp.float32
BF16 = jnp.bfloat16
HIGHEST = lax.Precision.HIGHEST

D_MODEL = 1024
PAST_LEN = 16384
RET_WIDTH = 512
RET_HEADS = 4
HEAD_DIM = 128
ROPE_BASE = 10000.0
SSM_WIDTH = 512
SSM_GROUP = 16
SSM_GROUPS = 32
SSM_STATE = 64
SSM_CH = SSM_GROUPS * SSM_STATE
IN_WIDTH = 4 * RET_WIDTH + SSM_WIDTH
N_EXPERTS = 32
TOP_K = 4
D_FF = 1024
SWIGLU_LIMIT = 7.0
SWIGLU_ALPHA = 1.702
NORM_EPS = 1e-6

LANES = 128
SUBLANES = 8
VMEM_LIMIT = 56 * 1024 * 1024

SEQ_PER_BLOCK = 8
PROMPT_CHUNK = 64
S5_BLOCK_GROUPS = 8
N_S5_BLOCKS = SSM_GROUPS // S5_BLOCK_GROUPS
S5_BLOCK_IN = S5_BLOCK_GROUPS * SSM_GROUP
S5_BLOCK_CH = S5_BLOCK_GROUPS * SSM_STATE
ROUTER_TILE = 512
SLOT_TILE = 256
COMBINE_TILE = 256
SC_GATHER_WINDOW = 128
SC_WORKERS = 32


def _silu(x):
    return x * jax.nn.sigmoid(x)


def _ada_kernel(c_ref, w_ref, b_ref, o_ref):
    s = _silu(c_ref[...])
    o_ref[...] = jnp.dot(s, w_ref[...], precision=HIGHEST, preferred_element_type=F32) + b_ref[...]


def _ada(c_all, w_ada, b_ada):
    n_rows, n_out = c_all.shape[0], w_ada.shape[1]
    tn = 1536
    return pl.pallas_call(
        _ada_kernel,
        grid=(n_out // tn,),
        in_specs=[
            pl.BlockSpec((n_rows, D_MODEL), lambda j: (0, 0)),
            pl.BlockSpec((D_MODEL, tn), lambda j: (0, j)),
            pl.BlockSpec((1, tn), lambda j: (0, j)),
        ],
        out_specs=pl.BlockSpec((n_rows, tn), lambda j: (0, j)),
        out_shape=jax.ShapeDtypeStruct((n_rows, n_out), F32),
        compiler_params=pltpu.CompilerParams(dimension_semantics=("arbitrary",), vmem_limit_bytes=VMEM_LIMIT),
        name="ada",
    )(c_all, w_ada, b_ada.reshape(1, n_out))


def _s5prep_kernel(lre_ref, lim_ref, ldt_ref, bre_ref, bim_ref, lbr_ref, lbi_ref, bbr_ref, bbi_ref):
    lam_re, lam_im = lre_ref[...], lim_ref[...]
    dt = jnp.exp(ldt_ref[...])
    mag = jnp.exp(lam_re * dt)
    ang = lam_im * dt
    lb_re, lb_im = mag * jnp.cos(ang), mag * jnp.sin(ang)
    den = lam_re * lam_re + lam_im * lam_im
    f_re = ((lb_re - 1.0) * lam_re + lb_im * lam_im) / den
    f_im = (lb_im * lam_re - (lb_re - 1.0) * lam_im) / den
    lbr_ref[...] = lb_re
    lbi_ref[...] = lb_im
    b_re, b_im = bre_ref[...], bim_ref[...]
    bbr_ref[...] = f_re[:, None, :] * b_re - f_im[:, None, :] * b_im
    bbi_ref[...] = f_re[:, None, :] * b_im + f_im[:, None, :] * b_re


def _s5prep(lam_re, lam_im, log_dt, b_re, b_im):
    g, p = lam_re.shape
    bt_re = jnp.transpose(b_re, (0, 2, 1))
    bt_im = jnp.transpose(b_im, (0, 2, 1))
    return pl.pallas_call(
        _s5prep_kernel,
        out_shape=(
            jax.ShapeDtypeStruct((g, p), F32), jax.ShapeDtypeStruct((g, p), F32),
            jax.ShapeDtypeStruct((g, SSM_GROUP, p), F32), jax.ShapeDtypeStruct((g, SSM_GROUP, p), F32),
        ),
        name="s5prep",
    )(lam_re, lam_im, log_dt.reshape(g, 1), bt_re, bt_im)


def _block_diag(blocks):
    _, r, c = blocks.shape
    b4 = blocks.reshape(N_S5_BLOCKS, S5_BLOCK_GROUPS, r, c)
    eye = jnp.eye(S5_BLOCK_GROUPS, dtype=blocks.dtype)
    out = b4[:, :, :, None, :] * eye[None, :, None, :, None]
    return out.reshape(N_S5_BLOCKS, S5_BLOCK_GROUPS * r, S5_BLOCK_GROUPS * c)


def _mixer_kernel(x_ref, mod_ref, n1w_ref, win_ref, cos_ref, sin_ref, dmask_ref, cdec_ref, sdec_ref,
                  rnw_ref, bmat_ref, cre_ref, cim_ref, lbr_ref, lbi_ref, dsk_ref, wglu_ref, bglu_ref,
                  snw_ref, wout_ref, sret0_ref, sre0_ref, sim0_ref,
                  x1_ref, sret_ref, sre_ref, sim_ref,
                  hb_ref, z_ref, zu_ref, oy_ref, utb_ref, bur_ref, bui_ref, ytb_ref, yb_ref,
                  *, n_seq, chunk, tile_rows, carry, chunk_decay):
    rows = n_seq * chunk
    seq_per_tile = tile_rows // chunk
    n_tiles = rows // tile_rows
    per_row_mod = mod_ref.shape[0] == rows

    def load_states():
        sret_ref[...] = sret0_ref[...]
        sre_ref[...] = sre0_ref[...]
        sim_ref[...] = sim0_ref[...]

    if carry:
        pl.when(pl.program_id(0) == 0)(load_states)
    else:
        load_states()

    n1w = n1w_ref[...]
    mod_rows = rows if per_row_mod else chunk
    for i in range(rows // mod_rows):
        r0 = i * mod_rows
        xb = _load_rows(x_ref, r0, mod_rows, chunk)
        if per_row_mod:
            sh = mod_ref[pl.ds(r0, mod_rows), pl.ds(0, D_MODEL)]
            sc = mod_ref[pl.ds(r0, mod_rows), pl.ds(D_MODEL, D_MODEL)]
        else:
            sh = mod_ref[pl.ds(i, 1), pl.ds(0, D_MODEL)]
            sc = mod_ref[pl.ds(i, 1), pl.ds(D_MODEL, D_MODEL)]
        ms = jnp.mean(xb * xb, axis=-1, keepdims=True)
        hn = xb * lax.rsqrt(ms + NORM_EPS) * n1w
        hb_ref[pl.ds(r0, mod_rows), :] = (hn * (1.0 + sc) + sh).astype(BF16)
    ret_w = 4 * RET_WIDTH
    z_ref[...] = jnp.dot(hb_ref[...], win_ref[:, pl.ds(0, ret_w)], preferred_element_type=F32)
    zu = jnp.dot(hb_ref[...], win_ref[:, pl.ds(ret_w, SSM_WIDTH)], preferred_element_type=F32)
    pitch = zu_ref.shape[1] // n_seq
    for c in range(SSM_WIDTH // LANES):
        for b in range(n_seq if pitch != chunk else 1):
            nb = chunk if pitch != chunk else rows
            zu_ref[c, pl.ds(b * pitch, nb), :] = zu[b * chunk:b * chunk + nb, c * LANES:(c + 1) * LANES]

    cos = cos_ref[...]
    sin = sin_ref[...]
    scale = HEAD_DIM ** -0.5
    if seq_per_tile > 1:
        row_id = lax.broadcasted_iota(jnp.int32, (tile_rows, HEAD_DIM), 0)

    def rope(t):
        return t * cos + pltpu.roll(t, HEAD_DIM // 2, 1) * sin

    def ret_tile(ti, c):
        r0 = pl.multiple_of(ti * tile_rows, tile_rows)
        for h in range(RET_HEADS):
            c0 = h * HEAD_DIM
            q = rope(z_ref[pl.ds(r0, tile_rows), pl.ds(c0, HEAD_DIM)])
            k = rope(z_ref[pl.ds(r0, tile_rows), pl.ds(RET_WIDTH + c0, HEAD_DIM)]) * scale
            v = z_ref[pl.ds(r0, tile_rows), pl.ds(2 * RET_WIDTH + c0, HEAD_DIM)]
            g = z_ref[pl.ds(r0, tile_rows), pl.ds(3 * RET_WIDTH + c0, HEAD_DIM)]
            kd = k * sdec_ref[h]
            if tile_rows < HEAD_DIM:
                pad = jnp.zeros((HEAD_DIM - tile_rows, HEAD_DIM), F32)
                k, v, kd = (jnp.concatenate([t, pad], axis=0) for t in (k, v, kd))
                if seq_per_tile > 1:
                    row_kv = lax.broadcasted_iota(jnp.int32, (HEAD_DIM, HEAD_DIM), 0)
            elif seq_per_tile > 1:
                row_kv = row_id
            qb, kb, vb = q.astype(BF16), k.astype(BF16), v.astype(BF16)
            s = lax.dot_general(qb, kb, (((1,), (1,)), ((), ())), preferred_element_type=F32) * dmask_ref[h]
            o = jnp.dot(s.astype(BF16), vb, preferred_element_type=F32)
            cross = None
            for si in range(seq_per_tile):
                sidx = ti * seq_per_tile + si
                st = sret_ref[sidx, h]
                cr = jnp.dot(qb, st.astype(BF16), preferred_element_type=F32)
                if seq_per_tile > 1:
                    in_seq = (row_id >= si * chunk) & (row_id < (si + 1) * chunk)
                    cross = jnp.where(in_seq, cr, 0.0 if cross is None else cross)
                    kds = jnp.where((row_kv >= si * chunk) & (row_kv < (si + 1) * chunk), kd, 0.0)
                else:
                    cross, kds = cr, kd
                upd = lax.dot_general(kds.astype(BF16), vb, (((0,), (0,)), ((), ())), preferred_element_type=F32)
                sret_ref[sidx, h] = st * chunk_decay[h] + upd
            o = o + cross * cdec_ref[h]
            o = o * lax.rsqrt(jnp.mean(o * o, axis=-1, keepdims=True) + NORM_EPS)
            o = o * rnw_ref[:, pl.ds(c0, HEAD_DIM)] * _silu(g)
            oy_ref[pl.ds(r0, tile_rows), pl.ds(c0, HEAD_DIM)] = o
        return c

    lax.fori_loop(0, n_tiles, ret_tile, 0, unroll=True)

    for t in range(chunk):
        for c in range(SSM_WIDTH // LANES):
            utb_ref[pl.ds(t * n_seq, n_seq), pl.ds(c * LANES, LANES)] = zu_ref[c, pl.ds(t, n_seq, stride=pitch), :]
    for blk in range(N_S5_BLOCKS):
        ub = utb_ref[:, pl.ds(blk * S5_BLOCK_IN, S5_BLOCK_IN)].astype(BF16)
        bu = jnp.dot(ub, bmat_ref[blk], preferred_element_type=F32)
        bur_ref[:, pl.ds(blk * S5_BLOCK_CH, S5_BLOCK_CH)] = bu[:, :S5_BLOCK_CH]
        bui_ref[:, pl.ds(blk * S5_BLOCK_CH, S5_BLOCK_CH)] = bu[:, S5_BLOCK_CH:]

    scan_w = 2 * S5_BLOCK_CH
    for p in range(SSM_CH // scan_w):
        cols = pl.ds(p * scan_w, scan_w)
        lbr = jnp.broadcast_to(lbr_ref[:, cols], (n_seq, scan_w))
        lbi = jnp.broadcast_to(lbi_ref[:, cols], (n_seq, scan_w))

        def scan_step(t, hc, cols=cols, lbr=lbr, lbi=lbi):
            hr, hi = hc
            r0 = pl.multiple_of(t * n_seq, n_seq)
            nr = lbr * hr - lbi * hi + bur_ref[pl.ds(r0, n_seq), cols]
            ni = lbr * hi + lbi * hr + bui_ref[pl.ds(r0, n_seq), cols]
            bur_ref[pl.ds(r0, n_seq), cols] = nr
            bui_ref[pl.ds(r0, n_seq), cols] = ni
            return nr, ni

        h0 = (sre_ref[:, cols], sim_ref[:, cols])
        if chunk <= 8:
            hc = h0
            for t in range(chunk):
                hc = scan_step(t, hc)
        else:
            unroll = 4

            def scan_group(tg, hc):
                for j in range(unroll):
                    hc = scan_step(tg * unroll + j, hc)
                return hc

            hc = lax.fori_loop(0, chunk // unroll, scan_group, h0)
        sre_ref[:, cols] = hc[0]
        sim_ref[:, cols] = hc[1]

    for blk in range(N_S5_BLOCKS):
        cols = pl.ds(blk * S5_BLOCK_CH, S5_BLOCK_CH)
        yb = jnp.dot(bur_ref[:, cols].astype(BF16), cre_ref[blk], preferred_element_type=F32)
        yb = yb + jnp.dot(bui_ref[:, cols].astype(BF16), cim_ref[blk], preferred_element_type=F32)
        ucols = pl.ds(blk * S5_BLOCK_IN, S5_BLOCK_IN)
        ytb_ref[:, ucols] = yb + dsk_ref[:, ucols] * utb_ref[:, ucols]
    for t in range(chunk):
        for c in range(SSM_WIDTH // LANES):
            yb_ref[c, pl.ds(t, n_seq, stride=pitch), :] = ytb_ref[pl.ds(t * n_seq, n_seq), pl.ds(c * LANES, LANES)]

    def seq_major(c):
        if pitch == chunk:
            return yb_ref[c]
        return jnp.concatenate([yb_ref[c, pl.ds(b * pitch, chunk), :] for b in range(n_seq)], axis=0)

    y = jnp.concatenate([seq_major(c) for c in range(SSM_WIDTH // LANES)], axis=1)
    y = jax.nn.gelu(y, approximate=True)
    gate = jnp.dot(y.astype(BF16), wglu_ref[...], preferred_element_type=F32) + bglu_ref[...]
    y = y * jax.nn.sigmoid(gate)
    y = y * lax.rsqrt(jnp.mean(y * y, axis=-1, keepdims=True) + NORM_EPS) * snw_ref[...]
    oy_ref[:, pl.ds(RET_WIDTH, SSM_WIDTH)] = y

    mix = jnp.dot(oy_ref[...].astype(BF16), wout_ref[...], preferred_element_type=F32)
    for i in range(rows // mod_rows):
        r0 = i * mod_rows
        if per_row_mod:
            g1 = mod_ref[pl.ds(r0, mod_rows), pl.ds(2 * D_MODEL, D_MODEL)]
        else:
            g1 = mod_ref[pl.ds(i, 1), pl.ds(2 * D_MODEL, D_MODEL)]
        _store_rows(x1_ref, r0, mod_rows, chunk,
                    _load_rows(x_ref, r0, mod_rows, chunk) + g1 * mix[r0:r0 + mod_rows])


def _load_rows(ref, r0, n, chunk):
    if len(ref.shape) == 2:
        return ref[pl.ds(r0, n), :]
    assert n == chunk and r0 % chunk == 0
    return ref[r0 // chunk]


def _store_rows(ref, r0, n, chunk, val):
    if len(ref.shape) == 2:
        ref[pl.ds(r0, n), :] = val
    else:
        assert n == chunk and r0 % chunk == 0
        ref[r0 // chunk] = val


def _seq_pitch(chunk):
    return chunk + SUBLANES if chunk % SUBLANES == 0 else chunk


def _const_spec(shape):
    nd = len(shape)
    return pl.BlockSpec(shape, lambda j, _n=nd: (0,) * _n)


def _decay_tables(chunk, tile_rows):
    log_gamma = jnp.log1p(-jnp.exp2(-5.0 - jnp.arange(RET_HEADS, dtype=F32)))
    r = jnp.arange(tile_rows)
    seq, loc = r // chunk, (r % chunk).astype(F32)
    rel = loc[:, None] - loc[None, :]
    ok = (seq[:, None] == seq[None, :]) & (rel >= 0)
    dmask = jnp.where(ok[None], jnp.exp(jnp.where(ok, rel, 0.0)[None] * log_gamma[:, None, None]), 0.0)
    if tile_rows < HEAD_DIM:
        dmask = jnp.pad(dmask, ((0, 0), (0, 0), (0, HEAD_DIM - tile_rows)))
    cdec = jnp.exp((loc[None, :] + 1.0) * log_gamma[:, None])
    sdec = jnp.exp((chunk - 1.0 - loc)[None, :] * log_gamma[:, None])
    bcast = lambda t: jnp.broadcast_to(t[:, :, None], (RET_HEADS, tile_rows, HEAD_DIM))
    return dmask, bcast(cdec), bcast(sdec), log_gamma


def _rope_tables(pos):
    half = HEAD_DIM // 2
    inv_freq = ROPE_BASE ** (-jnp.arange(half, dtype=F32) / half)
    ang = pos[:, None] * inv_freq[None, :]
    cos, sin = jnp.cos(ang), jnp.sin(ang)
    return jnp.concatenate([cos, cos], axis=-1), jnp.concatenate([-sin, sin], axis=-1)


def _mixer(x, mod, pos, states, wts, *, prompt):
    n_seq = SEQ_PER_BLOCK
    if prompt:
        n_total, seq_len, _ = x.shape
        assert n_total == n_seq
        chunk, tile_rows, n_steps = PROMPT_CHUNK, PROMPT_CHUNK, seq_len // PROMPT_CHUNK
        x_spec = pl.BlockSpec((n_seq, chunk, D_MODEL), lambda j: (0, j, 0))
        mod_spec = _const_spec(mod.shape)
        tab_spec = pl.BlockSpec((chunk, HEAD_DIM), lambda j: (j, 0))
        seq_map = lambda j: 0
    else:
        chunk = pos.shape[0]
        tile_rows = SUBLANES
        n_total = x.shape[0] // chunk
        n_steps = n_total // n_seq
        x_spec = pl.BlockSpec((n_seq * chunk, D_MODEL), lambda j: (j, 0))
        mod_spec = pl.BlockSpec((n_seq * chunk, mod.shape[1]), lambda j: (j, 0))
        tab_spec = _const_spec((tile_rows, HEAD_DIM))
        seq_map = lambda j: j
    rows = n_seq * chunk
    cos, sin = _rope_tables(pos)
    if not prompt:
        reps = tile_rows // chunk
        cos, sin = jnp.tile(cos, (reps, 1)), jnp.tile(sin, (reps, 1))
    dmask, cdec, sdec, log_gamma = _decay_tables(chunk, tile_rows)
    chunk_decay = tuple(float(math.exp(chunk * math.log1p(-2.0 ** (-5.0 - h)))) for h in range(RET_HEADS))
    del log_gamma
    sret0, sre0, sim0 = states

    st_ret_spec = pl.BlockSpec((n_seq, RET_HEADS, HEAD_DIM, HEAD_DIM), lambda j: (seq_map(j), 0, 0, 0))
    st_s5_spec = pl.BlockSpec((n_seq, SSM_CH), lambda j: (seq_map(j), 0))
    consts = [dmask, cdec, sdec, wts["rnw"], wts["bmat"], wts["cre"], wts["cim"], wts["lbr"], wts["lbi"],
              wts["dsk"], wts["w_glu"], wts["b_glu"], wts["snw"], wts["w_out"]]
    args = [x, mod, wts["n1w"], wts["w_in"], cos, sin] + consts + [sret0, sre0, sim0]
    in_specs = ([x_spec, mod_spec, _const_spec(wts["n1w"].shape), _const_spec(wts["w_in"].shape), tab_spec, tab_spec]
                + [_const_spec(a.shape) for a in consts] + [st_ret_spec, st_s5_spec, st_s5_spec])

    kern = functools.partial(_mixer_kernel, n_seq=n_seq, chunk=chunk, tile_rows=tile_rows, carry=prompt,
                             chunk_decay=chunk_decay)
    out_shape = (
        jax.ShapeDtypeStruct(x.shape, F32),
        jax.ShapeDtypeStruct((n_total, RET_HEADS, HEAD_DIM, HEAD_DIM), F32),
        jax.ShapeDtypeStruct((n_total, SSM_CH), F32),
        jax.ShapeDtypeStruct((n_total, SSM_CH), F32),
    )
    scratch = [
        pltpu.VMEM((rows, D_MODEL), BF16),
        pltpu.VMEM((rows, 4 * RET_WIDTH), F32),
        pltpu.VMEM((SSM_WIDTH // LANES, n_seq * _seq_pitch(chunk), LANES), F32),
        pltpu.VMEM((rows, D_MODEL), F32),
        pltpu.VMEM((rows, SSM_WIDTH), F32),
        pltpu.VMEM((rows, SSM_CH), F32),
        pltpu.VMEM((rows, SSM_CH), F32),
        pltpu.VMEM((rows, SSM_WIDTH), F32),
        pltpu.VMEM((SSM_WIDTH // LANES, n_seq * _seq_pitch(chunk), LANES), F32),
    ]
    return pl.pallas_call(
        kern,
        grid=(n_steps,),
        in_specs=in_specs,
        out_specs=(x_spec, st_ret_spec, st_s5_spec, st_s5_spec),
        out_shape=out_shape,
        scratch_shapes=scratch,
        compiler_params=pltpu.CompilerParams(dimension_semantics=("arbitrary",), vmem_limit_bytes=VMEM_LIMIT),
        name="mixer_prompt" if prompt else "mixer_sample",
    )(*args)


PACK_ROWS = D_MODEL // (2 * LANES)


def _store_packed(ref, x):
    half = D_MODEL // 2
    bits = lax.bitcast_convert_type(x.astype(BF16).astype(F32), jnp.uint32)
    words = bits[:, :half] | (bits[:, half:] >> 16)
    for c in range(PACK_ROWS):
        ref[c] = words[:, c * LANES:(c + 1) * LANES]


def _load_packed(ref, n, first_row=0, row_stride=1):
    hi, lo = [], []
    for c in range(PACK_ROWS):
        w = ref[c] if row_stride == 1 else ref[c, pl.ds(first_row, n, stride=row_stride), :]
        hi.append(lax.bitcast_convert_type(w & jnp.uint32(0xFFFF0000), F32))
        lo.append(lax.bitcast_convert_type(w << 16, F32))
    return jnp.concatenate(hi + lo, axis=1)


def _split_bf16(x):
    hi = x.astype(BF16)
    return hi, (x - hi.astype(F32)).astype(BF16)


def _route_tile(x, sh, sc, n2w_ref, wrh_ref, wrm_ref, br_ref, ltri_ref, count_ref, h2_ref, route_ref, topw_ref):
    ms = jnp.mean(x * x, axis=-1, keepdims=True)
    h2 = x * lax.rsqrt(ms + NORM_EPS) * n2w_ref[...] * (1.0 + sc) + sh
    _store_packed(h2_ref, h2)
    hh, hm = _split_bf16(h2)
    logits = (jnp.dot(hh, wrh_ref[...], preferred_element_type=F32)
              + (jnp.dot(hh, wrm_ref[...], preferred_element_type=F32)
                 + jnp.dot(hm, wrh_ref[...], preferred_element_type=F32))) + br_ref[...]
    lane = lax.broadcasted_iota(jnp.int32, logits.shape, 1)
    work = logits
    vals, idxs = [], []
    for _ in range(TOP_K):
        m = jnp.max(work, axis=-1, keepdims=True)
        idx = jnp.min(jnp.where(work == m, lane, LANES), axis=-1, keepdims=True)
        vals.append(m)
        idxs.append(idx)
        work = jnp.where(lane == idx, -jnp.inf, work)
    exps = [jnp.exp(v - vals[0]) for v in vals]
    tot = exps[0] + exps[1] + exps[2] + exps[3]
    topw = jnp.zeros(logits.shape, F32)
    for k in range(TOP_K):
        topw = jnp.where(lane == k, exps[k] / tot, topw)
    topw_ref[...] = topw

    onehot = [(lane == idxs[k]).astype(F32) for k in range(TOP_K)]
    chosen = onehot[0] + onehot[1] + onehot[2] + onehot[3]
    before = jnp.dot(ltri_ref[...], chosen.astype(BF16), preferred_element_type=F32) + count_ref[...]
    info = jnp.zeros(logits.shape, jnp.int32)
    for k in range(TOP_K):
        rank = jnp.sum(onehot[k] * before, axis=-1, keepdims=True).astype(jnp.int32)
        info = jnp.where(lane == k, idxs[k], info)
        info = jnp.where(lane == TOP_K + k, rank, info)
    route_ref[...] = jnp.transpose(info)[:2 * TOP_K, :]
    count_ref[...] = count_ref[...] + jnp.sum(chosen, axis=0, keepdims=True)


def _router_kernel(xp_ref, modp_ref, xs_ref, mods_ref, n2w_ref, wrh_ref, wrm_ref, br_ref, ltri_ref,
                   h2_ref, route_ref, topw_ref, count_ref, *, n_prompt_tiles):
    i = pl.program_id(0)
    rest = (n2w_ref, wrh_ref, wrm_ref, br_ref, ltri_ref, count_ref, h2_ref, route_ref, topw_ref)

    @pl.when(i == 0)
    def _():
        count_ref[...] = jnp.zeros(count_ref.shape, F32)

    @pl.when(i < n_prompt_tiles)
    def _():
        _route_tile(xp_ref[...], modp_ref[0, :, pl.ds(0, D_MODEL)], modp_ref[0, :, pl.ds(D_MODEL, D_MODEL)], *rest)

    @pl.when(i >= n_prompt_tiles)
    def _():
        _route_tile(xs_ref[...], mods_ref[:, pl.ds(0, D_MODEL)], mods_ref[:, pl.ds(D_MODEL, D_MODEL)], *rest)


def _router(xp_rows, mod_p, seq_len, xs_rows, mod_s, n2w, w_router, b_router):
    tile = ROUTER_TILE
    n_p, n_s = xp_rows.shape[0], xs_rows.shape[0]
    tp, ts = n_p // tile, n_s // tile
    n_total = n_p + n_s
    mod_p = mod_p.reshape(mod_p.shape[0], 1, mod_p.shape[1])
    wr_pad = jnp.pad(w_router, ((0, 0), (0, LANES - N_EXPERTS)))
    wr_hi = wr_pad.astype(BF16)
    wr_mid = (wr_pad - wr_hi.astype(F32)).astype(BF16)
    br_pad = jnp.pad(b_router, ((0, 0), (0, LANES - N_EXPERTS)), constant_values=-1e30)
    ltri = jnp.tril(jnp.ones((tile, tile), BF16), -1)
    clamp_p = lambda i: jnp.minimum(i, tp - 1)
    clamp_s = lambda i: jnp.maximum(i - tp, 0)
    return pl.pallas_call(
        functools.partial(_router_kernel, n_prompt_tiles=tp),
        grid=(tp + ts,),
        in_specs=[pl.BlockSpec((tile, D_MODEL), lambda i: (clamp_p(i), 0)),
                  pl.BlockSpec((1, 1, mod_p.shape[2]), lambda i: ((clamp_p(i) * tile) // seq_len, 0, 0)),
                  pl.BlockSpec((tile, D_MODEL), lambda i: (clamp_s(i), 0)),
                  pl.BlockSpec((tile, mod_s.shape[1]), lambda i: (clamp_s(i), 0)),
                  _const_spec(n2w.shape), _const_spec(wr_hi.shape), _const_spec(wr_mid.shape),
                  _const_spec(br_pad.shape), _const_spec(ltri.shape)],
        out_specs=(pl.BlockSpec((PACK_ROWS, tile, LANES), lambda i: (0, i, 0)),
                   pl.BlockSpec((2 * TOP_K, tile), lambda i: (0, i)),
                   pl.BlockSpec((tile, LANES), lambda i: (i, 0)),
                   pl.BlockSpec((1, LANES), lambda i: (0, 0))),
        out_shape=(jax.ShapeDtypeStruct((PACK_ROWS, n_total, LANES), jnp.uint32),
                   jax.ShapeDtypeStruct((2 * TOP_K, n_total), jnp.int32),
                   jax.ShapeDtypeStruct((n_total, LANES), F32),
                   jax.ShapeDtypeStruct((1, LANES), F32)),
        compiler_params=pltpu.CompilerParams(dimension_semantics=("arbitrary",), vmem_limit_bytes=VMEM_LIMIT),
        name="router",
    )(xp_rows, mod_p, xs_rows, mod_s, n2w, wr_hi, wr_mid, br_pad, ltri)


def _gather_rows(table, idx):
    n = idx.shape[0]
    steps = n // SC_GATHER_WINDOW
    assert n % SC_GATHER_WINDOW == 0 and steps % SC_WORKERS == 0
    mesh = plsc.VectorSubcoreMesh(core_axis_name="c", subcore_axis_name="s")

    @functools.partial(pl.kernel, out_type=jax.ShapeDtypeStruct((n, table.shape[1]), table.dtype), mesh=mesh,
                       scratch_types=[])
    def gather_kernel(table_hbm, idx_hbm, out_hbm):
        def body(idx_vmem, out_vmem):
            pltpu.sync_copy(table_hbm.at[idx_vmem.at[0]], out_vmem)

        pltpu.emit_pipeline(
            body,
            grid=(steps,),
            in_specs=[pl.BlockSpec((1, SC_GATHER_WINDOW), lambda i: (0, i))],
            out_specs=[pl.BlockSpec((SC_GATHER_WINDOW, table.shape[1]), lambda i: (i, 0))],
            core_axis_name=("c", "s"),
            dimension_semantics=(pltpu.PARALLEL,),
        )(idx_hbm, out_hbm)

    return gather_kernel(table, idx.reshape(1, n))


def _scatter_rows(table, idx, src_block, n_out):
    n = idx.shape[0]
    steps = n // SC_GATHER_WINDOW
    assert n % SC_GATHER_WINDOW == 0 and steps % SC_WORKERS == 0
    mesh = plsc.VectorSubcoreMesh(core_axis_name="c", subcore_axis_name="s")

    @functools.partial(pl.kernel, out_type=jax.ShapeDtypeStruct((n_out, table.shape[1]), table.dtype), mesh=mesh,
                       scratch_types=[])
    def scatter_kernel(table_hbm, idx_hbm, out_hbm):
        def body(rows_vmem, idx_vmem):
            pltpu.sync_copy(rows_vmem, out_hbm.at[idx_vmem.at[0]])

        pltpu.emit_pipeline(
            body,
            grid=(steps,),
            in_specs=[pl.BlockSpec((SC_GATHER_WINDOW, table.shape[1]), lambda g: (src_block(g), 0)),
                      pl.BlockSpec((1, SC_GATHER_WINDOW), lambda g: (0, g))],
            out_specs=[],
            core_axis_name=("c", "s"),
            dimension_semantics=(pltpu.PARALLEL,),
        )(table_hbm, idx_hbm)

    return scatter_kernel(table, idx.reshape(1, n))


def _dispatch_packed(table, slot_kt, n_slots):
    planes, n_tok, lanes = table.shape
    blocks = n_tok // SC_GATHER_WINDOW
    idx = slot_kt[None, :, :] + (jnp.arange(planes, dtype=jnp.int32) * n_slots)[:, None, None]
    src_block = lambda g: (g // (TOP_K * blocks)) * blocks + g % blocks
    out = _scatter_rows(table.reshape(planes * n_tok, lanes), idx.reshape(-1), src_block, planes * n_slots)
    return out.reshape(planes, n_slots, lanes)


def _gather_packed(table, rows):
    planes, n_table, lanes = table.shape
    idx = jnp.concatenate([rows + c * n_table for c in range(planes)])
    out = _gather_rows(table.reshape(planes * n_table, lanes), idx)
    return out.reshape(planes, rows.shape[0], lanes)


def _experts_kernel(te_ref, tr_ref, nv_ref, xs_ref, w1_ref, b1_ref, w2_ref, b2_ref, ys_ref, w1b_ref, w2b_ref):
    i = pl.program_id(0)
    prev = te_ref[jnp.maximum(i - 1, 0)]
    new_expert = (i == 0) | (te_ref[i] != prev)

    @pl.when(new_expert)
    def _():
        w1b_ref[...] = w1_ref[0].astype(BF16)
        w2b_ref[...] = w2_ref[0].astype(BF16)

    @pl.when(i < nv_ref[0])
    def _():
        row = lax.broadcasted_iota(jnp.int32, (SLOT_TILE, D_MODEL), 0)
        x = jnp.where(row < tr_ref[i], _load_packed(xs_ref, SLOT_TILE), 0.0).astype(BF16)
        hu = jnp.dot(x, w1b_ref[...], preferred_element_type=F32) + b1_ref[0]
        x_glu = jnp.minimum(hu[:, :D_FF], SWIGLU_LIMIT)
        x_lin = jnp.clip(hu[:, D_FF:], -SWIGLU_LIMIT, SWIGLU_LIMIT)
        act = x_glu * jax.nn.sigmoid(SWIGLU_ALPHA * x_glu) * (x_lin + 1.0)
        _store_packed(ys_ref, jnp.dot(act.astype(BF16), w2b_ref[...], preferred_element_type=F32) + b2_ref[0])

    @pl.when(i >= nv_ref[0])
    def _():
        ys_ref[...] = jnp.zeros(ys_ref.shape, jnp.uint32)


def _experts(tile_expert, tile_rows, n_valid, xs, w1, b1, w2, b2):
    n_slots = xs.shape[1]
    n_tiles = n_slots // SLOT_TILE
    grid_spec = pltpu.PrefetchScalarGridSpec(
        num_scalar_prefetch=3,
        grid=(n_tiles,),
        in_specs=[
            pl.BlockSpec((PACK_ROWS, SLOT_TILE, LANES), lambda i, te, tr, nv: (0, i, 0)),
            pl.BlockSpec((1, D_MODEL, 2 * D_FF), lambda i, te, tr, nv: (te[i], 0, 0)),
            pl.BlockSpec((1, 1, 2 * D_FF), lambda i, te, tr, nv: (te[i], 0, 0)),
            pl.BlockSpec((1, D_FF, D_MODEL), lambda i, te, tr, nv: (te[i], 0, 0)),
            pl.BlockSpec((1, 1, D_MODEL), lambda i, te, tr, nv: (te[i], 0, 0)),
        ],
        out_specs=pl.BlockSpec((PACK_ROWS, SLOT_TILE, LANES), lambda i, te, tr, nv: (0, i, 0)),
        scratch_shapes=[pltpu.VMEM((D_MODEL, 2 * D_FF), BF16), pltpu.VMEM((D_FF, D_MODEL), BF16)],
    )
    return pl.pallas_call(
        _experts_kernel,
        grid_spec=grid_spec,
        out_shape=jax.ShapeDtypeStruct((PACK_ROWS, n_slots, LANES), jnp.uint32),
        compiler_params=pltpu.CompilerParams(dimension_semantics=("arbitrary",), vmem_limit_bytes=VMEM_LIMIT),
        name="experts",
    )(tile_expert, tile_rows, n_valid, xs, w1, b1.reshape(N_EXPERTS, 1, 2 * D_FF), w2,
      b2.reshape(N_EXPERTS, 1, D_MODEL))


def _combine_kernel(y4_ref, x1_ref, topw_ref, g2_ref, fw_ref, o_ref, *, per_row_mod):
    w = topw_ref[...]
    n = w.shape[0]
    ff = None
    for k in range(TOP_K):
        yk = w[:, k:k + 1] * _load_packed(y4_ref.at[:, k], n)
        ff = yk if ff is None else ff + yk
    g2 = g2_ref[...] if per_row_mod else g2_ref[0]
    x = x1_ref[...] + g2 * ff
    ms = jnp.mean(x * x, axis=-1, keepdims=True)
    o_ref[...] = x * lax.rsqrt(ms + NORM_EPS) * fw_ref[...]


def _combine(y4, x1_rows, topw, g2, fw, row_offset, rows_per_mod):
    n = x1_rows.shape[0]
    tile = COMBINE_TILE
    off = row_offset // tile
    per_row = rows_per_mod == 1
    if per_row:
        g2_spec = pl.BlockSpec((tile, D_MODEL), lambda i: (i, 0))
    else:
        g2 = g2.reshape(g2.shape[0], 1, D_MODEL)
        g2_spec = pl.BlockSpec((1, 1, D_MODEL), lambda i: ((i * tile) // rows_per_mod, 0, 0))
    return pl.pallas_call(
        functools.partial(_combine_kernel, per_row_mod=per_row),
        grid=(n // tile,),
        in_specs=[pl.BlockSpec((PACK_ROWS, TOP_K, tile, LANES), lambda i: (0, 0, i + off, 0)),
                  pl.BlockSpec((tile, D_MODEL), lambda i: (i, 0)),
                  pl.BlockSpec((tile, LANES), lambda i: (i + off, 0)),
                  g2_spec, _const_spec(fw.shape)],
        out_specs=pl.BlockSpec((tile, D_MODEL), lambda i: (i, 0)),
        out_shape=jax.ShapeDtypeStruct((n, D_MODEL), F32),
        compiler_params=pltpu.CompilerParams(dimension_semantics=("arbitrary",), vmem_limit_bytes=VMEM_LIMIT),
        name="combine",
    )(y4, x1_rows, topw, g2, fw)


def _routing_tables(route, counts, n_slots):
    padded = ((counts + SLOT_TILE - 1) // SLOT_TILE) * SLOT_TILE
    pend = jnp.cumsum(padded)
    poff = pend - padded
    expert_kt, rank_kt = route[:TOP_K], route[TOP_K:]
    experts = jnp.arange(N_EXPERTS, dtype=jnp.int32)
    start_kt = jnp.sum((expert_kt[None] == experts[:, None, None]).astype(jnp.int32) * poff[:, None, None], axis=0)
    slot_kt = start_kt + rank_kt
    n_tiles = n_slots // SLOT_TILE
    n_valid = (pend[-1] // SLOT_TILE).astype(jnp.int32)
    tile_row = jnp.minimum(jnp.arange(n_tiles, dtype=jnp.int32), n_valid - 1) * SLOT_TILE
    in_later = (pend[None, :] <= tile_row[:, None]).astype(jnp.int32)
    tile_e = jnp.sum(in_later, axis=1).astype(jnp.int32)
    is_e = (experts[None, :] == tile_e[:, None]).astype(jnp.int32)
    used_end = jnp.sum(is_e * (poff + counts)[None, :], axis=1)
    tile_rows = jnp.clip(used_end - tile_row, 0, SLOT_TILE).astype(jnp.int32)
    return slot_kt, tile_e, tile_rows, n_valid.reshape(1)


def _round_up(n, m):
    return ((n + m - 1) // m) * m


def kernel(x_prompt, x_sample, c_prompt, c_sample, state_ret, state_s5_re, state_s5_im, norm1_w, norm2_w, w_ada, b_ada, w_in, ret_norm_w, s5_lam_re, s5_lam_im, s5_log_dt, s5_b_re, s5_b_im, s5_c_re, s5_c_im, s5_d, w_glu, b_glu, s5_norm_w, w_out, w_router, b_router, w1, b1, w2, b2, final_w):
    bp, lp, _ = x_prompt.shape
    bs, ls, _ = x_sample.shape
    assert norm1_w.shape[0] == 1, "single-layer model"
    n_p, n_s = bp * lp, bs * ls
    n_tok = n_p + n_s

    mod = _ada(jnp.concatenate([c_prompt, c_sample], axis=0), w_ada[0], b_ada[0])
    mod_p, mod_s = mod[:bp], jnp.repeat(mod[bp:], ls, axis=0)
    d3 = 3 * D_MODEL

    lbr, lbi, bbr, bbi = _s5prep(s5_lam_re[0], s5_lam_im[0], s5_log_dt[0], s5_b_re[0], s5_b_im[0])
    bmat = jnp.concatenate([_block_diag(bbr), _block_diag(bbi)], axis=-1).astype(BF16)
    cre = _block_diag(jnp.transpose(s5_c_re[0], (0, 2, 1))).astype(BF16)
    cim = _block_diag(jnp.transpose(-s5_c_im[0], (0, 2, 1))).astype(BF16)
    wts = dict(
        n1w=norm1_w, w_in=w_in[0].astype(BF16), rnw=ret_norm_w, bmat=bmat, cre=cre, cim=cim,
        lbr=lbr.reshape(1, SSM_CH), lbi=lbi.reshape(1, SSM_CH), dsk=s5_d[0].reshape(1, SSM_WIDTH),
        w_glu=w_glu[0].astype(BF16), b_glu=b_glu, snw=s5_norm_w, w_out=w_out[0].astype(BF16),
    )

    zero_states = (jnp.zeros((bp, RET_HEADS, HEAD_DIM, HEAD_DIM), F32), jnp.zeros((bp, SSM_CH), F32),
                   jnp.zeros((bp, SSM_CH), F32))
    x1_p, ret_p, re_p, im_p = _mixer(x_prompt, mod_p[:, :d3], jnp.arange(lp, dtype=F32), zero_states, wts,
                                     prompt=True)
    sample_states = (state_ret[0], state_s5_re[0].reshape(bs, SSM_CH), state_s5_im[0].reshape(bs, SSM_CH))
    x1_s, ret_s, re_s, im_s = _mixer(x_sample.reshape(n_s, D_MODEL), mod_s[:, :d3],
                                     PAST_LEN + jnp.arange(ls, dtype=F32), sample_states, wts, prompt=False)

    x1_p_rows = x1_p.reshape(n_p, D_MODEL)
    h2, route, topw, counts = _router(x1_p_rows, mod_p[:, d3:d3 + 2 * D_MODEL], lp, x1_s,
                                      mod_s[:, d3:d3 + 2 * D_MODEL], norm2_w, w_router[0], b_router)

    n_assign = n_tok * TOP_K
    gather_quantum = SC_GATHER_WINDOW * SC_WORKERS // PACK_ROWS
    assert n_assign % gather_quantum == 0
    n_slots = _round_up(_round_up(n_assign, SLOT_TILE) + N_EXPERTS * SLOT_TILE, gather_quantum)
    slot_kt, tile_e, tile_rows, n_valid = _routing_tables(route, counts[0, :N_EXPERTS].astype(jnp.int32), n_slots)
    xs = _dispatch_packed(h2, slot_kt, n_slots)
    ys = _experts(tile_e, tile_rows, n_valid, xs, w1[0], b1[0], w2[0], b2[0])
    y4 = _gather_packed(ys, slot_kt.reshape(-1))
    y4 = y4.reshape(PACK_ROWS, TOP_K, n_tok, LANES)

    g2_p, g2_s = mod_p[:, 5 * D_MODEL:], mod_s[:, 5 * D_MODEL:]
    fw = final_w.reshape(1, D_MODEL)
    y_p = _combine(y4, x1_p_rows, topw, g2_p, fw, 0, lp)
    y_s = _combine(y4, x1_s, topw, g2_s, fw, n_p, 1)

    g, p = SSM_GROUPS, SSM_STATE
    return (y_p.reshape(bp, lp, D_MODEL), y_s.reshape(bs, ls, D_MODEL),
            ret_p[None], re_p.reshape(1, bp, g, p), im_p.reshape(1, bp, g, p),
            ret_s[None], re_s.reshape(1, bs, g, p), im_s.reshape(1, bs, g, p))
```

```python
import functools
import math

import jax
import jax.numpy as jnp
from jax import lax
from jax.experimental import pallas as pl
from jax.experimental.pallas import tpu as pltpu
from jax.experimental.pallas import tpu_sc as plsc

F32 = jnp.float32
BF16 = jnp.bfloat16
HIGHEST = lax.Precision.HIGHEST

D_MODEL = 1024
PAST_LEN = 16384
RET_WIDTH = 512
RET_HEADS = 4
HEAD_DIM = 128
ROPE_BASE = 10000.0
SSM_WIDTH = 512
SSM_GROUP = 16
SSM_GROUPS = 32
SSM_STATE = 64
SSM_CH = SSM_GROUPS * SSM_STATE
IN_WIDTH = 4 * RET_WIDTH + SSM_WIDTH
N_EXPERTS = 32
TOP_K = 4
D_FF = 1024
SWIGLU_LIMIT = 7.0
SWIGLU_ALPHA = 1.702
NORM_EPS = 1e-6

LANES = 128
SUBLANES = 8
VMEM_LIMIT = 56 * 1024 * 1024

SEQ_PER_BLOCK = 8
PROMPT_CHUNK = 64
S5_BLOCK_GROUPS = 8
N_S5_BLOCKS = SSM_GROUPS // S5_BLOCK_GROUPS
S5_BLOCK_IN = S5_BLOCK_GROUPS * SSM_GROUP
S5_BLOCK_CH = S5_BLOCK_GROUPS * SSM_STATE
ROUTER_TILE = 512
SLOT_TILE = 256
COMBINE_TILE = 256
SC_GATHER_WINDOW = 128
SC_WORKERS = 32


def _silu(x):
    return x * jax.nn.sigmoid(x)


def _ada_kernel(c_ref, w_ref, b_ref, o_ref):
    s = _silu(c_ref[...])
    o_ref[...] = jnp.dot(s, w_ref[...], precision=HIGHEST, preferred_element_type=F32) + b_ref[...]


def _ada(c_all, w_ada, b_ada):
    n_rows, n_out = c_all.shape[0], w_ada.shape[1]
    tn = 1536
    return pl.pallas_call(
        _ada_kernel,
        grid=(n_out // tn,),
        in_specs=[
            pl.BlockSpec((n_rows, D_MODEL), lambda j: (0, 0)),
            pl.BlockSpec((D_MODEL, tn), lambda j: (0, j)),
            pl.BlockSpec((1, tn), lambda j: (0, j)),
        ],
        out_specs=pl.BlockSpec((n_rows, tn), lambda j: (0, j)),
        out_shape=jax.ShapeDtypeStruct((n_rows, n_out), F32),
        compiler_params=pltpu.CompilerParams(dimension_semantics=("arbitrary",), vmem_limit_bytes=VMEM_LIMIT),
        name="ada",
    )(c_all, w_ada, b_ada.reshape(1, n_out))


def _s5prep_kernel(lre_ref, lim_ref, ldt_ref, bre_ref, bim_ref, lbr_ref, lbi_ref, bbr_ref, bbi_ref):
    lam_re, lam_im = lre_ref[...], lim_ref[...]
    dt = jnp.exp(ldt_ref[...])
    mag = jnp.exp(lam_re * dt)
    ang = lam_im * dt
    lb_re, lb_im = mag * jnp.cos(ang), mag * jnp.sin(ang)
    den = lam_re * lam_re + lam_im * lam_im
    f_re = ((lb_re - 1.0) * lam_re + lb_im * lam_im) / den
    f_im = (lb_im * lam_re - (lb_re - 1.0) * lam_im) / den
    lbr_ref[...] = lb_re
    lbi_ref[...] = lb_im
    b_re, b_im = bre_ref[...], bim_ref[...]
    bbr_ref[...] = f_re[:, None, :] * b_re - f_im[:, None, :] * b_im
    bbi_ref[...] = f_re[:, None, :] * b_im + f_im[:, None, :] * b_re


def _s5prep(lam_re, lam_im, log_dt, b_re, b_im):
    g, p = lam_re.shape
    bt_re = jnp.transpose(b_re, (0, 2, 1))
    bt_im = jnp.transpose(b_im, (0, 2, 1))
    return pl.pallas_call(
        _s5prep_kernel,
        out_shape=(
            jax.ShapeDtypeStruct((g, p), F32), jax.ShapeDtypeStruct((g, p), F32),
            jax.ShapeDtypeStruct((g, SSM_GROUP, p), F32), jax.ShapeDtypeStruct((g, SSM_GROUP, p), F32),
        ),
        name="s5prep",
    )(lam_re, lam_im, log_dt.reshape(g, 1), bt_re, bt_im)


def _block_diag(blocks):
    _, r, c = blocks.shape
    b4 = blocks.reshape(N_S5_BLOCKS, S5_BLOCK_GROUPS, r, c)
    eye = jnp.eye(S5_BLOCK_GROUPS, dtype=blocks.dtype)
    out = b4[:, :, :, None, :] * eye[None, :, None, :, None]
    return out.reshape(N_S5_BLOCKS, S5_BLOCK_GROUPS * r, S5_BLOCK_GROUPS * c)


def _mixer_kernel(x_ref, mod_ref, n1w_ref, win_ref, cos_ref, sin_ref, dmask_ref, cdec_ref, sdec_ref,
                  rnw_ref, bmat_ref, cre_ref, cim_ref, lbr_ref, lbi_ref, dsk_ref, wglu_ref, bglu_ref,
                  snw_ref, wout_ref, sret0_ref, sre0_ref, sim0_ref,
                  x1_ref, sret_ref, sre_ref, sim_ref,
                  hb_ref, z_ref, zu_ref, oy_ref, utb_ref, bur_ref, bui_ref, ytb_ref, yb_ref,
                  *, n_seq, chunk, tile_rows, carry, chunk_decay):
    rows = n_seq * chunk
    seq_per_tile = tile_rows // chunk
    n_tiles = rows // tile_rows
    per_row_mod = mod_ref.shape[0] == rows

    def load_states():
        sret_ref[...] = sret0_ref[...]
        sre_ref[...] = sre0_ref[...]
        sim_ref[...] = sim0_ref[...]

    if carry:
        pl.when(pl.program_id(0) == 0)(load_states)
    else:
        load_states()

    n1w = n1w_ref[...]
    mod_rows = rows if per_row_mod else chunk
    for i in range(rows // mod_rows):
        r0 = i * mod_rows
        xb = _load_rows(x_ref, r0, mod_rows, chunk)
        if per_row_mod:
            sh = mod_ref[pl.ds(r0, mod_rows), pl.ds(0, D_MODEL)]
            sc = mod_ref[pl.ds(r0, mod_rows), pl.ds(D_MODEL, D_MODEL)]
        else:
            sh = mod_ref[pl.ds(i, 1), pl.ds(0, D_MODEL)]
            sc = mod_ref[pl.ds(i, 1), pl.ds(D_MODEL, D_MODEL)]
        ms = jnp.mean(xb * xb, axis=-1, keepdims=True)
        hn = xb * lax.rsqrt(ms + NORM_EPS) * n1w
        hb_ref[pl.ds(r0, mod_rows), :] = (hn * (1.0 + sc) + sh).astype(BF16)
    ret_w = 4 * RET_WIDTH
    z_ref[...] = jnp.dot(hb_ref[...], win_ref[:, pl.ds(0, ret_w)], preferred_element_type=F32)
    zu = jnp.dot(hb_ref[...], win_ref[:, pl.ds(ret_w, SSM_WIDTH)], preferred_element_type=F32)
    pitch = zu_ref.shape[1] // n_seq
    for c in range(SSM_WIDTH // LANES):
        for b in range(n_seq if pitch != chunk else 1):
            nb = chunk if pitch != chunk else rows
            zu_ref[c, pl.ds(b * pitch, nb), :] = zu[b * chunk:b * chunk + nb, c * LANES:(c + 1) * LANES]

    cos = cos_ref[...]
    sin = sin_ref[...]
    scale = HEAD_DIM ** -0.5
    if seq_per_tile > 1:
        row_id = lax.broadcasted_iota(jnp.int32, (tile_rows, HEAD_DIM), 0)

    def rope(t):
        return t * cos + pltpu.roll(t, HEAD_DIM // 2, 1) * sin

    def ret_tile(ti, c):
        r0 = pl.multiple_of(ti * tile_rows, tile_rows)
        for h in range(RET_HEADS):
            c0 = h * HEAD_DIM
            q = rope(z_ref[pl.ds(r0, tile_rows), pl.ds(c0, HEAD_DIM)])
            k = rope(z_ref[pl.ds(r0, tile_rows), pl.ds(RET_WIDTH + c0, HEAD_DIM)]) * scale
            v = z_ref[pl.ds(r0, tile_rows), pl.ds(2 * RET_WIDTH + c0, HEAD_DIM)]
            g = z_ref[pl.ds(r0, tile_rows), pl.ds(3 * RET_WIDTH + c0, HEAD_DIM)]
            kd = k * sdec_ref[h]
            if tile_rows < HEAD_DIM:
                pad = jnp.zeros((HEAD_DIM - tile_rows, HEAD_DIM), F32)
                k, v, kd = (jnp.concatenate([t, pad], axis=0) for t in (k, v, kd))
                if seq_per_tile > 1:
                    row_kv = lax.broadcasted_iota(jnp.int32, (HEAD_DIM, HEAD_DIM), 0)
            elif seq_per_tile > 1:
                row_kv = row_id
            qb, kb, vb = q.astype(BF16), k.astype(BF16), v.astype(BF16)
            s = lax.dot_general(qb, kb, (((1,), (1,)), ((), ())), preferred_element_type=F32) * dmask_ref[h]
            o = jnp.dot(s.astype(BF16), vb, preferred_element_type=F32)
            cross = None
            for si in range(seq_per_tile):
                sidx = ti * seq_per_tile + si
                st = sret_ref[sidx, h]
                cr = jnp.dot(qb, st.astype(BF16), preferred_element_type=F32)
                if seq_per_tile > 1:
                    in_seq = (row_id >= si * chunk) & (row_id < (si + 1) * chunk)
                    cross = jnp.where(in_seq, cr, 0.0 if cross is None else cross)
                    kds = jnp.where((row_kv >= si * chunk) & (row_kv < (si + 1) * chunk), kd, 0.0)
                else:
                    cross, kds = cr, kd
                upd = lax.dot_general(kds.astype(BF16), vb, (((0,), (0,)), ((), ())), preferred_element_type=F32)
                sret_ref[sidx, h] = st * chunk_decay[h] + upd
            o = o + cross * cdec_ref[h]
            o = o * lax.rsqrt(jnp.mean(o * o, axis=-1, keepdims=True) + NORM_EPS)
            o = o * rnw_ref[:, pl.ds(c0, HEAD_DIM)] * _silu(g)
            oy_ref[pl.ds(r0, tile_rows), pl.ds(c0, HEAD_DIM)] = o
        return c

    lax.fori_loop(0, n_tiles, ret_tile, 0, unroll=True)

    for t in range(chunk):
        for c in range(SSM_WIDTH // LANES):
            utb_ref[pl.ds(t * n_seq, n_seq), pl.ds(c * LANES, LANES)] = zu_ref[c, pl.ds(t, n_seq, stride=pitch), :]
    for blk in range(N_S5_BLOCKS):
        ub = utb_ref[:, pl.ds(blk * S5_BLOCK_IN, S5_BLOCK_IN)].astype(BF16)
        bu = jnp.dot(ub, bmat_ref[blk], preferred_element_type=F32)
        bur_ref[:, pl.ds(blk * S5_BLOCK_CH, S5_BLOCK_CH)] = bu[:, :S5_BLOCK_CH]
        bui_ref[:, pl.ds(blk * S5_BLOCK_CH, S5_BLOCK_CH)] = bu[:, S5_BLOCK_CH:]

    scan_w = 2 * S5_BLOCK_CH
    for p in range(SSM_CH // scan_w):
        cols = pl.ds(p * scan_w, scan_w)
        lbr = jnp.broadcast_to(lbr_ref[:, cols], (n_seq, scan_w))
        lbi = jnp.broadcast_to(lbi_ref[:, cols], (n_seq, scan_w))

        def scan_step(t, hc, cols=cols, lbr=lbr, lbi=lbi):
            hr, hi = hc
            r0 = pl.multiple_of(t * n_seq, n_seq)
            nr = lbr * hr - lbi * hi + bur_ref[pl.ds(r0, n_seq), cols]
            ni = lbr * hi + lbi * hr + bui_ref[pl.ds(r0, n_seq), cols]
            bur_ref[pl.ds(r0, n_seq), cols] = nr
            bui_ref[pl.ds(r0, n_seq), cols] = ni
            return nr, ni

        h0 = (sre_ref[:, cols], sim_ref[:, cols])
        if chunk <= 8:
            hc = h0
            for t in range(chunk):
                hc = scan_step(t, hc)
        else:
            unroll = 4

            def scan_group(tg, hc):
                for j in range(unroll):
                    hc = scan_step(tg * unroll + j, hc)
                return hc

            hc = lax.fori_loop(0, chunk // unroll, scan_group, h0)
        sre_ref[:, cols] = hc[0]
        sim_ref[:, cols] = hc[1]

    for blk in range(N_S5_BLOCKS):
        cols = pl.ds(blk * S5_BLOCK_CH, S5_BLOCK_CH)
        yb = jnp.dot(bur_ref[:, cols].astype(BF16), cre_ref[blk], preferred_element_type=F32)
        yb = yb + jnp.dot(bui_ref[:, cols].astype(BF16), cim_ref[blk], preferred_element_type=F32)
        ucols = pl.ds(blk * S5_BLOCK_IN, S5_BLOCK_IN)
        ytb_ref[:, ucols] = yb + dsk_ref[:, ucols] * utb_ref[:, ucols]
    for t in range(chunk):
        for c in range(SSM_WIDTH // LANES):
            yb_ref[c, pl.ds(t, n_seq, stride=pitch), :] = ytb_ref[pl.ds(t * n_seq, n_seq), pl.ds(c * LANES, LANES)]

    def seq_major(c):
        if pitch == chunk:
            return yb_ref[c]
        return jnp.concatenate([yb_ref[c, pl.ds(b * pitch, chunk), :] for b in range(n_seq)], axis=0)

    y = jnp.concatenate([seq_major(c) for c in range(SSM_WIDTH // LANES)], axis=1)
    y = jax.nn.gelu(y, approximate=True)
    gate = jnp.dot(y.astype(BF16), wglu_ref[...], preferred_element_type=F32) + bglu_ref[...]
    y = y * jax.nn.sigmoid(gate)
    y = y * lax.rsqrt(jnp.mean(y * y, axis=-1, keepdims=True) + NORM_EPS) * snw_ref[...]
    oy_ref[:, pl.ds(RET_WIDTH, SSM_WIDTH)] = y

    mix = jnp.dot(oy_ref[...].astype(BF16), wout_ref[...], preferred_element_type=F32)
    for i in range(rows // mod_rows):
        r0 = i * mod_rows
        if per_row_mod:
            g1 = mod_ref[pl.ds(r0, mod_rows), pl.ds(2 * D_MODEL, D_MODEL)]
        else:
            g1 = mod_ref[pl.ds(i, 1), pl.ds(2 * D_MODEL, D_MODEL)]
        _store_rows(x1_ref, r0, mod_rows, chunk,
                    _load_rows(x_ref, r0, mod_rows, chunk) + g1 * mix[r0:r0 + mod_rows])


def _load_rows(ref, r0, n, chunk):
    if len(ref.shape) == 2:
        return ref[pl.ds(r0, n), :]
    assert n == chunk and r0 % chunk == 0
    return ref[r0 // chunk]


def _store_rows(ref, r0, n, chunk, val):
    if len(ref.shape) == 2:
        ref[pl.ds(r0, n), :] = val
    else:
        assert n == chunk and r0 % chunk == 0
        ref[r0 // chunk] = val


def _seq_pitch(chunk):
    return chunk + SUBLANES if chunk % SUBLANES == 0 else chunk


def _const_spec(shape):
    nd = len(shape)
    return pl.BlockSpec(shape, lambda j, _n=nd: (0,) * _n)


def _decay_tables(chunk, tile_rows):
    log_gamma = jnp.log1p(-jnp.exp2(-5.0 - jnp.arange(RET_HEADS, dtype=F32)))
    r = jnp.arange(tile_rows)
    seq, loc = r // chunk, (r % chunk).astype(F32)
    rel = loc[:, None] - loc[None, :]
    ok = (seq[:, None] == seq[None, :]) & (rel >= 0)
    dmask = jnp.where(ok[None], jnp.exp(jnp.where(ok, rel, 0.0)[None] * log_gamma[:, None, None]), 0.0)
    if tile_rows < HEAD_DIM:
        dmask = jnp.pad(dmask, ((0, 0), (0, 0), (0, HEAD_DIM - tile_rows)))
    cdec = jnp.exp((loc[None, :] + 1.0) * log_gamma[:, None])
    sdec = jnp.exp((chunk - 1.0 - loc)[None, :] * log_gamma[:, None])
    bcast = lambda t: jnp.broadcast_to(t[:, :, None], (RET_HEADS, tile_rows, HEAD_DIM))
    return dmask, bcast(cdec), bcast(sdec), log_gamma


def _rope_tables(pos):
    half = HEAD_DIM // 2
    inv_freq = ROPE_BASE ** (-jnp.arange(half, dtype=F32) / half)
    ang = pos[:, None] * inv_freq[None, :]
    cos, sin = jnp.cos(ang), jnp.sin(ang)
    return jnp.concatenate([cos, cos], axis=-1), jnp.concatenate([-sin, sin], axis=-1)


def _mixer(x, mod, pos, states, wts, *, prompt):
    n_seq = SEQ_PER_BLOCK
    if prompt:
        n_total, seq_len, _ = x.shape
        assert n_total == n_seq
        chunk, tile_rows, n_steps = PROMPT_CHUNK, PROMPT_CHUNK, seq_len // PROMPT_CHUNK
        x_spec = pl.BlockSpec((n_seq, chunk, D_MODEL), lambda j: (0, j, 0))
        mod_spec = _const_spec(mod.shape)
        tab_spec = pl.BlockSpec((chunk, HEAD_DIM), lambda j: (j, 0))
        seq_map = lambda j: 0
    else:
        chunk = pos.shape[0]
        tile_rows = SUBLANES
        n_total = x.shape[0] // chunk
        n_steps = n_total // n_seq
        x_spec = pl.BlockSpec((n_seq * chunk, D_MODEL), lambda j: (j, 0))
        mod_spec = pl.BlockSpec((n_seq * chunk, mod.shape[1]), lambda j: (j, 0))
        tab_spec = _const_spec((tile_rows, HEAD_DIM))
        seq_map = lambda j: j
    rows = n_seq * chunk
    cos, sin = _rope_tables(pos)
    if not prompt:
        reps = tile_rows // chunk
        cos, sin = jnp.tile(cos, (reps, 1)), jnp.tile(sin, (reps, 1))
    dmask, cdec, sdec, log_gamma = _decay_tables(chunk, tile_rows)
    chunk_decay = tuple(float(math.exp(chunk * math.log1p(-2.0 ** (-5.0 - h)))) for h in range(RET_HEADS))
    del log_gamma
    sret0, sre0, sim0 = states

    st_ret_spec = pl.BlockSpec((n_seq, RET_HEADS, HEAD_DIM, HEAD_DIM), lambda j: (seq_map(j), 0, 0, 0))
    st_s5_spec = pl.BlockSpec((n_seq, SSM_CH), lambda j: (seq_map(j), 0))
    consts = [dmask, cdec, sdec, wts["rnw"], wts["bmat"], wts["cre"], wts["cim"], wts["lbr"], wts["lbi"],
              wts["dsk"], wts["w_glu"], wts["b_glu"], wts["snw"], wts["w_out"]]
    args = [x, mod, wts["n1w"], wts["w_in"], cos, sin] + consts + [sret0, sre0, sim0]
    in_specs = ([x_spec, mod_spec, _const_spec(wts["n1w"].shape), _const_spec(wts["w_in"].shape), tab_spec, tab_spec]
                + [_const_spec(a.shape) for a in consts] + [st_ret_spec, st_s5_spec, st_s5_spec])

    kern = functools.partial(_mixer_kernel, n_seq=n_seq, chunk=chunk, tile_rows=tile_rows, carry=prompt,
                             chunk_decay=chunk_decay)
    out_shape = (
        jax.ShapeDtypeStruct(x.shape, F32),
        jax.ShapeDtypeStruct((n_total, RET_HEADS, HEAD_DIM, HEAD_DIM), F32),
        jax.ShapeDtypeStruct((n_total, SSM_CH), F32),
        jax.ShapeDtypeStruct((n_total, SSM_CH), F32),
    )
    scratch = [
        pltpu.VMEM((rows, D_MODEL), BF16),
        pltpu.VMEM((rows, 4 * RET_WIDTH), F32),
        pltpu.VMEM((SSM_WIDTH // LANES, n_seq * _seq_pitch(chunk), LANES), F32),
        pltpu.VMEM((rows, D_MODEL), F32),
        pltpu.VMEM((rows, SSM_WIDTH), F32),
        pltpu.VMEM((rows, SSM_CH), F32),
        pltpu.VMEM((rows, SSM_CH), F32),
        pltpu.VMEM((rows, SSM_WIDTH), F32),
        pltpu.VMEM((SSM_WIDTH // LANES, n_seq * _seq_pitch(chunk), LANES), F32),
    ]
    return pl.pallas_call(
        kern,
        grid=(n_steps,),
        in_specs=in_specs,
        out_specs=(x_spec, st_ret_spec, st_s5_spec, st_s5_spec),
        out_shape=out_shape,
        scratch_shapes=scratch,
        compiler_params=pltpu.CompilerParams(dimension_semantics=("arbitrary",), vmem_limit_bytes=VMEM_LIMIT),
        name="mixer_prompt" if prompt else "mixer_sample",
    )(*args)


PACK_ROWS = D_MODEL // (2 * LANES)


def _store_packed(ref, x):
    half = D_MODEL // 2
    bits = lax.bitcast_convert_type(x.astype(BF16).astype(F32), jnp.uint32)
    words = bits[:, :half] | (bits[:, half:] >> 16)
    for c in range(PACK_ROWS):
        ref[c] = words[:, c * LANES:(c + 1) * LANES]


def _load_packed(ref, n, first_row=0, row_stride=1):
    hi, lo = [], []
    for c in range(PACK_ROWS):
        w = ref[c] if row_stride == 1 else ref[c, pl.ds(first_row, n, stride=row_stride), :]
        hi.append(lax.bitcast_convert_type(w & jnp.uint32(0xFFFF0000), F32))
        lo.append(lax.bitcast_convert_type(w << 16, F32))
    return jnp.concatenate(hi + lo, axis=1)


def _split_bf16(x):
    hi = x.astype(BF16)
    return hi, (x - hi.astype(F32)).astype(BF16)


def _route_tile(x, sh, sc, n2w_ref, wrh_ref, wrm_ref, br_ref, ltri_ref, count_ref, h2_ref, route_ref, topw_ref):
    ms = jnp.mean(x * x, axis=-1, keepdims=True)
    h2 = x * lax.rsqrt(ms + NORM_EPS) * n2w_ref[...] * (1.0 + sc) + sh
    _store_packed(h2_ref, h2)
    hh, hm = _split_bf16(h2)
    logits = (jnp.dot(hh, wrh_ref[...], preferred_element_type=F32)
              + (jnp.dot(hh, wrm_ref[...], preferred_element_type=F32)
                 + jnp.dot(hm, wrh_ref[...], preferred_element_type=F32))) + br_ref[...]
    lane = lax.broadcasted_iota(jnp.int32, logits.shape, 1)
    work = logits
    vals, idxs = [], []
    for _ in range(TOP_K):
        m = jnp.max(work, axis=-1, keepdims=True)
        idx = jnp.min(jnp.where(work == m, lane, LANES), axis=-1, keepdims=True)
        vals.append(m)
        idxs.append(idx)
        work = jnp.where(lane == idx, -jnp.inf, work)
    exps = [jnp.exp(v - vals[0]) for v in vals]
    tot = exps[0] + exps[1] + exps[2] + exps[3]
    topw = jnp.zeros(logits.shape, F32)
    for k in range(TOP_K):
        topw = jnp.where(lane == k, exps[k] / tot, topw)
    topw_ref[...] = topw

    onehot = [(lane == idxs[k]).astype(F32) for k in range(TOP_K)]
    chosen = onehot[0] + onehot[1] + onehot[2] + onehot[3]
    before = jnp.dot(ltri_ref[...], chosen.astype(BF16), preferred_element_type=F32) + count_ref[...]
    info = jnp.zeros(logits.shape, jnp.int32)
    for k in range(TOP_K):
        rank = jnp.sum(onehot[k] * before, axis=-1, keepdims=True).astype(jnp.int32)
        info = jnp.where(lane == k, idxs[k], info)
        info = jnp.where(lane == TOP_K + k, rank, info)
    route_ref[...] = jnp.transpose(info)[:2 * TOP_K, :]
    count_ref[...] = count_ref[...] + jnp.sum(chosen, axis=0, keepdims=True)


def _router_kernel(xp_ref, modp_ref, xs_ref, mods_ref, n2w_ref, wrh_ref, wrm_ref, br_ref, ltri_ref,
                   h2_ref, route_ref, topw_ref, count_ref, *, n_prompt_tiles):
    i = pl.program_id(0)
    rest = (n2w_ref, wrh_ref, wrm_ref, br_ref, ltri_ref, count_ref, h2_ref, route_ref, topw_ref)

    @pl.when(i == 0)
    def _():
        count_ref[...] = jnp.zeros(count_ref.shape, F32)

    @pl.when(i < n_prompt_tiles)
    def _():
        _route_tile(xp_ref[...], modp_ref[0, :, pl.ds(0, D_MODEL)], modp_ref[0, :, pl.ds(D_MODEL, D_MODEL)], *rest)

    @pl.when(i >= n_prompt_tiles)
    def _():
        _route_tile(xs_ref[...], mods_ref[:, pl.ds(0, D_MODEL)], mods_ref[:, pl.ds(D_MODEL, D_MODEL)], *rest)


def _router(xp_rows, mod_p, seq_len, xs_rows, mod_s, n2w, w_router, b_router):
    tile = ROUTER_TILE
    n_p, n_s = xp_rows.shape[0], xs_rows.shape[0]
    tp, ts = n_p // tile, n_s // tile
    n_total = n_p + n_s
    mod_p = mod_p.reshape(mod_p.shape[0], 1, mod_p.shape[1])
    wr_pad = jnp.pad(w_router, ((0, 0), (0, LANES - N_EXPERTS)))
    wr_hi = wr_pad.astype(BF16)
    wr_mid = (wr_pad - wr_hi.astype(F32)).astype(BF16)
    br_pad = jnp.pad(b_router, ((0, 0), (0, LANES - N_EXPERTS)), constant_values=-1e30)
    ltri = jnp.tril(jnp.ones((tile, tile), BF16), -1)
    clamp_p = lambda i: jnp.minimum(i, tp - 1)
    clamp_s = lambda i: jnp.maximum(i - tp, 0)
    return pl.pallas_call(
        functools.partial(_router_kernel, n_prompt_tiles=tp),
        grid=(tp + ts,),
        in_specs=[pl.BlockSpec((tile, D_MODEL), lambda i: (clamp_p(i), 0)),
                  pl.BlockSpec((1, 1, mod_p.shape[2]), lambda i: ((clamp_p(i) * tile) // seq_len, 0, 0)),
                  pl.BlockSpec((tile, D_MODEL), lambda i: (clamp_s(i), 0)),
                  pl.BlockSpec((tile, mod_s.shape[1]), lambda i: (clamp_s(i), 0)),
                  _const_spec(n2w.shape), _const_spec(wr_hi.shape), _const_spec(wr_mid.shape),
                  _const_spec(br_pad.shape), _const_spec(ltri.shape)],
        out_specs=(pl.BlockSpec((PACK_ROWS, tile, LANES), lambda i: (0, i, 0)),
                   pl.BlockSpec((2 * TOP_K, tile), lambda i: (0, i)),
                   pl.BlockSpec((tile, LANES), lambda i: (i, 0)),
                   pl.BlockSpec((1, LANES), lambda i: (0, 0))),
        out_shape=(jax.ShapeDtypeStruct((PACK_ROWS, n_total, LANES), jnp.uint32),
                   jax.ShapeDtypeStruct((2 * TOP_K, n_total), jnp.int32),
                   jax.ShapeDtypeStruct((n_total, LANES), F32),
                   jax.ShapeDtypeStruct((1, LANES), F32)),
        compiler_params=pltpu.CompilerParams(dimension_semantics=("arbitrary",), vmem_limit_bytes=VMEM_LIMIT),
        name="router",
    )(xp_rows, mod_p, xs_rows, mod_s, n2w, wr_hi, wr_mid, br_pad, ltri)


def _gather_rows(table, idx):
    n = idx.shape[0]
    steps = n // SC_GATHER_WINDOW
    assert n % SC_GATHER_WINDOW == 0 and steps % SC_WORKERS == 0
    mesh = plsc.VectorSubcoreMesh(core_axis_name="c", subcore_axis_name="s")

    @functools.partial(pl.kernel, out_type=jax.ShapeDtypeStruct((n, table.shape[1]), table.dtype), mesh=mesh,
                       scratch_types=[])
    def gather_kernel(table_hbm, idx_hbm, out_hbm):
        def body(idx_vmem, out_vmem):
            pltpu.sync_copy(table_hbm.at[idx_vmem.at[0]], out_vmem)

        pltpu.emit_pipeline(
            body,
            grid=(steps,),
            in_specs=[pl.BlockSpec((1, SC_GATHER_WINDOW), lambda i: (0, i))],
            out_specs=[pl.BlockSpec((SC_GATHER_WINDOW, table.shape[1]), lambda i: (i, 0))],
            core_axis_name=("c", "s"),
            dimension_semantics=(pltpu.PARALLEL,),
        )(idx_hbm, out_hbm)

    return gather_kernel(table, idx.reshape(1, n))


def _scatter_rows(table, idx, src_block, n_out):
    n = idx.shape[0]
    steps = n // SC_GATHER_WINDOW
    assert n % SC_GATHER_WINDOW == 0 and steps % SC_WORKERS == 0
    mesh = plsc.VectorSubcoreMesh(core_axis_name="c", subcore_axis_name="s")

    @functools.partial(pl.kernel, out_type=jax.ShapeDtypeStruct((n_out, table.shape[1]), table.dtype), mesh=mesh,
                       scratch_types=[])
    def scatter_kernel(table_hbm, idx_hbm, out_hbm):
        def body(rows_vmem, idx_vmem):
            pltpu.sync_copy(rows_vmem, out_hbm.at[idx_vmem.at[0]])

        pltpu.emit_pipeline(
            body,
            grid=(steps,),
            in_specs=[pl.BlockSpec((SC_GATHER_WINDOW, table.shape[1]), lambda g: (src_block(g), 0)),
                      pl.BlockSpec((1, SC_GATHER_WINDOW), lambda g: (0, g))],
            out_specs=[],
            core_axis_name=("c", "s"),
            dimension_semantics=(pltpu.PARALLEL,),
        )(table_hbm, idx_hbm)

    return scatter_kernel(table, idx.reshape(1, n))


def _dispatch_packed(table, slot_kt, n_slots):
    planes, n_tok, lanes = table.shape
    blocks = n_tok // SC_GATHER_WINDOW
    idx = slot_kt[None, :, :] + (jnp.arange(planes, dtype=jnp.int32) * n_slots)[:, None, None]
    src_block = lambda g: (g // (TOP_K * blocks)) * blocks + g % blocks
    out = _scatter_rows(table.reshape(planes * n_tok, lanes), idx.reshape(-1), src_block, planes * n_slots)
    return out.reshape(planes, n_slots, lanes)


def _gather_packed(table, rows):
    planes, n_table, lanes = table.shape
    idx = jnp.concatenate([rows + c * n_table for c in range(planes)])
    out = _gather_rows(table.reshape(planes * n_table, lanes), idx)
    return out.reshape(planes, rows.shape[0], lanes)


def _expert_weight_copies(e, w1_hbm, w2_hbm, w1s_ref, w2s_ref, sem):
    return (pltpu.make_async_copy(w1_hbm.at[e], w1s_ref, sem.at[0]),
            pltpu.make_async_copy(w2_hbm.at[e], w2s_ref, sem.at[1]))


def _experts_kernel(te_ref, tr_ref, nx_ref, nv_ref, xs_ref, w1_hbm, b1_ref, w2_hbm, b2_ref, ys_ref,
                    w1s_ref, w2s_ref, w1b_ref, w2b_ref, sem):
    i = pl.program_id(0)
    e = te_ref[i]
    new_expert = (i == 0) | (e != te_ref[jnp.maximum(i - 1, 0)])
    copies = functools.partial(_expert_weight_copies, w1_hbm=w1_hbm, w2_hbm=w2_hbm, w1s_ref=w1s_ref,
                               w2s_ref=w2s_ref, sem=sem)

    @pl.when(i == 0)
    def _():
        for c in copies(e):
            c.start()

    @pl.when(new_expert)
    def _():
        for c in copies(e):
            c.wait()
        w1b_ref[...] = w1s_ref[...].astype(BF16)
        w2b_ref[...] = w2s_ref[...].astype(BF16)

        @pl.when(nx_ref[i] >= 0)
        def _():
            for c in copies(nx_ref[i]):
                c.start()

    @pl.when(i < nv_ref[0])
    def _():
        row = lax.broadcasted_iota(jnp.int32, (SLOT_TILE, D_MODEL), 0)
        x = jnp.where(row < tr_ref[i], _load_packed(xs_ref, SLOT_TILE), 0.0).astype(BF16)
        hu = jnp.dot(x, w1b_ref[...], preferred_element_type=F32) + b1_ref[0]
        x_glu = jnp.minimum(hu[:, :D_FF], SWIGLU_LIMIT)
        x_lin = jnp.clip(hu[:, D_FF:], -SWIGLU_LIMIT, SWIGLU_LIMIT)
        act = x_glu * jax.nn.sigmoid(SWIGLU_ALPHA * x_glu) * (x_lin + 1.0)
        _store_packed(ys_ref, jnp.dot(act.astype(BF16), w2b_ref[...], preferred_element_type=F32) + b2_ref[0])

    @pl.when(i >= nv_ref[0])
    def _():
        ys_ref[...] = jnp.zeros(ys_ref.shape, jnp.uint32)


def _experts(tile_expert, tile_rows, next_expert, n_valid, xs, w1, b1, w2, b2):
    n_slots = xs.shape[1]
    n_tiles = n_slots // SLOT_TILE
    grid_spec = pltpu.PrefetchScalarGridSpec(
        num_scalar_prefetch=4,
        grid=(n_tiles,),
        in_specs=[
            pl.BlockSpec((PACK_ROWS, SLOT_TILE, LANES), lambda i, te, tr, nx, nv: (0, i, 0)),
            pl.BlockSpec(memory_space=pl.ANY),
            pl.BlockSpec((1, 1, 2 * D_FF), lambda i, te, tr, nx, nv: (te[i], 0, 0)),
            pl.BlockSpec(memory_space=pl.ANY),
            pl.BlockSpec((1, 1, D_MODEL), lambda i, te, tr, nx, nv: (te[i], 0, 0)),
        ],
        out_specs=pl.BlockSpec((PACK_ROWS, SLOT_TILE, LANES), lambda i, te, tr, nx, nv: (0, i, 0)),
        scratch_shapes=[pltpu.VMEM((D_MODEL, 2 * D_FF), F32), pltpu.VMEM((D_FF, D_MODEL), F32),
                        pltpu.VMEM((D_MODEL, 2 * D_FF), BF16), pltpu.VMEM((D_FF, D_MODEL), BF16),
                        pltpu.SemaphoreType.DMA((2,))],
    )
    return pl.pallas_call(
        _experts_kernel,
        grid_spec=grid_spec,
        out_shape=jax.ShapeDtypeStruct((PACK_ROWS, n_slots, LANES), jnp.uint32),
        compiler_params=pltpu.CompilerParams(dimension_semantics=("arbitrary",), vmem_limit_bytes=VMEM_LIMIT),
        name="experts",
    )(tile_expert, tile_rows, next_expert, n_valid, xs, w1, b1.reshape(N_EXPERTS, 1, 2 * D_FF), w2,
      b2.reshape(N_EXPERTS, 1, D_MODEL))


def _combine_kernel(y4_ref, x1_ref, topw_ref, g2_ref, fw_ref, o_ref, *, per_row_mod):
    w = topw_ref[...]
    n = w.shape[0]
    ff = None
    for k in range(TOP_K):
        yk = w[:, k:k + 1] * _load_packed(y4_ref.at[:, k], n)
        ff = yk if ff is None else ff + yk
    g2 = g2_ref[...] if per_row_mod else g2_ref[0]
    x = x1_ref[...] + g2 * ff
    ms = jnp.mean(x * x, axis=-1, keepdims=True)
    o_ref[...] = x * lax.rsqrt(ms + NORM_EPS) * fw_ref[...]


def _combine(y4, x1_rows, topw, g2, fw, row_offset, rows_per_mod):
    n = x1_rows.shape[0]
    tile = COMBINE_TILE
    off = row_offset // tile
    per_row = rows_per_mod == 1
    if per_row:
        g2_spec = pl.BlockSpec((tile, D_MODEL), lambda i: (i, 0))
    else:
        g2 = g2.reshape(g2.shape[0], 1, D_MODEL)
        g2_spec = pl.BlockSpec((1, 1, D_MODEL), lambda i: ((i * tile) // rows_per_mod, 0, 0))
    return pl.pallas_call(
        functools.partial(_combine_kernel, per_row_mod=per_row),
        grid=(n // tile,),
        in_specs=[pl.BlockSpec((PACK_ROWS, TOP_K, tile, LANES), lambda i: (0, 0, i + off, 0)),
                  pl.BlockSpec((tile, D_MODEL), lambda i: (i, 0)),
                  pl.BlockSpec((tile, LANES), lambda i: (i + off, 0)),
                  g2_spec, _const_spec(fw.shape)],
        out_specs=pl.BlockSpec((tile, D_MODEL), lambda i: (i, 0)),
        out_shape=jax.ShapeDtypeStruct((n, D_MODEL), F32),
        compiler_params=pltpu.CompilerParams(dimension_semantics=("arbitrary",), vmem_limit_bytes=VMEM_LIMIT),
        name="combine",
    )(y4, x1_rows, topw, g2, fw)


def _routing_tables(route, counts, n_slots):
    padded = ((counts + SLOT_TILE - 1) // SLOT_TILE) * SLOT_TILE
    pend = jnp.cumsum(padded)
    poff = pend - padded
    expert_kt, rank_kt = route[:TOP_K], route[TOP_K:]
    experts = jnp.arange(N_EXPERTS, dtype=jnp.int32)
    start_kt = jnp.sum((expert_kt[None] == experts[:, None, None]).astype(jnp.int32) * poff[:, None, None], axis=0)
    slot_kt = start_kt + rank_kt
    n_tiles = n_slots // SLOT_TILE
    n_valid = (pend[-1] // SLOT_TILE).astype(jnp.int32)
    tile_row = jnp.minimum(jnp.arange(n_tiles, dtype=jnp.int32), n_valid - 1) * SLOT_TILE
    in_later = (pend[None, :] <= tile_row[:, None]).astype(jnp.int32)
    tile_e = jnp.sum(in_later, axis=1).astype(jnp.int32)
    is_e = (experts[None, :] == tile_e[:, None]).astype(jnp.int32)
    used_end = jnp.sum(is_e * (poff + counts)[None, :], axis=1)
    tile_rows = jnp.clip(used_end - tile_row, 0, SLOT_TILE).astype(jnp.int32)
    later_used = (experts[None, :] > tile_e[:, None]) & (counts[None, :] > 0)
    next_e = jnp.min(jnp.where(later_used, experts[None, :], N_EXPERTS), axis=1)
    next_e = jnp.where(next_e < N_EXPERTS, next_e, -1).astype(jnp.int32)
    return slot_kt, tile_e, tile_rows, next_e, n_valid.reshape(1)


def _round_up(n, m):
    return ((n + m - 1) // m) * m


def kernel(x_prompt, x_sample, c_prompt, c_sample, state_ret, state_s5_re, state_s5_im, norm1_w, norm2_w, w_ada, b_ada, w_in, ret_norm_w, s5_lam_re, s5_lam_im, s5_log_dt, s5_b_re, s5_b_im, s5_c_re, s5_c_im, s5_d, w_glu, b_glu, s5_norm_w, w_out, w_router, b_router, w1, b1, w2, b2, final_w):
    bp, lp, _ = x_prompt.shape
    bs, ls, _ = x_sample.shape
    assert norm1_w.shape[0] == 1, "single-layer model"
    n_p, n_s = bp * lp, bs * ls
    n_tok = n_p + n_s

    mod = _ada(jnp.concatenate([c_prompt, c_sample], axis=0), w_ada[0], b_ada[0])
    mod_p, mod_s = mod[:bp], jnp.repeat(mod[bp:], ls, axis=0)
    d3 = 3 * D_MODEL

    lbr, lbi, bbr, bbi = _s5prep(s5_lam_re[0], s5_lam_im[0], s5_log_dt[0], s5_b_re[0], s5_b_im[0])
    bmat = jnp.concatenate([_block_diag(bbr), _block_diag(bbi)], axis=-1).astype(BF16)
    cre = _block_diag(jnp.transpose(s5_c_re[0], (0, 2, 1))).astype(BF16)
    cim = _block_diag(jnp.transpose(-s5_c_im[0], (0, 2, 1))).astype(BF16)
    wts = dict(
        n1w=norm1_w, w_in=w_in[0].astype(BF16), rnw=ret_norm_w, bmat=bmat, cre=cre, cim=cim,
        lbr=lbr.reshape(1, SSM_CH), lbi=lbi.reshape(1, SSM_CH), dsk=s5_d[0].reshape(1, SSM_WIDTH),
        w_glu=w_glu[0].astype(BF16), b_glu=b_glu, snw=s5_norm_w, w_out=w_out[0].astype(BF16),
    )

    zero_states = (jnp.zeros((bp, RET_HEADS, HEAD_DIM, HEAD_DIM), F32), jnp.zeros((bp, SSM_CH), F32),
                   jnp.zeros((bp, SSM_CH), F32))
    x1_p, ret_p, re_p, im_p = _mixer(x_prompt, mod_p[:, :d3], jnp.arange(lp, dtype=F32), zero_states, wts,
                                     prompt=True)
    sample_states = (state_ret[0], state_s5_re[0].reshape(bs, SSM_CH), state_s5_im[0].reshape(bs, SSM_CH))
    x1_s, ret_s, re_s, im_s = _mixer(x_sample.reshape(n_s, D_MODEL), mod_s[:, :d3],
                                     PAST_LEN + jnp.arange(ls, dtype=F32), sample_states, wts, prompt=False)

    x1_p_rows = x1_p.reshape(n_p, D_MODEL)
    h2, route, topw, counts = _router(x1_p_rows, mod_p[:, d3:d3 + 2 * D_MODEL], lp, x1_s,
                                      mod_s[:, d3:d3 + 2 * D_MODEL], norm2_w, w_router[0], b_router)

    n_assign = n_tok * TOP_K
    gather_quantum = SC_GATHER_WINDOW * SC_WORKERS // PACK_ROWS
    assert n_assign % gather_quantum == 0
    n_slots = _round_up(_round_up(n_assign, SLOT_TILE) + N_EXPERTS * SLOT_TILE, gather_quantum)
    slot_kt, tile_e, tile_rows, next_e, n_valid = _routing_tables(route, counts[0, :N_EXPERTS].astype(jnp.int32),
                                                                  n_slots)
    xs = _dispatch_packed(h2, slot_kt, n_slots)
    ys = _experts(tile_e, tile_rows, next_e, n_valid, xs, w1[0], b1[0], w2[0], b2[0])
    y4 = _gather_packed(ys, slot_kt.reshape(-1))
    y4 = y4.reshape(PACK_ROWS, TOP_K, n_tok, LANES)

    g2_p, g2_s = mod_p[:, 5 * D_MODEL:], mod_s[:, 5 * D_MODEL:]
    fw = final_w.reshape(1, D_MODEL)
    y_p = _combine(y4, x1_p_rows, topw, g2_p, fw, 0, lp)
    y_s = _combine(y4, x1_s, topw, g2_s, fw, n_p, 1)

    g, p = SSM_GROUPS, SSM_STATE
    return (y_p.reshape(bp, lp, D_MODEL), y_s.reshape(bs, ls, D_MODEL),
            ret_p[None], re_p.reshape(1, bp, g, p), im_p.reshape(1, bp, g, p),
            ret_s[None], re_s.reshape(1, bs, g, p), im_s.reshape(1, bs, g, p))
```

```python
import functools
import math

import jax
import jax.numpy as jnp
import numpy as np
from jax import lax
from jax.experimental import pallas as pl
from jax.experimental.pallas import tpu as pltpu
from jax.experimental.pallas import tpu_sc as plsc

F32 = jnp.float32
BF16 = jnp.bfloat16
HIGHEST = lax.Precision.HIGHEST

D_MODEL = 1024
PAST_LEN = 16384
RET_WIDTH = 512
RET_HEADS = 4
HEAD_DIM = 128
ROPE_BASE = 10000.0
SSM_WIDTH = 512
SSM_GROUP = 16
SSM_GROUPS = 32
SSM_STATE = 64
SSM_CH = SSM_GROUPS * SSM_STATE
IN_WIDTH = 4 * RET_WIDTH + SSM_WIDTH
N_EXPERTS = 32
TOP_K = 4
D_FF = 1024
SWIGLU_LIMIT = 7.0
SWIGLU_ALPHA = 1.702
NORM_EPS = 1e-6

LANES = 128
SUBLANES = 8
VMEM_LIMIT = 56 * 1024 * 1024

SEQ_PER_BLOCK = 8
PROMPT_CHUNK = 64
S5_BLOCK_GROUPS = 8
N_S5_BLOCKS = SSM_GROUPS // S5_BLOCK_GROUPS
S5_BLOCK_IN = S5_BLOCK_GROUPS * SSM_GROUP
S5_BLOCK_CH = S5_BLOCK_GROUPS * SSM_STATE
ROUTER_TILE = 512
SLOT_TILE = 256
COMBINE_TILE = 256
SC_GATHER_WINDOW = 128
SC_WORKERS = 32


def _silu(x):
    return x * jax.nn.sigmoid(x)


def _ada_kernel(c_ref, w_ref, b_ref, o_ref):
    s = _silu(c_ref[...])
    o_ref[...] = jnp.dot(s, w_ref[...], precision=HIGHEST, preferred_element_type=F32) + b_ref[...]


def _ada(c_all, w_ada, b_ada):
    n_rows, n_out = c_all.shape[0], w_ada.shape[1]
    tn = 1536
    return pl.pallas_call(
        _ada_kernel,
        grid=(n_out // tn,),
        in_specs=[
            pl.BlockSpec((n_rows, D_MODEL), lambda j: (0, 0)),
            pl.BlockSpec((D_MODEL, tn), lambda j: (0, j)),
            pl.BlockSpec((1, tn), lambda j: (0, j)),
        ],
        out_specs=pl.BlockSpec((n_rows, tn), lambda j: (0, j)),
        out_shape=jax.ShapeDtypeStruct((n_rows, n_out), F32),
        compiler_params=pltpu.CompilerParams(dimension_semantics=("arbitrary",), vmem_limit_bytes=VMEM_LIMIT),
        name="ada",
    )(c_all, w_ada, b_ada.reshape(1, n_out))


def _s5prep_kernel(lre_ref, lim_ref, ldt_ref, bre_ref, bim_ref, lbr_ref, lbi_ref, bbr_ref, bbi_ref):
    lam_re, lam_im = lre_ref[...], lim_ref[...]
    dt = jnp.exp(ldt_ref[...])
    mag = jnp.exp(lam_re * dt)
    ang = lam_im * dt
    lb_re, lb_im = mag * jnp.cos(ang), mag * jnp.sin(ang)
    den = lam_re * lam_re + lam_im * lam_im
    f_re = ((lb_re - 1.0) * lam_re + lb_im * lam_im) / den
    f_im = (lb_im * lam_re - (lb_re - 1.0) * lam_im) / den
    lbr_ref[...] = lb_re
    lbi_ref[...] = lb_im
    b_re, b_im = bre_ref[...], bim_ref[...]
    bbr_ref[...] = f_re[:, None, :] * b_re - f_im[:, None, :] * b_im
    bbi_ref[...] = f_re[:, None, :] * b_im + f_im[:, None, :] * b_re


def _s5prep(lam_re, lam_im, log_dt, b_re, b_im):
    g, p = lam_re.shape
    bt_re = jnp.transpose(b_re, (0, 2, 1))
    bt_im = jnp.transpose(b_im, (0, 2, 1))
    return pl.pallas_call(
        _s5prep_kernel,
        out_shape=(
            jax.ShapeDtypeStruct((g, p), F32), jax.ShapeDtypeStruct((g, p), F32),
            jax.ShapeDtypeStruct((g, SSM_GROUP, p), F32), jax.ShapeDtypeStruct((g, SSM_GROUP, p), F32),
        ),
        name="s5prep",
    )(lam_re, lam_im, log_dt.reshape(g, 1), bt_re, bt_im)


def _block_diag(blocks):
    _, r, c = blocks.shape
    b4 = blocks.reshape(N_S5_BLOCKS, S5_BLOCK_GROUPS, r, c)
    eye = jnp.eye(S5_BLOCK_GROUPS, dtype=blocks.dtype)
    out = b4[:, :, :, None, :] * eye[None, :, None, :, None]
    return out.reshape(N_S5_BLOCKS, S5_BLOCK_GROUPS * r, S5_BLOCK_GROUPS * c)


def _mixer_kernel(x_ref, mod_ref, n1w_ref, win_ref, cos_ref, sin_ref, dmask_ref, cdec_ref, sdec_ref,
                  rnw_ref, bmat_ref, cre_ref, cim_ref, lbr_ref, lbi_ref, dsk_ref, wglu_ref, bglu_ref,
                  snw_ref, wout_ref, sret0_ref, sre0_ref, sim0_ref,
                  x1_ref, sret_ref, sre_ref, sim_ref,
                  hb_ref, z_ref, zu_ref, oy_ref, utb_ref, bur_ref, bui_ref, ytb_ref, yb_ref,
                  *, n_seq, chunk, tile_rows, carry, chunk_decay):
    rows = n_seq * chunk
    seq_per_tile = tile_rows // chunk
    n_tiles = rows // tile_rows
    per_row_mod = mod_ref.shape[0] == rows

    def load_states():
        sret_ref[...] = sret0_ref[...]
        sre_ref[...] = sre0_ref[...]
        sim_ref[...] = sim0_ref[...]

    if carry:
        pl.when(pl.program_id(0) == 0)(load_states)
    else:
        load_states()

    n1w = n1w_ref[...]
    mod_rows = rows if per_row_mod else chunk
    for i in range(rows // mod_rows):
        r0 = i * mod_rows
        xb = _load_rows(x_ref, r0, mod_rows, chunk)
        if per_row_mod:
            sh = mod_ref[pl.ds(r0, mod_rows), pl.ds(0, D_MODEL)]
            sc = mod_ref[pl.ds(r0, mod_rows), pl.ds(D_MODEL, D_MODEL)]
        else:
            sh = mod_ref[pl.ds(i, 1), pl.ds(0, D_MODEL)]
            sc = mod_ref[pl.ds(i, 1), pl.ds(D_MODEL, D_MODEL)]
        ms = jnp.mean(xb * xb, axis=-1, keepdims=True)
        hn = xb * lax.rsqrt(ms + NORM_EPS) * n1w
        hb_ref[pl.ds(r0, mod_rows), :] = (hn * (1.0 + sc) + sh).astype(BF16)
    ret_w = 4 * RET_WIDTH
    z_ref[...] = jnp.dot(hb_ref[...], win_ref[:, pl.ds(0, ret_w)], preferred_element_type=F32)
    zu = jnp.dot(hb_ref[...], win_ref[:, pl.ds(ret_w, SSM_WIDTH)], preferred_element_type=F32)
    pitch = zu_ref.shape[1] // n_seq
    for c in range(SSM_WIDTH // LANES):
        for b in range(n_seq if pitch != chunk else 1):
            nb = chunk if pitch != chunk else rows
            zu_ref[c, pl.ds(b * pitch, nb), :] = zu[b * chunk:b * chunk + nb, c * LANES:(c + 1) * LANES]

    cos = cos_ref[...]
    sin = sin_ref[...]
    scale = HEAD_DIM ** -0.5
    if seq_per_tile > 1:
        row_id = lax.broadcasted_iota(jnp.int32, (tile_rows, HEAD_DIM), 0)

    def rope(t):
        return t * cos + pltpu.roll(t, HEAD_DIM // 2, 1) * sin

    def ret_tile(ti, c):
        r0 = pl.multiple_of(ti * tile_rows, tile_rows)
        for h in range(RET_HEADS):
            c0 = h * HEAD_DIM
            q = rope(z_ref[pl.ds(r0, tile_rows), pl.ds(c0, HEAD_DIM)])
            k = rope(z_ref[pl.ds(r0, tile_rows), pl.ds(RET_WIDTH + c0, HEAD_DIM)]) * scale
            v = z_ref[pl.ds(r0, tile_rows), pl.ds(2 * RET_WIDTH + c0, HEAD_DIM)]
            g = z_ref[pl.ds(r0, tile_rows), pl.ds(3 * RET_WIDTH + c0, HEAD_DIM)]
            kd = k * sdec_ref[h]
            if tile_rows < HEAD_DIM:
                pad = jnp.zeros((HEAD_DIM - tile_rows, HEAD_DIM), F32)
                k, v, kd = (jnp.concatenate([t, pad], axis=0) for t in (k, v, kd))
                if seq_per_tile > 1:
                    row_kv = lax.broadcasted_iota(jnp.int32, (HEAD_DIM, HEAD_DIM), 0)
            elif seq_per_tile > 1:
                row_kv = row_id
            qb, kb, vb = q.astype(BF16), k.astype(BF16), v.astype(BF16)
            s = lax.dot_general(qb, kb, (((1,), (1,)), ((), ())), preferred_element_type=F32) * dmask_ref[h]
            o = jnp.dot(s.astype(BF16), vb, preferred_element_type=F32)
            cross = None
            for si in range(seq_per_tile):
                sidx = ti * seq_per_tile + si
                st = sret_ref[sidx, h]
                cr = jnp.dot(qb, st.astype(BF16), preferred_element_type=F32)
                if seq_per_tile > 1:
                    in_seq = (row_id >= si * chunk) & (row_id < (si + 1) * chunk)
                    cross = jnp.where(in_seq, cr, 0.0 if cross is None else cross)
                    kds = jnp.where((row_kv >= si * chunk) & (row_kv < (si + 1) * chunk), kd, 0.0)
                else:
                    cross, kds = cr, kd
                upd = lax.dot_general(kds.astype(BF16), vb, (((0,), (0,)), ((), ())), preferred_element_type=F32)
                sret_ref[sidx, h] = st * chunk_decay[h] + upd
            o = o + cross * cdec_ref[h]
            o = o * lax.rsqrt(jnp.mean(o * o, axis=-1, keepdims=True) + NORM_EPS)
            o = o * rnw_ref[:, pl.ds(c0, HEAD_DIM)] * _silu(g)
            oy_ref[pl.ds(r0, tile_rows), pl.ds(c0, HEAD_DIM)] = o
        return c

    lax.fori_loop(0, n_tiles, ret_tile, 0, unroll=True)

    for t in range(chunk):
        for c in range(SSM_WIDTH // LANES):
            utb_ref[pl.ds(t * n_seq, n_seq), pl.ds(c * LANES, LANES)] = zu_ref[c, pl.ds(t, n_seq, stride=pitch), :]
    for blk in range(N_S5_BLOCKS):
        ub = utb_ref[:, pl.ds(blk * S5_BLOCK_IN, S5_BLOCK_IN)].astype(BF16)
        bu = jnp.dot(ub, bmat_ref[blk], preferred_element_type=F32)
        bur_ref[:, pl.ds(blk * S5_BLOCK_CH, S5_BLOCK_CH)] = bu[:, :S5_BLOCK_CH]
        bui_ref[:, pl.ds(blk * S5_BLOCK_CH, S5_BLOCK_CH)] = bu[:, S5_BLOCK_CH:]

    scan_w = 2 * S5_BLOCK_CH
    for p in range(SSM_CH // scan_w):
        cols = pl.ds(p * scan_w, scan_w)
        lbr = jnp.broadcast_to(lbr_ref[:, cols], (n_seq, scan_w))
        lbi = jnp.broadcast_to(lbi_ref[:, cols], (n_seq, scan_w))

        def scan_step(t, hc, cols=cols, lbr=lbr, lbi=lbi):
            hr, hi = hc
            r0 = pl.multiple_of(t * n_seq, n_seq)
            nr = lbr * hr - lbi * hi + bur_ref[pl.ds(r0, n_seq), cols]
            ni = lbr * hi + lbi * hr + bui_ref[pl.ds(r0, n_seq), cols]
            bur_ref[pl.ds(r0, n_seq), cols] = nr
            bui_ref[pl.ds(r0, n_seq), cols] = ni
            return nr, ni

        h0 = (sre_ref[:, cols], sim_ref[:, cols])
        if chunk <= 8:
            hc = h0
            for t in range(chunk):
                hc = scan_step(t, hc)
        else:
            unroll = 4

            def scan_group(tg, hc):
                for j in range(unroll):
                    hc = scan_step(tg * unroll + j, hc)
                return hc

            hc = lax.fori_loop(0, chunk // unroll, scan_group, h0)
        sre_ref[:, cols] = hc[0]
        sim_ref[:, cols] = hc[1]

    for blk in range(N_S5_BLOCKS):
        cols = pl.ds(blk * S5_BLOCK_CH, S5_BLOCK_CH)
        yb = jnp.dot(bur_ref[:, cols].astype(BF16), cre_ref[blk], preferred_element_type=F32)
        yb = yb + jnp.dot(bui_ref[:, cols].astype(BF16), cim_ref[blk], preferred_element_type=F32)
        ucols = pl.ds(blk * S5_BLOCK_IN, S5_BLOCK_IN)
        ytb_ref[:, ucols] = yb + dsk_ref[:, ucols] * utb_ref[:, ucols]
    for t in range(chunk):
        for c in range(SSM_WIDTH // LANES):
            yb_ref[c, pl.ds(t, n_seq, stride=pitch), :] = ytb_ref[pl.ds(t * n_seq, n_seq), pl.ds(c * LANES, LANES)]

    def seq_major(c):
        if pitch == chunk:
            return yb_ref[c]
        return jnp.concatenate([yb_ref[c, pl.ds(b * pitch, chunk), :] for b in range(n_seq)], axis=0)

    y = jnp.concatenate([seq_major(c) for c in range(SSM_WIDTH // LANES)], axis=1)
    y = jax.nn.gelu(y, approximate=True)
    gate = jnp.dot(y.astype(BF16), wglu_ref[...], preferred_element_type=F32) + bglu_ref[...]
    y = y * jax.nn.sigmoid(gate)
    y = y * lax.rsqrt(jnp.mean(y * y, axis=-1, keepdims=True) + NORM_EPS) * snw_ref[...]
    oy_ref[:, pl.ds(RET_WIDTH, SSM_WIDTH)] = y

    mix = jnp.dot(oy_ref[...].astype(BF16), wout_ref[...], preferred_element_type=F32)
    for i in range(rows // mod_rows):
        r0 = i * mod_rows
        if per_row_mod:
            g1 = mod_ref[pl.ds(r0, mod_rows), pl.ds(2 * D_MODEL, D_MODEL)]
        else:
            g1 = mod_ref[pl.ds(i, 1), pl.ds(2 * D_MODEL, D_MODEL)]
        _store_rows(x1_ref, r0, mod_rows, chunk,
                    _load_rows(x_ref, r0, mod_rows, chunk) + g1 * mix[r0:r0 + mod_rows])


def _load_rows(ref, r0, n, chunk):
    if len(ref.shape) == 2:
        return ref[pl.ds(r0, n), :]
    assert n == chunk and r0 % chunk == 0
    return ref[r0 // chunk]


def _store_rows(ref, r0, n, chunk, val):
    if len(ref.shape) == 2:
        ref[pl.ds(r0, n), :] = val
    else:
        assert n == chunk and r0 % chunk == 0
        ref[r0 // chunk] = val


def _seq_pitch(chunk):
    return chunk + SUBLANES if chunk % SUBLANES == 0 else chunk


def _const_spec(shape):
    nd = len(shape)
    return pl.BlockSpec(shape, lambda j, _n=nd: (0,) * _n)


def _decay_tables(chunk, tile_rows):
    f32 = np.float32
    log_gamma = np.log1p(-np.exp2(f32(-5.0) - np.arange(RET_HEADS, dtype=f32))).astype(f32)
    r = np.arange(tile_rows)
    seq, loc = r // chunk, (r % chunk).astype(f32)
    rel = loc[:, None] - loc[None, :]
    ok = (seq[:, None] == seq[None, :]) & (rel >= 0)
    dmask = np.where(ok[None], np.exp(np.where(ok, rel, f32(0.0))[None] * log_gamma[:, None, None]), f32(0.0))
    if tile_rows < HEAD_DIM:
        dmask = np.pad(dmask, ((0, 0), (0, 0), (0, HEAD_DIM - tile_rows)))
    cdec = np.exp((loc[None, :] + f32(1.0)) * log_gamma[:, None])
    sdec = np.exp((f32(chunk) - f32(1.0) - loc)[None, :] * log_gamma[:, None])
    bcast = lambda t: np.ascontiguousarray(np.broadcast_to(t[:, :, None], (RET_HEADS, tile_rows, HEAD_DIM)))
    return dmask.astype(f32), bcast(cdec.astype(f32)), bcast(sdec.astype(f32))


def _rope_tables(pos):
    f32 = np.float32
    half = HEAD_DIM // 2
    inv_freq = (f32(ROPE_BASE) ** (-np.arange(half, dtype=f32) / f32(half))).astype(f32)
    ang = (pos.astype(f32)[:, None] * inv_freq[None, :]).astype(f32)
    cos, sin = np.cos(ang).astype(f32), np.sin(ang).astype(f32)
    return np.concatenate([cos, cos], axis=-1), np.concatenate([-sin, sin], axis=-1)


def _mixer(x, mod, pos, states, wts, *, prompt):
    n_seq = SEQ_PER_BLOCK
    if prompt:
        n_total, seq_len, _ = x.shape
        assert n_total == n_seq
        chunk, tile_rows, n_steps = PROMPT_CHUNK, PROMPT_CHUNK, seq_len // PROMPT_CHUNK
        x_spec = pl.BlockSpec((n_seq, chunk, D_MODEL), lambda j: (0, j, 0))
        mod_spec = pl.BlockSpec((n_seq, 3 * D_MODEL), lambda j: (0, 0))
        tab_spec = pl.BlockSpec((chunk, HEAD_DIM), lambda j: (j, 0))
        seq_map = lambda j: 0
    else:
        chunk = pos.shape[0]
        tile_rows = SUBLANES
        n_total = x.shape[0] // chunk
        n_steps = n_total // n_seq
        x_spec = pl.BlockSpec((n_seq * chunk, D_MODEL), lambda j: (j, 0))
        mod_spec = pl.BlockSpec((n_seq * chunk, 3 * D_MODEL), lambda j: (j, 0))
        tab_spec = _const_spec((tile_rows, HEAD_DIM))
        seq_map = lambda j: j
    rows = n_seq * chunk
    cos, sin = _rope_tables(pos)
    if not prompt:
        reps = tile_rows // chunk
        cos, sin = np.tile(cos, (reps, 1)), np.tile(sin, (reps, 1))
    dmask, cdec, sdec = _decay_tables(chunk, tile_rows)
    chunk_decay = tuple(float(math.exp(chunk * math.log1p(-2.0 ** (-5.0 - h)))) for h in range(RET_HEADS))
    sret0, sre0, sim0 = states

    st_ret_spec = pl.BlockSpec((n_seq, RET_HEADS, HEAD_DIM, HEAD_DIM), lambda j: (seq_map(j), 0, 0, 0))
    st_s5_spec = pl.BlockSpec((n_seq, SSM_CH), lambda j: (seq_map(j), 0))
    consts = [dmask, cdec, sdec, wts["rnw"], wts["bmat"], wts["cre"], wts["cim"], wts["lbr"], wts["lbi"],
              wts["dsk"], wts["w_glu"], wts["b_glu"], wts["snw"], wts["w_out"]]
    args = [x, mod, wts["n1w"], wts["w_in"], cos, sin] + consts + [sret0, sre0, sim0]
    in_specs = ([x_spec, mod_spec, _const_spec(wts["n1w"].shape), _const_spec(wts["w_in"].shape), tab_spec, tab_spec]
                + [_const_spec(a.shape) for a in consts] + [st_ret_spec, st_s5_spec, st_s5_spec])

    kern = functools.partial(_mixer_kernel, n_seq=n_seq, chunk=chunk, tile_rows=tile_rows, carry=prompt,
                             chunk_decay=chunk_decay)
    out_shape = (
        jax.ShapeDtypeStruct(x.shape, F32),
        jax.ShapeDtypeStruct((n_total, RET_HEADS, HEAD_DIM, HEAD_DIM), F32),
        jax.ShapeDtypeStruct((n_total, SSM_CH), F32),
        jax.ShapeDtypeStruct((n_total, SSM_CH), F32),
    )
    scratch = [
        pltpu.VMEM((rows, D_MODEL), BF16),
        pltpu.VMEM((rows, 4 * RET_WIDTH), F32),
        pltpu.VMEM((SSM_WIDTH // LANES, n_seq * _seq_pitch(chunk), LANES), F32),
        pltpu.VMEM((rows, D_MODEL), F32),
        pltpu.VMEM((rows, SSM_WIDTH), F32),
        pltpu.VMEM((rows, SSM_CH), F32),
        pltpu.VMEM((rows, SSM_CH), F32),
        pltpu.VMEM((rows, SSM_WIDTH), F32),
        pltpu.VMEM((SSM_WIDTH // LANES, n_seq * _seq_pitch(chunk), LANES), F32),
    ]
    return pl.pallas_call(
        kern,
        grid=(n_steps,),
        in_specs=in_specs,
        out_specs=(x_spec, st_ret_spec, st_s5_spec, st_s5_spec),
        out_shape=out_shape,
        scratch_shapes=scratch,
        compiler_params=pltpu.CompilerParams(dimension_semantics=("arbitrary",), vmem_limit_bytes=VMEM_LIMIT),
        name="mixer_prompt" if prompt else "mixer_sample",
    )(*args)


PACK_ROWS = D_MODEL // (2 * LANES)


def _store_packed(ref, x):
    half = D_MODEL // 2
    bits = lax.bitcast_convert_type(x.astype(BF16).astype(F32), jnp.uint32)
    words = bits[:, :half] | (bits[:, half:] >> 16)
    for c in range(PACK_ROWS):
        ref[c] = words[:, c * LANES:(c + 1) * LANES]


def _load_packed(ref, n, first_row=0, row_stride=1):
    hi, lo = [], []
    for c in range(PACK_ROWS):
        w = ref[c] if row_stride == 1 else ref[c, pl.ds(first_row, n, stride=row_stride), :]
        hi.append(lax.bitcast_convert_type(w & jnp.uint32(0xFFFF0000), F32))
        lo.append(lax.bitcast_convert_type(w << 16, F32))
    return jnp.concatenate(hi + lo, axis=1)


def _split_bf16(x):
    hi = x.astype(BF16)
    return hi, (x - hi.astype(F32)).astype(BF16)


def _route_tile(x, sh, sc, n2w_ref, wrh_ref, wrm_ref, br_ref, ltri_ref, count_ref, h2_ref, route_ref, topw_ref):
    ms = jnp.mean(x * x, axis=-1, keepdims=True)
    h2 = x * lax.rsqrt(ms + NORM_EPS) * n2w_ref[...] * (1.0 + sc) + sh
    _store_packed(h2_ref, h2)
    hh, hm = _split_bf16(h2)
    logits = (jnp.dot(hh, wrh_ref[...], preferred_element_type=F32)
              + (jnp.dot(hh, wrm_ref[...], preferred_element_type=F32)
                 + jnp.dot(hm, wrh_ref[...], preferred_element_type=F32))) + br_ref[...]
    lane = lax.broadcasted_iota(jnp.int32, logits.shape, 1)
    work = logits
    vals, idxs = [], []
    for _ in range(TOP_K):
        m = jnp.max(work, axis=-1, keepdims=True)
        idx = jnp.min(jnp.where(work == m, lane, LANES), axis=-1, keepdims=True)
        vals.append(m)
        idxs.append(idx)
        work = jnp.where(lane == idx, -jnp.inf, work)
    exps = [jnp.exp(v - vals[0]) for v in vals]
    tot = exps[0] + exps[1] + exps[2] + exps[3]
    topw = jnp.zeros(logits.shape, F32)
    for k in range(TOP_K):
        topw = jnp.where(lane == k, exps[k] / tot, topw)
    topw_ref[...] = topw

    onehot = [(lane == idxs[k]).astype(F32) for k in range(TOP_K)]
    chosen = onehot[0] + onehot[1] + onehot[2] + onehot[3]
    before = jnp.dot(ltri_ref[...], chosen.astype(BF16), preferred_element_type=F32) + count_ref[...]
    info = jnp.zeros(logits.shape, jnp.int32)
    for k in range(TOP_K):
        rank = jnp.sum(onehot[k] * before, axis=-1, keepdims=True).astype(jnp.int32)
        info = jnp.where(lane == k, idxs[k], info)
        info = jnp.where(lane == TOP_K + k, rank, info)
    route_ref[...] = jnp.transpose(info)[:2 * TOP_K, :]
    count_ref[...] = count_ref[...] + jnp.sum(chosen, axis=0, keepdims=True)


def _router_kernel(xp_ref, shp_ref, scp_ref, xs_ref, shs_ref, scs_ref, n2w_ref, wrh_ref, wrm_ref, br_ref, ltri_ref,
                   h2_ref, route_ref, topw_ref, count_ref, *, n_prompt_tiles):
    i = pl.program_id(0)
    rest = (n2w_ref, wrh_ref, wrm_ref, br_ref, ltri_ref, count_ref, h2_ref, route_ref, topw_ref)

    @pl.when(i == 0)
    def _():
        count_ref[...] = jnp.zeros(count_ref.shape, F32)

    @pl.when(i < n_prompt_tiles)
    def _():
        _route_tile(xp_ref[...], shp_ref[0], scp_ref[0], *rest)

    @pl.when(i >= n_prompt_tiles)
    def _():
        _route_tile(xs_ref[...], shs_ref[...], scs_ref[...], *rest)


def _router(xp_rows, mod_p, seq_len, xs_rows, mod_s, n2w, w_router, b_router):
    tile = ROUTER_TILE
    n_p, n_s = xp_rows.shape[0], xs_rows.shape[0]
    tp, ts = n_p // tile, n_s // tile
    n_total = n_p + n_s
    mod_p = mod_p.reshape(mod_p.shape[0], 1, mod_p.shape[1])
    seq_of = lambda i: (jnp.minimum(i, tp - 1) * tile) // seq_len
    wr_pad = jnp.pad(w_router, ((0, 0), (0, LANES - N_EXPERTS)))
    wr_hi = wr_pad.astype(BF16)
    wr_mid = (wr_pad - wr_hi.astype(F32)).astype(BF16)
    br_pad = jnp.pad(b_router, ((0, 0), (0, LANES - N_EXPERTS)), constant_values=-1e30)
    ltri = jnp.asarray(np.tril(np.ones((tile, tile), np.float32), -1), BF16)
    clamp_p = lambda i: jnp.minimum(i, tp - 1)
    clamp_s = lambda i: jnp.maximum(i - tp, 0)
    return pl.pallas_call(
        functools.partial(_router_kernel, n_prompt_tiles=tp),
        grid=(tp + ts,),
        in_specs=[pl.BlockSpec((tile, D_MODEL), lambda i: (clamp_p(i), 0)),
                  pl.BlockSpec((1, 1, D_MODEL), lambda i: (seq_of(i), 0, 3)),
                  pl.BlockSpec((1, 1, D_MODEL), lambda i: (seq_of(i), 0, 4)),
                  pl.BlockSpec((tile, D_MODEL), lambda i: (clamp_s(i), 0)),
                  pl.BlockSpec((tile, D_MODEL), lambda i: (clamp_s(i), 3)),
                  pl.BlockSpec((tile, D_MODEL), lambda i: (clamp_s(i), 4)),
                  _const_spec(n2w.shape), _const_spec(wr_hi.shape), _const_spec(wr_mid.shape),
                  _const_spec(br_pad.shape), _const_spec(ltri.shape)],
        out_specs=(pl.BlockSpec((PACK_ROWS, tile, LANES), lambda i: (0, i, 0)),
                   pl.BlockSpec((2 * TOP_K, tile), lambda i: (0, i)),
                   pl.BlockSpec((tile, LANES), lambda i: (i, 0)),
                   pl.BlockSpec((1, LANES), lambda i: (0, 0))),
        out_shape=(jax.ShapeDtypeStruct((PACK_ROWS, n_total, LANES), jnp.uint32),
                   jax.ShapeDtypeStruct((2 * TOP_K, n_total), jnp.int32),
                   jax.ShapeDtypeStruct((n_total, LANES), F32),
                   jax.ShapeDtypeStruct((1, LANES), F32)),
        compiler_params=pltpu.CompilerParams(dimension_semantics=("arbitrary",), vmem_limit_bytes=VMEM_LIMIT),
        name="router",
    )(xp_rows, mod_p, mod_p, xs_rows, mod_s, mod_s, n2w, wr_hi, wr_mid, br_pad, ltri)


def _gather_rows(table, idx):
    n = idx.shape[0]
    steps = n // SC_GATHER_WINDOW
    assert n % SC_GATHER_WINDOW == 0 and steps % SC_WORKERS == 0
    mesh = plsc.VectorSubcoreMesh(core_axis_name="c", subcore_axis_name="s")

    @functools.partial(pl.kernel, out_type=jax.ShapeDtypeStruct((n, table.shape[1]), table.dtype), mesh=mesh,
                       scratch_types=[])
    def gather_kernel(table_hbm, idx_hbm, out_hbm):
        def body(idx_vmem, out_vmem):
            pltpu.sync_copy(table_hbm.at[idx_vmem.at[0]], out_vmem)

        pltpu.emit_pipeline(
            body,
            grid=(steps,),
            in_specs=[pl.BlockSpec((1, SC_GATHER_WINDOW), lambda i: (0, i))],
            out_specs=[pl.BlockSpec((SC_GATHER_WINDOW, table.shape[1]), lambda i: (i, 0))],
            core_axis_name=("c", "s"),
            dimension_semantics=(pltpu.PARALLEL,),
        )(idx_hbm, out_hbm)

    return gather_kernel(table, idx.reshape(1, n))


def _scatter_rows(table, idx, src_block, n_out):
    n = idx.shape[0]
    steps = n // SC_GATHER_WINDOW
    assert n % SC_GATHER_WINDOW == 0 and steps % SC_WORKERS == 0
    mesh = plsc.VectorSubcoreMesh(core_axis_name="c", subcore_axis_name="s")

    @functools.partial(pl.kernel, out_type=jax.ShapeDtypeStruct((n_out, table.shape[1]), table.dtype), mesh=mesh,
                       scratch_types=[])
    def scatter_kernel(table_hbm, idx_hbm, out_hbm):
        def body(rows_vmem, idx_vmem):
            pltpu.sync_copy(rows_vmem, out_hbm.at[idx_vmem.at[0]])

        pltpu.emit_pipeline(
            body,
            grid=(steps,),
            in_specs=[pl.BlockSpec((SC_GATHER_WINDOW, table.shape[1]), lambda g: (src_block(g), 0)),
                      pl.BlockSpec((1, SC_GATHER_WINDOW), lambda g: (0, g))],
            out_specs=[],
            core_axis_name=("c", "s"),
            dimension_semantics=(pltpu.PARALLEL,),
        )(table_hbm, idx_hbm)

    return scatter_kernel(table, idx.reshape(1, n))


def _dispatch_packed(table, slot_kt, n_slots):
    planes, n_tok, lanes = table.shape
    blocks = n_tok // SC_GATHER_WINDOW
    idx = slot_kt[None, :, :] + (jnp.arange(planes, dtype=jnp.int32) * n_slots)[:, None, None]
    src_block = lambda g: (g // (TOP_K * blocks)) * blocks + g % blocks
    out = _scatter_rows(table.reshape(planes * n_tok, lanes), idx.reshape(-1), src_block, planes * n_slots)
    return out.reshape(planes, n_slots, lanes)


def _gather_packed(table, rows):
    planes, n_table, lanes = table.shape
    idx = jnp.concatenate([rows + c * n_table for c in range(planes)])
    out = _gather_rows(table.reshape(planes * n_table, lanes), idx)
    return out.reshape(planes, rows.shape[0], lanes)


def _expert_weight_copies(e, w1_hbm, w2_hbm, w1s_ref, w2s_ref, sem):
    return (pltpu.make_async_copy(w1_hbm.at[e], w1s_ref, sem.at[0]),
            pltpu.make_async_copy(w2_hbm.at[e], w2s_ref, sem.at[1]))


def _experts_kernel(te_ref, tr_ref, nx_ref, nv_ref, xs_ref, w1_hbm, b1_ref, w2_hbm, b2_ref, ys_ref,
                    w1s_ref, w2s_ref, w1b_ref, w2b_ref, sem):
    i = pl.program_id(0)
    e = te_ref[i]
    new_expert = (i == 0) | (e != te_ref[jnp.maximum(i - 1, 0)])
    copies = functools.partial(_expert_weight_copies, w1_hbm=w1_hbm, w2_hbm=w2_hbm, w1s_ref=w1s_ref,
                               w2s_ref=w2s_ref, sem=sem)

    @pl.when(i == 0)
    def _():
        for c in copies(e):
            c.start()

    @pl.when(new_expert)
    def _():
        for c in copies(e):
            c.wait()
        w1b_ref[...] = w1s_ref[...].astype(BF16)
        w2b_ref[...] = w2s_ref[...].astype(BF16)

        @pl.when(nx_ref[i] >= 0)
        def _():
            for c in copies(nx_ref[i]):
                c.start()

    @pl.when(i < nv_ref[0])
    def _():
        row = lax.broadcasted_iota(jnp.int32, (SLOT_TILE, D_MODEL), 0)
        x = jnp.where(row < tr_ref[i], _load_packed(xs_ref, SLOT_TILE), 0.0).astype(BF16)
        hu = jnp.dot(x, w1b_ref[...], preferred_element_type=F32) + b1_ref[0]
        x_glu = jnp.minimum(hu[:, :D_FF], SWIGLU_LIMIT)
        x_lin = jnp.clip(hu[:, D_FF:], -SWIGLU_LIMIT, SWIGLU_LIMIT)
        act = x_glu * jax.nn.sigmoid(SWIGLU_ALPHA * x_glu) * (x_lin + 1.0)
        _store_packed(ys_ref, jnp.dot(act.astype(BF16), w2b_ref[...], preferred_element_type=F32) + b2_ref[0])

    @pl.when(i >= nv_ref[0])
    def _():
        ys_ref[...] = jnp.zeros(ys_ref.shape, jnp.uint32)


def _experts(tile_expert, tile_rows, next_expert, n_valid, xs, w1, b1, w2, b2):
    n_slots = xs.shape[1]
    n_tiles = n_slots // SLOT_TILE
    grid_spec = pltpu.PrefetchScalarGridSpec(
        num_scalar_prefetch=4,
        grid=(n_tiles,),
        in_specs=[
            pl.BlockSpec((PACK_ROWS, SLOT_TILE, LANES), lambda i, te, tr, nx, nv: (0, i, 0)),
            pl.BlockSpec(memory_space=pl.ANY),
            pl.BlockSpec((1, 1, 2 * D_FF), lambda i, te, tr, nx, nv: (te[i], 0, 0)),
            pl.BlockSpec(memory_space=pl.ANY),
            pl.BlockSpec((1, 1, D_MODEL), lambda i, te, tr, nx, nv: (te[i], 0, 0)),
        ],
        out_specs=pl.BlockSpec((PACK_ROWS, SLOT_TILE, LANES), lambda i, te, tr, nx, nv: (0, i, 0)),
        scratch_shapes=[pltpu.VMEM((D_MODEL, 2 * D_FF), F32), pltpu.VMEM((D_FF, D_MODEL), F32),
                        pltpu.VMEM((D_MODEL, 2 * D_FF), BF16), pltpu.VMEM((D_FF, D_MODEL), BF16),
                        pltpu.SemaphoreType.DMA((2,))],
    )
    return pl.pallas_call(
        _experts_kernel,
        grid_spec=grid_spec,
        out_shape=jax.ShapeDtypeStruct((PACK_ROWS, n_slots, LANES), jnp.uint32),
        compiler_params=pltpu.CompilerParams(dimension_semantics=("arbitrary",), vmem_limit_bytes=VMEM_LIMIT),
        name="experts",
    )(tile_expert, tile_rows, next_expert, n_valid, xs, w1, b1.reshape(N_EXPERTS, 1, 2 * D_FF), w2,
      b2.reshape(N_EXPERTS, 1, D_MODEL))


def _combine_kernel(y4_ref, x1_ref, topw_ref, g2_ref, fw_ref, o_ref, *, per_row_mod):
    w = topw_ref[...]
    n = w.shape[0]
    ff = None
    for k in range(TOP_K):
        yk = w[:, k:k + 1] * _load_packed(y4_ref.at[:, k], n)
        ff = yk if ff is None else ff + yk
    g2 = g2_ref[...] if per_row_mod else g2_ref[0]
    x = x1_ref[...] + g2 * ff
    ms = jnp.mean(x * x, axis=-1, keepdims=True)
    o_ref[...] = x * lax.rsqrt(ms + NORM_EPS) * fw_ref[...]


def _combine(y4, x1_rows, topw, g2, fw, row_offset, rows_per_mod):
    n = x1_rows.shape[0]
    tile = COMBINE_TILE
    off = row_offset // tile
    per_row = rows_per_mod == 1
    if per_row:
        g2_spec = pl.BlockSpec((tile, D_MODEL), lambda i: (i, 5))
    else:
        g2 = g2.reshape(g2.shape[0], 1, g2.shape[1])
        g2_spec = pl.BlockSpec((1, 1, D_MODEL), lambda i: ((i * tile) // rows_per_mod, 0, 5))
    return pl.pallas_call(
        functools.partial(_combine_kernel, per_row_mod=per_row),
        grid=(n // tile,),
        in_specs=[pl.BlockSpec((PACK_ROWS, TOP_K, tile, LANES), lambda i: (0, 0, i + off, 0)),
                  pl.BlockSpec((tile, D_MODEL), lambda i: (i, 0)),
                  pl.BlockSpec((tile, LANES), lambda i: (i + off, 0)),
                  g2_spec, _const_spec(fw.shape)],
        out_specs=pl.BlockSpec((tile, D_MODEL), lambda i: (i, 0)),
        out_shape=jax.ShapeDtypeStruct((n, D_MODEL), F32),
        compiler_params=pltpu.CompilerParams(dimension_semantics=("arbitrary",), vmem_limit_bytes=VMEM_LIMIT),
        name="combine",
    )(y4, x1_rows, topw, g2, fw)


def _routing_tables(route, counts, n_slots):
    padded = ((counts + SLOT_TILE - 1) // SLOT_TILE) * SLOT_TILE
    pend = jnp.cumsum(padded)
    poff = pend - padded
    expert_kt, rank_kt = route[:TOP_K], route[TOP_K:]
    experts = jnp.arange(N_EXPERTS, dtype=jnp.int32)
    start_kt = jnp.sum((expert_kt[None] == experts[:, None, None]).astype(jnp.int32) * poff[:, None, None], axis=0)
    slot_kt = start_kt + rank_kt
    n_tiles = n_slots // SLOT_TILE
    n_valid = (pend[-1] // SLOT_TILE).astype(jnp.int32)
    tile_row = jnp.minimum(jnp.arange(n_tiles, dtype=jnp.int32), n_valid - 1) * SLOT_TILE
    in_later = (pend[None, :] <= tile_row[:, None]).astype(jnp.int32)
    tile_e = jnp.sum(in_later, axis=1).astype(jnp.int32)
    is_e = (experts[None, :] == tile_e[:, None]).astype(jnp.int32)
    used_end = jnp.sum(is_e * (poff + counts)[None, :], axis=1)
    tile_rows = jnp.clip(used_end - tile_row, 0, SLOT_TILE).astype(jnp.int32)
    later_used = (experts[None, :] > tile_e[:, None]) & (counts[None, :] > 0)
    next_e = jnp.min(jnp.where(later_used, experts[None, :], N_EXPERTS), axis=1)
    next_e = jnp.where(next_e < N_EXPERTS, next_e, -1).astype(jnp.int32)
    return slot_kt, tile_e, tile_rows, next_e, n_valid.reshape(1)


def _round_up(n, m):
    return ((n + m - 1) // m) * m


def kernel(x_prompt, x_sample, c_prompt, c_sample, state_ret, state_s5_re, state_s5_im, norm1_w, norm2_w, w_ada, b_ada, w_in, ret_norm_w, s5_lam_re, s5_lam_im, s5_log_dt, s5_b_re, s5_b_im, s5_c_re, s5_c_im, s5_d, w_glu, b_glu, s5_norm_w, w_out, w_router, b_router, w1, b1, w2, b2, final_w):
    bp, lp, _ = x_prompt.shape
    bs, ls, _ = x_sample.shape
    assert norm1_w.shape[0] == 1, "single-layer model"
    n_p, n_s = bp * lp, bs * ls
    n_tok = n_p + n_s

    mod = _ada(jnp.concatenate([c_prompt, c_sample], axis=0), w_ada[0], b_ada[0])
    mod_p, mod_s = mod[:bp], jnp.repeat(mod[bp:], ls, axis=0)

    lbr, lbi, bbr, bbi = _s5prep(s5_lam_re[0], s5_lam_im[0], s5_log_dt[0], s5_b_re[0], s5_b_im[0])
    bmat = jnp.concatenate([_block_diag(bbr), _block_diag(bbi)], axis=-1).astype(BF16)
    cre = _block_diag(jnp.transpose(s5_c_re[0], (0, 2, 1))).astype(BF16)
    cim = _block_diag(jnp.transpose(-s5_c_im[0], (0, 2, 1))).astype(BF16)
    wts = dict(
        n1w=norm1_w, w_in=w_in[0].astype(BF16), rnw=ret_norm_w, bmat=bmat, cre=cre, cim=cim,
        lbr=lbr.reshape(1, SSM_CH), lbi=lbi.reshape(1, SSM_CH), dsk=s5_d[0].reshape(1, SSM_WIDTH),
        w_glu=w_glu[0].astype(BF16), b_glu=b_glu, snw=s5_norm_w, w_out=w_out[0].astype(BF16),
    )

    zero_states = (jnp.zeros((bp, RET_HEADS, HEAD_DIM, HEAD_DIM), F32), jnp.zeros((bp, SSM_CH), F32),
                   jnp.zeros((bp, SSM_CH), F32))
    x1_p, ret_p, re_p, im_p = _mixer(x_prompt, mod_p, np.arange(lp, dtype=np.float32), zero_states, wts,
                                     prompt=True)
    sample_states = (state_ret[0], state_s5_re[0].reshape(bs, SSM_CH), state_s5_im[0].reshape(bs, SSM_CH))
    x1_s, ret_s, re_s, im_s = _mixer(x_sample.reshape(n_s, D_MODEL), mod_s,
                                     PAST_LEN + np.arange(ls, dtype=np.float32), sample_states, wts, prompt=False)

    x1_p_rows = x1_p.reshape(n_p, D_MODEL)
    h2, route, topw, counts = _router(x1_p_rows, mod_p, lp, x1_s, mod_s, norm2_w, w_router[0], b_router)

    n_assign = n_tok * TOP_K
    gather_quantum = SC_GATHER_WINDOW * SC_WORKERS // PACK_ROWS
    assert n_assign % gather_quantum == 0
    n_slots = _round_up(_round_up(n_assign, SLOT_TILE) + N_EXPERTS * SLOT_TILE, gather_quantum)
    slot_kt, tile_e, tile_rows, next_e, n_valid = _routing_tables(route, counts[0, :N_EXPERTS].astype(jnp.int32),
                                                                  n_slots)
    xs = _dispatch_packed(h2, slot_kt, n_slots)
    ys = _experts(tile_e, tile_rows, next_e, n_valid, xs, w1[0], b1[0], w2[0], b2[0])
    y4 = _gather_packed(ys, slot_kt.reshape(-1))
    y4 = y4.reshape(PACK_ROWS, TOP_K, n_tok, LANES)

    fw = final_w.reshape(1, D_MODEL)
    y_p = _combine(y4, x1_p_rows, topw, mod_p, fw, 0, lp)
    y_s = _combine(y4, x1_s, topw, mod_s, fw, n_p, 1)

    g, p = SSM_GROUPS, SSM_STATE
    return (y_p.reshape(bp, lp, D_MODEL), y_s.reshape(bs, ls, D_MODEL),
            ret_p[None], re_p.reshape(1, bp, g, p), im_p.reshape(1, bp, g, p),
            ret_s[None], re_s.reshape(1, bs, g, p), im_s.reshape(1, bs, g, p))
```

```python
import functools
import math

import jax
import jax.numpy as jnp
import numpy as np
from jax import lax
from jax.experimental import pallas as pl
from jax.experimental.pallas import tpu as pltpu
from jax.experimental.pallas import tpu_sc as plsc

F32 = jnp.float32
BF16 = jnp.bfloat16
HIGHEST = lax.Precision.HIGHEST

D_MODEL = 1024
PAST_LEN = 16384
RET_WIDTH = 512
RET_HEADS = 4
HEAD_DIM = 128
ROPE_BASE = 10000.0
SSM_WIDTH = 512
SSM_GROUP = 16
SSM_GROUPS = 32
SSM_STATE = 64
SSM_CH = SSM_GROUPS * SSM_STATE
IN_WIDTH = 4 * RET_WIDTH + SSM_WIDTH
N_EXPERTS = 32
TOP_K = 4
D_FF = 1024
SWIGLU_LIMIT = 7.0
SWIGLU_ALPHA = 1.702
NORM_EPS = 1e-6

LANES = 128
SUBLANES = 8
VMEM_LIMIT = 56 * 1024 * 1024

SEQ_PER_BLOCK = 8
PROMPT_CHUNK = 64
S5_BLOCK_GROUPS = 8
N_S5_BLOCKS = SSM_GROUPS // S5_BLOCK_GROUPS
S5_BLOCK_IN = S5_BLOCK_GROUPS * SSM_GROUP
S5_BLOCK_CH = S5_BLOCK_GROUPS * SSM_STATE
ROUTER_TILE = 512
SLOT_TILE = 512
EXPERT_ROWS = 256
COMBINE_TILE = 256
SC_GATHER_WINDOW = 128
SC_WORKERS = 32


def _silu(x):
    return x * jax.nn.sigmoid(x)


def _ada_kernel(c_ref, w_ref, b_ref, o_ref):
    s = _silu(c_ref[...])
    o_ref[...] = jnp.dot(s, w_ref[...], precision=HIGHEST, preferred_element_type=F32) + b_ref[...]


def _ada(c_all, w_ada, b_ada):
    n_rows, n_out = c_all.shape[0], w_ada.shape[1]
    tn = 1536
    return pl.pallas_call(
        _ada_kernel,
        grid=(n_out // tn,),
        in_specs=[
            pl.BlockSpec((n_rows, D_MODEL), lambda j: (0, 0)),
            pl.BlockSpec((D_MODEL, tn), lambda j: (0, j)),
            pl.BlockSpec((1, tn), lambda j: (0, j)),
        ],
        out_specs=pl.BlockSpec((n_rows, tn), lambda j: (0, j)),
        out_shape=jax.ShapeDtypeStruct((n_rows, n_out), F32),
        compiler_params=pltpu.CompilerParams(dimension_semantics=("arbitrary",), vmem_limit_bytes=VMEM_LIMIT),
        name="ada",
    )(c_all, w_ada, b_ada.reshape(1, n_out))


def _s5prep_kernel(lre_ref, lim_ref, ldt_ref, bre_ref, bim_ref, lbr_ref, lbi_ref, bbr_ref, bbi_ref):
    lam_re, lam_im = lre_ref[...], lim_ref[...]
    dt = jnp.exp(ldt_ref[...])
    mag = jnp.exp(lam_re * dt)
    ang = lam_im * dt
    lb_re, lb_im = mag * jnp.cos(ang), mag * jnp.sin(ang)
    den = lam_re * lam_re + lam_im * lam_im
    f_re = ((lb_re - 1.0) * lam_re + lb_im * lam_im) / den
    f_im = (lb_im * lam_re - (lb_re - 1.0) * lam_im) / den
    lbr_ref[...] = lb_re
    lbi_ref[...] = lb_im
    b_re, b_im = bre_ref[...], bim_ref[...]
    bbr_ref[...] = f_re[:, None, :] * b_re - f_im[:, None, :] * b_im
    bbi_ref[...] = f_re[:, None, :] * b_im + f_im[:, None, :] * b_re


def _s5prep(lam_re, lam_im, log_dt, b_re, b_im):
    g, p = lam_re.shape
    bt_re = jnp.transpose(b_re, (0, 2, 1))
    bt_im = jnp.transpose(b_im, (0, 2, 1))
    return pl.pallas_call(
        _s5prep_kernel,
        out_shape=(
            jax.ShapeDtypeStruct((g, p), F32), jax.ShapeDtypeStruct((g, p), F32),
            jax.ShapeDtypeStruct((g, SSM_GROUP, p), F32), jax.ShapeDtypeStruct((g, SSM_GROUP, p), F32),
        ),
        name="s5prep",
    )(lam_re, lam_im, log_dt.reshape(g, 1), bt_re, bt_im)


def _block_diag(blocks):
    _, r, c = blocks.shape
    b4 = blocks.reshape(N_S5_BLOCKS, S5_BLOCK_GROUPS, r, c)
    eye = jnp.eye(S5_BLOCK_GROUPS, dtype=blocks.dtype)
    out = b4[:, :, :, None, :] * eye[None, :, None, :, None]
    return out.reshape(N_S5_BLOCKS, S5_BLOCK_GROUPS * r, S5_BLOCK_GROUPS * c)


def _mixer_kernel(x_ref, mod_ref, n1w_ref, win_ref, cos_ref, sin_ref, dmask_ref, cdec_ref, sdec_ref,
                  rnw_ref, bmat_ref, cre_ref, cim_ref, lbr_ref, lbi_ref, dsk_ref, wglu_ref, bglu_ref,
                  snw_ref, wout_ref, sret0_ref, sre0_ref, sim0_ref,
                  x1_ref, sret_ref, sre_ref, sim_ref,
                  hb_ref, z_ref, zu_ref, oy_ref, utb_ref, bur_ref, bui_ref, ytb_ref, yb_ref,
                  *, n_seq, chunk, tile_rows, carry, chunk_decay):
    rows = n_seq * chunk
    seq_per_tile = tile_rows // chunk
    n_tiles = rows // tile_rows
    per_row_mod = mod_ref.shape[0] == rows

    def load_states():
        sret_ref[...] = sret0_ref[...]
        sre_ref[...] = sre0_ref[...]
        sim_ref[...] = sim0_ref[...]

    if carry:
        pl.when(pl.program_id(0) == 0)(load_states)
    else:
        load_states()

    n1w = n1w_ref[...]
    mod_rows = rows if per_row_mod else chunk
    for i in range(rows // mod_rows):
        r0 = i * mod_rows
        xb = _load_rows(x_ref, r0, mod_rows, chunk)
        if per_row_mod:
            sh = mod_ref[pl.ds(r0, mod_rows), pl.ds(0, D_MODEL)]
            sc = mod_ref[pl.ds(r0, mod_rows), pl.ds(D_MODEL, D_MODEL)]
        else:
            sh = mod_ref[pl.ds(i, 1), pl.ds(0, D_MODEL)]
            sc = mod_ref[pl.ds(i, 1), pl.ds(D_MODEL, D_MODEL)]
        ms = jnp.mean(xb * xb, axis=-1, keepdims=True)
        hn = xb * lax.rsqrt(ms + NORM_EPS) * n1w
        hb_ref[pl.ds(r0, mod_rows), :] = (hn * (1.0 + sc) + sh).astype(BF16)
    ret_w = 4 * RET_WIDTH
    z_ref[...] = jnp.dot(hb_ref[...], win_ref[:, pl.ds(0, ret_w)], preferred_element_type=F32)
    zu = jnp.dot(hb_ref[...], win_ref[:, pl.ds(ret_w, SSM_WIDTH)], preferred_element_type=F32)
    pitch = zu_ref.shape[1] // n_seq
    for c in range(SSM_WIDTH // LANES):
        for b in range(n_seq if pitch != chunk else 1):
            nb = chunk if pitch != chunk else rows
            zu_ref[c, pl.ds(b * pitch, nb), :] = zu[b * chunk:b * chunk + nb, c * LANES:(c + 1) * LANES]

    cos = cos_ref[...]
    sin = sin_ref[...]
    scale = HEAD_DIM ** -0.5
    if seq_per_tile > 1:
        row_id = lax.broadcasted_iota(jnp.int32, (tile_rows, HEAD_DIM), 0)

    def rope(t):
        return t * cos + pltpu.roll(t, HEAD_DIM // 2, 1) * sin

    def ret_tile(ti, c):
        r0 = pl.multiple_of(ti * tile_rows, tile_rows)
        for h in range(RET_HEADS):
            c0 = h * HEAD_DIM
            q = rope(z_ref[pl.ds(r0, tile_rows), pl.ds(c0, HEAD_DIM)])
            k = rope(z_ref[pl.ds(r0, tile_rows), pl.ds(RET_WIDTH + c0, HEAD_DIM)]) * scale
            v = z_ref[pl.ds(r0, tile_rows), pl.ds(2 * RET_WIDTH + c0, HEAD_DIM)]
            g = z_ref[pl.ds(r0, tile_rows), pl.ds(3 * RET_WIDTH + c0, HEAD_DIM)]
            kd = k * sdec_ref[h]
            if tile_rows < HEAD_DIM:
                pad = jnp.zeros((HEAD_DIM - tile_rows, HEAD_DIM), F32)
                k, v, kd = (jnp.concatenate([t, pad], axis=0) for t in (k, v, kd))
                if seq_per_tile > 1:
                    row_kv = lax.broadcasted_iota(jnp.int32, (HEAD_DIM, HEAD_DIM), 0)
            elif seq_per_tile > 1:
                row_kv = row_id
            qb, kb, vb = q.astype(BF16), k.astype(BF16), v.astype(BF16)
            s = lax.dot_general(qb, kb, (((1,), (1,)), ((), ())), preferred_element_type=F32) * dmask_ref[h]
            o = jnp.dot(s.astype(BF16), vb, preferred_element_type=F32)
            cross = None
            for si in range(seq_per_tile):
                sidx = ti * seq_per_tile + si
                st = sret_ref[sidx, h]
                cr = jnp.dot(qb, st.astype(BF16), preferred_element_type=F32)
                if seq_per_tile > 1:
                    in_seq = (row_id >= si * chunk) & (row_id < (si + 1) * chunk)
                    cross = jnp.where(in_seq, cr, 0.0 if cross is None else cross)
                    kds = jnp.where((row_kv >= si * chunk) & (row_kv < (si + 1) * chunk), kd, 0.0)
                else:
                    cross, kds = cr, kd
                upd = lax.dot_general(kds.astype(BF16), vb, (((0,), (0,)), ((), ())), preferred_element_type=F32)
                sret_ref[sidx, h] = st * chunk_decay[h] + upd
            o = o + cross * cdec_ref[h]
            o = o * lax.rsqrt(jnp.mean(o * o, axis=-1, keepdims=True) + NORM_EPS)
            o = o * rnw_ref[:, pl.ds(c0, HEAD_DIM)] * _silu(g)
            oy_ref[pl.ds(r0, tile_rows), pl.ds(c0, HEAD_DIM)] = o
        return c

    lax.fori_loop(0, n_tiles, ret_tile, 0, unroll=True)

    for t in range(chunk):
        for c in range(SSM_WIDTH // LANES):
            utb_ref[pl.ds(t * n_seq, n_seq), pl.ds(c * LANES, LANES)] = zu_ref[c, pl.ds(t, n_seq, stride=pitch), :]
    for blk in range(N_S5_BLOCKS):
        ub = utb_ref[:, pl.ds(blk * S5_BLOCK_IN, S5_BLOCK_IN)].astype(BF16)
        bu = jnp.dot(ub, bmat_ref[blk], preferred_element_type=F32)
        bur_ref[:, pl.ds(blk * S5_BLOCK_CH, S5_BLOCK_CH)] = bu[:, :S5_BLOCK_CH]
        bui_ref[:, pl.ds(blk * S5_BLOCK_CH, S5_BLOCK_CH)] = bu[:, S5_BLOCK_CH:]

    scan_w = 2 * S5_BLOCK_CH
    for p in range(SSM_CH // scan_w):
        cols = pl.ds(p * scan_w, scan_w)
        lbr = jnp.broadcast_to(lbr_ref[:, cols], (n_seq, scan_w))
        lbi = jnp.broadcast_to(lbi_ref[:, cols], (n_seq, scan_w))

        def scan_step(t, hc, cols=cols, lbr=lbr, lbi=lbi):
            hr, hi = hc
            r0 = pl.multiple_of(t * n_seq, n_seq)
            nr = lbr * hr - lbi * hi + bur_ref[pl.ds(r0, n_seq), cols]
            ni = lbr * hi + lbi * hr + bui_ref[pl.ds(r0, n_seq), cols]
            bur_ref[pl.ds(r0, n_seq), cols] = nr
            bui_ref[pl.ds(r0, n_seq), cols] = ni
            return nr, ni

        h0 = (sre_ref[:, cols], sim_ref[:, cols])
        if chunk <= 8:
            hc = h0
            for t in range(chunk):
                hc = scan_step(t, hc)
        else:
            unroll = 4

            def scan_group(tg, hc):
                for j in range(unroll):
                    hc = scan_step(tg * unroll + j, hc)
                return hc

            hc = lax.fori_loop(0, chunk // unroll, scan_group, h0)
        sre_ref[:, cols] = hc[0]
        sim_ref[:, cols] = hc[1]

    for blk in range(N_S5_BLOCKS):
        cols = pl.ds(blk * S5_BLOCK_CH, S5_BLOCK_CH)
        yb = jnp.dot(bur_ref[:, cols].astype(BF16), cre_ref[blk], preferred_element_type=F32)
        yb = yb + jnp.dot(bui_ref[:, cols].astype(BF16), cim_ref[blk], preferred_element_type=F32)
        ucols = pl.ds(blk * S5_BLOCK_IN, S5_BLOCK_IN)
        ytb_ref[:, ucols] = yb + dsk_ref[:, ucols] * utb_ref[:, ucols]
    for t in range(chunk):
        for c in range(SSM_WIDTH // LANES):
            yb_ref[c, pl.ds(t, n_seq, stride=pitch), :] = ytb_ref[pl.ds(t * n_seq, n_seq), pl.ds(c * LANES, LANES)]

    def seq_major(c):
        if pitch == chunk:
            return yb_ref[c]
        return jnp.concatenate([yb_ref[c, pl.ds(b * pitch, chunk), :] for b in range(n_seq)], axis=0)

    y = jnp.concatenate([seq_major(c) for c in range(SSM_WIDTH // LANES)], axis=1)
    y = jax.nn.gelu(y, approximate=True)
    gate = jnp.dot(y.astype(BF16), wglu_ref[...], preferred_element_type=F32) + bglu_ref[...]
    y = y * jax.nn.sigmoid(gate)
    y = y * lax.rsqrt(jnp.mean(y * y, axis=-1, keepdims=True) + NORM_EPS) * snw_ref[...]
    oy_ref[:, pl.ds(RET_WIDTH, SSM_WIDTH)] = y

    mix = jnp.dot(oy_ref[...].astype(BF16), wout_ref[...], preferred_element_type=F32)
    for i in range(rows // mod_rows):
        r0 = i * mod_rows
        if per_row_mod:
            g1 = mod_ref[pl.ds(r0, mod_rows), pl.ds(2 * D_MODEL, D_MODEL)]
        else:
            g1 = mod_ref[pl.ds(i, 1), pl.ds(2 * D_MODEL, D_MODEL)]
        _store_rows(x1_ref, r0, mod_rows, chunk,
                    _load_rows(x_ref, r0, mod_rows, chunk) + g1 * mix[r0:r0 + mod_rows])


def _load_rows(ref, r0, n, chunk):
    if len(ref.shape) == 2:
        return ref[pl.ds(r0, n), :]
    assert n == chunk and r0 % chunk == 0
    return ref[r0 // chunk]


def _store_rows(ref, r0, n, chunk, val):
    if len(ref.shape) == 2:
        ref[pl.ds(r0, n), :] = val
    else:
        assert n == chunk and r0 % chunk == 0
        ref[r0 // chunk] = val


def _seq_pitch(chunk):
    return chunk + SUBLANES if chunk % SUBLANES == 0 else chunk


def _const_spec(shape):
    nd = len(shape)
    return pl.BlockSpec(shape, lambda j, _n=nd: (0,) * _n)


def _decay_tables(chunk, tile_rows):
    f32 = np.float32
    log_gamma = np.log1p(-np.exp2(f32(-5.0) - np.arange(RET_HEADS, dtype=f32))).astype(f32)
    r = np.arange(tile_rows)
    seq, loc = r // chunk, (r % chunk).astype(f32)
    rel = loc[:, None] - loc[None, :]
    ok = (seq[:, None] == seq[None, :]) & (rel >= 0)
    dmask = np.where(ok[None], np.exp(np.where(ok, rel, f32(0.0))[None] * log_gamma[:, None, None]), f32(0.0))
    if tile_rows < HEAD_DIM:
        dmask = np.pad(dmask, ((0, 0), (0, 0), (0, HEAD_DIM - tile_rows)))
    cdec = np.exp((loc[None, :] + f32(1.0)) * log_gamma[:, None])
    sdec = np.exp((f32(chunk) - f32(1.0) - loc)[None, :] * log_gamma[:, None])
    bcast = lambda t: np.ascontiguousarray(np.broadcast_to(t[:, :, None], (RET_HEADS, tile_rows, HEAD_DIM)))
    return dmask.astype(f32), bcast(cdec.astype(f32)), bcast(sdec.astype(f32))


def _rope_tables(pos):
    f32 = np.float32
    half = HEAD_DIM // 2
    inv_freq = (f32(ROPE_BASE) ** (-np.arange(half, dtype=f32) / f32(half))).astype(f32)
    ang = (pos.astype(f32)[:, None] * inv_freq[None, :]).astype(f32)
    cos, sin = np.cos(ang).astype(f32), np.sin(ang).astype(f32)
    return np.concatenate([cos, cos], axis=-1), np.concatenate([-sin, sin], axis=-1)


def _mixer(x, mod, pos, states, wts, *, prompt):
    n_seq = SEQ_PER_BLOCK
    if prompt:
        n_total, seq_len, _ = x.shape
        assert n_total == n_seq
        chunk, tile_rows, n_steps = PROMPT_CHUNK, PROMPT_CHUNK, seq_len // PROMPT_CHUNK
        x_spec = pl.BlockSpec((n_seq, chunk, D_MODEL), lambda j: (0, j, 0))
        mod_spec = pl.BlockSpec((n_seq, 3 * D_MODEL), lambda j: (0, 0))
        tab_spec = pl.BlockSpec((chunk, HEAD_DIM), lambda j: (j, 0))
        seq_map = lambda j: 0
    else:
        chunk = pos.shape[0]
        tile_rows = SUBLANES
        n_total = x.shape[0] // chunk
        n_steps = n_total // n_seq
        x_spec = pl.BlockSpec((n_seq * chunk, D_MODEL), lambda j: (j, 0))
        mod_spec = pl.BlockSpec((n_seq * chunk, 3 * D_MODEL), lambda j: (j, 0))
        tab_spec = _const_spec((tile_rows, HEAD_DIM))
        seq_map = lambda j: j
    rows = n_seq * chunk
    cos, sin = _rope_tables(pos)
    if not prompt:
        reps = tile_rows // chunk
        cos, sin = np.tile(cos, (reps, 1)), np.tile(sin, (reps, 1))
    dmask, cdec, sdec = _decay_tables(chunk, tile_rows)
    chunk_decay = tuple(float(math.exp(chunk * math.log1p(-2.0 ** (-5.0 - h)))) for h in range(RET_HEADS))
    sret0, sre0, sim0 = states

    st_ret_spec = pl.BlockSpec((n_seq, RET_HEADS, HEAD_DIM, HEAD_DIM), lambda j: (seq_map(j), 0, 0, 0))
    st_s5_spec = pl.BlockSpec((n_seq, SSM_CH), lambda j: (seq_map(j), 0))
    consts = [dmask, cdec, sdec, wts["rnw"], wts["bmat"], wts["cre"], wts["cim"], wts["lbr"], wts["lbi"],
              wts["dsk"], wts["w_glu"], wts["b_glu"], wts["snw"], wts["w_out"]]
    args = [x, mod, wts["n1w"], wts["w_in"], cos, sin] + consts + [sret0, sre0, sim0]
    in_specs = ([x_spec, mod_spec, _const_spec(wts["n1w"].shape), _const_spec(wts["w_in"].shape), tab_spec, tab_spec]
                + [_const_spec(a.shape) for a in consts] + [st_ret_spec, st_s5_spec, st_s5_spec])

    kern = functools.partial(_mixer_kernel, n_seq=n_seq, chunk=chunk, tile_rows=tile_rows, carry=prompt,
                             chunk_decay=chunk_decay)
    out_shape = (
        jax.ShapeDtypeStruct(x.shape, F32),
        jax.ShapeDtypeStruct((n_total, RET_HEADS, HEAD_DIM, HEAD_DIM), F32),
        jax.ShapeDtypeStruct((n_total, SSM_CH), F32),
        jax.ShapeDtypeStruct((n_total, SSM_CH), F32),
    )
    scratch = [
        pltpu.VMEM((rows, D_MODEL), BF16),
        pltpu.VMEM((rows, 4 * RET_WIDTH), F32),
        pltpu.VMEM((SSM_WIDTH // LANES, n_seq * _seq_pitch(chunk), LANES), F32),
        pltpu.VMEM((rows, D_MODEL), F32),
        pltpu.VMEM((rows, SSM_WIDTH), F32),
        pltpu.VMEM((rows, SSM_CH), F32),
        pltpu.VMEM((rows, SSM_CH), F32),
        pltpu.VMEM((rows, SSM_WIDTH), F32),
        pltpu.VMEM((SSM_WIDTH // LANES, n_seq * _seq_pitch(chunk), LANES), F32),
    ]
    return pl.pallas_call(
        kern,
        grid=(n_steps,),
        in_specs=in_specs,
        out_specs=(x_spec, st_ret_spec, st_s5_spec, st_s5_spec),
        out_shape=out_shape,
        scratch_shapes=scratch,
        compiler_params=pltpu.CompilerParams(dimension_semantics=("arbitrary",), vmem_limit_bytes=VMEM_LIMIT),
        name="mixer_prompt" if prompt else "mixer_sample",
    )(*args)


PACK_ROWS = D_MODEL // (2 * LANES)


def _store_packed(ref, x):
    half = D_MODEL // 2
    bits = lax.bitcast_convert_type(x.astype(BF16).astype(F32), jnp.uint32)
    words = bits[:, :half] | (bits[:, half:] >> 16)
    for c in range(PACK_ROWS):
        ref[c] = words[:, c * LANES:(c + 1) * LANES]


def _load_packed(ref, n, first_row=0, row_stride=1):
    hi, lo = [], []
    for c in range(PACK_ROWS):
        w = ref[c] if row_stride == 1 else ref[c, pl.ds(first_row, n, stride=row_stride), :]
        hi.append(lax.bitcast_convert_type(w & jnp.uint32(0xFFFF0000), F32))
        lo.append(lax.bitcast_convert_type(w << 16, F32))
    return jnp.concatenate(hi + lo, axis=1)


def _split_bf16(x):
    hi = x.astype(BF16)
    return hi, (x - hi.astype(F32)).astype(BF16)


def _route_tile(x, sh, sc, n2w_ref, wrh_ref, wrm_ref, br_ref, ltri_ref, count_ref, h2_ref, route_ref, topw_ref):
    ms = jnp.mean(x * x, axis=-1, keepdims=True)
    h2 = x * lax.rsqrt(ms + NORM_EPS) * n2w_ref[...] * (1.0 + sc) + sh
    _store_packed(h2_ref, h2)
    hh, hm = _split_bf16(h2)
    logits = (jnp.dot(hh, wrh_ref[...], preferred_element_type=F32)
              + (jnp.dot(hh, wrm_ref[...], preferred_element_type=F32)
                 + jnp.dot(hm, wrh_ref[...], preferred_element_type=F32))) + br_ref[...]
    lane = lax.broadcasted_iota(jnp.int32, logits.shape, 1)
    work = logits
    vals, idxs = [], []
    for _ in range(TOP_K):
        m = jnp.max(work, axis=-1, keepdims=True)
        idx = jnp.min(jnp.where(work == m, lane, LANES), axis=-1, keepdims=True)
        vals.append(m)
        idxs.append(idx)
        work = jnp.where(lane == idx, -jnp.inf, work)
    exps = [jnp.exp(v - vals[0]) for v in vals]
    tot = exps[0] + exps[1] + exps[2] + exps[3]
    topw = jnp.zeros(logits.shape, F32)
    for k in range(TOP_K):
        topw = jnp.where(lane == k, exps[k] / tot, topw)
    topw_ref[...] = topw

    onehot = [(lane == idxs[k]).astype(F32) for k in range(TOP_K)]
    chosen = onehot[0] + onehot[1] + onehot[2] + onehot[3]
    before = jnp.dot(ltri_ref[...], chosen.astype(BF16), preferred_element_type=F32) + count_ref[...]
    info = jnp.zeros(logits.shape, jnp.int32)
    for k in range(TOP_K):
        rank = jnp.sum(onehot[k] * before, axis=-1, keepdims=True).astype(jnp.int32)
        info = jnp.where(lane == k, idxs[k], info)
        info = jnp.where(lane == TOP_K + k, rank, info)
    route_ref[...] = jnp.transpose(info)[:2 * TOP_K, :]
    count_ref[...] = count_ref[...] + jnp.sum(chosen, axis=0, keepdims=True)


def _router_kernel(xp_ref, shp_ref, scp_ref, xs_ref, shs_ref, scs_ref, n2w_ref, wrh_ref, wrm_ref, br_ref, ltri_ref,
                   h2_ref, route_ref, topw_ref, count_ref, *, n_prompt_tiles):
    i = pl.program_id(0)
    rest = (n2w_ref, wrh_ref, wrm_ref, br_ref, ltri_ref, count_ref, h2_ref, route_ref, topw_ref)

    @pl.when(i == 0)
    def _():
        count_ref[...] = jnp.zeros(count_ref.shape, F32)

    @pl.when(i < n_prompt_tiles)
    def _():
        _route_tile(xp_ref[...], shp_ref[0], scp_ref[0], *rest)

    @pl.when(i >= n_prompt_tiles)
    def _():
        _route_tile(xs_ref[...], shs_ref[...], scs_ref[...], *rest)


def _router(xp_rows, mod_p, seq_len, xs_rows, mod_s, n2w, w_router, b_router):
    tile = ROUTER_TILE
    n_p, n_s = xp_rows.shape[0], xs_rows.shape[0]
    tp, ts = n_p // tile, n_s // tile
    n_total = n_p + n_s
    mod_p = mod_p.reshape(mod_p.shape[0], 1, mod_p.shape[1])
    seq_of = lambda i: (jnp.minimum(i, tp - 1) * tile) // seq_len
    wr_pad = jnp.pad(w_router, ((0, 0), (0, LANES - N_EXPERTS)))
    wr_hi = wr_pad.astype(BF16)
    wr_mid = (wr_pad - wr_hi.astype(F32)).astype(BF16)
    br_pad = jnp.pad(b_router, ((0, 0), (0, LANES - N_EXPERTS)), constant_values=-1e30)
    ltri = jnp.asarray(np.tril(np.ones((tile, tile), np.float32), -1), BF16)
    clamp_p = lambda i: jnp.minimum(i, tp - 1)
    clamp_s = lambda i: jnp.maximum(i - tp, 0)
    return pl.pallas_call(
        functools.partial(_router_kernel, n_prompt_tiles=tp),
        grid=(tp + ts,),
        in_specs=[pl.BlockSpec((tile, D_MODEL), lambda i: (clamp_p(i), 0)),
                  pl.BlockSpec((1, 1, D_MODEL), lambda i: (seq_of(i), 0, 3)),
                  pl.BlockSpec((1, 1, D_MODEL), lambda i: (seq_of(i), 0, 4)),
                  pl.BlockSpec((tile, D_MODEL), lambda i: (clamp_s(i), 0)),
                  pl.BlockSpec((tile, D_MODEL), lambda i: (clamp_s(i), 3)),
                  pl.BlockSpec((tile, D_MODEL), lambda i: (clamp_s(i), 4)),
                  _const_spec(n2w.shape), _const_spec(wr_hi.shape), _const_spec(wr_mid.shape),
                  _const_spec(br_pad.shape), _const_spec(ltri.shape)],
        out_specs=(pl.BlockSpec((PACK_ROWS, tile, LANES), lambda i: (0, i, 0)),
                   pl.BlockSpec((2 * TOP_K, tile), lambda i: (0, i)),
                   pl.BlockSpec((tile, LANES), lambda i: (i, 0)),
                   pl.BlockSpec((1, LANES), lambda i: (0, 0))),
        out_shape=(jax.ShapeDtypeStruct((PACK_ROWS, n_total, LANES), jnp.uint32),
                   jax.ShapeDtypeStruct((2 * TOP_K, n_total), jnp.int32),
                   jax.ShapeDtypeStruct((n_total, LANES), F32),
                   jax.ShapeDtypeStruct((1, LANES), F32)),
        compiler_params=pltpu.CompilerParams(dimension_semantics=("arbitrary",), vmem_limit_bytes=VMEM_LIMIT),
        name="router",
    )(xp_rows, mod_p, mod_p, xs_rows, mod_s, mod_s, n2w, wr_hi, wr_mid, br_pad, ltri)


def _gather_rows(table, idx):
    n = idx.shape[0]
    steps = n // SC_GATHER_WINDOW
    assert n % SC_GATHER_WINDOW == 0 and steps % SC_WORKERS == 0
    mesh = plsc.VectorSubcoreMesh(core_axis_name="c", subcore_axis_name="s")

    @functools.partial(pl.kernel, out_type=jax.ShapeDtypeStruct((n, table.shape[1]), table.dtype), mesh=mesh,
                       scratch_types=[])
    def gather_kernel(table_hbm, idx_hbm, out_hbm):
        def body(idx_vmem, out_vmem):
            pltpu.sync_copy(table_hbm.at[idx_vmem.at[0]], out_vmem)

        pltpu.emit_pipeline(
            body,
            grid=(steps,),
            in_specs=[pl.BlockSpec((1, SC_GATHER_WINDOW), lambda i: (0, i))],
            out_specs=[pl.BlockSpec((SC_GATHER_WINDOW, table.shape[1]), lambda i: (i, 0))],
            core_axis_name=("c", "s"),
            dimension_semantics=(pltpu.PARALLEL,),
        )(idx_hbm, out_hbm)

    return gather_kernel(table, idx.reshape(1, n))


def _scatter_rows(table, idx, src_block, n_out):
    n = idx.shape[0]
    steps = n // SC_GATHER_WINDOW
    assert n % SC_GATHER_WINDOW == 0 and steps % SC_WORKERS == 0
    mesh = plsc.VectorSubcoreMesh(core_axis_name="c", subcore_axis_name="s")

    @functools.partial(pl.kernel, out_type=jax.ShapeDtypeStruct((n_out, table.shape[1]), table.dtype), mesh=mesh,
                       scratch_types=[])
    def scatter_kernel(table_hbm, idx_hbm, out_hbm):
        def body(rows_vmem, idx_vmem):
            pltpu.sync_copy(rows_vmem, out_hbm.at[idx_vmem.at[0]])

        pltpu.emit_pipeline(
            body,
            grid=(steps,),
            in_specs=[pl.BlockSpec((SC_GATHER_WINDOW, table.shape[1]), lambda g: (src_block(g), 0)),
                      pl.BlockSpec((1, SC_GATHER_WINDOW), lambda g: (0, g))],
            out_specs=[],
            core_axis_name=("c", "s"),
            dimension_semantics=(pltpu.PARALLEL,),
        )(table_hbm, idx_hbm)

    return scatter_kernel(table, idx.reshape(1, n))


def _dispatch_packed(table, slot_kt, n_slots):
    planes, n_tok, lanes = table.shape
    blocks = n_tok // SC_GATHER_WINDOW
    idx = slot_kt[None, :, :] + (jnp.arange(planes, dtype=jnp.int32) * n_slots)[:, None, None]
    src_block = lambda g: (g // (TOP_K * blocks)) * blocks + g % blocks
    out = _scatter_rows(table.reshape(planes * n_tok, lanes), idx.reshape(-1), src_block, planes * n_slots)
    return out.reshape(planes, n_slots, lanes)


def _gather_packed(table, rows):
    planes, n_table, lanes = table.shape
    idx = jnp.concatenate([rows + c * n_table for c in range(planes)])
    out = _gather_rows(table.reshape(planes * n_table, lanes), idx)
    return out.reshape(planes, rows.shape[0], lanes)


def _expert_weight_copies(e, w1_hbm, w2_hbm, w1s_ref, w2s_ref, sem):
    return (pltpu.make_async_copy(w1_hbm.at[e], w1s_ref, sem.at[0]),
            pltpu.make_async_copy(w2_hbm.at[e], w2s_ref, sem.at[1]))


def _experts_kernel(te_ref, tr_ref, nx_ref, nv_ref, xs_ref, w1_hbm, b1_ref, w2_hbm, b2_ref, ys_ref,
                    w1s_ref, w2s_ref, w1b_ref, w2b_ref, sem):
    i = pl.program_id(0)
    e = te_ref[i]
    new_expert = (i == 0) | (e != te_ref[jnp.maximum(i - 1, 0)])
    copies = functools.partial(_expert_weight_copies, w1_hbm=w1_hbm, w2_hbm=w2_hbm, w1s_ref=w1s_ref,
                               w2s_ref=w2s_ref, sem=sem)

    @pl.when(i == 0)
    def _():
        for c in copies(e):
            c.start()

    @pl.when(new_expert)
    def _():
        for c in copies(e):
            c.wait()
        w1b_ref[...] = w1s_ref[...].astype(BF16)
        w2b_ref[...] = w2s_ref[...].astype(BF16)

        @pl.when(nx_ref[i] >= 0)
        def _():
            for c in copies(nx_ref[i]):
                c.start()

    for h in range(SLOT_TILE // EXPERT_ROWS):
        rows_here = jnp.where(i < nv_ref[0], tr_ref[i] - h * EXPERT_ROWS, 0)
        xs_h = xs_ref.at[:, pl.ds(h * EXPERT_ROWS, EXPERT_ROWS)]
        ys_h = ys_ref.at[:, pl.ds(h * EXPERT_ROWS, EXPERT_ROWS)]

        @pl.when(rows_here > 0)
        def _(rows_here=rows_here, xs_h=xs_h, ys_h=ys_h):
            row = lax.broadcasted_iota(jnp.int32, (EXPERT_ROWS, D_MODEL), 0)
            x = jnp.where(row < rows_here, _load_packed(xs_h, EXPERT_ROWS), 0.0).astype(BF16)
            hu = jnp.dot(x, w1b_ref[...], preferred_element_type=F32) + b1_ref[0]
            x_glu = jnp.minimum(hu[:, :D_FF], SWIGLU_LIMIT)
            x_lin = jnp.clip(hu[:, D_FF:], -SWIGLU_LIMIT, SWIGLU_LIMIT)
            act = x_glu * jax.nn.sigmoid(SWIGLU_ALPHA * x_glu) * (x_lin + 1.0)
            _store_packed(ys_h, jnp.dot(act.astype(BF16), w2b_ref[...], preferred_element_type=F32) + b2_ref[0])

        @pl.when(rows_here <= 0)
        def _(ys_h=ys_h):
            for c in range(PACK_ROWS):
                ys_h[c] = jnp.zeros((EXPERT_ROWS, LANES), jnp.uint32)


def _experts(tile_expert, tile_rows, next_expert, n_valid, xs, w1, b1, w2, b2):
    n_slots = xs.shape[1]
    n_tiles = n_slots // SLOT_TILE
    grid_spec = pltpu.PrefetchScalarGridSpec(
        num_scalar_prefetch=4,
        grid=(n_tiles,),
        in_specs=[
            pl.BlockSpec((PACK_ROWS, SLOT_TILE, LANES), lambda i, te, tr, nx, nv: (0, i, 0)),
            pl.BlockSpec(memory_space=pl.ANY),
            pl.BlockSpec((1, 1, 2 * D_FF), lambda i, te, tr, nx, nv: (te[i], 0, 0)),
            pl.BlockSpec(memory_space=pl.ANY),
            pl.BlockSpec((1, 1, D_MODEL), lambda i, te, tr, nx, nv: (te[i], 0, 0)),
        ],
        out_specs=pl.BlockSpec((PACK_ROWS, SLOT_TILE, LANES), lambda i, te, tr, nx, nv: (0, i, 0)),
        scratch_shapes=[pltpu.VMEM((D_MODEL, 2 * D_FF), F32), pltpu.VMEM((D_FF, D_MODEL), F32),
                        pltpu.VMEM((D_MODEL, 2 * D_FF), BF16), pltpu.VMEM((D_FF, D_MODEL), BF16),
                        pltpu.SemaphoreType.DMA((2,))],
    )
    return pl.pallas_call(
        _experts_kernel,
        grid_spec=grid_spec,
        out_shape=jax.ShapeDtypeStruct((PACK_ROWS, n_slots, LANES), jnp.uint32),
        compiler_params=pltpu.CompilerParams(dimension_semantics=("arbitrary",), vmem_limit_bytes=VMEM_LIMIT),
        name="experts",
    )(tile_expert, tile_rows, next_expert, n_valid, xs, w1, b1.reshape(N_EXPERTS, 1, 2 * D_FF), w2,
      b2.reshape(N_EXPERTS, 1, D_MODEL))


def _combine_kernel(y4_ref, x1_ref, topw_ref, g2_ref, fw_ref, *rest, per_row_mod):
    o_ref = rest[-1]
    w = topw_ref[...]
    n = w.shape[0]
    ff = None
    for k in range(TOP_K):
        yk = w[:, k:k + 1] * _load_packed(y4_ref.at[:, k], n)
        ff = yk if ff is None else ff + yk
    g2 = g2_ref[...] if per_row_mod else g2_ref[0]
    x = x1_ref[...] + g2 * ff
    ms = jnp.mean(x * x, axis=-1, keepdims=True)
    o_ref[...] = x * lax.rsqrt(ms + NORM_EPS) * fw_ref[...]


def _combine(y4, y4_row0, x1, x1_row0, n_rows, topw, topw_row0, mod, rows_per_mod, fw, out_buf):
    tile = COMBINE_TILE
    y4_off, x1_off, tw_off = y4_row0 // tile, x1_row0 // tile, topw_row0 // tile
    per_row = rows_per_mod == 1
    if per_row:
        g2_spec = pl.BlockSpec((tile, D_MODEL), lambda i: (i + x1_off, 5))
    else:
        mod = mod.reshape(mod.shape[0], 1, mod.shape[1])
        g2_spec = pl.BlockSpec((1, 1, D_MODEL), lambda i: (((i + x1_off) * tile) // rows_per_mod, 0, 5))
    in_specs = [pl.BlockSpec((PACK_ROWS, TOP_K, tile, LANES), lambda i: (0, 0, i + y4_off, 0)),
                pl.BlockSpec((tile, D_MODEL), lambda i: (i + x1_off, 0)),
                pl.BlockSpec((tile, LANES), lambda i: (i + tw_off, 0)),
                g2_spec, _const_spec(fw.shape)]
    args = [y4, x1, topw, mod, fw]
    aliases = {}
    if out_buf is not None:
        in_specs.append(pl.BlockSpec(memory_space=pl.ANY))
        args.append(out_buf)
        aliases = {len(args) - 1: 0}
    return pl.pallas_call(
        functools.partial(_combine_kernel, per_row_mod=per_row),
        grid=(n_rows // tile,),
        in_specs=in_specs,
        out_specs=pl.BlockSpec((tile, D_MODEL), lambda i: (i + x1_off, 0)),
        out_shape=jax.ShapeDtypeStruct(x1.shape, F32),
        input_output_aliases=aliases,
        compiler_params=pltpu.CompilerParams(dimension_semantics=("arbitrary",), vmem_limit_bytes=VMEM_LIMIT),
        name="combine",
    )(*args)


def _routing_tables(route, counts, n_slots):
    padded = ((counts + SLOT_TILE - 1) // SLOT_TILE) * SLOT_TILE
    pend = jnp.cumsum(padded)
    poff = pend - padded
    expert_kt, rank_kt = route[:TOP_K], route[TOP_K:]
    experts = jnp.arange(N_EXPERTS, dtype=jnp.int32)
    start_kt = jnp.sum((expert_kt[None] == experts[:, None, None]).astype(jnp.int32) * poff[:, None, None], axis=0)
    slot_kt = start_kt + rank_kt
    n_tiles = n_slots // SLOT_TILE
    n_valid = (pend[-1] // SLOT_TILE).astype(jnp.int32)
    tile_row = jnp.minimum(jnp.arange(n_tiles, dtype=jnp.int32), n_valid - 1) * SLOT_TILE
    in_later = (pend[None, :] <= tile_row[:, None]).astype(jnp.int32)
    tile_e = jnp.sum(in_later, axis=1).astype(jnp.int32)
    is_e = (experts[None, :] == tile_e[:, None]).astype(jnp.int32)
    used_end = jnp.sum(is_e * (poff + counts)[None, :], axis=1)
    tile_rows = jnp.clip(used_end - tile_row, 0, SLOT_TILE).astype(jnp.int32)
    later_used = (experts[None, :] > tile_e[:, None]) & (counts[None, :] > 0)
    next_e = jnp.min(jnp.where(later_used, experts[None, :], N_EXPERTS), axis=1)
    next_e = jnp.where(next_e < N_EXPERTS, next_e, -1).astype(jnp.int32)
    return slot_kt, tile_e, tile_rows, next_e, n_valid.reshape(1)


def _round_up(n, m):
    return ((n + m - 1) // m) * m


def kernel(x_prompt, x_sample, c_prompt, c_sample, state_ret, state_s5_re, state_s5_im, norm1_w, norm2_w, w_ada, b_ada, w_in, ret_norm_w, s5_lam_re, s5_lam_im, s5_log_dt, s5_b_re, s5_b_im, s5_c_re, s5_c_im, s5_d, w_glu, b_glu, s5_norm_w, w_out, w_router, b_router, w1, b1, w2, b2, final_w):
    bp, lp, _ = x_prompt.shape
    bs, ls, _ = x_sample.shape
    assert norm1_w.shape[0] == 1, "single-layer model"
    n_p, n_s = bp * lp, bs * ls
    n_tok = n_p + n_s

    mod = _ada(jnp.concatenate([c_prompt, c_sample], axis=0), w_ada[0], b_ada[0])
    mod_p, mod_s = mod[:bp], jnp.repeat(mod[bp:], ls, axis=0)

    lbr, lbi, bbr, bbi = _s5prep(s5_lam_re[0], s5_lam_im[0], s5_log_dt[0], s5_b_re[0], s5_b_im[0])
    bmat = jnp.concatenate([_block_diag(bbr), _block_diag(bbi)], axis=-1).astype(BF16)
    cre = _block_diag(jnp.transpose(s5_c_re[0], (0, 2, 1))).astype(BF16)
    cim = _block_diag(jnp.transpose(-s5_c_im[0], (0, 2, 1))).astype(BF16)
    wts = dict(
        n1w=norm1_w, w_in=w_in[0].astype(BF16), rnw=ret_norm_w, bmat=bmat, cre=cre, cim=cim,
        lbr=lbr.reshape(1, SSM_CH), lbi=lbi.reshape(1, SSM_CH), dsk=s5_d[0].reshape(1, SSM_WIDTH),
        w_glu=w_glu[0].astype(BF16), b_glu=b_glu, snw=s5_norm_w, w_out=w_out[0].astype(BF16),
    )

    zero_states = (jnp.zeros((bp, RET_HEADS, HEAD_DIM, HEAD_DIM), F32), jnp.zeros((bp, SSM_CH), F32),
                   jnp.zeros((bp, SSM_CH), F32))
    x1_p, ret_p, re_p, im_p = _mixer(x_prompt, mod_p, np.arange(lp, dtype=np.float32), zero_states, wts,
                                     prompt=True)
    sample_states = (state_ret[0], state_s5_re[0].reshape(bs, SSM_CH), state_s5_im[0].reshape(bs, SSM_CH))
    x1_s, ret_s, re_s, im_s = _mixer(x_sample.reshape(n_s, D_MODEL), mod_s,
                                     PAST_LEN + np.arange(ls, dtype=np.float32), sample_states, wts, prompt=False)

    x1_p_rows = x1_p.reshape(n_p, D_MODEL)
    h2, route, topw, counts = _router(x1_p_rows, mod_p, lp, x1_s, mod_s, norm2_w, w_router[0], b_router)

    n_assign = n_tok * TOP_K
    gather_quantum = SC_GATHER_WINDOW * SC_WORKERS // PACK_ROWS
    assert n_assign % gather_quantum == 0
    n_slots = _round_up(_round_up(n_assign, SLOT_TILE) + N_EXPERTS * SLOT_TILE, gather_quantum)
    slot_kt, tile_e, tile_rows, next_e, n_valid = _routing_tables(route, counts[0, :N_EXPERTS].astype(jnp.int32),
                                                                  n_slots)
    xs = _dispatch_packed(h2, slot_kt, n_slots)
    ys = _experts(tile_e, tile_rows, next_e, n_valid, xs, w1[0], b1[0], w2[0], b2[0])
    fw = final_w.reshape(1, D_MODEL)
    y_p, y_s = None, None
    for lo, hi in ((0, n_p // 2), (n_p // 2, n_tok)):
        y4 = _gather_packed(ys, slot_kt[:, lo:hi].reshape(-1)).reshape(PACK_ROWS, TOP_K, hi - lo, LANES)
        y_p = _combine(y4, 0, x1_p_rows, lo, min(hi, n_p) - lo, topw, lo, mod_p, lp, fw, y_p)
        if hi > n_p:
            y_s = _combine(y4, n_p - lo, x1_s, 0, n_s, topw, n_p, mod_s, 1, fw, None)

    g, p = SSM_GROUPS, SSM_STATE
    return (y_p.reshape(bp, lp, D_MODEL), y_s.reshape(bs, ls, D_MODEL),
            ret_p[None], re_p.reshape(1, bp, g, p), im_p.reshape(1, bp, g, p),
            ret_s[None], re_s.reshape(1, bs, g, p), im_s.reshape(1, bs, g, p))
```

```python
import functools
import math

import jax
import jax.numpy as jnp
import numpy as np
from jax import lax
from jax.experimental import pallas as pl
from jax.experimental.pallas import tpu as pltpu
from jax.experimental.pallas import tpu_sc as plsc

F32 = jnp.float32
BF16 = jnp.bfloat16
HIGHEST = lax.Precision.HIGHEST

D_MODEL = 1024
PAST_LEN = 16384
RET_WIDTH = 512
RET_HEADS = 4
HEAD_DIM = 128
ROPE_BASE = 10000.0
SSM_WIDTH = 512
SSM_GROUP = 16
SSM_GROUPS = 32
SSM_STATE = 64
SSM_CH = SSM_GROUPS * SSM_STATE
IN_WIDTH = 4 * RET_WIDTH + SSM_WIDTH
N_EXPERTS = 32
TOP_K = 4
D_FF = 1024
SWIGLU_LIMIT = 7.0
SWIGLU_ALPHA = 1.702
NORM_EPS = 1e-6

LANES = 128
SUBLANES = 8
VMEM_LIMIT = 56 * 1024 * 1024

SEQ_PER_BLOCK = 8
PROMPT_CHUNK = 64
S5_BLOCK_GROUPS = 8
N_S5_BLOCKS = SSM_GROUPS // S5_BLOCK_GROUPS
S5_BLOCK_IN = S5_BLOCK_GROUPS * SSM_GROUP
S5_BLOCK_CH = S5_BLOCK_GROUPS * SSM_STATE
ROUTER_TILE = 512
SLOT_TILE = 1024
EXPERT_ROWS = 256
COMBINE_TILE = 256
COMBINE_RANGES = 4
SC_GATHER_WINDOW = 128
SC_WORKERS = 32


def _silu(x):
    return x * jax.nn.sigmoid(x)


def _ada_kernel(c_ref, w_ref, b_ref, o_ref):
    s = _silu(c_ref[...])
    o_ref[...] = jnp.dot(s, w_ref[...], precision=HIGHEST, preferred_element_type=F32) + b_ref[...]


def _ada(c_all, w_ada, b_ada):
    n_rows, n_out = c_all.shape[0], w_ada.shape[1]
    tn = 1536
    return pl.pallas_call(
        _ada_kernel,
        grid=(n_out // tn,),
        in_specs=[
            pl.BlockSpec((n_rows, D_MODEL), lambda j: (0, 0)),
            pl.BlockSpec((D_MODEL, tn), lambda j: (0, j)),
            pl.BlockSpec((1, tn), lambda j: (0, j)),
        ],
        out_specs=pl.BlockSpec((n_rows, tn), lambda j: (0, j)),
        out_shape=jax.ShapeDtypeStruct((n_rows, n_out), F32),
        compiler_params=pltpu.CompilerParams(dimension_semantics=("arbitrary",), vmem_limit_bytes=VMEM_LIMIT),
        name="ada",
    )(c_all, w_ada, b_ada.reshape(1, n_out))


def _s5prep_kernel(lre_ref, lim_ref, ldt_ref, bre_ref, bim_ref, lbr_ref, lbi_ref, bbr_ref, bbi_ref):
    lam_re, lam_im = lre_ref[...], lim_ref[...]
    dt = jnp.exp(ldt_ref[...])
    mag = jnp.exp(lam_re * dt)
    ang = lam_im * dt
    lb_re, lb_im = mag * jnp.cos(ang), mag * jnp.sin(ang)
    den = lam_re * lam_re + lam_im * lam_im
    f_re = ((lb_re - 1.0) * lam_re + lb_im * lam_im) / den
    f_im = (lb_im * lam_re - (lb_re - 1.0) * lam_im) / den
    lbr_ref[...] = lb_re
    lbi_ref[...] = lb_im
    b_re, b_im = bre_ref[...], bim_ref[...]
    bbr_ref[...] = f_re[:, None, :] * b_re - f_im[:, None, :] * b_im
    bbi_ref[...] = f_re[:, None, :] * b_im + f_im[:, None, :] * b_re


def _s5prep(lam_re, lam_im, log_dt, b_re, b_im):
    g, p = lam_re.shape
    bt_re = jnp.transpose(b_re, (0, 2, 1))
    bt_im = jnp.transpose(b_im, (0, 2, 1))
    return pl.pallas_call(
        _s5prep_kernel,
        out_shape=(
            jax.ShapeDtypeStruct((g, p), F32), jax.ShapeDtypeStruct((g, p), F32),
            jax.ShapeDtypeStruct((g, SSM_GROUP, p), F32), jax.ShapeDtypeStruct((g, SSM_GROUP, p), F32),
        ),
        name="s5prep",
    )(lam_re, lam_im, log_dt.reshape(g, 1), bt_re, bt_im)


def _block_diag(blocks):
    _, r, c = blocks.shape
    b4 = blocks.reshape(N_S5_BLOCKS, S5_BLOCK_GROUPS, r, c)
    eye = jnp.eye(S5_BLOCK_GROUPS, dtype=blocks.dtype)
    out = b4[:, :, :, None, :] * eye[None, :, None, :, None]
    return out.reshape(N_S5_BLOCKS, S5_BLOCK_GROUPS * r, S5_BLOCK_GROUPS * c)


def _mixer_kernel(x_ref, mod_ref, n1w_ref, win_ref, cos_ref, sin_ref, dmask_ref, cdec_ref, sdec_ref,
                  rnw_ref, bmat_ref, cre_ref, cim_ref, lbr_ref, lbi_ref, dsk_ref, wglu_ref, bglu_ref,
                  snw_ref, wout_ref, sret0_ref, sre0_ref, sim0_ref,
                  x1_ref, sret_ref, sre_ref, sim_ref,
                  hb_ref, z_ref, zu_ref, oy_ref, utb_ref, bur_ref, bui_ref, ytb_ref, yb_ref,
                  *, n_seq, chunk, tile_rows, carry, chunk_decay):
    rows = n_seq * chunk
    seq_per_tile = tile_rows // chunk
    n_tiles = rows // tile_rows
    per_row_mod = mod_ref.shape[0] == rows

    def load_states():
        sret_ref[...] = sret0_ref[...]
        sre_ref[...] = sre0_ref[...]
        sim_ref[...] = sim0_ref[...]

    if carry:
        pl.when(pl.program_id(0) == 0)(load_states)
    else:
        load_states()

    n1w = n1w_ref[...]
    mod_rows = rows if per_row_mod else chunk
    for i in range(rows // mod_rows):
        r0 = i * mod_rows
        xb = _load_rows(x_ref, r0, mod_rows, chunk)
        if per_row_mod:
            sh = mod_ref[pl.ds(r0, mod_rows), pl.ds(0, D_MODEL)]
            sc = mod_ref[pl.ds(r0, mod_rows), pl.ds(D_MODEL, D_MODEL)]
        else:
            sh = mod_ref[pl.ds(i, 1), pl.ds(0, D_MODEL)]
            sc = mod_ref[pl.ds(i, 1), pl.ds(D_MODEL, D_MODEL)]
        ms = jnp.mean(xb * xb, axis=-1, keepdims=True)
        hn = xb * lax.rsqrt(ms + NORM_EPS) * n1w
        hb_ref[pl.ds(r0, mod_rows), :] = (hn * (1.0 + sc) + sh).astype(BF16)
    ret_w = 4 * RET_WIDTH
    z_ref[...] = jnp.dot(hb_ref[...], win_ref[:, pl.ds(0, ret_w)], preferred_element_type=F32)
    zu = jnp.dot(hb_ref[...], win_ref[:, pl.ds(ret_w, SSM_WIDTH)], preferred_element_type=F32)
    pitch = zu_ref.shape[1] // n_seq
    for c in range(SSM_WIDTH // LANES):
        for b in range(n_seq if pitch != chunk else 1):
            nb = chunk if pitch != chunk else rows
            zu_ref[c, pl.ds(b * pitch, nb), :] = zu[b * chunk:b * chunk + nb, c * LANES:(c + 1) * LANES]

    cos = cos_ref[...]
    sin = sin_ref[...]
    scale = HEAD_DIM ** -0.5
    if seq_per_tile > 1:
        row_id = lax.broadcasted_iota(jnp.int32, (tile_rows, HEAD_DIM), 0)

    def rope(t):
        return t * cos + pltpu.roll(t, HEAD_DIM // 2, 1) * sin

    def ret_tile(ti, c):
        r0 = pl.multiple_of(ti * tile_rows, tile_rows)
        for h in range(RET_HEADS):
            c0 = h * HEAD_DIM
            q = rope(z_ref[pl.ds(r0, tile_rows), pl.ds(c0, HEAD_DIM)])
            k = rope(z_ref[pl.ds(r0, tile_rows), pl.ds(RET_WIDTH + c0, HEAD_DIM)]) * scale
            v = z_ref[pl.ds(r0, tile_rows), pl.ds(2 * RET_WIDTH + c0, HEAD_DIM)]
            g = z_ref[pl.ds(r0, tile_rows), pl.ds(3 * RET_WIDTH + c0, HEAD_DIM)]
            kd = k * sdec_ref[h]
            if tile_rows < HEAD_DIM:
                pad = jnp.zeros((HEAD_DIM - tile_rows, HEAD_DIM), F32)
                k, v, kd = (jnp.concatenate([t, pad], axis=0) for t in (k, v, kd))
                if seq_per_tile > 1:
                    row_kv = lax.broadcasted_iota(jnp.int32, (HEAD_DIM, HEAD_DIM), 0)
            elif seq_per_tile > 1:
                row_kv = row_id
            qb, kb, vb = q.astype(BF16), k.astype(BF16), v.astype(BF16)
            s = lax.dot_general(qb, kb, (((1,), (1,)), ((), ())), preferred_element_type=F32) * dmask_ref[h]
            o = jnp.dot(s.astype(BF16), vb, preferred_element_type=F32)
            cross = None
            for si in range(seq_per_tile):
                sidx = ti * seq_per_tile + si
                st = sret_ref[sidx, h]
                cr = jnp.dot(qb, st.astype(BF16), preferred_element_type=F32)
                if seq_per_tile > 1:
                    in_seq = (row_id >= si * chunk) & (row_id < (si + 1) * chunk)
                    cross = jnp.where(in_seq, cr, 0.0 if cross is None else cross)
                    kds = jnp.where((row_kv >= si * chunk) & (row_kv < (si + 1) * chunk), kd, 0.0)
                else:
                    cross, kds = cr, kd
                upd = lax.dot_general(kds.astype(BF16), vb, (((0,), (0,)), ((), ())), preferred_element_type=F32)
                sret_ref[sidx, h] = st * chunk_decay[h] + upd
            o = o + cross * cdec_ref[h]
            o = o * lax.rsqrt(jnp.mean(o * o, axis=-1, keepdims=True) + NORM_EPS)
            o = o * rnw_ref[:, pl.ds(c0, HEAD_DIM)] * _silu(g)
            oy_ref[pl.ds(r0, tile_rows), pl.ds(c0, HEAD_DIM)] = o
        return c

    lax.fori_loop(0, n_tiles, ret_tile, 0, unroll=True)

    for t in range(chunk):
        for c in range(SSM_WIDTH // LANES):
            utb_ref[pl.ds(t * n_seq, n_seq), pl.ds(c * LANES, LANES)] = zu_ref[c, pl.ds(t, n_seq, stride=pitch), :]
    for blk in range(N_S5_BLOCKS):
        ub = utb_ref[:, pl.ds(blk * S5_BLOCK_IN, S5_BLOCK_IN)].astype(BF16)
        bu = jnp.dot(ub, bmat_ref[blk], preferred_element_type=F32)
        bur_ref[:, pl.ds(blk * S5_BLOCK_CH, S5_BLOCK_CH)] = bu[:, :S5_BLOCK_CH]
        bui_ref[:, pl.ds(blk * S5_BLOCK_CH, S5_BLOCK_CH)] = bu[:, S5_BLOCK_CH:]

    scan_w = 2 * S5_BLOCK_CH
    for p in range(SSM_CH // scan_w):
        cols = pl.ds(p * scan_w, scan_w)
        lbr = jnp.broadcast_to(lbr_ref[:, cols], (n_seq, scan_w))
        lbi = jnp.broadcast_to(lbi_ref[:, cols], (n_seq, scan_w))

        def scan_step(t, hc, cols=cols, lbr=lbr, lbi=lbi):
            hr, hi = hc
            r0 = pl.multiple_of(t * n_seq, n_seq)
            nr = lbr * hr - lbi * hi + bur_ref[pl.ds(r0, n_seq), cols]
            ni = lbr * hi + lbi * hr + bui_ref[pl.ds(r0, n_seq), cols]
            bur_ref[pl.ds(r0, n_seq), cols] = nr
            bui_ref[pl.ds(r0, n_seq), cols] = ni
            return nr, ni

        h0 = (sre_ref[:, cols], sim_ref[:, cols])
        if chunk <= 8:
            hc = h0
            for t in range(chunk):
                hc = scan_step(t, hc)
        else:
            unroll = 4

            def scan_group(tg, hc):
                for j in range(unroll):
                    hc = scan_step(tg * unroll + j, hc)
                return hc

            hc = lax.fori_loop(0, chunk // unroll, scan_group, h0)
        sre_ref[:, cols] = hc[0]
        sim_ref[:, cols] = hc[1]

    for blk in range(N_S5_BLOCKS):
        cols = pl.ds(blk * S5_BLOCK_CH, S5_BLOCK_CH)
        yb = jnp.dot(bur_ref[:, cols].astype(BF16), cre_ref[blk], preferred_element_type=F32)
        yb = yb + jnp.dot(bui_ref[:, cols].astype(BF16), cim_ref[blk], preferred_element_type=F32)
        ucols = pl.ds(blk * S5_BLOCK_IN, S5_BLOCK_IN)
        ytb_ref[:, ucols] = yb + dsk_ref[:, ucols] * utb_ref[:, ucols]
    for t in range(chunk):
        for c in range(SSM_WIDTH // LANES):
            yb_ref[c, pl.ds(t, n_seq, stride=pitch), :] = ytb_ref[pl.ds(t * n_seq, n_seq), pl.ds(c * LANES, LANES)]

    def seq_major(c):
        if pitch == chunk:
            return yb_ref[c]
        return jnp.concatenate([yb_ref[c, pl.ds(b * pitch, chunk), :] for b in range(n_seq)], axis=0)

    y = jnp.concatenate([seq_major(c) for c in range(SSM_WIDTH // LANES)], axis=1)
    y = jax.nn.gelu(y, approximate=True)
    gate = jnp.dot(y.astype(BF16), wglu_ref[...], preferred_element_type=F32) + bglu_ref[...]
    y = y * jax.nn.sigmoid(gate)
    y = y * lax.rsqrt(jnp.mean(y * y, axis=-1, keepdims=True) + NORM_EPS) * snw_ref[...]
    oy_ref[:, pl.ds(RET_WIDTH, SSM_WIDTH)] = y

    mix = jnp.dot(oy_ref[...].astype(BF16), wout_ref[...], preferred_element_type=F32)
    for i in range(rows // mod_rows):
        r0 = i * mod_rows
        if per_row_mod:
            g1 = mod_ref[pl.ds(r0, mod_rows), pl.ds(2 * D_MODEL, D_MODEL)]
        else:
            g1 = mod_ref[pl.ds(i, 1), pl.ds(2 * D_MODEL, D_MODEL)]
        _store_rows(x1_ref, r0, mod_rows, chunk,
                    _load_rows(x_ref, r0, mod_rows, chunk) + g1 * mix[r0:r0 + mod_rows])


def _load_rows(ref, r0, n, chunk):
    if len(ref.shape) == 2:
        return ref[pl.ds(r0, n), :]
    assert n == chunk and r0 % chunk == 0
    return ref[r0 // chunk]


def _store_rows(ref, r0, n, chunk, val):
    if len(ref.shape) == 2:
        ref[pl.ds(r0, n), :] = val
    else:
        assert n == chunk and r0 % chunk == 0
        ref[r0 // chunk] = val


def _seq_pitch(chunk):
    return chunk + SUBLANES if chunk % SUBLANES == 0 else chunk


def _const_spec(shape):
    nd = len(shape)
    return pl.BlockSpec(shape, lambda j, _n=nd: (0,) * _n)


def _decay_tables(chunk, tile_rows):
    f32 = np.float32
    log_gamma = np.log1p(-np.exp2(f32(-5.0) - np.arange(RET_HEADS, dtype=f32))).astype(f32)
    r = np.arange(tile_rows)
    seq, loc = r // chunk, (r % chunk).astype(f32)
    rel = loc[:, None] - loc[None, :]
    ok = (seq[:, None] == seq[None, :]) & (rel >= 0)
    dmask = np.where(ok[None], np.exp(np.where(ok, rel, f32(0.0))[None] * log_gamma[:, None, None]), f32(0.0))
    if tile_rows < HEAD_DIM:
        dmask = np.pad(dmask, ((0, 0), (0, 0), (0, HEAD_DIM - tile_rows)))
    cdec = np.exp((loc[None, :] + f32(1.0)) * log_gamma[:, None])
    sdec = np.exp((f32(chunk) - f32(1.0) - loc)[None, :] * log_gamma[:, None])
    bcast = lambda t: np.ascontiguousarray(np.broadcast_to(t[:, :, None], (RET_HEADS, tile_rows, HEAD_DIM)))
    return dmask.astype(f32), bcast(cdec.astype(f32)), bcast(sdec.astype(f32))


def _rope_tables(pos):
    f32 = np.float32
    half = HEAD_DIM // 2
    inv_freq = (f32(ROPE_BASE) ** (-np.arange(half, dtype=f32) / f32(half))).astype(f32)
    ang = (pos.astype(f32)[:, None] * inv_freq[None, :]).astype(f32)
    cos, sin = np.cos(ang).astype(f32), np.sin(ang).astype(f32)
    return np.concatenate([cos, cos], axis=-1), np.concatenate([-sin, sin], axis=-1)


def _mixer(x, mod, pos, states, wts, *, prompt):
    n_seq = SEQ_PER_BLOCK
    if prompt:
        n_total, seq_len, _ = x.shape
        assert n_total == n_seq
        chunk, tile_rows, n_steps = PROMPT_CHUNK, PROMPT_CHUNK, seq_len // PROMPT_CHUNK
        x_spec = pl.BlockSpec((n_seq, chunk, D_MODEL), lambda j: (0, j, 0))
        mod_spec = pl.BlockSpec((n_seq, 3 * D_MODEL), lambda j: (0, 0))
        tab_spec = pl.BlockSpec((chunk, HEAD_DIM), lambda j: (j, 0))
        seq_map = lambda j: 0
    else:
        chunk = pos.shape[0]
        tile_rows = SUBLANES
        n_total = x.shape[0] // chunk
        n_steps = n_total // n_seq
        x_spec = pl.BlockSpec((n_seq * chunk, D_MODEL), lambda j: (j, 0))
        mod_spec = pl.BlockSpec((n_seq * chunk, 3 * D_MODEL), lambda j: (j, 0))
        tab_spec = _const_spec((tile_rows, HEAD_DIM))
        seq_map = lambda j: j
    rows = n_seq * chunk
    cos, sin = _rope_tables(pos)
    if not prompt:
        reps = tile_rows // chunk
        cos, sin = np.tile(cos, (reps, 1)), np.tile(sin, (reps, 1))
    dmask, cdec, sdec = _decay_tables(chunk, tile_rows)
    chunk_decay = tuple(float(math.exp(chunk * math.log1p(-2.0 ** (-5.0 - h)))) for h in range(RET_HEADS))
    sret0, sre0, sim0 = states

    st_ret_spec = pl.BlockSpec((n_seq, RET_HEADS, HEAD_DIM, HEAD_DIM), lambda j: (seq_map(j), 0, 0, 0))
    st_s5_spec = pl.BlockSpec((n_seq, SSM_CH), lambda j: (seq_map(j), 0))
    consts = [dmask, cdec, sdec, wts["rnw"], wts["bmat"], wts["cre"], wts["cim"], wts["lbr"], wts["lbi"],
              wts["dsk"], wts["w_glu"], wts["b_glu"], wts["snw"], wts["w_out"]]
    args = [x, mod, wts["n1w"], wts["w_in"], cos, sin] + consts + [sret0, sre0, sim0]
    in_specs = ([x_spec, mod_spec, _const_spec(wts["n1w"].shape), _const_spec(wts["w_in"].shape), tab_spec, tab_spec]
                + [_const_spec(a.shape) for a in consts] + [st_ret_spec, st_s5_spec, st_s5_spec])

    kern = functools.partial(_mixer_kernel, n_seq=n_seq, chunk=chunk, tile_rows=tile_rows, carry=prompt,
                             chunk_decay=chunk_decay)
    out_shape = (
        jax.ShapeDtypeStruct(x.shape, F32),
        jax.ShapeDtypeStruct((n_total, RET_HEADS, HEAD_DIM, HEAD_DIM), F32),
        jax.ShapeDtypeStruct((n_total, SSM_CH), F32),
        jax.ShapeDtypeStruct((n_total, SSM_CH), F32),
    )
    scratch = [
        pltpu.VMEM((rows, D_MODEL), BF16),
        pltpu.VMEM((rows, 4 * RET_WIDTH), F32),
        pltpu.VMEM((SSM_WIDTH // LANES, n_seq * _seq_pitch(chunk), LANES), F32),
        pltpu.VMEM((rows, D_MODEL), F32),
        pltpu.VMEM((rows, SSM_WIDTH), F32),
        pltpu.VMEM((rows, SSM_CH), F32),
        pltpu.VMEM((rows, SSM_CH), F32),
        pltpu.VMEM((rows, SSM_WIDTH), F32),
        pltpu.VMEM((SSM_WIDTH // LANES, n_seq * _seq_pitch(chunk), LANES), F32),
    ]
    return pl.pallas_call(
        kern,
        grid=(n_steps,),
        in_specs=in_specs,
        out_specs=(x_spec, st_ret_spec, st_s5_spec, st_s5_spec),
        out_shape=out_shape,
        scratch_shapes=scratch,
        compiler_params=pltpu.CompilerParams(dimension_semantics=("arbitrary",), vmem_limit_bytes=VMEM_LIMIT),
        name="mixer_prompt" if prompt else "mixer_sample",
    )(*args)


PACK_ROWS = D_MODEL // (2 * LANES)


def _store_packed(ref, x):
    half = D_MODEL // 2
    bits = lax.bitcast_convert_type(x.astype(BF16).astype(F32), jnp.uint32)
    words = bits[:, :half] | (bits[:, half:] >> 16)
    for c in range(PACK_ROWS):
        ref[c] = words[:, c * LANES:(c + 1) * LANES]


def _load_packed(ref, n, first_row=0, row_stride=1):
    hi, lo = [], []
    for c in range(PACK_ROWS):
        w = ref[c] if row_stride == 1 else ref[c, pl.ds(first_row, n, stride=row_stride), :]
        hi.append(lax.bitcast_convert_type(w & jnp.uint32(0xFFFF0000), F32))
        lo.append(lax.bitcast_convert_type(w << 16, F32))
    return jnp.concatenate(hi + lo, axis=1)


def _split_bf16(x):
    hi = x.astype(BF16)
    return hi, (x - hi.astype(F32)).astype(BF16)


def _route_tile(x, sh, sc, n2w_ref, wrh_ref, wrm_ref, br_ref, ltri_ref, count_ref, h2_ref, route_ref, topw_ref):
    ms = jnp.mean(x * x, axis=-1, keepdims=True)
    h2 = x * lax.rsqrt(ms + NORM_EPS) * n2w_ref[...] * (1.0 + sc) + sh
    _store_packed(h2_ref, h2)
    hh, hm = _split_bf16(h2)
    logits = (jnp.dot(hh, wrh_ref[...], preferred_element_type=F32)
              + (jnp.dot(hh, wrm_ref[...], preferred_element_type=F32)
                 + jnp.dot(hm, wrh_ref[...], preferred_element_type=F32))) + br_ref[...]
    lane = lax.broadcasted_iota(jnp.int32, logits.shape, 1)
    work = logits
    vals, idxs = [], []
    for _ in range(TOP_K):
        m = jnp.max(work, axis=-1, keepdims=True)
        idx = jnp.min(jnp.where(work == m, lane, LANES), axis=-1, keepdims=True)
        vals.append(m)
        idxs.append(idx)
        work = jnp.where(lane == idx, -jnp.inf, work)
    exps = [jnp.exp(v - vals[0]) for v in vals]
    tot = exps[0] + exps[1] + exps[2] + exps[3]
    topw = jnp.zeros(logits.shape, F32)
    for k in range(TOP_K):
        topw = jnp.where(lane == k, exps[k] / tot, topw)
    topw_ref[...] = topw

    onehot = [(lane == idxs[k]).astype(F32) for k in range(TOP_K)]
    chosen = onehot[0] + onehot[1] + onehot[2] + onehot[3]
    before = jnp.dot(ltri_ref[...], chosen.astype(BF16), preferred_element_type=F32) + count_ref[...]
    info = jnp.zeros(logits.shape, jnp.int32)
    for k in range(TOP_K):
        rank = jnp.sum(onehot[k] * before, axis=-1, keepdims=True).astype(jnp.int32)
        info = jnp.where(lane == k, idxs[k], info)
        info = jnp.where(lane == TOP_K + k, rank, info)
    route_ref[...] = jnp.transpose(info)[:2 * TOP_K, :]
    count_ref[...] = count_ref[...] + jnp.sum(chosen, axis=0, keepdims=True)


def _router_kernel(xp_ref, shp_ref, scp_ref, xs_ref, shs_ref, scs_ref, n2w_ref, wrh_ref, wrm_ref, br_ref, ltri_ref,
                   h2_ref, route_ref, topw_ref, count_ref, *, n_prompt_tiles):
    i = pl.program_id(0)
    rest = (n2w_ref, wrh_ref, wrm_ref, br_ref, ltri_ref, count_ref, h2_ref, route_ref, topw_ref)

    @pl.when(i == 0)
    def _():
        count_ref[...] = jnp.zeros(count_ref.shape, F32)

    @pl.when(i < n_prompt_tiles)
    def _():
        _route_tile(xp_ref[...], shp_ref[0], scp_ref[0], *rest)

    @pl.when(i >= n_prompt_tiles)
    def _():
        _route_tile(xs_ref[...], shs_ref[...], scs_ref[...], *rest)


def _router(xp_rows, mod_p, seq_len, xs_rows, mod_s, n2w, w_router, b_router):
    tile = ROUTER_TILE
    n_p, n_s = xp_rows.shape[0], xs_rows.shape[0]
    tp, ts = n_p // tile, n_s // tile
    n_total = n_p + n_s
    mod_p = mod_p.reshape(mod_p.shape[0], 1, mod_p.shape[1])
    seq_of = lambda i: (jnp.minimum(i, tp - 1) * tile) // seq_len
    wr_pad = jnp.pad(w_router, ((0, 0), (0, LANES - N_EXPERTS)))
    wr_hi = wr_pad.astype(BF16)
    wr_mid = (wr_pad - wr_hi.astype(F32)).astype(BF16)
    br_pad = jnp.pad(b_router, ((0, 0), (0, LANES - N_EXPERTS)), constant_values=-1e30)
    ltri = jnp.asarray(np.tril(np.ones((tile, tile), np.float32), -1), BF16)
    clamp_p = lambda i: jnp.minimum(i, tp - 1)
    clamp_s = lambda i: jnp.maximum(i - tp, 0)
    return pl.pallas_call(
        functools.partial(_router_kernel, n_prompt_tiles=tp),
        grid=(tp + ts,),
        in_specs=[pl.BlockSpec((tile, D_MODEL), lambda i: (clamp_p(i), 0)),
                  pl.BlockSpec((1, 1, D_MODEL), lambda i: (seq_of(i), 0, 3)),
                  pl.BlockSpec((1, 1, D_MODEL), lambda i: (seq_of(i), 0, 4)),
                  pl.BlockSpec((tile, D_MODEL), lambda i: (clamp_s(i), 0)),
                  pl.BlockSpec((tile, D_MODEL), lambda i: (clamp_s(i), 3)),
                  pl.BlockSpec((tile, D_MODEL), lambda i: (clamp_s(i), 4)),
                  _const_spec(n2w.shape), _const_spec(wr_hi.shape), _const_spec(wr_mid.shape),
                  _const_spec(br_pad.shape), _const_spec(ltri.shape)],
        out_specs=(pl.BlockSpec((PACK_ROWS, tile, LANES), lambda i: (0, i, 0)),
                   pl.BlockSpec((2 * TOP_K, tile), lambda i: (0, i)),
                   pl.BlockSpec((tile, LANES), lambda i: (i, 0)),
                   pl.BlockSpec((1, LANES), lambda i: (0, 0))),
        out_shape=(jax.ShapeDtypeStruct((PACK_ROWS, n_total, LANES), jnp.uint32),
                   jax.ShapeDtypeStruct((2 * TOP_K, n_total), jnp.int32),
                   jax.ShapeDtypeStruct((n_total, LANES), F32),
                   jax.ShapeDtypeStruct((1, LANES), F32)),
        compiler_params=pltpu.CompilerParams(dimension_semantics=("arbitrary",), vmem_limit_bytes=VMEM_LIMIT),
        name="router",
    )(xp_rows, mod_p, mod_p, xs_rows, mod_s, mod_s, n2w, wr_hi, wr_mid, br_pad, ltri)


def _gather_rows(table, idx):
    n = idx.shape[0]
    steps = n // SC_GATHER_WINDOW
    assert n % SC_GATHER_WINDOW == 0 and steps % SC_WORKERS == 0
    mesh = plsc.VectorSubcoreMesh(core_axis_name="c", subcore_axis_name="s")

    @functools.partial(pl.kernel, out_type=jax.ShapeDtypeStruct((n, table.shape[1]), table.dtype), mesh=mesh,
                       scratch_types=[])
    def gather_kernel(table_hbm, idx_hbm, out_hbm):
        def body(idx_vmem, out_vmem):
            pltpu.sync_copy(table_hbm.at[idx_vmem.at[0]], out_vmem)

        pltpu.emit_pipeline(
            body,
            grid=(steps,),
            in_specs=[pl.BlockSpec((1, SC_GATHER_WINDOW), lambda i: (0, i))],
            out_specs=[pl.BlockSpec((SC_GATHER_WINDOW, table.shape[1]), lambda i: (i, 0))],
            core_axis_name=("c", "s"),
            dimension_semantics=(pltpu.PARALLEL,),
        )(idx_hbm, out_hbm)

    return gather_kernel(table, idx.reshape(1, n))


def _scatter_rows(table, idx, src_block, n_out):
    n = idx.shape[0]
    steps = n // SC_GATHER_WINDOW
    assert n % SC_GATHER_WINDOW == 0 and steps % SC_WORKERS == 0
    mesh = plsc.VectorSubcoreMesh(core_axis_name="c", subcore_axis_name="s")

    @functools.partial(pl.kernel, out_type=jax.ShapeDtypeStruct((n_out, table.shape[1]), table.dtype), mesh=mesh,
                       scratch_types=[])
    def scatter_kernel(table_hbm, idx_hbm, out_hbm):
        def body(rows_vmem, idx_vmem):
            pltpu.sync_copy(rows_vmem, out_hbm.at[idx_vmem.at[0]])

        pltpu.emit_pipeline(
            body,
            grid=(steps,),
            in_specs=[pl.BlockSpec((SC_GATHER_WINDOW, table.shape[1]), lambda g: (src_block(g), 0)),
                      pl.BlockSpec((1, SC_GATHER_WINDOW), lambda g: (0, g))],
            out_specs=[],
            core_axis_name=("c", "s"),
            dimension_semantics=(pltpu.PARALLEL,),
        )(table_hbm, idx_hbm)

    return scatter_kernel(table, idx.reshape(1, n))


def _dispatch_packed(table, slot_kt, n_slots):
    planes, n_tok, lanes = table.shape
    blocks = n_tok // SC_GATHER_WINDOW
    idx = slot_kt[None, :, :] + (jnp.arange(planes, dtype=jnp.int32) * n_slots)[:, None, None]
    src_block = lambda g: (g // (TOP_K * blocks)) * blocks + g % blocks
    out = _scatter_rows(table.reshape(planes * n_tok, lanes), idx.reshape(-1), src_block, planes * n_slots)
    return out.reshape(planes, n_slots, lanes)


def _gather_packed(table, rows):
    planes, n_table, lanes = table.shape
    idx = jnp.concatenate([rows + c * n_table for c in range(planes)])
    out = _gather_rows(table.reshape(planes * n_table, lanes), idx)
    return out.reshape(planes, rows.shape[0], lanes)


def _expert_weight_copies(e, w1_hbm, w2_hbm, w1s_ref, w2s_ref, sem):
    return (pltpu.make_async_copy(w1_hbm.at[e], w1s_ref, sem.at[0]),
            pltpu.make_async_copy(w2_hbm.at[e], w2s_ref, sem.at[1]))


def _experts_kernel(te_ref, tr_ref, nx_ref, nv_ref, xs_ref, w1_hbm, b1_ref, w2_hbm, b2_ref, ys_ref,
                    w1s_ref, w2s_ref, w1b_ref, w2b_ref, sem):
    i = pl.program_id(0)
    e = te_ref[i]
    new_expert = (i == 0) | (e != te_ref[jnp.maximum(i - 1, 0)])
    copies = functools.partial(_expert_weight_copies, w1_hbm=w1_hbm, w2_hbm=w2_hbm, w1s_ref=w1s_ref,
                               w2s_ref=w2s_ref, sem=sem)

    @pl.when(i == 0)
    def _():
        for c in copies(e):
            c.start()

    @pl.when(new_expert)
    def _():
        for c in copies(e):
            c.wait()
        w1b_ref[...] = w1s_ref[...].astype(BF16)
        w2b_ref[...] = w2s_ref[...].astype(BF16)

        @pl.when(nx_ref[i] >= 0)
        def _():
            for c in copies(nx_ref[i]):
                c.start()

    for h in range(SLOT_TILE // EXPERT_ROWS):
        rows_here = jnp.where(i < nv_ref[0], tr_ref[i] - h * EXPERT_ROWS, 0)
        xs_h = xs_ref.at[:, pl.ds(h * EXPERT_ROWS, EXPERT_ROWS)]
        ys_h = ys_ref.at[:, pl.ds(h * EXPERT_ROWS, EXPERT_ROWS)]

        @pl.when(rows_here > 0)
        def _(rows_here=rows_here, xs_h=xs_h, ys_h=ys_h):
            row = lax.broadcasted_iota(jnp.int32, (EXPERT_ROWS, D_MODEL), 0)
            x = jnp.where(row < rows_here, _load_packed(xs_h, EXPERT_ROWS), 0.0).astype(BF16)
            hu = jnp.dot(x, w1b_ref[...], preferred_element_type=F32) + b1_ref[0]
            x_glu = jnp.minimum(hu[:, :D_FF], SWIGLU_LIMIT)
            x_lin = jnp.clip(hu[:, D_FF:], -SWIGLU_LIMIT, SWIGLU_LIMIT)
            act = x_glu * jax.nn.sigmoid(SWIGLU_ALPHA * x_glu) * (x_lin + 1.0)
            _store_packed(ys_h, jnp.dot(act.astype(BF16), w2b_ref[...], preferred_element_type=F32) + b2_ref[0])

        @pl.when(rows_here <= 0)
        def _(ys_h=ys_h):
            for c in range(PACK_ROWS):
                ys_h[c] = jnp.zeros((EXPERT_ROWS, LANES), jnp.uint32)


def _experts(tile_expert, tile_rows, next_expert, n_valid, xs, w1, b1, w2, b2):
    n_slots = xs.shape[1]
    n_tiles = n_slots // SLOT_TILE
    grid_spec = pltpu.PrefetchScalarGridSpec(
        num_scalar_prefetch=4,
        grid=(n_tiles,),
        in_specs=[
            pl.BlockSpec((PACK_ROWS, SLOT_TILE, LANES), lambda i, te, tr, nx, nv: (0, i, 0)),
            pl.BlockSpec(memory_space=pl.ANY),
            pl.BlockSpec((1, 1, 2 * D_FF), lambda i, te, tr, nx, nv: (te[i], 0, 0)),
            pl.BlockSpec(memory_space=pl.ANY),
            pl.BlockSpec((1, 1, D_MODEL), lambda i, te, tr, nx, nv: (te[i], 0, 0)),
        ],
        out_specs=pl.BlockSpec((PACK_ROWS, SLOT_TILE, LANES), lambda i, te, tr, nx, nv: (0, i, 0)),
        scratch_shapes=[pltpu.VMEM((D_MODEL, 2 * D_FF), F32), pltpu.VMEM((D_FF, D_MODEL), F32),
                        pltpu.VMEM((D_MODEL, 2 * D_FF), BF16), pltpu.VMEM((D_FF, D_MODEL), BF16),
                        pltpu.SemaphoreType.DMA((2,))],
    )
    return pl.pallas_call(
        _experts_kernel,
        grid_spec=grid_spec,
        out_shape=jax.ShapeDtypeStruct((PACK_ROWS, n_slots, LANES), jnp.uint32),
        compiler_params=pltpu.CompilerParams(dimension_semantics=("arbitrary",), vmem_limit_bytes=VMEM_LIMIT),
        name="experts",
    )(tile_expert, tile_rows, next_expert, n_valid, xs, w1, b1.reshape(N_EXPERTS, 1, 2 * D_FF), w2,
      b2.reshape(N_EXPERTS, 1, D_MODEL))


def _combine_kernel(y4_ref, x1_ref, topw_ref, g2_ref, fw_ref, *rest, per_row_mod):
    o_ref = rest[-1]
    w = topw_ref[...]
    n = w.shape[0]
    ff = None
    for k in range(TOP_K):
        yk = w[:, k:k + 1] * _load_packed(y4_ref.at[:, k], n)
        ff = yk if ff is None else ff + yk
    g2 = g2_ref[...] if per_row_mod else g2_ref[0]
    x = x1_ref[...] + g2 * ff
    ms = jnp.mean(x * x, axis=-1, keepdims=True)
    o_ref[...] = x * lax.rsqrt(ms + NORM_EPS) * fw_ref[...]


def _combine(y4, y4_row0, x1, x1_row0, n_rows, topw, topw_row0, mod, rows_per_mod, fw, out_buf):
    tile = COMBINE_TILE
    y4_off, x1_off, tw_off = y4_row0 // tile, x1_row0 // tile, topw_row0 // tile
    per_row = rows_per_mod == 1
    if per_row:
        g2_spec = pl.BlockSpec((tile, D_MODEL), lambda i: (i + x1_off, 5))
    else:
        mod = mod.reshape(mod.shape[0], 1, mod.shape[1])
        g2_spec = pl.BlockSpec((1, 1, D_MODEL), lambda i: (((i + x1_off) * tile) // rows_per_mod, 0, 5))
    in_specs = [pl.BlockSpec((PACK_ROWS, TOP_K, tile, LANES), lambda i: (0, 0, i + y4_off, 0)),
                pl.BlockSpec((tile, D_MODEL), lambda i: (i + x1_off, 0)),
                pl.BlockSpec((tile, LANES), lambda i: (i + tw_off, 0)),
                g2_spec, _const_spec(fw.shape)]
    args = [y4, x1, topw, mod, fw]
    aliases = {}
    if out_buf is not None:
        in_specs.append(pl.BlockSpec(memory_space=pl.ANY))
        args.append(out_buf)
        aliases = {len(args) - 1: 0}
    return pl.pallas_call(
        functools.partial(_combine_kernel, per_row_mod=per_row),
        grid=(n_rows // tile,),
        in_specs=in_specs,
        out_specs=pl.BlockSpec((tile, D_MODEL), lambda i: (i + x1_off, 0)),
        out_shape=jax.ShapeDtypeStruct(x1.shape, F32),
        input_output_aliases=aliases,
        compiler_params=pltpu.CompilerParams(dimension_semantics=("arbitrary",), vmem_limit_bytes=VMEM_LIMIT),
        name="combine",
    )(*args)


def _routing_tables(route, counts, n_slots):
    padded = ((counts + SLOT_TILE - 1) // SLOT_TILE) * SLOT_TILE
    pend = jnp.cumsum(padded)
    poff = pend - padded
    expert_kt, rank_kt = route[:TOP_K], route[TOP_K:]
    experts = jnp.arange(N_EXPERTS, dtype=jnp.int32)
    start_kt = jnp.sum((expert_kt[None] == experts[:, None, None]).astype(jnp.int32) * poff[:, None, None], axis=0)
    slot_kt = start_kt + rank_kt
    n_tiles = n_slots // SLOT_TILE
    n_valid = (pend[-1] // SLOT_TILE).astype(jnp.int32)
    tile_row = jnp.minimum(jnp.arange(n_tiles, dtype=jnp.int32), n_valid - 1) * SLOT_TILE
    in_later = (pend[None, :] <= tile_row[:, None]).astype(jnp.int32)
    tile_e = jnp.sum(in_later, axis=1).astype(jnp.int32)
    is_e = (experts[None, :] == tile_e[:, None]).astype(jnp.int32)
    used_end = jnp.sum(is_e * (poff + counts)[None, :], axis=1)
    tile_rows = jnp.clip(used_end - tile_row, 0, SLOT_TILE).astype(jnp.int32)
    later_used = (experts[None, :] > tile_e[:, None]) & (counts[None, :] > 0)
    next_e = jnp.min(jnp.where(later_used, experts[None, :], N_EXPERTS), axis=1)
    next_e = jnp.where(next_e < N_EXPERTS, next_e, -1).astype(jnp.int32)
    return slot_kt, tile_e, tile_rows, next_e, n_valid.reshape(1)


def _round_up(n, m):
    return ((n + m - 1) // m) * m


def kernel(x_prompt, x_sample, c_prompt, c_sample, state_ret, state_s5_re, state_s5_im, norm1_w, norm2_w, w_ada, b_ada, w_in, ret_norm_w, s5_lam_re, s5_lam_im, s5_log_dt, s5_b_re, s5_b_im, s5_c_re, s5_c_im, s5_d, w_glu, b_glu, s5_norm_w, w_out, w_router, b_router, w1, b1, w2, b2, final_w):
    bp, lp, _ = x_prompt.shape
    bs, ls, _ = x_sample.shape
    assert norm1_w.shape[0] == 1, "single-layer model"
    n_p, n_s = bp * lp, bs * ls
    n_tok = n_p + n_s

    mod = _ada(jnp.concatenate([c_prompt, c_sample], axis=0), w_ada[0], b_ada[0])
    mod_p, mod_s = mod[:bp], jnp.repeat(mod[bp:], ls, axis=0)

    lbr, lbi, bbr, bbi = _s5prep(s5_lam_re[0], s5_lam_im[0], s5_log_dt[0], s5_b_re[0], s5_b_im[0])
    bmat = jnp.concatenate([_block_diag(bbr), _block_diag(bbi)], axis=-1).astype(BF16)
    cre = _block_diag(jnp.transpose(s5_c_re[0], (0, 2, 1))).astype(BF16)
    cim = _block_diag(jnp.transpose(-s5_c_im[0], (0, 2, 1))).astype(BF16)
    wts = dict(
        n1w=norm1_w, w_in=w_in[0].astype(BF16), rnw=ret_norm_w, bmat=bmat, cre=cre, cim=cim,
        lbr=lbr.reshape(1, SSM_CH), lbi=lbi.reshape(1, SSM_CH), dsk=s5_d[0].reshape(1, SSM_WIDTH),
        w_glu=w_glu[0].astype(BF16), b_glu=b_glu, snw=s5_norm_w, w_out=w_out[0].astype(BF16),
    )

    zero_states = (jnp.zeros((bp, RET_HEADS, HEAD_DIM, HEAD_DIM), F32), jnp.zeros((bp, SSM_CH), F32),
                   jnp.zeros((bp, SSM_CH), F32))
    x1_p, ret_p, re_p, im_p = _mixer(x_prompt, mod_p, np.arange(lp, dtype=np.float32), zero_states, wts,
                                     prompt=True)
    sample_states = (state_ret[0], state_s5_re[0].reshape(bs, SSM_CH), state_s5_im[0].reshape(bs, SSM_CH))
    x1_s, ret_s, re_s, im_s = _mixer(x_sample.reshape(n_s, D_MODEL), mod_s,
                                     PAST_LEN + np.arange(ls, dtype=np.float32), sample_states, wts, prompt=False)

    x1_p_rows = x1_p.reshape(n_p, D_MODEL)
    h2, route, topw, counts = _router(x1_p_rows, mod_p, lp, x1_s, mod_s, norm2_w, w_router[0], b_router)

    n_assign = n_tok * TOP_K
    gather_quantum = SC_GATHER_WINDOW * SC_WORKERS // PACK_ROWS
    assert n_assign % gather_quantum == 0
    n_slots = _round_up(_round_up(n_assign, SLOT_TILE) + N_EXPERTS * SLOT_TILE, gather_quantum)
    slot_kt, tile_e, tile_rows, next_e, n_valid = _routing_tables(route, counts[0, :N_EXPERTS].astype(jnp.int32),
                                                                  n_slots)
    xs = _dispatch_packed(h2, slot_kt, n_slots)
    ys = _experts(tile_e, tile_rows, next_e, n_valid, xs, w1[0], b1[0], w2[0], b2[0])
    fw = final_w.reshape(1, D_MODEL)
    y_p, y_s = None, None
    bounds = [r * (n_p // COMBINE_RANGES) for r in range(COMBINE_RANGES)] + [n_tok]
    for lo, hi in zip(bounds[:-1], bounds[1:]):
        y4 = _gather_packed(ys, slot_kt[:, lo:hi].reshape(-1)).reshape(PACK_ROWS, TOP_K, hi - lo, LANES)
        y_p = _combine(y4, 0, x1_p_rows, lo, min(hi, n_p) - lo, topw, lo, mod_p, lp, fw, y_p)
        if hi > n_p:
            y_s = _combine(y4, n_p - lo, x1_s, 0, n_s, topw, n_p, mod_s, 1, fw, None)

    g, p = SSM_GROUPS, SSM_STATE
    return (y_p.reshape(bp, lp, D_MODEL), y_s.reshape(bs, ls, D_MODEL),
            ret_p[None], re_p.reshape(1, bp, g, p), im_p.reshape(1, bp, g, p),
            ret_s[None], re_s.reshape(1, bs, g, p), im_s.reshape(1, bs, g, p))
```

```python
import functools
import math

import jax
import jax.numpy as jnp
import numpy as np
from jax import lax
from jax.experimental import pallas as pl
from jax.experimental.pallas import tpu as pltpu
from jax.experimental.pallas import tpu_sc as plsc

F32 = jnp.float32
BF16 = jnp.bfloat16
HIGHEST = lax.Precision.HIGHEST

D_MODEL = 1024
PAST_LEN = 16384
RET_WIDTH = 512
RET_HEADS = 4
HEAD_DIM = 128
ROPE_BASE = 10000.0
SSM_WIDTH = 512
SSM_GROUP = 16
SSM_GROUPS = 32
SSM_STATE = 64
SSM_CH = SSM_GROUPS * SSM_STATE
IN_WIDTH = 4 * RET_WIDTH + SSM_WIDTH
N_EXPERTS = 32
TOP_K = 4
D_FF = 1024
SWIGLU_LIMIT = 7.0
SWIGLU_ALPHA = 1.702
NORM_EPS = 1e-6

LANES = 128
SUBLANES = 8
VMEM_LIMIT = 56 * 1024 * 1024

SEQ_PER_BLOCK = 8
SAMPLE_SEQ_PER_BLOCK = 16
SCAN_ELEMS = 8 * 1024
PROMPT_CHUNK = 64
S5_BLOCK_GROUPS = 8
N_S5_BLOCKS = SSM_GROUPS // S5_BLOCK_GROUPS
S5_BLOCK_IN = S5_BLOCK_GROUPS * SSM_GROUP
S5_BLOCK_CH = S5_BLOCK_GROUPS * SSM_STATE
ROUTER_TILE = 512
SLOT_TILE = 512
EXPERT_ROWS = 256
COMBINE_TILE = 256
COMBINE_RANGES = 2
SC_GATHER_WINDOW = 128
SC_WORKERS = 32


def _silu(x):
    return x * jax.nn.sigmoid(x)


def _ada_kernel(c_ref, w_ref, b_ref, o_ref):
    s = _silu(c_ref[...])
    o_ref[...] = jnp.dot(s, w_ref[...], precision=HIGHEST, preferred_element_type=F32) + b_ref[...]


def _ada(c_all, w_ada, b_ada):
    n_rows, n_out = c_all.shape[0], w_ada.shape[1]
    tn = 1536
    return pl.pallas_call(
        _ada_kernel,
        grid=(n_out // tn,),
        in_specs=[
            pl.BlockSpec((n_rows, D_MODEL), lambda j: (0, 0)),
            pl.BlockSpec((D_MODEL, tn), lambda j: (0, j)),
            pl.BlockSpec((1, tn), lambda j: (0, j)),
        ],
        out_specs=pl.BlockSpec((n_rows, tn), lambda j: (0, j)),
        out_shape=jax.ShapeDtypeStruct((n_rows, n_out), F32),
        compiler_params=pltpu.CompilerParams(dimension_semantics=("arbitrary",), vmem_limit_bytes=VMEM_LIMIT),
        name="ada",
    )(c_all, w_ada, b_ada.reshape(1, n_out))


def _s5prep_kernel(lre_ref, lim_ref, ldt_ref, bre_ref, bim_ref, lbr_ref, lbi_ref, bbr_ref, bbi_ref):
    lam_re, lam_im = lre_ref[...], lim_ref[...]
    dt = jnp.exp(ldt_ref[...])
    mag = jnp.exp(lam_re * dt)
    ang = lam_im * dt
    lb_re, lb_im = mag * jnp.cos(ang), mag * jnp.sin(ang)
    den = lam_re * lam_re + lam_im * lam_im
    f_re = ((lb_re - 1.0) * lam_re + lb_im * lam_im) / den
    f_im = (lb_im * lam_re - (lb_re - 1.0) * lam_im) / den
    lbr_ref[...] = lb_re
    lbi_ref[...] = lb_im
    b_re, b_im = bre_ref[...], bim_ref[...]
    bbr_ref[...] = f_re[:, None, :] * b_re - f_im[:, None, :] * b_im
    bbi_ref[...] = f_re[:, None, :] * b_im + f_im[:, None, :] * b_re


def _s5prep(lam_re, lam_im, log_dt, b_re, b_im):
    g, p = lam_re.shape
    bt_re = jnp.transpose(b_re, (0, 2, 1))
    bt_im = jnp.transpose(b_im, (0, 2, 1))
    return pl.pallas_call(
        _s5prep_kernel,
        out_shape=(
            jax.ShapeDtypeStruct((g, p), F32), jax.ShapeDtypeStruct((g, p), F32),
            jax.ShapeDtypeStruct((g, SSM_GROUP, p), F32), jax.ShapeDtypeStruct((g, SSM_GROUP, p), F32),
        ),
        name="s5prep",
    )(lam_re, lam_im, log_dt.reshape(g, 1), bt_re, bt_im)


def _block_diag(blocks):
    _, r, c = blocks.shape
    b4 = blocks.reshape(N_S5_BLOCKS, S5_BLOCK_GROUPS, r, c)
    eye = jnp.eye(S5_BLOCK_GROUPS, dtype=blocks.dtype)
    out = b4[:, :, :, None, :] * eye[None, :, None, :, None]
    return out.reshape(N_S5_BLOCKS, S5_BLOCK_GROUPS * r, S5_BLOCK_GROUPS * c)


def _mixer_kernel(x_ref, mod_ref, n1w_ref, win_ref, cos_ref, sin_ref, dmask_ref, cdec_ref, sdec_ref,
                  rnw_ref, bmat_ref, cre_ref, cim_ref, lbr_ref, lbi_ref, dsk_ref, wglu_ref, bglu_ref,
                  snw_ref, wout_ref, sret0_ref, sre0_ref, sim0_ref,
                  x1_ref, sret_ref, sre_ref, sim_ref,
                  hb_ref, z_ref, zu_ref, oy_ref, utb_ref, bur_ref, bui_ref, ytb_ref, yb_ref,
                  *, n_seq, chunk, tile_rows, carry, chunk_decay):
    rows = n_seq * chunk
    seq_per_tile = tile_rows // chunk
    n_tiles = rows // tile_rows
    per_row_mod = mod_ref.shape[0] == rows

    def load_states():
        sret_ref[...] = sret0_ref[...]
        sre_ref[...] = sre0_ref[...]
        sim_ref[...] = sim0_ref[...]

    if carry:
        pl.when(pl.program_id(0) == 0)(load_states)
    else:
        load_states()

    n1w = n1w_ref[...]
    mod_rows = rows if per_row_mod else chunk
    for i in range(rows // mod_rows):
        r0 = i * mod_rows
        xb = _load_rows(x_ref, r0, mod_rows, chunk)
        if per_row_mod:
            sh = mod_ref[pl.ds(r0, mod_rows), pl.ds(0, D_MODEL)]
            sc = mod_ref[pl.ds(r0, mod_rows), pl.ds(D_MODEL, D_MODEL)]
        else:
            sh = mod_ref[pl.ds(i, 1), pl.ds(0, D_MODEL)]
            sc = mod_ref[pl.ds(i, 1), pl.ds(D_MODEL, D_MODEL)]
        ms = jnp.mean(xb * xb, axis=-1, keepdims=True)
        hn = xb * lax.rsqrt(ms + NORM_EPS) * n1w
        hb_ref[pl.ds(r0, mod_rows), :] = (hn * (1.0 + sc) + sh).astype(BF16)
    ret_w = 4 * RET_WIDTH
    z_ref[...] = jnp.dot(hb_ref[...], win_ref[:, pl.ds(0, ret_w)], preferred_element_type=F32)
    zu = jnp.dot(hb_ref[...], win_ref[:, pl.ds(ret_w, SSM_WIDTH)], preferred_element_type=F32)
    pitch = zu_ref.shape[1] // n_seq
    for c in range(SSM_WIDTH // LANES):
        for b in range(n_seq if pitch != chunk else 1):
            nb = chunk if pitch != chunk else rows
            zu_ref[c, pl.ds(b * pitch, nb), :] = zu[b * chunk:b * chunk + nb, c * LANES:(c + 1) * LANES]

    cos = cos_ref[...]
    sin = sin_ref[...]
    scale = HEAD_DIM ** -0.5
    if seq_per_tile > 1:
        row_id = lax.broadcasted_iota(jnp.int32, (tile_rows, HEAD_DIM), 0)

    def rope(t):
        return t * cos + pltpu.roll(t, HEAD_DIM // 2, 1) * sin

    def ret_tile(ti, c):
        r0 = pl.multiple_of(ti * tile_rows, tile_rows)
        for h in range(RET_HEADS):
            c0 = h * HEAD_DIM
            q = rope(z_ref[pl.ds(r0, tile_rows), pl.ds(c0, HEAD_DIM)])
            k = rope(z_ref[pl.ds(r0, tile_rows), pl.ds(RET_WIDTH + c0, HEAD_DIM)]) * scale
            v = z_ref[pl.ds(r0, tile_rows), pl.ds(2 * RET_WIDTH + c0, HEAD_DIM)]
            g = z_ref[pl.ds(r0, tile_rows), pl.ds(3 * RET_WIDTH + c0, HEAD_DIM)]
            kd = k * sdec_ref[h]
            if tile_rows < HEAD_DIM:
                pad = jnp.zeros((HEAD_DIM - tile_rows, HEAD_DIM), F32)
                k, v, kd = (jnp.concatenate([t, pad], axis=0) for t in (k, v, kd))
                if seq_per_tile > 1:
                    row_kv = lax.broadcasted_iota(jnp.int32, (HEAD_DIM, HEAD_DIM), 0)
            elif seq_per_tile > 1:
                row_kv = row_id
            qb, kb, vb = q.astype(BF16), k.astype(BF16), v.astype(BF16)
            s = lax.dot_general(qb, kb, (((1,), (1,)), ((), ())), preferred_element_type=F32) * dmask_ref[h]
            o = jnp.dot(s.astype(BF16), vb, preferred_element_type=F32)
            cross = None
            for si in range(seq_per_tile):
                sidx = ti * seq_per_tile + si
                st = sret_ref[sidx, h]
                cr = jnp.dot(qb, st.astype(BF16), preferred_element_type=F32)
                if seq_per_tile > 1:
                    in_seq = (row_id >= si * chunk) & (row_id < (si + 1) * chunk)
                    cross = jnp.where(in_seq, cr, 0.0 if cross is None else cross)
                    kds = jnp.where((row_kv >= si * chunk) & (row_kv < (si + 1) * chunk), kd, 0.0)
                else:
                    cross, kds = cr, kd
                upd = lax.dot_general(kds.astype(BF16), vb, (((0,), (0,)), ((), ())), preferred_element_type=F32)
                sret_ref[sidx, h] = st * chunk_decay[h] + upd
            o = o + cross * cdec_ref[h]
            o = o * lax.rsqrt(jnp.mean(o * o, axis=-1, keepdims=True) + NORM_EPS)
            o = o * rnw_ref[:, pl.ds(c0, HEAD_DIM)] * _silu(g)
            oy_ref[pl.ds(r0, tile_rows), pl.ds(c0, HEAD_DIM)] = o
        return c

    lax.fori_loop(0, n_tiles, ret_tile, 0, unroll=True)

    for t in range(chunk):
        for c in range(SSM_WIDTH // LANES):
            utb_ref[pl.ds(t * n_seq, n_seq), pl.ds(c * LANES, LANES)] = zu_ref[c, pl.ds(t, n_seq, stride=pitch), :]
    for blk in range(N_S5_BLOCKS):
        ub = utb_ref[:, pl.ds(blk * S5_BLOCK_IN, S5_BLOCK_IN)].astype(BF16)
        bu = jnp.dot(ub, bmat_ref[blk], preferred_element_type=F32)
        bur_ref[:, pl.ds(blk * S5_BLOCK_CH, S5_BLOCK_CH)] = bu[:, :S5_BLOCK_CH]
        bui_ref[:, pl.ds(blk * S5_BLOCK_CH, S5_BLOCK_CH)] = bu[:, S5_BLOCK_CH:]

    scan_w = SCAN_ELEMS // n_seq
    for p in range(SSM_CH // scan_w):
        cols = pl.ds(p * scan_w, scan_w)
        lbr = jnp.broadcast_to(lbr_ref[:, cols], (n_seq, scan_w))
        lbi = jnp.broadcast_to(lbi_ref[:, cols], (n_seq, scan_w))

        def scan_step(t, hc, cols=cols, lbr=lbr, lbi=lbi):
            hr, hi = hc
            r0 = pl.multiple_of(t * n_seq, n_seq)
            nr = lbr * hr - lbi * hi + bur_ref[pl.ds(r0, n_seq), cols]
            ni = lbr * hi + lbi * hr + bui_ref[pl.ds(r0, n_seq), cols]
            bur_ref[pl.ds(r0, n_seq), cols] = nr
            bui_ref[pl.ds(r0, n_seq), cols] = ni
            return nr, ni

        h0 = (sre_ref[:, cols], sim_ref[:, cols])
        if chunk <= 8:
            hc = h0
            for t in range(chunk):
                hc = scan_step(t, hc)
        else:
            unroll = 4

            def scan_group(tg, hc):
                for j in range(unroll):
                    hc = scan_step(tg * unroll + j, hc)
                return hc

            hc = lax.fori_loop(0, chunk // unroll, scan_group, h0)
        sre_ref[:, cols] = hc[0]
        sim_ref[:, cols] = hc[1]

    for blk in range(N_S5_BLOCKS):
        cols = pl.ds(blk * S5_BLOCK_CH, S5_BLOCK_CH)
        yb = jnp.dot(bur_ref[:, cols].astype(BF16), cre_ref[blk], preferred_element_type=F32)
        yb = yb + jnp.dot(bui_ref[:, cols].astype(BF16), cim_ref[blk], preferred_element_type=F32)
        ucols = pl.ds(blk * S5_BLOCK_IN, S5_BLOCK_IN)
        ytb_ref[:, ucols] = yb + dsk_ref[:, ucols] * utb_ref[:, ucols]
    for t in range(chunk):
        for c in range(SSM_WIDTH // LANES):
            yb_ref[c, pl.ds(t, n_seq, stride=pitch), :] = ytb_ref[pl.ds(t * n_seq, n_seq), pl.ds(c * LANES, LANES)]

    def seq_major(c):
        if pitch == chunk:
            return yb_ref[c]
        return jnp.concatenate([yb_ref[c, pl.ds(b * pitch, chunk), :] for b in range(n_seq)], axis=0)

    y = jnp.concatenate([seq_major(c) for c in range(SSM_WIDTH // LANES)], axis=1)
    y = jax.nn.gelu(y, approximate=True)
    gate = jnp.dot(y.astype(BF16), wglu_ref[...], preferred_element_type=F32) + bglu_ref[...]
    y = y * jax.nn.sigmoid(gate)
    y = y * lax.rsqrt(jnp.mean(y * y, axis=-1, keepdims=True) + NORM_EPS) * snw_ref[...]
    oy_ref[:, pl.ds(RET_WIDTH, SSM_WIDTH)] = y

    mix = jnp.dot(oy_ref[...].astype(BF16), wout_ref[...], preferred_element_type=F32)
    for i in range(rows // mod_rows):
        r0 = i * mod_rows
        if per_row_mod:
            g1 = mod_ref[pl.ds(r0, mod_rows), pl.ds(2 * D_MODEL, D_MODEL)]
        else:
            g1 = mod_ref[pl.ds(i, 1), pl.ds(2 * D_MODEL, D_MODEL)]
        _store_rows(x1_ref, r0, mod_rows, chunk,
                    _load_rows(x_ref, r0, mod_rows, chunk) + g1 * mix[r0:r0 + mod_rows])


def _load_rows(ref, r0, n, chunk):
    if len(ref.shape) == 2:
        return ref[pl.ds(r0, n), :]
    assert n == chunk and r0 % chunk == 0
    return ref[r0 // chunk]


def _store_rows(ref, r0, n, chunk, val):
    if len(ref.shape) == 2:
        ref[pl.ds(r0, n), :] = val
    else:
        assert n == chunk and r0 % chunk == 0
        ref[r0 // chunk] = val


def _seq_pitch(chunk):
    return chunk + SUBLANES if chunk % SUBLANES == 0 else chunk


def _const_spec(shape):
    nd = len(shape)
    return pl.BlockSpec(shape, lambda j, _n=nd: (0,) * _n)


def _decay_tables(chunk, tile_rows):
    f32 = np.float32
    log_gamma = np.log1p(-np.exp2(f32(-5.0) - np.arange(RET_HEADS, dtype=f32))).astype(f32)
    r = np.arange(tile_rows)
    seq, loc = r // chunk, (r % chunk).astype(f32)
    rel = loc[:, None] - loc[None, :]
    ok = (seq[:, None] == seq[None, :]) & (rel >= 0)
    dmask = np.where(ok[None], np.exp(np.where(ok, rel, f32(0.0))[None] * log_gamma[:, None, None]), f32(0.0))
    if tile_rows < HEAD_DIM:
        dmask = np.pad(dmask, ((0, 0), (0, 0), (0, HEAD_DIM - tile_rows)))
    cdec = np.exp((loc[None, :] + f32(1.0)) * log_gamma[:, None])
    sdec = np.exp((f32(chunk) - f32(1.0) - loc)[None, :] * log_gamma[:, None])
    bcast = lambda t: np.ascontiguousarray(np.broadcast_to(t[:, :, None], (RET_HEADS, tile_rows, HEAD_DIM)))
    return dmask.astype(f32), bcast(cdec.astype(f32)), bcast(sdec.astype(f32))


def _rope_tables(pos):
    f32 = np.float32
    half = HEAD_DIM // 2
    inv_freq = (f32(ROPE_BASE) ** (-np.arange(half, dtype=f32) / f32(half))).astype(f32)
    ang = (pos.astype(f32)[:, None] * inv_freq[None, :]).astype(f32)
    cos, sin = np.cos(ang).astype(f32), np.sin(ang).astype(f32)
    return np.concatenate([cos, cos], axis=-1), np.concatenate([-sin, sin], axis=-1)


def _mixer(x, mod, pos, states, wts, *, prompt):
    n_seq = SEQ_PER_BLOCK if prompt else SAMPLE_SEQ_PER_BLOCK
    if prompt:
        n_total, seq_len, _ = x.shape
        assert n_total == n_seq
        chunk, tile_rows, n_steps = PROMPT_CHUNK, PROMPT_CHUNK, seq_len // PROMPT_CHUNK
        x_spec = pl.BlockSpec((n_seq, chunk, D_MODEL), lambda j: (0, j, 0))
        mod_spec = pl.BlockSpec((n_seq, 3 * D_MODEL), lambda j: (0, 0))
        tab_spec = pl.BlockSpec((chunk, HEAD_DIM), lambda j: (j, 0))
        seq_map = lambda j: 0
    else:
        chunk = pos.shape[0]
        tile_rows = SUBLANES
        n_total = x.shape[0] // chunk
        n_steps = n_total // n_seq
        x_spec = pl.BlockSpec((n_seq * chunk, D_MODEL), lambda j: (j, 0))
        mod_spec = pl.BlockSpec((n_seq * chunk, 3 * D_MODEL), lambda j: (j, 0))
        tab_spec = _const_spec((tile_rows, HEAD_DIM))
        seq_map = lambda j: j
    rows = n_seq * chunk
    cos, sin = _rope_tables(pos)
    if not prompt:
        reps = tile_rows // chunk
        cos, sin = np.tile(cos, (reps, 1)), np.tile(sin, (reps, 1))
    dmask, cdec, sdec = _decay_tables(chunk, tile_rows)
    chunk_decay = tuple(float(math.exp(chunk * math.log1p(-2.0 ** (-5.0 - h)))) for h in range(RET_HEADS))
    sret0, sre0, sim0 = states

    st_ret_spec = pl.BlockSpec((n_seq, RET_HEADS, HEAD_DIM, HEAD_DIM), lambda j: (seq_map(j), 0, 0, 0))
    st_s5_spec = pl.BlockSpec((n_seq, SSM_CH), lambda j: (seq_map(j), 0))
    consts = [dmask, cdec, sdec, wts["rnw"], wts["bmat"], wts["cre"], wts["cim"], wts["lbr"], wts["lbi"],
              wts["dsk"], wts["w_glu"], wts["b_glu"], wts["snw"], wts["w_out"]]
    args = [x, mod, wts["n1w"], wts["w_in"], cos, sin] + consts + [sret0, sre0, sim0]
    in_specs = ([x_spec, mod_spec, _const_spec(wts["n1w"].shape), _const_spec(wts["w_in"].shape), tab_spec, tab_spec]
                + [_const_spec(a.shape) for a in consts] + [st_ret_spec, st_s5_spec, st_s5_spec])

    kern = functools.partial(_mixer_kernel, n_seq=n_seq, chunk=chunk, tile_rows=tile_rows, carry=prompt,
                             chunk_decay=chunk_decay)
    out_shape = (
        jax.ShapeDtypeStruct(x.shape, F32),
        jax.ShapeDtypeStruct((n_total, RET_HEADS, HEAD_DIM, HEAD_DIM), F32),
        jax.ShapeDtypeStruct((n_total, SSM_CH), F32),
        jax.ShapeDtypeStruct((n_total, SSM_CH), F32),
    )
    scratch = [
        pltpu.VMEM((rows, D_MODEL), BF16),
        pltpu.VMEM((rows, 4 * RET_WIDTH), F32),
        pltpu.VMEM((SSM_WIDTH // LANES, n_seq * _seq_pitch(chunk), LANES), F32),
        pltpu.VMEM((rows, D_MODEL), F32),
        pltpu.VMEM((rows, SSM_WIDTH), F32),
        pltpu.VMEM((rows, SSM_CH), F32),
        pltpu.VMEM((rows, SSM_CH), F32),
        pltpu.VMEM((rows, SSM_WIDTH), F32),
        pltpu.VMEM((SSM_WIDTH // LANES, n_seq * _seq_pitch(chunk), LANES), F32),
    ]
    return pl.pallas_call(
        kern,
        grid=(n_steps,),
        in_specs=in_specs,
        out_specs=(x_spec, st_ret_spec, st_s5_spec, st_s5_spec),
        out_shape=out_shape,
        scratch_shapes=scratch,
        compiler_params=pltpu.CompilerParams(dimension_semantics=("arbitrary",), vmem_limit_bytes=VMEM_LIMIT),
        name="mixer_prompt" if prompt else "mixer_sample",
    )(*args)


PACK_ROWS = D_MODEL // (2 * LANES)


def _store_packed(ref, x):
    half = D_MODEL // 2
    bits = lax.bitcast_convert_type(x.astype(BF16).astype(F32), jnp.uint32)
    words = bits[:, :half] | (bits[:, half:] >> 16)
    for c in range(PACK_ROWS):
        ref[c] = words[:, c * LANES:(c + 1) * LANES]


def _load_packed(ref, n, first_row=0, row_stride=1):
    hi, lo = [], []
    for c in range(PACK_ROWS):
        w = ref[c] if row_stride == 1 else ref[c, pl.ds(first_row, n, stride=row_stride), :]
        hi.append(lax.bitcast_convert_type(w & jnp.uint32(0xFFFF0000), F32))
        lo.append(lax.bitcast_convert_type(w << 16, F32))
    return jnp.concatenate(hi + lo, axis=1)


def _split_bf16(x):
    hi = x.astype(BF16)
    return hi, (x - hi.astype(F32)).astype(BF16)


def _route_tile(x, sh, sc, n2w_ref, wrh_ref, wrm_ref, br_ref, ltri_ref, count_ref, h2_ref, route_ref, topw_ref):
    ms = jnp.mean(x * x, axis=-1, keepdims=True)
    h2 = x * lax.rsqrt(ms + NORM_EPS) * n2w_ref[...] * (1.0 + sc) + sh
    _store_packed(h2_ref, h2)
    hh, hm = _split_bf16(h2)
    logits = (jnp.dot(hh, wrh_ref[...], preferred_element_type=F32)
              + (jnp.dot(hh, wrm_ref[...], preferred_element_type=F32)
                 + jnp.dot(hm, wrh_ref[...], preferred_element_type=F32))) + br_ref[...]
    lane = lax.broadcasted_iota(jnp.int32, logits.shape, 1)
    work = logits
    vals, idxs = [], []
    for _ in range(TOP_K):
        m = jnp.max(work, axis=-1, keepdims=True)
        idx = jnp.min(jnp.where(work == m, lane, LANES), axis=-1, keepdims=True)
        vals.append(m)
        idxs.append(idx)
        work = jnp.where(lane == idx, -jnp.inf, work)
    exps = [jnp.exp(v - vals[0]) for v in vals]
    tot = exps[0] + exps[1] + exps[2] + exps[3]
    topw = jnp.zeros(logits.shape, F32)
    for k in range(TOP_K):
        topw = jnp.where(lane == k, exps[k] / tot, topw)
    topw_ref[...] = topw

    onehot = [(lane == idxs[k]).astype(F32) for k in range(TOP_K)]
    chosen = onehot[0] + onehot[1] + onehot[2] + onehot[3]
    before = jnp.dot(ltri_ref[...], chosen.astype(BF16), preferred_element_type=F32) + count_ref[...]
    info = jnp.zeros(logits.shape, jnp.int32)
    for k in range(TOP_K):
        rank = jnp.sum(onehot[k] * before, axis=-1, keepdims=True).astype(jnp.int32)
        info = jnp.where(lane == k, idxs[k], info)
        info = jnp.where(lane == TOP_K + k, rank, info)
    route_ref[...] = jnp.transpose(info)[:2 * TOP_K, :]
    count_ref[...] = count_ref[...] + jnp.sum(chosen, axis=0, keepdims=True)


def _router_kernel(xp_ref, shp_ref, scp_ref, xs_ref, shs_ref, scs_ref, n2w_ref, wrh_ref, wrm_ref, br_ref, ltri_ref,
                   h2_ref, route_ref, topw_ref, count_ref, *, n_prompt_tiles):
    i = pl.program_id(0)
    rest = (n2w_ref, wrh_ref, wrm_ref, br_ref, ltri_ref, count_ref, h2_ref, route_ref, topw_ref)

    @pl.when(i == 0)
    def _():
        count_ref[...] = jnp.zeros(count_ref.shape, F32)

    @pl.when(i < n_prompt_tiles)
    def _():
        _route_tile(xp_ref[...], shp_ref[0], scp_ref[0], *rest)

    @pl.when(i >= n_prompt_tiles)
    def _():
        _route_tile(xs_ref[...], shs_ref[...], scs_ref[...], *rest)


def _router(xp_rows, mod_p, seq_len, xs_rows, mod_s, n2w, w_router, b_router):
    tile = ROUTER_TILE
    n_p, n_s = xp_rows.shape[0], xs_rows.shape[0]
    tp, ts = n_p // tile, n_s // tile
    n_total = n_p + n_s
    mod_p = mod_p.reshape(mod_p.shape[0], 1, mod_p.shape[1])
    seq_of = lambda i: (jnp.minimum(i, tp - 1) * tile) // seq_len
    wr_pad = jnp.pad(w_router, ((0, 0), (0, LANES - N_EXPERTS)))
    wr_hi = wr_pad.astype(BF16)
    wr_mid = (wr_pad - wr_hi.astype(F32)).astype(BF16)
    br_pad = jnp.pad(b_router, ((0, 0), (0, LANES - N_EXPERTS)), constant_values=-1e30)
    ltri = jnp.asarray(np.tril(np.ones((tile, tile), np.float32), -1), BF16)
    clamp_p = lambda i: jnp.minimum(i, tp - 1)
    clamp_s = lambda i: jnp.maximum(i - tp, 0)
    return pl.pallas_call(
        functools.partial(_router_kernel, n_prompt_tiles=tp),
        grid=(tp + ts,),
        in_specs=[pl.BlockSpec((tile, D_MODEL), lambda i: (clamp_p(i), 0)),
                  pl.BlockSpec((1, 1, D_MODEL), lambda i: (seq_of(i), 0, 3)),
                  pl.BlockSpec((1, 1, D_MODEL), lambda i: (seq_of(i), 0, 4)),
                  pl.BlockSpec((tile, D_MODEL), lambda i: (clamp_s(i), 0)),
                  pl.BlockSpec((tile, D_MODEL), lambda i: (clamp_s(i), 3)),
                  pl.BlockSpec((tile, D_MODEL), lambda i: (clamp_s(i), 4)),
                  _const_spec(n2w.shape), _const_spec(wr_hi.shape), _const_spec(wr_mid.shape),
                  _const_spec(br_pad.shape), _const_spec(ltri.shape)],
        out_specs=(pl.BlockSpec((PACK_ROWS, tile, LANES), lambda i: (0, i, 0)),
                   pl.BlockSpec((2 * TOP_K, tile), lambda i: (0, i)),
                   pl.BlockSpec((tile, LANES), lambda i: (i, 0)),
                   pl.BlockSpec((1, LANES), lambda i: (0, 0))),
        out_shape=(jax.ShapeDtypeStruct((PACK_ROWS, n_total, LANES), jnp.uint32),
                   jax.ShapeDtypeStruct((2 * TOP_K, n_total), jnp.int32),
                   jax.ShapeDtypeStruct((n_total, LANES), F32),
                   jax.ShapeDtypeStruct((1, LANES), F32)),
        compiler_params=pltpu.CompilerParams(dimension_semantics=("arbitrary",), vmem_limit_bytes=VMEM_LIMIT),
        name="router",
    )(xp_rows, mod_p, mod_p, xs_rows, mod_s, mod_s, n2w, wr_hi, wr_mid, br_pad, ltri)


def _gather_rows(table, idx):
    n = idx.shape[0]
    steps = n // SC_GATHER_WINDOW
    assert n % SC_GATHER_WINDOW == 0 and steps % SC_WORKERS == 0
    mesh = plsc.VectorSubcoreMesh(core_axis_name="c", subcore_axis_name="s")

    @functools.partial(pl.kernel, out_type=jax.ShapeDtypeStruct((n, table.shape[1]), table.dtype), mesh=mesh,
                       scratch_types=[])
    def gather_kernel(table_hbm, idx_hbm, out_hbm):
        def body(idx_vmem, out_vmem):
            pltpu.sync_copy(table_hbm.at[idx_vmem.at[0]], out_vmem)

        pltpu.emit_pipeline(
            body,
            grid=(steps,),
            in_specs=[pl.BlockSpec((1, SC_GATHER_WINDOW), lambda i: (0, i))],
            out_specs=[pl.BlockSpec((SC_GATHER_WINDOW, table.shape[1]), lambda i: (i, 0))],
            core_axis_name=("c", "s"),
            dimension_semantics=(pltpu.PARALLEL,),
        )(idx_hbm, out_hbm)

    return gather_kernel(table, idx.reshape(1, n))


def _dispatch_packed(table, slot_kt, n_slots):
    planes, n_tok, lanes = table.shape
    win = SC_GATHER_WINDOW
    blocks = n_tok // win
    assert n_tok % win == 0
    blocks_pad = _round_up(blocks, SC_WORKERS // math.gcd(SC_WORKERS, planes))
    n_spare = (blocks_pad - blocks) * TOP_K * win
    n_ext = n_slots + n_spare
    dest = jnp.transpose(slot_kt.reshape(TOP_K, blocks, win), (1, 0, 2))
    spare = n_slots + jnp.arange(n_spare, dtype=jnp.int32).reshape(blocks_pad - blocks, TOP_K, win)
    dest = jnp.concatenate([dest, spare], axis=0)[None] + (jnp.arange(planes, dtype=jnp.int32) * n_ext)[:, None, None, None]
    dest = dest.reshape(planes * blocks_pad * TOP_K, win)
    steps = planes * blocks_pad
    src_block = lambda g: (g // blocks_pad) * blocks + jnp.minimum(g % blocks_pad, blocks - 1)
    mesh = plsc.VectorSubcoreMesh(core_axis_name="c", subcore_axis_name="s")

    @functools.partial(pl.kernel, out_type=jax.ShapeDtypeStruct((planes * n_ext, lanes), table.dtype), mesh=mesh,
                       scratch_types=[])
    def scatter_kernel(table_hbm, dest_hbm, out_hbm):
        def body(rows_vmem, dest_vmem):
            for k in range(TOP_K):
                pltpu.sync_copy(rows_vmem, out_hbm.at[dest_vmem.at[k]])

        pltpu.emit_pipeline(
            body,
            grid=(steps,),
            in_specs=[pl.BlockSpec((win, lanes), lambda g: (src_block(g), 0)),
                      pl.BlockSpec((TOP_K, win), lambda g: (g, 0))],
            out_specs=[],
            core_axis_name=("c", "s"),
            dimension_semantics=(pltpu.PARALLEL,),
        )(table_hbm, dest_hbm)

    return scatter_kernel(table.reshape(planes * n_tok, lanes), dest).reshape(planes, n_ext, lanes)


def _gather_packed(table, rows):
    planes, n_table, lanes = table.shape
    idx = jnp.concatenate([rows + c * n_table for c in range(planes)])
    out = _gather_rows(table.reshape(planes * n_table, lanes), idx)
    return out.reshape(planes, rows.shape[0], lanes)


def _expert_weight_copies(e, w1_hbm, w2_hbm, w1s_ref, w2s_ref, sem):
    return (pltpu.make_async_copy(w1_hbm.at[e], w1s_ref, sem.at[0]),
            pltpu.make_async_copy(w2_hbm.at[e], w2s_ref, sem.at[1]))


def _experts_kernel(te_ref, tr_ref, nx_ref, nv_ref, xs_ref, w1_hbm, b1_ref, w2_hbm, b2_ref, ys_ref,
                    w1s_ref, w2s_ref, w1b_ref, w2b_ref, sem):
    i = pl.program_id(0)
    e = te_ref[i]
    new_expert = (i == 0) | (e != te_ref[jnp.maximum(i - 1, 0)])
    copies = functools.partial(_expert_weight_copies, w1_hbm=w1_hbm, w2_hbm=w2_hbm, w1s_ref=w1s_ref,
                               w2s_ref=w2s_ref, sem=sem)

    @pl.when(i == 0)
    def _():
        for c in copies(e):
            c.start()

    @pl.when(new_expert)
    def _():
        for c in copies(e):
            c.wait()
        w1b_ref[...] = w1s_ref[...].astype(BF16)
        w2b_ref[...] = w2s_ref[...].astype(BF16)

        @pl.when(nx_ref[i] >= 0)
        def _():
            for c in copies(nx_ref[i]):
                c.start()

    for h in range(SLOT_TILE // EXPERT_ROWS):
        rows_here = jnp.where(i < nv_ref[0], tr_ref[i] - h * EXPERT_ROWS, 0)
        xs_h = xs_ref.at[:, pl.ds(h * EXPERT_ROWS, EXPERT_ROWS)]
        ys_h = ys_ref.at[:, pl.ds(h * EXPERT_ROWS, EXPERT_ROWS)]

        @pl.when(rows_here > 0)
        def _(rows_here=rows_here, xs_h=xs_h, ys_h=ys_h):
            row = lax.broadcasted_iota(jnp.int32, (EXPERT_ROWS, D_MODEL), 0)
            x = jnp.where(row < rows_here, _load_packed(xs_h, EXPERT_ROWS), 0.0).astype(BF16)
            hu = jnp.dot(x, w1b_ref[...], preferred_element_type=F32) + b1_ref[0]
            x_glu = jnp.minimum(hu[:, :D_FF], SWIGLU_LIMIT)
            x_lin = jnp.clip(hu[:, D_FF:], -SWIGLU_LIMIT, SWIGLU_LIMIT)
            act = x_glu * jax.nn.sigmoid(SWIGLU_ALPHA * x_glu) * (x_lin + 1.0)
            _store_packed(ys_h, jnp.dot(act.astype(BF16), w2b_ref[...], preferred_element_type=F32) + b2_ref[0])

        @pl.when(rows_here <= 0)
        def _(ys_h=ys_h):
            for c in range(PACK_ROWS):
                ys_h[c] = jnp.zeros((EXPERT_ROWS, LANES), jnp.uint32)


def _experts(tile_expert, tile_rows, next_expert, n_valid, xs, w1, b1, w2, b2):
    n_tiles = tile_expert.shape[0]
    n_slots = n_tiles * SLOT_TILE
    grid_spec = pltpu.PrefetchScalarGridSpec(
        num_scalar_prefetch=4,
        grid=(n_tiles,),
        in_specs=[
            pl.BlockSpec((PACK_ROWS, SLOT_TILE, LANES), lambda i, te, tr, nx, nv: (0, i, 0)),
            pl.BlockSpec(memory_space=pl.ANY),
            pl.BlockSpec((1, 1, 2 * D_FF), lambda i, te, tr, nx, nv: (te[i], 0, 0)),
            pl.BlockSpec(memory_space=pl.ANY),
            pl.BlockSpec((1, 1, D_MODEL), lambda i, te, tr, nx, nv: (te[i], 0, 0)),
        ],
        out_specs=pl.BlockSpec((PACK_ROWS, SLOT_TILE, LANES), lambda i, te, tr, nx, nv: (0, i, 0)),
        scratch_shapes=[pltpu.VMEM((D_MODEL, 2 * D_FF), F32), pltpu.VMEM((D_FF, D_MODEL), F32),
                        pltpu.VMEM((D_MODEL, 2 * D_FF), BF16), pltpu.VMEM((D_FF, D_MODEL), BF16),
                        pltpu.SemaphoreType.DMA((2,))],
    )
    return pl.pallas_call(
        _experts_kernel,
        grid_spec=grid_spec,
        out_shape=jax.ShapeDtypeStruct((PACK_ROWS, n_slots, LANES), jnp.uint32),
        compiler_params=pltpu.CompilerParams(dimension_semantics=("arbitrary",), vmem_limit_bytes=VMEM_LIMIT),
        name="experts",
    )(tile_expert, tile_rows, next_expert, n_valid, xs, w1, b1.reshape(N_EXPERTS, 1, 2 * D_FF), w2,
      b2.reshape(N_EXPERTS, 1, D_MODEL))


def _combine_kernel(y4_ref, x1_ref, topw_ref, g2_ref, fw_ref, *rest, per_row_mod):
    o_ref = rest[-1]
    w = topw_ref[...]
    n = w.shape[0]
    ff = None
    for k in range(TOP_K):
        yk = w[:, k:k + 1] * _load_packed(y4_ref.at[:, k], n)
        ff = yk if ff is None else ff + yk
    g2 = g2_ref[...] if per_row_mod else g2_ref[0]
    x = x1_ref[...] + g2 * ff
    ms = jnp.mean(x * x, axis=-1, keepdims=True)
    o_ref[...] = x * lax.rsqrt(ms + NORM_EPS) * fw_ref[...]


def _combine(y4, y4_row0, x1, x1_row0, n_rows, topw, topw_row0, mod, rows_per_mod, fw, out_buf):
    tile = COMBINE_TILE
    y4_off, x1_off, tw_off = y4_row0 // tile, x1_row0 // tile, topw_row0 // tile
    per_row = rows_per_mod == 1
    if per_row:
        g2_spec = pl.BlockSpec((tile, D_MODEL), lambda i: (i + x1_off, 5))
    else:
        mod = mod.reshape(mod.shape[0], 1, mod.shape[1])
        g2_spec = pl.BlockSpec((1, 1, D_MODEL), lambda i: (((i + x1_off) * tile) // rows_per_mod, 0, 5))
    in_specs = [pl.BlockSpec((PACK_ROWS, TOP_K, tile, LANES), lambda i: (0, 0, i + y4_off, 0)),
                pl.BlockSpec((tile, D_MODEL), lambda i: (i + x1_off, 0)),
                pl.BlockSpec((tile, LANES), lambda i: (i + tw_off, 0)),
                g2_spec, _const_spec(fw.shape)]
    args = [y4, x1, topw, mod, fw]
    aliases = {}
    if out_buf is not None:
        in_specs.append(pl.BlockSpec(memory_space=pl.ANY))
        args.append(out_buf)
        aliases = {len(args) - 1: 0}
    return pl.pallas_call(
        functools.partial(_combine_kernel, per_row_mod=per_row),
        grid=(n_rows // tile,),
        in_specs=in_specs,
        out_specs=pl.BlockSpec((tile, D_MODEL), lambda i: (i + x1_off, 0)),
        out_shape=jax.ShapeDtypeStruct(x1.shape, F32),
        input_output_aliases=aliases,
        compiler_params=pltpu.CompilerParams(dimension_semantics=("arbitrary",), vmem_limit_bytes=VMEM_LIMIT),
        name="combine",
    )(*args)


def _routing_tables(route, counts, n_slots):
    padded = ((counts + SLOT_TILE - 1) // SLOT_TILE) * SLOT_TILE
    pend = jnp.cumsum(padded)
    poff = pend - padded
    expert_kt, rank_kt = route[:TOP_K], route[TOP_K:]
    experts = jnp.arange(N_EXPERTS, dtype=jnp.int32)
    start_kt = jnp.sum((expert_kt[None] == experts[:, None, None]).astype(jnp.int32) * poff[:, None, None], axis=0)
    slot_kt = start_kt + rank_kt
    n_tiles = n_slots // SLOT_TILE
    n_valid = (pend[-1] // SLOT_TILE).astype(jnp.int32)
    tile_row = jnp.minimum(jnp.arange(n_tiles, dtype=jnp.int32), n_valid - 1) * SLOT_TILE
    in_later = (pend[None, :] <= tile_row[:, None]).astype(jnp.int32)
    tile_e = jnp.sum(in_later, axis=1).astype(jnp.int32)
    is_e = (experts[None, :] == tile_e[:, None]).astype(jnp.int32)
    used_end = jnp.sum(is_e * (poff + counts)[None, :], axis=1)
    tile_rows = jnp.clip(used_end - tile_row, 0, SLOT_TILE).astype(jnp.int32)
    later_used = (experts[None, :] > tile_e[:, None]) & (counts[None, :] > 0)
    next_e = jnp.min(jnp.where(later_used, experts[None, :], N_EXPERTS), axis=1)
    next_e = jnp.where(next_e < N_EXPERTS, next_e, -1).astype(jnp.int32)
    return slot_kt, tile_e, tile_rows, next_e, n_valid.reshape(1)


def _round_up(n, m):
    return ((n + m - 1) // m) * m


def kernel(x_prompt, x_sample, c_prompt, c_sample, state_ret, state_s5_re, state_s5_im, norm1_w, norm2_w, w_ada, b_ada, w_in, ret_norm_w, s5_lam_re, s5_lam_im, s5_log_dt, s5_b_re, s5_b_im, s5_c_re, s5_c_im, s5_d, w_glu, b_glu, s5_norm_w, w_out, w_router, b_router, w1, b1, w2, b2, final_w):
    bp, lp, _ = x_prompt.shape
    bs, ls, _ = x_sample.shape
    assert norm1_w.shape[0] == 1, "single-layer model"
    n_p, n_s = bp * lp, bs * ls
    n_tok = n_p + n_s

    mod = _ada(jnp.concatenate([c_prompt, c_sample], axis=0), w_ada[0], b_ada[0])
    mod_p, mod_s = mod[:bp], jnp.repeat(mod[bp:], ls, axis=0)

    lbr, lbi, bbr, bbi = _s5prep(s5_lam_re[0], s5_lam_im[0], s5_log_dt[0], s5_b_re[0], s5_b_im[0])
    bmat = jnp.concatenate([_block_diag(bbr), _block_diag(bbi)], axis=-1).astype(BF16)
    cre = _block_diag(jnp.transpose(s5_c_re[0], (0, 2, 1))).astype(BF16)
    cim = _block_diag(jnp.transpose(-s5_c_im[0], (0, 2, 1))).astype(BF16)
    wts = dict(
        n1w=norm1_w, w_in=w_in[0].astype(BF16), rnw=ret_norm_w, bmat=bmat, cre=cre, cim=cim,
        lbr=lbr.reshape(1, SSM_CH), lbi=lbi.reshape(1, SSM_CH), dsk=s5_d[0].reshape(1, SSM_WIDTH),
        w_glu=w_glu[0].astype(BF16), b_glu=b_glu, snw=s5_norm_w, w_out=w_out[0].astype(BF16),
    )

    zero_states = (jnp.zeros((bp, RET_HEADS, HEAD_DIM, HEAD_DIM), F32), jnp.zeros((bp, SSM_CH), F32),
                   jnp.zeros((bp, SSM_CH), F32))
    x1_p, ret_p, re_p, im_p = _mixer(x_prompt, mod_p, np.arange(lp, dtype=np.float32), zero_states, wts,
                                     prompt=True)
    sample_states = (state_ret[0], state_s5_re[0].reshape(bs, SSM_CH), state_s5_im[0].reshape(bs, SSM_CH))
    x1_s, ret_s, re_s, im_s = _mixer(x_sample.reshape(n_s, D_MODEL), mod_s,
                                     PAST_LEN + np.arange(ls, dtype=np.float32), sample_states, wts, prompt=False)

    x1_p_rows = x1_p.reshape(n_p, D_MODEL)
    h2, route, topw, counts = _router(x1_p_rows, mod_p, lp, x1_s, mod_s, norm2_w, w_router[0], b_router)

    n_assign = n_tok * TOP_K
    gather_quantum = SC_GATHER_WINDOW * SC_WORKERS // PACK_ROWS
    assert n_assign % gather_quantum == 0
    n_slots = _round_up(_round_up(n_assign, SLOT_TILE) + N_EXPERTS * SLOT_TILE, gather_quantum)
    slot_kt, tile_e, tile_rows, next_e, n_valid = _routing_tables(route, counts[0, :N_EXPERTS].astype(jnp.int32),
                                                                  n_slots)
    xs = _dispatch_packed(h2, slot_kt, n_slots)
    ys = _experts(tile_e, tile_rows, next_e, n_valid, xs, w1[0], b1[0], w2[0], b2[0])
    fw = final_w.reshape(1, D_MODEL)
    y_p, y_s = None, None
    bounds = [r * (n_p // COMBINE_RANGES) for r in range(COMBINE_RANGES)] + [n_tok]
    for lo, hi in zip(bounds[:-1], bounds[1:]):
        y4 = _gather_packed(ys, slot_kt[:, lo:hi].reshape(-1)).reshape(PACK_ROWS, TOP_K, hi - lo, LANES)
        y_p = _combine(y4, 0, x1_p_rows, lo, min(hi, n_p) - lo, topw, lo, mod_p, lp, fw, y_p)
        if hi > n_p:
            y_s = _combine(y4, n_p - lo, x1_s, 0, n_s, topw, n_p, mod_s, 1, fw, None)

    g, p = SSM_GROUPS, SSM_STATE
    return (y_p.reshape(bp, lp, D_MODEL), y_s.reshape(bs, ls, D_MODEL),
            ret_p[None], re_p.reshape(1, bp, g, p), im_p.reshape(1, bp, g, p),
            ret_s[None], re_s.reshape(1, bs, g, p), im_s.reshape(1, bs, g, p))
```

```python
import functools
import math

import jax
import jax.numpy as jnp
import numpy as np
from jax import lax
from jax.experimental import pallas as pl
from jax.experimental.pallas import tpu as pltpu
from jax.experimental.pallas import tpu_sc as plsc

F32 = jnp.float32
BF16 = jnp.bfloat16
HIGHEST = lax.Precision.HIGHEST

D_MODEL = 1024
PAST_LEN = 16384
RET_WIDTH = 512
RET_HEADS = 4
HEAD_DIM = 128
ROPE_BASE = 10000.0
SSM_WIDTH = 512
SSM_GROUP = 16
SSM_GROUPS = 32
SSM_STATE = 64
SSM_CH = SSM_GROUPS * SSM_STATE
IN_WIDTH = 4 * RET_WIDTH + SSM_WIDTH
N_EXPERTS = 32
TOP_K = 4
D_FF = 1024
SWIGLU_LIMIT = 7.0
SWIGLU_ALPHA = 1.702
NORM_EPS = 1e-6

LANES = 128
SUBLANES = 8
VMEM_LIMIT = 56 * 1024 * 1024

SEQ_PER_BLOCK = 8
SAMPLE_SEQ_PER_BLOCK = 16
SCAN_ELEMS = 8 * 1024
PROMPT_CHUNK = 64
S5_BLOCK_GROUPS = 8
N_S5_BLOCKS = SSM_GROUPS // S5_BLOCK_GROUPS
S5_BLOCK_IN = S5_BLOCK_GROUPS * SSM_GROUP
S5_BLOCK_CH = S5_BLOCK_GROUPS * SSM_STATE
ROUTER_TILE = 512
SLOT_TILE = 512
EXPERT_ROWS = 256
COMBINE_TILE = 256
COMBINE_RANGES = 2
SC_GATHER_WINDOW = 128
SC_WORKERS = 32


def _silu(x):
    return x * jax.nn.sigmoid(x)


def _ada_kernel(c_ref, w_ref, b_ref, o_ref):
    s = _silu(c_ref[...])
    o_ref[...] = jnp.dot(s, w_ref[...], precision=HIGHEST, preferred_element_type=F32) + b_ref[...]


def _ada(c_all, w_ada, b_ada):
    n_rows, n_out = c_all.shape[0], w_ada.shape[1]
    tn = 1536
    return pl.pallas_call(
        _ada_kernel,
        grid=(n_out // tn,),
        in_specs=[
            pl.BlockSpec((n_rows, D_MODEL), lambda j: (0, 0)),
            pl.BlockSpec((D_MODEL, tn), lambda j: (0, j)),
            pl.BlockSpec((1, tn), lambda j: (0, j)),
        ],
        out_specs=pl.BlockSpec((n_rows, tn), lambda j: (0, j)),
        out_shape=jax.ShapeDtypeStruct((n_rows, n_out), F32),
        compiler_params=pltpu.CompilerParams(dimension_semantics=("arbitrary",), vmem_limit_bytes=VMEM_LIMIT),
        name="ada",
    )(c_all, w_ada, b_ada.reshape(1, n_out))


def _s5prep_kernel(lre_ref, lim_ref, ldt_ref, bre_ref, bim_ref, lbr_ref, lbi_ref, bbr_ref, bbi_ref):
    lam_re, lam_im = lre_ref[...], lim_ref[...]
    dt = jnp.exp(ldt_ref[...])
    mag = jnp.exp(lam_re * dt)
    ang = lam_im * dt
    lb_re, lb_im = mag * jnp.cos(ang), mag * jnp.sin(ang)
    den = lam_re * lam_re + lam_im * lam_im
    f_re = ((lb_re - 1.0) * lam_re + lb_im * lam_im) / den
    f_im = (lb_im * lam_re - (lb_re - 1.0) * lam_im) / den
    lbr_ref[...] = lb_re
    lbi_ref[...] = lb_im
    b_re, b_im = bre_ref[...], bim_ref[...]
    bbr_ref[...] = f_re[:, None, :] * b_re - f_im[:, None, :] * b_im
    bbi_ref[...] = f_re[:, None, :] * b_im + f_im[:, None, :] * b_re


def _s5prep(lam_re, lam_im, log_dt, b_re, b_im):
    g, p = lam_re.shape
    bt_re = jnp.transpose(b_re, (0, 2, 1))
    bt_im = jnp.transpose(b_im, (0, 2, 1))
    return pl.pallas_call(
        _s5prep_kernel,
        out_shape=(
            jax.ShapeDtypeStruct((g, p), F32), jax.ShapeDtypeStruct((g, p), F32),
            jax.ShapeDtypeStruct((g, SSM_GROUP, p), F32), jax.ShapeDtypeStruct((g, SSM_GROUP, p), F32),
        ),
        name="s5prep",
    )(lam_re, lam_im, log_dt.reshape(g, 1), bt_re, bt_im)


def _block_diag(blocks):
    _, r, c = blocks.shape
    b4 = blocks.reshape(N_S5_BLOCKS, S5_BLOCK_GROUPS, r, c)
    eye = jnp.eye(S5_BLOCK_GROUPS, dtype=blocks.dtype)
    out = b4[:, :, :, None, :] * eye[None, :, None, :, None]
    return out.reshape(N_S5_BLOCKS, S5_BLOCK_GROUPS * r, S5_BLOCK_GROUPS * c)


def _mixer_kernel(x_ref, mod_ref, n1w_ref, win_ref, cos_ref, sin_ref, dmask_ref, cdec_ref, sdec_ref,
                  rnw_ref, bmat_ref, cre_ref, cim_ref, lbr_ref, lbi_ref, dsk_ref, wglu_ref, bglu_ref,
                  snw_ref, wout_ref, sret0_ref, sre0_ref, sim0_ref,
                  x1_ref, sret_ref, sre_ref, sim_ref,
                  hb_ref, z_ref, zu_ref, oy_ref, utb_ref, bur0_ref, bur1_ref, bui0_ref, bui1_ref, ytb_ref, yb_ref,
                  *, n_seq, chunk, tile_rows, carry, chunk_decay):
    rows = n_seq * chunk
    seq_per_tile = tile_rows // chunk
    n_tiles = rows // tile_rows
    per_row_mod = mod_ref.shape[0] == rows

    def load_states():
        sret_ref[...] = sret0_ref[...]
        sre_ref[...] = sre0_ref[...]
        sim_ref[...] = sim0_ref[...]

    if carry:
        pl.when(pl.program_id(0) == 0)(load_states)
    else:
        load_states()

    n1w = n1w_ref[...]
    mod_rows = rows if per_row_mod else chunk
    for i in range(rows // mod_rows):
        r0 = i * mod_rows
        xb = _load_rows(x_ref, r0, mod_rows, chunk)
        if per_row_mod:
            sh = mod_ref[pl.ds(r0, mod_rows), pl.ds(0, D_MODEL)]
            sc = mod_ref[pl.ds(r0, mod_rows), pl.ds(D_MODEL, D_MODEL)]
        else:
            sh = mod_ref[pl.ds(i, 1), pl.ds(0, D_MODEL)]
            sc = mod_ref[pl.ds(i, 1), pl.ds(D_MODEL, D_MODEL)]
        ms = jnp.mean(xb * xb, axis=-1, keepdims=True)
        hn = xb * lax.rsqrt(ms + NORM_EPS) * n1w
        hb_ref[pl.ds(r0, mod_rows), :] = (hn * (1.0 + sc) + sh).astype(BF16)
    ret_w = 4 * RET_WIDTH
    z_ref[...] = jnp.dot(hb_ref[...], win_ref[:, pl.ds(0, ret_w)], preferred_element_type=F32)
    zu = jnp.dot(hb_ref[...], win_ref[:, pl.ds(ret_w, SSM_WIDTH)], preferred_element_type=F32)
    pitch = zu_ref.shape[1] // n_seq
    for c in range(SSM_WIDTH // LANES):
        for b in range(n_seq if pitch != chunk else 1):
            nb = chunk if pitch != chunk else rows
            zu_ref[c, pl.ds(b * pitch, nb), :] = zu[b * chunk:b * chunk + nb, c * LANES:(c + 1) * LANES]

    cos = cos_ref[...]
    sin = sin_ref[...]
    scale = HEAD_DIM ** -0.5
    if seq_per_tile > 1:
        row_id = lax.broadcasted_iota(jnp.int32, (tile_rows, HEAD_DIM), 0)

    def rope(t):
        return t * cos + pltpu.roll(t, HEAD_DIM // 2, 1) * sin

    def ret_tile(ti, c):
        r0 = pl.multiple_of(ti * tile_rows, tile_rows)
        for h in range(RET_HEADS):
            c0 = h * HEAD_DIM
            q = rope(z_ref[pl.ds(r0, tile_rows), pl.ds(c0, HEAD_DIM)])
            k = rope(z_ref[pl.ds(r0, tile_rows), pl.ds(RET_WIDTH + c0, HEAD_DIM)]) * scale
            v = z_ref[pl.ds(r0, tile_rows), pl.ds(2 * RET_WIDTH + c0, HEAD_DIM)]
            g = z_ref[pl.ds(r0, tile_rows), pl.ds(3 * RET_WIDTH + c0, HEAD_DIM)]
            kd = k * sdec_ref[h]
            if tile_rows < HEAD_DIM:
                pad = jnp.zeros((HEAD_DIM - tile_rows, HEAD_DIM), F32)
                k, v, kd = (jnp.concatenate([t, pad], axis=0) for t in (k, v, kd))
                if seq_per_tile > 1:
                    row_kv = lax.broadcasted_iota(jnp.int32, (HEAD_DIM, HEAD_DIM), 0)
            elif seq_per_tile > 1:
                row_kv = row_id
            qb, kb, vb = q.astype(BF16), k.astype(BF16), v.astype(BF16)
            s = lax.dot_general(qb, kb, (((1,), (1,)), ((), ())), preferred_element_type=F32) * dmask_ref[h]
            o = jnp.dot(s.astype(BF16), vb, preferred_element_type=F32)
            cross = None
            for si in range(seq_per_tile):
                sidx = ti * seq_per_tile + si
                st = sret_ref[sidx, h]
                cr = jnp.dot(qb, st.astype(BF16), preferred_element_type=F32)
                if seq_per_tile > 1:
                    in_seq = (row_id >= si * chunk) & (row_id < (si + 1) * chunk)
                    cross = jnp.where(in_seq, cr, 0.0 if cross is None else cross)
                    kds = jnp.where((row_kv >= si * chunk) & (row_kv < (si + 1) * chunk), kd, 0.0)
                else:
                    cross, kds = cr, kd
                upd = lax.dot_general(kds.astype(BF16), vb, (((0,), (0,)), ((), ())), preferred_element_type=F32)
                sret_ref[sidx, h] = st * chunk_decay[h] + upd
            o = o + cross * cdec_ref[h]
            o = o * lax.rsqrt(jnp.mean(o * o, axis=-1, keepdims=True) + NORM_EPS)
            o = o * rnw_ref[:, pl.ds(c0, HEAD_DIM)] * _silu(g)
            oy_ref[pl.ds(r0, tile_rows), pl.ds(c0, HEAD_DIM)] = o
        return c

    lax.fori_loop(0, n_tiles, ret_tile, 0, unroll=True)

    for t in range(chunk):
        for c in range(SSM_WIDTH // LANES):
            utb_ref[pl.ds(t * n_seq, n_seq), pl.ds(c * LANES, LANES)] = zu_ref[c, pl.ds(t, n_seq, stride=pitch), :]
    half_ch = SSM_CH // 2
    blk_per_half = N_S5_BLOCKS // 2
    bur_refs, bui_refs = (bur0_ref, bur1_ref), (bui0_ref, bui1_ref)
    for blk in range(N_S5_BLOCKS):
        hf, lcols = blk // blk_per_half, pl.ds((blk % blk_per_half) * S5_BLOCK_CH, S5_BLOCK_CH)
        ub = utb_ref[:, pl.ds(blk * S5_BLOCK_IN, S5_BLOCK_IN)].astype(BF16)
        bu = jnp.dot(ub, bmat_ref[blk], preferred_element_type=F32)
        bur_refs[hf][:, lcols] = bu[:, :S5_BLOCK_CH]
        bui_refs[hf][:, lcols] = bu[:, S5_BLOCK_CH:]

    scan_w = min(half_ch, SCAN_ELEMS // n_seq)
    for hf in range(2):
        bur_ref, bui_ref = bur_refs[hf], bui_refs[hf]
        for p in range(half_ch // scan_w):
            cols = pl.ds(p * scan_w, scan_w)
            gcols = pl.ds(hf * half_ch + p * scan_w, scan_w)
            lbr = jnp.broadcast_to(lbr_ref[:, gcols], (n_seq, scan_w))
            lbi = jnp.broadcast_to(lbi_ref[:, gcols], (n_seq, scan_w))
            hr, hi = sre_ref[:, gcols], sim_ref[:, gcols]
            for t in range(chunk):
                rws = pl.ds(t * n_seq, n_seq)
                hr, hi = (lbr * hr - lbi * hi + bur_ref[rws, cols], lbr * hi + lbi * hr + bui_ref[rws, cols])
                bur_ref[rws, cols] = hr
                bui_ref[rws, cols] = hi
            sre_ref[:, gcols] = hr
            sim_ref[:, gcols] = hi

    for blk in range(N_S5_BLOCKS):
        hf, lcols = blk // blk_per_half, pl.ds((blk % blk_per_half) * S5_BLOCK_CH, S5_BLOCK_CH)
        yb = jnp.dot(bur_refs[hf][:, lcols].astype(BF16), cre_ref[blk], preferred_element_type=F32)
        yb = yb + jnp.dot(bui_refs[hf][:, lcols].astype(BF16), cim_ref[blk], preferred_element_type=F32)
        ucols = pl.ds(blk * S5_BLOCK_IN, S5_BLOCK_IN)
        ytb_ref[:, ucols] = yb + dsk_ref[:, ucols] * utb_ref[:, ucols]
    for t in range(chunk):
        for c in range(SSM_WIDTH // LANES):
            yb_ref[c, pl.ds(t, n_seq, stride=pitch), :] = ytb_ref[pl.ds(t * n_seq, n_seq), pl.ds(c * LANES, LANES)]

    def seq_major(c):
        if pitch == chunk:
            return yb_ref[c]
        return jnp.concatenate([yb_ref[c, pl.ds(b * pitch, chunk), :] for b in range(n_seq)], axis=0)

    y = jnp.concatenate([seq_major(c) for c in range(SSM_WIDTH // LANES)], axis=1)
    y = jax.nn.gelu(y, approximate=True)
    gate = jnp.dot(y.astype(BF16), wglu_ref[...], preferred_element_type=F32) + bglu_ref[...]
    y = y * jax.nn.sigmoid(gate)
    y = y * lax.rsqrt(jnp.mean(y * y, axis=-1, keepdims=True) + NORM_EPS) * snw_ref[...]
    oy_ref[:, pl.ds(RET_WIDTH, SSM_WIDTH)] = y

    mix = jnp.dot(oy_ref[...].astype(BF16), wout_ref[...], preferred_element_type=F32)
    for i in range(rows // mod_rows):
        r0 = i * mod_rows
        if per_row_mod:
            g1 = mod_ref[pl.ds(r0, mod_rows), pl.ds(2 * D_MODEL, D_MODEL)]
        else:
            g1 = mod_ref[pl.ds(i, 1), pl.ds(2 * D_MODEL, D_MODEL)]
        _store_rows(x1_ref, r0, mod_rows, chunk,
                    _load_rows(x_ref, r0, mod_rows, chunk) + g1 * mix[r0:r0 + mod_rows])


def _load_rows(ref, r0, n, chunk):
    if len(ref.shape) == 2:
        return ref[pl.ds(r0, n), :]
    assert n == chunk and r0 % chunk == 0
    return ref[r0 // chunk]


def _store_rows(ref, r0, n, chunk, val):
    if len(ref.shape) == 2:
        ref[pl.ds(r0, n), :] = val
    else:
        assert n == chunk and r0 % chunk == 0
        ref[r0 // chunk] = val


def _seq_pitch(chunk):
    return chunk + SUBLANES if chunk % SUBLANES == 0 else chunk


def _const_spec(shape):
    nd = len(shape)
    return pl.BlockSpec(shape, lambda j, _n=nd: (0,) * _n)


def _decay_tables(chunk, tile_rows):
    f32 = np.float32
    log_gamma = np.log1p(-np.exp2(f32(-5.0) - np.arange(RET_HEADS, dtype=f32))).astype(f32)
    r = np.arange(tile_rows)
    seq, loc = r // chunk, (r % chunk).astype(f32)
    rel = loc[:, None] - loc[None, :]
    ok = (seq[:, None] == seq[None, :]) & (rel >= 0)
    dmask = np.where(ok[None], np.exp(np.where(ok, rel, f32(0.0))[None] * log_gamma[:, None, None]), f32(0.0))
    if tile_rows < HEAD_DIM:
        dmask = np.pad(dmask, ((0, 0), (0, 0), (0, HEAD_DIM - tile_rows)))
    cdec = np.exp((loc[None, :] + f32(1.0)) * log_gamma[:, None])
    sdec = np.exp((f32(chunk) - f32(1.0) - loc)[None, :] * log_gamma[:, None])
    bcast = lambda t: np.ascontiguousarray(np.broadcast_to(t[:, :, None], (RET_HEADS, tile_rows, HEAD_DIM)))
    return dmask.astype(f32), bcast(cdec.astype(f32)), bcast(sdec.astype(f32))


def _rope_tables(pos):
    f32 = np.float32
    half = HEAD_DIM // 2
    inv_freq = (f32(ROPE_BASE) ** (-np.arange(half, dtype=f32) / f32(half))).astype(f32)
    ang = (pos.astype(f32)[:, None] * inv_freq[None, :]).astype(f32)
    cos, sin = np.cos(ang).astype(f32), np.sin(ang).astype(f32)
    return np.concatenate([cos, cos], axis=-1), np.concatenate([-sin, sin], axis=-1)


def _mixer(x, mod, pos, states, wts, *, prompt):
    n_seq = SEQ_PER_BLOCK if prompt else SAMPLE_SEQ_PER_BLOCK
    if prompt:
        n_total, seq_len, _ = x.shape
        assert n_total == n_seq
        chunk, tile_rows, n_steps = PROMPT_CHUNK, PROMPT_CHUNK, seq_len // PROMPT_CHUNK
        x_spec = pl.BlockSpec((n_seq, chunk, D_MODEL), lambda j: (0, j, 0))
        mod_spec = pl.BlockSpec((n_seq, 3 * D_MODEL), lambda j: (0, 0))
        tab_spec = pl.BlockSpec((chunk, HEAD_DIM), lambda j: (j, 0))
        seq_map = lambda j: 0
    else:
        chunk = pos.shape[0]
        tile_rows = SUBLANES
        n_total = x.shape[0] // chunk
        n_steps = n_total // n_seq
        x_spec = pl.BlockSpec((n_seq * chunk, D_MODEL), lambda j: (j, 0))
        mod_spec = pl.BlockSpec((n_seq * chunk, 3 * D_MODEL), lambda j: (j, 0))
        tab_spec = _const_spec((tile_rows, HEAD_DIM))
        seq_map = lambda j: j
    rows = n_seq * chunk
    cos, sin = _rope_tables(pos)
    if not prompt:
        reps = tile_rows // chunk
        cos, sin = np.tile(cos, (reps, 1)), np.tile(sin, (reps, 1))
    dmask, cdec, sdec = _decay_tables(chunk, tile_rows)
    chunk_decay = tuple(float(math.exp(chunk * math.log1p(-2.0 ** (-5.0 - h)))) for h in range(RET_HEADS))
    sret0, sre0, sim0 = states

    st_ret_spec = pl.BlockSpec((n_seq, RET_HEADS, HEAD_DIM, HEAD_DIM), lambda j: (seq_map(j), 0, 0, 0))
    st_s5_spec = pl.BlockSpec((n_seq, SSM_CH), lambda j: (seq_map(j), 0))
    consts = [dmask, cdec, sdec, wts["rnw"], wts["bmat"], wts["cre"], wts["cim"], wts["lbr"], wts["lbi"],
              wts["dsk"], wts["w_glu"], wts["b_glu"], wts["snw"], wts["w_out"]]
    args = [x, mod, wts["n1w"], wts["w_in"], cos, sin] + consts + [sret0, sre0, sim0]
    in_specs = ([x_spec, mod_spec, _const_spec(wts["n1w"].shape), _const_spec(wts["w_in"].shape), tab_spec, tab_spec]
                + [_const_spec(a.shape) for a in consts] + [st_ret_spec, st_s5_spec, st_s5_spec])

    kern = functools.partial(_mixer_kernel, n_seq=n_seq, chunk=chunk, tile_rows=tile_rows, carry=prompt,
                             chunk_decay=chunk_decay)
    out_shape = (
        jax.ShapeDtypeStruct(x.shape, F32),
        jax.ShapeDtypeStruct((n_total, RET_HEADS, HEAD_DIM, HEAD_DIM), F32),
        jax.ShapeDtypeStruct((n_total, SSM_CH), F32),
        jax.ShapeDtypeStruct((n_total, SSM_CH), F32),
    )
    scratch = [
        pltpu.VMEM((rows, D_MODEL), BF16),
        pltpu.VMEM((rows, 4 * RET_WIDTH), F32),
        pltpu.VMEM((SSM_WIDTH // LANES, n_seq * _seq_pitch(chunk), LANES), F32),
        pltpu.VMEM((rows, D_MODEL), F32),
        pltpu.VMEM((rows, SSM_WIDTH), F32),
        pltpu.VMEM((rows, SSM_CH // 2), F32),
        pltpu.VMEM((rows, SSM_CH // 2), F32),
        pltpu.VMEM((rows, SSM_CH // 2), F32),
        pltpu.VMEM((rows, SSM_CH // 2), F32),
        pltpu.VMEM((rows, SSM_WIDTH), F32),
        pltpu.VMEM((SSM_WIDTH // LANES, n_seq * _seq_pitch(chunk), LANES), F32),
    ]
    return pl.pallas_call(
        kern,
        grid=(n_steps,),
        in_specs=in_specs,
        out_specs=(x_spec, st_ret_spec, st_s5_spec, st_s5_spec),
        out_shape=out_shape,
        scratch_shapes=scratch,
        compiler_params=pltpu.CompilerParams(dimension_semantics=("arbitrary",), vmem_limit_bytes=VMEM_LIMIT),
        name="mixer_prompt" if prompt else "mixer_sample",
    )(*args)


PACK_ROWS = D_MODEL // (2 * LANES)


def _store_packed(ref, x):
    half = D_MODEL // 2
    bits = lax.bitcast_convert_type(x.astype(BF16).astype(F32), jnp.uint32)
    words = bits[:, :half] | (bits[:, half:] >> 16)
    for c in range(PACK_ROWS):
        ref[c] = words[:, c * LANES:(c + 1) * LANES]


def _load_packed(ref, n, first_row=0, row_stride=1):
    hi, lo = [], []
    for c in range(PACK_ROWS):
        w = ref[c] if row_stride == 1 else ref[c, pl.ds(first_row, n, stride=row_stride), :]
        hi.append(lax.bitcast_convert_type(w & jnp.uint32(0xFFFF0000), F32))
        lo.append(lax.bitcast_convert_type(w << 16, F32))
    return jnp.concatenate(hi + lo, axis=1)


def _split_bf16(x):
    hi = x.astype(BF16)
    return hi, (x - hi.astype(F32)).astype(BF16)


def _route_tile(x, sh, sc, n2w_ref, wrh_ref, wrm_ref, br_ref, ltri_ref, count_ref, h2_ref, route_ref, topw_ref):
    ms = jnp.mean(x * x, axis=-1, keepdims=True)
    h2 = x * lax.rsqrt(ms + NORM_EPS) * n2w_ref[...] * (1.0 + sc) + sh
    _store_packed(h2_ref, h2)
    hh, hm = _split_bf16(h2)
    logits = (jnp.dot(hh, wrh_ref[...], preferred_element_type=F32)
              + (jnp.dot(hh, wrm_ref[...], preferred_element_type=F32)
                 + jnp.dot(hm, wrh_ref[...], preferred_element_type=F32))) + br_ref[...]
    lane = lax.broadcasted_iota(jnp.int32, logits.shape, 1)
    work = logits
    vals, idxs = [], []
    for _ in range(TOP_K):
        m = jnp.max(work, axis=-1, keepdims=True)
        idx = jnp.min(jnp.where(work == m, lane, LANES), axis=-1, keepdims=True)
        vals.append(m)
        idxs.append(idx)
        work = jnp.where(lane == idx, -jnp.inf, work)
    exps = [jnp.exp(v - vals[0]) for v in vals]
    tot = exps[0] + exps[1] + exps[2] + exps[3]
    topw = jnp.zeros(logits.shape, F32)
    for k in range(TOP_K):
        topw = jnp.where(lane == k, exps[k] / tot, topw)
    topw_ref[...] = topw

    onehot = [(lane == idxs[k]).astype(F32) for k in range(TOP_K)]
    chosen = onehot[0] + onehot[1] + onehot[2] + onehot[3]
    before = jnp.dot(ltri_ref[...], chosen.astype(BF16), preferred_element_type=F32) + count_ref[...]
    info = jnp.zeros(logits.shape, jnp.int32)
    for k in range(TOP_K):
        rank = jnp.sum(onehot[k] * before, axis=-1, keepdims=True).astype(jnp.int32)
        info = jnp.where(lane == k, idxs[k], info)
        info = jnp.where(lane == TOP_K + k, rank, info)
    route_ref[...] = jnp.transpose(info)[:2 * TOP_K, :]
    count_ref[...] = count_ref[...] + jnp.sum(chosen, axis=0, keepdims=True)


def _router_kernel(xp_ref, shp_ref, scp_ref, xs_ref, shs_ref, scs_ref, n2w_ref, wrh_ref, wrm_ref, br_ref, ltri_ref,
                   h2_ref, route_ref, topw_ref, count_ref, *, n_prompt_tiles):
    i = pl.program_id(0)
    rest = (n2w_ref, wrh_ref, wrm_ref, br_ref, ltri_ref, count_ref, h2_ref, route_ref, topw_ref)

    @pl.when(i == 0)
    def _():
        count_ref[...] = jnp.zeros(count_ref.shape, F32)

    @pl.when(i < n_prompt_tiles)
    def _():
        _route_tile(xp_ref[...], shp_ref[0], scp_ref[0], *rest)

    @pl.when(i >= n_prompt_tiles)
    def _():
        _route_tile(xs_ref[...], shs_ref[...], scs_ref[...], *rest)


def _router(xp_rows, mod_p, seq_len, xs_rows, mod_s, n2w, w_router, b_router):
    tile = ROUTER_TILE
    n_p, n_s = xp_rows.shape[0], xs_rows.shape[0]
    tp, ts = n_p // tile, n_s // tile
    n_total = n_p + n_s
    mod_p = mod_p.reshape(mod_p.shape[0], 1, mod_p.shape[1])
    seq_of = lambda i: (jnp.minimum(i, tp - 1) * tile) // seq_len
    wr_pad = jnp.pad(w_router, ((0, 0), (0, LANES - N_EXPERTS)))
    wr_hi = wr_pad.astype(BF16)
    wr_mid = (wr_pad - wr_hi.astype(F32)).astype(BF16)
    br_pad = jnp.pad(b_router, ((0, 0), (0, LANES - N_EXPERTS)), constant_values=-1e30)
    ltri = jnp.asarray(np.tril(np.ones((tile, tile), np.float32), -1), BF16)
    clamp_p = lambda i: jnp.minimum(i, tp - 1)
    clamp_s = lambda i: jnp.maximum(i - tp, 0)
    return pl.pallas_call(
        functools.partial(_router_kernel, n_prompt_tiles=tp),
        grid=(tp + ts,),
        in_specs=[pl.BlockSpec((tile, D_MODEL), lambda i: (clamp_p(i), 0)),
                  pl.BlockSpec((1, 1, D_MODEL), lambda i: (seq_of(i), 0, 3)),
                  pl.BlockSpec((1, 1, D_MODEL), lambda i: (seq_of(i), 0, 4)),
                  pl.BlockSpec((tile, D_MODEL), lambda i: (clamp_s(i), 0)),
                  pl.BlockSpec((tile, D_MODEL), lambda i: (clamp_s(i), 3)),
                  pl.BlockSpec((tile, D_MODEL), lambda i: (clamp_s(i), 4)),
                  _const_spec(n2w.shape), _const_spec(wr_hi.shape), _const_spec(wr_mid.shape),
                  _const_spec(br_pad.shape), _const_spec(ltri.shape)],
        out_specs=(pl.BlockSpec((PACK_ROWS, tile, LANES), lambda i: (0, i, 0)),
                   pl.BlockSpec((2 * TOP_K, tile), lambda i: (0, i)),
                   pl.BlockSpec((tile, LANES), lambda i: (i, 0)),
                   pl.BlockSpec((1, LANES), lambda i: (0, 0))),
        out_shape=(jax.ShapeDtypeStruct((PACK_ROWS, n_total, LANES), jnp.uint32),
                   jax.ShapeDtypeStruct((2 * TOP_K, n_total), jnp.int32),
                   jax.ShapeDtypeStruct((n_total, LANES), F32),
                   jax.ShapeDtypeStruct((1, LANES), F32)),
        compiler_params=pltpu.CompilerParams(dimension_semantics=("arbitrary",), vmem_limit_bytes=VMEM_LIMIT),
        name="router",
    )(xp_rows, mod_p, mod_p, xs_rows, mod_s, mod_s, n2w, wr_hi, wr_mid, br_pad, ltri)


def _gather_rows(table, idx):
    n = idx.shape[0]
    steps = n // SC_GATHER_WINDOW
    assert n % SC_GATHER_WINDOW == 0 and steps % SC_WORKERS == 0
    mesh = plsc.VectorSubcoreMesh(core_axis_name="c", subcore_axis_name="s")

    @functools.partial(pl.kernel, out_type=jax.ShapeDtypeStruct((n, table.shape[1]), table.dtype), mesh=mesh,
                       scratch_types=[])
    def gather_kernel(table_hbm, idx_hbm, out_hbm):
        def body(idx_vmem, out_vmem):
            pltpu.sync_copy(table_hbm.at[idx_vmem.at[0]], out_vmem)

        pltpu.emit_pipeline(
            body,
            grid=(steps,),
            in_specs=[pl.BlockSpec((1, SC_GATHER_WINDOW), lambda i: (0, i))],
            out_specs=[pl.BlockSpec((SC_GATHER_WINDOW, table.shape[1]), lambda i: (i, 0))],
            core_axis_name=("c", "s"),
            dimension_semantics=(pltpu.PARALLEL,),
        )(idx_hbm, out_hbm)

    return gather_kernel(table, idx.reshape(1, n))


def _dispatch_packed(table, slot_kt, n_slots):
    planes, n_tok, lanes = table.shape
    win = SC_GATHER_WINDOW
    blocks = n_tok // win
    assert n_tok % win == 0
    blocks_pad = _round_up(blocks, SC_WORKERS // math.gcd(SC_WORKERS, planes))
    n_spare = (blocks_pad - blocks) * TOP_K * win
    n_ext = n_slots + n_spare
    dest = jnp.transpose(slot_kt.reshape(TOP_K, blocks, win), (1, 0, 2))
    spare = n_slots + jnp.arange(n_spare, dtype=jnp.int32).reshape(blocks_pad - blocks, TOP_K, win)
    dest = jnp.concatenate([dest, spare], axis=0)[None] + (jnp.arange(planes, dtype=jnp.int32) * n_ext)[:, None, None, None]
    dest = dest.reshape(planes * blocks_pad * TOP_K, win)
    steps = planes * blocks_pad
    src_block = lambda g: (g // blocks_pad) * blocks + jnp.minimum(g % blocks_pad, blocks - 1)
    mesh = plsc.VectorSubcoreMesh(core_axis_name="c", subcore_axis_name="s")

    @functools.partial(pl.kernel, out_type=jax.ShapeDtypeStruct((planes * n_ext, lanes), table.dtype), mesh=mesh,
                       scratch_types=[])
    def scatter_kernel(table_hbm, dest_hbm, out_hbm):
        def body(rows_vmem, dest_vmem):
            for k in range(TOP_K):
                pltpu.sync_copy(rows_vmem, out_hbm.at[dest_vmem.at[k]])

        pltpu.emit_pipeline(
            body,
            grid=(steps,),
            in_specs=[pl.BlockSpec((win, lanes), lambda g: (src_block(g), 0)),
                      pl.BlockSpec((TOP_K, win), lambda g: (g, 0))],
            out_specs=[],
            core_axis_name=("c", "s"),
            dimension_semantics=(pltpu.PARALLEL,),
        )(table_hbm, dest_hbm)

    return scatter_kernel(table.reshape(planes * n_tok, lanes), dest).reshape(planes, n_ext, lanes)


def _gather_packed(table, rows):
    planes, n_table, lanes = table.shape
    idx = jnp.concatenate([rows + c * n_table for c in range(planes)])
    out = _gather_rows(table.reshape(planes * n_table, lanes), idx)
    return out.reshape(planes, rows.shape[0], lanes)


def _expert_weight_copies(e, w1_hbm, w2_hbm, w1s_ref, w2s_ref, sem):
    return (pltpu.make_async_copy(w1_hbm.at[e], w1s_ref, sem.at[0]),
            pltpu.make_async_copy(w2_hbm.at[e], w2s_ref, sem.at[1]))


def _experts_kernel(te_ref, tr_ref, nx_ref, nv_ref, xs_ref, w1_hbm, b1_ref, w2_hbm, b2_ref, ys_ref,
                    w1s_ref, w2s_ref, w1b_ref, w2b_ref, sem):
    i = pl.program_id(0)
    e = te_ref[i]
    new_expert = (i == 0) | (e != te_ref[jnp.maximum(i - 1, 0)])
    copies = functools.partial(_expert_weight_copies, w1_hbm=w1_hbm, w2_hbm=w2_hbm, w1s_ref=w1s_ref,
                               w2s_ref=w2s_ref, sem=sem)

    @pl.when(i == 0)
    def _():
        for c in copies(e):
            c.start()

    @pl.when(new_expert)
    def _():
        for c in copies(e):
            c.wait()
        w1b_ref[...] = w1s_ref[...].astype(BF16)
        w2b_ref[...] = w2s_ref[...].astype(BF16)

        @pl.when(nx_ref[i] >= 0)
        def _():
            for c in copies(nx_ref[i]):
                c.start()

    for h in range(SLOT_TILE // EXPERT_ROWS):
        rows_here = jnp.where(i < nv_ref[0], tr_ref[i] - h * EXPERT_ROWS, 0)
        xs_h = xs_ref.at[:, pl.ds(h * EXPERT_ROWS, EXPERT_ROWS)]
        ys_h = ys_ref.at[:, pl.ds(h * EXPERT_ROWS, EXPERT_ROWS)]

        @pl.when(rows_here > 0)
        def _(rows_here=rows_here, xs_h=xs_h, ys_h=ys_h):
            row = lax.broadcasted_iota(jnp.int32, (EXPERT_ROWS, D_MODEL), 0)
            x = jnp.where(row < rows_here, _load_packed(xs_h, EXPERT_ROWS), 0.0).astype(BF16)
            hu = jnp.dot(x, w1b_ref[...], preferred_element_type=F32) + b1_ref[0]
            x_glu = jnp.minimum(hu[:, :D_FF], SWIGLU_LIMIT)
            x_lin = jnp.clip(hu[:, D_FF:], -SWIGLU_LIMIT, SWIGLU_LIMIT)
            act = x_glu * jax.nn.sigmoid(SWIGLU_ALPHA * x_glu) * (x_lin + 1.0)
            _store_packed(ys_h, jnp.dot(act.astype(BF16), w2b_ref[...], preferred_element_type=F32) + b2_ref[0])

        @pl.when(rows_here <= 0)
        def _(ys_h=ys_h):
            for c in range(PACK_ROWS):
                ys_h[c] = jnp.zeros((EXPERT_ROWS, LANES), jnp.uint32)


def _experts(tile_expert, tile_rows, next_expert, n_valid, xs, w1, b1, w2, b2):
    n_tiles = tile_expert.shape[0]
    n_slots = n_tiles * SLOT_TILE
    grid_spec = pltpu.PrefetchScalarGridSpec(
        num_scalar_prefetch=4,
        grid=(n_tiles,),
        in_specs=[
            pl.BlockSpec((PACK_ROWS, SLOT_TILE, LANES), lambda i, te, tr, nx, nv: (0, i, 0)),
            pl.BlockSpec(memory_space=pl.ANY),
            pl.BlockSpec((1, 1, 2 * D_FF), lambda i, te, tr, nx, nv: (te[i], 0, 0)),
            pl.BlockSpec(memory_space=pl.ANY),
            pl.BlockSpec((1, 1, D_MODEL), lambda i, te, tr, nx, nv: (te[i], 0, 0)),
        ],
        out_specs=pl.BlockSpec((PACK_ROWS, SLOT_TILE, LANES), lambda i, te, tr, nx, nv: (0, i, 0)),
        scratch_shapes=[pltpu.VMEM((D_MODEL, 2 * D_FF), F32), pltpu.VMEM((D_FF, D_MODEL), F32),
                        pltpu.VMEM((D_MODEL, 2 * D_FF), BF16), pltpu.VMEM((D_FF, D_MODEL), BF16),
                        pltpu.SemaphoreType.DMA((2,))],
    )
    return pl.pallas_call(
        _experts_kernel,
        grid_spec=grid_spec,
        out_shape=jax.ShapeDtypeStruct((PACK_ROWS, n_slots, LANES), jnp.uint32),
        compiler_params=pltpu.CompilerParams(dimension_semantics=("arbitrary",), vmem_limit_bytes=VMEM_LIMIT),
        name="experts",
    )(tile_expert, tile_rows, next_expert, n_valid, xs, w1, b1.reshape(N_EXPERTS, 1, 2 * D_FF), w2,
      b2.reshape(N_EXPERTS, 1, D_MODEL))


def _combine_kernel(y4_ref, x1_ref, topw_ref, g2_ref, fw_ref, *rest, per_row_mod):
    o_ref = rest[-1]
    w = topw_ref[...]
    n = w.shape[0]
    ff = None
    for k in range(TOP_K):
        yk = w[:, k:k + 1] * _load_packed(y4_ref.at[:, k], n)
        ff = yk if ff is None else ff + yk
    g2 = g2_ref[...] if per_row_mod else g2_ref[0]
    x = x1_ref[...] + g2 * ff
    ms = jnp.mean(x * x, axis=-1, keepdims=True)
    o_ref[...] = x * lax.rsqrt(ms + NORM_EPS) * fw_ref[...]


def _combine(y4, y4_row0, x1, x1_row0, n_rows, topw, topw_row0, mod, rows_per_mod, fw, out_buf):
    tile = COMBINE_TILE
    y4_off, x1_off, tw_off = y4_row0 // tile, x1_row0 // tile, topw_row0 // tile
    per_row = rows_per_mod == 1
    if per_row:
        g2_spec = pl.BlockSpec((tile, D_MODEL), lambda i: (i + x1_off, 5))
    else:
        mod = mod.reshape(mod.shape[0], 1, mod.shape[1])
        g2_spec = pl.BlockSpec((1, 1, D_MODEL), lambda i: (((i + x1_off) * tile) // rows_per_mod, 0, 5))
    in_specs = [pl.BlockSpec((PACK_ROWS, TOP_K, tile, LANES), lambda i: (0, 0, i + y4_off, 0)),
                pl.BlockSpec((tile, D_MODEL), lambda i: (i + x1_off, 0)),
                pl.BlockSpec((tile, LANES), lambda i: (i + tw_off, 0)),
                g2_spec, _const_spec(fw.shape)]
    args = [y4, x1, topw, mod, fw]
    aliases = {}
    if out_buf is not None:
        in_specs.append(pl.BlockSpec(memory_space=pl.ANY))
        args.append(out_buf)
        aliases = {len(args) - 1: 0}
    return pl.pallas_call(
        functools.partial(_combine_kernel, per_row_mod=per_row),
        grid=(n_rows // tile,),
        in_specs=in_specs,
        out_specs=pl.BlockSpec((tile, D_MODEL), lambda i: (i + x1_off, 0)),
        out_shape=jax.ShapeDtypeStruct(x1.shape, F32),
        input_output_aliases=aliases,
        compiler_params=pltpu.CompilerParams(dimension_semantics=("arbitrary",), vmem_limit_bytes=VMEM_LIMIT),
        name="combine",
    )(*args)


def _routing_tables(route, counts, n_slots):
    padded = ((counts + SLOT_TILE - 1) // SLOT_TILE) * SLOT_TILE
    pend = jnp.cumsum(padded)
    poff = pend - padded
    expert_kt, rank_kt = route[:TOP_K], route[TOP_K:]
    experts = jnp.arange(N_EXPERTS, dtype=jnp.int32)
    start_kt = jnp.sum((expert_kt[None] == experts[:, None, None]).astype(jnp.int32) * poff[:, None, None], axis=0)
    slot_kt = start_kt + rank_kt
    n_tiles = n_slots // SLOT_TILE
    n_valid = (pend[-1] // SLOT_TILE).astype(jnp.int32)
    tile_row = jnp.minimum(jnp.arange(n_tiles, dtype=jnp.int32), n_valid - 1) * SLOT_TILE
    in_later = (pend[None, :] <= tile_row[:, None]).astype(jnp.int32)
    tile_e = jnp.sum(in_later, axis=1).astype(jnp.int32)
    is_e = (experts[None, :] == tile_e[:, None]).astype(jnp.int32)
    used_end = jnp.sum(is_e * (poff + counts)[None, :], axis=1)
    tile_rows = jnp.clip(used_end - tile_row, 0, SLOT_TILE).astype(jnp.int32)
    later_used = (experts[None, :] > tile_e[:, None]) & (counts[None, :] > 0)
    next_e = jnp.min(jnp.where(later_used, experts[None, :], N_EXPERTS), axis=1)
    next_e = jnp.where(next_e < N_EXPERTS, next_e, -1).astype(jnp.int32)
    return slot_kt, tile_e, tile_rows, next_e, n_valid.reshape(1)


def _round_up(n, m):
    return ((n + m - 1) // m) * m


def kernel(x_prompt, x_sample, c_prompt, c_sample, state_ret, state_s5_re, state_s5_im, norm1_w, norm2_w, w_ada, b_ada, w_in, ret_norm_w, s5_lam_re, s5_lam_im, s5_log_dt, s5_b_re, s5_b_im, s5_c_re, s5_c_im, s5_d, w_glu, b_glu, s5_norm_w, w_out, w_router, b_router, w1, b1, w2, b2, final_w):
    bp, lp, _ = x_prompt.shape
    bs, ls, _ = x_sample.shape
    assert norm1_w.shape[0] == 1, "single-layer model"
    n_p, n_s = bp * lp, bs * ls
    n_tok = n_p + n_s

    mod = _ada(jnp.concatenate([c_prompt, c_sample], axis=0), w_ada[0], b_ada[0])
    mod_p, mod_s = mod[:bp], jnp.repeat(mod[bp:], ls, axis=0)

    lbr, lbi, bbr, bbi = _s5prep(s5_lam_re[0], s5_lam_im[0], s5_log_dt[0], s5_b_re[0], s5_b_im[0])
    bmat = jnp.concatenate([_block_diag(bbr), _block_diag(bbi)], axis=-1).astype(BF16)
    cre = _block_diag(jnp.transpose(s5_c_re[0], (0, 2, 1))).astype(BF16)
    cim = _block_diag(jnp.transpose(-s5_c_im[0], (0, 2, 1))).astype(BF16)
    wts = dict(
        n1w=norm1_w, w_in=w_in[0].astype(BF16), rnw=ret_norm_w, bmat=bmat, cre=cre, cim=cim,
        lbr=lbr.reshape(1, SSM_CH), lbi=lbi.reshape(1, SSM_CH), dsk=s5_d[0].reshape(1, SSM_WIDTH),
        w_glu=w_glu[0].astype(BF16), b_glu=b_glu, snw=s5_norm_w, w_out=w_out[0].astype(BF16),
    )

    zero_states = (jnp.zeros((bp, RET_HEADS, HEAD_DIM, HEAD_DIM), F32), jnp.zeros((bp, SSM_CH), F32),
                   jnp.zeros((bp, SSM_CH), F32))
    x1_p, ret_p, re_p, im_p = _mixer(x_prompt, mod_p, np.arange(lp, dtype=np.float32), zero_states, wts,
                                     prompt=True)
    sample_states = (state_ret[0], state_s5_re[0].reshape(bs, SSM_CH), state_s5_im[0].reshape(bs, SSM_CH))
    x1_s, ret_s, re_s, im_s = _mixer(x_sample.reshape(n_s, D_MODEL), mod_s,
                                     PAST_LEN + np.arange(ls, dtype=np.float32), sample_states, wts, prompt=False)

    x1_p_rows = x1_p.reshape(n_p, D_MODEL)
    h2, route, topw, counts = _router(x1_p_rows, mod_p, lp, x1_s, mod_s, norm2_w, w_router[0], b_router)

    n_assign = n_tok * TOP_K
    gather_quantum = SC_GATHER_WINDOW * SC_WORKERS // PACK_ROWS
    assert n_assign % gather_quantum == 0
    n_slots = _round_up(_round_up(n_assign, SLOT_TILE) + N_EXPERTS * SLOT_TILE, gather_quantum)
    slot_kt, tile_e, tile_rows, next_e, n_valid = _routing_tables(route, counts[0, :N_EXPERTS].astype(jnp.int32),
                                                                  n_slots)
    xs = _dispatch_packed(h2, slot_kt, n_slots)
    ys = _experts(tile_e, tile_rows, next_e, n_valid, xs, w1[0], b1[0], w2[0], b2[0])
    fw = final_w.reshape(1, D_MODEL)
    y_p, y_s = None, None
    bounds = [r * (n_p // COMBINE_RANGES) for r in range(COMBINE_RANGES)] + [n_tok]
    for lo, hi in zip(bounds[:-1], bounds[1:]):
        y4 = _gather_packed(ys, slot_kt[:, lo:hi].reshape(-1)).reshape(PACK_ROWS, TOP_K, hi - lo, LANES)
        y_p = _combine(y4, 0, x1_p_rows, lo, min(hi, n_p) - lo, topw, lo, mod_p, lp, fw, y_p)
        if hi > n_p:
            y_s = _combine(y4, n_p - lo, x1_s, 0, n_s, topw, n_p, mod_s, 1, fw, None)

    g, p = SSM_GROUPS, SSM_STATE
    return (y_p.reshape(bp, lp, D_MODEL), y_s.reshape(bs, ls, D_MODEL),
            ret_p[None], re_p.reshape(1, bp, g, p), im_p.reshape(1, bp, g, p),
            ret_s[None], re_s.reshape(1, bs, g, p), im_s.reshape(1, bs, g, p))
```

```python
import functools
import math

import jax
import jax.numpy as jnp
import numpy as np
from jax import lax
from jax.experimental import pallas as pl
from jax.experimental.pallas import tpu as pltpu
from jax.experimental.pallas import tpu_sc as plsc

F32 = jnp.float32
BF16 = jnp.bfloat16

D_MODEL = 1024
PAST_LEN = 16384
RET_WIDTH = 512
RET_HEADS = 4
HEAD_DIM = 128
ROPE_BASE = 10000.0
SSM_WIDTH = 512
SSM_GROUP = 16
SSM_GROUPS = 32
SSM_STATE = 64
SSM_CH = SSM_GROUPS * SSM_STATE
IN_WIDTH = 4 * RET_WIDTH + SSM_WIDTH
N_EXPERTS = 32
TOP_K = 4
D_FF = 1024
SWIGLU_LIMIT = 7.0
SWIGLU_ALPHA = 1.702
NORM_EPS = 1e-6

LANES = 128
SUBLANES = 8
VMEM_LIMIT = 56 * 1024 * 1024

SEQ_PER_BLOCK = 8
SAMPLE_SEQ_PER_BLOCK = 16
SCAN_ELEMS = 8 * 1024
PROMPT_CHUNK = 64
S5_BLOCK_GROUPS = 8
N_S5_BLOCKS = SSM_GROUPS // S5_BLOCK_GROUPS
S5_BLOCK_IN = S5_BLOCK_GROUPS * SSM_GROUP
S5_BLOCK_CH = S5_BLOCK_GROUPS * SSM_STATE
ROUTER_TILE = 512
SLOT_TILE = 512
EXPERT_ROWS = 256
COMBINE_TILE = 256
COMBINE_RANGES = 2
SC_GATHER_WINDOW = 128
SC_WORKERS = 32


def _silu(x):
    return x * jax.nn.sigmoid(x)


def _ada_kernel(c_ref, w_ref, b_ref, o_ref):
    sh, sm = _split_bf16(_silu(c_ref[...]))
    wh, wm = _split_bf16(w_ref[...])
    o_ref[...] = (jnp.dot(sh, wh, preferred_element_type=F32)
                  + (jnp.dot(sh, wm, preferred_element_type=F32) + jnp.dot(sm, wh, preferred_element_type=F32))
                  + b_ref[...])


def _ada(c_all, w_ada, b_ada):
    n_rows, n_out = c_all.shape[0], w_ada.shape[1]
    tn = 1536
    return pl.pallas_call(
        _ada_kernel,
        grid=(n_out // tn,),
        in_specs=[
            pl.BlockSpec((n_rows, D_MODEL), lambda j: (0, 0)),
            pl.BlockSpec((D_MODEL, tn), lambda j: (0, j)),
            pl.BlockSpec((1, tn), lambda j: (0, j)),
        ],
        out_specs=pl.BlockSpec((n_rows, tn), lambda j: (0, j)),
        out_shape=jax.ShapeDtypeStruct((n_rows, n_out), F32),
        compiler_params=pltpu.CompilerParams(dimension_semantics=("arbitrary",), vmem_limit_bytes=VMEM_LIMIT),
        name="ada",
    )(c_all, w_ada, b_ada.reshape(1, n_out))


def _s5prep_kernel(lre_ref, lim_ref, ldt_ref, bre_ref, bim_ref, lbr_ref, lbi_ref, bbr_ref, bbi_ref):
    lam_re, lam_im = lre_ref[...], lim_ref[...]
    dt = jnp.exp(ldt_ref[...])
    mag = jnp.exp(lam_re * dt)
    ang = lam_im * dt
    lb_re, lb_im = mag * jnp.cos(ang), mag * jnp.sin(ang)
    den = lam_re * lam_re + lam_im * lam_im
    f_re = ((lb_re - 1.0) * lam_re + lb_im * lam_im) / den
    f_im = (lb_im * lam_re - (lb_re - 1.0) * lam_im) / den
    lbr_ref[...] = lb_re
    lbi_ref[...] = lb_im
    b_re, b_im = bre_ref[...], bim_ref[...]
    bbr_ref[...] = f_re[:, None, :] * b_re - f_im[:, None, :] * b_im
    bbi_ref[...] = f_re[:, None, :] * b_im + f_im[:, None, :] * b_re


def _s5prep(lam_re, lam_im, log_dt, b_re, b_im):
    g, p = lam_re.shape
    bt_re = jnp.transpose(b_re, (0, 2, 1))
    bt_im = jnp.transpose(b_im, (0, 2, 1))
    return pl.pallas_call(
        _s5prep_kernel,
        out_shape=(
            jax.ShapeDtypeStruct((g, p), F32), jax.ShapeDtypeStruct((g, p), F32),
            jax.ShapeDtypeStruct((g, SSM_GROUP, p), F32), jax.ShapeDtypeStruct((g, SSM_GROUP, p), F32),
        ),
        name="s5prep",
    )(lam_re, lam_im, log_dt.reshape(g, 1), bt_re, bt_im)


def _block_diag(blocks):
    _, r, c = blocks.shape
    b4 = blocks.reshape(N_S5_BLOCKS, S5_BLOCK_GROUPS, r, c)
    eye = jnp.eye(S5_BLOCK_GROUPS, dtype=blocks.dtype)
    out = b4[:, :, :, None, :] * eye[None, :, None, :, None]
    return out.reshape(N_S5_BLOCKS, S5_BLOCK_GROUPS * r, S5_BLOCK_GROUPS * c)


def _mixer_kernel(x_ref, mod_ref, n1w_ref, win_ref, cos_ref, sin_ref, dmask_ref, cdec_ref, sdec_ref,
                  rnw_ref, bmat_ref, cre_ref, cim_ref, lbr_ref, lbi_ref, dsk_ref, wglu_ref, bglu_ref,
                  snw_ref, wout_ref, sret0_ref, sre0_ref, sim0_ref,
                  x1_ref, sret_ref, sre_ref, sim_ref,
                  hb_ref, z_ref, zu_ref, oy_ref, utb_ref, bur0_ref, bur1_ref, bui0_ref, bui1_ref, ytb_ref, yb_ref,
                  *, n_seq, chunk, tile_rows, carry, chunk_decay):
    rows = n_seq * chunk
    seq_per_tile = tile_rows // chunk
    n_tiles = rows // tile_rows
    per_row_mod = mod_ref.shape[0] == rows

    def load_states():
        sret_ref[...] = sret0_ref[...]
        sre_ref[...] = sre0_ref[...]
        sim_ref[...] = sim0_ref[...]

    if carry:
        pl.when(pl.program_id(0) == 0)(load_states)
    else:
        load_states()

    n1w = n1w_ref[...]
    mod_rows = rows if per_row_mod else chunk
    for i in range(rows // mod_rows):
        r0 = i * mod_rows
        xb = _load_rows(x_ref, r0, mod_rows, chunk)
        if per_row_mod:
            sh = mod_ref[pl.ds(r0, mod_rows), pl.ds(0, D_MODEL)]
            sc = mod_ref[pl.ds(r0, mod_rows), pl.ds(D_MODEL, D_MODEL)]
        else:
            sh = mod_ref[pl.ds(i, 1), pl.ds(0, D_MODEL)]
            sc = mod_ref[pl.ds(i, 1), pl.ds(D_MODEL, D_MODEL)]
        ms = jnp.mean(xb * xb, axis=-1, keepdims=True)
        hn = xb * lax.rsqrt(ms + NORM_EPS) * n1w
        hb_ref[pl.ds(r0, mod_rows), :] = (hn * (1.0 + sc) + sh).astype(BF16)
    ret_w = 4 * RET_WIDTH
    z_ref[...] = jnp.dot(hb_ref[...], win_ref[:, pl.ds(0, ret_w)], preferred_element_type=F32)
    zu = jnp.dot(hb_ref[...], win_ref[:, pl.ds(ret_w, SSM_WIDTH)], preferred_element_type=F32)
    pitch = zu_ref.shape[1] // n_seq
    for c in range(SSM_WIDTH // LANES):
        for b in range(n_seq if pitch != chunk else 1):
            nb = chunk if pitch != chunk else rows
            zu_ref[c, pl.ds(b * pitch, nb), :] = zu[b * chunk:b * chunk + nb, c * LANES:(c + 1) * LANES]

    cos = cos_ref[...]
    sin = sin_ref[...]
    scale = HEAD_DIM ** -0.5
    if seq_per_tile > 1:
        row_id = lax.broadcasted_iota(jnp.int32, (tile_rows, HEAD_DIM), 0)

    def rope(t):
        return t * cos + pltpu.roll(t, HEAD_DIM // 2, 1) * sin

    def ret_tile(ti, c):
        r0 = pl.multiple_of(ti * tile_rows, tile_rows)
        for h in range(RET_HEADS):
            c0 = h * HEAD_DIM
            q = rope(z_ref[pl.ds(r0, tile_rows), pl.ds(c0, HEAD_DIM)])
            k = rope(z_ref[pl.ds(r0, tile_rows), pl.ds(RET_WIDTH + c0, HEAD_DIM)]) * scale
            v = z_ref[pl.ds(r0, tile_rows), pl.ds(2 * RET_WIDTH + c0, HEAD_DIM)]
            g = z_ref[pl.ds(r0, tile_rows), pl.ds(3 * RET_WIDTH + c0, HEAD_DIM)]
            kd = k * sdec_ref[h]
            if tile_rows < HEAD_DIM:
                pad = jnp.zeros((HEAD_DIM - tile_rows, HEAD_DIM), F32)
                k, v, kd = (jnp.concatenate([t, pad], axis=0) for t in (k, v, kd))
                if seq_per_tile > 1:
                    row_kv = lax.broadcasted_iota(jnp.int32, (HEAD_DIM, HEAD_DIM), 0)
            elif seq_per_tile > 1:
                row_kv = row_id
            qb, kb, vb = q.astype(BF16), k.astype(BF16), v.astype(BF16)
            s = lax.dot_general(qb, kb, (((1,), (1,)), ((), ())), preferred_element_type=F32) * dmask_ref[h]
            o = jnp.dot(s.astype(BF16), vb, preferred_element_type=F32)
            cross = None
            for si in range(seq_per_tile):
                sidx = ti * seq_per_tile + si
                st = sret_ref[sidx, h]
                cr = jnp.dot(qb, st.astype(BF16), preferred_element_type=F32)
                if seq_per_tile > 1:
                    in_seq = (row_id >= si * chunk) & (row_id < (si + 1) * chunk)
                    cross = jnp.where(in_seq, cr, 0.0 if cross is None else cross)
                    kds = jnp.where((row_kv >= si * chunk) & (row_kv < (si + 1) * chunk), kd, 0.0)
                else:
                    cross, kds = cr, kd
                upd = lax.dot_general(kds.astype(BF16), vb, (((0,), (0,)), ((), ())), preferred_element_type=F32)
                sret_ref[sidx, h] = st * chunk_decay[h] + upd
            o = o + cross * cdec_ref[h]
            o = o * lax.rsqrt(jnp.mean(o * o, axis=-1, keepdims=True) + NORM_EPS)
            o = o * rnw_ref[:, pl.ds(c0, HEAD_DIM)] * _silu(g)
            oy_ref[pl.ds(r0, tile_rows), pl.ds(c0, HEAD_DIM)] = o
        return c

    lax.fori_loop(0, n_tiles, ret_tile, 0, unroll=True)

    for t in range(chunk):
        for c in range(SSM_WIDTH // LANES):
            utb_ref[pl.ds(t * n_seq, n_seq), pl.ds(c * LANES, LANES)] = zu_ref[c, pl.ds(t, n_seq, stride=pitch), :]
    half_ch = SSM_CH // 2
    blk_per_half = N_S5_BLOCKS // 2
    bur_refs, bui_refs = (bur0_ref, bur1_ref), (bui0_ref, bui1_ref)
    for blk in range(N_S5_BLOCKS):
        hf, lcols = blk // blk_per_half, pl.ds((blk % blk_per_half) * S5_BLOCK_CH, S5_BLOCK_CH)
        ub = utb_ref[:, pl.ds(blk * S5_BLOCK_IN, S5_BLOCK_IN)].astype(BF16)
        bu = jnp.dot(ub, bmat_ref[blk], preferred_element_type=F32)
        bur_refs[hf][:, lcols] = bu[:, :S5_BLOCK_CH]
        bui_refs[hf][:, lcols] = bu[:, S5_BLOCK_CH:]

    scan_w = min(half_ch, SCAN_ELEMS // n_seq)
    for hf in range(2):
        bur_ref, bui_ref = bur_refs[hf], bui_refs[hf]
        for p in range(half_ch // scan_w):
            cols = pl.ds(p * scan_w, scan_w)
            gcols = pl.ds(hf * half_ch + p * scan_w, scan_w)
            lbr = jnp.broadcast_to(lbr_ref[:, gcols], (n_seq, scan_w))
            lbi = jnp.broadcast_to(lbi_ref[:, gcols], (n_seq, scan_w))
            hr, hi = sre_ref[:, gcols], sim_ref[:, gcols]
            for t in range(chunk):
                rws = pl.ds(t * n_seq, n_seq)
                hr, hi = (lbr * hr - lbi * hi + bur_ref[rws, cols], lbr * hi + lbi * hr + bui_ref[rws, cols])
                bur_ref[rws, cols] = hr
                bui_ref[rws, cols] = hi
            sre_ref[:, gcols] = hr
            sim_ref[:, gcols] = hi

    for blk in range(N_S5_BLOCKS):
        hf, lcols = blk // blk_per_half, pl.ds((blk % blk_per_half) * S5_BLOCK_CH, S5_BLOCK_CH)
        yb = jnp.dot(bur_refs[hf][:, lcols].astype(BF16), cre_ref[blk], preferred_element_type=F32)
        yb = yb + jnp.dot(bui_refs[hf][:, lcols].astype(BF16), cim_ref[blk], preferred_element_type=F32)
        ucols = pl.ds(blk * S5_BLOCK_IN, S5_BLOCK_IN)
        ytb_ref[:, ucols] = yb + dsk_ref[:, ucols] * utb_ref[:, ucols]
    for t in range(chunk):
        for c in range(SSM_WIDTH // LANES):
            yb_ref[c, pl.ds(t, n_seq, stride=pitch), :] = ytb_ref[pl.ds(t * n_seq, n_seq), pl.ds(c * LANES, LANES)]

    def seq_major(c):
        if pitch == chunk:
            return yb_ref[c]
        return jnp.concatenate([yb_ref[c, pl.ds(b * pitch, chunk), :] for b in range(n_seq)], axis=0)

    y = jnp.concatenate([seq_major(c) for c in range(SSM_WIDTH // LANES)], axis=1)
    y = jax.nn.gelu(y, approximate=True)
    gate = jnp.dot(y.astype(BF16), wglu_ref[...], preferred_element_type=F32) + bglu_ref[...]
    y = y * jax.nn.sigmoid(gate)
    y = y * lax.rsqrt(jnp.mean(y * y, axis=-1, keepdims=True) + NORM_EPS) * snw_ref[...]
    oy_ref[:, pl.ds(RET_WIDTH, SSM_WIDTH)] = y

    mix = jnp.dot(oy_ref[...].astype(BF16), wout_ref[...], preferred_element_type=F32)
    for i in range(rows // mod_rows):
        r0 = i * mod_rows
        if per_row_mod:
            g1 = mod_ref[pl.ds(r0, mod_rows), pl.ds(2 * D_MODEL, D_MODEL)]
        else:
            g1 = mod_ref[pl.ds(i, 1), pl.ds(2 * D_MODEL, D_MODEL)]
        _store_rows(x1_ref, r0, mod_rows, chunk,
                    _load_rows(x_ref, r0, mod_rows, chunk) + g1 * mix[r0:r0 + mod_rows])


def _load_rows(ref, r0, n, chunk):
    if len(ref.shape) == 2:
        return ref[pl.ds(r0, n), :]
    assert n == chunk and r0 % chunk == 0
    return ref[r0 // chunk]


def _store_rows(ref, r0, n, chunk, val):
    if len(ref.shape) == 2:
        ref[pl.ds(r0, n), :] = val
    else:
        assert n == chunk and r0 % chunk == 0
        ref[r0 // chunk] = val


def _seq_pitch(chunk):
    return chunk + SUBLANES if chunk % SUBLANES == 0 else chunk


def _const_spec(shape):
    nd = len(shape)
    return pl.BlockSpec(shape, lambda j, _n=nd: (0,) * _n)


def _decay_tables(chunk, tile_rows):
    f32 = np.float32
    log_gamma = np.log1p(-np.exp2(f32(-5.0) - np.arange(RET_HEADS, dtype=f32))).astype(f32)
    r = np.arange(tile_rows)
    seq, loc = r // chunk, (r % chunk).astype(f32)
    rel = loc[:, None] - loc[None, :]
    ok = (seq[:, None] == seq[None, :]) & (rel >= 0)
    dmask = np.where(ok[None], np.exp(np.where(ok, rel, f32(0.0))[None] * log_gamma[:, None, None]), f32(0.0))
    if tile_rows < HEAD_DIM:
        dmask = np.pad(dmask, ((0, 0), (0, 0), (0, HEAD_DIM - tile_rows)))
    cdec = np.exp((loc[None, :] + f32(1.0)) * log_gamma[:, None])
    sdec = np.exp((f32(chunk) - f32(1.0) - loc)[None, :] * log_gamma[:, None])
    bcast = lambda t: np.ascontiguousarray(np.broadcast_to(t[:, :, None], (RET_HEADS, tile_rows, HEAD_DIM)))
    return dmask.astype(f32), bcast(cdec.astype(f32)), bcast(sdec.astype(f32))


def _rope_tables(pos):
    f32 = np.float32
    half = HEAD_DIM // 2
    inv_freq = (f32(ROPE_BASE) ** (-np.arange(half, dtype=f32) / f32(half))).astype(f32)
    ang = (pos.astype(f32)[:, None] * inv_freq[None, :]).astype(f32)
    cos, sin = np.cos(ang).astype(f32), np.sin(ang).astype(f32)
    return np.concatenate([cos, cos], axis=-1), np.concatenate([-sin, sin], axis=-1)


def _mixer(x, mod, pos, states, wts, *, prompt):
    n_seq = SEQ_PER_BLOCK if prompt else SAMPLE_SEQ_PER_BLOCK
    if prompt:
        n_total, seq_len, _ = x.shape
        assert n_total == n_seq
        chunk, tile_rows, n_steps = PROMPT_CHUNK, PROMPT_CHUNK, seq_len // PROMPT_CHUNK
        x_spec = pl.BlockSpec((n_seq, chunk, D_MODEL), lambda j: (0, j, 0))
        mod_spec = pl.BlockSpec((n_seq, 3 * D_MODEL), lambda j: (0, 0))
        tab_spec = pl.BlockSpec((chunk, HEAD_DIM), lambda j: (j, 0))
        seq_map = lambda j: 0
    else:
        chunk = pos.shape[0]
        tile_rows = SUBLANES
        n_total = x.shape[0] // chunk
        n_steps = n_total // n_seq
        x_spec = pl.BlockSpec((n_seq * chunk, D_MODEL), lambda j: (j, 0))
        mod_spec = pl.BlockSpec((n_seq * chunk, 3 * D_MODEL), lambda j: (j, 0))
        tab_spec = _const_spec((tile_rows, HEAD_DIM))
        seq_map = lambda j: j
    rows = n_seq * chunk
    cos, sin = _rope_tables(pos)
    if not prompt:
        reps = tile_rows // chunk
        cos, sin = np.tile(cos, (reps, 1)), np.tile(sin, (reps, 1))
    dmask, cdec, sdec = _decay_tables(chunk, tile_rows)
    chunk_decay = tuple(float(math.exp(chunk * math.log1p(-2.0 ** (-5.0 - h)))) for h in range(RET_HEADS))
    sret0, sre0, sim0 = states

    st_ret_spec = pl.BlockSpec((n_seq, RET_HEADS, HEAD_DIM, HEAD_DIM), lambda j: (seq_map(j), 0, 0, 0))
    st_s5_spec = pl.BlockSpec((n_seq, SSM_CH), lambda j: (seq_map(j), 0))
    consts = [dmask, cdec, sdec, wts["rnw"], wts["bmat"], wts["cre"], wts["cim"], wts["lbr"], wts["lbi"],
              wts["dsk"], wts["w_glu"], wts["b_glu"], wts["snw"], wts["w_out"]]
    args = [x, mod, wts["n1w"], wts["w_in"], cos, sin] + consts + [sret0, sre0, sim0]
    in_specs = ([x_spec, mod_spec, _const_spec(wts["n1w"].shape), _const_spec(wts["w_in"].shape), tab_spec, tab_spec]
                + [_const_spec(a.shape) for a in consts] + [st_ret_spec, st_s5_spec, st_s5_spec])

    kern = functools.partial(_mixer_kernel, n_seq=n_seq, chunk=chunk, tile_rows=tile_rows, carry=prompt,
                             chunk_decay=chunk_decay)
    out_shape = (
        jax.ShapeDtypeStruct(x.shape, F32),
        jax.ShapeDtypeStruct((n_total, RET_HEADS, HEAD_DIM, HEAD_DIM), F32),
        jax.ShapeDtypeStruct((n_total, SSM_CH), F32),
        jax.ShapeDtypeStruct((n_total, SSM_CH), F32),
    )
    scratch = [
        pltpu.VMEM((rows, D_MODEL), BF16),
        pltpu.VMEM((rows, 4 * RET_WIDTH), F32),
        pltpu.VMEM((SSM_WIDTH // LANES, n_seq * _seq_pitch(chunk), LANES), F32),
        pltpu.VMEM((rows, D_MODEL), F32),
        pltpu.VMEM((rows, SSM_WIDTH), F32),
        pltpu.VMEM((rows, SSM_CH // 2), F32),
        pltpu.VMEM((rows, SSM_CH // 2), F32),
        pltpu.VMEM((rows, SSM_CH // 2), F32),
        pltpu.VMEM((rows, SSM_CH // 2), F32),
        pltpu.VMEM((rows, SSM_WIDTH), F32),
        pltpu.VMEM((SSM_WIDTH // LANES, n_seq * _seq_pitch(chunk), LANES), F32),
    ]
    return pl.pallas_call(
        kern,
        grid=(n_steps,),
        in_specs=in_specs,
        out_specs=(x_spec, st_ret_spec, st_s5_spec, st_s5_spec),
        out_shape=out_shape,
        scratch_shapes=scratch,
        compiler_params=pltpu.CompilerParams(dimension_semantics=("arbitrary",), vmem_limit_bytes=VMEM_LIMIT),
        name="mixer_prompt" if prompt else "mixer_sample",
    )(*args)


PACK_ROWS = D_MODEL // (2 * LANES)


def _store_packed(ref, x):
    half = D_MODEL // 2
    bits = lax.bitcast_convert_type(x.astype(BF16).astype(F32), jnp.uint32)
    words = bits[:, :half] | (bits[:, half:] >> 16)
    for c in range(PACK_ROWS):
        ref[c] = words[:, c * LANES:(c + 1) * LANES]


def _load_packed(ref, n, first_row=0, row_stride=1):
    hi, lo = [], []
    for c in range(PACK_ROWS):
        w = ref[c] if row_stride == 1 else ref[c, pl.ds(first_row, n, stride=row_stride), :]
        hi.append(lax.bitcast_convert_type(w & jnp.uint32(0xFFFF0000), F32))
        lo.append(lax.bitcast_convert_type(w << 16, F32))
    return jnp.concatenate(hi + lo, axis=1)


def _split_bf16(x):
    hi = x.astype(BF16)
    return hi, (x - hi.astype(F32)).astype(BF16)


def _route_tile(x, sh, sc, n2w_ref, wrh_ref, wrm_ref, br_ref, ltri_ref, count_ref, h2_ref, route_ref, topw_ref):
    ms = jnp.mean(x * x, axis=-1, keepdims=True)
    h2 = x * lax.rsqrt(ms + NORM_EPS) * n2w_ref[...] * (1.0 + sc) + sh
    _store_packed(h2_ref, h2)
    hh, hm = _split_bf16(h2)
    logits = (jnp.dot(hh, wrh_ref[...], preferred_element_type=F32)
              + (jnp.dot(hh, wrm_ref[...], preferred_element_type=F32)
                 + jnp.dot(hm, wrh_ref[...], preferred_element_type=F32))) + br_ref[...]
    lane = lax.broadcasted_iota(jnp.int32, logits.shape, 1)
    work = logits
    vals, idxs = [], []
    for _ in range(TOP_K):
        m = jnp.max(work, axis=-1, keepdims=True)
        idx = jnp.min(jnp.where(work == m, lane, LANES), axis=-1, keepdims=True)
        vals.append(m)
        idxs.append(idx)
        work = jnp.where(lane == idx, -jnp.inf, work)
    exps = [jnp.exp(v - vals[0]) for v in vals]
    tot = exps[0] + exps[1] + exps[2] + exps[3]
    topw = jnp.zeros(logits.shape, F32)
    for k in range(TOP_K):
        topw = jnp.where(lane == k, exps[k] / tot, topw)
    topw_ref[...] = topw

    onehot = [(lane == idxs[k]).astype(F32) for k in range(TOP_K)]
    chosen = onehot[0] + onehot[1] + onehot[2] + onehot[3]
    before = jnp.dot(ltri_ref[...], chosen.astype(BF16), preferred_element_type=F32) + count_ref[...]
    info = jnp.zeros(logits.shape, jnp.int32)
    for k in range(TOP_K):
        rank = jnp.sum(onehot[k] * before, axis=-1, keepdims=True).astype(jnp.int32)
        info = jnp.where(lane == k, idxs[k], info)
        info = jnp.where(lane == TOP_K + k, rank, info)
    route_ref[...] = jnp.transpose(info)[:2 * TOP_K, :]
    count_ref[...] = count_ref[...] + jnp.sum(chosen, axis=0, keepdims=True)


def _router_kernel(xp_ref, shp_ref, scp_ref, xs_ref, shs_ref, scs_ref, n2w_ref, wrh_ref, wrm_ref, br_ref, ltri_ref,
                   h2_ref, route_ref, topw_ref, count_ref, *, n_prompt_tiles):
    i = pl.program_id(0)
    rest = (n2w_ref, wrh_ref, wrm_ref, br_ref, ltri_ref, count_ref, h2_ref, route_ref, topw_ref)

    @pl.when(i == 0)
    def _():
        count_ref[...] = jnp.zeros(count_ref.shape, F32)

    @pl.when(i < n_prompt_tiles)
    def _():
        _route_tile(xp_ref[...], shp_ref[0], scp_ref[0], *rest)

    @pl.when(i >= n_prompt_tiles)
    def _():
        _route_tile(xs_ref[...], shs_ref[...], scs_ref[...], *rest)


def _router(xp_rows, mod_p, seq_len, xs_rows, mod_s, n2w, w_router, b_router):
    tile = ROUTER_TILE
    n_p, n_s = xp_rows.shape[0], xs_rows.shape[0]
    tp, ts = n_p // tile, n_s // tile
    n_total = n_p + n_s
    mod_p = mod_p.reshape(mod_p.shape[0], 1, mod_p.shape[1])
    seq_of = lambda i: (jnp.minimum(i, tp - 1) * tile) // seq_len
    wr_pad = jnp.pad(w_router, ((0, 0), (0, LANES - N_EXPERTS)))
    wr_hi = wr_pad.astype(BF16)
    wr_mid = (wr_pad - wr_hi.astype(F32)).astype(BF16)
    br_pad = jnp.pad(b_router, ((0, 0), (0, LANES - N_EXPERTS)), constant_values=-1e30)
    ltri = jnp.asarray(np.tril(np.ones((tile, tile), np.float32), -1), BF16)
    clamp_p = lambda i: jnp.minimum(i, tp - 1)
    clamp_s = lambda i: jnp.maximum(i - tp, 0)
    return pl.pallas_call(
        functools.partial(_router_kernel, n_prompt_tiles=tp),
        grid=(tp + ts,),
        in_specs=[pl.BlockSpec((tile, D_MODEL), lambda i: (clamp_p(i), 0)),
                  pl.BlockSpec((1, 1, D_MODEL), lambda i: (seq_of(i), 0, 3)),
                  pl.BlockSpec((1, 1, D_MODEL), lambda i: (seq_of(i), 0, 4)),
                  pl.BlockSpec((tile, D_MODEL), lambda i: (clamp_s(i), 0)),
                  pl.BlockSpec((tile, D_MODEL), lambda i: (clamp_s(i), 3)),
                  pl.BlockSpec((tile, D_MODEL), lambda i: (clamp_s(i), 4)),
                  _const_spec(n2w.shape), _const_spec(wr_hi.shape), _const_spec(wr_mid.shape),
                  _const_spec(br_pad.shape), _const_spec(ltri.shape)],
        out_specs=(pl.BlockSpec((PACK_ROWS, tile, LANES), lambda i: (0, i, 0)),
                   pl.BlockSpec((2 * TOP_K, tile), lambda i: (0, i)),
                   pl.BlockSpec((tile, LANES), lambda i: (i, 0)),
                   pl.BlockSpec((1, LANES), lambda i: (0, 0))),
        out_shape=(jax.ShapeDtypeStruct((PACK_ROWS, n_total, LANES), jnp.uint32),
                   jax.ShapeDtypeStruct((2 * TOP_K, n_total), jnp.int32),
                   jax.ShapeDtypeStruct((n_total, LANES), F32),
                   jax.ShapeDtypeStruct((1, LANES), F32)),
        compiler_params=pltpu.CompilerParams(dimension_semantics=("arbitrary",), vmem_limit_bytes=VMEM_LIMIT),
        name="router",
    )(xp_rows, mod_p, mod_p, xs_rows, mod_s, mod_s, n2w, wr_hi, wr_mid, br_pad, ltri)


def _gather_rows(table, idx):
    n = idx.shape[0]
    steps = n // SC_GATHER_WINDOW
    assert n % SC_GATHER_WINDOW == 0 and steps % SC_WORKERS == 0
    mesh = plsc.VectorSubcoreMesh(core_axis_name="c", subcore_axis_name="s")

    @functools.partial(pl.kernel, out_type=jax.ShapeDtypeStruct((n, table.shape[1]), table.dtype), mesh=mesh,
                       scratch_types=[])
    def gather_kernel(table_hbm, idx_hbm, out_hbm):
        def body(idx_vmem, out_vmem):
            pltpu.sync_copy(table_hbm.at[idx_vmem.at[0]], out_vmem)

        pltpu.emit_pipeline(
            body,
            grid=(steps,),
            in_specs=[pl.BlockSpec((1, SC_GATHER_WINDOW), lambda i: (0, i))],
            out_specs=[pl.BlockSpec((SC_GATHER_WINDOW, table.shape[1]), lambda i: (i, 0))],
            core_axis_name=("c", "s"),
            dimension_semantics=(pltpu.PARALLEL,),
        )(idx_hbm, out_hbm)

    return gather_kernel(table, idx.reshape(1, n))


def _dispatch_packed(table, slot_kt, n_slots):
    planes, n_tok, lanes = table.shape
    win = SC_GATHER_WINDOW
    blocks = n_tok // win
    assert n_tok % win == 0
    blocks_pad = _round_up(blocks, SC_WORKERS // math.gcd(SC_WORKERS, planes))
    n_spare = (blocks_pad - blocks) * TOP_K * win
    n_ext = n_slots + n_spare
    dest = jnp.transpose(slot_kt.reshape(TOP_K, blocks, win), (1, 0, 2))
    spare = n_slots + jnp.arange(n_spare, dtype=jnp.int32).reshape(blocks_pad - blocks, TOP_K, win)
    dest = jnp.concatenate([dest, spare], axis=0)[None] + (jnp.arange(planes, dtype=jnp.int32) * n_ext)[:, None, None, None]
    dest = dest.reshape(planes * blocks_pad * TOP_K, win)
    steps = planes * blocks_pad
    src_block = lambda g: (g // blocks_pad) * blocks + jnp.minimum(g % blocks_pad, blocks - 1)
    mesh = plsc.VectorSubcoreMesh(core_axis_name="c", subcore_axis_name="s")

    @functools.partial(pl.kernel, out_type=jax.ShapeDtypeStruct((planes * n_ext, lanes), table.dtype), mesh=mesh,
                       scratch_types=[])
    def scatter_kernel(table_hbm, dest_hbm, out_hbm):
        def body(rows_vmem, dest_vmem):
            for k in range(TOP_K):
                pltpu.sync_copy(rows_vmem, out_hbm.at[dest_vmem.at[k]])

        pltpu.emit_pipeline(
            body,
            grid=(steps,),
            in_specs=[pl.BlockSpec((win, lanes), lambda g: (src_block(g), 0)),
                      pl.BlockSpec((TOP_K, win), lambda g: (g, 0))],
            out_specs=[],
            core_axis_name=("c", "s"),
            dimension_semantics=(pltpu.PARALLEL,),
        )(table_hbm, dest_hbm)

    return scatter_kernel(table.reshape(planes * n_tok, lanes), dest).reshape(planes, n_ext, lanes)


def _gather_packed(table, rows):
    planes, n_table, lanes = table.shape
    idx = jnp.concatenate([rows + c * n_table for c in range(planes)])
    out = _gather_rows(table.reshape(planes * n_table, lanes), idx)
    return out.reshape(planes, rows.shape[0], lanes)


def _expert_weight_copies(e, w1_hbm, w2_hbm, w1s_ref, w2s_ref, sem):
    return (pltpu.make_async_copy(w1_hbm.at[e], w1s_ref, sem.at[0]),
            pltpu.make_async_copy(w2_hbm.at[e], w2s_ref, sem.at[1]))


def _experts_kernel(te_ref, tr_ref, nx_ref, nv_ref, xs_ref, w1_hbm, b1_ref, w2_hbm, b2_ref, ys_ref,
                    w1s_ref, w2s_ref, w1b_ref, w2b_ref, sem):
    i = pl.program_id(0)
    e = te_ref[i]
    new_expert = (i == 0) | (e != te_ref[jnp.maximum(i - 1, 0)])
    copies = functools.partial(_expert_weight_copies, w1_hbm=w1_hbm, w2_hbm=w2_hbm, w1s_ref=w1s_ref,
                               w2s_ref=w2s_ref, sem=sem)

    @pl.when(i == 0)
    def _():
        for c in copies(e):
            c.start()

    @pl.when(new_expert)
    def _():
        for c in copies(e):
            c.wait()
        w1b_ref[...] = w1s_ref[...].astype(BF16)
        w2b_ref[...] = w2s_ref[...].astype(BF16)

        @pl.when(nx_ref[i] >= 0)
        def _():
            for c in copies(nx_ref[i]):
                c.start()

    def expert_pass(r0, n_rows, rows_valid):
        xs_v, ys_v = xs_ref.at[:, pl.ds(r0, n_rows)], ys_ref.at[:, pl.ds(r0, n_rows)]
        row = lax.broadcasted_iota(jnp.int32, (n_rows, D_MODEL), 0)
        x = jnp.where(row < rows_valid, _load_packed(xs_v, n_rows), 0.0).astype(BF16)
        hu = jnp.dot(x, w1b_ref[...], preferred_element_type=F32) + b1_ref[0]
        x_glu = jnp.minimum(hu[:, :D_FF], SWIGLU_LIMIT)
        x_lin = jnp.clip(hu[:, D_FF:], -SWIGLU_LIMIT, SWIGLU_LIMIT)
        act = x_glu * jax.nn.sigmoid(SWIGLU_ALPHA * x_glu) * (x_lin + 1.0)
        _store_packed(ys_v, jnp.dot(act.astype(BF16), w2b_ref[...], preferred_element_type=F32) + b2_ref[0])

    def zero_rows(r0, n_rows):
        for c in range(PACK_ROWS):
            ys_ref[c, pl.ds(r0, n_rows), :] = jnp.zeros((n_rows, LANES), jnp.uint32)

    half = EXPERT_ROWS // 2
    for h in range(SLOT_TILE // EXPERT_ROWS):
        r0 = h * EXPERT_ROWS
        rows_here = jnp.where(i < nv_ref[0], tr_ref[i] - r0, 0)

        @pl.when(rows_here > half)
        def _(r0=r0, rows_here=rows_here):
            expert_pass(r0, EXPERT_ROWS, rows_here)

        @pl.when((rows_here > 0) & (rows_here <= half))
        def _(r0=r0, rows_here=rows_here):
            expert_pass(r0, half, rows_here)
            zero_rows(r0 + half, half)

        @pl.when(rows_here <= 0)
        def _(r0=r0):
            zero_rows(r0, EXPERT_ROWS)


def _experts(tile_expert, tile_rows, next_expert, n_valid, xs, w1, b1, w2, b2):
    n_tiles = tile_expert.shape[0]
    n_slots = n_tiles * SLOT_TILE
    grid_spec = pltpu.PrefetchScalarGridSpec(
        num_scalar_prefetch=4,
        grid=(n_tiles,),
        in_specs=[
            pl.BlockSpec((PACK_ROWS, SLOT_TILE, LANES), lambda i, te, tr, nx, nv: (0, i, 0)),
            pl.BlockSpec(memory_space=pl.ANY),
            pl.BlockSpec((1, 1, 2 * D_FF), lambda i, te, tr, nx, nv: (te[i], 0, 0)),
            pl.BlockSpec(memory_space=pl.ANY),
            pl.BlockSpec((1, 1, D_MODEL), lambda i, te, tr, nx, nv: (te[i], 0, 0)),
        ],
        out_specs=pl.BlockSpec((PACK_ROWS, SLOT_TILE, LANES), lambda i, te, tr, nx, nv: (0, i, 0)),
        scratch_shapes=[pltpu.VMEM((D_MODEL, 2 * D_FF), F32), pltpu.VMEM((D_FF, D_MODEL), F32),
                        pltpu.VMEM((D_MODEL, 2 * D_FF), BF16), pltpu.VMEM((D_FF, D_MODEL), BF16),
                        pltpu.SemaphoreType.DMA((2,))],
    )
    return pl.pallas_call(
        _experts_kernel,
        grid_spec=grid_spec,
        out_shape=jax.ShapeDtypeStruct((PACK_ROWS, n_slots, LANES), jnp.uint32),
        compiler_params=pltpu.CompilerParams(dimension_semantics=("arbitrary",), vmem_limit_bytes=VMEM_LIMIT),
        name="experts",
    )(tile_expert, tile_rows, next_expert, n_valid, xs, w1, b1.reshape(N_EXPERTS, 1, 2 * D_FF), w2,
      b2.reshape(N_EXPERTS, 1, D_MODEL))


def _combine_kernel(y4_ref, x1_ref, topw_ref, g2_ref, fw_ref, *rest, per_row_mod):
    o_ref = rest[-1]
    w = topw_ref[...]
    n = w.shape[0]
    ff = None
    for k in range(TOP_K):
        yk = w[:, k:k + 1] * _load_packed(y4_ref.at[:, k], n)
        ff = yk if ff is None else ff + yk
    g2 = g2_ref[...] if per_row_mod else g2_ref[0]
    x = x1_ref[...] + g2 * ff
    ms = jnp.mean(x * x, axis=-1, keepdims=True)
    o_ref[...] = x * lax.rsqrt(ms + NORM_EPS) * fw_ref[...]


def _combine(y4, y4_row0, x1, x1_row0, n_rows, topw, topw_row0, mod, rows_per_mod, fw, out_buf):
    tile = COMBINE_TILE
    y4_off, x1_off, tw_off = y4_row0 // tile, x1_row0 // tile, topw_row0 // tile
    per_row = rows_per_mod == 1
    if per_row:
        g2_spec = pl.BlockSpec((tile, D_MODEL), lambda i: (i + x1_off, 5))
    else:
        mod = mod.reshape(mod.shape[0], 1, mod.shape[1])
        g2_spec = pl.BlockSpec((1, 1, D_MODEL), lambda i: (((i + x1_off) * tile) // rows_per_mod, 0, 5))
    in_specs = [pl.BlockSpec((PACK_ROWS, TOP_K, tile, LANES), lambda i: (0, 0, i + y4_off, 0)),
                pl.BlockSpec((tile, D_MODEL), lambda i: (i + x1_off, 0)),
                pl.BlockSpec((tile, LANES), lambda i: (i + tw_off, 0)),
                g2_spec, _const_spec(fw.shape)]
    args = [y4, x1, topw, mod, fw]
    aliases = {}
    if out_buf is not None:
        in_specs.append(pl.BlockSpec(memory_space=pl.ANY))
        args.append(out_buf)
        aliases = {len(args) - 1: 0}
    return pl.pallas_call(
        functools.partial(_combine_kernel, per_row_mod=per_row),
        grid=(n_rows // tile,),
        in_specs=in_specs,
        out_specs=pl.BlockSpec((tile, D_MODEL), lambda i: (i + x1_off, 0)),
        out_shape=jax.ShapeDtypeStruct(x1.shape, F32),
        input_output_aliases=aliases,
        compiler_params=pltpu.CompilerParams(dimension_semantics=("arbitrary",), vmem_limit_bytes=VMEM_LIMIT),
        name="combine",
    )(*args)


def _routing_tables(route, counts, n_slots):
    padded = ((counts + SLOT_TILE - 1) // SLOT_TILE) * SLOT_TILE
    pend = jnp.cumsum(padded)
    poff = pend - padded
    expert_kt, rank_kt = route[:TOP_K], route[TOP_K:]
    experts = jnp.arange(N_EXPERTS, dtype=jnp.int32)
    start_kt = jnp.sum((expert_kt[None] == experts[:, None, None]).astype(jnp.int32) * poff[:, None, None], axis=0)
    slot_kt = start_kt + rank_kt
    n_tiles = n_slots // SLOT_TILE
    n_valid = (pend[-1] // SLOT_TILE).astype(jnp.int32)
    tile_row = jnp.minimum(jnp.arange(n_tiles, dtype=jnp.int32), n_valid - 1) * SLOT_TILE
    in_later = (pend[None, :] <= tile_row[:, None]).astype(jnp.int32)
    tile_e = jnp.sum(in_later, axis=1).astype(jnp.int32)
    is_e = (experts[None, :] == tile_e[:, None]).astype(jnp.int32)
    used_end = jnp.sum(is_e * (poff + counts)[None, :], axis=1)
    tile_rows = jnp.clip(used_end - tile_row, 0, SLOT_TILE).astype(jnp.int32)
    later_used = (experts[None, :] > tile_e[:, None]) & (counts[None, :] > 0)
    next_e = jnp.min(jnp.where(later_used, experts[None, :], N_EXPERTS), axis=1)
    next_e = jnp.where(next_e < N_EXPERTS, next_e, -1).astype(jnp.int32)
    return slot_kt, tile_e, tile_rows, next_e, n_valid.reshape(1)


def _round_up(n, m):
    return ((n + m - 1) // m) * m


def kernel(x_prompt, x_sample, c_prompt, c_sample, state_ret, state_s5_re, state_s5_im, norm1_w, norm2_w, w_ada, b_ada, w_in, ret_norm_w, s5_lam_re, s5_lam_im, s5_log_dt, s5_b_re, s5_b_im, s5_c_re, s5_c_im, s5_d, w_glu, b_glu, s5_norm_w, w_out, w_router, b_router, w1, b1, w2, b2, final_w):
    bp, lp, _ = x_prompt.shape
    bs, ls, _ = x_sample.shape
    assert norm1_w.shape[0] == 1, "single-layer model"
    n_p, n_s = bp * lp, bs * ls
    n_tok = n_p + n_s

    mod = _ada(jnp.concatenate([c_prompt, c_sample], axis=0), w_ada[0], b_ada[0])
    mod_p, mod_s = mod[:bp], jnp.repeat(mod[bp:], ls, axis=0)

    lbr, lbi, bbr, bbi = _s5prep(s5_lam_re[0], s5_lam_im[0], s5_log_dt[0], s5_b_re[0], s5_b_im[0])
    bmat = jnp.concatenate([_block_diag(bbr), _block_diag(bbi)], axis=-1).astype(BF16)
    cre = _block_diag(jnp.transpose(s5_c_re[0], (0, 2, 1))).astype(BF16)
    cim = _block_diag(jnp.transpose(-s5_c_im[0], (0, 2, 1))).astype(BF16)
    wts = dict(
        n1w=norm1_w, w_in=w_in[0].astype(BF16), rnw=ret_norm_w, bmat=bmat, cre=cre, cim=cim,
        lbr=lbr.reshape(1, SSM_CH), lbi=lbi.reshape(1, SSM_CH), dsk=s5_d[0].reshape(1, SSM_WIDTH),
        w_glu=w_glu[0].astype(BF16), b_glu=b_glu, snw=s5_norm_w, w_out=w_out[0].astype(BF16),
    )

    zero_states = (jnp.zeros((bp, RET_HEADS, HEAD_DIM, HEAD_DIM), F32), jnp.zeros((bp, SSM_CH), F32),
                   jnp.zeros((bp, SSM_CH), F32))
    x1_p, ret_p, re_p, im_p = _mixer(x_prompt, mod_p, np.arange(lp, dtype=np.float32), zero_states, wts,
                                     prompt=True)
    sample_states = (state_ret[0], state_s5_re[0].reshape(bs, SSM_CH), state_s5_im[0].reshape(bs, SSM_CH))
    x1_s, ret_s, re_s, im_s = _mixer(x_sample.reshape(n_s, D_MODEL), mod_s,
                                     PAST_LEN + np.arange(ls, dtype=np.float32), sample_states, wts, prompt=False)

    x1_p_rows = x1_p.reshape(n_p, D_MODEL)
    h2, route, topw, counts = _router(x1_p_rows, mod_p, lp, x1_s, mod_s, norm2_w, w_router[0], b_router)

    n_assign = n_tok * TOP_K
    gather_quantum = SC_GATHER_WINDOW * SC_WORKERS // PACK_ROWS
    assert n_assign % gather_quantum == 0
    n_slots = _round_up(_round_up(n_assign, SLOT_TILE) + N_EXPERTS * SLOT_TILE, gather_quantum)
    slot_kt, tile_e, tile_rows, next_e, n_valid = _routing_tables(route, counts[0, :N_EXPERTS].astype(jnp.int32),
                                                                  n_slots)
    xs = _dispatch_packed(h2, slot_kt, n_slots)
    ys = _experts(tile_e, tile_rows, next_e, n_valid, xs, w1[0], b1[0], w2[0], b2[0])
    fw = final_w.reshape(1, D_MODEL)
    y_p, y_s = None, None
    bounds = [r * (n_p // COMBINE_RANGES) for r in range(COMBINE_RANGES)] + [n_tok]
    for lo, hi in zip(bounds[:-1], bounds[1:]):
        y4 = _gather_packed(ys, slot_kt[:, lo:hi].reshape(-1)).reshape(PACK_ROWS, TOP_K, hi - lo, LANES)
        y_p = _combine(y4, 0, x1_p_rows, lo, min(hi, n_p) - lo, topw, lo, mod_p, lp, fw, y_p)
        if hi > n_p:
            y_s = _combine(y4, n_p - lo, x1_s, 0, n_s, topw, n_p, mod_s, 1, fw, None)

    g, p = SSM_GROUPS, SSM_STATE
    return (y_p.reshape(bp, lp, D_MODEL), y_s.reshape(bs, ls, D_MODEL),
            ret_p[None], re_p.reshape(1, bp, g, p), im_p.reshape(1, bp, g, p),
            ret_s[None], re_s.reshape(1, bs, g, p), im_s.reshape(1, bs, g, p))
```

```python
import functools
import math

import jax
import jax.numpy as jnp
import numpy as np
from jax import lax
from jax.experimental import pallas as pl
from jax.experimental.pallas import tpu as pltpu
from jax.experimental.pallas import tpu_sc as plsc

F32 = jnp.float32
BF16 = jnp.bfloat16

D_MODEL = 1024
PAST_LEN = 16384
RET_WIDTH = 512
RET_HEADS = 4
HEAD_DIM = 128
ROPE_BASE = 10000.0
SSM_WIDTH = 512
SSM_GROUP = 16
SSM_GROUPS = 32
SSM_STATE = 64
SSM_CH = SSM_GROUPS * SSM_STATE
IN_WIDTH = 4 * RET_WIDTH + SSM_WIDTH
N_EXPERTS = 32
TOP_K = 4
D_FF = 1024
SWIGLU_LIMIT = 7.0
SWIGLU_ALPHA = 1.702
NORM_EPS = 1e-6

LANES = 128
SUBLANES = 8
VMEM_LIMIT = 56 * 1024 * 1024

SEQ_PER_BLOCK = 8
SAMPLE_SEQ_PER_BLOCK = 16
SCAN_ELEMS = 8 * 1024
PROMPT_CHUNK = 64
S5_BLOCK_GROUPS = 8
N_S5_BLOCKS = SSM_GROUPS // S5_BLOCK_GROUPS
S5_BLOCK_IN = S5_BLOCK_GROUPS * SSM_GROUP
S5_BLOCK_CH = S5_BLOCK_GROUPS * SSM_STATE
ROUTER_TILE = 512
SLOT_TILE = 512
EXPERT_ROWS = 256
COMBINE_TILE = 256
COMBINE_RANGES = 2
SC_GATHER_WINDOW = 128
SC_WORKERS = 32


def _silu(x):
    return x * jax.nn.sigmoid(x)


def _ada_kernel(c_ref, w_ref, b_ref, o_ref):
    sh, sm = _split_bf16(_silu(c_ref[...]))
    wh, wm = _split_bf16(w_ref[...])
    o_ref[...] = (jnp.dot(sh, wh, preferred_element_type=F32)
                  + (jnp.dot(sh, wm, preferred_element_type=F32) + jnp.dot(sm, wh, preferred_element_type=F32))
                  + b_ref[...])


def _ada(c_all, w_ada, b_ada):
    n_rows, n_out = c_all.shape[0], w_ada.shape[1]
    tn = 1536
    return pl.pallas_call(
        _ada_kernel,
        grid=(n_out // tn,),
        in_specs=[
            pl.BlockSpec((n_rows, D_MODEL), lambda j: (0, 0)),
            pl.BlockSpec((D_MODEL, tn), lambda j: (0, j)),
            pl.BlockSpec((1, tn), lambda j: (0, j)),
        ],
        out_specs=pl.BlockSpec((n_rows, tn), lambda j: (0, j)),
        out_shape=jax.ShapeDtypeStruct((n_rows, n_out), F32),
        compiler_params=pltpu.CompilerParams(dimension_semantics=("arbitrary",), vmem_limit_bytes=VMEM_LIMIT),
        name="ada",
    )(c_all, w_ada, b_ada.reshape(1, n_out))


def _s5prep_kernel(lre_ref, lim_ref, ldt_ref, bre_ref, bim_ref, lbr_ref, lbi_ref, bbr_ref, bbi_ref):
    lam_re, lam_im = lre_ref[...], lim_ref[...]
    dt = jnp.exp(ldt_ref[...])
    mag = jnp.exp(lam_re * dt)
    ang = lam_im * dt
    lb_re, lb_im = mag * jnp.cos(ang), mag * jnp.sin(ang)
    den = lam_re * lam_re + lam_im * lam_im
    f_re = ((lb_re - 1.0) * lam_re + lb_im * lam_im) / den
    f_im = (lb_im * lam_re - (lb_re - 1.0) * lam_im) / den
    lbr_ref[...] = lb_re
    lbi_ref[...] = lb_im
    b_re, b_im = bre_ref[...], bim_ref[...]
    bbr_ref[...] = f_re[:, None, :] * b_re - f_im[:, None, :] * b_im
    bbi_ref[...] = f_re[:, None, :] * b_im + f_im[:, None, :] * b_re


def _s5prep(lam_re, lam_im, log_dt, b_re, b_im):
    g, p = lam_re.shape
    bt_re = jnp.transpose(b_re, (0, 2, 1))
    bt_im = jnp.transpose(b_im, (0, 2, 1))
    return pl.pallas_call(
        _s5prep_kernel,
        out_shape=(
            jax.ShapeDtypeStruct((g, p), F32), jax.ShapeDtypeStruct((g, p), F32),
            jax.ShapeDtypeStruct((g, SSM_GROUP, p), F32), jax.ShapeDtypeStruct((g, SSM_GROUP, p), F32),
        ),
        name="s5prep",
    )(lam_re, lam_im, log_dt.reshape(g, 1), bt_re, bt_im)


def _block_diag(blocks):
    _, r, c = blocks.shape
    b4 = blocks.reshape(N_S5_BLOCKS, S5_BLOCK_GROUPS, r, c)
    eye = jnp.eye(S5_BLOCK_GROUPS, dtype=blocks.dtype)
    out = b4[:, :, :, None, :] * eye[None, :, None, :, None]
    return out.reshape(N_S5_BLOCKS, S5_BLOCK_GROUPS * r, S5_BLOCK_GROUPS * c)


def _mixer_kernel(x_ref, mod_ref, n1w_ref, win_ref, cos_ref, sin_ref, dmask_ref, cdec_ref, sdec_ref,
                  rnw_ref, bmat_ref, cre_ref, cim_ref, lbr_ref, lbi_ref, dsk_ref, wglu_ref, bglu_ref,
                  snw_ref, wout_ref, sret0_ref, sre0_ref, sim0_ref,
                  x1_ref, sret_ref, sre_ref, sim_ref,
                  hb_ref, z_ref, zu_ref, oy_ref, utb_ref, bur0_ref, bur1_ref, bui0_ref, bui1_ref, ytb_ref, yb_ref,
                  *, n_seq, chunk, tile_rows, carry, chunk_decay):
    rows = n_seq * chunk
    seq_per_tile = tile_rows // chunk
    n_tiles = rows // tile_rows
    per_row_mod = mod_ref.shape[0] == rows

    def load_states():
        sret_ref[...] = sret0_ref[...]
        sre_ref[...] = sre0_ref[...]
        sim_ref[...] = sim0_ref[...]

    if carry:
        pl.when(pl.program_id(0) == 0)(load_states)
    else:
        load_states()

    n1w = n1w_ref[...]
    mod_rows = rows if per_row_mod else chunk
    for i in range(rows // mod_rows):
        r0 = i * mod_rows
        xb = _load_rows(x_ref, r0, mod_rows, chunk)
        if per_row_mod:
            sh = mod_ref[pl.ds(r0, mod_rows), pl.ds(0, D_MODEL)]
            sc = mod_ref[pl.ds(r0, mod_rows), pl.ds(D_MODEL, D_MODEL)]
        else:
            sh = mod_ref[pl.ds(i, 1), pl.ds(0, D_MODEL)]
            sc = mod_ref[pl.ds(i, 1), pl.ds(D_MODEL, D_MODEL)]
        ms = jnp.mean(xb * xb, axis=-1, keepdims=True)
        hn = xb * lax.rsqrt(ms + NORM_EPS) * n1w
        hb_ref[pl.ds(r0, mod_rows), :] = (hn * (1.0 + sc) + sh).astype(BF16)
    ret_w = 4 * RET_WIDTH
    z_ref[...] = jnp.dot(hb_ref[...], win_ref[:, pl.ds(0, ret_w)], preferred_element_type=F32)
    zu = jnp.dot(hb_ref[...], win_ref[:, pl.ds(ret_w, SSM_WIDTH)], preferred_element_type=F32)
    pitch = zu_ref.shape[1] // n_seq
    for c in range(SSM_WIDTH // LANES):
        for b in range(n_seq if pitch != chunk else 1):
            nb = chunk if pitch != chunk else rows
            zu_ref[c, pl.ds(b * pitch, nb), :] = zu[b * chunk:b * chunk + nb, c * LANES:(c + 1) * LANES]

    cos = cos_ref[...]
    sin = sin_ref[...]
    scale = HEAD_DIM ** -0.5
    if seq_per_tile > 1:
        row_id = lax.broadcasted_iota(jnp.int32, (tile_rows, HEAD_DIM), 0)

    def rope(t):
        return t * cos + pltpu.roll(t, HEAD_DIM // 2, 1) * sin

    def ret_tile(ti, c):
        r0 = pl.multiple_of(ti * tile_rows, tile_rows)
        for h in range(RET_HEADS):
            c0 = h * HEAD_DIM
            q = rope(z_ref[pl.ds(r0, tile_rows), pl.ds(c0, HEAD_DIM)])
            k = rope(z_ref[pl.ds(r0, tile_rows), pl.ds(RET_WIDTH + c0, HEAD_DIM)]) * scale
            v = z_ref[pl.ds(r0, tile_rows), pl.ds(2 * RET_WIDTH + c0, HEAD_DIM)]
            g = z_ref[pl.ds(r0, tile_rows), pl.ds(3 * RET_WIDTH + c0, HEAD_DIM)]
            kd = k * sdec_ref[h]
            if tile_rows < HEAD_DIM:
                pad = jnp.zeros((HEAD_DIM - tile_rows, HEAD_DIM), F32)
                k, v, kd = (jnp.concatenate([t, pad], axis=0) for t in (k, v, kd))
                if seq_per_tile > 1:
                    row_kv = lax.broadcasted_iota(jnp.int32, (HEAD_DIM, HEAD_DIM), 0)
            elif seq_per_tile > 1:
                row_kv = row_id
            qb, kb, vb = q.astype(BF16), k.astype(BF16), v.astype(BF16)
            s = lax.dot_general(qb, kb, (((1,), (1,)), ((), ())), preferred_element_type=F32) * dmask_ref[h]
            o = jnp.dot(s.astype(BF16), vb, preferred_element_type=F32)
            cross = None
            for si in range(seq_per_tile):
                sidx = ti * seq_per_tile + si
                st = sret_ref[sidx, h]
                cr = jnp.dot(qb, st.astype(BF16), preferred_element_type=F32)
                if seq_per_tile > 1:
                    in_seq = (row_id >= si * chunk) & (row_id < (si + 1) * chunk)
                    cross = jnp.where(in_seq, cr, 0.0 if cross is None else cross)
                    kds = jnp.where((row_kv >= si * chunk) & (row_kv < (si + 1) * chunk), kd, 0.0)
                else:
                    cross, kds = cr, kd
                upd = lax.dot_general(kds.astype(BF16), vb, (((0,), (0,)), ((), ())), preferred_element_type=F32)
                sret_ref[sidx, h] = st * chunk_decay[h] + upd
            o = o + cross * cdec_ref[h]
            o = o * lax.rsqrt(jnp.mean(o * o, axis=-1, keepdims=True) + NORM_EPS)
            o = o * rnw_ref[:, pl.ds(c0, HEAD_DIM)] * _silu(g)
            oy_ref[pl.ds(r0, tile_rows), pl.ds(c0, HEAD_DIM)] = o
        return c

    lax.fori_loop(0, n_tiles, ret_tile, 0, unroll=True)

    for t in range(chunk):
        for c in range(SSM_WIDTH // LANES):
            utb_ref[pl.ds(t * n_seq, n_seq), pl.ds(c * LANES, LANES)] = zu_ref[c, pl.ds(t, n_seq, stride=pitch), :]
    half_ch = SSM_CH // 2
    blk_per_half = N_S5_BLOCKS // 2
    bur_refs, bui_refs = (bur0_ref, bur1_ref), (bui0_ref, bui1_ref)
    for blk in range(N_S5_BLOCKS):
        hf, lcols = blk // blk_per_half, pl.ds((blk % blk_per_half) * S5_BLOCK_CH, S5_BLOCK_CH)
        ub = utb_ref[:, pl.ds(blk * S5_BLOCK_IN, S5_BLOCK_IN)].astype(BF16)
        bu = jnp.dot(ub, bmat_ref[blk], preferred_element_type=F32)
        bur_refs[hf][:, lcols] = bu[:, :S5_BLOCK_CH]
        bui_refs[hf][:, lcols] = bu[:, S5_BLOCK_CH:]

    scan_w = min(half_ch, SCAN_ELEMS // n_seq)
    for hf in range(2):
        bur_ref, bui_ref = bur_refs[hf], bui_refs[hf]
        for p in range(half_ch // scan_w):
            cols = pl.ds(p * scan_w, scan_w)
            gcols = pl.ds(hf * half_ch + p * scan_w, scan_w)
            lbr = jnp.broadcast_to(lbr_ref[:, gcols], (n_seq, scan_w))
            lbi = jnp.broadcast_to(lbi_ref[:, gcols], (n_seq, scan_w))
            hr, hi = sre_ref[:, gcols], sim_ref[:, gcols]
            for t in range(chunk):
                rws = pl.ds(t * n_seq, n_seq)
                hr, hi = (lbr * hr - lbi * hi + bur_ref[rws, cols], lbr * hi + lbi * hr + bui_ref[rws, cols])
                bur_ref[rws, cols] = hr
                bui_ref[rws, cols] = hi
            sre_ref[:, gcols] = hr
            sim_ref[:, gcols] = hi

    for blk in range(N_S5_BLOCKS):
        hf, lcols = blk // blk_per_half, pl.ds((blk % blk_per_half) * S5_BLOCK_CH, S5_BLOCK_CH)
        yb = jnp.dot(bur_refs[hf][:, lcols].astype(BF16), cre_ref[blk], preferred_element_type=F32)
        yb = yb + jnp.dot(bui_refs[hf][:, lcols].astype(BF16), cim_ref[blk], preferred_element_type=F32)
        ucols = pl.ds(blk * S5_BLOCK_IN, S5_BLOCK_IN)
        ytb_ref[:, ucols] = yb + dsk_ref[:, ucols] * utb_ref[:, ucols]
    for t in range(chunk):
        for c in range(SSM_WIDTH // LANES):
            yb_ref[c, pl.ds(t, n_seq, stride=pitch), :] = ytb_ref[pl.ds(t * n_seq, n_seq), pl.ds(c * LANES, LANES)]

    def seq_major(c):
        if pitch == chunk:
            return yb_ref[c]
        return jnp.concatenate([yb_ref[c, pl.ds(b * pitch, chunk), :] for b in range(n_seq)], axis=0)

    y = jnp.concatenate([seq_major(c) for c in range(SSM_WIDTH // LANES)], axis=1)
    y = jax.nn.gelu(y, approximate=True)
    gate = jnp.dot(y.astype(BF16), wglu_ref[...], preferred_element_type=F32) + bglu_ref[...]
    y = y * jax.nn.sigmoid(gate)
    y = y * lax.rsqrt(jnp.mean(y * y, axis=-1, keepdims=True) + NORM_EPS) * snw_ref[...]
    oy_ref[:, pl.ds(RET_WIDTH, SSM_WIDTH)] = y

    mix = jnp.dot(oy_ref[...].astype(BF16), wout_ref[...], preferred_element_type=F32)
    for i in range(rows // mod_rows):
        r0 = i * mod_rows
        if per_row_mod:
            g1 = mod_ref[pl.ds(r0, mod_rows), pl.ds(2 * D_MODEL, D_MODEL)]
        else:
            g1 = mod_ref[pl.ds(i, 1), pl.ds(2 * D_MODEL, D_MODEL)]
        _store_rows(x1_ref, r0, mod_rows, chunk,
                    _load_rows(x_ref, r0, mod_rows, chunk) + g1 * mix[r0:r0 + mod_rows])


def _load_rows(ref, r0, n, chunk):
    if len(ref.shape) == 2:
        return ref[pl.ds(r0, n), :]
    assert n == chunk and r0 % chunk == 0
    return ref[r0 // chunk]


def _store_rows(ref, r0, n, chunk, val):
    if len(ref.shape) == 2:
        ref[pl.ds(r0, n), :] = val
    else:
        assert n == chunk and r0 % chunk == 0
        ref[r0 // chunk] = val


def _seq_pitch(chunk):
    return chunk + SUBLANES if chunk % SUBLANES == 0 else chunk


def _const_spec(shape):
    nd = len(shape)
    return pl.BlockSpec(shape, lambda j, _n=nd: (0,) * _n)


def _decay_tables(chunk, tile_rows):
    f32 = np.float32
    log_gamma = np.log1p(-np.exp2(f32(-5.0) - np.arange(RET_HEADS, dtype=f32))).astype(f32)
    r = np.arange(tile_rows)
    seq, loc = r // chunk, (r % chunk).astype(f32)
    rel = loc[:, None] - loc[None, :]
    ok = (seq[:, None] == seq[None, :]) & (rel >= 0)
    dmask = np.where(ok[None], np.exp(np.where(ok, rel, f32(0.0))[None] * log_gamma[:, None, None]), f32(0.0))
    if tile_rows < HEAD_DIM:
        dmask = np.pad(dmask, ((0, 0), (0, 0), (0, HEAD_DIM - tile_rows)))
    cdec = np.exp((loc[None, :] + f32(1.0)) * log_gamma[:, None])
    sdec = np.exp((f32(chunk) - f32(1.0) - loc)[None, :] * log_gamma[:, None])
    bcast = lambda t: np.ascontiguousarray(np.broadcast_to(t[:, :, None], (RET_HEADS, tile_rows, HEAD_DIM)))
    return dmask.astype(f32), bcast(cdec.astype(f32)), bcast(sdec.astype(f32))


def _rope_tables(pos):
    f32 = np.float32
    half = HEAD_DIM // 2
    inv_freq = (f32(ROPE_BASE) ** (-np.arange(half, dtype=f32) / f32(half))).astype(f32)
    ang = (pos.astype(f32)[:, None] * inv_freq[None, :]).astype(f32)
    cos, sin = np.cos(ang).astype(f32), np.sin(ang).astype(f32)
    return np.concatenate([cos, cos], axis=-1), np.concatenate([-sin, sin], axis=-1)


def _mixer(x, mod, pos, states, wts, *, prompt):
    n_seq = SEQ_PER_BLOCK if prompt else SAMPLE_SEQ_PER_BLOCK
    if prompt:
        n_total, seq_len, _ = x.shape
        assert n_total == n_seq
        chunk, tile_rows, n_steps = PROMPT_CHUNK, PROMPT_CHUNK, seq_len // PROMPT_CHUNK
        x_spec = pl.BlockSpec((n_seq, chunk, D_MODEL), lambda j: (0, j, 0))
        mod_spec = pl.BlockSpec((n_seq, 3 * D_MODEL), lambda j: (0, 0))
        tab_spec = pl.BlockSpec((chunk, HEAD_DIM), lambda j: (j, 0))
        seq_map = lambda j: 0
    else:
        chunk = pos.shape[0]
        tile_rows = SUBLANES
        n_total = x.shape[0] // chunk
        n_steps = n_total // n_seq
        x_spec = pl.BlockSpec((n_seq * chunk, D_MODEL), lambda j: (j, 0))
        mod_spec = pl.BlockSpec((n_seq * chunk, 3 * D_MODEL), lambda j: (j, 0))
        tab_spec = _const_spec((tile_rows, HEAD_DIM))
        seq_map = lambda j: j
    rows = n_seq * chunk
    cos, sin = _rope_tables(pos)
    if not prompt:
        reps = tile_rows // chunk
        cos, sin = np.tile(cos, (reps, 1)), np.tile(sin, (reps, 1))
    dmask, cdec, sdec = _decay_tables(chunk, tile_rows)
    chunk_decay = tuple(float(math.exp(chunk * math.log1p(-2.0 ** (-5.0 - h)))) for h in range(RET_HEADS))
    sret0, sre0, sim0 = states

    st_ret_spec = pl.BlockSpec((n_seq, RET_HEADS, HEAD_DIM, HEAD_DIM), lambda j: (seq_map(j), 0, 0, 0))
    st_s5_spec = pl.BlockSpec((n_seq, SSM_CH), lambda j: (seq_map(j), 0))
    consts = [dmask, cdec, sdec, wts["rnw"], wts["bmat"], wts["cre"], wts["cim"], wts["lbr"], wts["lbi"],
              wts["dsk"], wts["w_glu"], wts["b_glu"], wts["snw"], wts["w_out"]]
    args = [x, mod, wts["n1w"], wts["w_in"], cos, sin] + consts + [sret0, sre0, sim0]
    in_specs = ([x_spec, mod_spec, _const_spec(wts["n1w"].shape), _const_spec(wts["w_in"].shape), tab_spec, tab_spec]
                + [_const_spec(a.shape) for a in consts] + [st_ret_spec, st_s5_spec, st_s5_spec])

    kern = functools.partial(_mixer_kernel, n_seq=n_seq, chunk=chunk, tile_rows=tile_rows, carry=prompt,
                             chunk_decay=chunk_decay)
    out_shape = (
        jax.ShapeDtypeStruct(x.shape, F32),
        jax.ShapeDtypeStruct((n_total, RET_HEADS, HEAD_DIM, HEAD_DIM), F32),
        jax.ShapeDtypeStruct((n_total, SSM_CH), F32),
        jax.ShapeDtypeStruct((n_total, SSM_CH), F32),
    )
    scratch = [
        pltpu.VMEM((rows, D_MODEL), BF16),
        pltpu.VMEM((rows, 4 * RET_WIDTH), F32),
        pltpu.VMEM((SSM_WIDTH // LANES, n_seq * _seq_pitch(chunk), LANES), F32),
        pltpu.VMEM((rows, D_MODEL), F32),
        pltpu.VMEM((rows, SSM_WIDTH), F32),
        pltpu.VMEM((rows, SSM_CH // 2), F32),
        pltpu.VMEM((rows, SSM_CH // 2), F32),
        pltpu.VMEM((rows, SSM_CH // 2), F32),
        pltpu.VMEM((rows, SSM_CH // 2), F32),
        pltpu.VMEM((rows, SSM_WIDTH), F32),
        pltpu.VMEM((SSM_WIDTH // LANES, n_seq * _seq_pitch(chunk), LANES), F32),
    ]
    return pl.pallas_call(
        kern,
        grid=(n_steps,),
        in_specs=in_specs,
        out_specs=(x_spec, st_ret_spec, st_s5_spec, st_s5_spec),
        out_shape=out_shape,
        scratch_shapes=scratch,
        compiler_params=pltpu.CompilerParams(dimension_semantics=("arbitrary",), vmem_limit_bytes=VMEM_LIMIT),
        name="mixer_prompt" if prompt else "mixer_sample",
    )(*args)


PACK_ROWS = D_MODEL // (2 * LANES)


def _store_packed(ref, x):
    half = D_MODEL // 2
    bits = lax.bitcast_convert_type(x.astype(BF16).astype(F32), jnp.uint32)
    words = bits[:, :half] | (bits[:, half:] >> 16)
    for c in range(PACK_ROWS):
        ref[c] = words[:, c * LANES:(c + 1) * LANES]


def _load_packed(ref, n, first_row=0, row_stride=1):
    hi, lo = [], []
    for c in range(PACK_ROWS):
        w = ref[c] if row_stride == 1 else ref[c, pl.ds(first_row, n, stride=row_stride), :]
        hi.append(lax.bitcast_convert_type(w & jnp.uint32(0xFFFF0000), F32))
        lo.append(lax.bitcast_convert_type(w << 16, F32))
    return jnp.concatenate(hi + lo, axis=1)


def _split_bf16(x):
    hi = x.astype(BF16)
    return hi, (x - hi.astype(F32)).astype(BF16)


def _route_tile(x, sh, sc, n2w_ref, wrh_ref, wrm_ref, br_ref, ltri_ref, count_ref, h2_ref, route_ref, topw_ref):
    ms = jnp.mean(x * x, axis=-1, keepdims=True)
    h2 = x * lax.rsqrt(ms + NORM_EPS) * n2w_ref[...] * (1.0 + sc) + sh
    _store_packed(h2_ref, h2)
    hh, hm = _split_bf16(h2)
    logits = (jnp.dot(hh, wrh_ref[...], preferred_element_type=F32)
              + (jnp.dot(hh, wrm_ref[...], preferred_element_type=F32)
                 + jnp.dot(hm, wrh_ref[...], preferred_element_type=F32))) + br_ref[...]
    lane = lax.broadcasted_iota(jnp.int32, logits.shape, 1)
    work = logits
    vals, idxs = [], []
    for _ in range(TOP_K):
        m = jnp.max(work, axis=-1, keepdims=True)
        idx = jnp.min(jnp.where(work == m, lane, LANES), axis=-1, keepdims=True)
        vals.append(m)
        idxs.append(idx)
        work = jnp.where(lane == idx, -jnp.inf, work)
    exps = [jnp.exp(v - vals[0]) for v in vals]
    tot = exps[0] + exps[1] + exps[2] + exps[3]
    topw = jnp.zeros(logits.shape, F32)
    for k in range(TOP_K):
        topw = jnp.where(lane == k, exps[k] / tot, topw)
    topw_ref[...] = topw

    onehot = [(lane == idxs[k]).astype(F32) for k in range(TOP_K)]
    chosen = onehot[0] + onehot[1] + onehot[2] + onehot[3]
    before = jnp.dot(ltri_ref[...], chosen.astype(BF16), preferred_element_type=F32) + count_ref[...]
    info = jnp.zeros(logits.shape, jnp.int32)
    for k in range(TOP_K):
        rank = jnp.sum(onehot[k] * before, axis=-1, keepdims=True).astype(jnp.int32)
        info = jnp.where(lane == k, idxs[k], info)
        info = jnp.where(lane == TOP_K + k, rank, info)
    route_ref[...] = jnp.transpose(info)[:2 * TOP_K, :]
    count_ref[...] = count_ref[...] + jnp.sum(chosen, axis=0, keepdims=True)


def _router_kernel(xp_ref, shp_ref, scp_ref, xs_ref, shs_ref, scs_ref, n2w_ref, wrh_ref, wrm_ref, br_ref, ltri_ref,
                   h2_ref, route_ref, topw_ref, count_ref, *, n_prompt_tiles):
    i = pl.program_id(0)
    rest = (n2w_ref, wrh_ref, wrm_ref, br_ref, ltri_ref, count_ref, h2_ref, route_ref, topw_ref)

    @pl.when(i == 0)
    def _():
        count_ref[...] = jnp.zeros(count_ref.shape, F32)

    @pl.when(i < n_prompt_tiles)
    def _():
        _route_tile(xp_ref[...], shp_ref[0], scp_ref[0], *rest)

    @pl.when(i >= n_prompt_tiles)
    def _():
        _route_tile(xs_ref[...], shs_ref[...], scs_ref[...], *rest)


def _router(xp_rows, mod_p, seq_len, xs_rows, mod_s, n2w, w_router, b_router):
    tile = ROUTER_TILE
    n_p, n_s = xp_rows.shape[0], xs_rows.shape[0]
    tp, ts = n_p // tile, n_s // tile
    n_total = n_p + n_s
    mod_p = mod_p.reshape(mod_p.shape[0], 1, mod_p.shape[1])
    seq_of = lambda i: (jnp.minimum(i, tp - 1) * tile) // seq_len
    wr_pad = jnp.pad(w_router, ((0, 0), (0, LANES - N_EXPERTS)))
    wr_hi = wr_pad.astype(BF16)
    wr_mid = (wr_pad - wr_hi.astype(F32)).astype(BF16)
    br_pad = jnp.pad(b_router, ((0, 0), (0, LANES - N_EXPERTS)), constant_values=-1e30)
    ltri = jnp.asarray(np.tril(np.ones((tile, tile), np.float32), -1), BF16)
    clamp_p = lambda i: jnp.minimum(i, tp - 1)
    clamp_s = lambda i: jnp.maximum(i - tp, 0)
    return pl.pallas_call(
        functools.partial(_router_kernel, n_prompt_tiles=tp),
        grid=(tp + ts,),
        in_specs=[pl.BlockSpec((tile, D_MODEL), lambda i: (clamp_p(i), 0)),
                  pl.BlockSpec((1, 1, D_MODEL), lambda i: (seq_of(i), 0, 3)),
                  pl.BlockSpec((1, 1, D_MODEL), lambda i: (seq_of(i), 0, 4)),
                  pl.BlockSpec((tile, D_MODEL), lambda i: (clamp_s(i), 0)),
                  pl.BlockSpec((tile, D_MODEL), lambda i: (clamp_s(i), 3)),
                  pl.BlockSpec((tile, D_MODEL), lambda i: (clamp_s(i), 4)),
                  _const_spec(n2w.shape), _const_spec(wr_hi.shape), _const_spec(wr_mid.shape),
                  _const_spec(br_pad.shape), _const_spec(ltri.shape)],
        out_specs=(pl.BlockSpec((PACK_ROWS, tile, LANES), lambda i: (0, i, 0)),
                   pl.BlockSpec((2 * TOP_K, tile), lambda i: (0, i)),
                   pl.BlockSpec((tile, LANES), lambda i: (i, 0)),
                   pl.BlockSpec((1, LANES), lambda i: (0, 0))),
        out_shape=(jax.ShapeDtypeStruct((PACK_ROWS, n_total, LANES), jnp.uint32),
                   jax.ShapeDtypeStruct((2 * TOP_K, n_total), jnp.int32),
                   jax.ShapeDtypeStruct((n_total, LANES), F32),
                   jax.ShapeDtypeStruct((1, LANES), F32)),
        compiler_params=pltpu.CompilerParams(dimension_semantics=("arbitrary",), vmem_limit_bytes=VMEM_LIMIT),
        name="router",
    )(xp_rows, mod_p, mod_p, xs_rows, mod_s, mod_s, n2w, wr_hi, wr_mid, br_pad, ltri)


def _gather_rows(table, idx):
    n = idx.shape[0]
    steps = n // SC_GATHER_WINDOW
    assert n % SC_GATHER_WINDOW == 0 and steps % SC_WORKERS == 0
    mesh = plsc.VectorSubcoreMesh(core_axis_name="c", subcore_axis_name="s")

    @functools.partial(pl.kernel, out_type=jax.ShapeDtypeStruct((n, table.shape[1]), table.dtype), mesh=mesh,
                       scratch_types=[])
    def gather_kernel(table_hbm, idx_hbm, out_hbm):
        def body(idx_vmem, out_vmem):
            pltpu.sync_copy(table_hbm.at[idx_vmem.at[0]], out_vmem)

        pltpu.emit_pipeline(
            body,
            grid=(steps,),
            in_specs=[pl.BlockSpec((1, SC_GATHER_WINDOW), lambda i: (0, i))],
            out_specs=[pl.BlockSpec((SC_GATHER_WINDOW, table.shape[1]), lambda i: (i, 0))],
            core_axis_name=("c", "s"),
            dimension_semantics=(pltpu.PARALLEL,),
        )(idx_hbm, out_hbm)

    return gather_kernel(table, idx.reshape(1, n))


def _dispatch_packed(table, slot_kt, n_slots):
    planes, n_tok, lanes = table.shape
    win = SC_GATHER_WINDOW
    blocks = n_tok // win
    assert n_tok % win == 0
    blocks_pad = _round_up(blocks, SC_WORKERS // math.gcd(SC_WORKERS, planes))
    n_spare = (blocks_pad - blocks) * TOP_K * win
    n_ext = n_slots + n_spare
    dest = jnp.transpose(slot_kt.reshape(TOP_K, blocks, win), (1, 0, 2))
    spare = n_slots + jnp.arange(n_spare, dtype=jnp.int32).reshape(blocks_pad - blocks, TOP_K, win)
    dest = jnp.concatenate([dest, spare], axis=0)[None] + (jnp.arange(planes, dtype=jnp.int32) * n_ext)[:, None, None, None]
    dest = dest.reshape(planes * blocks_pad * TOP_K, win)
    steps = planes * blocks_pad
    src_block = lambda g: (g // blocks_pad) * blocks + jnp.minimum(g % blocks_pad, blocks - 1)
    mesh = plsc.VectorSubcoreMesh(core_axis_name="c", subcore_axis_name="s")

    @functools.partial(pl.kernel, out_type=jax.ShapeDtypeStruct((planes * n_ext, lanes), table.dtype), mesh=mesh,
                       scratch_types=[])
    def scatter_kernel(table_hbm, dest_hbm, out_hbm):
        def body(rows_vmem, dest_vmem):
            for k in range(TOP_K):
                pltpu.sync_copy(rows_vmem, out_hbm.at[dest_vmem.at[k]])

        pltpu.emit_pipeline(
            body,
            grid=(steps,),
            in_specs=[pl.BlockSpec((win, lanes), lambda g: (src_block(g), 0)),
                      pl.BlockSpec((TOP_K, win), lambda g: (g, 0))],
            out_specs=[],
            core_axis_name=("c", "s"),
            dimension_semantics=(pltpu.PARALLEL,),
        )(table_hbm, dest_hbm)

    return scatter_kernel(table.reshape(planes * n_tok, lanes), dest).reshape(planes, n_ext, lanes)


def _gather_packed(table, rows):
    planes, n_table, lanes = table.shape
    idx = jnp.concatenate([rows + c * n_table for c in range(planes)])
    out = _gather_rows(table.reshape(planes * n_table, lanes), idx)
    return out.reshape(planes, rows.shape[0], lanes)


def _expert_weight_copies(e, w1_hbm, w2_hbm, w1s_ref, w2s_ref, sem):
    return (pltpu.make_async_copy(w1_hbm.at[e], w1s_ref, sem.at[0]),
            pltpu.make_async_copy(w2_hbm.at[e], w2s_ref, sem.at[1]))


def _experts_kernel(te_ref, tr_ref, nx_ref, nv_ref, xs_ref, w1_hbm, b1_ref, w2_hbm, b2_ref, ys_ref,
                    w1s_ref, w2s_ref, w1b_ref, w2b_ref, sem):
    i = pl.program_id(0)
    e = te_ref[i]
    new_expert = (i == 0) | (e != te_ref[jnp.maximum(i - 1, 0)])
    copies = functools.partial(_expert_weight_copies, w1_hbm=w1_hbm, w2_hbm=w2_hbm, w1s_ref=w1s_ref,
                               w2s_ref=w2s_ref, sem=sem)

    @pl.when(i == 0)
    def _():
        for c in copies(e):
            c.start()

    @pl.when(new_expert)
    def _():
        for c in copies(e):
            c.wait()
        w1b_ref[...] = w1s_ref[...].astype(BF16)
        w2b_ref[...] = w2s_ref[...].astype(BF16)

        @pl.when(nx_ref[i] >= 0)
        def _():
            for c in copies(nx_ref[i]):
                c.start()

    def expert_pass(r0, n_rows, rows_valid):
        xs_v, ys_v = xs_ref.at[:, pl.ds(r0, n_rows)], ys_ref.at[:, pl.ds(r0, n_rows)]
        row = lax.broadcasted_iota(jnp.int32, (n_rows, D_MODEL), 0)
        x = jnp.where(row < rows_valid, _load_packed(xs_v, n_rows), 0.0).astype(BF16)
        hu = jnp.dot(x, w1b_ref[...], preferred_element_type=F32) + b1_ref[0]
        x_glu = jnp.minimum(hu[:, :D_FF], SWIGLU_LIMIT)
        x_lin = jnp.clip(hu[:, D_FF:], -SWIGLU_LIMIT, SWIGLU_LIMIT)
        act = x_glu * jax.nn.sigmoid(SWIGLU_ALPHA * x_glu) * (x_lin + 1.0)
        _store_packed(ys_v, jnp.dot(act.astype(BF16), w2b_ref[...], preferred_element_type=F32) + b2_ref[0])

    def zero_rows(r0, n_rows):
        for c in range(PACK_ROWS):
            ys_ref[c, pl.ds(r0, n_rows), :] = jnp.zeros((n_rows, LANES), jnp.uint32)

    half = EXPERT_ROWS // 2
    rows_tile = jnp.where(i < nv_ref[0], tr_ref[i], 0)
    full_tile = rows_tile > SLOT_TILE - half

    @pl.when(full_tile)
    def _():
        expert_pass(0, SLOT_TILE, rows_tile)

    for h in range(SLOT_TILE // EXPERT_ROWS):
        r0 = h * EXPERT_ROWS
        rows_here = jnp.where(full_tile, -1, rows_tile - r0)

        @pl.when(rows_here > half)
        def _(r0=r0, rows_here=rows_here):
            expert_pass(r0, EXPERT_ROWS, rows_here)

        @pl.when((rows_here > 0) & (rows_here <= half))
        def _(r0=r0, rows_here=rows_here):
            expert_pass(r0, half, rows_here)
            zero_rows(r0 + half, half)

        @pl.when((rows_here <= 0) & jnp.logical_not(full_tile))
        def _(r0=r0):
            zero_rows(r0, EXPERT_ROWS)


def _experts(tile_expert, tile_rows, next_expert, n_valid, xs, w1, b1, w2, b2):
    n_tiles = tile_expert.shape[0]
    n_slots = n_tiles * SLOT_TILE
    grid_spec = pltpu.PrefetchScalarGridSpec(
        num_scalar_prefetch=4,
        grid=(n_tiles,),
        in_specs=[
            pl.BlockSpec((PACK_ROWS, SLOT_TILE, LANES), lambda i, te, tr, nx, nv: (0, i, 0)),
            pl.BlockSpec(memory_space=pl.ANY),
            pl.BlockSpec((1, 1, 2 * D_FF), lambda i, te, tr, nx, nv: (te[i], 0, 0)),
            pl.BlockSpec(memory_space=pl.ANY),
            pl.BlockSpec((1, 1, D_MODEL), lambda i, te, tr, nx, nv: (te[i], 0, 0)),
        ],
        out_specs=pl.BlockSpec((PACK_ROWS, SLOT_TILE, LANES), lambda i, te, tr, nx, nv: (0, i, 0)),
        scratch_shapes=[pltpu.VMEM((D_MODEL, 2 * D_FF), F32), pltpu.VMEM((D_FF, D_MODEL), F32),
                        pltpu.VMEM((D_MODEL, 2 * D_FF), BF16), pltpu.VMEM((D_FF, D_MODEL), BF16),
                        pltpu.SemaphoreType.DMA((2,))],
    )
    return pl.pallas_call(
        _experts_kernel,
        grid_spec=grid_spec,
        out_shape=jax.ShapeDtypeStruct((PACK_ROWS, n_slots, LANES), jnp.uint32),
        compiler_params=pltpu.CompilerParams(dimension_semantics=("arbitrary",), vmem_limit_bytes=VMEM_LIMIT),
        name="experts",
    )(tile_expert, tile_rows, next_expert, n_valid, xs, w1, b1.reshape(N_EXPERTS, 1, 2 * D_FF), w2,
      b2.reshape(N_EXPERTS, 1, D_MODEL))


def _combine_kernel(y4_ref, x1_ref, topw_ref, g2_ref, fw_ref, *rest, per_row_mod):
    o_ref = rest[-1]
    w = topw_ref[...]
    n = w.shape[0]
    ff = None
    for k in range(TOP_K):
        yk = w[:, k:k + 1] * _load_packed(y4_ref.at[:, k], n)
        ff = yk if ff is None else ff + yk
    g2 = g2_ref[...] if per_row_mod else g2_ref[0]
    x = x1_ref[...] + g2 * ff
    ms = jnp.mean(x * x, axis=-1, keepdims=True)
    o_ref[...] = x * lax.rsqrt(ms + NORM_EPS) * fw_ref[...]


def _combine(y4, y4_row0, x1, x1_row0, n_rows, topw, topw_row0, mod, rows_per_mod, fw, out_buf):
    tile = COMBINE_TILE
    y4_off, x1_off, tw_off = y4_row0 // tile, x1_row0 // tile, topw_row0 // tile
    per_row = rows_per_mod == 1
    if per_row:
        g2_spec = pl.BlockSpec((tile, D_MODEL), lambda i: (i + x1_off, 5))
    else:
        mod = mod.reshape(mod.shape[0], 1, mod.shape[1])
        g2_spec = pl.BlockSpec((1, 1, D_MODEL), lambda i: (((i + x1_off) * tile) // rows_per_mod, 0, 5))
    in_specs = [pl.BlockSpec((PACK_ROWS, TOP_K, tile, LANES), lambda i: (0, 0, i + y4_off, 0)),
                pl.BlockSpec((tile, D_MODEL), lambda i: (i + x1_off, 0)),
                pl.BlockSpec((tile, LANES), lambda i: (i + tw_off, 0)),
                g2_spec, _const_spec(fw.shape)]
    args = [y4, x1, topw, mod, fw]
    aliases = {}
    if out_buf is not None:
        in_specs.append(pl.BlockSpec(memory_space=pl.ANY))
        args.append(out_buf)
        aliases = {len(args) - 1: 0}
    return pl.pallas_call(
        functools.partial(_combine_kernel, per_row_mod=per_row),
        grid=(n_rows // tile,),
        in_specs=in_specs,
        out_specs=pl.BlockSpec((tile, D_MODEL), lambda i: (i + x1_off, 0)),
        out_shape=jax.ShapeDtypeStruct(x1.shape, F32),
        input_output_aliases=aliases,
        compiler_params=pltpu.CompilerParams(dimension_semantics=("arbitrary",), vmem_limit_bytes=VMEM_LIMIT),
        name="combine",
    )(*args)


def _routing_tables(route, counts, n_slots):
    padded = ((counts + SLOT_TILE - 1) // SLOT_TILE) * SLOT_TILE
    pend = jnp.cumsum(padded)
    poff = pend - padded
    expert_kt, rank_kt = route[:TOP_K], route[TOP_K:]
    experts = jnp.arange(N_EXPERTS, dtype=jnp.int32)
    start_kt = jnp.sum((expert_kt[None] == experts[:, None, None]).astype(jnp.int32) * poff[:, None, None], axis=0)
    slot_kt = start_kt + rank_kt
    n_tiles = n_slots // SLOT_TILE
    n_valid = (pend[-1] // SLOT_TILE).astype(jnp.int32)
    tile_row = jnp.minimum(jnp.arange(n_tiles, dtype=jnp.int32), n_valid - 1) * SLOT_TILE
    in_later = (pend[None, :] <= tile_row[:, None]).astype(jnp.int32)
    tile_e = jnp.sum(in_later, axis=1).astype(jnp.int32)
    is_e = (experts[None, :] == tile_e[:, None]).astype(jnp.int32)
    used_end = jnp.sum(is_e * (poff + counts)[None, :], axis=1)
    tile_rows = jnp.clip(used_end - tile_row, 0, SLOT_TILE).astype(jnp.int32)
    later_used = (experts[None, :] > tile_e[:, None]) & (counts[None, :] > 0)
    next_e = jnp.min(jnp.where(later_used, experts[None, :], N_EXPERTS), axis=1)
    next_e = jnp.where(next_e < N_EXPERTS, next_e, -1).astype(jnp.int32)
    return slot_kt, tile_e, tile_rows, next_e, n_valid.reshape(1)


def _round_up(n, m):
    return ((n + m - 1) // m) * m


def kernel(x_prompt, x_sample, c_prompt, c_sample, state_ret, state_s5_re, state_s5_im, norm1_w, norm2_w, w_ada, b_ada, w_in, ret_norm_w, s5_lam_re, s5_lam_im, s5_log_dt, s5_b_re, s5_b_im, s5_c_re, s5_c_im, s5_d, w_glu, b_glu, s5_norm_w, w_out, w_router, b_router, w1, b1, w2, b2, final_w):
    bp, lp, _ = x_prompt.shape
    bs, ls, _ = x_sample.shape
    assert norm1_w.shape[0] == 1, "single-layer model"
    n_p, n_s = bp * lp, bs * ls
    n_tok = n_p + n_s

    mod = _ada(jnp.concatenate([c_prompt, c_sample], axis=0), w_ada[0], b_ada[0])
    mod_p, mod_s = mod[:bp], jnp.repeat(mod[bp:], ls, axis=0)

    lbr, lbi, bbr, bbi = _s5prep(s5_lam_re[0], s5_lam_im[0], s5_log_dt[0], s5_b_re[0], s5_b_im[0])
    bmat = jnp.concatenate([_block_diag(bbr), _block_diag(bbi)], axis=-1).astype(BF16)
    cre = _block_diag(jnp.transpose(s5_c_re[0], (0, 2, 1))).astype(BF16)
    cim = _block_diag(jnp.transpose(-s5_c_im[0], (0, 2, 1))).astype(BF16)
    wts = dict(
        n1w=norm1_w, w_in=w_in[0].astype(BF16), rnw=ret_norm_w, bmat=bmat, cre=cre, cim=cim,
        lbr=lbr.reshape(1, SSM_CH), lbi=lbi.reshape(1, SSM_CH), dsk=s5_d[0].reshape(1, SSM_WIDTH),
        w_glu=w_glu[0].astype(BF16), b_glu=b_glu, snw=s5_norm_w, w_out=w_out[0].astype(BF16),
    )

    zero_states = (jnp.zeros((bp, RET_HEADS, HEAD_DIM, HEAD_DIM), F32), jnp.zeros((bp, SSM_CH), F32),
                   jnp.zeros((bp, SSM_CH), F32))
    x1_p, ret_p, re_p, im_p = _mixer(x_prompt, mod_p, np.arange(lp, dtype=np.float32), zero_states, wts,
                                     prompt=True)
    sample_states = (state_ret[0], state_s5_re[0].reshape(bs, SSM_CH), state_s5_im[0].reshape(bs, SSM_CH))
    x1_s, ret_s, re_s, im_s = _mixer(x_sample.reshape(n_s, D_MODEL), mod_s,
                                     PAST_LEN + np.arange(ls, dtype=np.float32), sample_states, wts, prompt=False)

    x1_p_rows = x1_p.reshape(n_p, D_MODEL)
    h2, route, topw, counts = _router(x1_p_rows, mod_p, lp, x1_s, mod_s, norm2_w, w_router[0], b_router)

    n_assign = n_tok * TOP_K
    gather_quantum = SC_GATHER_WINDOW * SC_WORKERS // PACK_ROWS
    assert n_assign % gather_quantum == 0
    n_slots = _round_up(_round_up(n_assign, SLOT_TILE) + N_EXPERTS * SLOT_TILE, gather_quantum)
    slot_kt, tile_e, tile_rows, next_e, n_valid = _routing_tables(route, counts[0, :N_EXPERTS].astype(jnp.int32),
                                                                  n_slots)
    xs = _dispatch_packed(h2, slot_kt, n_slots)
    ys = _experts(tile_e, tile_rows, next_e, n_valid, xs, w1[0], b1[0], w2[0], b2[0])
    fw = final_w.reshape(1, D_MODEL)
    y_p, y_s = None, None
    bounds = [r * (n_p // COMBINE_RANGES) for r in range(COMBINE_RANGES)] + [n_tok]
    for lo, hi in zip(bounds[:-1], bounds[1:]):
        y4 = _gather_packed(ys, slot_kt[:, lo:hi].reshape(-1)).reshape(PACK_ROWS, TOP_K, hi - lo, LANES)
        y_p = _combine(y4, 0, x1_p_rows, lo, min(hi, n_p) - lo, topw, lo, mod_p, lp, fw, y_p)
        if hi > n_p:
            y_s = _combine(y4, n_p - lo, x1_s, 0, n_s, topw, n_p, mod_s, 1, fw, None)

    g, p = SSM_GROUPS, SSM_STATE
    return (y_p.reshape(bp, lp, D_MODEL), y_s.reshape(bs, ls, D_MODEL),
            ret_p[None], re_p.reshape(1, bp, g, p), im_p.reshape(1, bp, g, p),
            ret_s[None], re_s.reshape(1, bs, g, p), im_s.reshape(1, bs, g, p))
```

```python
import functools
import math

import jax
import jax.numpy as jnp
import numpy as np
from jax import lax
from jax.experimental import pallas as pl
from jax.experimental.pallas import tpu as pltpu
from jax.experimental.pallas import tpu_sc as plsc

F32 = jnp.float32
BF16 = jnp.bfloat16

D_MODEL = 1024
PAST_LEN = 16384
RET_WIDTH = 512
RET_HEADS = 4
HEAD_DIM = 128
ROPE_BASE = 10000.0
SSM_WIDTH = 512
SSM_GROUP = 16
SSM_GROUPS = 32
SSM_STATE = 64
SSM_CH = SSM_GROUPS * SSM_STATE
IN_WIDTH = 4 * RET_WIDTH + SSM_WIDTH
N_EXPERTS = 32
TOP_K = 4
D_FF = 1024
SWIGLU_LIMIT = 7.0
SWIGLU_ALPHA = 1.702
NORM_EPS = 1e-6

LANES = 128
SUBLANES = 8
VMEM_LIMIT = 56 * 1024 * 1024

SEQ_PER_BLOCK = 8
SAMPLE_SEQ_PER_BLOCK = 16
SCAN_ELEMS = 8 * 1024
PROMPT_CHUNK = 64
S5_BLOCK_GROUPS = 8
N_S5_BLOCKS = SSM_GROUPS // S5_BLOCK_GROUPS
S5_BLOCK_IN = S5_BLOCK_GROUPS * SSM_GROUP
S5_BLOCK_CH = S5_BLOCK_GROUPS * SSM_STATE
ROUTER_TILE = 512
SLOT_TILE = 512
EXPERT_ROWS = 256
COMBINE_TILE = 256
COMBINE_RANGES = 2
SC_GATHER_WINDOW = 128
SC_WORKERS = 32


def _silu(x):
    return x * jax.nn.sigmoid(x)


def _expand_rows(dst_ref, src, reps, first_chunk=0):
    n = src.shape[0]
    for c in range(src.shape[1] // LANES):
        piece = src[:, c * LANES:(c + 1) * LANES]
        for t in range(reps):
            dst_ref[first_chunk + c, pl.ds(t, n, stride=reps), :] = piece


def _expanded(ref, first_chunk, n_chunks=D_MODEL // LANES):
    return jnp.concatenate([ref[c] for c in range(first_chunk, first_chunk + n_chunks)], axis=1)


def _ada_kernel(c_ref, w_ref, b_ref, o_ref):
    sh, sm = _split_bf16(_silu(c_ref[...]))
    wh, wm = _split_bf16(w_ref[...])
    o_ref[...] = (jnp.dot(sh, wh, preferred_element_type=F32)
                  + (jnp.dot(sh, wm, preferred_element_type=F32) + jnp.dot(sm, wh, preferred_element_type=F32))
                  + b_ref[...])


def _ada(c_all, w_ada, b_ada):
    n_rows, n_out = c_all.shape[0], w_ada.shape[1]
    tn = 1536
    return pl.pallas_call(
        _ada_kernel,
        grid=(n_out // tn,),
        in_specs=[
            pl.BlockSpec((n_rows, D_MODEL), lambda j: (0, 0)),
            pl.BlockSpec((D_MODEL, tn), lambda j: (0, j)),
            pl.BlockSpec((1, tn), lambda j: (0, j)),
        ],
        out_specs=pl.BlockSpec((n_rows, tn), lambda j: (0, j)),
        out_shape=jax.ShapeDtypeStruct((n_rows, n_out), F32),
        compiler_params=pltpu.CompilerParams(dimension_semantics=("arbitrary",), vmem_limit_bytes=VMEM_LIMIT),
        name="ada",
    )(c_all, w_ada, b_ada.reshape(1, n_out))


def _s5prep_kernel(lre_ref, lim_ref, ldt_ref, bre_ref, bim_ref, lbr_ref, lbi_ref, bbr_ref, bbi_ref):
    lam_re, lam_im = lre_ref[...], lim_ref[...]
    dt = jnp.exp(ldt_ref[...])
    mag = jnp.exp(lam_re * dt)
    ang = lam_im * dt
    lb_re, lb_im = mag * jnp.cos(ang), mag * jnp.sin(ang)
    den = lam_re * lam_re + lam_im * lam_im
    f_re = ((lb_re - 1.0) * lam_re + lb_im * lam_im) / den
    f_im = (lb_im * lam_re - (lb_re - 1.0) * lam_im) / den
    lbr_ref[...] = lb_re
    lbi_ref[...] = lb_im
    b_re, b_im = bre_ref[...], bim_ref[...]
    bbr_ref[...] = f_re[:, None, :] * b_re - f_im[:, None, :] * b_im
    bbi_ref[...] = f_re[:, None, :] * b_im + f_im[:, None, :] * b_re


def _s5prep(lam_re, lam_im, log_dt, b_re, b_im):
    g, p = lam_re.shape
    bt_re = jnp.transpose(b_re, (0, 2, 1))
    bt_im = jnp.transpose(b_im, (0, 2, 1))
    return pl.pallas_call(
        _s5prep_kernel,
        out_shape=(
            jax.ShapeDtypeStruct((g, p), F32), jax.ShapeDtypeStruct((g, p), F32),
            jax.ShapeDtypeStruct((g, SSM_GROUP, p), F32), jax.ShapeDtypeStruct((g, SSM_GROUP, p), F32),
        ),
        name="s5prep",
    )(lam_re, lam_im, log_dt.reshape(g, 1), bt_re, bt_im)


def _block_diag(blocks):
    _, r, c = blocks.shape
    b4 = blocks.reshape(N_S5_BLOCKS, S5_BLOCK_GROUPS, r, c)
    eye = jnp.eye(S5_BLOCK_GROUPS, dtype=blocks.dtype)
    out = b4[:, :, :, None, :] * eye[None, :, None, :, None]
    return out.reshape(N_S5_BLOCKS, S5_BLOCK_GROUPS * r, S5_BLOCK_GROUPS * c)


def _mixer_kernel(x_ref, mod_ref, n1w_ref, win_ref, cos_ref, sin_ref, dmask_ref, cdec_ref, sdec_ref,
                  rnw_ref, bmat_ref, cre_ref, cim_ref, lbr_ref, lbi_ref, dsk_ref, wglu_ref, bglu_ref,
                  snw_ref, wout_ref, sret0_ref, sre0_ref, sim0_ref,
                  x1_ref, sret_ref, sre_ref, sim_ref,
                  hb_ref, z_ref, zu_ref, oy_ref, utb_ref, bur0_ref, bur1_ref, bui0_ref, bui1_ref, ytb_ref, yb_ref, modx_ref,
                  *, n_seq, chunk, tile_rows, carry, chunk_decay):
    rows = n_seq * chunk
    seq_per_tile = tile_rows // chunk
    n_tiles = rows // tile_rows
    per_row_mod = chunk % SUBLANES != 0
    dl = D_MODEL // LANES

    def load_states():
        sret_ref[...] = sret0_ref[...]
        sre_ref[...] = sre0_ref[...]
        sim_ref[...] = sim0_ref[...]

    if carry:
        pl.when(pl.program_id(0) == 0)(load_states)
    else:
        load_states()

    if per_row_mod:
        _expand_rows(modx_ref, mod_ref[...], chunk)

    n1w = n1w_ref[...]
    mod_rows = rows if per_row_mod else chunk
    for i in range(rows // mod_rows):
        r0 = i * mod_rows
        xb = _load_rows(x_ref, r0, mod_rows, chunk)
        if per_row_mod:
            sh, sc = _expanded(modx_ref, 0), _expanded(modx_ref, dl)
        else:
            sh = mod_ref[pl.ds(i, 1), pl.ds(0, D_MODEL)]
            sc = mod_ref[pl.ds(i, 1), pl.ds(D_MODEL, D_MODEL)]
        ms = jnp.mean(xb * xb, axis=-1, keepdims=True)
        hn = xb * lax.rsqrt(ms + NORM_EPS) * n1w
        hb_ref[pl.ds(r0, mod_rows), :] = (hn * (1.0 + sc) + sh).astype(BF16)
    ret_w = 4 * RET_WIDTH
    z_ref[...] = jnp.dot(hb_ref[...], win_ref[:, pl.ds(0, ret_w)], preferred_element_type=F32)
    zu = jnp.dot(hb_ref[...], win_ref[:, pl.ds(ret_w, SSM_WIDTH)], preferred_element_type=F32)
    pitch = zu_ref.shape[1] // n_seq
    for c in range(SSM_WIDTH // LANES):
        for b in range(n_seq if pitch != chunk else 1):
            nb = chunk if pitch != chunk else rows
            zu_ref[c, pl.ds(b * pitch, nb), :] = zu[b * chunk:b * chunk + nb, c * LANES:(c + 1) * LANES]

    cos = cos_ref[...]
    sin = sin_ref[...]
    scale = HEAD_DIM ** -0.5
    if seq_per_tile > 1:
        row_id = lax.broadcasted_iota(jnp.int32, (tile_rows, HEAD_DIM), 0)

    def rope(t):
        return t * cos + pltpu.roll(t, HEAD_DIM // 2, 1) * sin

    def ret_tile(ti, c):
        r0 = pl.multiple_of(ti * tile_rows, tile_rows)
        for h in range(RET_HEADS):
            c0 = h * HEAD_DIM
            q = rope(z_ref[pl.ds(r0, tile_rows), pl.ds(c0, HEAD_DIM)])
            k = rope(z_ref[pl.ds(r0, tile_rows), pl.ds(RET_WIDTH + c0, HEAD_DIM)]) * scale
            v = z_ref[pl.ds(r0, tile_rows), pl.ds(2 * RET_WIDTH + c0, HEAD_DIM)]
            g = z_ref[pl.ds(r0, tile_rows), pl.ds(3 * RET_WIDTH + c0, HEAD_DIM)]
            kd = k * sdec_ref[h]
            if tile_rows < HEAD_DIM:
                pad = jnp.zeros((HEAD_DIM - tile_rows, HEAD_DIM), F32)
                k, v, kd = (jnp.concatenate([t, pad], axis=0) for t in (k, v, kd))
                if seq_per_tile > 1:
                    row_kv = lax.broadcasted_iota(jnp.int32, (HEAD_DIM, HEAD_DIM), 0)
            elif seq_per_tile > 1:
                row_kv = row_id
            qb, kb, vb = q.astype(BF16), k.astype(BF16), v.astype(BF16)
            s = lax.dot_general(qb, kb, (((1,), (1,)), ((), ())), preferred_element_type=F32) * dmask_ref[h]
            o = jnp.dot(s.astype(BF16), vb, preferred_element_type=F32)
            cross = None
            for si in range(seq_per_tile):
                sidx = ti * seq_per_tile + si
                st = sret_ref[sidx, h]
                cr = jnp.dot(qb, st.astype(BF16), preferred_element_type=F32)
                if seq_per_tile > 1:
                    in_seq = (row_id >= si * chunk) & (row_id < (si + 1) * chunk)
                    cross = jnp.where(in_seq, cr, 0.0 if cross is None else cross)
                    kds = jnp.where((row_kv >= si * chunk) & (row_kv < (si + 1) * chunk), kd, 0.0)
                else:
                    cross, kds = cr, kd
                upd = lax.dot_general(kds.astype(BF16), vb, (((0,), (0,)), ((), ())), preferred_element_type=F32)
                sret_ref[sidx, h] = st * chunk_decay[h] + upd
            o = o + cross * cdec_ref[h]
            o = o * lax.rsqrt(jnp.mean(o * o, axis=-1, keepdims=True) + NORM_EPS)
            o = o * rnw_ref[:, pl.ds(c0, HEAD_DIM)] * _silu(g)
            oy_ref[pl.ds(r0, tile_rows), pl.ds(c0, HEAD_DIM)] = o
        return c

    lax.fori_loop(0, n_tiles, ret_tile, 0, unroll=True)

    for t in range(chunk):
        for c in range(SSM_WIDTH // LANES):
            utb_ref[pl.ds(t * n_seq, n_seq), pl.ds(c * LANES, LANES)] = zu_ref[c, pl.ds(t, n_seq, stride=pitch), :]
    half_ch = SSM_CH // 2
    blk_per_half = N_S5_BLOCKS // 2
    bur_refs, bui_refs = (bur0_ref, bur1_ref), (bui0_ref, bui1_ref)
    for blk in range(N_S5_BLOCKS):
        hf, lcols = blk // blk_per_half, pl.ds((blk % blk_per_half) * S5_BLOCK_CH, S5_BLOCK_CH)
        ub = utb_ref[:, pl.ds(blk * S5_BLOCK_IN, S5_BLOCK_IN)].astype(BF16)
        bu = jnp.dot(ub, bmat_ref[blk], preferred_element_type=F32)
        bur_refs[hf][:, lcols] = bu[:, :S5_BLOCK_CH]
        bui_refs[hf][:, lcols] = bu[:, S5_BLOCK_CH:]

    scan_w = min(half_ch, SCAN_ELEMS // n_seq)
    for hf in range(2):
        bur_ref, bui_ref = bur_refs[hf], bui_refs[hf]
        for p in range(half_ch // scan_w):
            cols = pl.ds(p * scan_w, scan_w)
            gcols = pl.ds(hf * half_ch + p * scan_w, scan_w)
            lbr = jnp.broadcast_to(lbr_ref[:, gcols], (n_seq, scan_w))
            lbi = jnp.broadcast_to(lbi_ref[:, gcols], (n_seq, scan_w))
            hr, hi = sre_ref[:, gcols], sim_ref[:, gcols]
            for t in range(chunk):
                rws = pl.ds(t * n_seq, n_seq)
                hr, hi = (lbr * hr - lbi * hi + bur_ref[rws, cols], lbr * hi + lbi * hr + bui_ref[rws, cols])
                bur_ref[rws, cols] = hr
                bui_ref[rws, cols] = hi
            sre_ref[:, gcols] = hr
            sim_ref[:, gcols] = hi

    for blk in range(N_S5_BLOCKS):
        hf, lcols = blk // blk_per_half, pl.ds((blk % blk_per_half) * S5_BLOCK_CH, S5_BLOCK_CH)
        yb = jnp.dot(bur_refs[hf][:, lcols].astype(BF16), cre_ref[blk], preferred_element_type=F32)
        yb = yb + jnp.dot(bui_refs[hf][:, lcols].astype(BF16), cim_ref[blk], preferred_element_type=F32)
        ucols = pl.ds(blk * S5_BLOCK_IN, S5_BLOCK_IN)
        ytb_ref[:, ucols] = yb + dsk_ref[:, ucols] * utb_ref[:, ucols]
    for t in range(chunk):
        for c in range(SSM_WIDTH // LANES):
            yb_ref[c, pl.ds(t, n_seq, stride=pitch), :] = ytb_ref[pl.ds(t * n_seq, n_seq), pl.ds(c * LANES, LANES)]

    def seq_major(c):
        if pitch == chunk:
            return yb_ref[c]
        return jnp.concatenate([yb_ref[c, pl.ds(b * pitch, chunk), :] for b in range(n_seq)], axis=0)

    y = jnp.concatenate([seq_major(c) for c in range(SSM_WIDTH // LANES)], axis=1)
    y = jax.nn.gelu(y, approximate=True)
    gate = jnp.dot(y.astype(BF16), wglu_ref[...], preferred_element_type=F32) + bglu_ref[...]
    y = y * jax.nn.sigmoid(gate)
    y = y * lax.rsqrt(jnp.mean(y * y, axis=-1, keepdims=True) + NORM_EPS) * snw_ref[...]
    oy_ref[:, pl.ds(RET_WIDTH, SSM_WIDTH)] = y

    mix = jnp.dot(oy_ref[...].astype(BF16), wout_ref[...], preferred_element_type=F32)
    for i in range(rows // mod_rows):
        r0 = i * mod_rows
        if per_row_mod:
            g1 = _expanded(modx_ref, 2 * dl)
        else:
            g1 = mod_ref[pl.ds(i, 1), pl.ds(2 * D_MODEL, D_MODEL)]
        _store_rows(x1_ref, r0, mod_rows, chunk,
                    _load_rows(x_ref, r0, mod_rows, chunk) + g1 * mix[r0:r0 + mod_rows])


def _load_rows(ref, r0, n, chunk):
    if len(ref.shape) == 2:
        return ref[pl.ds(r0, n), :]
    assert n == chunk and r0 % chunk == 0
    return ref[r0 // chunk]


def _store_rows(ref, r0, n, chunk, val):
    if len(ref.shape) == 2:
        ref[pl.ds(r0, n), :] = val
    else:
        assert n == chunk and r0 % chunk == 0
        ref[r0 // chunk] = val


def _seq_pitch(chunk):
    return chunk + SUBLANES if chunk % SUBLANES == 0 else chunk


def _const_spec(shape):
    nd = len(shape)
    return pl.BlockSpec(shape, lambda j, _n=nd: (0,) * _n)


def _decay_tables(chunk, tile_rows):
    f32 = np.float32
    log_gamma = np.log1p(-np.exp2(f32(-5.0) - np.arange(RET_HEADS, dtype=f32))).astype(f32)
    r = np.arange(tile_rows)
    seq, loc = r // chunk, (r % chunk).astype(f32)
    rel = loc[:, None] - loc[None, :]
    ok = (seq[:, None] == seq[None, :]) & (rel >= 0)
    dmask = np.where(ok[None], np.exp(np.where(ok, rel, f32(0.0))[None] * log_gamma[:, None, None]), f32(0.0))
    if tile_rows < HEAD_DIM:
        dmask = np.pad(dmask, ((0, 0), (0, 0), (0, HEAD_DIM - tile_rows)))
    cdec = np.exp((loc[None, :] + f32(1.0)) * log_gamma[:, None])
    sdec = np.exp((f32(chunk) - f32(1.0) - loc)[None, :] * log_gamma[:, None])
    bcast = lambda t: np.ascontiguousarray(np.broadcast_to(t[:, :, None], (RET_HEADS, tile_rows, HEAD_DIM)))
    return dmask.astype(f32), bcast(cdec.astype(f32)), bcast(sdec.astype(f32))


def _rope_tables(pos):
    f32 = np.float32
    half = HEAD_DIM // 2
    inv_freq = (f32(ROPE_BASE) ** (-np.arange(half, dtype=f32) / f32(half))).astype(f32)
    ang = (pos.astype(f32)[:, None] * inv_freq[None, :]).astype(f32)
    cos, sin = np.cos(ang).astype(f32), np.sin(ang).astype(f32)
    return np.concatenate([cos, cos], axis=-1), np.concatenate([-sin, sin], axis=-1)


def _mixer(x, mod, pos, states, wts, *, prompt):
    n_seq = SEQ_PER_BLOCK if prompt else SAMPLE_SEQ_PER_BLOCK
    if prompt:
        n_total, seq_len, _ = x.shape
        assert n_total == n_seq
        chunk, tile_rows, n_steps = PROMPT_CHUNK, PROMPT_CHUNK, seq_len // PROMPT_CHUNK
        x_spec = pl.BlockSpec((n_seq, chunk, D_MODEL), lambda j: (0, j, 0))
        mod_spec = pl.BlockSpec((n_seq, 3 * D_MODEL), lambda j: (0, 0))
        tab_spec = pl.BlockSpec((chunk, HEAD_DIM), lambda j: (j, 0))
        seq_map = lambda j: 0
    else:
        chunk = pos.shape[0]
        tile_rows = SUBLANES
        n_total = x.shape[0] // chunk
        n_steps = n_total // n_seq
        x_spec = pl.BlockSpec((n_seq * chunk, D_MODEL), lambda j: (j, 0))
        mod_spec = pl.BlockSpec((n_seq, 3 * D_MODEL), lambda j: (j, 0))
        tab_spec = _const_spec((tile_rows, HEAD_DIM))
        seq_map = lambda j: j
    rows = n_seq * chunk
    cos, sin = _rope_tables(pos)
    if not prompt:
        reps = tile_rows // chunk
        cos, sin = np.tile(cos, (reps, 1)), np.tile(sin, (reps, 1))
    dmask, cdec, sdec = _decay_tables(chunk, tile_rows)
    chunk_decay = tuple(float(math.exp(chunk * math.log1p(-2.0 ** (-5.0 - h)))) for h in range(RET_HEADS))
    sret0, sre0, sim0 = states

    st_ret_spec = pl.BlockSpec((n_seq, RET_HEADS, HEAD_DIM, HEAD_DIM), lambda j: (seq_map(j), 0, 0, 0))
    st_s5_spec = pl.BlockSpec((n_seq, SSM_CH), lambda j: (seq_map(j), 0))
    consts = [dmask, cdec, sdec, wts["rnw"], wts["bmat"], wts["cre"], wts["cim"], wts["lbr"], wts["lbi"],
              wts["dsk"], wts["w_glu"], wts["b_glu"], wts["snw"], wts["w_out"]]
    args = [x, mod, wts["n1w"], wts["w_in"], cos, sin] + consts + [sret0, sre0, sim0]
    in_specs = ([x_spec, mod_spec, _const_spec(wts["n1w"].shape), _const_spec(wts["w_in"].shape), tab_spec, tab_spec]
                + [_const_spec(a.shape) for a in consts] + [st_ret_spec, st_s5_spec, st_s5_spec])

    kern = functools.partial(_mixer_kernel, n_seq=n_seq, chunk=chunk, tile_rows=tile_rows, carry=prompt,
                             chunk_decay=chunk_decay)
    out_shape = (
        jax.ShapeDtypeStruct(x.shape, F32),
        jax.ShapeDtypeStruct((n_total, RET_HEADS, HEAD_DIM, HEAD_DIM), F32),
        jax.ShapeDtypeStruct((n_total, SSM_CH), F32),
        jax.ShapeDtypeStruct((n_total, SSM_CH), F32),
    )
    scratch = [
        pltpu.VMEM((rows, D_MODEL), BF16),
        pltpu.VMEM((rows, 4 * RET_WIDTH), F32),
        pltpu.VMEM((SSM_WIDTH // LANES, n_seq * _seq_pitch(chunk), LANES), F32),
        pltpu.VMEM((rows, D_MODEL), F32),
        pltpu.VMEM((rows, SSM_WIDTH), F32),
        pltpu.VMEM((rows, SSM_CH // 2), F32),
        pltpu.VMEM((rows, SSM_CH // 2), F32),
        pltpu.VMEM((rows, SSM_CH // 2), F32),
        pltpu.VMEM((rows, SSM_CH // 2), F32),
        pltpu.VMEM((rows, SSM_WIDTH), F32),
        pltpu.VMEM((SSM_WIDTH // LANES, n_seq * _seq_pitch(chunk), LANES), F32),
        pltpu.VMEM((3 * D_MODEL // LANES, SUBLANES if prompt else rows, LANES), F32),
    ]
    return pl.pallas_call(
        kern,
        grid=(n_steps,),
        in_specs=in_specs,
        out_specs=(x_spec, st_ret_spec, st_s5_spec, st_s5_spec),
        out_shape=out_shape,
        scratch_shapes=scratch,
        compiler_params=pltpu.CompilerParams(dimension_semantics=("arbitrary",), vmem_limit_bytes=VMEM_LIMIT),
        name="mixer_prompt" if prompt else "mixer_sample",
    )(*args)


PACK_ROWS = D_MODEL // (2 * LANES)


def _store_packed(ref, x):
    half = D_MODEL // 2
    bits = lax.bitcast_convert_type(x.astype(BF16).astype(F32), jnp.uint32)
    words = bits[:, :half] | (bits[:, half:] >> 16)
    for c in range(PACK_ROWS):
        ref[c] = words[:, c * LANES:(c + 1) * LANES]


def _load_packed(ref, n, first_row=0, row_stride=1):
    hi, lo = [], []
    for c in range(PACK_ROWS):
        w = ref[c] if row_stride == 1 else ref[c, pl.ds(first_row, n, stride=row_stride), :]
        hi.append(lax.bitcast_convert_type(w & jnp.uint32(0xFFFF0000), F32))
        lo.append(lax.bitcast_convert_type(w << 16, F32))
    return jnp.concatenate(hi + lo, axis=1)


def _split_bf16(x):
    hi = x.astype(BF16)
    return hi, (x - hi.astype(F32)).astype(BF16)


def _route_tile(x, sh, sc, n2w_ref, wrh_ref, wrm_ref, br_ref, ltri_ref, count_ref, h2_ref, route_ref, topw_ref):
    ms = jnp.mean(x * x, axis=-1, keepdims=True)
    h2 = x * lax.rsqrt(ms + NORM_EPS) * n2w_ref[...] * (1.0 + sc) + sh
    _store_packed(h2_ref, h2)
    hh, hm = _split_bf16(h2)
    logits = (jnp.dot(hh, wrh_ref[...], preferred_element_type=F32)
              + (jnp.dot(hh, wrm_ref[...], preferred_element_type=F32)
                 + jnp.dot(hm, wrh_ref[...], preferred_element_type=F32))) + br_ref[...]
    lane = lax.broadcasted_iota(jnp.int32, logits.shape, 1)
    work = logits
    vals, idxs = [], []
    for _ in range(TOP_K):
        m = jnp.max(work, axis=-1, keepdims=True)
        idx = jnp.min(jnp.where(work == m, lane, LANES), axis=-1, keepdims=True)
        vals.append(m)
        idxs.append(idx)
        work = jnp.where(lane == idx, -jnp.inf, work)
    exps = [jnp.exp(v - vals[0]) for v in vals]
    tot = exps[0] + exps[1] + exps[2] + exps[3]
    topw = jnp.zeros(logits.shape, F32)
    for k in range(TOP_K):
        topw = jnp.where(lane == k, exps[k] / tot, topw)
    topw_ref[...] = topw

    onehot = [(lane == idxs[k]).astype(F32) for k in range(TOP_K)]
    chosen = onehot[0] + onehot[1] + onehot[2] + onehot[3]
    before = jnp.dot(ltri_ref[...], chosen.astype(BF16), preferred_element_type=F32) + count_ref[...]
    info = jnp.zeros(logits.shape, jnp.int32)
    for k in range(TOP_K):
        rank = jnp.sum(onehot[k] * before, axis=-1, keepdims=True).astype(jnp.int32)
        info = jnp.where(lane == k, idxs[k], info)
        info = jnp.where(lane == TOP_K + k, rank, info)
    route_ref[...] = jnp.transpose(info)[:2 * TOP_K, :]
    count_ref[...] = count_ref[...] + jnp.sum(chosen, axis=0, keepdims=True)


def _router_kernel(xp_ref, shp_ref, scp_ref, xs_ref, shs_ref, scs_ref, n2w_ref, wrh_ref, wrm_ref, br_ref, ltri_ref,
                   h2_ref, route_ref, topw_ref, count_ref, modx_ref, *, n_prompt_tiles, sample_len):
    i = pl.program_id(0)
    rest = (n2w_ref, wrh_ref, wrm_ref, br_ref, ltri_ref, count_ref, h2_ref, route_ref, topw_ref)

    @pl.when(i == 0)
    def _():
        count_ref[...] = jnp.zeros(count_ref.shape, F32)

    @pl.when(i < n_prompt_tiles)
    def _():
        _route_tile(xp_ref[...], shp_ref[0], scp_ref[0], *rest)

    @pl.when(i >= n_prompt_tiles)
    def _():
        dl = D_MODEL // LANES
        _expand_rows(modx_ref, shs_ref[...], sample_len)
        _expand_rows(modx_ref, scs_ref[...], sample_len, first_chunk=dl)
        _route_tile(xs_ref[...], _expanded(modx_ref, 0), _expanded(modx_ref, dl), *rest)


def _router(xp_rows, mod_p, seq_len, xs_rows, mod_s, sample_len, n2w, w_router, b_router):
    tile = ROUTER_TILE
    n_p, n_s = xp_rows.shape[0], xs_rows.shape[0]
    tp, ts = n_p // tile, n_s // tile
    n_total = n_p + n_s
    mod_p = mod_p.reshape(mod_p.shape[0], 1, mod_p.shape[1])
    seq_of = lambda i: (jnp.minimum(i, tp - 1) * tile) // seq_len
    wr_pad = jnp.pad(w_router, ((0, 0), (0, LANES - N_EXPERTS)))
    wr_hi = wr_pad.astype(BF16)
    wr_mid = (wr_pad - wr_hi.astype(F32)).astype(BF16)
    br_pad = jnp.pad(b_router, ((0, 0), (0, LANES - N_EXPERTS)), constant_values=-1e30)
    ltri = jnp.asarray(np.tril(np.ones((tile, tile), np.float32), -1), BF16)
    clamp_p = lambda i: jnp.minimum(i, tp - 1)
    clamp_s = lambda i: jnp.maximum(i - tp, 0)
    return pl.pallas_call(
        functools.partial(_router_kernel, n_prompt_tiles=tp, sample_len=sample_len),
        grid=(tp + ts,),
        in_specs=[pl.BlockSpec((tile, D_MODEL), lambda i: (clamp_p(i), 0)),
                  pl.BlockSpec((1, 1, D_MODEL), lambda i: (seq_of(i), 0, 3)),
                  pl.BlockSpec((1, 1, D_MODEL), lambda i: (seq_of(i), 0, 4)),
                  pl.BlockSpec((tile, D_MODEL), lambda i: (clamp_s(i), 0)),
                  pl.BlockSpec((tile // sample_len, D_MODEL), lambda i: (clamp_s(i), 3)),
                  pl.BlockSpec((tile // sample_len, D_MODEL), lambda i: (clamp_s(i), 4)),
                  _const_spec(n2w.shape), _const_spec(wr_hi.shape), _const_spec(wr_mid.shape),
                  _const_spec(br_pad.shape), _const_spec(ltri.shape)],
        out_specs=(pl.BlockSpec((PACK_ROWS, tile, LANES), lambda i: (0, i, 0)),
                   pl.BlockSpec((2 * TOP_K, tile), lambda i: (0, i)),
                   pl.BlockSpec((tile, LANES), lambda i: (i, 0)),
                   pl.BlockSpec((1, LANES), lambda i: (0, 0))),
        out_shape=(jax.ShapeDtypeStruct((PACK_ROWS, n_total, LANES), jnp.uint32),
                   jax.ShapeDtypeStruct((2 * TOP_K, n_total), jnp.int32),
                   jax.ShapeDtypeStruct((n_total, LANES), F32),
                   jax.ShapeDtypeStruct((1, LANES), F32)),
        scratch_shapes=[pltpu.VMEM((2 * D_MODEL // LANES, tile, LANES), F32)],
        compiler_params=pltpu.CompilerParams(dimension_semantics=("arbitrary",), vmem_limit_bytes=VMEM_LIMIT),
        name="router",
    )(xp_rows, mod_p, mod_p, xs_rows, mod_s, mod_s, n2w, wr_hi, wr_mid, br_pad, ltri)


def _gather_rows(table, idx):
    n = idx.shape[0]
    steps = n // SC_GATHER_WINDOW
    assert n % SC_GATHER_WINDOW == 0 and steps % SC_WORKERS == 0
    mesh = plsc.VectorSubcoreMesh(core_axis_name="c", subcore_axis_name="s")

    @functools.partial(pl.kernel, out_type=jax.ShapeDtypeStruct((n, table.shape[1]), table.dtype), mesh=mesh,
                       scratch_types=[])
    def gather_kernel(table_hbm, idx_hbm, out_hbm):
        def body(idx_vmem, out_vmem):
            pltpu.sync_copy(table_hbm.at[idx_vmem.at[0]], out_vmem)

        pltpu.emit_pipeline(
            body,
            grid=(steps,),
            in_specs=[pl.BlockSpec((1, SC_GATHER_WINDOW), lambda i: (0, i))],
            out_specs=[pl.BlockSpec((SC_GATHER_WINDOW, table.shape[1]), lambda i: (i, 0))],
            core_axis_name=("c", "s"),
            dimension_semantics=(pltpu.PARALLEL,),
        )(idx_hbm, out_hbm)

    return gather_kernel(table, idx.reshape(1, n))


def _dispatch_packed(table, slot_kt, n_slots):
    planes, n_tok, lanes = table.shape
    win = SC_GATHER_WINDOW
    blocks = n_tok // win
    assert n_tok % win == 0
    blocks_pad = _round_up(blocks, SC_WORKERS // math.gcd(SC_WORKERS, planes))
    n_spare = (blocks_pad - blocks) * TOP_K * win
    n_ext = n_slots + n_spare
    dest = jnp.transpose(slot_kt.reshape(TOP_K, blocks, win), (1, 0, 2))
    spare = n_slots + jnp.arange(n_spare, dtype=jnp.int32).reshape(blocks_pad - blocks, TOP_K, win)
    dest = jnp.concatenate([dest, spare], axis=0)[None] + (jnp.arange(planes, dtype=jnp.int32) * n_ext)[:, None, None, None]
    dest = dest.reshape(planes * blocks_pad * TOP_K, win)
    steps = planes * blocks_pad
    src_block = lambda g: (g // blocks_pad) * blocks + jnp.minimum(g % blocks_pad, blocks - 1)
    mesh = plsc.VectorSubcoreMesh(core_axis_name="c", subcore_axis_name="s")

    @functools.partial(pl.kernel, out_type=jax.ShapeDtypeStruct((planes * n_ext, lanes), table.dtype), mesh=mesh,
                       scratch_types=[])
    def scatter_kernel(table_hbm, dest_hbm, out_hbm):
        def body(rows_vmem, dest_vmem):
            for k in range(TOP_K):
                pltpu.sync_copy(rows_vmem, out_hbm.at[dest_vmem.at[k]])

        pltpu.emit_pipeline(
            body,
            grid=(steps,),
            in_specs=[pl.BlockSpec((win, lanes), lambda g: (src_block(g), 0)),
                      pl.BlockSpec((TOP_K, win), lambda g: (g, 0))],
            out_specs=[],
            core_axis_name=("c", "s"),
            dimension_semantics=(pltpu.PARALLEL,),
        )(table_hbm, dest_hbm)

    return scatter_kernel(table.reshape(planes * n_tok, lanes), dest).reshape(planes, n_ext, lanes)


def _gather_packed(table, rows):
    planes, n_table, lanes = table.shape
    idx = jnp.concatenate([rows + c * n_table for c in range(planes)])
    out = _gather_rows(table.reshape(planes * n_table, lanes), idx)
    return out.reshape(planes, rows.shape[0], lanes)


def _expert_weight_copies(e, w1_hbm, w2_hbm, w1s_ref, w2s_ref, sem):
    return (pltpu.make_async_copy(w1_hbm.at[e], w1s_ref, sem.at[0]),
            pltpu.make_async_copy(w2_hbm.at[e], w2s_ref, sem.at[1]))


def _experts_kernel(te_ref, tr_ref, nx_ref, nv_ref, xs_ref, w1_hbm, b1_ref, w2_hbm, b2_ref, ys_ref,
                    w1s_ref, w2s_ref, w1b_ref, w2b_ref, sem):
    i = pl.program_id(0)
    e = te_ref[i]
    new_expert = (i == 0) | (e != te_ref[jnp.maximum(i - 1, 0)])
    copies = functools.partial(_expert_weight_copies, w1_hbm=w1_hbm, w2_hbm=w2_hbm, w1s_ref=w1s_ref,
                               w2s_ref=w2s_ref, sem=sem)

    @pl.when(i == 0)
    def _():
        for c in copies(e):
            c.start()

    @pl.when(new_expert)
    def _():
        for c in copies(e):
            c.wait()
        w1b_ref[...] = w1s_ref[...].astype(BF16)
        w2b_ref[...] = w2s_ref[...].astype(BF16)

        @pl.when(nx_ref[i] >= 0)
        def _():
            for c in copies(nx_ref[i]):
                c.start()

    def expert_pass(r0, n_rows, rows_valid):
        xs_v, ys_v = xs_ref.at[:, pl.ds(r0, n_rows)], ys_ref.at[:, pl.ds(r0, n_rows)]
        row = lax.broadcasted_iota(jnp.int32, (n_rows, D_MODEL), 0)
        x = jnp.where(row < rows_valid, _load_packed(xs_v, n_rows), 0.0).astype(BF16)
        hu = jnp.dot(x, w1b_ref[...], preferred_element_type=F32) + b1_ref[0]
        x_glu = jnp.minimum(hu[:, :D_FF], SWIGLU_LIMIT)
        x_lin = jnp.clip(hu[:, D_FF:], -SWIGLU_LIMIT, SWIGLU_LIMIT)
        act = x_glu * jax.nn.sigmoid(SWIGLU_ALPHA * x_glu) * (x_lin + 1.0)
        _store_packed(ys_v, jnp.dot(act.astype(BF16), w2b_ref[...], preferred_element_type=F32) + b2_ref[0])

    def zero_rows(r0, n_rows):
        for c in range(PACK_ROWS):
            ys_ref[c, pl.ds(r0, n_rows), :] = jnp.zeros((n_rows, LANES), jnp.uint32)

    half = EXPERT_ROWS // 2
    rows_tile = jnp.where(i < nv_ref[0], tr_ref[i], 0)
    full_tile = rows_tile > SLOT_TILE - half

    @pl.when(full_tile)
    def _():
        expert_pass(0, SLOT_TILE, rows_tile)

    for h in range(SLOT_TILE // EXPERT_ROWS):
        r0 = h * EXPERT_ROWS
        rows_here = jnp.where(full_tile, -1, rows_tile - r0)

        @pl.when(rows_here > half)
        def _(r0=r0, rows_here=rows_here):
            expert_pass(r0, EXPERT_ROWS, rows_here)

        @pl.when((rows_here > 0) & (rows_here <= half))
        def _(r0=r0, rows_here=rows_here):
            expert_pass(r0, half, rows_here)
            zero_rows(r0 + half, half)

        @pl.when((rows_here <= 0) & jnp.logical_not(full_tile))
        def _(r0=r0):
            zero_rows(r0, EXPERT_ROWS)


def _experts(tile_expert, tile_rows, next_expert, n_valid, xs, w1, b1, w2, b2):
    n_tiles = tile_expert.shape[0]
    n_slots = n_tiles * SLOT_TILE
    grid_spec = pltpu.PrefetchScalarGridSpec(
        num_scalar_prefetch=4,
        grid=(n_tiles,),
        in_specs=[
            pl.BlockSpec((PACK_ROWS, SLOT_TILE, LANES), lambda i, te, tr, nx, nv: (0, i, 0)),
            pl.BlockSpec(memory_space=pl.ANY),
            pl.BlockSpec((1, 1, 2 * D_FF), lambda i, te, tr, nx, nv: (te[i], 0, 0)),
            pl.BlockSpec(memory_space=pl.ANY),
            pl.BlockSpec((1, 1, D_MODEL), lambda i, te, tr, nx, nv: (te[i], 0, 0)),
        ],
        out_specs=pl.BlockSpec((PACK_ROWS, SLOT_TILE, LANES), lambda i, te, tr, nx, nv: (0, i, 0)),
        scratch_shapes=[pltpu.VMEM((D_MODEL, 2 * D_FF), F32), pltpu.VMEM((D_FF, D_MODEL), F32),
                        pltpu.VMEM((D_MODEL, 2 * D_FF), BF16), pltpu.VMEM((D_FF, D_MODEL), BF16),
                        pltpu.SemaphoreType.DMA((2,))],
    )
    return pl.pallas_call(
        _experts_kernel,
        grid_spec=grid_spec,
        out_shape=jax.ShapeDtypeStruct((PACK_ROWS, n_slots, LANES), jnp.uint32),
        compiler_params=pltpu.CompilerParams(dimension_semantics=("arbitrary",), vmem_limit_bytes=VMEM_LIMIT),
        name="experts",
    )(tile_expert, tile_rows, next_expert, n_valid, xs, w1, b1.reshape(N_EXPERTS, 1, 2 * D_FF), w2,
      b2.reshape(N_EXPERTS, 1, D_MODEL))


def _combine_kernel(y4_ref, x1_ref, topw_ref, g2_ref, fw_ref, *rest, reps):
    o_ref = rest[-2] if reps else rest[-1]
    w = topw_ref[...]
    n = w.shape[0]
    ff = None
    for k in range(TOP_K):
        yk = w[:, k:k + 1] * _load_packed(y4_ref.at[:, k], n)
        ff = yk if ff is None else ff + yk
    if reps:
        _expand_rows(rest[-1], g2_ref[...], reps)
        g2 = _expanded(rest[-1], 0)
    else:
        g2 = g2_ref[0]
    x = x1_ref[...] + g2 * ff
    ms = jnp.mean(x * x, axis=-1, keepdims=True)
    o_ref[...] = x * lax.rsqrt(ms + NORM_EPS) * fw_ref[...]


def _combine(y4, y4_row0, x1, x1_row0, n_rows, topw, topw_row0, mod, rows_per_mod, fw, out_buf):
    tile = COMBINE_TILE
    y4_off, x1_off, tw_off = y4_row0 // tile, x1_row0 // tile, topw_row0 // tile
    reps = rows_per_mod if rows_per_mod < tile else 0
    scratch = []
    if reps:
        g2_spec = pl.BlockSpec((tile // reps, D_MODEL), lambda i: (i + x1_off, 5))
        scratch = [pltpu.VMEM((D_MODEL // LANES, tile, LANES), F32)]
    else:
        mod = mod.reshape(mod.shape[0], 1, mod.shape[1])
        g2_spec = pl.BlockSpec((1, 1, D_MODEL), lambda i: (((i + x1_off) * tile) // rows_per_mod, 0, 5))
    in_specs = [pl.BlockSpec((PACK_ROWS, TOP_K, tile, LANES), lambda i: (0, 0, i + y4_off, 0)),
                pl.BlockSpec((tile, D_MODEL), lambda i: (i + x1_off, 0)),
                pl.BlockSpec((tile, LANES), lambda i: (i + tw_off, 0)),
                g2_spec, _const_spec(fw.shape)]
    args = [y4, x1, topw, mod, fw]
    aliases = {}
    if out_buf is not None:
        in_specs.append(pl.BlockSpec(memory_space=pl.ANY))
        args.append(out_buf)
        aliases = {len(args) - 1: 0}
    return pl.pallas_call(
        functools.partial(_combine_kernel, reps=reps),
        grid=(n_rows // tile,),
        in_specs=in_specs,
        out_specs=pl.BlockSpec((tile, D_MODEL), lambda i: (i + x1_off, 0)),
        out_shape=jax.ShapeDtypeStruct(x1.shape, F32),
        input_output_aliases=aliases,
        scratch_shapes=scratch,
        compiler_params=pltpu.CompilerParams(dimension_semantics=("arbitrary",), vmem_limit_bytes=VMEM_LIMIT),
        name="combine",
    )(*args)


def _routing_tables(route, counts, n_slots):
    padded = ((counts + SLOT_TILE - 1) // SLOT_TILE) * SLOT_TILE
    pend = jnp.cumsum(padded)
    poff = pend - padded
    expert_kt, rank_kt = route[:TOP_K], route[TOP_K:]
    experts = jnp.arange(N_EXPERTS, dtype=jnp.int32)
    start_kt = jnp.sum((expert_kt[None] == experts[:, None, None]).astype(jnp.int32) * poff[:, None, None], axis=0)
    slot_kt = start_kt + rank_kt
    n_tiles = n_slots // SLOT_TILE
    n_valid = (pend[-1] // SLOT_TILE).astype(jnp.int32)
    tile_row = jnp.minimum(jnp.arange(n_tiles, dtype=jnp.int32), n_valid - 1) * SLOT_TILE
    in_later = (pend[None, :] <= tile_row[:, None]).astype(jnp.int32)
    tile_e = jnp.sum(in_later, axis=1).astype(jnp.int32)
    is_e = (experts[None, :] == tile_e[:, None]).astype(jnp.int32)
    used_end = jnp.sum(is_e * (poff + counts)[None, :], axis=1)
    tile_rows = jnp.clip(used_end - tile_row, 0, SLOT_TILE).astype(jnp.int32)
    later_used = (experts[None, :] > tile_e[:, None]) & (counts[None, :] > 0)
    next_e = jnp.min(jnp.where(later_used, experts[None, :], N_EXPERTS), axis=1)
    next_e = jnp.where(next_e < N_EXPERTS, next_e, -1).astype(jnp.int32)
    return slot_kt, tile_e, tile_rows, next_e, n_valid.reshape(1)


def _round_up(n, m):
    return ((n + m - 1) // m) * m


def kernel(x_prompt, x_sample, c_prompt, c_sample, state_ret, state_s5_re, state_s5_im, norm1_w, norm2_w, w_ada, b_ada, w_in, ret_norm_w, s5_lam_re, s5_lam_im, s5_log_dt, s5_b_re, s5_b_im, s5_c_re, s5_c_im, s5_d, w_glu, b_glu, s5_norm_w, w_out, w_router, b_router, w1, b1, w2, b2, final_w):
    bp, lp, _ = x_prompt.shape
    bs, ls, _ = x_sample.shape
    assert norm1_w.shape[0] == 1, "single-layer model"
    n_p, n_s = bp * lp, bs * ls
    n_tok = n_p + n_s

    mod = _ada(jnp.concatenate([c_prompt, c_sample], axis=0), w_ada[0], b_ada[0])
    mod_p, mod_s = mod[:bp], mod[bp:]

    lbr, lbi, bbr, bbi = _s5prep(s5_lam_re[0], s5_lam_im[0], s5_log_dt[0], s5_b_re[0], s5_b_im[0])
    bmat = jnp.concatenate([_block_diag(bbr), _block_diag(bbi)], axis=-1).astype(BF16)
    cre = _block_diag(jnp.transpose(s5_c_re[0], (0, 2, 1))).astype(BF16)
    cim = _block_diag(jnp.transpose(-s5_c_im[0], (0, 2, 1))).astype(BF16)
    wts = dict(
        n1w=norm1_w, w_in=w_in[0].astype(BF16), rnw=ret_norm_w, bmat=bmat, cre=cre, cim=cim,
        lbr=lbr.reshape(1, SSM_CH), lbi=lbi.reshape(1, SSM_CH), dsk=s5_d[0].reshape(1, SSM_WIDTH),
        w_glu=w_glu[0].astype(BF16), b_glu=b_glu, snw=s5_norm_w, w_out=w_out[0].astype(BF16),
    )

    zero_states = (jnp.zeros((bp, RET_HEADS, HEAD_DIM, HEAD_DIM), F32), jnp.zeros((bp, SSM_CH), F32),
                   jnp.zeros((bp, SSM_CH), F32))
    x1_p, ret_p, re_p, im_p = _mixer(x_prompt, mod_p, np.arange(lp, dtype=np.float32), zero_states, wts,
                                     prompt=True)
    sample_states = (state_ret[0], state_s5_re[0].reshape(bs, SSM_CH), state_s5_im[0].reshape(bs, SSM_CH))
    x1_s, ret_s, re_s, im_s = _mixer(x_sample.reshape(n_s, D_MODEL), mod_s,
                                     PAST_LEN + np.arange(ls, dtype=np.float32), sample_states, wts, prompt=False)

    x1_p_rows = x1_p.reshape(n_p, D_MODEL)
    h2, route, topw, counts = _router(x1_p_rows, mod_p, lp, x1_s, mod_s, ls, norm2_w, w_router[0], b_router)

    n_assign = n_tok * TOP_K
    gather_quantum = SC_GATHER_WINDOW * SC_WORKERS // PACK_ROWS
    assert n_assign % gather_quantum == 0
    n_slots = _round_up(_round_up(n_assign, SLOT_TILE) + N_EXPERTS * SLOT_TILE, gather_quantum)
    slot_kt, tile_e, tile_rows, next_e, n_valid = _routing_tables(route, counts[0, :N_EXPERTS].astype(jnp.int32),
                                                                  n_slots)
    xs = _dispatch_packed(h2, slot_kt, n_slots)
    ys = _experts(tile_e, tile_rows, next_e, n_valid, xs, w1[0], b1[0], w2[0], b2[0])
    fw = final_w.reshape(1, D_MODEL)
    y_p, y_s = None, None
    bounds = [r * (n_p // COMBINE_RANGES) for r in range(COMBINE_RANGES)] + [n_tok]
    for lo, hi in zip(bounds[:-1], bounds[1:]):
        y4 = _gather_packed(ys, slot_kt[:, lo:hi].reshape(-1)).reshape(PACK_ROWS, TOP_K, hi - lo, LANES)
        y_p = _combine(y4, 0, x1_p_rows, lo, min(hi, n_p) - lo, topw, lo, mod_p, lp, fw, y_p)
        if hi > n_p:
            y_s = _combine(y4, n_p - lo, x1_s, 0, n_s, topw, n_p, mod_s, ls, fw, None)

    g, p = SSM_GROUPS, SSM_STATE
    return (y_p.reshape(bp, lp, D_MODEL), y_s.reshape(bs, ls, D_MODEL),
            ret_p[None], re_p.reshape(1, bp, g, p), im_p.reshape(1, bp, g, p),
            ret_s[None], re_s.reshape(1, bs, g, p), im_s.reshape(1, bs, g, p))
```

```python
import functools
import math

import jax
import jax.numpy as jnp
import numpy as np
from jax import lax
from jax.experimental import pallas as pl
from jax.experimental.pallas import tpu as pltpu
from jax.experimental.pallas import tpu_sc as plsc

F32 = jnp.float32
BF16 = jnp.bfloat16

D_MODEL = 1024
PAST_LEN = 16384
RET_WIDTH = 512
RET_HEADS = 4
HEAD_DIM = 128
ROPE_BASE = 10000.0
SSM_WIDTH = 512
SSM_GROUP = 16
SSM_GROUPS = 32
SSM_STATE = 64
SSM_CH = SSM_GROUPS * SSM_STATE
N_EXPERTS = 32
TOP_K = 4
D_FF = 1024
SWIGLU_LIMIT = 7.0
SWIGLU_ALPHA = 1.702
NORM_EPS = 1e-6

LANES = 128
SUBLANES = 8
VMEM_LIMIT = 56 * 1024 * 1024

SEQ_PER_BLOCK = 8
SAMPLE_SEQ_PER_BLOCK = 16
SCAN_ELEMS = 8 * 1024
PROMPT_CHUNK = 64
S5_BLOCK_GROUPS = 8
N_S5_BLOCKS = SSM_GROUPS // S5_BLOCK_GROUPS
S5_BLOCK_IN = S5_BLOCK_GROUPS * SSM_GROUP
S5_BLOCK_CH = S5_BLOCK_GROUPS * SSM_STATE
ROUTER_TILE = 512
SLOT_TILE = 512
EXPERT_ROWS = 256
COMBINE_TILE = 256
COMBINE_RANGES = 2
SC_GATHER_WINDOW = 128
SC_WORKERS = 32


def _silu(x):
    return x * jax.nn.sigmoid(x)


def _expand_rows(dst_ref, src, reps, first_chunk=0):
    n = src.shape[0]
    for c in range(src.shape[1] // LANES):
        piece = src[:, c * LANES:(c + 1) * LANES]
        for t in range(reps):
            dst_ref[first_chunk + c, pl.ds(t, n, stride=reps), :] = piece


def _expanded(ref, first_chunk, n_chunks=D_MODEL // LANES):
    return jnp.concatenate([ref[c] for c in range(first_chunk, first_chunk + n_chunks)], axis=1)


def _ada_kernel(c_ref, w_ref, b_ref, o_ref):
    sh, sm = _split_bf16(_silu(c_ref[...]))
    wh, wm = _split_bf16(w_ref[...])
    o_ref[...] = (jnp.dot(sh, wh, preferred_element_type=F32)
                  + (jnp.dot(sh, wm, preferred_element_type=F32) + jnp.dot(sm, wh, preferred_element_type=F32))
                  + b_ref[...])


def _ada(c_all, w_ada, b_ada):
    n_rows, n_out = c_all.shape[0], w_ada.shape[1]
    tn = 1536
    return pl.pallas_call(
        _ada_kernel,
        grid=(n_out // tn,),
        in_specs=[
            pl.BlockSpec((n_rows, D_MODEL), lambda j: (0, 0)),
            pl.BlockSpec((D_MODEL, tn), lambda j: (0, j)),
            pl.BlockSpec((1, tn), lambda j: (0, j)),
        ],
        out_specs=pl.BlockSpec((n_rows, tn), lambda j: (0, j)),
        out_shape=jax.ShapeDtypeStruct((n_rows, n_out), F32),
        compiler_params=pltpu.CompilerParams(dimension_semantics=("arbitrary",), vmem_limit_bytes=VMEM_LIMIT),
        name="ada",
    )(c_all, w_ada, b_ada.reshape(1, n_out))


def _s5prep_kernel(lre_ref, lim_ref, ldt_ref, bre_ref, bim_ref, lbr_ref, lbi_ref, bbr_ref, bbi_ref):
    lam_re, lam_im = lre_ref[...], lim_ref[...]
    dt = jnp.exp(ldt_ref[...])
    mag = jnp.exp(lam_re * dt)
    ang = lam_im * dt
    lb_re, lb_im = mag * jnp.cos(ang), mag * jnp.sin(ang)
    den = lam_re * lam_re + lam_im * lam_im
    f_re = ((lb_re - 1.0) * lam_re + lb_im * lam_im) / den
    f_im = (lb_im * lam_re - (lb_re - 1.0) * lam_im) / den
    lbr_ref[...] = lb_re
    lbi_ref[...] = lb_im
    b_re, b_im = bre_ref[...], bim_ref[...]
    bbr_ref[...] = f_re[:, None, :] * b_re - f_im[:, None, :] * b_im
    bbi_ref[...] = f_re[:, None, :] * b_im + f_im[:, None, :] * b_re


def _s5prep(lam_re, lam_im, log_dt, b_re, b_im):
    g, p = lam_re.shape
    bt_re = jnp.transpose(b_re, (0, 2, 1))
    bt_im = jnp.transpose(b_im, (0, 2, 1))
    return pl.pallas_call(
        _s5prep_kernel,
        out_shape=(
            jax.ShapeDtypeStruct((g, p), F32), jax.ShapeDtypeStruct((g, p), F32),
            jax.ShapeDtypeStruct((g, SSM_GROUP, p), F32), jax.ShapeDtypeStruct((g, SSM_GROUP, p), F32),
        ),
        name="s5prep",
    )(lam_re, lam_im, log_dt.reshape(g, 1), bt_re, bt_im)


def _block_diag(blocks):
    _, r, c = blocks.shape
    b4 = blocks.reshape(N_S5_BLOCKS, S5_BLOCK_GROUPS, r, c)
    eye = jnp.eye(S5_BLOCK_GROUPS, dtype=blocks.dtype)
    out = b4[:, :, :, None, :] * eye[None, :, None, :, None]
    return out.reshape(N_S5_BLOCKS, S5_BLOCK_GROUPS * r, S5_BLOCK_GROUPS * c)


def _mixer_kernel(x_ref, mod_ref, n1w_ref, win_ref, cos_ref, sin_ref, dmask_ref, cdec_ref, sdec_ref,
                  rnw_ref, bmat_ref, cre_ref, cim_ref, lbr_ref, lbi_ref, dsk_ref, wglu_ref, bglu_ref,
                  snw_ref, wout_ref, sret0_ref, sre0_ref, sim0_ref,
                  x1_ref, sret_ref, sre_ref, sim_ref,
                  hb_ref, z_ref, zu_ref, oy_ref, utb_ref, bur0_ref, bur1_ref, bui0_ref, bui1_ref, ytb_ref, yb_ref, modx_ref,
                  *, n_seq, chunk, tile_rows, carry, chunk_decay):
    rows = n_seq * chunk
    seq_per_tile = tile_rows // chunk
    n_tiles = rows // tile_rows
    per_row_mod = chunk % SUBLANES != 0
    dl = D_MODEL // LANES

    def load_states():
        sret_ref[...] = sret0_ref[...]
        sre_ref[...] = sre0_ref[...]
        sim_ref[...] = sim0_ref[...]

    if carry:
        pl.when(pl.program_id(0) == 0)(load_states)
    else:
        load_states()

    if per_row_mod:
        _expand_rows(modx_ref, mod_ref[...], chunk)

    n1w = n1w_ref[...]
    mod_rows = rows if per_row_mod else chunk
    for i in range(rows // mod_rows):
        r0 = i * mod_rows
        xb = _load_rows(x_ref, r0, mod_rows, chunk)
        if per_row_mod:
            sh, sc = _expanded(modx_ref, 0), _expanded(modx_ref, dl)
        else:
            sh = mod_ref[pl.ds(i, 1), pl.ds(0, D_MODEL)]
            sc = mod_ref[pl.ds(i, 1), pl.ds(D_MODEL, D_MODEL)]
        ms = jnp.mean(xb * xb, axis=-1, keepdims=True)
        hn = xb * lax.rsqrt(ms + NORM_EPS) * n1w
        hb_ref[pl.ds(r0, mod_rows), :] = (hn * (1.0 + sc) + sh).astype(BF16)
    ret_w = 4 * RET_WIDTH
    z_ref[...] = jnp.dot(hb_ref[...], win_ref[:, pl.ds(0, ret_w)], preferred_element_type=F32)
    zu = jnp.dot(hb_ref[...], win_ref[:, pl.ds(ret_w, SSM_WIDTH)], preferred_element_type=F32)
    pitch = zu_ref.shape[1] // n_seq
    for c in range(SSM_WIDTH // LANES):
        for b in range(n_seq if pitch != chunk else 1):
            nb = chunk if pitch != chunk else rows
            zu_ref[c, pl.ds(b * pitch, nb), :] = zu[b * chunk:b * chunk + nb, c * LANES:(c + 1) * LANES]

    cos = cos_ref[...]
    sin = sin_ref[...]
    scale = HEAD_DIM ** -0.5
    if seq_per_tile > 1:
        row_id = lax.broadcasted_iota(jnp.int32, (tile_rows, HEAD_DIM), 0)

    def rope(t):
        return t * cos + pltpu.roll(t, HEAD_DIM // 2, 1) * sin

    def ret_tile(ti, c):
        r0 = pl.multiple_of(ti * tile_rows, tile_rows)
        for h in range(RET_HEADS):
            c0 = h * HEAD_DIM
            q = rope(z_ref[pl.ds(r0, tile_rows), pl.ds(c0, HEAD_DIM)])
            k = rope(z_ref[pl.ds(r0, tile_rows), pl.ds(RET_WIDTH + c0, HEAD_DIM)]) * scale
            v = z_ref[pl.ds(r0, tile_rows), pl.ds(2 * RET_WIDTH + c0, HEAD_DIM)]
            g = z_ref[pl.ds(r0, tile_rows), pl.ds(3 * RET_WIDTH + c0, HEAD_DIM)]
            kd = k * sdec_ref[h]
            if tile_rows < HEAD_DIM:
                pad = jnp.zeros((HEAD_DIM - tile_rows, HEAD_DIM), F32)
                k, v, kd = (jnp.concatenate([t, pad], axis=0) for t in (k, v, kd))
                if seq_per_tile > 1:
                    row_kv = lax.broadcasted_iota(jnp.int32, (HEAD_DIM, HEAD_DIM), 0)
            elif seq_per_tile > 1:
                row_kv = row_id
            qb, kb, vb = q.astype(BF16), k.astype(BF16), v.astype(BF16)
            s = lax.dot_general(qb, kb, (((1,), (1,)), ((), ())), preferred_element_type=F32) * dmask_ref[h]
            o = jnp.dot(s.astype(BF16), vb, preferred_element_type=F32)
            cross = None
            for si in range(seq_per_tile):
                sidx = ti * seq_per_tile + si
                st = sret_ref[sidx, h]
                cr = jnp.dot(qb, st.astype(BF16), preferred_element_type=F32)
                if seq_per_tile > 1:
                    in_seq = (row_id >= si * chunk) & (row_id < (si + 1) * chunk)
                    cross = jnp.where(in_seq, cr, 0.0 if cross is None else cross)
                    kds = jnp.where((row_kv >= si * chunk) & (row_kv < (si + 1) * chunk), kd, 0.0)
                else:
                    cross, kds = cr, kd
                upd = lax.dot_general(kds.astype(BF16), vb, (((0,), (0,)), ((), ())), preferred_element_type=F32)
                sret_ref[sidx, h] = st * chunk_decay[h] + upd
            o = o + cross * cdec_ref[h]
            o = o * lax.rsqrt(jnp.mean(o * o, axis=-1, keepdims=True) + NORM_EPS)
            o = o * rnw_ref[:, pl.ds(c0, HEAD_DIM)] * _silu(g)
            oy_ref[pl.ds(r0, tile_rows), pl.ds(c0, HEAD_DIM)] = o
        return c

    lax.fori_loop(0, n_tiles, ret_tile, 0, unroll=True)

    for t in range(chunk):
        for c in range(SSM_WIDTH // LANES):
            utb_ref[pl.ds(t * n_seq, n_seq), pl.ds(c * LANES, LANES)] = zu_ref[c, pl.ds(t, n_seq, stride=pitch), :]
    half_ch = SSM_CH // 2
    blk_per_half = N_S5_BLOCKS // 2
    bur_refs, bui_refs = (bur0_ref, bur1_ref), (bui0_ref, bui1_ref)
    for blk in range(N_S5_BLOCKS):
        hf, lcols = blk // blk_per_half, pl.ds((blk % blk_per_half) * S5_BLOCK_CH, S5_BLOCK_CH)
        ub = utb_ref[:, pl.ds(blk * S5_BLOCK_IN, S5_BLOCK_IN)].astype(BF16)
        bu = jnp.dot(ub, bmat_ref[blk], preferred_element_type=F32)
        bur_refs[hf][:, lcols] = bu[:, :S5_BLOCK_CH]
        bui_refs[hf][:, lcols] = bu[:, S5_BLOCK_CH:]

    scan_w = min(half_ch, SCAN_ELEMS // n_seq)
    for hf in range(2):
        bur_ref, bui_ref = bur_refs[hf], bui_refs[hf]
        for p in range(half_ch // scan_w):
            cols = pl.ds(p * scan_w, scan_w)
            gcols = pl.ds(hf * half_ch + p * scan_w, scan_w)
            lbr = jnp.broadcast_to(lbr_ref[:, gcols], (n_seq, scan_w))
            lbi = jnp.broadcast_to(lbi_ref[:, gcols], (n_seq, scan_w))
            hr, hi = sre_ref[:, gcols], sim_ref[:, gcols]
            for t in range(chunk):
                rws = pl.ds(t * n_seq, n_seq)
                hr, hi = (lbr * hr - lbi * hi + bur_ref[rws, cols], lbr * hi + lbi * hr + bui_ref[rws, cols])
                bur_ref[rws, cols] = hr
                bui_ref[rws, cols] = hi
            sre_ref[:, gcols] = hr
            sim_ref[:, gcols] = hi

    for blk in range(N_S5_BLOCKS):
        hf, lcols = blk // blk_per_half, pl.ds((blk % blk_per_half) * S5_BLOCK_CH, S5_BLOCK_CH)
        yb = jnp.dot(bur_refs[hf][:, lcols].astype(BF16), cre_ref[blk], preferred_element_type=F32)
        yb = yb + jnp.dot(bui_refs[hf][:, lcols].astype(BF16), cim_ref[blk], preferred_element_type=F32)
        ucols = pl.ds(blk * S5_BLOCK_IN, S5_BLOCK_IN)
        ytb_ref[:, ucols] = yb + dsk_ref[:, ucols] * utb_ref[:, ucols]
    for t in range(chunk):
        for c in range(SSM_WIDTH // LANES):
            yb_ref[c, pl.ds(t, n_seq, stride=pitch), :] = ytb_ref[pl.ds(t * n_seq, n_seq), pl.ds(c * LANES, LANES)]

    def seq_major(c):
        if pitch == chunk:
            return yb_ref[c]
        return jnp.concatenate([yb_ref[c, pl.ds(b * pitch, chunk), :] for b in range(n_seq)], axis=0)

    y = jnp.concatenate([seq_major(c) for c in range(SSM_WIDTH // LANES)], axis=1)
    y = jax.nn.gelu(y, approximate=True)
    gate = jnp.dot(y.astype(BF16), wglu_ref[...], preferred_element_type=F32) + bglu_ref[...]
    y = y * jax.nn.sigmoid(gate)
    y = y * lax.rsqrt(jnp.mean(y * y, axis=-1, keepdims=True) + NORM_EPS) * snw_ref[...]
    oy_ref[:, pl.ds(RET_WIDTH, SSM_WIDTH)] = y

    mix = jnp.dot(oy_ref[...].astype(BF16), wout_ref[...], preferred_element_type=F32)
    for i in range(rows // mod_rows):
        r0 = i * mod_rows
        if per_row_mod:
            g1 = _expanded(modx_ref, 2 * dl)
        else:
            g1 = mod_ref[pl.ds(i, 1), pl.ds(2 * D_MODEL, D_MODEL)]
        _store_rows(x1_ref, r0, mod_rows, chunk,
                    _load_rows(x_ref, r0, mod_rows, chunk) + g1 * mix[r0:r0 + mod_rows])


def _load_rows(ref, r0, n, chunk):
    if len(ref.shape) == 2:
        return ref[pl.ds(r0, n), :]
    assert n == chunk and r0 % chunk == 0
    return ref[r0 // chunk]


def _store_rows(ref, r0, n, chunk, val):
    if len(ref.shape) == 2:
        ref[pl.ds(r0, n), :] = val
    else:
        assert n == chunk and r0 % chunk == 0
        ref[r0 // chunk] = val


def _seq_pitch(chunk):
    return chunk + SUBLANES if chunk % SUBLANES == 0 else chunk


def _const_spec(shape):
    nd = len(shape)
    return pl.BlockSpec(shape, lambda j, _n=nd: (0,) * _n)


def _decay_tables(chunk, tile_rows):
    f32 = np.float32
    log_gamma = np.log1p(-np.exp2(f32(-5.0) - np.arange(RET_HEADS, dtype=f32))).astype(f32)
    r = np.arange(tile_rows)
    seq, loc = r // chunk, (r % chunk).astype(f32)
    rel = loc[:, None] - loc[None, :]
    ok = (seq[:, None] == seq[None, :]) & (rel >= 0)
    dmask = np.where(ok[None], np.exp(np.where(ok, rel, f32(0.0))[None] * log_gamma[:, None, None]), f32(0.0))
    if tile_rows < HEAD_DIM:
        dmask = np.pad(dmask, ((0, 0), (0, 0), (0, HEAD_DIM - tile_rows)))
    cdec = np.exp((loc[None, :] + f32(1.0)) * log_gamma[:, None])
    sdec = np.exp((f32(chunk) - f32(1.0) - loc)[None, :] * log_gamma[:, None])
    bcast = lambda t: np.ascontiguousarray(np.broadcast_to(t[:, :, None], (RET_HEADS, tile_rows, HEAD_DIM)))
    return dmask.astype(f32), bcast(cdec.astype(f32)), bcast(sdec.astype(f32))


def _rope_tables(pos):
    f32 = np.float32
    half = HEAD_DIM // 2
    inv_freq = (f32(ROPE_BASE) ** (-np.arange(half, dtype=f32) / f32(half))).astype(f32)
    ang = (pos.astype(f32)[:, None] * inv_freq[None, :]).astype(f32)
    cos, sin = np.cos(ang).astype(f32), np.sin(ang).astype(f32)
    return np.concatenate([cos, cos], axis=-1), np.concatenate([-sin, sin], axis=-1)


def _mixer(x, mod, pos, states, wts, *, prompt):
    n_seq = SEQ_PER_BLOCK if prompt else SAMPLE_SEQ_PER_BLOCK
    if prompt:
        n_total, seq_len, _ = x.shape
        assert n_total == n_seq
        chunk, tile_rows, n_steps = PROMPT_CHUNK, PROMPT_CHUNK, seq_len // PROMPT_CHUNK
        x_spec = pl.BlockSpec((n_seq, chunk, D_MODEL), lambda j: (0, j, 0))
        mod_spec = pl.BlockSpec((n_seq, 3 * D_MODEL), lambda j: (0, 0))
        tab_spec = pl.BlockSpec((chunk, HEAD_DIM), lambda j: (j, 0))
        seq_map = lambda j: 0
    else:
        chunk = pos.shape[0]
        tile_rows = SUBLANES
        n_total = x.shape[0] // chunk
        n_steps = n_total // n_seq
        x_spec = pl.BlockSpec((n_seq * chunk, D_MODEL), lambda j: (j, 0))
        mod_spec = pl.BlockSpec((n_seq, 3 * D_MODEL), lambda j: (j, 0))
        tab_spec = _const_spec((tile_rows, HEAD_DIM))
        seq_map = lambda j: j
    rows = n_seq * chunk
    cos, sin = _rope_tables(pos)
    if not prompt:
        reps = tile_rows // chunk
        cos, sin = np.tile(cos, (reps, 1)), np.tile(sin, (reps, 1))
    dmask, cdec, sdec = _decay_tables(chunk, tile_rows)
    chunk_decay = tuple(float(math.exp(chunk * math.log1p(-2.0 ** (-5.0 - h)))) for h in range(RET_HEADS))
    sret0, sre0, sim0 = states

    st_ret_spec = pl.BlockSpec((n_seq, RET_HEADS, HEAD_DIM, HEAD_DIM), lambda j: (seq_map(j), 0, 0, 0))
    st_s5_spec = pl.BlockSpec((n_seq, SSM_CH), lambda j: (seq_map(j), 0))
    consts = [dmask, cdec, sdec, wts["rnw"], wts["bmat"], wts["cre"], wts["cim"], wts["lbr"], wts["lbi"],
              wts["dsk"], wts["w_glu"], wts["b_glu"], wts["snw"], wts["w_out"]]
    args = [x, mod, wts["n1w"], wts["w_in"], cos, sin] + consts + [sret0, sre0, sim0]
    in_specs = ([x_spec, mod_spec, _const_spec(wts["n1w"].shape), _const_spec(wts["w_in"].shape), tab_spec, tab_spec]
                + [_const_spec(a.shape) for a in consts] + [st_ret_spec, st_s5_spec, st_s5_spec])

    kern = functools.partial(_mixer_kernel, n_seq=n_seq, chunk=chunk, tile_rows=tile_rows, carry=prompt,
                             chunk_decay=chunk_decay)
    out_shape = (
        jax.ShapeDtypeStruct(x.shape, F32),
        jax.ShapeDtypeStruct((n_total, RET_HEADS, HEAD_DIM, HEAD_DIM), F32),
        jax.ShapeDtypeStruct((n_total, SSM_CH), F32),
        jax.ShapeDtypeStruct((n_total, SSM_CH), F32),
    )
    scratch = [
        pltpu.VMEM((rows, D_MODEL), BF16),
        pltpu.VMEM((rows, 4 * RET_WIDTH), F32),
        pltpu.VMEM((SSM_WIDTH // LANES, n_seq * _seq_pitch(chunk), LANES), F32),
        pltpu.VMEM((rows, D_MODEL), F32),
        pltpu.VMEM((rows, SSM_WIDTH), F32),
        pltpu.VMEM((rows, SSM_CH // 2), F32),
        pltpu.VMEM((rows, SSM_CH // 2), F32),
        pltpu.VMEM((rows, SSM_CH // 2), F32),
        pltpu.VMEM((rows, SSM_CH // 2), F32),
        pltpu.VMEM((rows, SSM_WIDTH), F32),
        pltpu.VMEM((SSM_WIDTH // LANES, n_seq * _seq_pitch(chunk), LANES), F32),
        pltpu.VMEM((3 * D_MODEL // LANES, SUBLANES if prompt else rows, LANES), F32),
    ]
    return pl.pallas_call(
        kern,
        grid=(n_steps,),
        in_specs=in_specs,
        out_specs=(x_spec, st_ret_spec, st_s5_spec, st_s5_spec),
        out_shape=out_shape,
        scratch_shapes=scratch,
        compiler_params=pltpu.CompilerParams(dimension_semantics=("arbitrary",), vmem_limit_bytes=VMEM_LIMIT),
        name="mixer_prompt" if prompt else "mixer_sample",
    )(*args)


PACK_ROWS = D_MODEL // (2 * LANES)


def _store_packed(ref, x):
    half = D_MODEL // 2
    bits = lax.bitcast_convert_type(x.astype(BF16).astype(F32), jnp.uint32)
    words = bits[:, :half] | (bits[:, half:] >> 16)
    for c in range(PACK_ROWS):
        ref[c] = words[:, c * LANES:(c + 1) * LANES]


def _load_packed(ref, n, first_row=0, row_stride=1):
    hi, lo = [], []
    for c in range(PACK_ROWS):
        w = ref[c] if row_stride == 1 else ref[c, pl.ds(first_row, n, stride=row_stride), :]
        hi.append(lax.bitcast_convert_type(w & jnp.uint32(0xFFFF0000), F32))
        lo.append(lax.bitcast_convert_type(w << 16, F32))
    return jnp.concatenate(hi + lo, axis=1)


def _split_bf16(x):
    hi = x.astype(BF16)
    return hi, (x - hi.astype(F32)).astype(BF16)


def _route_tile(x, sh, sc, n2w_ref, wrh_ref, wrm_ref, br_ref, ltri_ref, count_ref, h2_ref, route_ref, topw_ref):
    ms = jnp.mean(x * x, axis=-1, keepdims=True)
    h2 = x * lax.rsqrt(ms + NORM_EPS) * n2w_ref[...] * (1.0 + sc) + sh
    _store_packed(h2_ref, h2)
    hh, hm = _split_bf16(h2)
    logits = (jnp.dot(hh, wrh_ref[...], preferred_element_type=F32)
              + (jnp.dot(hh, wrm_ref[...], preferred_element_type=F32)
                 + jnp.dot(hm, wrh_ref[...], preferred_element_type=F32))) + br_ref[...]
    lane = lax.broadcasted_iota(jnp.int32, logits.shape, 1)
    work = logits
    vals, idxs = [], []
    for _ in range(TOP_K):
        m = jnp.max(work, axis=-1, keepdims=True)
        idx = jnp.min(jnp.where(work == m, lane, LANES), axis=-1, keepdims=True)
        vals.append(m)
        idxs.append(idx)
        work = jnp.where(lane == idx, -jnp.inf, work)
    exps = [jnp.exp(v - vals[0]) for v in vals]
    tot = exps[0] + exps[1] + exps[2] + exps[3]
    topw = jnp.zeros(logits.shape, F32)
    for k in range(TOP_K):
        topw = jnp.where(lane == k, exps[k] / tot, topw)
    topw_ref[...] = topw

    onehot = [(lane == idxs[k]).astype(F32) for k in range(TOP_K)]
    chosen = onehot[0] + onehot[1] + onehot[2] + onehot[3]
    before = jnp.dot(ltri_ref[...], chosen.astype(BF16), preferred_element_type=F32) + count_ref[...]
    info = jnp.zeros(logits.shape, jnp.int32)
    for k in range(TOP_K):
        rank = jnp.sum(onehot[k] * before, axis=-1, keepdims=True).astype(jnp.int32)
        info = jnp.where(lane == k, idxs[k], info)
        info = jnp.where(lane == TOP_K + k, rank, info)
    route_ref[...] = jnp.transpose(info)[:2 * TOP_K, :]
    count_ref[...] = count_ref[...] + jnp.sum(chosen, axis=0, keepdims=True)


def _router_kernel(xp_ref, shp_ref, scp_ref, xs_ref, shs_ref, scs_ref, n2w_ref, wrh_ref, wrm_ref, br_ref, ltri_ref,
                   h2_ref, route_ref, topw_ref, count_ref, modx_ref, *, n_prompt_tiles, sample_len):
    i = pl.program_id(0)
    rest = (n2w_ref, wrh_ref, wrm_ref, br_ref, ltri_ref, count_ref, h2_ref, route_ref, topw_ref)

    @pl.when(i == 0)
    def _():
        count_ref[...] = jnp.zeros(count_ref.shape, F32)

    @pl.when(i < n_prompt_tiles)
    def _():
        _route_tile(xp_ref[...], shp_ref[0], scp_ref[0], *rest)

    @pl.when(i >= n_prompt_tiles)
    def _():
        dl = D_MODEL // LANES
        _expand_rows(modx_ref, shs_ref[...], sample_len)
        _expand_rows(modx_ref, scs_ref[...], sample_len, first_chunk=dl)
        _route_tile(xs_ref[...], _expanded(modx_ref, 0), _expanded(modx_ref, dl), *rest)


def _router(xp_rows, mod_p, seq_len, xs_rows, mod_s, sample_len, n2w, w_router, b_router):
    tile = ROUTER_TILE
    n_p, n_s = xp_rows.shape[0], xs_rows.shape[0]
    tp, ts = n_p // tile, n_s // tile
    n_total = n_p + n_s
    mod_p = mod_p.reshape(mod_p.shape[0], 1, mod_p.shape[1])
    seq_of = lambda i: (jnp.minimum(i, tp - 1) * tile) // seq_len
    wr_pad = jnp.pad(w_router, ((0, 0), (0, LANES - N_EXPERTS)))
    wr_hi = wr_pad.astype(BF16)
    wr_mid = (wr_pad - wr_hi.astype(F32)).astype(BF16)
    br_pad = jnp.pad(b_router, ((0, 0), (0, LANES - N_EXPERTS)), constant_values=-1e30)
    ltri = jnp.asarray(np.tril(np.ones((tile, tile), np.float32), -1), BF16)
    clamp_p = lambda i: jnp.minimum(i, tp - 1)
    clamp_s = lambda i: jnp.maximum(i - tp, 0)
    return pl.pallas_call(
        functools.partial(_router_kernel, n_prompt_tiles=tp, sample_len=sample_len),
        grid=(tp + ts,),
        in_specs=[pl.BlockSpec((tile, D_MODEL), lambda i: (clamp_p(i), 0)),
                  pl.BlockSpec((1, 1, D_MODEL), lambda i: (seq_of(i), 0, 3)),
                  pl.BlockSpec((1, 1, D_MODEL), lambda i: (seq_of(i), 0, 4)),
                  pl.BlockSpec((tile, D_MODEL), lambda i: (clamp_s(i), 0)),
                  pl.BlockSpec((tile // sample_len, D_MODEL), lambda i: (clamp_s(i), 3)),
                  pl.BlockSpec((tile // sample_len, D_MODEL), lambda i: (clamp_s(i), 4)),
                  _const_spec(n2w.shape), _const_spec(wr_hi.shape), _const_spec(wr_mid.shape),
                  _const_spec(br_pad.shape), _const_spec(ltri.shape)],
        out_specs=(pl.BlockSpec((PACK_ROWS, tile, LANES), lambda i: (0, i, 0)),
                   pl.BlockSpec((2 * TOP_K, tile), lambda i: (0, i)),
                   pl.BlockSpec((tile, LANES), lambda i: (i, 0)),
                   pl.BlockSpec((1, LANES), lambda i: (0, 0))),
        out_shape=(jax.ShapeDtypeStruct((PACK_ROWS, n_total, LANES), jnp.uint32),
                   jax.ShapeDtypeStruct((2 * TOP_K, n_total), jnp.int32),
                   jax.ShapeDtypeStruct((n_total, LANES), F32),
                   jax.ShapeDtypeStruct((1, LANES), F32)),
        scratch_shapes=[pltpu.VMEM((2 * D_MODEL // LANES, tile, LANES), F32)],
        compiler_params=pltpu.CompilerParams(dimension_semantics=("arbitrary",), vmem_limit_bytes=VMEM_LIMIT),
        name="router",
    )(xp_rows, mod_p, mod_p, xs_rows, mod_s, mod_s, n2w, wr_hi, wr_mid, br_pad, ltri)


def _gather_rows(table, idx):
    n = idx.shape[0]
    steps = n // SC_GATHER_WINDOW
    assert n % SC_GATHER_WINDOW == 0 and steps % SC_WORKERS == 0
    mesh = plsc.VectorSubcoreMesh(core_axis_name="c", subcore_axis_name="s")

    @functools.partial(pl.kernel, out_type=jax.ShapeDtypeStruct((n, table.shape[1]), table.dtype), mesh=mesh,
                       scratch_types=[])
    def gather_kernel(table_hbm, idx_hbm, out_hbm):
        def body(idx_vmem, out_vmem):
            pltpu.sync_copy(table_hbm.at[idx_vmem.at[0]], out_vmem)

        pltpu.emit_pipeline(
            body,
            grid=(steps,),
            in_specs=[pl.BlockSpec((1, SC_GATHER_WINDOW), lambda i: (0, i))],
            out_specs=[pl.BlockSpec((SC_GATHER_WINDOW, table.shape[1]), lambda i: (i, 0))],
            core_axis_name=("c", "s"),
            dimension_semantics=(pltpu.PARALLEL,),
        )(idx_hbm, out_hbm)

    return gather_kernel(table, idx.reshape(1, n))


def _dispatch_packed(table, slot_kt, n_slots):
    planes, n_tok, lanes = table.shape
    win = SC_GATHER_WINDOW
    blocks = n_tok // win
    assert n_tok % win == 0
    blocks_pad = _round_up(blocks, SC_WORKERS // math.gcd(SC_WORKERS, planes))
    n_spare = (blocks_pad - blocks) * TOP_K * win
    n_ext = n_slots + n_spare
    dest = jnp.transpose(slot_kt.reshape(TOP_K, blocks, win), (1, 0, 2))
    spare = n_slots + jnp.arange(n_spare, dtype=jnp.int32).reshape(blocks_pad - blocks, TOP_K, win)
    dest = jnp.concatenate([dest, spare], axis=0)[None] + (jnp.arange(planes, dtype=jnp.int32) * n_ext)[:, None, None, None]
    dest = dest.reshape(planes * blocks_pad * TOP_K, win)
    steps = planes * blocks_pad
    src_block = lambda g: (g // blocks_pad) * blocks + jnp.minimum(g % blocks_pad, blocks - 1)
    mesh = plsc.VectorSubcoreMesh(core_axis_name="c", subcore_axis_name="s")

    @functools.partial(pl.kernel, out_type=jax.ShapeDtypeStruct((planes * n_ext, lanes), table.dtype), mesh=mesh,
                       scratch_types=[])
    def scatter_kernel(table_hbm, dest_hbm, out_hbm):
        def body(rows_vmem, dest_vmem):
            for k in range(TOP_K):
                pltpu.sync_copy(rows_vmem, out_hbm.at[dest_vmem.at[k]])

        pltpu.emit_pipeline(
            body,
            grid=(steps,),
            in_specs=[pl.BlockSpec((win, lanes), lambda g: (src_block(g), 0)),
                      pl.BlockSpec((TOP_K, win), lambda g: (g, 0))],
            out_specs=[],
            core_axis_name=("c", "s"),
            dimension_semantics=(pltpu.PARALLEL,),
        )(table_hbm, dest_hbm)

    return scatter_kernel(table.reshape(planes * n_tok, lanes), dest).reshape(planes, n_ext, lanes)


def _gather_packed(table, rows):
    planes, n_table, lanes = table.shape
    idx = jnp.concatenate([rows + c * n_table for c in range(planes)])
    out = _gather_rows(table.reshape(planes * n_table, lanes), idx)
    return out.reshape(planes, rows.shape[0], lanes)


def _expert_weight_copies(e, w1_hbm, w2_hbm, w1s_ref, w2s_ref, sem):
    return (pltpu.make_async_copy(w1_hbm.at[e], w1s_ref, sem.at[0]),
            pltpu.make_async_copy(w2_hbm.at[e], w2s_ref, sem.at[1]))


def _experts_kernel(te_ref, tr_ref, nx_ref, nv_ref, xs_ref, w1_hbm, b1_ref, w2_hbm, b2_ref, ys_ref,
                    w1s_ref, w2s_ref, w1b_ref, w2b_ref, sem):
    i = pl.program_id(0)
    e = te_ref[i]
    new_expert = (i == 0) | (e != te_ref[jnp.maximum(i - 1, 0)])
    copies = functools.partial(_expert_weight_copies, w1_hbm=w1_hbm, w2_hbm=w2_hbm, w1s_ref=w1s_ref,
                               w2s_ref=w2s_ref, sem=sem)

    @pl.when(i == 0)
    def _():
        for c in copies(e):
            c.start()

    @pl.when(new_expert)
    def _():
        for c in copies(e):
            c.wait()
        w1b_ref[...] = w1s_ref[...].astype(BF16)
        w2b_ref[...] = w2s_ref[...].astype(BF16)

        @pl.when(nx_ref[i] >= 0)
        def _():
            for c in copies(nx_ref[i]):
                c.start()

    def expert_pass(r0, n_rows, rows_valid):
        xs_v, ys_v = xs_ref.at[:, pl.ds(r0, n_rows)], ys_ref.at[:, pl.ds(r0, n_rows)]
        row = lax.broadcasted_iota(jnp.int32, (n_rows, D_MODEL), 0)
        x = jnp.where(row < rows_valid, _load_packed(xs_v, n_rows), 0.0).astype(BF16)
        hu = jnp.dot(x, w1b_ref[...], preferred_element_type=F32) + b1_ref[0]
        x_glu = jnp.minimum(hu[:, :D_FF], SWIGLU_LIMIT)
        x_lin = jnp.clip(hu[:, D_FF:], -SWIGLU_LIMIT, SWIGLU_LIMIT)
        act = x_glu * jax.nn.sigmoid(SWIGLU_ALPHA * x_glu) * (x_lin + 1.0)
        _store_packed(ys_v, jnp.dot(act.astype(BF16), w2b_ref[...], preferred_element_type=F32) + b2_ref[0])

    def zero_rows(r0, n_rows):
        for c in range(PACK_ROWS):
            ys_ref[c, pl.ds(r0, n_rows), :] = jnp.zeros((n_rows, LANES), jnp.uint32)

    half = EXPERT_ROWS // 2
    rows_tile = jnp.where(i < nv_ref[0], tr_ref[i], 0)
    full_tile = rows_tile > SLOT_TILE - half

    @pl.when(full_tile)
    def _():
        expert_pass(0, SLOT_TILE, rows_tile)

    for h in range(SLOT_TILE // EXPERT_ROWS):
        r0 = h * EXPERT_ROWS
        rows_here = jnp.where(full_tile, -1, rows_tile - r0)

        @pl.when(rows_here > half)
        def _(r0=r0, rows_here=rows_here):
            expert_pass(r0, EXPERT_ROWS, rows_here)

        @pl.when((rows_here > 0) & (rows_here <= half))
        def _(r0=r0, rows_here=rows_here):
            expert_pass(r0, half, rows_here)
            zero_rows(r0 + half, half)

        @pl.when((rows_here <= 0) & jnp.logical_not(full_tile))
        def _(r0=r0):
            zero_rows(r0, EXPERT_ROWS)


def _experts(tile_expert, tile_rows, next_expert, n_valid, xs, w1, b1, w2, b2):
    n_tiles = tile_expert.shape[0]
    n_slots = n_tiles * SLOT_TILE
    grid_spec = pltpu.PrefetchScalarGridSpec(
        num_scalar_prefetch=4,
        grid=(n_tiles,),
        in_specs=[
            pl.BlockSpec((PACK_ROWS, SLOT_TILE, LANES), lambda i, te, tr, nx, nv: (0, i, 0)),
            pl.BlockSpec(memory_space=pl.ANY),
            pl.BlockSpec((1, 1, 2 * D_FF), lambda i, te, tr, nx, nv: (te[i], 0, 0)),
            pl.BlockSpec(memory_space=pl.ANY),
            pl.BlockSpec((1, 1, D_MODEL), lambda i, te, tr, nx, nv: (te[i], 0, 0)),
        ],
        out_specs=pl.BlockSpec((PACK_ROWS, SLOT_TILE, LANES), lambda i, te, tr, nx, nv: (0, i, 0)),
        scratch_shapes=[pltpu.VMEM((D_MODEL, 2 * D_FF), F32), pltpu.VMEM((D_FF, D_MODEL), F32),
                        pltpu.VMEM((D_MODEL, 2 * D_FF), BF16), pltpu.VMEM((D_FF, D_MODEL), BF16),
                        pltpu.SemaphoreType.DMA((2,))],
    )
    return pl.pallas_call(
        _experts_kernel,
        grid_spec=grid_spec,
        out_shape=jax.ShapeDtypeStruct((PACK_ROWS, n_slots, LANES), jnp.uint32),
        compiler_params=pltpu.CompilerParams(dimension_semantics=("arbitrary",), vmem_limit_bytes=VMEM_LIMIT),
        name="experts",
    )(tile_expert, tile_rows, next_expert, n_valid, xs, w1, b1.reshape(N_EXPERTS, 1, 2 * D_FF), w2,
      b2.reshape(N_EXPERTS, 1, D_MODEL))


def _combine_kernel(y4_ref, x1_ref, topw_ref, g2_ref, fw_ref, *rest, reps):
    o_ref = rest[-2] if reps else rest[-1]
    w = topw_ref[...]
    n = w.shape[0]
    ff = None
    for k in range(TOP_K):
        yk = w[:, k:k + 1] * _load_packed(y4_ref.at[:, k], n)
        ff = yk if ff is None else ff + yk
    if reps:
        _expand_rows(rest[-1], g2_ref[...], reps)
        g2 = _expanded(rest[-1], 0)
    else:
        g2 = g2_ref[0]
    x = x1_ref[...] + g2 * ff
    ms = jnp.mean(x * x, axis=-1, keepdims=True)
    o_ref[...] = x * lax.rsqrt(ms + NORM_EPS) * fw_ref[...]


def _combine(y4, y4_row0, x1, x1_row0, n_rows, topw, topw_row0, mod, rows_per_mod, fw, out_buf):
    tile = COMBINE_TILE
    y4_off, x1_off, tw_off = y4_row0 // tile, x1_row0 // tile, topw_row0 // tile
    reps = rows_per_mod if rows_per_mod < tile else 0
    scratch = []
    if reps:
        g2_spec = pl.BlockSpec((tile // reps, D_MODEL), lambda i: (i + x1_off, 5))
        scratch = [pltpu.VMEM((D_MODEL // LANES, tile, LANES), F32)]
    else:
        mod = mod.reshape(mod.shape[0], 1, mod.shape[1])
        g2_spec = pl.BlockSpec((1, 1, D_MODEL), lambda i: (((i + x1_off) * tile) // rows_per_mod, 0, 5))
    in_specs = [pl.BlockSpec((PACK_ROWS, TOP_K, tile, LANES), lambda i: (0, 0, i + y4_off, 0)),
                pl.BlockSpec((tile, D_MODEL), lambda i: (i + x1_off, 0)),
                pl.BlockSpec((tile, LANES), lambda i: (i + tw_off, 0)),
                g2_spec, _const_spec(fw.shape)]
    args = [y4, x1, topw, mod, fw]
    aliases = {}
    if out_buf is not None:
        in_specs.append(pl.BlockSpec(memory_space=pl.ANY))
        args.append(out_buf)
        aliases = {len(args) - 1: 0}
    return pl.pallas_call(
        functools.partial(_combine_kernel, reps=reps),
        grid=(n_rows // tile,),
        in_specs=in_specs,
        out_specs=pl.BlockSpec((tile, D_MODEL), lambda i: (i + x1_off, 0)),
        out_shape=jax.ShapeDtypeStruct(x1.shape, F32),
        input_output_aliases=aliases,
        scratch_shapes=scratch,
        compiler_params=pltpu.CompilerParams(dimension_semantics=("arbitrary",), vmem_limit_bytes=VMEM_LIMIT),
        name="combine",
    )(*args)


def _routing_tables(route, counts, n_slots):
    padded = ((counts + SLOT_TILE - 1) // SLOT_TILE) * SLOT_TILE
    pend = jnp.cumsum(padded)
    poff = pend - padded
    expert_kt, rank_kt = route[:TOP_K], route[TOP_K:]
    experts = jnp.arange(N_EXPERTS, dtype=jnp.int32)
    start_kt = jnp.sum((expert_kt[None] == experts[:, None, None]).astype(jnp.int32) * poff[:, None, None], axis=0)
    slot_kt = start_kt + rank_kt
    n_tiles = n_slots // SLOT_TILE
    n_valid = (pend[-1] // SLOT_TILE).astype(jnp.int32)
    tile_row = jnp.minimum(jnp.arange(n_tiles, dtype=jnp.int32), n_valid - 1) * SLOT_TILE
    in_later = (pend[None, :] <= tile_row[:, None]).astype(jnp.int32)
    tile_e = jnp.sum(in_later, axis=1).astype(jnp.int32)
    is_e = (experts[None, :] == tile_e[:, None]).astype(jnp.int32)
    used_end = jnp.sum(is_e * (poff + counts)[None, :], axis=1)
    tile_rows = jnp.clip(used_end - tile_row, 0, SLOT_TILE).astype(jnp.int32)
    later_used = (experts[None, :] > tile_e[:, None]) & (counts[None, :] > 0)
    next_e = jnp.min(jnp.where(later_used, experts[None, :], N_EXPERTS), axis=1)
    next_e = jnp.where(next_e < N_EXPERTS, next_e, -1).astype(jnp.int32)
    return slot_kt, tile_e, tile_rows, next_e, n_valid.reshape(1)


def _round_up(n, m):
    return ((n + m - 1) // m) * m


def kernel(x_prompt, x_sample, c_prompt, c_sample, state_ret, state_s5_re, state_s5_im, norm1_w, norm2_w, w_ada, b_ada, w_in, ret_norm_w, s5_lam_re, s5_lam_im, s5_log_dt, s5_b_re, s5_b_im, s5_c_re, s5_c_im, s5_d, w_glu, b_glu, s5_norm_w, w_out, w_router, b_router, w1, b1, w2, b2, final_w):
    bp, lp, _ = x_prompt.shape
    bs, ls, _ = x_sample.shape
    assert norm1_w.shape[0] == 1, "single-layer model"
    n_p, n_s = bp * lp, bs * ls
    n_tok = n_p + n_s

    mod = _ada(jnp.concatenate([c_prompt, c_sample], axis=0), w_ada[0], b_ada[0])
    mod_p, mod_s = mod[:bp], mod[bp:]

    lbr, lbi, bbr, bbi = _s5prep(s5_lam_re[0], s5_lam_im[0], s5_log_dt[0], s5_b_re[0], s5_b_im[0])
    bmat = jnp.concatenate([_block_diag(bbr), _block_diag(bbi)], axis=-1).astype(BF16)
    cre = _block_diag(jnp.transpose(s5_c_re[0], (0, 2, 1))).astype(BF16)
    cim = _block_diag(jnp.transpose(-s5_c_im[0], (0, 2, 1))).astype(BF16)
    wts = dict(
        n1w=norm1_w, w_in=w_in[0].astype(BF16), rnw=ret_norm_w, bmat=bmat, cre=cre, cim=cim,
        lbr=lbr.reshape(1, SSM_CH), lbi=lbi.reshape(1, SSM_CH), dsk=s5_d[0].reshape(1, SSM_WIDTH),
        w_glu=w_glu[0].astype(BF16), b_glu=b_glu, snw=s5_norm_w, w_out=w_out[0].astype(BF16),
    )

    zero_states = (jnp.zeros((bp, RET_HEADS, HEAD_DIM, HEAD_DIM), F32), jnp.zeros((bp, SSM_CH), F32),
                   jnp.zeros((bp, SSM_CH), F32))
    x1_p, ret_p, re_p, im_p = _mixer(x_prompt, mod_p, np.arange(lp, dtype=np.float32), zero_states, wts,
                                     prompt=True)
    sample_states = (state_ret[0], state_s5_re[0].reshape(bs, SSM_CH), state_s5_im[0].reshape(bs, SSM_CH))
    x1_s, ret_s, re_s, im_s = _mixer(x_sample.reshape(n_s, D_MODEL), mod_s,
                                     PAST_LEN + np.arange(ls, dtype=np.float32), sample_states, wts, prompt=False)

    x1_p_rows = x1_p.reshape(n_p, D_MODEL)
    h2, route, topw, counts = _router(x1_p_rows, mod_p, lp, x1_s, mod_s, ls, norm2_w, w_router[0], b_router)

    n_assign = n_tok * TOP_K
    gather_quantum = SC_GATHER_WINDOW * SC_WORKERS // PACK_ROWS
    assert n_assign % gather_quantum == 0
    n_slots = _round_up(_round_up(n_assign, SLOT_TILE) + N_EXPERTS * SLOT_TILE, gather_quantum)
    slot_kt, tile_e, tile_rows, next_e, n_valid = _routing_tables(route, counts[0, :N_EXPERTS].astype(jnp.int32),
                                                                  n_slots)
    xs = _dispatch_packed(h2, slot_kt, n_slots)
    ys = _experts(tile_e, tile_rows, next_e, n_valid, xs, w1[0], b1[0], w2[0], b2[0])
    fw = final_w.reshape(1, D_MODEL)
    y_p, y_s = None, None
    bounds = [r * (n_p // COMBINE_RANGES) for r in range(COMBINE_RANGES)] + [n_tok]
    for lo, hi in zip(bounds[:-1], bounds[1:]):
        y4 = _gather_packed(ys, slot_kt[:, lo:hi].reshape(-1)).reshape(PACK_ROWS, TOP_K, hi - lo, LANES)
        y_p = _combine(y4, 0, x1_p_rows, lo, min(hi, n_p) - lo, topw, lo, mod_p, lp, fw, y_p)
        if hi > n_p:
            y_s = _combine(y4, n_p - lo, x1_s, 0, n_s, topw, n_p, mod_s, ls, fw, None)

    g, p = SSM_GROUPS, SSM_STATE
    return (y_p.reshape(bp, lp, D_MODEL), y_s.reshape(bs, ls, D_MODEL),
            ret_p[None], re_p.reshape(1, bp, g, p), im_p.reshape(1, bp, g, p),
            ret_s[None], re_s.reshape(1, bs, g, p), im_s.reshape(1, bs, g, p))
```

```python
import functools
import math

import jax
import jax.numpy as jnp
import numpy as np
from jax import lax
from jax.experimental import pallas as pl
from jax.experimental.pallas import tpu as pltpu
from jax.experimental.pallas import tpu_sc as plsc

F32 = jnp.float32
BF16 = jnp.bfloat16

D_MODEL = 1024
PAST_LEN = 16384
RET_WIDTH = 512
RET_HEADS = 4
HEAD_DIM = 128
ROPE_BASE = 10000.0
SSM_WIDTH = 512
SSM_GROUP = 16
SSM_GROUPS = 32
SSM_STATE = 64
SSM_CH = SSM_GROUPS * SSM_STATE
N_EXPERTS = 32
TOP_K = 4
D_FF = 1024
SWIGLU_LIMIT = 7.0
SWIGLU_ALPHA = 1.702
NORM_EPS = 1e-6

LANES = 128
SUBLANES = 8
VMEM_LIMIT = 56 * 1024 * 1024

SEQ_PER_BLOCK = 8
SAMPLE_SEQ_PER_BLOCK = 16
SCAN_ELEMS = 8 * 1024
PROMPT_CHUNK = 64
S5_BLOCK_GROUPS = 8
N_S5_BLOCKS = SSM_GROUPS // S5_BLOCK_GROUPS
S5_BLOCK_IN = S5_BLOCK_GROUPS * SSM_GROUP
S5_BLOCK_CH = S5_BLOCK_GROUPS * SSM_STATE
ROUTER_TILE = 512
SLOT_TILE = 512
EXPERT_ROWS = 256
COMBINE_TILE = 256
COMBINE_RANGES = 2
SC_GATHER_WINDOW = 128
SC_WORKERS = 32


def _silu(x):
    return x * jax.nn.sigmoid(x)


def _expand_rows(dst_ref, src, reps, first_chunk=0):
    n = src.shape[0]
    for c in range(src.shape[1] // LANES):
        piece = src[:, c * LANES:(c + 1) * LANES]
        for t in range(reps):
            dst_ref[first_chunk + c, pl.ds(t, n, stride=reps), :] = piece


def _expanded(ref, first_chunk, n_chunks=D_MODEL // LANES):
    return jnp.concatenate([ref[c] for c in range(first_chunk, first_chunk + n_chunks)], axis=1)


def _ada_kernel(c_ref, w_ref, b_ref, o_ref):
    sh, sm = _split_bf16(_silu(c_ref[...]))
    wh, wm = _split_bf16(w_ref[...])
    o_ref[...] = (jnp.dot(sh, wh, preferred_element_type=F32)
                  + (jnp.dot(sh, wm, preferred_element_type=F32) + jnp.dot(sm, wh, preferred_element_type=F32))
                  + b_ref[...])


def _ada(c_all, w_ada, b_ada):
    n_rows, n_out = c_all.shape[0], w_ada.shape[1]
    tn = 1536
    return pl.pallas_call(
        _ada_kernel,
        grid=(n_out // tn,),
        in_specs=[
            pl.BlockSpec((n_rows, D_MODEL), lambda j: (0, 0)),
            pl.BlockSpec((D_MODEL, tn), lambda j: (0, j)),
            pl.BlockSpec((1, tn), lambda j: (0, j)),
        ],
        out_specs=pl.BlockSpec((n_rows, tn), lambda j: (0, j)),
        out_shape=jax.ShapeDtypeStruct((n_rows, n_out), F32),
        compiler_params=pltpu.CompilerParams(dimension_semantics=("arbitrary",), vmem_limit_bytes=VMEM_LIMIT),
        name="ada",
    )(c_all, w_ada, b_ada.reshape(1, n_out))


def _s5prep_kernel(lre_ref, lim_ref, ldt_ref, bre_ref, bim_ref, lbr_ref, lbi_ref, bbr_ref, bbi_ref):
    lam_re, lam_im = lre_ref[...], lim_ref[...]
    dt = jnp.exp(ldt_ref[...])
    mag = jnp.exp(lam_re * dt)
    ang = lam_im * dt
    lb_re, lb_im = mag * jnp.cos(ang), mag * jnp.sin(ang)
    den = lam_re * lam_re + lam_im * lam_im
    f_re = ((lb_re - 1.0) * lam_re + lb_im * lam_im) / den
    f_im = (lb_im * lam_re - (lb_re - 1.0) * lam_im) / den
    lbr_ref[...] = lb_re
    lbi_ref[...] = lb_im
    b_re, b_im = bre_ref[...], bim_ref[...]
    bbr_ref[...] = f_re[:, None, :] * b_re - f_im[:, None, :] * b_im
    bbi_ref[...] = f_re[:, None, :] * b_im + f_im[:, None, :] * b_re


def _s5prep(lam_re, lam_im, log_dt, b_re, b_im):
    g, p = lam_re.shape
    bt_re = jnp.transpose(b_re, (0, 2, 1))
    bt_im = jnp.transpose(b_im, (0, 2, 1))
    return pl.pallas_call(
        _s5prep_kernel,
        out_shape=(
            jax.ShapeDtypeStruct((g, p), F32), jax.ShapeDtypeStruct((g, p), F32),
            jax.ShapeDtypeStruct((g, SSM_GROUP, p), F32), jax.ShapeDtypeStruct((g, SSM_GROUP, p), F32),
        ),
        name="s5prep",
    )(lam_re, lam_im, log_dt.reshape(g, 1), bt_re, bt_im)


def _block_diag(blocks):
    _, r, c = blocks.shape
    b4 = blocks.reshape(N_S5_BLOCKS, S5_BLOCK_GROUPS, r, c)
    eye = jnp.eye(S5_BLOCK_GROUPS, dtype=blocks.dtype)
    out = b4[:, :, :, None, :] * eye[None, :, None, :, None]
    return out.reshape(N_S5_BLOCKS, S5_BLOCK_GROUPS * r, S5_BLOCK_GROUPS * c)


def _mixer_kernel(x_ref, mod_ref, n1w_ref, win_ref, cos_ref, sin_ref, dmask_ref, cdec_ref, sdec_ref,
                  rnw_ref, bmat_ref, cre_ref, cim_ref, lbr_ref, lbi_ref, dsk_ref, wglu_ref, bglu_ref,
                  snw_ref, wout_ref, sret0_ref, sre0_ref, sim0_ref,
                  x1_ref, sret_ref, sre_ref, sim_ref,
                  hb_ref, z_ref, zu_ref, oy_ref, utb_ref, bur0_ref, bur1_ref, bui0_ref, bui1_ref, ytb_ref, yb_ref, modx_ref,
                  *, n_seq, chunk, tile_rows, carry, chunk_decay):
    rows = n_seq * chunk
    seq_per_tile = tile_rows // chunk
    n_tiles = rows // tile_rows
    per_row_mod = chunk % SUBLANES != 0
    dl = D_MODEL // LANES

    def load_states():
        sret_ref[...] = sret0_ref[...]
        sre_ref[...] = sre0_ref[...]
        sim_ref[...] = sim0_ref[...]

    if carry:
        pl.when(pl.program_id(0) == 0)(load_states)
    else:
        load_states()

    if per_row_mod:
        _expand_rows(modx_ref, mod_ref[...], chunk)

    n1w = n1w_ref[...]
    mod_rows = rows if per_row_mod else chunk
    for i in range(rows // mod_rows):
        r0 = i * mod_rows
        xb = _load_rows(x_ref, r0, mod_rows, chunk)
        if per_row_mod:
            sh, sc = _expanded(modx_ref, 0), _expanded(modx_ref, dl)
        else:
            sh = mod_ref[pl.ds(i, 1), pl.ds(0, D_MODEL)]
            sc = mod_ref[pl.ds(i, 1), pl.ds(D_MODEL, D_MODEL)]
        ms = jnp.mean(xb * xb, axis=-1, keepdims=True)
        hn = xb * lax.rsqrt(ms + NORM_EPS) * n1w
        hb_ref[pl.ds(r0, mod_rows), :] = (hn * (1.0 + sc) + sh).astype(BF16)
    ret_w = 4 * RET_WIDTH
    z_ref[...] = jnp.dot(hb_ref[...], win_ref[:, pl.ds(0, ret_w)], preferred_element_type=F32)
    zu = jnp.dot(hb_ref[...], win_ref[:, pl.ds(ret_w, SSM_WIDTH)], preferred_element_type=F32)
    pitch = zu_ref.shape[1] // n_seq
    for c in range(SSM_WIDTH // LANES):
        for b in range(n_seq if pitch != chunk else 1):
            nb = chunk if pitch != chunk else rows
            zu_ref[c, pl.ds(b * pitch, nb), :] = zu[b * chunk:b * chunk + nb, c * LANES:(c + 1) * LANES]

    cos = cos_ref[...]
    sin = sin_ref[...]
    scale = HEAD_DIM ** -0.5
    if seq_per_tile > 1:
        row_id = lax.broadcasted_iota(jnp.int32, (tile_rows, HEAD_DIM), 0)

    def rope(t):
        return t * cos + pltpu.roll(t, HEAD_DIM // 2, 1) * sin

    def ret_tile(ti, c):
        r0 = pl.multiple_of(ti * tile_rows, tile_rows)
        for h in range(RET_HEADS):
            c0 = h * HEAD_DIM
            q = rope(z_ref[pl.ds(r0, tile_rows), pl.ds(c0, HEAD_DIM)])
            k = rope(z_ref[pl.ds(r0, tile_rows), pl.ds(RET_WIDTH + c0, HEAD_DIM)]) * scale
            v = z_ref[pl.ds(r0, tile_rows), pl.ds(2 * RET_WIDTH + c0, HEAD_DIM)]
            g = z_ref[pl.ds(r0, tile_rows), pl.ds(3 * RET_WIDTH + c0, HEAD_DIM)]
            kd = k * sdec_ref[h]
            if tile_rows < HEAD_DIM:
                pad = jnp.zeros((HEAD_DIM - tile_rows, HEAD_DIM), F32)
                k, v, kd = (jnp.concatenate([t, pad], axis=0) for t in (k, v, kd))
                if seq_per_tile > 1:
                    row_kv = lax.broadcasted_iota(jnp.int32, (HEAD_DIM, HEAD_DIM), 0)
            elif seq_per_tile > 1:
                row_kv = row_id
            qb, kb, vb = q.astype(BF16), k.astype(BF16), v.astype(BF16)
            s = lax.dot_general(qb, kb, (((1,), (1,)), ((), ())), preferred_element_type=F32) * dmask_ref[h]
            o = jnp.dot(s.astype(BF16), vb, preferred_element_type=F32)
            cross = None
            for si in range(seq_per_tile):
                sidx = ti * seq_per_tile + si
                st = sret_ref[sidx, h]
                cr = jnp.dot(qb, st.astype(BF16), preferred_element_type=F32)
                if seq_per_tile > 1:
                    in_seq = (row_id >= si * chunk) & (row_id < (si + 1) * chunk)
                    cross = jnp.where(in_seq, cr, 0.0 if cross is None else cross)
                    kds = jnp.where((row_kv >= si * chunk) & (row_kv < (si + 1) * chunk), kd, 0.0)
                else:
                    cross, kds = cr, kd
                upd = lax.dot_general(kds.astype(BF16), vb, (((0,), (0,)), ((), ())), preferred_element_type=F32)
                sret_ref[sidx, h] = st * chunk_decay[h] + upd
            o = o + cross * cdec_ref[h]
            o = o * lax.rsqrt(jnp.mean(o * o, axis=-1, keepdims=True) + NORM_EPS)
            o = o * rnw_ref[:, pl.ds(c0, HEAD_DIM)] * _silu(g)
            oy_ref[pl.ds(r0, tile_rows), pl.ds(c0, HEAD_DIM)] = o
        return c

    lax.fori_loop(0, n_tiles, ret_tile, 0, unroll=True)

    for t in range(chunk):
        for c in range(SSM_WIDTH // LANES):
            utb_ref[pl.ds(t * n_seq, n_seq), pl.ds(c * LANES, LANES)] = zu_ref[c, pl.ds(t, n_seq, stride=pitch), :]
    half_ch = SSM_CH // 2
    blk_per_half = N_S5_BLOCKS // 2
    bur_refs, bui_refs = (bur0_ref, bur1_ref), (bui0_ref, bui1_ref)
    for blk in range(N_S5_BLOCKS):
        hf, lcols = blk // blk_per_half, pl.ds((blk % blk_per_half) * S5_BLOCK_CH, S5_BLOCK_CH)
        ub = utb_ref[:, pl.ds(blk * S5_BLOCK_IN, S5_BLOCK_IN)].astype(BF16)
        bu = jnp.dot(ub, bmat_ref[blk], preferred_element_type=F32)
        bur_refs[hf][:, lcols] = bu[:, :S5_BLOCK_CH]
        bui_refs[hf][:, lcols] = bu[:, S5_BLOCK_CH:]

    scan_w = min(half_ch, SCAN_ELEMS // n_seq)
    for hf in range(2):
        bur_ref, bui_ref = bur_refs[hf], bui_refs[hf]
        for p in range(half_ch // scan_w):
            cols = pl.ds(p * scan_w, scan_w)
            gcols = pl.ds(hf * half_ch + p * scan_w, scan_w)
            lbr = jnp.broadcast_to(lbr_ref[:, gcols], (n_seq, scan_w))
            lbi = jnp.broadcast_to(lbi_ref[:, gcols], (n_seq, scan_w))
            hr, hi = sre_ref[:, gcols], sim_ref[:, gcols]
            for t in range(chunk):
                rws = pl.ds(t * n_seq, n_seq)
                hr, hi = (lbr * hr - lbi * hi + bur_ref[rws, cols], lbr * hi + lbi * hr + bui_ref[rws, cols])
                bur_ref[rws, cols] = hr
                bui_ref[rws, cols] = hi
            sre_ref[:, gcols] = hr
            sim_ref[:, gcols] = hi

    for blk in range(N_S5_BLOCKS):
        hf, lcols = blk // blk_per_half, pl.ds((blk % blk_per_half) * S5_BLOCK_CH, S5_BLOCK_CH)
        yb = jnp.dot(bur_refs[hf][:, lcols].astype(BF16), cre_ref[blk], preferred_element_type=F32)
        yb = yb + jnp.dot(bui_refs[hf][:, lcols].astype(BF16), cim_ref[blk], preferred_element_type=F32)
        ucols = pl.ds(blk * S5_BLOCK_IN, S5_BLOCK_IN)
        ytb_ref[:, ucols] = yb + dsk_ref[:, ucols] * utb_ref[:, ucols]
    for t in range(chunk):
        for c in range(SSM_WIDTH // LANES):
            yb_ref[c, pl.ds(t, n_seq, stride=pitch), :] = ytb_ref[pl.ds(t * n_seq, n_seq), pl.ds(c * LANES, LANES)]

    def seq_major(c):
        if pitch == chunk:
            return yb_ref[c]
        return jnp.concatenate([yb_ref[c, pl.ds(b * pitch, chunk), :] for b in range(n_seq)], axis=0)

    y = jnp.concatenate([seq_major(c) for c in range(SSM_WIDTH // LANES)], axis=1)
    y = jax.nn.gelu(y, approximate=True)
    gate = jnp.dot(y.astype(BF16), wglu_ref[...], preferred_element_type=F32) + bglu_ref[...]
    y = y * jax.nn.sigmoid(gate)
    y = y * lax.rsqrt(jnp.mean(y * y, axis=-1, keepdims=True) + NORM_EPS) * snw_ref[...]
    oy_ref[:, pl.ds(RET_WIDTH, SSM_WIDTH)] = y

    mix = jnp.dot(oy_ref[...].astype(BF16), wout_ref[...], preferred_element_type=F32)
    for i in range(rows // mod_rows):
        r0 = i * mod_rows
        if per_row_mod:
            g1 = _expanded(modx_ref, 2 * dl)
        else:
            g1 = mod_ref[pl.ds(i, 1), pl.ds(2 * D_MODEL, D_MODEL)]
        _store_rows(x1_ref, r0, mod_rows, chunk,
                    _load_rows(x_ref, r0, mod_rows, chunk) + g1 * mix[r0:r0 + mod_rows])


def _load_rows(ref, r0, n, chunk):
    if len(ref.shape) == 2:
        return ref[pl.ds(r0, n), :]
    assert n == chunk and r0 % chunk == 0
    return ref[r0 // chunk]


def _store_rows(ref, r0, n, chunk, val):
    if len(ref.shape) == 2:
        ref[pl.ds(r0, n), :] = val
    else:
        assert n == chunk and r0 % chunk == 0
        ref[r0 // chunk] = val


def _seq_pitch(chunk):
    return chunk + SUBLANES if chunk % SUBLANES == 0 else chunk


def _const_spec(shape):
    nd = len(shape)
    return pl.BlockSpec(shape, lambda j, _n=nd: (0,) * _n)


def _decay_tables(chunk, tile_rows):
    f32 = np.float32
    log_gamma = np.log1p(-np.exp2(f32(-5.0) - np.arange(RET_HEADS, dtype=f32))).astype(f32)
    r = np.arange(tile_rows)
    seq, loc = r // chunk, (r % chunk).astype(f32)
    rel = loc[:, None] - loc[None, :]
    ok = (seq[:, None] == seq[None, :]) & (rel >= 0)
    dmask = np.where(ok[None], np.exp(np.where(ok, rel, f32(0.0))[None] * log_gamma[:, None, None]), f32(0.0))
    if tile_rows < HEAD_DIM:
        dmask = np.pad(dmask, ((0, 0), (0, 0), (0, HEAD_DIM - tile_rows)))
    cdec = np.exp((loc[None, :] + f32(1.0)) * log_gamma[:, None])
    sdec = np.exp((f32(chunk) - f32(1.0) - loc)[None, :] * log_gamma[:, None])
    bcast = lambda t: np.ascontiguousarray(np.broadcast_to(t[:, :, None], (RET_HEADS, tile_rows, HEAD_DIM)))
    return dmask.astype(f32), bcast(cdec.astype(f32)), bcast(sdec.astype(f32))


def _rope_tables(pos):
    f32 = np.float32
    half = HEAD_DIM // 2
    inv_freq = (f32(ROPE_BASE) ** (-np.arange(half, dtype=f32) / f32(half))).astype(f32)
    ang = (pos.astype(f32)[:, None] * inv_freq[None, :]).astype(f32)
    cos, sin = np.cos(ang).astype(f32), np.sin(ang).astype(f32)
    return np.concatenate([cos, cos], axis=-1), np.concatenate([-sin, sin], axis=-1)


def _mixer(x, mod, pos, states, wts, *, prompt):
    n_seq = SEQ_PER_BLOCK if prompt else SAMPLE_SEQ_PER_BLOCK
    if prompt:
        n_total, seq_len, _ = x.shape
        assert n_total == n_seq
        chunk, tile_rows, n_steps = PROMPT_CHUNK, PROMPT_CHUNK, seq_len // PROMPT_CHUNK
        x_spec = pl.BlockSpec((n_seq, chunk, D_MODEL), lambda j: (0, j, 0))
        mod_spec = pl.BlockSpec((n_seq, 3 * D_MODEL), lambda j: (0, 0))
        tab_spec = pl.BlockSpec((chunk, HEAD_DIM), lambda j: (j, 0))
        seq_map = lambda j: 0
    else:
        chunk = pos.shape[0]
        tile_rows = SUBLANES
        n_total = x.shape[0] // chunk
        n_steps = n_total // n_seq
        x_spec = pl.BlockSpec((n_seq * chunk, D_MODEL), lambda j: (j, 0))
        mod_spec = pl.BlockSpec((n_seq, 3 * D_MODEL), lambda j: (j, 0))
        tab_spec = _const_spec((tile_rows, HEAD_DIM))
        seq_map = lambda j: j
    rows = n_seq * chunk
    cos, sin = _rope_tables(pos)
    if not prompt:
        reps = tile_rows // chunk
        cos, sin = np.tile(cos, (reps, 1)), np.tile(sin, (reps, 1))
    dmask, cdec, sdec = _decay_tables(chunk, tile_rows)
    chunk_decay = tuple(float(math.exp(chunk * math.log1p(-2.0 ** (-5.0 - h)))) for h in range(RET_HEADS))
    sret0, sre0, sim0 = states

    st_ret_spec = pl.BlockSpec((n_seq, RET_HEADS, HEAD_DIM, HEAD_DIM), lambda j: (seq_map(j), 0, 0, 0))
    st_s5_spec = pl.BlockSpec((n_seq, SSM_CH), lambda j: (seq_map(j), 0))
    consts = [dmask, cdec, sdec, wts["rnw"], wts["bmat"], wts["cre"], wts["cim"], wts["lbr"], wts["lbi"],
              wts["dsk"], wts["w_glu"], wts["b_glu"], wts["snw"], wts["w_out"]]
    args = [x, mod, wts["n1w"], wts["w_in"], cos, sin] + consts + [sret0, sre0, sim0]
    in_specs = ([x_spec, mod_spec, _const_spec(wts["n1w"].shape), _const_spec(wts["w_in"].shape), tab_spec, tab_spec]
                + [_const_spec(a.shape) for a in consts] + [st_ret_spec, st_s5_spec, st_s5_spec])

    kern = functools.partial(_mixer_kernel, n_seq=n_seq, chunk=chunk, tile_rows=tile_rows, carry=prompt,
                             chunk_decay=chunk_decay)
    out_shape = (
        jax.ShapeDtypeStruct(x.shape, F32),
        jax.ShapeDtypeStruct((n_total, RET_HEADS, HEAD_DIM, HEAD_DIM), F32),
        jax.ShapeDtypeStruct((n_total, SSM_CH), F32),
        jax.ShapeDtypeStruct((n_total, SSM_CH), F32),
    )
    scratch = [
        pltpu.VMEM((rows, D_MODEL), BF16),
        pltpu.VMEM((rows, 4 * RET_WIDTH), F32),
        pltpu.VMEM((SSM_WIDTH // LANES, n_seq * _seq_pitch(chunk), LANES), F32),
        pltpu.VMEM((rows, D_MODEL), F32),
        pltpu.VMEM((rows, SSM_WIDTH), F32),
        pltpu.VMEM((rows, SSM_CH // 2), F32),
        pltpu.VMEM((rows, SSM_CH // 2), F32),
        pltpu.VMEM((rows, SSM_CH // 2), F32),
        pltpu.VMEM((rows, SSM_CH // 2), F32),
        pltpu.VMEM((rows, SSM_WIDTH), F32),
        pltpu.VMEM((SSM_WIDTH // LANES, n_seq * _seq_pitch(chunk), LANES), F32),
        pltpu.VMEM((3 * D_MODEL // LANES, SUBLANES if prompt else rows, LANES), F32),
    ]
    return pl.pallas_call(
        kern,
        grid=(n_steps,),
        in_specs=in_specs,
        out_specs=(x_spec, st_ret_spec, st_s5_spec, st_s5_spec),
        out_shape=out_shape,
        scratch_shapes=scratch,
        compiler_params=pltpu.CompilerParams(dimension_semantics=("arbitrary",), vmem_limit_bytes=VMEM_LIMIT),
        name="mixer_prompt" if prompt else "mixer_sample",
    )(*args)


PACK_ROWS = D_MODEL // (2 * LANES)


def _store_packed(ref, x):
    half = D_MODEL // 2
    bits = lax.bitcast_convert_type(x.astype(BF16).astype(F32), jnp.uint32)
    words = bits[:, :half] | (bits[:, half:] >> 16)
    for c in range(PACK_ROWS):
        ref[c] = words[:, c * LANES:(c + 1) * LANES]


def _load_packed(ref, n, first_row=0, row_stride=1):
    hi, lo = [], []
    for c in range(PACK_ROWS):
        w = ref[c] if row_stride == 1 else ref[c, pl.ds(first_row, n, stride=row_stride), :]
        hi.append(lax.bitcast_convert_type(w & jnp.uint32(0xFFFF0000), F32))
        lo.append(lax.bitcast_convert_type(w << 16, F32))
    return jnp.concatenate(hi + lo, axis=1)


def _split_bf16(x):
    hi = x.astype(BF16)
    return hi, (x - hi.astype(F32)).astype(BF16)


def _route_tile(x, sh, sc, n2w_ref, wrh_ref, wrm_ref, br_ref, ltri_ref, count_ref, h2_ref, route_ref, topw_ref):
    ms = jnp.mean(x * x, axis=-1, keepdims=True)
    h2 = x * lax.rsqrt(ms + NORM_EPS) * n2w_ref[...] * (1.0 + sc) + sh
    _store_packed(h2_ref, h2)
    hh, hm = _split_bf16(h2)
    logits = (jnp.dot(hh, wrh_ref[...], preferred_element_type=F32)
              + (jnp.dot(hh, wrm_ref[...], preferred_element_type=F32)
                 + jnp.dot(hm, wrh_ref[...], preferred_element_type=F32))) + br_ref[...]
    lane = lax.broadcasted_iota(jnp.int32, logits.shape, 1)
    lane_f = lane.astype(F32)
    work = logits
    vals, idxs = [], []
    for _ in range(TOP_K):
        m = jnp.max(work, axis=-1, keepdims=True)
        idx = jnp.min(jnp.where(work == m, lane_f, float(LANES)), axis=-1, keepdims=True)
        vals.append(m)
        idxs.append(idx)
        work = jnp.where(lane_f == idx, -jnp.inf, work)
    exps = [jnp.exp(v - vals[0]) for v in vals]
    tot = exps[0] + exps[1] + exps[2] + exps[3]
    topw = jnp.zeros(logits.shape, F32)
    for k in range(TOP_K):
        topw = jnp.where(lane == k, exps[k] / tot, topw)
    topw_ref[...] = topw

    onehot = [(lane_f == idxs[k]).astype(F32) for k in range(TOP_K)]
    chosen = onehot[0] + onehot[1] + onehot[2] + onehot[3]
    before = jnp.dot(ltri_ref[...], chosen.astype(BF16), preferred_element_type=F32) + count_ref[...]
    info = jnp.zeros(logits.shape, jnp.int32)
    for k in range(TOP_K):
        rank = jnp.sum(onehot[k] * before, axis=-1, keepdims=True).astype(jnp.int32)
        info = jnp.where(lane == k, idxs[k].astype(jnp.int32), info)
        info = jnp.where(lane == TOP_K + k, rank, info)
    route_ref[...] = jnp.transpose(info)[:2 * TOP_K, :]
    count_ref[...] = count_ref[...] + jnp.sum(chosen, axis=0, keepdims=True)


def _router_kernel(xp_ref, shp_ref, scp_ref, xs_ref, shs_ref, scs_ref, n2w_ref, wrh_ref, wrm_ref, br_ref, ltri_ref,
                   h2_ref, route_ref, topw_ref, count_ref, modx_ref, *, n_prompt_tiles, sample_len):
    i = pl.program_id(0)
    rest = (n2w_ref, wrh_ref, wrm_ref, br_ref, ltri_ref, count_ref, h2_ref, route_ref, topw_ref)

    @pl.when(i == 0)
    def _():
        count_ref[...] = jnp.zeros(count_ref.shape, F32)

    @pl.when(i < n_prompt_tiles)
    def _():
        _route_tile(xp_ref[...], shp_ref[0], scp_ref[0], *rest)

    @pl.when(i >= n_prompt_tiles)
    def _():
        dl = D_MODEL // LANES
        _expand_rows(modx_ref, shs_ref[...], sample_len)
        _expand_rows(modx_ref, scs_ref[...], sample_len, first_chunk=dl)
        _route_tile(xs_ref[...], _expanded(modx_ref, 0), _expanded(modx_ref, dl), *rest)


def _router(xp_rows, mod_p, seq_len, xs_rows, mod_s, sample_len, n2w, w_router, b_router):
    tile = ROUTER_TILE
    n_p, n_s = xp_rows.shape[0], xs_rows.shape[0]
    tp, ts = n_p // tile, n_s // tile
    n_total = n_p + n_s
    mod_p = mod_p.reshape(mod_p.shape[0], 1, mod_p.shape[1])
    seq_of = lambda i: (jnp.minimum(i, tp - 1) * tile) // seq_len
    wr_pad = jnp.pad(w_router, ((0, 0), (0, LANES - N_EXPERTS)))
    wr_hi = wr_pad.astype(BF16)
    wr_mid = (wr_pad - wr_hi.astype(F32)).astype(BF16)
    br_pad = jnp.pad(b_router, ((0, 0), (0, LANES - N_EXPERTS)), constant_values=-1e30)
    ltri = jnp.asarray(np.tril(np.ones((tile, tile), np.float32), -1), BF16)
    clamp_p = lambda i: jnp.minimum(i, tp - 1)
    clamp_s = lambda i: jnp.maximum(i - tp, 0)
    return pl.pallas_call(
        functools.partial(_router_kernel, n_prompt_tiles=tp, sample_len=sample_len),
        grid=(tp + ts,),
        in_specs=[pl.BlockSpec((tile, D_MODEL), lambda i: (clamp_p(i), 0)),
                  pl.BlockSpec((1, 1, D_MODEL), lambda i: (seq_of(i), 0, 3)),
                  pl.BlockSpec((1, 1, D_MODEL), lambda i: (seq_of(i), 0, 4)),
                  pl.BlockSpec((tile, D_MODEL), lambda i: (clamp_s(i), 0)),
                  pl.BlockSpec((tile // sample_len, D_MODEL), lambda i: (clamp_s(i), 3)),
                  pl.BlockSpec((tile // sample_len, D_MODEL), lambda i: (clamp_s(i), 4)),
                  _const_spec(n2w.shape), _const_spec(wr_hi.shape), _const_spec(wr_mid.shape),
                  _const_spec(br_pad.shape), _const_spec(ltri.shape)],
        out_specs=(pl.BlockSpec((PACK_ROWS, tile, LANES), lambda i: (0, i, 0)),
                   pl.BlockSpec((2 * TOP_K, tile), lambda i: (0, i)),
                   pl.BlockSpec((tile, LANES), lambda i: (i, 0)),
                   pl.BlockSpec((1, LANES), lambda i: (0, 0))),
        out_shape=(jax.ShapeDtypeStruct((PACK_ROWS, n_total, LANES), jnp.uint32),
                   jax.ShapeDtypeStruct((2 * TOP_K, n_total), jnp.int32),
                   jax.ShapeDtypeStruct((n_total, LANES), F32),
                   jax.ShapeDtypeStruct((1, LANES), F32)),
        scratch_shapes=[pltpu.VMEM((2 * D_MODEL // LANES, tile, LANES), F32)],
        compiler_params=pltpu.CompilerParams(dimension_semantics=("arbitrary",), vmem_limit_bytes=VMEM_LIMIT),
        name="router",
    )(xp_rows, mod_p, mod_p, xs_rows, mod_s, mod_s, n2w, wr_hi, wr_mid, br_pad, ltri)


def _gather_rows(table, idx):
    n = idx.shape[0]
    steps = n // SC_GATHER_WINDOW
    assert n % SC_GATHER_WINDOW == 0 and steps % SC_WORKERS == 0
    mesh = plsc.VectorSubcoreMesh(core_axis_name="c", subcore_axis_name="s")

    @functools.partial(pl.kernel, out_type=jax.ShapeDtypeStruct((n, table.shape[1]), table.dtype), mesh=mesh,
                       scratch_types=[])
    def gather_kernel(table_hbm, idx_hbm, out_hbm):
        def body(idx_vmem, out_vmem):
            pltpu.sync_copy(table_hbm.at[idx_vmem.at[0]], out_vmem)

        pltpu.emit_pipeline(
            body,
            grid=(steps,),
            in_specs=[pl.BlockSpec((1, SC_GATHER_WINDOW), lambda i: (0, i))],
            out_specs=[pl.BlockSpec((SC_GATHER_WINDOW, table.shape[1]), lambda i: (i, 0))],
            core_axis_name=("c", "s"),
            dimension_semantics=(pltpu.PARALLEL,),
        )(idx_hbm, out_hbm)

    return gather_kernel(table, idx.reshape(1, n))


def _dispatch_packed(table, slot_kt, n_slots):
    planes, n_tok, lanes = table.shape
    win = SC_GATHER_WINDOW
    blocks = n_tok // win
    assert n_tok % win == 0
    blocks_pad = _round_up(blocks, SC_WORKERS // math.gcd(SC_WORKERS, planes))
    n_spare = (blocks_pad - blocks) * TOP_K * win
    n_ext = n_slots + n_spare
    dest = jnp.transpose(slot_kt.reshape(TOP_K, blocks, win), (1, 0, 2))
    spare = n_slots + jnp.arange(n_spare, dtype=jnp.int32).reshape(blocks_pad - blocks, TOP_K, win)
    dest = jnp.concatenate([dest, spare], axis=0)[None] + (jnp.arange(planes, dtype=jnp.int32) * n_ext)[:, None, None, None]
    dest = dest.reshape(planes * blocks_pad * TOP_K, win)
    steps = planes * blocks_pad
    src_block = lambda g: (g // blocks_pad) * blocks + jnp.minimum(g % blocks_pad, blocks - 1)
    mesh = plsc.VectorSubcoreMesh(core_axis_name="c", subcore_axis_name="s")

    @functools.partial(pl.kernel, out_type=jax.ShapeDtypeStruct((planes * n_ext, lanes), table.dtype), mesh=mesh,
                       scratch_types=[])
    def scatter_kernel(table_hbm, dest_hbm, out_hbm):
        def body(rows_vmem, dest_vmem):
            for k in range(TOP_K):
                pltpu.sync_copy(rows_vmem, out_hbm.at[dest_vmem.at[k]])

        pltpu.emit_pipeline(
            body,
            grid=(steps,),
            in_specs=[pl.BlockSpec((win, lanes), lambda g: (src_block(g), 0)),
                      pl.BlockSpec((TOP_K, win), lambda g: (g, 0))],
            out_specs=[],
            core_axis_name=("c", "s"),
            dimension_semantics=(pltpu.PARALLEL,),
        )(table_hbm, dest_hbm)

    return scatter_kernel(table.reshape(planes * n_tok, lanes), dest).reshape(planes, n_ext, lanes)


def _gather_packed(table, rows):
    planes, n_table, lanes = table.shape
    idx = jnp.concatenate([rows + c * n_table for c in range(planes)])
    out = _gather_rows(table.reshape(planes * n_table, lanes), idx)
    return out.reshape(planes, rows.shape[0], lanes)


def _expert_weight_copies(e, w1_hbm, w2_hbm, w1s_ref, w2s_ref, sem):
    return (pltpu.make_async_copy(w1_hbm.at[e], w1s_ref, sem.at[0]),
            pltpu.make_async_copy(w2_hbm.at[e], w2s_ref, sem.at[1]))


def _experts_kernel(te_ref, tr_ref, nx_ref, nv_ref, xs_ref, w1_hbm, b1_ref, w2_hbm, b2_ref, ys_ref,
                    w1s_ref, w2s_ref, w1b_ref, w2b_ref, sem):
    i = pl.program_id(0)
    e = te_ref[i]
    new_expert = (i == 0) | (e != te_ref[jnp.maximum(i - 1, 0)])
    copies = functools.partial(_expert_weight_copies, w1_hbm=w1_hbm, w2_hbm=w2_hbm, w1s_ref=w1s_ref,
                               w2s_ref=w2s_ref, sem=sem)

    @pl.when(i == 0)
    def _():
        for c in copies(e):
            c.start()

    @pl.when(new_expert)
    def _():
        for c in copies(e):
            c.wait()
        w1b_ref[...] = w1s_ref[...].astype(BF16)
        w2b_ref[...] = w2s_ref[...].astype(BF16)

        @pl.when(nx_ref[i] >= 0)
        def _():
            for c in copies(nx_ref[i]):
                c.start()

    def expert_pass(r0, n_rows, rows_valid):
        xs_v, ys_v = xs_ref.at[:, pl.ds(r0, n_rows)], ys_ref.at[:, pl.ds(r0, n_rows)]
        row = lax.broadcasted_iota(jnp.int32, (n_rows, D_MODEL), 0)
        x = jnp.where(row < rows_valid, _load_packed(xs_v, n_rows), 0.0).astype(BF16)
        hu = jnp.dot(x, w1b_ref[...], preferred_element_type=F32) + b1_ref[0]
        x_glu = jnp.minimum(hu[:, :D_FF], SWIGLU_LIMIT)
        x_lin = jnp.clip(hu[:, D_FF:], -SWIGLU_LIMIT, SWIGLU_LIMIT)
        act = x_glu * jax.nn.sigmoid(SWIGLU_ALPHA * x_glu) * (x_lin + 1.0)
        _store_packed(ys_v, jnp.dot(act.astype(BF16), w2b_ref[...], preferred_element_type=F32) + b2_ref[0])

    def zero_rows(r0, n_rows):
        for c in range(PACK_ROWS):
            ys_ref[c, pl.ds(r0, n_rows), :] = jnp.zeros((n_rows, LANES), jnp.uint32)

    half = EXPERT_ROWS // 2
    rows_tile = jnp.where(i < nv_ref[0], tr_ref[i], 0)
    full_tile = rows_tile > SLOT_TILE - half

    @pl.when(full_tile)
    def _():
        expert_pass(0, SLOT_TILE, rows_tile)

    for h in range(SLOT_TILE // EXPERT_ROWS):
        r0 = h * EXPERT_ROWS
        rows_here = jnp.where(full_tile, -1, rows_tile - r0)

        @pl.when(rows_here > half)
        def _(r0=r0, rows_here=rows_here):
            expert_pass(r0, EXPERT_ROWS, rows_here)

        @pl.when((rows_here > 0) & (rows_here <= half))
        def _(r0=r0, rows_here=rows_here):
            expert_pass(r0, half, rows_here)
            zero_rows(r0 + half, half)

        @pl.when((rows_here <= 0) & jnp.logical_not(full_tile))
        def _(r0=r0):
            zero_rows(r0, EXPERT_ROWS)


def _experts(tile_expert, tile_rows, next_expert, n_valid, xs, w1, b1, w2, b2):
    n_tiles = tile_expert.shape[0]
    n_slots = n_tiles * SLOT_TILE
    grid_spec = pltpu.PrefetchScalarGridSpec(
        num_scalar_prefetch=4,
        grid=(n_tiles,),
        in_specs=[
            pl.BlockSpec((PACK_ROWS, SLOT_TILE, LANES), lambda i, te, tr, nx, nv: (0, i, 0)),
            pl.BlockSpec(memory_space=pl.ANY),
            pl.BlockSpec((1, 1, 2 * D_FF), lambda i, te, tr, nx, nv: (te[i], 0, 0)),
            pl.BlockSpec(memory_space=pl.ANY),
            pl.BlockSpec((1, 1, D_MODEL), lambda i, te, tr, nx, nv: (te[i], 0, 0)),
        ],
        out_specs=pl.BlockSpec((PACK_ROWS, SLOT_TILE, LANES), lambda i, te, tr, nx, nv: (0, i, 0)),
        scratch_shapes=[pltpu.VMEM((D_MODEL, 2 * D_FF), F32), pltpu.VMEM((D_FF, D_MODEL), F32),
                        pltpu.VMEM((D_MODEL, 2 * D_FF), BF16), pltpu.VMEM((D_FF, D_MODEL), BF16),
                        pltpu.SemaphoreType.DMA((2,))],
    )
    return pl.pallas_call(
        _experts_kernel,
        grid_spec=grid_spec,
        out_shape=jax.ShapeDtypeStruct((PACK_ROWS, n_slots, LANES), jnp.uint32),
        compiler_params=pltpu.CompilerParams(dimension_semantics=("arbitrary",), vmem_limit_bytes=VMEM_LIMIT),
        name="experts",
    )(tile_expert, tile_rows, next_expert, n_valid, xs, w1, b1.reshape(N_EXPERTS, 1, 2 * D_FF), w2,
      b2.reshape(N_EXPERTS, 1, D_MODEL))


def _combine_kernel(y4_ref, x1_ref, topw_ref, g2_ref, fw_ref, *rest, reps):
    o_ref = rest[-2] if reps else rest[-1]
    w = topw_ref[...]
    n = w.shape[0]
    ff = None
    for k in range(TOP_K):
        yk = w[:, k:k + 1] * _load_packed(y4_ref.at[:, k], n)
        ff = yk if ff is None else ff + yk
    if reps:
        _expand_rows(rest[-1], g2_ref[...], reps)
        g2 = _expanded(rest[-1], 0)
    else:
        g2 = g2_ref[0]
    x = x1_ref[...] + g2 * ff
    ms = jnp.mean(x * x, axis=-1, keepdims=True)
    o_ref[...] = x * lax.rsqrt(ms + NORM_EPS) * fw_ref[...]


def _combine(y4, y4_row0, x1, x1_row0, n_rows, topw, topw_row0, mod, rows_per_mod, fw, out_buf):
    tile = COMBINE_TILE
    y4_off, x1_off, tw_off = y4_row0 // tile, x1_row0 // tile, topw_row0 // tile
    reps = rows_per_mod if rows_per_mod < tile else 0
    scratch = []
    if reps:
        g2_spec = pl.BlockSpec((tile // reps, D_MODEL), lambda i: (i + x1_off, 5))
        scratch = [pltpu.VMEM((D_MODEL // LANES, tile, LANES), F32)]
    else:
        mod = mod.reshape(mod.shape[0], 1, mod.shape[1])
        g2_spec = pl.BlockSpec((1, 1, D_MODEL), lambda i: (((i + x1_off) * tile) // rows_per_mod, 0, 5))
    in_specs = [pl.BlockSpec((PACK_ROWS, TOP_K, tile, LANES), lambda i: (0, 0, i + y4_off, 0)),
                pl.BlockSpec((tile, D_MODEL), lambda i: (i + x1_off, 0)),
                pl.BlockSpec((tile, LANES), lambda i: (i + tw_off, 0)),
                g2_spec, _const_spec(fw.shape)]
    args = [y4, x1, topw, mod, fw]
    aliases = {}
    if out_buf is not None:
        in_specs.append(pl.BlockSpec(memory_space=pl.ANY))
        args.append(out_buf)
        aliases = {len(args) - 1: 0}
    return pl.pallas_call(
        functools.partial(_combine_kernel, reps=reps),
        grid=(n_rows // tile,),
        in_specs=in_specs,
        out_specs=pl.BlockSpec((tile, D_MODEL), lambda i: (i + x1_off, 0)),
        out_shape=jax.ShapeDtypeStruct(x1.shape, F32),
        input_output_aliases=aliases,
        scratch_shapes=scratch,
        compiler_params=pltpu.CompilerParams(dimension_semantics=("arbitrary",), vmem_limit_bytes=VMEM_LIMIT),
        name="combine",
    )(*args)


def _routing_tables(route, counts, n_slots):
    padded = ((counts + SLOT_TILE - 1) // SLOT_TILE) * SLOT_TILE
    pend = jnp.cumsum(padded)
    poff = pend - padded
    expert_kt, rank_kt = route[:TOP_K], route[TOP_K:]
    experts = jnp.arange(N_EXPERTS, dtype=jnp.int32)
    start_kt = jnp.sum((expert_kt[None] == experts[:, None, None]).astype(jnp.int32) * poff[:, None, None], axis=0)
    slot_kt = start_kt + rank_kt
    n_tiles = n_slots // SLOT_TILE
    n_valid = (pend[-1] // SLOT_TILE).astype(jnp.int32)
    tile_row = jnp.minimum(jnp.arange(n_tiles, dtype=jnp.int32), n_valid - 1) * SLOT_TILE
    in_later = (pend[None, :] <= tile_row[:, None]).astype(jnp.int32)
    tile_e = jnp.sum(in_later, axis=1).astype(jnp.int32)
    is_e = (experts[None, :] == tile_e[:, None]).astype(jnp.int32)
    used_end = jnp.sum(is_e * (poff + counts)[None, :], axis=1)
    tile_rows = jnp.clip(used_end - tile_row, 0, SLOT_TILE).astype(jnp.int32)
    later_used = (experts[None, :] > tile_e[:, None]) & (counts[None, :] > 0)
    next_e = jnp.min(jnp.where(later_used, experts[None, :], N_EXPERTS), axis=1)
    next_e = jnp.where(next_e < N_EXPERTS, next_e, -1).astype(jnp.int32)
    return slot_kt, tile_e, tile_rows, next_e, n_valid.reshape(1)


def _round_up(n, m):
    return ((n + m - 1) // m) * m


def kernel(x_prompt, x_sample, c_prompt, c_sample, state_ret, state_s5_re, state_s5_im, norm1_w, norm2_w, w_ada, b_ada, w_in, ret_norm_w, s5_lam_re, s5_lam_im, s5_log_dt, s5_b_re, s5_b_im, s5_c_re, s5_c_im, s5_d, w_glu, b_glu, s5_norm_w, w_out, w_router, b_router, w1, b1, w2, b2, final_w):
    bp, lp, _ = x_prompt.shape
    bs, ls, _ = x_sample.shape
    assert norm1_w.shape[0] == 1, "single-layer model"
    n_p, n_s = bp * lp, bs * ls
    n_tok = n_p + n_s

    mod = _ada(jnp.concatenate([c_prompt, c_sample], axis=0), w_ada[0], b_ada[0])
    mod_p, mod_s = mod[:bp], mod[bp:]

    lbr, lbi, bbr, bbi = _s5prep(s5_lam_re[0], s5_lam_im[0], s5_log_dt[0], s5_b_re[0], s5_b_im[0])
    bmat = jnp.concatenate([_block_diag(bbr), _block_diag(bbi)], axis=-1).astype(BF16)
    cre = _block_diag(jnp.transpose(s5_c_re[0], (0, 2, 1))).astype(BF16)
    cim = _block_diag(jnp.transpose(-s5_c_im[0], (0, 2, 1))).astype(BF16)
    wts = dict(
        n1w=norm1_w, w_in=w_in[0].astype(BF16), rnw=ret_norm_w, bmat=bmat, cre=cre, cim=cim,
        lbr=lbr.reshape(1, SSM_CH), lbi=lbi.reshape(1, SSM_CH), dsk=s5_d[0].reshape(1, SSM_WIDTH),
        w_glu=w_glu[0].astype(BF16), b_glu=b_glu, snw=s5_norm_w, w_out=w_out[0].astype(BF16),
    )

    zero_states = (jnp.zeros((bp, RET_HEADS, HEAD_DIM, HEAD_DIM), F32), jnp.zeros((bp, SSM_CH), F32),
                   jnp.zeros((bp, SSM_CH), F32))
    x1_p, ret_p, re_p, im_p = _mixer(x_prompt, mod_p, np.arange(lp, dtype=np.float32), zero_states, wts,
                                     prompt=True)
    sample_states = (state_ret[0], state_s5_re[0].reshape(bs, SSM_CH), state_s5_im[0].reshape(bs, SSM_CH))
    x1_s, ret_s, re_s, im_s = _mixer(x_sample.reshape(n_s, D_MODEL), mod_s,
                                     PAST_LEN + np.arange(ls, dtype=np.float32), sample_states, wts, prompt=False)

    x1_p_rows = x1_p.reshape(n_p, D_MODEL)
    h2, route, topw, counts = _router(x1_p_rows, mod_p, lp, x1_s, mod_s, ls, norm2_w, w_router[0], b_router)

    n_assign = n_tok * TOP_K
    gather_quantum = SC_GATHER_WINDOW * SC_WORKERS // PACK_ROWS
    assert n_assign % gather_quantum == 0
    n_slots = _round_up(_round_up(n_assign, SLOT_TILE) + N_EXPERTS * SLOT_TILE, gather_quantum)
    slot_kt, tile_e, tile_rows, next_e, n_valid = _routing_tables(route, counts[0, :N_EXPERTS].astype(jnp.int32),
                                                                  n_slots)
    xs = _dispatch_packed(h2, slot_kt, n_slots)
    ys = _experts(tile_e, tile_rows, next_e, n_valid, xs, w1[0], b1[0], w2[0], b2[0])
    fw = final_w.reshape(1, D_MODEL)
    y_p, y_s = None, None
    bounds = [r * (n_p // COMBINE_RANGES) for r in range(COMBINE_RANGES)] + [n_tok]
    for lo, hi in zip(bounds[:-1], bounds[1:]):
        y4 = _gather_packed(ys, slot_kt[:, lo:hi].reshape(-1)).reshape(PACK_ROWS, TOP_K, hi - lo, LANES)
        y_p = _combine(y4, 0, x1_p_rows, lo, min(hi, n_p) - lo, topw, lo, mod_p, lp, fw, y_p)
        if hi > n_p:
            y_s = _combine(y4, n_p - lo, x1_s, 0, n_s, topw, n_p, mod_s, ls, fw, None)

    g, p = SSM_GROUPS, SSM_STATE
    return (y_p.reshape(bp, lp, D_MODEL), y_s.reshape(bs, ls, D_MODEL),
            ret_p[None], re_p.reshape(1, bp, g, p), im_p.reshape(1, bp, g, p),
            ret_s[None], re_s.reshape(1, bs, g, p), im_s.reshape(1, bs, g, p))
```

```python
import functools
import math

import jax
import jax.numpy as jnp
import numpy as np
from jax import lax
from jax.experimental import pallas as pl
from jax.experimental.pallas import tpu as pltpu
from jax.experimental.pallas import tpu_sc as plsc

F32 = jnp.float32
BF16 = jnp.bfloat16

D_MODEL = 1024
PAST_LEN = 16384
RET_WIDTH = 512
RET_HEADS = 4
HEAD_DIM = 128
ROPE_BASE = 10000.0
SSM_WIDTH = 512
SSM_GROUP = 16
SSM_GROUPS = 32
SSM_STATE = 64
SSM_CH = SSM_GROUPS * SSM_STATE
N_EXPERTS = 32
TOP_K = 4
D_FF = 1024
SWIGLU_LIMIT = 7.0
SWIGLU_ALPHA = 1.702
NORM_EPS = 1e-6

LANES = 128
SUBLANES = 8
VMEM_LIMIT = 56 * 1024 * 1024

SEQ_PER_BLOCK = 8
SAMPLE_SEQ_PER_BLOCK = 16
SCAN_ELEMS = 8 * 1024
PROMPT_CHUNK = 64
S5_BLOCK_GROUPS = 8
N_S5_BLOCKS = SSM_GROUPS // S5_BLOCK_GROUPS
S5_BLOCK_IN = S5_BLOCK_GROUPS * SSM_GROUP
S5_BLOCK_CH = S5_BLOCK_GROUPS * SSM_STATE
ROUTER_TILE = 512
SLOT_TILE = 512
EXPERT_ROWS = 256
COMBINE_TILE = 512
COMBINE_RANGES = 2
SC_GATHER_WINDOW = 128
SC_WORKERS = 32


def _silu(x):
    return x * jax.nn.sigmoid(x)


def _expand_rows(dst_ref, src, reps, first_chunk=0):
    n = src.shape[0]
    for c in range(src.shape[1] // LANES):
        piece = src[:, c * LANES:(c + 1) * LANES]
        for t in range(reps):
            dst_ref[first_chunk + c, pl.ds(t, n, stride=reps), :] = piece


def _expanded(ref, first_chunk, n_chunks=D_MODEL // LANES):
    return jnp.concatenate([ref[c] for c in range(first_chunk, first_chunk + n_chunks)], axis=1)


def _ada_kernel(c_ref, w_ref, b_ref, o_ref):
    sh, sm = _split_bf16(_silu(c_ref[...]))
    wh, wm = _split_bf16(w_ref[...])
    o_ref[...] = (jnp.dot(sh, wh, preferred_element_type=F32)
                  + (jnp.dot(sh, wm, preferred_element_type=F32) + jnp.dot(sm, wh, preferred_element_type=F32))
                  + b_ref[...])


def _ada(c_all, w_ada, b_ada):
    n_rows, n_out = c_all.shape[0], w_ada.shape[1]
    tn = 1536
    return pl.pallas_call(
        _ada_kernel,
        grid=(n_out // tn,),
        in_specs=[
            pl.BlockSpec((n_rows, D_MODEL), lambda j: (0, 0)),
            pl.BlockSpec((D_MODEL, tn), lambda j: (0, j)),
            pl.BlockSpec((1, tn), lambda j: (0, j)),
        ],
        out_specs=pl.BlockSpec((n_rows, tn), lambda j: (0, j)),
        out_shape=jax.ShapeDtypeStruct((n_rows, n_out), F32),
        compiler_params=pltpu.CompilerParams(dimension_semantics=("arbitrary",), vmem_limit_bytes=VMEM_LIMIT),
        name="ada",
    )(c_all, w_ada, b_ada.reshape(1, n_out))


def _s5prep_kernel(lre_ref, lim_ref, ldt_ref, bre_ref, bim_ref, lbr_ref, lbi_ref, bbr_ref, bbi_ref):
    lam_re, lam_im = lre_ref[...], lim_ref[...]
    dt = jnp.exp(ldt_ref[...])
    mag = jnp.exp(lam_re * dt)
    ang = lam_im * dt
    lb_re, lb_im = mag * jnp.cos(ang), mag * jnp.sin(ang)
    den = lam_re * lam_re + lam_im * lam_im
    f_re = ((lb_re - 1.0) * lam_re + lb_im * lam_im) / den
    f_im = (lb_im * lam_re - (lb_re - 1.0) * lam_im) / den
    lbr_ref[...] = lb_re
    lbi_ref[...] = lb_im
    b_re, b_im = bre_ref[...], bim_ref[...]
    bbr_ref[...] = f_re[:, None, :] * b_re - f_im[:, None, :] * b_im
    bbi_ref[...] = f_re[:, None, :] * b_im + f_im[:, None, :] * b_re


def _s5prep(lam_re, lam_im, log_dt, b_re, b_im):
    g, p = lam_re.shape
    bt_re = jnp.transpose(b_re, (0, 2, 1))
    bt_im = jnp.transpose(b_im, (0, 2, 1))
    return pl.pallas_call(
        _s5prep_kernel,
        out_shape=(
            jax.ShapeDtypeStruct((g, p), F32), jax.ShapeDtypeStruct((g, p), F32),
            jax.ShapeDtypeStruct((g, SSM_GROUP, p), F32), jax.ShapeDtypeStruct((g, SSM_GROUP, p), F32),
        ),
        name="s5prep",
    )(lam_re, lam_im, log_dt.reshape(g, 1), bt_re, bt_im)


def _block_diag(blocks):
    _, r, c = blocks.shape
    b4 = blocks.reshape(N_S5_BLOCKS, S5_BLOCK_GROUPS, r, c)
    eye = jnp.eye(S5_BLOCK_GROUPS, dtype=blocks.dtype)
    out = b4[:, :, :, None, :] * eye[None, :, None, :, None]
    return out.reshape(N_S5_BLOCKS, S5_BLOCK_GROUPS * r, S5_BLOCK_GROUPS * c)


def _mixer_kernel(x_ref, mod_ref, n1w_ref, win_ref, cos_ref, sin_ref, dmask_ref, cdec_ref, sdec_ref,
                  rnw_ref, bmat_ref, cre_ref, cim_ref, lbr_ref, lbi_ref, dsk_ref, wglu_ref, bglu_ref,
                  snw_ref, wout_ref, sret0_ref, sre0_ref, sim0_ref,
                  x1_ref, sret_ref, sre_ref, sim_ref,
                  hb_ref, z_ref, zu_ref, oy_ref, utb_ref, bur0_ref, bur1_ref, bui0_ref, bui1_ref, ytb_ref, yb_ref, modx_ref,
                  *, n_seq, chunk, tile_rows, carry, chunk_decay):
    rows = n_seq * chunk
    seq_per_tile = tile_rows // chunk
    n_tiles = rows // tile_rows
    per_row_mod = chunk % SUBLANES != 0
    dl = D_MODEL // LANES

    def load_states():
        sret_ref[...] = sret0_ref[...]
        sre_ref[...] = sre0_ref[...]
        sim_ref[...] = sim0_ref[...]

    if carry:
        pl.when(pl.program_id(0) == 0)(load_states)
    else:
        load_states()

    if per_row_mod:
        _expand_rows(modx_ref, mod_ref[...], chunk)

    n1w = n1w_ref[...]
    mod_rows = rows if per_row_mod else chunk
    for i in range(rows // mod_rows):
        r0 = i * mod_rows
        xb = _load_rows(x_ref, r0, mod_rows, chunk)
        if per_row_mod:
            sh, sc = _expanded(modx_ref, 0), _expanded(modx_ref, dl)
        else:
            sh = mod_ref[pl.ds(i, 1), pl.ds(0, D_MODEL)]
            sc = mod_ref[pl.ds(i, 1), pl.ds(D_MODEL, D_MODEL)]
        ms = jnp.mean(xb * xb, axis=-1, keepdims=True)
        hn = xb * lax.rsqrt(ms + NORM_EPS) * n1w
        hb_ref[pl.ds(r0, mod_rows), :] = (hn * (1.0 + sc) + sh).astype(BF16)
    ret_w = 4 * RET_WIDTH
    z_ref[...] = jnp.dot(hb_ref[...], win_ref[:, pl.ds(0, ret_w)], preferred_element_type=F32)
    zu = jnp.dot(hb_ref[...], win_ref[:, pl.ds(ret_w, SSM_WIDTH)], preferred_element_type=F32)
    pitch = zu_ref.shape[1] // n_seq
    for c in range(SSM_WIDTH // LANES):
        for b in range(n_seq if pitch != chunk else 1):
            nb = chunk if pitch != chunk else rows
            zu_ref[c, pl.ds(b * pitch, nb), :] = zu[b * chunk:b * chunk + nb, c * LANES:(c + 1) * LANES]

    cos = cos_ref[...]
    sin = sin_ref[...]
    scale = HEAD_DIM ** -0.5
    if seq_per_tile > 1:
        row_id = lax.broadcasted_iota(jnp.int32, (tile_rows, HEAD_DIM), 0)

    def rope(t):
        return t * cos + pltpu.roll(t, HEAD_DIM // 2, 1) * sin

    def ret_tile(ti, c):
        r0 = pl.multiple_of(ti * tile_rows, tile_rows)
        for h in range(RET_HEADS):
            c0 = h * HEAD_DIM
            q = rope(z_ref[pl.ds(r0, tile_rows), pl.ds(c0, HEAD_DIM)])
            k = rope(z_ref[pl.ds(r0, tile_rows), pl.ds(RET_WIDTH + c0, HEAD_DIM)]) * scale
            v = z_ref[pl.ds(r0, tile_rows), pl.ds(2 * RET_WIDTH + c0, HEAD_DIM)]
            g = z_ref[pl.ds(r0, tile_rows), pl.ds(3 * RET_WIDTH + c0, HEAD_DIM)]
            kd = k * sdec_ref[h]
            if tile_rows < HEAD_DIM:
                pad = jnp.zeros((HEAD_DIM - tile_rows, HEAD_DIM), F32)
                k, v, kd = (jnp.concatenate([t, pad], axis=0) for t in (k, v, kd))
                if seq_per_tile > 1:
                    row_kv = lax.broadcasted_iota(jnp.int32, (HEAD_DIM, HEAD_DIM), 0)
            elif seq_per_tile > 1:
                row_kv = row_id
            qb, kb, vb = q.astype(BF16), k.astype(BF16), v.astype(BF16)
            s = lax.dot_general(qb, kb, (((1,), (1,)), ((), ())), preferred_element_type=F32) * dmask_ref[h]
            o = jnp.dot(s.astype(BF16), vb, preferred_element_type=F32)
            cross = None
            for si in range(seq_per_tile):
                sidx = ti * seq_per_tile + si
                st = sret_ref[sidx, h]
                cr = jnp.dot(qb, st.astype(BF16), preferred_element_type=F32)
                if seq_per_tile > 1:
                    in_seq = (row_id >= si * chunk) & (row_id < (si + 1) * chunk)
                    cross = jnp.where(in_seq, cr, 0.0 if cross is None else cross)
                    kds = jnp.where((row_kv >= si * chunk) & (row_kv < (si + 1) * chunk), kd, 0.0)
                else:
                    cross, kds = cr, kd
                upd = lax.dot_general(kds.astype(BF16), vb, (((0,), (0,)), ((), ())), preferred_element_type=F32)
                sret_ref[sidx, h] = st * chunk_decay[h] + upd
            o = o + cross * cdec_ref[h]
            o = o * lax.rsqrt(jnp.mean(o * o, axis=-1, keepdims=True) + NORM_EPS)
            o = o * rnw_ref[:, pl.ds(c0, HEAD_DIM)] * _silu(g)
            oy_ref[pl.ds(r0, tile_rows), pl.ds(c0, HEAD_DIM)] = o
        return c

    lax.fori_loop(0, n_tiles, ret_tile, 0, unroll=True)

    for t in range(chunk):
        for c in range(SSM_WIDTH // LANES):
            utb_ref[pl.ds(t * n_seq, n_seq), pl.ds(c * LANES, LANES)] = zu_ref[c, pl.ds(t, n_seq, stride=pitch), :]
    half_ch = SSM_CH // 2
    blk_per_half = N_S5_BLOCKS // 2
    bur_refs, bui_refs = (bur0_ref, bur1_ref), (bui0_ref, bui1_ref)
    for blk in range(N_S5_BLOCKS):
        hf, lcols = blk // blk_per_half, pl.ds((blk % blk_per_half) * S5_BLOCK_CH, S5_BLOCK_CH)
        ub = utb_ref[:, pl.ds(blk * S5_BLOCK_IN, S5_BLOCK_IN)].astype(BF16)
        bu = jnp.dot(ub, bmat_ref[blk], preferred_element_type=F32)
        bur_refs[hf][:, lcols] = bu[:, :S5_BLOCK_CH]
        bui_refs[hf][:, lcols] = bu[:, S5_BLOCK_CH:]

    scan_w = min(half_ch, SCAN_ELEMS // n_seq)
    for hf in range(2):
        bur_ref, bui_ref = bur_refs[hf], bui_refs[hf]
        for p in range(half_ch // scan_w):
            cols = pl.ds(p * scan_w, scan_w)
            gcols = pl.ds(hf * half_ch + p * scan_w, scan_w)
            lbr = jnp.broadcast_to(lbr_ref[:, gcols], (n_seq, scan_w))
            lbi = jnp.broadcast_to(lbi_ref[:, gcols], (n_seq, scan_w))
            hr, hi = sre_ref[:, gcols], sim_ref[:, gcols]
            for t in range(chunk):
                rws = pl.ds(t * n_seq, n_seq)
                hr, hi = (lbr * hr - lbi * hi + bur_ref[rws, cols], lbr * hi + lbi * hr + bui_ref[rws, cols])
                bur_ref[rws, cols] = hr
                bui_ref[rws, cols] = hi
            sre_ref[:, gcols] = hr
            sim_ref[:, gcols] = hi

    for blk in range(N_S5_BLOCKS):
        hf, lcols = blk // blk_per_half, pl.ds((blk % blk_per_half) * S5_BLOCK_CH, S5_BLOCK_CH)
        yb = jnp.dot(bur_refs[hf][:, lcols].astype(BF16), cre_ref[blk], preferred_element_type=F32)
        yb = yb + jnp.dot(bui_refs[hf][:, lcols].astype(BF16), cim_ref[blk], preferred_element_type=F32)
        ucols = pl.ds(blk * S5_BLOCK_IN, S5_BLOCK_IN)
        ytb_ref[:, ucols] = yb + dsk_ref[:, ucols] * utb_ref[:, ucols]
    for t in range(chunk):
        for c in range(SSM_WIDTH // LANES):
            yb_ref[c, pl.ds(t, n_seq, stride=pitch), :] = ytb_ref[pl.ds(t * n_seq, n_seq), pl.ds(c * LANES, LANES)]

    def seq_major(c):
        if pitch == chunk:
            return yb_ref[c]
        return jnp.concatenate([yb_ref[c, pl.ds(b * pitch, chunk), :] for b in range(n_seq)], axis=0)

    y = jnp.concatenate([seq_major(c) for c in range(SSM_WIDTH // LANES)], axis=1)
    y = jax.nn.gelu(y, approximate=True)
    gate = jnp.dot(y.astype(BF16), wglu_ref[...], preferred_element_type=F32) + bglu_ref[...]
    y = y * jax.nn.sigmoid(gate)
    y = y * lax.rsqrt(jnp.mean(y * y, axis=-1, keepdims=True) + NORM_EPS) * snw_ref[...]
    oy_ref[:, pl.ds(RET_WIDTH, SSM_WIDTH)] = y

    mix = jnp.dot(oy_ref[...].astype(BF16), wout_ref[...], preferred_element_type=F32)
    for i in range(rows // mod_rows):
        r0 = i * mod_rows
        if per_row_mod:
            g1 = _expanded(modx_ref, 2 * dl)
        else:
            g1 = mod_ref[pl.ds(i, 1), pl.ds(2 * D_MODEL, D_MODEL)]
        _store_rows(x1_ref, r0, mod_rows, chunk,
                    _load_rows(x_ref, r0, mod_rows, chunk) + g1 * mix[r0:r0 + mod_rows])


def _load_rows(ref, r0, n, chunk):
    if len(ref.shape) == 2:
        return ref[pl.ds(r0, n), :]
    assert n == chunk and r0 % chunk == 0
    return ref[r0 // chunk]


def _store_rows(ref, r0, n, chunk, val):
    if len(ref.shape) == 2:
        ref[pl.ds(r0, n), :] = val
    else:
        assert n == chunk and r0 % chunk == 0
        ref[r0 // chunk] = val


def _seq_pitch(chunk):
    return chunk + SUBLANES if chunk % SUBLANES == 0 else chunk


def _const_spec(shape):
    nd = len(shape)
    return pl.BlockSpec(shape, lambda j, _n=nd: (0,) * _n)


def _decay_tables(chunk, tile_rows):
    f32 = np.float32
    log_gamma = np.log1p(-np.exp2(f32(-5.0) - np.arange(RET_HEADS, dtype=f32))).astype(f32)
    r = np.arange(tile_rows)
    seq, loc = r // chunk, (r % chunk).astype(f32)
    rel = loc[:, None] - loc[None, :]
    ok = (seq[:, None] == seq[None, :]) & (rel >= 0)
    dmask = np.where(ok[None], np.exp(np.where(ok, rel, f32(0.0))[None] * log_gamma[:, None, None]), f32(0.0))
    if tile_rows < HEAD_DIM:
        dmask = np.pad(dmask, ((0, 0), (0, 0), (0, HEAD_DIM - tile_rows)))
    cdec = np.exp((loc[None, :] + f32(1.0)) * log_gamma[:, None])
    sdec = np.exp((f32(chunk) - f32(1.0) - loc)[None, :] * log_gamma[:, None])
    bcast = lambda t: np.ascontiguousarray(np.broadcast_to(t[:, :, None], (RET_HEADS, tile_rows, HEAD_DIM)))
    return dmask.astype(f32), bcast(cdec.astype(f32)), bcast(sdec.astype(f32))


def _rope_tables(pos):
    f32 = np.float32
    half = HEAD_DIM // 2
    inv_freq = (f32(ROPE_BASE) ** (-np.arange(half, dtype=f32) / f32(half))).astype(f32)
    ang = (pos.astype(f32)[:, None] * inv_freq[None, :]).astype(f32)
    cos, sin = np.cos(ang).astype(f32), np.sin(ang).astype(f32)
    return np.concatenate([cos, cos], axis=-1), np.concatenate([-sin, sin], axis=-1)


def _mixer(x, mod, pos, states, wts, *, prompt):
    n_seq = SEQ_PER_BLOCK if prompt else SAMPLE_SEQ_PER_BLOCK
    if prompt:
        n_total, seq_len, _ = x.shape
        assert n_total == n_seq
        chunk, tile_rows, n_steps = PROMPT_CHUNK, PROMPT_CHUNK, seq_len // PROMPT_CHUNK
        x_spec = pl.BlockSpec((n_seq, chunk, D_MODEL), lambda j: (0, j, 0))
        mod_spec = pl.BlockSpec((n_seq, 3 * D_MODEL), lambda j: (0, 0))
        tab_spec = pl.BlockSpec((chunk, HEAD_DIM), lambda j: (j, 0))
        seq_map = lambda j: 0
    else:
        chunk = pos.shape[0]
        tile_rows = SUBLANES
        n_total = x.shape[0] // chunk
        n_steps = n_total // n_seq
        x_spec = pl.BlockSpec((n_seq * chunk, D_MODEL), lambda j: (j, 0))
        mod_spec = pl.BlockSpec((n_seq, 3 * D_MODEL), lambda j: (j, 0))
        tab_spec = _const_spec((tile_rows, HEAD_DIM))
        seq_map = lambda j: j
    rows = n_seq * chunk
    cos, sin = _rope_tables(pos)
    if not prompt:
        reps = tile_rows // chunk
        cos, sin = np.tile(cos, (reps, 1)), np.tile(sin, (reps, 1))
    dmask, cdec, sdec = _decay_tables(chunk, tile_rows)
    chunk_decay = tuple(float(math.exp(chunk * math.log1p(-2.0 ** (-5.0 - h)))) for h in range(RET_HEADS))
    sret0, sre0, sim0 = states

    st_ret_spec = pl.BlockSpec((n_seq, RET_HEADS, HEAD_DIM, HEAD_DIM), lambda j: (seq_map(j), 0, 0, 0))
    st_s5_spec = pl.BlockSpec((n_seq, SSM_CH), lambda j: (seq_map(j), 0))
    consts = [dmask, cdec, sdec, wts["rnw"], wts["bmat"], wts["cre"], wts["cim"], wts["lbr"], wts["lbi"],
              wts["dsk"], wts["w_glu"], wts["b_glu"], wts["snw"], wts["w_out"]]
    args = [x, mod, wts["n1w"], wts["w_in"], cos, sin] + consts + [sret0, sre0, sim0]
    in_specs = ([x_spec, mod_spec, _const_spec(wts["n1w"].shape), _const_spec(wts["w_in"].shape), tab_spec, tab_spec]
                + [_const_spec(a.shape) for a in consts] + [st_ret_spec, st_s5_spec, st_s5_spec])

    kern = functools.partial(_mixer_kernel, n_seq=n_seq, chunk=chunk, tile_rows=tile_rows, carry=prompt,
                             chunk_decay=chunk_decay)
    out_shape = (
        jax.ShapeDtypeStruct(x.shape, F32),
        jax.ShapeDtypeStruct((n_total, RET_HEADS, HEAD_DIM, HEAD_DIM), F32),
        jax.ShapeDtypeStruct((n_total, SSM_CH), F32),
        jax.ShapeDtypeStruct((n_total, SSM_CH), F32),
    )
    scratch = [
        pltpu.VMEM((rows, D_MODEL), BF16),
        pltpu.VMEM((rows, 4 * RET_WIDTH), F32),
        pltpu.VMEM((SSM_WIDTH // LANES, n_seq * _seq_pitch(chunk), LANES), F32),
        pltpu.VMEM((rows, D_MODEL), F32),
        pltpu.VMEM((rows, SSM_WIDTH), F32),
        pltpu.VMEM((rows, SSM_CH // 2), F32),
        pltpu.VMEM((rows, SSM_CH // 2), F32),
        pltpu.VMEM((rows, SSM_CH // 2), F32),
        pltpu.VMEM((rows, SSM_CH // 2), F32),
        pltpu.VMEM((rows, SSM_WIDTH), F32),
        pltpu.VMEM((SSM_WIDTH // LANES, n_seq * _seq_pitch(chunk), LANES), F32),
        pltpu.VMEM((3 * D_MODEL // LANES, SUBLANES if prompt else rows, LANES), F32),
    ]
    return pl.pallas_call(
        kern,
        grid=(n_steps,),
        in_specs=in_specs,
        out_specs=(x_spec, st_ret_spec, st_s5_spec, st_s5_spec),
        out_shape=out_shape,
        scratch_shapes=scratch,
        compiler_params=pltpu.CompilerParams(dimension_semantics=("arbitrary",), vmem_limit_bytes=VMEM_LIMIT),
        name="mixer_prompt" if prompt else "mixer_sample",
    )(*args)


PACK_ROWS = D_MODEL // (2 * LANES)


def _store_packed(ref, x):
    half = D_MODEL // 2
    bits = lax.bitcast_convert_type(x.astype(BF16).astype(F32), jnp.uint32)
    words = bits[:, :half] | (bits[:, half:] >> 16)
    for c in range(PACK_ROWS):
        ref[c] = words[:, c * LANES:(c + 1) * LANES]


def _load_packed(ref, n, first_row=0, row_stride=1):
    hi, lo = [], []
    for c in range(PACK_ROWS):
        w = ref[c] if row_stride == 1 else ref[c, pl.ds(first_row, n, stride=row_stride), :]
        hi.append(lax.bitcast_convert_type(w & jnp.uint32(0xFFFF0000), F32))
        lo.append(lax.bitcast_convert_type(w << 16, F32))
    return jnp.concatenate(hi + lo, axis=1)


def _split_bf16(x):
    hi = x.astype(BF16)
    return hi, (x - hi.astype(F32)).astype(BF16)


def _route_tile(x, sh, sc, n2w_ref, wrh_ref, wrm_ref, br_ref, ltri_ref, count_ref, h2_ref, route_ref, topw_ref):
    ms = jnp.mean(x * x, axis=-1, keepdims=True)
    h2 = x * lax.rsqrt(ms + NORM_EPS) * n2w_ref[...] * (1.0 + sc) + sh
    _store_packed(h2_ref, h2)
    hh, hm = _split_bf16(h2)
    logits = (jnp.dot(hh, wrh_ref[...], preferred_element_type=F32)
              + (jnp.dot(hh, wrm_ref[...], preferred_element_type=F32)
                 + jnp.dot(hm, wrh_ref[...], preferred_element_type=F32))) + br_ref[...]
    lane = lax.broadcasted_iota(jnp.int32, logits.shape, 1)
    lane_f = lane.astype(F32)
    work = logits
    vals, idxs = [], []
    for _ in range(TOP_K):
        m = jnp.max(work, axis=-1, keepdims=True)
        idx = jnp.min(jnp.where(work == m, lane_f, float(LANES)), axis=-1, keepdims=True)
        vals.append(m)
        idxs.append(idx)
        work = jnp.where(lane_f == idx, -jnp.inf, work)
    exps = [jnp.exp(v - vals[0]) for v in vals]
    tot = exps[0] + exps[1] + exps[2] + exps[3]
    topw = jnp.zeros(logits.shape, F32)
    for k in range(TOP_K):
        topw = jnp.where(lane == k, exps[k] / tot, topw)
    topw_ref[...] = topw

    onehot = [(lane_f == idxs[k]).astype(F32) for k in range(TOP_K)]
    chosen = onehot[0] + onehot[1] + onehot[2] + onehot[3]
    before = jnp.dot(ltri_ref[...], chosen.astype(BF16), preferred_element_type=F32) + count_ref[...]
    info = jnp.zeros(logits.shape, jnp.int32)
    for k in range(TOP_K):
        rank = jnp.sum(onehot[k] * before, axis=-1, keepdims=True).astype(jnp.int32)
        info = jnp.where(lane == k, idxs[k].astype(jnp.int32), info)
        info = jnp.where(lane == TOP_K + k, rank, info)
    route_ref[...] = jnp.transpose(info)[:2 * TOP_K, :]
    count_ref[...] = count_ref[...] + jnp.sum(chosen, axis=0, keepdims=True)


def _router_kernel(xp_ref, shp_ref, scp_ref, xs_ref, shs_ref, scs_ref, n2w_ref, wrh_ref, wrm_ref, br_ref, ltri_ref,
                   h2_ref, route_ref, topw_ref, count_ref, modx_ref, *, n_prompt_tiles, sample_len):
    i = pl.program_id(0)
    rest = (n2w_ref, wrh_ref, wrm_ref, br_ref, ltri_ref, count_ref, h2_ref, route_ref, topw_ref)

    @pl.when(i == 0)
    def _():
        count_ref[...] = jnp.zeros(count_ref.shape, F32)

    @pl.when(i < n_prompt_tiles)
    def _():
        _route_tile(xp_ref[...], shp_ref[0], scp_ref[0], *rest)

    @pl.when(i >= n_prompt_tiles)
    def _():
        dl = D_MODEL // LANES
        _expand_rows(modx_ref, shs_ref[...], sample_len)
        _expand_rows(modx_ref, scs_ref[...], sample_len, first_chunk=dl)
        _route_tile(xs_ref[...], _expanded(modx_ref, 0), _expanded(modx_ref, dl), *rest)


def _router(xp_rows, mod_p, seq_len, xs_rows, mod_s, sample_len, n2w, w_router, b_router):
    tile = ROUTER_TILE
    n_p, n_s = xp_rows.shape[0], xs_rows.shape[0]
    tp, ts = n_p // tile, n_s // tile
    n_total = n_p + n_s
    mod_p = mod_p.reshape(mod_p.shape[0], 1, mod_p.shape[1])
    seq_of = lambda i: (jnp.minimum(i, tp - 1) * tile) // seq_len
    wr_pad = jnp.pad(w_router, ((0, 0), (0, LANES - N_EXPERTS)))
    wr_hi = wr_pad.astype(BF16)
    wr_mid = (wr_pad - wr_hi.astype(F32)).astype(BF16)
    br_pad = jnp.pad(b_router, ((0, 0), (0, LANES - N_EXPERTS)), constant_values=-1e30)
    ltri = jnp.asarray(np.tril(np.ones((tile, tile), np.float32), -1), BF16)
    clamp_p = lambda i: jnp.minimum(i, tp - 1)
    clamp_s = lambda i: jnp.maximum(i - tp, 0)
    return pl.pallas_call(
        functools.partial(_router_kernel, n_prompt_tiles=tp, sample_len=sample_len),
        grid=(tp + ts,),
        in_specs=[pl.BlockSpec((tile, D_MODEL), lambda i: (clamp_p(i), 0)),
                  pl.BlockSpec((1, 1, D_MODEL), lambda i: (seq_of(i), 0, 3)),
                  pl.BlockSpec((1, 1, D_MODEL), lambda i: (seq_of(i), 0, 4)),
                  pl.BlockSpec((tile, D_MODEL), lambda i: (clamp_s(i), 0)),
                  pl.BlockSpec((tile // sample_len, D_MODEL), lambda i: (clamp_s(i), 3)),
                  pl.BlockSpec((tile // sample_len, D_MODEL), lambda i: (clamp_s(i), 4)),
                  _const_spec(n2w.shape), _const_spec(wr_hi.shape), _const_spec(wr_mid.shape),
                  _const_spec(br_pad.shape), _const_spec(ltri.shape)],
        out_specs=(pl.BlockSpec((PACK_ROWS, tile, LANES), lambda i: (0, i, 0)),
                   pl.BlockSpec((2 * TOP_K, tile), lambda i: (0, i)),
                   pl.BlockSpec((tile, LANES), lambda i: (i, 0)),
                   pl.BlockSpec((1, LANES), lambda i: (0, 0))),
        out_shape=(jax.ShapeDtypeStruct((PACK_ROWS, n_total, LANES), jnp.uint32),
                   jax.ShapeDtypeStruct((2 * TOP_K, n_total), jnp.int32),
                   jax.ShapeDtypeStruct((n_total, LANES), F32),
                   jax.ShapeDtypeStruct((1, LANES), F32)),
        scratch_shapes=[pltpu.VMEM((2 * D_MODEL // LANES, tile, LANES), F32)],
        compiler_params=pltpu.CompilerParams(dimension_semantics=("arbitrary",), vmem_limit_bytes=VMEM_LIMIT),
        name="router",
    )(xp_rows, mod_p, mod_p, xs_rows, mod_s, mod_s, n2w, wr_hi, wr_mid, br_pad, ltri)


def _gather_rows(table, idx):
    n = idx.shape[0]
    steps = n // SC_GATHER_WINDOW
    assert n % SC_GATHER_WINDOW == 0 and steps % SC_WORKERS == 0
    mesh = plsc.VectorSubcoreMesh(core_axis_name="c", subcore_axis_name="s")

    @functools.partial(pl.kernel, out_type=jax.ShapeDtypeStruct((n, table.shape[1]), table.dtype), mesh=mesh,
                       scratch_types=[])
    def gather_kernel(table_hbm, idx_hbm, out_hbm):
        def body(idx_vmem, out_vmem):
            pltpu.sync_copy(table_hbm.at[idx_vmem.at[0]], out_vmem)

        pltpu.emit_pipeline(
            body,
            grid=(steps,),
            in_specs=[pl.BlockSpec((1, SC_GATHER_WINDOW), lambda i: (0, i))],
            out_specs=[pl.BlockSpec((SC_GATHER_WINDOW, table.shape[1]), lambda i: (i, 0))],
            core_axis_name=("c", "s"),
            dimension_semantics=(pltpu.PARALLEL,),
        )(idx_hbm, out_hbm)

    return gather_kernel(table, idx.reshape(1, n))


def _dispatch_packed(table, slot_kt, n_slots):
    planes, n_tok, lanes = table.shape
    win = SC_GATHER_WINDOW
    blocks = n_tok // win
    assert n_tok % win == 0
    blocks_pad = _round_up(blocks, SC_WORKERS // math.gcd(SC_WORKERS, planes))
    n_spare = (blocks_pad - blocks) * TOP_K * win
    n_ext = n_slots + n_spare
    dest = jnp.transpose(slot_kt.reshape(TOP_K, blocks, win), (1, 0, 2))
    spare = n_slots + jnp.arange(n_spare, dtype=jnp.int32).reshape(blocks_pad - blocks, TOP_K, win)
    dest = jnp.concatenate([dest, spare], axis=0)[None] + (jnp.arange(planes, dtype=jnp.int32) * n_ext)[:, None, None, None]
    dest = dest.reshape(planes * blocks_pad * TOP_K, win)
    steps = planes * blocks_pad
    src_block = lambda g: (g // blocks_pad) * blocks + jnp.minimum(g % blocks_pad, blocks - 1)
    mesh = plsc.VectorSubcoreMesh(core_axis_name="c", subcore_axis_name="s")

    @functools.partial(pl.kernel, out_type=jax.ShapeDtypeStruct((planes * n_ext, lanes), table.dtype), mesh=mesh,
                       scratch_types=[])
    def scatter_kernel(table_hbm, dest_hbm, out_hbm):
        def body(rows_vmem, dest_vmem):
            for k in range(TOP_K):
                pltpu.sync_copy(rows_vmem, out_hbm.at[dest_vmem.at[k]])

        pltpu.emit_pipeline(
            body,
            grid=(steps,),
            in_specs=[pl.BlockSpec((win, lanes), lambda g: (src_block(g), 0)),
                      pl.BlockSpec((TOP_K, win), lambda g: (g, 0))],
            out_specs=[],
            core_axis_name=("c", "s"),
            dimension_semantics=(pltpu.PARALLEL,),
        )(table_hbm, dest_hbm)

    return scatter_kernel(table.reshape(planes * n_tok, lanes), dest).reshape(planes, n_ext, lanes)


def _gather_packed(table, rows):
    planes, n_table, lanes = table.shape
    idx = jnp.concatenate([rows + c * n_table for c in range(planes)])
    out = _gather_rows(table.reshape(planes * n_table, lanes), idx)
    return out.reshape(planes, rows.shape[0], lanes)


def _expert_weight_copies(e, w1_hbm, w2_hbm, w1s_ref, w2s_ref, sem):
    return (pltpu.make_async_copy(w1_hbm.at[e], w1s_ref, sem.at[0]),
            pltpu.make_async_copy(w2_hbm.at[e], w2s_ref, sem.at[1]))


def _experts_kernel(te_ref, tr_ref, nx_ref, nv_ref, xs_ref, w1_hbm, b1_ref, w2_hbm, b2_ref, ys_ref,
                    w1s_ref, w2s_ref, w1b_ref, w2b_ref, sem):
    i = pl.program_id(0)
    e = te_ref[i]
    new_expert = (i == 0) | (e != te_ref[jnp.maximum(i - 1, 0)])
    copies = functools.partial(_expert_weight_copies, w1_hbm=w1_hbm, w2_hbm=w2_hbm, w1s_ref=w1s_ref,
                               w2s_ref=w2s_ref, sem=sem)

    @pl.when(i == 0)
    def _():
        for c in copies(e):
            c.start()

    @pl.when(new_expert)
    def _():
        for c in copies(e):
            c.wait()
        w1b_ref[...] = w1s_ref[...].astype(BF16)
        w2b_ref[...] = w2s_ref[...].astype(BF16)

        @pl.when(nx_ref[i] >= 0)
        def _():
            for c in copies(nx_ref[i]):
                c.start()

    def expert_pass(r0, n_rows, rows_valid):
        xs_v, ys_v = xs_ref.at[:, pl.ds(r0, n_rows)], ys_ref.at[:, pl.ds(r0, n_rows)]
        row = lax.broadcasted_iota(jnp.int32, (n_rows, D_MODEL), 0)
        x = jnp.where(row < rows_valid, _load_packed(xs_v, n_rows), 0.0).astype(BF16)
        hu = jnp.dot(x, w1b_ref[...], preferred_element_type=F32) + b1_ref[0]
        x_glu = jnp.minimum(hu[:, :D_FF], SWIGLU_LIMIT)
        x_lin = jnp.clip(hu[:, D_FF:], -SWIGLU_LIMIT, SWIGLU_LIMIT)
        act = x_glu * jax.nn.sigmoid(SWIGLU_ALPHA * x_glu) * (x_lin + 1.0)
        _store_packed(ys_v, jnp.dot(act.astype(BF16), w2b_ref[...], preferred_element_type=F32) + b2_ref[0])

    def zero_rows(r0, n_rows):
        for c in range(PACK_ROWS):
            ys_ref[c, pl.ds(r0, n_rows), :] = jnp.zeros((n_rows, LANES), jnp.uint32)

    half = EXPERT_ROWS // 2
    rows_tile = jnp.where(i < nv_ref[0], tr_ref[i], 0)
    full_tile = rows_tile > SLOT_TILE - half

    @pl.when(full_tile)
    def _():
        expert_pass(0, SLOT_TILE, rows_tile)

    for h in range(SLOT_TILE // EXPERT_ROWS):
        r0 = h * EXPERT_ROWS
        rows_here = jnp.where(full_tile, -1, rows_tile - r0)

        @pl.when(rows_here > half)
        def _(r0=r0, rows_here=rows_here):
            expert_pass(r0, EXPERT_ROWS, rows_here)

        @pl.when((rows_here > 0) & (rows_here <= half))
        def _(r0=r0, rows_here=rows_here):
            expert_pass(r0, half, rows_here)
            zero_rows(r0 + half, half)

        @pl.when((rows_here <= 0) & jnp.logical_not(full_tile))
        def _(r0=r0):
            zero_rows(r0, EXPERT_ROWS)


def _experts(tile_expert, tile_rows, next_expert, n_valid, xs, w1, b1, w2, b2):
    n_tiles = tile_expert.shape[0]
    n_slots = n_tiles * SLOT_TILE
    grid_spec = pltpu.PrefetchScalarGridSpec(
        num_scalar_prefetch=4,
        grid=(n_tiles,),
        in_specs=[
            pl.BlockSpec((PACK_ROWS, SLOT_TILE, LANES), lambda i, te, tr, nx, nv: (0, i, 0)),
            pl.BlockSpec(memory_space=pl.ANY),
            pl.BlockSpec((1, 1, 2 * D_FF), lambda i, te, tr, nx, nv: (te[i], 0, 0)),
            pl.BlockSpec(memory_space=pl.ANY),
            pl.BlockSpec((1, 1, D_MODEL), lambda i, te, tr, nx, nv: (te[i], 0, 0)),
        ],
        out_specs=pl.BlockSpec((PACK_ROWS, SLOT_TILE, LANES), lambda i, te, tr, nx, nv: (0, i, 0)),
        scratch_shapes=[pltpu.VMEM((D_MODEL, 2 * D_FF), F32), pltpu.VMEM((D_FF, D_MODEL), F32),
                        pltpu.VMEM((D_MODEL, 2 * D_FF), BF16), pltpu.VMEM((D_FF, D_MODEL), BF16),
                        pltpu.SemaphoreType.DMA((2,))],
    )
    return pl.pallas_call(
        _experts_kernel,
        grid_spec=grid_spec,
        out_shape=jax.ShapeDtypeStruct((PACK_ROWS, n_slots, LANES), jnp.uint32),
        compiler_params=pltpu.CompilerParams(dimension_semantics=("arbitrary",), vmem_limit_bytes=VMEM_LIMIT),
        name="experts",
    )(tile_expert, tile_rows, next_expert, n_valid, xs, w1, b1.reshape(N_EXPERTS, 1, 2 * D_FF), w2,
      b2.reshape(N_EXPERTS, 1, D_MODEL))


def _combine_kernel(y4_ref, x1_ref, topw_ref, g2_ref, fw_ref, *rest, reps):
    o_ref = rest[-2] if reps else rest[-1]
    w = topw_ref[...]
    n = w.shape[0]
    ff = None
    for k in range(TOP_K):
        yk = w[:, k:k + 1] * _load_packed(y4_ref.at[:, k], n)
        ff = yk if ff is None else ff + yk
    if reps:
        _expand_rows(rest[-1], g2_ref[...], reps)
        g2 = _expanded(rest[-1], 0)
    else:
        g2 = g2_ref[0]
    x = x1_ref[...] + g2 * ff
    ms = jnp.mean(x * x, axis=-1, keepdims=True)
    o_ref[...] = x * lax.rsqrt(ms + NORM_EPS) * fw_ref[...]


def _combine(y4, y4_row0, x1, x1_row0, n_rows, topw, topw_row0, mod, rows_per_mod, fw, out_buf):
    tile = COMBINE_TILE
    y4_off, x1_off, tw_off = y4_row0 // tile, x1_row0 // tile, topw_row0 // tile
    reps = rows_per_mod if rows_per_mod < tile else 0
    scratch = []
    if reps:
        g2_spec = pl.BlockSpec((tile // reps, D_MODEL), lambda i: (i + x1_off, 5))
        scratch = [pltpu.VMEM((D_MODEL // LANES, tile, LANES), F32)]
    else:
        mod = mod.reshape(mod.shape[0], 1, mod.shape[1])
        g2_spec = pl.BlockSpec((1, 1, D_MODEL), lambda i: (((i + x1_off) * tile) // rows_per_mod, 0, 5))
    in_specs = [pl.BlockSpec((PACK_ROWS, TOP_K, tile, LANES), lambda i: (0, 0, i + y4_off, 0)),
                pl.BlockSpec((tile, D_MODEL), lambda i: (i + x1_off, 0)),
                pl.BlockSpec((tile, LANES), lambda i: (i + tw_off, 0)),
                g2_spec, _const_spec(fw.shape)]
    args = [y4, x1, topw, mod, fw]
    aliases = {}
    if out_buf is not None:
        in_specs.append(pl.BlockSpec(memory_space=pl.ANY))
        args.append(out_buf)
        aliases = {len(args) - 1: 0}
    return pl.pallas_call(
        functools.partial(_combine_kernel, reps=reps),
        grid=(n_rows // tile,),
        in_specs=in_specs,
        out_specs=pl.BlockSpec((tile, D_MODEL), lambda i: (i + x1_off, 0)),
        out_shape=jax.ShapeDtypeStruct(x1.shape, F32),
        input_output_aliases=aliases,
        scratch_shapes=scratch,
        compiler_params=pltpu.CompilerParams(dimension_semantics=("arbitrary",), vmem_limit_bytes=VMEM_LIMIT),
        name="combine",
    )(*args)


def _routing_tables(route, counts, n_slots):
    padded = ((counts + SLOT_TILE - 1) // SLOT_TILE) * SLOT_TILE
    pend = jnp.cumsum(padded)
    poff = pend - padded
    expert_kt, rank_kt = route[:TOP_K], route[TOP_K:]
    experts = jnp.arange(N_EXPERTS, dtype=jnp.int32)
    start_kt = jnp.sum((expert_kt[None] == experts[:, None, None]).astype(jnp.int32) * poff[:, None, None], axis=0)
    slot_kt = start_kt + rank_kt
    n_tiles = n_slots // SLOT_TILE
    n_valid = (pend[-1] // SLOT_TILE).astype(jnp.int32)
    tile_row = jnp.minimum(jnp.arange(n_tiles, dtype=jnp.int32), n_valid - 1) * SLOT_TILE
    in_later = (pend[None, :] <= tile_row[:, None]).astype(jnp.int32)
    tile_e = jnp.sum(in_later, axis=1).astype(jnp.int32)
    is_e = (experts[None, :] == tile_e[:, None]).astype(jnp.int32)
    used_end = jnp.sum(is_e * (poff + counts)[None, :], axis=1)
    tile_rows = jnp.clip(used_end - tile_row, 0, SLOT_TILE).astype(jnp.int32)
    later_used = (experts[None, :] > tile_e[:, None]) & (counts[None, :] > 0)
    next_e = jnp.min(jnp.where(later_used, experts[None, :], N_EXPERTS), axis=1)
    next_e = jnp.where(next_e < N_EXPERTS, next_e, -1).astype(jnp.int32)
    return slot_kt, tile_e, tile_rows, next_e, n_valid.reshape(1)


def _round_up(n, m):
    return ((n + m - 1) // m) * m


def kernel(x_prompt, x_sample, c_prompt, c_sample, state_ret, state_s5_re, state_s5_im, norm1_w, norm2_w, w_ada, b_ada, w_in, ret_norm_w, s5_lam_re, s5_lam_im, s5_log_dt, s5_b_re, s5_b_im, s5_c_re, s5_c_im, s5_d, w_glu, b_glu, s5_norm_w, w_out, w_router, b_router, w1, b1, w2, b2, final_w):
    bp, lp, _ = x_prompt.shape
    bs, ls, _ = x_sample.shape
    assert norm1_w.shape[0] == 1, "single-layer model"
    n_p, n_s = bp * lp, bs * ls
    n_tok = n_p + n_s

    mod = _ada(jnp.concatenate([c_prompt, c_sample], axis=0), w_ada[0], b_ada[0])
    mod_p, mod_s = mod[:bp], mod[bp:]

    lbr, lbi, bbr, bbi = _s5prep(s5_lam_re[0], s5_lam_im[0], s5_log_dt[0], s5_b_re[0], s5_b_im[0])
    bmat = jnp.concatenate([_block_diag(bbr), _block_diag(bbi)], axis=-1).astype(BF16)
    cre = _block_diag(jnp.transpose(s5_c_re[0], (0, 2, 1))).astype(BF16)
    cim = _block_diag(jnp.transpose(-s5_c_im[0], (0, 2, 1))).astype(BF16)
    wts = dict(
        n1w=norm1_w, w_in=w_in[0].astype(BF16), rnw=ret_norm_w, bmat=bmat, cre=cre, cim=cim,
        lbr=lbr.reshape(1, SSM_CH), lbi=lbi.reshape(1, SSM_CH), dsk=s5_d[0].reshape(1, SSM_WIDTH),
        w_glu=w_glu[0].astype(BF16), b_glu=b_glu, snw=s5_norm_w, w_out=w_out[0].astype(BF16),
    )

    zero_states = (jnp.zeros((bp, RET_HEADS, HEAD_DIM, HEAD_DIM), F32), jnp.zeros((bp, SSM_CH), F32),
                   jnp.zeros((bp, SSM_CH), F32))
    x1_p, ret_p, re_p, im_p = _mixer(x_prompt, mod_p, np.arange(lp, dtype=np.float32), zero_states, wts,
                                     prompt=True)
    sample_states = (state_ret[0], state_s5_re[0].reshape(bs, SSM_CH), state_s5_im[0].reshape(bs, SSM_CH))
    x1_s, ret_s, re_s, im_s = _mixer(x_sample.reshape(n_s, D_MODEL), mod_s,
                                     PAST_LEN + np.arange(ls, dtype=np.float32), sample_states, wts, prompt=False)

    x1_p_rows = x1_p.reshape(n_p, D_MODEL)
    h2, route, topw, counts = _router(x1_p_rows, mod_p, lp, x1_s, mod_s, ls, norm2_w, w_router[0], b_router)

    n_assign = n_tok * TOP_K
    gather_quantum = SC_GATHER_WINDOW * SC_WORKERS // PACK_ROWS
    assert n_assign % gather_quantum == 0
    n_slots = _round_up(_round_up(n_assign, SLOT_TILE) + N_EXPERTS * SLOT_TILE, gather_quantum)
    slot_kt, tile_e, tile_rows, next_e, n_valid = _routing_tables(route, counts[0, :N_EXPERTS].astype(jnp.int32),
                                                                  n_slots)
    xs = _dispatch_packed(h2, slot_kt, n_slots)
    ys = _experts(tile_e, tile_rows, next_e, n_valid, xs, w1[0], b1[0], w2[0], b2[0])
    fw = final_w.reshape(1, D_MODEL)
    y_p, y_s = None, None
    bounds = [r * (n_p // COMBINE_RANGES) for r in range(COMBINE_RANGES)] + [n_tok]
    for lo, hi in zip(bounds[:-1], bounds[1:]):
        y4 = _gather_packed(ys, slot_kt[:, lo:hi].reshape(-1)).reshape(PACK_ROWS, TOP_K, hi - lo, LANES)
        y_p = _combine(y4, 0, x1_p_rows, lo, min(hi, n_p) - lo, topw, lo, mod_p, lp, fw, y_p)
        if hi > n_p:
            y_s = _combine(y4, n_p - lo, x1_s, 0, n_s, topw, n_p, mod_s, ls, fw, None)

    g, p = SSM_GROUPS, SSM_STATE
    return (y_p.reshape(bp, lp, D_MODEL), y_s.reshape(bs, ls, D_MODEL),
            ret_p[None], re_p.reshape(1, bp, g, p), im_p.reshape(1, bp, g, p),
            ret_s[None], re_s.reshape(1, bs, g, p), im_s.reshape(1, bs, g, p))
```

```python
import functools
import math

import jax
import jax.numpy as jnp
import numpy as np
from jax import lax
from jax.experimental import pallas as pl
from jax.experimental.pallas import tpu as pltpu
from jax.experimental.pallas import tpu_sc as plsc

F32 = jnp.float32
BF16 = jnp.bfloat16

D_MODEL = 1024
PAST_LEN = 16384
RET_WIDTH = 512
RET_HEADS = 4
HEAD_DIM = 128
ROPE_BASE = 10000.0
SSM_WIDTH = 512
SSM_GROUP = 16
SSM_GROUPS = 32
SSM_STATE = 64
SSM_CH = SSM_GROUPS * SSM_STATE
N_EXPERTS = 32
TOP_K = 4
D_FF = 1024
SWIGLU_LIMIT = 7.0
SWIGLU_ALPHA = 1.702
NORM_EPS = 1e-6

LANES = 128
SUBLANES = 8
VMEM_LIMIT = 56 * 1024 * 1024

SEQ_PER_BLOCK = 8
SAMPLE_SEQ_PER_BLOCK = 16
SCAN_ELEMS = 8 * 1024
PROMPT_CHUNK = 64
S5_BLOCK_GROUPS = 8
N_S5_BLOCKS = SSM_GROUPS // S5_BLOCK_GROUPS
S5_BLOCK_IN = S5_BLOCK_GROUPS * SSM_GROUP
S5_BLOCK_CH = S5_BLOCK_GROUPS * SSM_STATE
ROUTER_TILE = 512
SLOT_TILE = 512
EXPERT_ROWS = 256
COMBINE_TILE = 512
COMBINE_SPLITS = (0, 2, 5)
SC_GATHER_WINDOW = 128
SC_WORKERS = 32


def _silu(x):
    return x * jax.nn.sigmoid(x)


def _expand_rows(dst_ref, src, reps, first_chunk=0):
    n = src.shape[0]
    for c in range(src.shape[1] // LANES):
        piece = src[:, c * LANES:(c + 1) * LANES]
        for t in range(reps):
            dst_ref[first_chunk + c, pl.ds(t, n, stride=reps), :] = piece


def _expanded(ref, first_chunk, n_chunks=D_MODEL // LANES):
    return jnp.concatenate([ref[c] for c in range(first_chunk, first_chunk + n_chunks)], axis=1)


def _ada_kernel(c_ref, w_ref, b_ref, o_ref):
    sh, sm = _split_bf16(_silu(c_ref[...]))
    wh, wm = _split_bf16(w_ref[...])
    o_ref[...] = (jnp.dot(sh, wh, preferred_element_type=F32)
                  + (jnp.dot(sh, wm, preferred_element_type=F32) + jnp.dot(sm, wh, preferred_element_type=F32))
                  + b_ref[...])


def _ada(c_all, w_ada, b_ada):
    n_rows, n_out = c_all.shape[0], w_ada.shape[1]
    tn = 1536
    return pl.pallas_call(
        _ada_kernel,
        grid=(n_out // tn,),
        in_specs=[
            pl.BlockSpec((n_rows, D_MODEL), lambda j: (0, 0)),
            pl.BlockSpec((D_MODEL, tn), lambda j: (0, j)),
            pl.BlockSpec((1, tn), lambda j: (0, j)),
        ],
        out_specs=pl.BlockSpec((n_rows, tn), lambda j: (0, j)),
        out_shape=jax.ShapeDtypeStruct((n_rows, n_out), F32),
        compiler_params=pltpu.CompilerParams(dimension_semantics=("arbitrary",), vmem_limit_bytes=VMEM_LIMIT),
        name="ada",
    )(c_all, w_ada, b_ada.reshape(1, n_out))


def _s5prep_kernel(lre_ref, lim_ref, ldt_ref, bre_ref, bim_ref, lbr_ref, lbi_ref, bbr_ref, bbi_ref):
    lam_re, lam_im = lre_ref[...], lim_ref[...]
    dt = jnp.exp(ldt_ref[...])
    mag = jnp.exp(lam_re * dt)
    ang = lam_im * dt
    lb_re, lb_im = mag * jnp.cos(ang), mag * jnp.sin(ang)
    den = lam_re * lam_re + lam_im * lam_im
    f_re = ((lb_re - 1.0) * lam_re + lb_im * lam_im) / den
    f_im = (lb_im * lam_re - (lb_re - 1.0) * lam_im) / den
    lbr_ref[...] = lb_re
    lbi_ref[...] = lb_im
    b_re, b_im = bre_ref[...], bim_ref[...]
    bbr_ref[...] = f_re[:, None, :] * b_re - f_im[:, None, :] * b_im
    bbi_ref[...] = f_re[:, None, :] * b_im + f_im[:, None, :] * b_re


def _s5prep(lam_re, lam_im, log_dt, b_re, b_im):
    g, p = lam_re.shape
    bt_re = jnp.transpose(b_re, (0, 2, 1))
    bt_im = jnp.transpose(b_im, (0, 2, 1))
    return pl.pallas_call(
        _s5prep_kernel,
        out_shape=(
            jax.ShapeDtypeStruct((g, p), F32), jax.ShapeDtypeStruct((g, p), F32),
            jax.ShapeDtypeStruct((g, SSM_GROUP, p), F32), jax.ShapeDtypeStruct((g, SSM_GROUP, p), F32),
        ),
        name="s5prep",
    )(lam_re, lam_im, log_dt.reshape(g, 1), bt_re, bt_im)


def _block_diag(blocks):
    _, r, c = blocks.shape
    b4 = blocks.reshape(N_S5_BLOCKS, S5_BLOCK_GROUPS, r, c)
    eye = jnp.eye(S5_BLOCK_GROUPS, dtype=blocks.dtype)
    out = b4[:, :, :, None, :] * eye[None, :, None, :, None]
    return out.reshape(N_S5_BLOCKS, S5_BLOCK_GROUPS * r, S5_BLOCK_GROUPS * c)


def _mixer_kernel(x_ref, mod_ref, n1w_ref, win_ref, cos_ref, sin_ref, dmask_ref, cdec_ref, sdec_ref,
                  rnw_ref, bmat_ref, cre_ref, cim_ref, lbr_ref, lbi_ref, dsk_ref, wglu_ref, bglu_ref,
                  snw_ref, wout_ref, sret0_ref, sre0_ref, sim0_ref,
                  x1_ref, sret_ref, sre_ref, sim_ref,
                  hb_ref, z_ref, zu_ref, oy_ref, utb_ref, bur0_ref, bur1_ref, bui0_ref, bui1_ref, ytb_ref, yb_ref, modx_ref,
                  *, n_seq, chunk, tile_rows, carry, chunk_decay):
    rows = n_seq * chunk
    seq_per_tile = tile_rows // chunk
    n_tiles = rows // tile_rows
    per_row_mod = chunk % SUBLANES != 0
    dl = D_MODEL // LANES

    def load_states():
        sret_ref[...] = sret0_ref[...]
        sre_ref[...] = sre0_ref[...]
        sim_ref[...] = sim0_ref[...]

    if carry:
        pl.when(pl.program_id(0) == 0)(load_states)
    else:
        load_states()

    if per_row_mod:
        _expand_rows(modx_ref, mod_ref[...], chunk)

    n1w = n1w_ref[...]
    mod_rows = rows if per_row_mod else chunk
    for i in range(rows // mod_rows):
        r0 = i * mod_rows
        xb = _load_rows(x_ref, r0, mod_rows, chunk)
        if per_row_mod:
            sh, sc = _expanded(modx_ref, 0), _expanded(modx_ref, dl)
        else:
            sh = mod_ref[pl.ds(i, 1), pl.ds(0, D_MODEL)]
            sc = mod_ref[pl.ds(i, 1), pl.ds(D_MODEL, D_MODEL)]
        ms = jnp.mean(xb * xb, axis=-1, keepdims=True)
        hn = xb * lax.rsqrt(ms + NORM_EPS) * n1w
        hb_ref[pl.ds(r0, mod_rows), :] = (hn * (1.0 + sc) + sh).astype(BF16)
    ret_w = 4 * RET_WIDTH
    z_ref[...] = jnp.dot(hb_ref[...], win_ref[:, pl.ds(0, ret_w)], preferred_element_type=F32)
    zu = jnp.dot(hb_ref[...], win_ref[:, pl.ds(ret_w, SSM_WIDTH)], preferred_element_type=F32)
    pitch = zu_ref.shape[1] // n_seq
    for c in range(SSM_WIDTH // LANES):
        for b in range(n_seq if pitch != chunk else 1):
            nb = chunk if pitch != chunk else rows
            zu_ref[c, pl.ds(b * pitch, nb), :] = zu[b * chunk:b * chunk + nb, c * LANES:(c + 1) * LANES]

    cos = cos_ref[...]
    sin = sin_ref[...]
    scale = HEAD_DIM ** -0.5
    if seq_per_tile > 1:
        row_id = lax.broadcasted_iota(jnp.int32, (tile_rows, HEAD_DIM), 0)

    def rope(t):
        return t * cos + pltpu.roll(t, HEAD_DIM // 2, 1) * sin

    def ret_tile(ti, c):
        r0 = pl.multiple_of(ti * tile_rows, tile_rows)
        for h in range(RET_HEADS):
            c0 = h * HEAD_DIM
            q = rope(z_ref[pl.ds(r0, tile_rows), pl.ds(c0, HEAD_DIM)])
            k = rope(z_ref[pl.ds(r0, tile_rows), pl.ds(RET_WIDTH + c0, HEAD_DIM)]) * scale
            v = z_ref[pl.ds(r0, tile_rows), pl.ds(2 * RET_WIDTH + c0, HEAD_DIM)]
            g = z_ref[pl.ds(r0, tile_rows), pl.ds(3 * RET_WIDTH + c0, HEAD_DIM)]
            kd = k * sdec_ref[h]
            if tile_rows < HEAD_DIM:
                pad = jnp.zeros((HEAD_DIM - tile_rows, HEAD_DIM), F32)
                k, v, kd = (jnp.concatenate([t, pad], axis=0) for t in (k, v, kd))
                if seq_per_tile > 1:
                    row_kv = lax.broadcasted_iota(jnp.int32, (HEAD_DIM, HEAD_DIM), 0)
            elif seq_per_tile > 1:
                row_kv = row_id
            qb, kb, vb = q.astype(BF16), k.astype(BF16), v.astype(BF16)
            s = lax.dot_general(qb, kb, (((1,), (1,)), ((), ())), preferred_element_type=F32) * dmask_ref[h]
            o = jnp.dot(s.astype(BF16), vb, preferred_element_type=F32)
            cross = None
            for si in range(seq_per_tile):
                sidx = ti * seq_per_tile + si
                st = sret_ref[sidx, h]
                cr = jnp.dot(qb, st.astype(BF16), preferred_element_type=F32)
                if seq_per_tile > 1:
                    in_seq = (row_id >= si * chunk) & (row_id < (si + 1) * chunk)
                    cross = jnp.where(in_seq, cr, 0.0 if cross is None else cross)
                    kds = jnp.where((row_kv >= si * chunk) & (row_kv < (si + 1) * chunk), kd, 0.0)
                else:
                    cross, kds = cr, kd
                upd = lax.dot_general(kds.astype(BF16), vb, (((0,), (0,)), ((), ())), preferred_element_type=F32)
                sret_ref[sidx, h] = st * chunk_decay[h] + upd
            o = o + cross * cdec_ref[h]
            o = o * lax.rsqrt(jnp.mean(o * o, axis=-1, keepdims=True) + NORM_EPS)
            o = o * rnw_ref[:, pl.ds(c0, HEAD_DIM)] * _silu(g)
            oy_ref[pl.ds(r0, tile_rows), pl.ds(c0, HEAD_DIM)] = o
        return c

    lax.fori_loop(0, n_tiles, ret_tile, 0, unroll=True)

    for t in range(chunk):
        for c in range(SSM_WIDTH // LANES):
            utb_ref[pl.ds(t * n_seq, n_seq), pl.ds(c * LANES, LANES)] = zu_ref[c, pl.ds(t, n_seq, stride=pitch), :]
    half_ch = SSM_CH // 2
    blk_per_half = N_S5_BLOCKS // 2
    bur_refs, bui_refs = (bur0_ref, bur1_ref), (bui0_ref, bui1_ref)
    for blk in range(N_S5_BLOCKS):
        hf, lcols = blk // blk_per_half, pl.ds((blk % blk_per_half) * S5_BLOCK_CH, S5_BLOCK_CH)
        ub = utb_ref[:, pl.ds(blk * S5_BLOCK_IN, S5_BLOCK_IN)].astype(BF16)
        bu = jnp.dot(ub, bmat_ref[blk], preferred_element_type=F32)
        bur_refs[hf][:, lcols] = bu[:, :S5_BLOCK_CH]
        bui_refs[hf][:, lcols] = bu[:, S5_BLOCK_CH:]

    scan_w = min(half_ch, SCAN_ELEMS // n_seq)
    for hf in range(2):
        bur_ref, bui_ref = bur_refs[hf], bui_refs[hf]
        for p in range(half_ch // scan_w):
            cols = pl.ds(p * scan_w, scan_w)
            gcols = pl.ds(hf * half_ch + p * scan_w, scan_w)
            lbr = jnp.broadcast_to(lbr_ref[:, gcols], (n_seq, scan_w))
            lbi = jnp.broadcast_to(lbi_ref[:, gcols], (n_seq, scan_w))
            hr, hi = sre_ref[:, gcols], sim_ref[:, gcols]
            for t in range(chunk):
                rws = pl.ds(t * n_seq, n_seq)
                hr, hi = (lbr * hr - lbi * hi + bur_ref[rws, cols], lbr * hi + lbi * hr + bui_ref[rws, cols])
                bur_ref[rws, cols] = hr
                bui_ref[rws, cols] = hi
            sre_ref[:, gcols] = hr
            sim_ref[:, gcols] = hi

    for blk in range(N_S5_BLOCKS):
        hf, lcols = blk // blk_per_half, pl.ds((blk % blk_per_half) * S5_BLOCK_CH, S5_BLOCK_CH)
        yb = jnp.dot(bur_refs[hf][:, lcols].astype(BF16), cre_ref[blk], preferred_element_type=F32)
        yb = yb + jnp.dot(bui_refs[hf][:, lcols].astype(BF16), cim_ref[blk], preferred_element_type=F32)
        ucols = pl.ds(blk * S5_BLOCK_IN, S5_BLOCK_IN)
        ytb_ref[:, ucols] = yb + dsk_ref[:, ucols] * utb_ref[:, ucols]
    for t in range(chunk):
        for c in range(SSM_WIDTH // LANES):
            yb_ref[c, pl.ds(t, n_seq, stride=pitch), :] = ytb_ref[pl.ds(t * n_seq, n_seq), pl.ds(c * LANES, LANES)]

    def seq_major(c):
        if pitch == chunk:
            return yb_ref[c]
        return jnp.concatenate([yb_ref[c, pl.ds(b * pitch, chunk), :] for b in range(n_seq)], axis=0)

    y = jnp.concatenate([seq_major(c) for c in range(SSM_WIDTH // LANES)], axis=1)
    y = jax.nn.gelu(y, approximate=True)
    gate = jnp.dot(y.astype(BF16), wglu_ref[...], preferred_element_type=F32) + bglu_ref[...]
    y = y * jax.nn.sigmoid(gate)
    y = y * lax.rsqrt(jnp.mean(y * y, axis=-1, keepdims=True) + NORM_EPS) * snw_ref[...]
    oy_ref[:, pl.ds(RET_WIDTH, SSM_WIDTH)] = y

    mix = jnp.dot(oy_ref[...].astype(BF16), wout_ref[...], preferred_element_type=F32)
    for i in range(rows // mod_rows):
        r0 = i * mod_rows
        if per_row_mod:
            g1 = _expanded(modx_ref, 2 * dl)
        else:
            g1 = mod_ref[pl.ds(i, 1), pl.ds(2 * D_MODEL, D_MODEL)]
        _store_rows(x1_ref, r0, mod_rows, chunk,
                    _load_rows(x_ref, r0, mod_rows, chunk) + g1 * mix[r0:r0 + mod_rows])


def _load_rows(ref, r0, n, chunk):
    if len(ref.shape) == 2:
        return ref[pl.ds(r0, n), :]
    assert n == chunk and r0 % chunk == 0
    return ref[r0 // chunk]


def _store_rows(ref, r0, n, chunk, val):
    if len(ref.shape) == 2:
        ref[pl.ds(r0, n), :] = val
    else:
        assert n == chunk and r0 % chunk == 0
        ref[r0 // chunk] = val


def _seq_pitch(chunk):
    return chunk + SUBLANES if chunk % SUBLANES == 0 else chunk


def _const_spec(shape):
    nd = len(shape)
    return pl.BlockSpec(shape, lambda j, _n=nd: (0,) * _n)


def _decay_tables(chunk, tile_rows):
    f32 = np.float32
    log_gamma = np.log1p(-np.exp2(f32(-5.0) - np.arange(RET_HEADS, dtype=f32))).astype(f32)
    r = np.arange(tile_rows)
    seq, loc = r // chunk, (r % chunk).astype(f32)
    rel = loc[:, None] - loc[None, :]
    ok = (seq[:, None] == seq[None, :]) & (rel >= 0)
    dmask = np.where(ok[None], np.exp(np.where(ok, rel, f32(0.0))[None] * log_gamma[:, None, None]), f32(0.0))
    if tile_rows < HEAD_DIM:
        dmask = np.pad(dmask, ((0, 0), (0, 0), (0, HEAD_DIM - tile_rows)))
    cdec = np.exp((loc[None, :] + f32(1.0)) * log_gamma[:, None])
    sdec = np.exp((f32(chunk) - f32(1.0) - loc)[None, :] * log_gamma[:, None])
    bcast = lambda t: np.ascontiguousarray(np.broadcast_to(t[:, :, None], (RET_HEADS, tile_rows, HEAD_DIM)))
    return dmask.astype(f32), bcast(cdec.astype(f32)), bcast(sdec.astype(f32))


def _rope_tables(pos):
    f32 = np.float32
    half = HEAD_DIM // 2
    inv_freq = (f32(ROPE_BASE) ** (-np.arange(half, dtype=f32) / f32(half))).astype(f32)
    ang = (pos.astype(f32)[:, None] * inv_freq[None, :]).astype(f32)
    cos, sin = np.cos(ang).astype(f32), np.sin(ang).astype(f32)
    return np.concatenate([cos, cos], axis=-1), np.concatenate([-sin, sin], axis=-1)


def _mixer(x, mod, pos, states, wts, *, prompt):
    n_seq = SEQ_PER_BLOCK if prompt else SAMPLE_SEQ_PER_BLOCK
    if prompt:
        n_total, seq_len, _ = x.shape
        assert n_total == n_seq
        chunk, tile_rows, n_steps = PROMPT_CHUNK, PROMPT_CHUNK, seq_len // PROMPT_CHUNK
        x_spec = pl.BlockSpec((n_seq, chunk, D_MODEL), lambda j: (0, j, 0))
        mod_spec = pl.BlockSpec((n_seq, 3 * D_MODEL), lambda j: (0, 0))
        tab_spec = pl.BlockSpec((chunk, HEAD_DIM), lambda j: (j, 0))
        seq_map = lambda j: 0
    else:
        chunk = pos.shape[0]
        tile_rows = SUBLANES
        n_total = x.shape[0] // chunk
        n_steps = n_total // n_seq
        x_spec = pl.BlockSpec((n_seq * chunk, D_MODEL), lambda j: (j, 0))
        mod_spec = pl.BlockSpec((n_seq, 3 * D_MODEL), lambda j: (j, 0))
        tab_spec = _const_spec((tile_rows, HEAD_DIM))
        seq_map = lambda j: j
    rows = n_seq * chunk
    cos, sin = _rope_tables(pos)
    if not prompt:
        reps = tile_rows // chunk
        cos, sin = np.tile(cos, (reps, 1)), np.tile(sin, (reps, 1))
    dmask, cdec, sdec = _decay_tables(chunk, tile_rows)
    chunk_decay = tuple(float(math.exp(chunk * math.log1p(-2.0 ** (-5.0 - h)))) for h in range(RET_HEADS))
    sret0, sre0, sim0 = states

    st_ret_spec = pl.BlockSpec((n_seq, RET_HEADS, HEAD_DIM, HEAD_DIM), lambda j: (seq_map(j), 0, 0, 0))
    st_s5_spec = pl.BlockSpec((n_seq, SSM_CH), lambda j: (seq_map(j), 0))
    consts = [dmask, cdec, sdec, wts["rnw"], wts["bmat"], wts["cre"], wts["cim"], wts["lbr"], wts["lbi"],
              wts["dsk"], wts["w_glu"], wts["b_glu"], wts["snw"], wts["w_out"]]
    args = [x, mod, wts["n1w"], wts["w_in"], cos, sin] + consts + [sret0, sre0, sim0]
    in_specs = ([x_spec, mod_spec, _const_spec(wts["n1w"].shape), _const_spec(wts["w_in"].shape), tab_spec, tab_spec]
                + [_const_spec(a.shape) for a in consts] + [st_ret_spec, st_s5_spec, st_s5_spec])

    kern = functools.partial(_mixer_kernel, n_seq=n_seq, chunk=chunk, tile_rows=tile_rows, carry=prompt,
                             chunk_decay=chunk_decay)
    out_shape = (
        jax.ShapeDtypeStruct(x.shape, F32),
        jax.ShapeDtypeStruct((n_total, RET_HEADS, HEAD_DIM, HEAD_DIM), F32),
        jax.ShapeDtypeStruct((n_total, SSM_CH), F32),
        jax.ShapeDtypeStruct((n_total, SSM_CH), F32),
    )
    scratch = [
        pltpu.VMEM((rows, D_MODEL), BF16),
        pltpu.VMEM((rows, 4 * RET_WIDTH), F32),
        pltpu.VMEM((SSM_WIDTH // LANES, n_seq * _seq_pitch(chunk), LANES), F32),
        pltpu.VMEM((rows, D_MODEL), F32),
        pltpu.VMEM((rows, SSM_WIDTH), F32),
        pltpu.VMEM((rows, SSM_CH // 2), F32),
        pltpu.VMEM((rows, SSM_CH // 2), F32),
        pltpu.VMEM((rows, SSM_CH // 2), F32),
        pltpu.VMEM((rows, SSM_CH // 2), F32),
        pltpu.VMEM((rows, SSM_WIDTH), F32),
        pltpu.VMEM((SSM_WIDTH // LANES, n_seq * _seq_pitch(chunk), LANES), F32),
        pltpu.VMEM((3 * D_MODEL // LANES, SUBLANES if prompt else rows, LANES), F32),
    ]
    return pl.pallas_call(
        kern,
        grid=(n_steps,),
        in_specs=in_specs,
        out_specs=(x_spec, st_ret_spec, st_s5_spec, st_s5_spec),
        out_shape=out_shape,
        scratch_shapes=scratch,
        compiler_params=pltpu.CompilerParams(dimension_semantics=("arbitrary",), vmem_limit_bytes=VMEM_LIMIT),
        name="mixer_prompt" if prompt else "mixer_sample",
    )(*args)


PACK_ROWS = D_MODEL // (2 * LANES)


def _store_packed(ref, x):
    half = D_MODEL // 2
    bits = lax.bitcast_convert_type(x.astype(BF16).astype(F32), jnp.uint32)
    words = bits[:, :half] | (bits[:, half:] >> 16)
    for c in range(PACK_ROWS):
        ref[c] = words[:, c * LANES:(c + 1) * LANES]


def _load_packed(ref, n, first_row=0, row_stride=1):
    hi, lo = [], []
    for c in range(PACK_ROWS):
        w = ref[c] if row_stride == 1 else ref[c, pl.ds(first_row, n, stride=row_stride), :]
        hi.append(lax.bitcast_convert_type(w & jnp.uint32(0xFFFF0000), F32))
        lo.append(lax.bitcast_convert_type(w << 16, F32))
    return jnp.concatenate(hi + lo, axis=1)


def _split_bf16(x):
    hi = x.astype(BF16)
    return hi, (x - hi.astype(F32)).astype(BF16)


def _route_tile(x, sh, sc, n2w_ref, wrh_ref, wrm_ref, br_ref, ltri_ref, count_ref, h2_ref, route_ref, topw_ref):
    ms = jnp.mean(x * x, axis=-1, keepdims=True)
    h2 = x * lax.rsqrt(ms + NORM_EPS) * n2w_ref[...] * (1.0 + sc) + sh
    _store_packed(h2_ref, h2)
    hh, hm = _split_bf16(h2)
    logits = (jnp.dot(hh, wrh_ref[...], preferred_element_type=F32)
              + (jnp.dot(hh, wrm_ref[...], preferred_element_type=F32)
                 + jnp.dot(hm, wrh_ref[...], preferred_element_type=F32))) + br_ref[...]
    lane = lax.broadcasted_iota(jnp.int32, logits.shape, 1)
    lane_f = lane.astype(F32)
    work = logits
    vals, idxs = [], []
    for _ in range(TOP_K):
        m = jnp.max(work, axis=-1, keepdims=True)
        idx = jnp.min(jnp.where(work == m, lane_f, float(LANES)), axis=-1, keepdims=True)
        vals.append(m)
        idxs.append(idx)
        work = jnp.where(lane_f == idx, -jnp.inf, work)
    exps = [jnp.exp(v - vals[0]) for v in vals]
    tot = exps[0] + exps[1] + exps[2] + exps[3]
    topw = jnp.zeros(logits.shape, F32)
    for k in range(TOP_K):
        topw = jnp.where(lane == k, exps[k] / tot, topw)
    topw_ref[...] = topw

    onehot = [(lane_f == idxs[k]).astype(F32) for k in range(TOP_K)]
    chosen = onehot[0] + onehot[1] + onehot[2] + onehot[3]
    before = jnp.dot(ltri_ref[...], chosen.astype(BF16), preferred_element_type=F32) + count_ref[...]
    info = jnp.zeros(logits.shape, jnp.int32)
    for k in range(TOP_K):
        rank = jnp.sum(onehot[k] * before, axis=-1, keepdims=True).astype(jnp.int32)
        info = jnp.where(lane == k, idxs[k].astype(jnp.int32), info)
        info = jnp.where(lane == TOP_K + k, rank, info)
    route_ref[...] = jnp.transpose(info)[:2 * TOP_K, :]
    count_ref[...] = count_ref[...] + jnp.sum(chosen, axis=0, keepdims=True)


def _router_kernel(xp_ref, shp_ref, scp_ref, xs_ref, shs_ref, scs_ref, n2w_ref, wrh_ref, wrm_ref, br_ref, ltri_ref,
                   h2_ref, route_ref, topw_ref, count_ref, modx_ref, *, n_prompt_tiles, sample_len):
    i = pl.program_id(0)
    rest = (n2w_ref, wrh_ref, wrm_ref, br_ref, ltri_ref, count_ref, h2_ref, route_ref, topw_ref)

    @pl.when(i == 0)
    def _():
        count_ref[...] = jnp.zeros(count_ref.shape, F32)

    @pl.when(i < n_prompt_tiles)
    def _():
        _route_tile(xp_ref[...], shp_ref[0], scp_ref[0], *rest)

    @pl.when(i >= n_prompt_tiles)
    def _():
        dl = D_MODEL // LANES
        _expand_rows(modx_ref, shs_ref[...], sample_len)
        _expand_rows(modx_ref, scs_ref[...], sample_len, first_chunk=dl)
        _route_tile(xs_ref[...], _expanded(modx_ref, 0), _expanded(modx_ref, dl), *rest)


def _router(xp_rows, mod_p, seq_len, xs_rows, mod_s, sample_len, n2w, w_router, b_router):
    tile = ROUTER_TILE
    n_p, n_s = xp_rows.shape[0], xs_rows.shape[0]
    tp, ts = n_p // tile, n_s // tile
    n_total = n_p + n_s
    mod_p = mod_p.reshape(mod_p.shape[0], 1, mod_p.shape[1])
    seq_of = lambda i: (jnp.minimum(i, tp - 1) * tile) // seq_len
    wr_pad = jnp.pad(w_router, ((0, 0), (0, LANES - N_EXPERTS)))
    wr_hi = wr_pad.astype(BF16)
    wr_mid = (wr_pad - wr_hi.astype(F32)).astype(BF16)
    br_pad = jnp.pad(b_router, ((0, 0), (0, LANES - N_EXPERTS)), constant_values=-1e30)
    ltri = jnp.asarray(np.tril(np.ones((tile, tile), np.float32), -1), BF16)
    clamp_p = lambda i: jnp.minimum(i, tp - 1)
    clamp_s = lambda i: jnp.maximum(i - tp, 0)
    return pl.pallas_call(
        functools.partial(_router_kernel, n_prompt_tiles=tp, sample_len=sample_len),
        grid=(tp + ts,),
        in_specs=[pl.BlockSpec((tile, D_MODEL), lambda i: (clamp_p(i), 0)),
                  pl.BlockSpec((1, 1, D_MODEL), lambda i: (seq_of(i), 0, 3)),
                  pl.BlockSpec((1, 1, D_MODEL), lambda i: (seq_of(i), 0, 4)),
                  pl.BlockSpec((tile, D_MODEL), lambda i: (clamp_s(i), 0)),
                  pl.BlockSpec((tile // sample_len, D_MODEL), lambda i: (clamp_s(i), 3)),
                  pl.BlockSpec((tile // sample_len, D_MODEL), lambda i: (clamp_s(i), 4)),
                  _const_spec(n2w.shape), _const_spec(wr_hi.shape), _const_spec(wr_mid.shape),
                  _const_spec(br_pad.shape), _const_spec(ltri.shape)],
        out_specs=(pl.BlockSpec((PACK_ROWS, tile, LANES), lambda i: (0, i, 0)),
                   pl.BlockSpec((2 * TOP_K, tile), lambda i: (0, i)),
                   pl.BlockSpec((tile, LANES), lambda i: (i, 0)),
                   pl.BlockSpec((1, LANES), lambda i: (0, 0))),
        out_shape=(jax.ShapeDtypeStruct((PACK_ROWS, n_total, LANES), jnp.uint32),
                   jax.ShapeDtypeStruct((2 * TOP_K, n_total), jnp.int32),
                   jax.ShapeDtypeStruct((n_total, LANES), F32),
                   jax.ShapeDtypeStruct((1, LANES), F32)),
        scratch_shapes=[pltpu.VMEM((2 * D_MODEL // LANES, tile, LANES), F32)],
        compiler_params=pltpu.CompilerParams(dimension_semantics=("arbitrary",), vmem_limit_bytes=VMEM_LIMIT),
        name="router",
    )(xp_rows, mod_p, mod_p, xs_rows, mod_s, mod_s, n2w, wr_hi, wr_mid, br_pad, ltri)


def _gather_rows(table, idx):
    n = idx.shape[0]
    steps = n // SC_GATHER_WINDOW
    assert n % SC_GATHER_WINDOW == 0 and steps % SC_WORKERS == 0
    mesh = plsc.VectorSubcoreMesh(core_axis_name="c", subcore_axis_name="s")

    @functools.partial(pl.kernel, out_type=jax.ShapeDtypeStruct((n, table.shape[1]), table.dtype), mesh=mesh,
                       scratch_types=[])
    def gather_kernel(table_hbm, idx_hbm, out_hbm):
        def body(idx_vmem, out_vmem):
            pltpu.sync_copy(table_hbm.at[idx_vmem.at[0]], out_vmem)

        pltpu.emit_pipeline(
            body,
            grid=(steps,),
            in_specs=[pl.BlockSpec((1, SC_GATHER_WINDOW), lambda i: (0, i))],
            out_specs=[pl.BlockSpec((SC_GATHER_WINDOW, table.shape[1]), lambda i: (i, 0))],
            core_axis_name=("c", "s"),
            dimension_semantics=(pltpu.PARALLEL,),
        )(idx_hbm, out_hbm)

    return gather_kernel(table, idx.reshape(1, n))


def _dispatch_packed(table, slot_kt, n_slots):
    planes, n_tok, lanes = table.shape
    win = SC_GATHER_WINDOW
    blocks = n_tok // win
    assert n_tok % win == 0
    blocks_pad = _round_up(blocks, SC_WORKERS // math.gcd(SC_WORKERS, planes))
    n_spare = (blocks_pad - blocks) * TOP_K * win
    n_ext = n_slots + n_spare
    dest = jnp.transpose(slot_kt.reshape(TOP_K, blocks, win), (1, 0, 2))
    spare = n_slots + jnp.arange(n_spare, dtype=jnp.int32).reshape(blocks_pad - blocks, TOP_K, win)
    dest = jnp.concatenate([dest, spare], axis=0)[None] + (jnp.arange(planes, dtype=jnp.int32) * n_ext)[:, None, None, None]
    dest = dest.reshape(planes * blocks_pad * TOP_K, win)
    steps = planes * blocks_pad
    src_block = lambda g: (g // blocks_pad) * blocks + jnp.minimum(g % blocks_pad, blocks - 1)
    mesh = plsc.VectorSubcoreMesh(core_axis_name="c", subcore_axis_name="s")

    @functools.partial(pl.kernel, out_type=jax.ShapeDtypeStruct((planes * n_ext, lanes), table.dtype), mesh=mesh,
                       scratch_types=[])
    def scatter_kernel(table_hbm, dest_hbm, out_hbm):
        def body(rows_vmem, dest_vmem):
            for k in range(TOP_K):
                pltpu.sync_copy(rows_vmem, out_hbm.at[dest_vmem.at[k]])

        pltpu.emit_pipeline(
            body,
            grid=(steps,),
            in_specs=[pl.BlockSpec((win, lanes), lambda g: (src_block(g), 0)),
                      pl.BlockSpec((TOP_K, win), lambda g: (g, 0))],
            out_specs=[],
            core_axis_name=("c", "s"),
            dimension_semantics=(pltpu.PARALLEL,),
        )(table_hbm, dest_hbm)

    return scatter_kernel(table.reshape(planes * n_tok, lanes), dest).reshape(planes, n_ext, lanes)


def _gather_packed(table, rows):
    planes, n_table, lanes = table.shape
    idx = jnp.concatenate([rows + c * n_table for c in range(planes)])
    out = _gather_rows(table.reshape(planes * n_table, lanes), idx)
    return out.reshape(planes, rows.shape[0], lanes)


def _expert_weight_copies(e, w1_hbm, w2_hbm, w1s_ref, w2s_ref, sem):
    return (pltpu.make_async_copy(w1_hbm.at[e], w1s_ref, sem.at[0]),
            pltpu.make_async_copy(w2_hbm.at[e], w2s_ref, sem.at[1]))


def _experts_kernel(te_ref, tr_ref, nx_ref, nv_ref, xs_ref, w1_hbm, b1_ref, w2_hbm, b2_ref, ys_ref,
                    w1s_ref, w2s_ref, w1b_ref, w2b_ref, sem):
    i = pl.program_id(0)
    e = te_ref[i]
    new_expert = (i == 0) | (e != te_ref[jnp.maximum(i - 1, 0)])
    copies = functools.partial(_expert_weight_copies, w1_hbm=w1_hbm, w2_hbm=w2_hbm, w1s_ref=w1s_ref,
                               w2s_ref=w2s_ref, sem=sem)

    @pl.when(i == 0)
    def _():
        for c in copies(e):
            c.start()

    @pl.when(new_expert)
    def _():
        for c in copies(e):
            c.wait()
        w1b_ref[...] = w1s_ref[...].astype(BF16)
        w2b_ref[...] = w2s_ref[...].astype(BF16)

        @pl.when(nx_ref[i] >= 0)
        def _():
            for c in copies(nx_ref[i]):
                c.start()

    def expert_pass(r0, n_rows, rows_valid):
        xs_v, ys_v = xs_ref.at[:, pl.ds(r0, n_rows)], ys_ref.at[:, pl.ds(r0, n_rows)]
        row = lax.broadcasted_iota(jnp.int32, (n_rows, D_MODEL), 0)
        x = jnp.where(row < rows_valid, _load_packed(xs_v, n_rows), 0.0).astype(BF16)
        hu = jnp.dot(x, w1b_ref[...], preferred_element_type=F32) + b1_ref[0]
        x_glu = jnp.minimum(hu[:, :D_FF], SWIGLU_LIMIT)
        x_lin = jnp.clip(hu[:, D_FF:], -SWIGLU_LIMIT, SWIGLU_LIMIT)
        act = x_glu * jax.nn.sigmoid(SWIGLU_ALPHA * x_glu) * (x_lin + 1.0)
        _store_packed(ys_v, jnp.dot(act.astype(BF16), w2b_ref[...], preferred_element_type=F32) + b2_ref[0])

    def zero_rows(r0, n_rows):
        for c in range(PACK_ROWS):
            ys_ref[c, pl.ds(r0, n_rows), :] = jnp.zeros((n_rows, LANES), jnp.uint32)

    half = EXPERT_ROWS // 2
    rows_tile = jnp.where(i < nv_ref[0], tr_ref[i], 0)
    full_tile = rows_tile > SLOT_TILE - half

    @pl.when(full_tile)
    def _():
        expert_pass(0, SLOT_TILE, rows_tile)

    for h in range(SLOT_TILE // EXPERT_ROWS):
        r0 = h * EXPERT_ROWS
        rows_here = jnp.where(full_tile, -1, rows_tile - r0)

        @pl.when(rows_here > half)
        def _(r0=r0, rows_here=rows_here):
            expert_pass(r0, EXPERT_ROWS, rows_here)

        @pl.when((rows_here > 0) & (rows_here <= half))
        def _(r0=r0, rows_here=rows_here):
            expert_pass(r0, half, rows_here)
            zero_rows(r0 + half, half)

        @pl.when((rows_here <= 0) & jnp.logical_not(full_tile))
        def _(r0=r0):
            zero_rows(r0, EXPERT_ROWS)


def _experts(tile_expert, tile_rows, next_expert, n_valid, xs, w1, b1, w2, b2):
    n_tiles = tile_expert.shape[0]
    n_slots = n_tiles * SLOT_TILE
    grid_spec = pltpu.PrefetchScalarGridSpec(
        num_scalar_prefetch=4,
        grid=(n_tiles,),
        in_specs=[
            pl.BlockSpec((PACK_ROWS, SLOT_TILE, LANES), lambda i, te, tr, nx, nv: (0, i, 0)),
            pl.BlockSpec(memory_space=pl.ANY),
            pl.BlockSpec((1, 1, 2 * D_FF), lambda i, te, tr, nx, nv: (te[i], 0, 0)),
            pl.BlockSpec(memory_space=pl.ANY),
            pl.BlockSpec((1, 1, D_MODEL), lambda i, te, tr, nx, nv: (te[i], 0, 0)),
        ],
        out_specs=pl.BlockSpec((PACK_ROWS, SLOT_TILE, LANES), lambda i, te, tr, nx, nv: (0, i, 0)),
        scratch_shapes=[pltpu.VMEM((D_MODEL, 2 * D_FF), F32), pltpu.VMEM((D_FF, D_MODEL), F32),
                        pltpu.VMEM((D_MODEL, 2 * D_FF), BF16), pltpu.VMEM((D_FF, D_MODEL), BF16),
                        pltpu.SemaphoreType.DMA((2,))],
    )
    return pl.pallas_call(
        _experts_kernel,
        grid_spec=grid_spec,
        out_shape=jax.ShapeDtypeStruct((PACK_ROWS, n_slots, LANES), jnp.uint32),
        compiler_params=pltpu.CompilerParams(dimension_semantics=("arbitrary",), vmem_limit_bytes=VMEM_LIMIT),
        name="experts",
    )(tile_expert, tile_rows, next_expert, n_valid, xs, w1, b1.reshape(N_EXPERTS, 1, 2 * D_FF), w2,
      b2.reshape(N_EXPERTS, 1, D_MODEL))


def _combine_kernel(y4_ref, x1_ref, topw_ref, g2_ref, fw_ref, *rest, reps):
    o_ref = rest[-2] if reps else rest[-1]
    w = topw_ref[...]
    n = w.shape[0]
    ff = None
    for k in range(TOP_K):
        yk = w[:, k:k + 1] * _load_packed(y4_ref.at[:, k], n)
        ff = yk if ff is None else ff + yk
    if reps:
        _expand_rows(rest[-1], g2_ref[...], reps)
        g2 = _expanded(rest[-1], 0)
    else:
        g2 = g2_ref[0]
    x = x1_ref[...] + g2 * ff
    ms = jnp.mean(x * x, axis=-1, keepdims=True)
    o_ref[...] = x * lax.rsqrt(ms + NORM_EPS) * fw_ref[...]


def _combine(y4, y4_row0, x1, x1_row0, n_rows, topw, topw_row0, mod, rows_per_mod, fw, out_buf):
    tile = COMBINE_TILE
    y4_off, x1_off, tw_off = y4_row0 // tile, x1_row0 // tile, topw_row0 // tile
    reps = rows_per_mod if rows_per_mod < tile else 0
    scratch = []
    if reps:
        g2_spec = pl.BlockSpec((tile // reps, D_MODEL), lambda i: (i + x1_off, 5))
        scratch = [pltpu.VMEM((D_MODEL // LANES, tile, LANES), F32)]
    else:
        mod = mod.reshape(mod.shape[0], 1, mod.shape[1])
        g2_spec = pl.BlockSpec((1, 1, D_MODEL), lambda i: (((i + x1_off) * tile) // rows_per_mod, 0, 5))
    in_specs = [pl.BlockSpec((PACK_ROWS, TOP_K, tile, LANES), lambda i: (0, 0, i + y4_off, 0)),
                pl.BlockSpec((tile, D_MODEL), lambda i: (i + x1_off, 0)),
                pl.BlockSpec((tile, LANES), lambda i: (i + tw_off, 0)),
                g2_spec, _const_spec(fw.shape)]
    args = [y4, x1, topw, mod, fw]
    aliases = {}
    if out_buf is not None:
        in_specs.append(pl.BlockSpec(memory_space=pl.ANY))
        args.append(out_buf)
        aliases = {len(args) - 1: 0}
    return pl.pallas_call(
        functools.partial(_combine_kernel, reps=reps),
        grid=(n_rows // tile,),
        in_specs=in_specs,
        out_specs=pl.BlockSpec((tile, D_MODEL), lambda i: (i + x1_off, 0)),
        out_shape=jax.ShapeDtypeStruct(x1.shape, F32),
        input_output_aliases=aliases,
        scratch_shapes=scratch,
        compiler_params=pltpu.CompilerParams(dimension_semantics=("arbitrary",), vmem_limit_bytes=VMEM_LIMIT),
        name="combine",
    )(*args)


def _routing_tables(route, counts, n_slots):
    padded = ((counts + SLOT_TILE - 1) // SLOT_TILE) * SLOT_TILE
    pend = jnp.cumsum(padded)
    poff = pend - padded
    expert_kt, rank_kt = route[:TOP_K], route[TOP_K:]
    experts = jnp.arange(N_EXPERTS, dtype=jnp.int32)
    start_kt = jnp.sum((expert_kt[None] == experts[:, None, None]).astype(jnp.int32) * poff[:, None, None], axis=0)
    slot_kt = start_kt + rank_kt
    n_tiles = n_slots // SLOT_TILE
    n_valid = (pend[-1] // SLOT_TILE).astype(jnp.int32)
    tile_row = jnp.minimum(jnp.arange(n_tiles, dtype=jnp.int32), n_valid - 1) * SLOT_TILE
    in_later = (pend[None, :] <= tile_row[:, None]).astype(jnp.int32)
    tile_e = jnp.sum(in_later, axis=1).astype(jnp.int32)
    is_e = (experts[None, :] == tile_e[:, None]).astype(jnp.int32)
    used_end = jnp.sum(is_e * (poff + counts)[None, :], axis=1)
    tile_rows = jnp.clip(used_end - tile_row, 0, SLOT_TILE).astype(jnp.int32)
    later_used = (experts[None, :] > tile_e[:, None]) & (counts[None, :] > 0)
    next_e = jnp.min(jnp.where(later_used, experts[None, :], N_EXPERTS), axis=1)
    next_e = jnp.where(next_e < N_EXPERTS, next_e, -1).astype(jnp.int32)
    return slot_kt, tile_e, tile_rows, next_e, n_valid.reshape(1)


def _round_up(n, m):
    return ((n + m - 1) // m) * m


def kernel(x_prompt, x_sample, c_prompt, c_sample, state_ret, state_s5_re, state_s5_im, norm1_w, norm2_w, w_ada, b_ada, w_in, ret_norm_w, s5_lam_re, s5_lam_im, s5_log_dt, s5_b_re, s5_b_im, s5_c_re, s5_c_im, s5_d, w_glu, b_glu, s5_norm_w, w_out, w_router, b_router, w1, b1, w2, b2, final_w):
    bp, lp, _ = x_prompt.shape
    bs, ls, _ = x_sample.shape
    assert norm1_w.shape[0] == 1, "single-layer model"
    n_p, n_s = bp * lp, bs * ls
    n_tok = n_p + n_s

    mod = _ada(jnp.concatenate([c_prompt, c_sample], axis=0), w_ada[0], b_ada[0])
    mod_p, mod_s = mod[:bp], mod[bp:]

    lbr, lbi, bbr, bbi = _s5prep(s5_lam_re[0], s5_lam_im[0], s5_log_dt[0], s5_b_re[0], s5_b_im[0])
    bmat = jnp.concatenate([_block_diag(bbr), _block_diag(bbi)], axis=-1).astype(BF16)
    cre = _block_diag(jnp.transpose(s5_c_re[0], (0, 2, 1))).astype(BF16)
    cim = _block_diag(jnp.transpose(-s5_c_im[0], (0, 2, 1))).astype(BF16)
    wts = dict(
        n1w=norm1_w, w_in=w_in[0].astype(BF16), rnw=ret_norm_w, bmat=bmat, cre=cre, cim=cim,
        lbr=lbr.reshape(1, SSM_CH), lbi=lbi.reshape(1, SSM_CH), dsk=s5_d[0].reshape(1, SSM_WIDTH),
        w_glu=w_glu[0].astype(BF16), b_glu=b_glu, snw=s5_norm_w, w_out=w_out[0].astype(BF16),
    )

    zero_states = (jnp.zeros((bp, RET_HEADS, HEAD_DIM, HEAD_DIM), F32), jnp.zeros((bp, SSM_CH), F32),
                   jnp.zeros((bp, SSM_CH), F32))
    x1_p, ret_p, re_p, im_p = _mixer(x_prompt, mod_p, np.arange(lp, dtype=np.float32), zero_states, wts,
                                     prompt=True)
    sample_states = (state_ret[0], state_s5_re[0].reshape(bs, SSM_CH), state_s5_im[0].reshape(bs, SSM_CH))
    x1_s, ret_s, re_s, im_s = _mixer(x_sample.reshape(n_s, D_MODEL), mod_s,
                                     PAST_LEN + np.arange(ls, dtype=np.float32), sample_states, wts, prompt=False)

    x1_p_rows = x1_p.reshape(n_p, D_MODEL)
    h2, route, topw, counts = _router(x1_p_rows, mod_p, lp, x1_s, mod_s, ls, norm2_w, w_router[0], b_router)

    n_assign = n_tok * TOP_K
    gather_quantum = SC_GATHER_WINDOW * SC_WORKERS // PACK_ROWS
    assert n_assign % gather_quantum == 0
    n_slots = _round_up(_round_up(n_assign, SLOT_TILE) + N_EXPERTS * SLOT_TILE, gather_quantum)
    slot_kt, tile_e, tile_rows, next_e, n_valid = _routing_tables(route, counts[0, :N_EXPERTS].astype(jnp.int32),
                                                                  n_slots)
    xs = _dispatch_packed(h2, slot_kt, n_slots)
    ys = _experts(tile_e, tile_rows, next_e, n_valid, xs, w1[0], b1[0], w2[0], b2[0])
    fw = final_w.reshape(1, D_MODEL)
    y_p, y_s = None, None
    bounds = [(n_p * f) // 8 for f in COMBINE_SPLITS] + [n_tok]
    for lo, hi in zip(bounds[:-1], bounds[1:]):
        y4 = _gather_packed(ys, slot_kt[:, lo:hi].reshape(-1)).reshape(PACK_ROWS, TOP_K, hi - lo, LANES)
        y_p = _combine(y4, 0, x1_p_rows, lo, min(hi, n_p) - lo, topw, lo, mod_p, lp, fw, y_p)
        if hi > n_p:
            y_s = _combine(y4, n_p - lo, x1_s, 0, n_s, topw, n_p, mod_s, ls, fw, None)

    g, p = SSM_GROUPS, SSM_STATE
    return (y_p.reshape(bp, lp, D_MODEL), y_s.reshape(bs, ls, D_MODEL),
            ret_p[None], re_p.reshape(1, bp, g, p), im_p.reshape(1, bp, g, p),
            ret_s[None], re_s.reshape(1, bs, g, p), im_s.reshape(1, bs, g, p))
```

```python
import functools
import math

import jax
import jax.numpy as jnp
import numpy as np
from jax import lax
from jax.experimental import pallas as pl
from jax.experimental.pallas import tpu as pltpu
from jax.experimental.pallas import tpu_sc as plsc

F32 = jnp.float32
BF16 = jnp.bfloat16

D_MODEL = 1024
PAST_LEN = 16384
RET_WIDTH = 512
RET_HEADS = 4
HEAD_DIM = 128
ROPE_BASE = 10000.0
SSM_WIDTH = 512
SSM_GROUP = 16
SSM_GROUPS = 32
SSM_STATE = 64
SSM_CH = SSM_GROUPS * SSM_STATE
N_EXPERTS = 32
TOP_K = 4
D_FF = 1024
SWIGLU_LIMIT = 7.0
SWIGLU_ALPHA = 1.702
NORM_EPS = 1e-6

LANES = 128
SUBLANES = 8
VMEM_LIMIT = 56 * 1024 * 1024

SEQ_PER_BLOCK = 8
SAMPLE_SEQ_PER_BLOCK = 16
SCAN_ELEMS = 8 * 1024
PROMPT_CHUNK = 64
S5_BLOCK_GROUPS = 8
N_S5_BLOCKS = SSM_GROUPS // S5_BLOCK_GROUPS
S5_BLOCK_IN = S5_BLOCK_GROUPS * SSM_GROUP
S5_BLOCK_CH = S5_BLOCK_GROUPS * SSM_STATE
ROUTER_TILE = 512
SLOT_TILE = 512
EXPERT_ROWS = 256
COMBINE_TILE = 512
COMBINE_RANGES = 2
SC_GATHER_WINDOW = 128
SC_WORKERS = 32


def _sigmoid(z):
    return 0.5 * (jnp.tanh(0.5 * z) + 1.0)


def _silu(x):
    return x * _sigmoid(x)


def _expand_rows(dst_ref, src, reps, first_chunk=0):
    n = src.shape[0]
    for c in range(src.shape[1] // LANES):
        piece = src[:, c * LANES:(c + 1) * LANES]
        for t in range(reps):
            dst_ref[first_chunk + c, pl.ds(t, n, stride=reps), :] = piece


def _expanded(ref, first_chunk, n_chunks=D_MODEL // LANES):
    return jnp.concatenate([ref[c] for c in range(first_chunk, first_chunk + n_chunks)], axis=1)


def _ada_kernel(c_ref, w_ref, b_ref, o_ref):
    sh, sm = _split_bf16(_silu(c_ref[...]))
    wh, wm = _split_bf16(w_ref[...])
    o_ref[...] = (jnp.dot(sh, wh, preferred_element_type=F32)
                  + (jnp.dot(sh, wm, preferred_element_type=F32) + jnp.dot(sm, wh, preferred_element_type=F32))
                  + b_ref[...])


def _ada(c_all, w_ada, b_ada):
    n_rows, n_out = c_all.shape[0], w_ada.shape[1]
    tn = 1536
    return pl.pallas_call(
        _ada_kernel,
        grid=(n_out // tn,),
        in_specs=[
            pl.BlockSpec((n_rows, D_MODEL), lambda j: (0, 0)),
            pl.BlockSpec((D_MODEL, tn), lambda j: (0, j)),
            pl.BlockSpec((1, tn), lambda j: (0, j)),
        ],
        out_specs=pl.BlockSpec((n_rows, tn), lambda j: (0, j)),
        out_shape=jax.ShapeDtypeStruct((n_rows, n_out), F32),
        compiler_params=pltpu.CompilerParams(dimension_semantics=("arbitrary",), vmem_limit_bytes=VMEM_LIMIT),
        name="ada",
    )(c_all, w_ada, b_ada.reshape(1, n_out))


def _s5prep_kernel(lre_ref, lim_ref, ldt_ref, bre_ref, bim_ref, lbr_ref, lbi_ref, bbr_ref, bbi_ref):
    lam_re, lam_im = lre_ref[...], lim_ref[...]
    dt = jnp.exp(ldt_ref[...])
    mag = jnp.exp(lam_re * dt)
    ang = lam_im * dt
    lb_re, lb_im = mag * jnp.cos(ang), mag * jnp.sin(ang)
    den = lam_re * lam_re + lam_im * lam_im
    f_re = ((lb_re - 1.0) * lam_re + lb_im * lam_im) / den
    f_im = (lb_im * lam_re - (lb_re - 1.0) * lam_im) / den
    lbr_ref[...] = lb_re
    lbi_ref[...] = lb_im
    b_re, b_im = bre_ref[...], bim_ref[...]
    bbr_ref[...] = f_re[:, None, :] * b_re - f_im[:, None, :] * b_im
    bbi_ref[...] = f_re[:, None, :] * b_im + f_im[:, None, :] * b_re


def _s5prep(lam_re, lam_im, log_dt, b_re, b_im):
    g, p = lam_re.shape
    bt_re = jnp.transpose(b_re, (0, 2, 1))
    bt_im = jnp.transpose(b_im, (0, 2, 1))
    return pl.pallas_call(
        _s5prep_kernel,
        out_shape=(
            jax.ShapeDtypeStruct((g, p), F32), jax.ShapeDtypeStruct((g, p), F32),
            jax.ShapeDtypeStruct((g, SSM_GROUP, p), F32), jax.ShapeDtypeStruct((g, SSM_GROUP, p), F32),
        ),
        name="s5prep",
    )(lam_re, lam_im, log_dt.reshape(g, 1), bt_re, bt_im)


def _block_diag(blocks):
    _, r, c = blocks.shape
    b4 = blocks.reshape(N_S5_BLOCKS, S5_BLOCK_GROUPS, r, c)
    eye = jnp.eye(S5_BLOCK_GROUPS, dtype=blocks.dtype)
    out = b4[:, :, :, None, :] * eye[None, :, None, :, None]
    return out.reshape(N_S5_BLOCKS, S5_BLOCK_GROUPS * r, S5_BLOCK_GROUPS * c)


def _mixer_kernel(x_ref, mod_ref, n1w_ref, win_ref, cos_ref, sin_ref, dmask_ref, cdec_ref, sdec_ref,
                  rnw_ref, bmat_ref, cre_ref, cim_ref, lbr_ref, lbi_ref, dsk_ref, wglu_ref, bglu_ref,
                  snw_ref, wout_ref, sret0_ref, sre0_ref, sim0_ref,
                  x1_ref, sret_ref, sre_ref, sim_ref,
                  hb_ref, z_ref, zu_ref, oy_ref, utb_ref, bur0_ref, bur1_ref, bui0_ref, bui1_ref, ytb_ref, yb_ref, modx_ref,
                  *, n_seq, chunk, tile_rows, carry, chunk_decay):
    rows = n_seq * chunk
    seq_per_tile = tile_rows // chunk
    n_tiles = rows // tile_rows
    per_row_mod = chunk % SUBLANES != 0
    dl = D_MODEL // LANES

    def load_states():
        sret_ref[...] = sret0_ref[...]
        sre_ref[...] = sre0_ref[...]
        sim_ref[...] = sim0_ref[...]

    if carry:
        pl.when(pl.program_id(0) == 0)(load_states)
    else:
        load_states()

    if per_row_mod:
        _expand_rows(modx_ref, mod_ref[...], chunk)

    n1w = n1w_ref[...]
    mod_rows = rows if per_row_mod else chunk
    for i in range(rows // mod_rows):
        r0 = i * mod_rows
        xb = _load_rows(x_ref, r0, mod_rows, chunk)
        if per_row_mod:
            sh, sc = _expanded(modx_ref, 0), _expanded(modx_ref, dl)
        else:
            sh = mod_ref[pl.ds(i, 1), pl.ds(0, D_MODEL)]
            sc = mod_ref[pl.ds(i, 1), pl.ds(D_MODEL, D_MODEL)]
        ms = jnp.mean(xb * xb, axis=-1, keepdims=True)
        hn = xb * lax.rsqrt(ms + NORM_EPS) * n1w
        hb_ref[pl.ds(r0, mod_rows), :] = (hn * (1.0 + sc) + sh).astype(BF16)
    ret_w = 4 * RET_WIDTH
    z_ref[...] = jnp.dot(hb_ref[...], win_ref[:, pl.ds(0, ret_w)], preferred_element_type=F32)
    zu = jnp.dot(hb_ref[...], win_ref[:, pl.ds(ret_w, SSM_WIDTH)], preferred_element_type=F32)
    pitch = zu_ref.shape[1] // n_seq
    for c in range(SSM_WIDTH // LANES):
        for b in range(n_seq if pitch != chunk else 1):
            nb = chunk if pitch != chunk else rows
            zu_ref[c, pl.ds(b * pitch, nb), :] = zu[b * chunk:b * chunk + nb, c * LANES:(c + 1) * LANES]

    cos = cos_ref[...]
    sin = sin_ref[...]
    scale = HEAD_DIM ** -0.5
    if seq_per_tile > 1:
        row_id = lax.broadcasted_iota(jnp.int32, (tile_rows, HEAD_DIM), 0)

    def rope(t):
        return t * cos + pltpu.roll(t, HEAD_DIM // 2, 1) * sin

    def ret_tile(ti, c):
        r0 = pl.multiple_of(ti * tile_rows, tile_rows)
        for h in range(RET_HEADS):
            c0 = h * HEAD_DIM
            q = rope(z_ref[pl.ds(r0, tile_rows), pl.ds(c0, HEAD_DIM)])
            k = rope(z_ref[pl.ds(r0, tile_rows), pl.ds(RET_WIDTH + c0, HEAD_DIM)]) * scale
            v = z_ref[pl.ds(r0, tile_rows), pl.ds(2 * RET_WIDTH + c0, HEAD_DIM)]
            g = z_ref[pl.ds(r0, tile_rows), pl.ds(3 * RET_WIDTH + c0, HEAD_DIM)]
            kd = k * sdec_ref[h]
            if tile_rows < HEAD_DIM:
                pad = jnp.zeros((HEAD_DIM - tile_rows, HEAD_DIM), F32)
                k, v, kd = (jnp.concatenate([t, pad], axis=0) for t in (k, v, kd))
                if seq_per_tile > 1:
                    row_kv = lax.broadcasted_iota(jnp.int32, (HEAD_DIM, HEAD_DIM), 0)
            elif seq_per_tile > 1:
                row_kv = row_id
            qb, kb, vb = q.astype(BF16), k.astype(BF16), v.astype(BF16)
            s = lax.dot_general(qb, kb, (((1,), (1,)), ((), ())), preferred_element_type=F32) * dmask_ref[h]
            o = jnp.dot(s.astype(BF16), vb, preferred_element_type=F32)
            cross = None
            for si in range(seq_per_tile):
                sidx = ti * seq_per_tile + si
                st = sret_ref[sidx, h]
                cr = jnp.dot(qb, st.astype(BF16), preferred_element_type=F32)
                if seq_per_tile > 1:
                    in_seq = (row_id >= si * chunk) & (row_id < (si + 1) * chunk)
                    cross = jnp.where(in_seq, cr, 0.0 if cross is None else cross)
                    kds = jnp.where((row_kv >= si * chunk) & (row_kv < (si + 1) * chunk), kd, 0.0)
                else:
                    cross, kds = cr, kd
                upd = lax.dot_general(kds.astype(BF16), vb, (((0,), (0,)), ((), ())), preferred_element_type=F32)
                sret_ref[sidx, h] = st * chunk_decay[h] + upd
            o = o + cross * cdec_ref[h]
            o = o * lax.rsqrt(jnp.mean(o * o, axis=-1, keepdims=True) + NORM_EPS)
            o = o * rnw_ref[:, pl.ds(c0, HEAD_DIM)] * _silu(g)
            oy_ref[pl.ds(r0, tile_rows), pl.ds(c0, HEAD_DIM)] = o
        return c

    lax.fori_loop(0, n_tiles, ret_tile, 0, unroll=True)

    for t in range(chunk):
        for c in range(SSM_WIDTH // LANES):
            utb_ref[pl.ds(t * n_seq, n_seq), pl.ds(c * LANES, LANES)] = zu_ref[c, pl.ds(t, n_seq, stride=pitch), :]
    half_ch = SSM_CH // 2
    blk_per_half = N_S5_BLOCKS // 2
    bur_refs, bui_refs = (bur0_ref, bur1_ref), (bui0_ref, bui1_ref)
    for blk in range(N_S5_BLOCKS):
        hf, lcols = blk // blk_per_half, pl.ds((blk % blk_per_half) * S5_BLOCK_CH, S5_BLOCK_CH)
        ub = utb_ref[:, pl.ds(blk * S5_BLOCK_IN, S5_BLOCK_IN)].astype(BF16)
        bu = jnp.dot(ub, bmat_ref[blk], preferred_element_type=F32)
        bur_refs[hf][:, lcols] = bu[:, :S5_BLOCK_CH]
        bui_refs[hf][:, lcols] = bu[:, S5_BLOCK_CH:]

    scan_w = min(half_ch, SCAN_ELEMS // n_seq)
    for hf in range(2):
        bur_ref, bui_ref = bur_refs[hf], bui_refs[hf]
        for p in range(half_ch // scan_w):
            cols = pl.ds(p * scan_w, scan_w)
            gcols = pl.ds(hf * half_ch + p * scan_w, scan_w)
            lbr = jnp.broadcast_to(lbr_ref[:, gcols], (n_seq, scan_w))
            lbi = jnp.broadcast_to(lbi_ref[:, gcols], (n_seq, scan_w))
            hr, hi = sre_ref[:, gcols], sim_ref[:, gcols]
            for t in range(chunk):
                rws = pl.ds(t * n_seq, n_seq)
                hr, hi = (lbr * hr - lbi * hi + bur_ref[rws, cols], lbr * hi + lbi * hr + bui_ref[rws, cols])
                bur_ref[rws, cols] = hr
                bui_ref[rws, cols] = hi
            sre_ref[:, gcols] = hr
            sim_ref[:, gcols] = hi

    for blk in range(N_S5_BLOCKS):
        hf, lcols = blk // blk_per_half, pl.ds((blk % blk_per_half) * S5_BLOCK_CH, S5_BLOCK_CH)
        yb = jnp.dot(bur_refs[hf][:, lcols].astype(BF16), cre_ref[blk], preferred_element_type=F32)
        yb = yb + jnp.dot(bui_refs[hf][:, lcols].astype(BF16), cim_ref[blk], preferred_element_type=F32)
        ucols = pl.ds(blk * S5_BLOCK_IN, S5_BLOCK_IN)
        ytb_ref[:, ucols] = yb + dsk_ref[:, ucols] * utb_ref[:, ucols]
    for t in range(chunk):
        for c in range(SSM_WIDTH // LANES):
            yb_ref[c, pl.ds(t, n_seq, stride=pitch), :] = ytb_ref[pl.ds(t * n_seq, n_seq), pl.ds(c * LANES, LANES)]

    def seq_major(c):
        if pitch == chunk:
            return yb_ref[c]
        return jnp.concatenate([yb_ref[c, pl.ds(b * pitch, chunk), :] for b in range(n_seq)], axis=0)

    y = jnp.concatenate([seq_major(c) for c in range(SSM_WIDTH // LANES)], axis=1)
    y = jax.nn.gelu(y, approximate=True)
    gate = jnp.dot(y.astype(BF16), wglu_ref[...], preferred_element_type=F32) + bglu_ref[...]
    y = y * _sigmoid(gate)
    y = y * lax.rsqrt(jnp.mean(y * y, axis=-1, keepdims=True) + NORM_EPS) * snw_ref[...]
    oy_ref[:, pl.ds(RET_WIDTH, SSM_WIDTH)] = y

    mix = jnp.dot(oy_ref[...].astype(BF16), wout_ref[...], preferred_element_type=F32)
    for i in range(rows // mod_rows):
        r0 = i * mod_rows
        if per_row_mod:
            g1 = _expanded(modx_ref, 2 * dl)
        else:
            g1 = mod_ref[pl.ds(i, 1), pl.ds(2 * D_MODEL, D_MODEL)]
        _store_rows(x1_ref, r0, mod_rows, chunk,
                    _load_rows(x_ref, r0, mod_rows, chunk) + g1 * mix[r0:r0 + mod_rows])


def _load_rows(ref, r0, n, chunk):
    if len(ref.shape) == 2:
        return ref[pl.ds(r0, n), :]
    assert n == chunk and r0 % chunk == 0
    return ref[r0 // chunk]


def _store_rows(ref, r0, n, chunk, val):
    if len(ref.shape) == 2:
        ref[pl.ds(r0, n), :] = val
    else:
        assert n == chunk and r0 % chunk == 0
        ref[r0 // chunk] = val


def _seq_pitch(chunk):
    return chunk + SUBLANES if chunk % SUBLANES == 0 else chunk


def _const_spec(shape):
    nd = len(shape)
    return pl.BlockSpec(shape, lambda j, _n=nd: (0,) * _n)


def _decay_tables(chunk, tile_rows):
    f32 = np.float32
    log_gamma = np.log1p(-np.exp2(f32(-5.0) - np.arange(RET_HEADS, dtype=f32))).astype(f32)
    r = np.arange(tile_rows)
    seq, loc = r // chunk, (r % chunk).astype(f32)
    rel = loc[:, None] - loc[None, :]
    ok = (seq[:, None] == seq[None, :]) & (rel >= 0)
    dmask = np.where(ok[None], np.exp(np.where(ok, rel, f32(0.0))[None] * log_gamma[:, None, None]), f32(0.0))
    if tile_rows < HEAD_DIM:
        dmask = np.pad(dmask, ((0, 0), (0, 0), (0, HEAD_DIM - tile_rows)))
    cdec = np.exp((loc[None, :] + f32(1.0)) * log_gamma[:, None])
    sdec = np.exp((f32(chunk) - f32(1.0) - loc)[None, :] * log_gamma[:, None])
    bcast = lambda t: np.ascontiguousarray(np.broadcast_to(t[:, :, None], (RET_HEADS, tile_rows, HEAD_DIM)))
    return dmask.astype(f32), bcast(cdec.astype(f32)), bcast(sdec.astype(f32))


def _rope_tables(pos):
    f32 = np.float32
    half = HEAD_DIM // 2
    inv_freq = (f32(ROPE_BASE) ** (-np.arange(half, dtype=f32) / f32(half))).astype(f32)
    ang = (pos.astype(f32)[:, None] * inv_freq[None, :]).astype(f32)
    cos, sin = np.cos(ang).astype(f32), np.sin(ang).astype(f32)
    return np.concatenate([cos, cos], axis=-1), np.concatenate([-sin, sin], axis=-1)


def _mixer(x, mod, pos, states, wts, *, prompt):
    n_seq = SEQ_PER_BLOCK if prompt else SAMPLE_SEQ_PER_BLOCK
    if prompt:
        n_total, seq_len, _ = x.shape
        assert n_total == n_seq
        chunk, tile_rows, n_steps = PROMPT_CHUNK, PROMPT_CHUNK, seq_len // PROMPT_CHUNK
        x_spec = pl.BlockSpec((n_seq, chunk, D_MODEL), lambda j: (0, j, 0))
        mod_spec = pl.BlockSpec((n_seq, 3 * D_MODEL), lambda j: (0, 0))
        tab_spec = pl.BlockSpec((chunk, HEAD_DIM), lambda j: (j, 0))
        seq_map = lambda j: 0
    else:
        chunk = pos.shape[0]
        tile_rows = SUBLANES
        n_total = x.shape[0] // chunk
        n_steps = n_total // n_seq
        x_spec = pl.BlockSpec((n_seq * chunk, D_MODEL), lambda j: (j, 0))
        mod_spec = pl.BlockSpec((n_seq, 3 * D_MODEL), lambda j: (j, 0))
        tab_spec = _const_spec((tile_rows, HEAD_DIM))
        seq_map = lambda j: j
    rows = n_seq * chunk
    cos, sin = _rope_tables(pos)
    if not prompt:
        reps = tile_rows // chunk
        cos, sin = np.tile(cos, (reps, 1)), np.tile(sin, (reps, 1))
    dmask, cdec, sdec = _decay_tables(chunk, tile_rows)
    chunk_decay = tuple(float(math.exp(chunk * math.log1p(-2.0 ** (-5.0 - h)))) for h in range(RET_HEADS))
    sret0, sre0, sim0 = states

    st_ret_spec = pl.BlockSpec((n_seq, RET_HEADS, HEAD_DIM, HEAD_DIM), lambda j: (seq_map(j), 0, 0, 0))
    st_s5_spec = pl.BlockSpec((n_seq, SSM_CH), lambda j: (seq_map(j), 0))
    consts = [dmask, cdec, sdec, wts["rnw"], wts["bmat"], wts["cre"], wts["cim"], wts["lbr"], wts["lbi"],
              wts["dsk"], wts["w_glu"], wts["b_glu"], wts["snw"], wts["w_out"]]
    args = [x, mod, wts["n1w"], wts["w_in"], cos, sin] + consts + [sret0, sre0, sim0]
    in_specs = ([x_spec, mod_spec, _const_spec(wts["n1w"].shape), _const_spec(wts["w_in"].shape), tab_spec, tab_spec]
                + [_const_spec(a.shape) for a in consts] + [st_ret_spec, st_s5_spec, st_s5_spec])

    kern = functools.partial(_mixer_kernel, n_seq=n_seq, chunk=chunk, tile_rows=tile_rows, carry=prompt,
                             chunk_decay=chunk_decay)
    out_shape = (
        jax.ShapeDtypeStruct(x.shape, F32),
        jax.ShapeDtypeStruct((n_total, RET_HEADS, HEAD_DIM, HEAD_DIM), F32),
        jax.ShapeDtypeStruct((n_total, SSM_CH), F32),
        jax.ShapeDtypeStruct((n_total, SSM_CH), F32),
    )
    scratch = [
        pltpu.VMEM((rows, D_MODEL), BF16),
        pltpu.VMEM((rows, 4 * RET_WIDTH), F32),
        pltpu.VMEM((SSM_WIDTH // LANES, n_seq * _seq_pitch(chunk), LANES), F32),
        pltpu.VMEM((rows, D_MODEL), F32),
        pltpu.VMEM((rows, SSM_WIDTH), F32),
        pltpu.VMEM((rows, SSM_CH // 2), F32),
        pltpu.VMEM((rows, SSM_CH // 2), F32),
        pltpu.VMEM((rows, SSM_CH // 2), F32),
        pltpu.VMEM((rows, SSM_CH // 2), F32),
        pltpu.VMEM((rows, SSM_WIDTH), F32),
        pltpu.VMEM((SSM_WIDTH // LANES, n_seq * _seq_pitch(chunk), LANES), F32),
        pltpu.VMEM((3 * D_MODEL // LANES, SUBLANES if prompt else rows, LANES), F32),
    ]
    return pl.pallas_call(
        kern,
        grid=(n_steps,),
        in_specs=in_specs,
        out_specs=(x_spec, st_ret_spec, st_s5_spec, st_s5_spec),
        out_shape=out_shape,
        scratch_shapes=scratch,
        compiler_params=pltpu.CompilerParams(dimension_semantics=("arbitrary",), vmem_limit_bytes=VMEM_LIMIT),
        name="mixer_prompt" if prompt else "mixer_sample",
    )(*args)


PACK_ROWS = D_MODEL // (2 * LANES)


def _store_packed(ref, x):
    half = D_MODEL // 2
    bits = lax.bitcast_convert_type(x.astype(BF16).astype(F32), jnp.uint32)
    words = bits[:, :half] | (bits[:, half:] >> 16)
    for c in range(PACK_ROWS):
        ref[c] = words[:, c * LANES:(c + 1) * LANES]


def _load_packed(ref, n, first_row=0, row_stride=1):
    hi, lo = [], []
    for c in range(PACK_ROWS):
        w = ref[c] if row_stride == 1 else ref[c, pl.ds(first_row, n, stride=row_stride), :]
        hi.append(lax.bitcast_convert_type(w & jnp.uint32(0xFFFF0000), F32))
        lo.append(lax.bitcast_convert_type(w << 16, F32))
    return jnp.concatenate(hi + lo, axis=1)


def _split_bf16(x):
    hi = x.astype(BF16)
    return hi, (x - hi.astype(F32)).astype(BF16)


def _route_tile(x, sh, sc, n2w_ref, wrh_ref, wrm_ref, br_ref, ltri_ref, count_ref, h2_ref, route_ref, topw_ref):
    ms = jnp.mean(x * x, axis=-1, keepdims=True)
    h2 = x * lax.rsqrt(ms + NORM_EPS) * n2w_ref[...] * (1.0 + sc) + sh
    _store_packed(h2_ref, h2)
    hh, hm = _split_bf16(h2)
    logits = (jnp.dot(hh, wrh_ref[...], preferred_element_type=F32)
              + (jnp.dot(hh, wrm_ref[...], preferred_element_type=F32)
                 + jnp.dot(hm, wrh_ref[...], preferred_element_type=F32))) + br_ref[...]
    lane = lax.broadcasted_iota(jnp.int32, logits.shape, 1)
    lane_f = lane.astype(F32)
    work = logits
    vals, idxs = [], []
    for _ in range(TOP_K):
        m = jnp.max(work, axis=-1, keepdims=True)
        idx = jnp.min(jnp.where(work == m, lane_f, float(LANES)), axis=-1, keepdims=True)
        vals.append(m)
        idxs.append(idx)
        work = jnp.where(lane_f == idx, -jnp.inf, work)
    exps = [jnp.exp(v - vals[0]) for v in vals]
    tot = exps[0] + exps[1] + exps[2] + exps[3]
    topw = jnp.zeros(logits.shape, F32)
    for k in range(TOP_K):
        topw = jnp.where(lane == k, exps[k] / tot, topw)
    topw_ref[...] = topw

    onehot = [(lane_f == idxs[k]).astype(F32) for k in range(TOP_K)]
    chosen = onehot[0] + onehot[1] + onehot[2] + onehot[3]
    before = jnp.dot(ltri_ref[...], chosen.astype(BF16), preferred_element_type=F32) + count_ref[...]
    info = jnp.zeros(logits.shape, jnp.int32)
    for k in range(TOP_K):
        rank = jnp.sum(onehot[k] * before, axis=-1, keepdims=True).astype(jnp.int32)
        info = jnp.where(lane == k, idxs[k].astype(jnp.int32), info)
        info = jnp.where(lane == TOP_K + k, rank, info)
    route_ref[...] = jnp.transpose(info)[:2 * TOP_K, :]
    count_ref[...] = count_ref[...] + jnp.sum(chosen, axis=0, keepdims=True)


def _router_kernel(xp_ref, shp_ref, scp_ref, xs_ref, shs_ref, scs_ref, n2w_ref, wrh_ref, wrm_ref, br_ref, ltri_ref,
                   h2_ref, route_ref, topw_ref, count_ref, modx_ref, *, n_prompt_tiles, sample_len):
    i = pl.program_id(0)
    rest = (n2w_ref, wrh_ref, wrm_ref, br_ref, ltri_ref, count_ref, h2_ref, route_ref, topw_ref)

    @pl.when(i == 0)
    def _():
        count_ref[...] = jnp.zeros(count_ref.shape, F32)

    @pl.when(i < n_prompt_tiles)
    def _():
        _route_tile(xp_ref[...], shp_ref[0], scp_ref[0], *rest)

    @pl.when(i >= n_prompt_tiles)
    def _():
        dl = D_MODEL // LANES
        _expand_rows(modx_ref, shs_ref[...], sample_len)
        _expand_rows(modx_ref, scs_ref[...], sample_len, first_chunk=dl)
        _route_tile(xs_ref[...], _expanded(modx_ref, 0), _expanded(modx_ref, dl), *rest)


def _router(xp_rows, mod_p, seq_len, xs_rows, mod_s, sample_len, n2w, w_router, b_router):
    tile = ROUTER_TILE
    n_p, n_s = xp_rows.shape[0], xs_rows.shape[0]
    tp, ts = n_p // tile, n_s // tile
    n_total = n_p + n_s
    mod_p = mod_p.reshape(mod_p.shape[0], 1, mod_p.shape[1])
    seq_of = lambda i: (jnp.minimum(i, tp - 1) * tile) // seq_len
    wr_pad = jnp.pad(w_router, ((0, 0), (0, LANES - N_EXPERTS)))
    wr_hi = wr_pad.astype(BF16)
    wr_mid = (wr_pad - wr_hi.astype(F32)).astype(BF16)
    br_pad = jnp.pad(b_router, ((0, 0), (0, LANES - N_EXPERTS)), constant_values=-1e30)
    ltri = jnp.asarray(np.tril(np.ones((tile, tile), np.float32), -1), BF16)
    clamp_p = lambda i: jnp.minimum(i, tp - 1)
    clamp_s = lambda i: jnp.maximum(i - tp, 0)
    return pl.pallas_call(
        functools.partial(_router_kernel, n_prompt_tiles=tp, sample_len=sample_len),
        grid=(tp + ts,),
        in_specs=[pl.BlockSpec((tile, D_MODEL), lambda i: (clamp_p(i), 0)),
                  pl.BlockSpec((1, 1, D_MODEL), lambda i: (seq_of(i), 0, 3)),
                  pl.BlockSpec((1, 1, D_MODEL), lambda i: (seq_of(i), 0, 4)),
                  pl.BlockSpec((tile, D_MODEL), lambda i: (clamp_s(i), 0)),
                  pl.BlockSpec((tile // sample_len, D_MODEL), lambda i: (clamp_s(i), 3)),
                  pl.BlockSpec((tile // sample_len, D_MODEL), lambda i: (clamp_s(i), 4)),
                  _const_spec(n2w.shape), _const_spec(wr_hi.shape), _const_spec(wr_mid.shape),
                  _const_spec(br_pad.shape), _const_spec(ltri.shape)],
        out_specs=(pl.BlockSpec((PACK_ROWS, tile, LANES), lambda i: (0, i, 0)),
                   pl.BlockSpec((2 * TOP_K, tile), lambda i: (0, i)),
                   pl.BlockSpec((tile, LANES), lambda i: (i, 0)),
                   pl.BlockSpec((1, LANES), lambda i: (0, 0))),
        out_shape=(jax.ShapeDtypeStruct((PACK_ROWS, n_total, LANES), jnp.uint32),
                   jax.ShapeDtypeStruct((2 * TOP_K, n_total), jnp.int32),
                   jax.ShapeDtypeStruct((n_total, LANES), F32),
                   jax.ShapeDtypeStruct((1, LANES), F32)),
        scratch_shapes=[pltpu.VMEM((2 * D_MODEL // LANES, tile, LANES), F32)],
        compiler_params=pltpu.CompilerParams(dimension_semantics=("arbitrary",), vmem_limit_bytes=VMEM_LIMIT),
        name="router",
    )(xp_rows, mod_p, mod_p, xs_rows, mod_s, mod_s, n2w, wr_hi, wr_mid, br_pad, ltri)


def _gather_rows(table, idx):
    n = idx.shape[0]
    steps = n // SC_GATHER_WINDOW
    assert n % SC_GATHER_WINDOW == 0 and steps % SC_WORKERS == 0
    mesh = plsc.VectorSubcoreMesh(core_axis_name="c", subcore_axis_name="s")

    @functools.partial(pl.kernel, out_type=jax.ShapeDtypeStruct((n, table.shape[1]), table.dtype), mesh=mesh,
                       scratch_types=[])
    def gather_kernel(table_hbm, idx_hbm, out_hbm):
        def body(idx_vmem, out_vmem):
            pltpu.sync_copy(table_hbm.at[idx_vmem.at[0]], out_vmem)

        pltpu.emit_pipeline(
            body,
            grid=(steps,),
            in_specs=[pl.BlockSpec((1, SC_GATHER_WINDOW), lambda i: (0, i))],
            out_specs=[pl.BlockSpec((SC_GATHER_WINDOW, table.shape[1]), lambda i: (i, 0))],
            core_axis_name=("c", "s"),
            dimension_semantics=(pltpu.PARALLEL,),
        )(idx_hbm, out_hbm)

    return gather_kernel(table, idx.reshape(1, n))


def _dispatch_packed(table, slot_kt, n_slots):
    planes, n_tok, lanes = table.shape
    win = SC_GATHER_WINDOW
    blocks = n_tok // win
    assert n_tok % win == 0
    blocks_pad = _round_up(blocks, SC_WORKERS // math.gcd(SC_WORKERS, planes))
    n_spare = (blocks_pad - blocks) * TOP_K * win
    n_ext = n_slots + n_spare
    dest = jnp.transpose(slot_kt.reshape(TOP_K, blocks, win), (1, 0, 2))
    spare = n_slots + jnp.arange(n_spare, dtype=jnp.int32).reshape(blocks_pad - blocks, TOP_K, win)
    dest = jnp.concatenate([dest, spare], axis=0)[None] + (jnp.arange(planes, dtype=jnp.int32) * n_ext)[:, None, None, None]
    dest = dest.reshape(planes * blocks_pad * TOP_K, win)
    steps = planes * blocks_pad
    src_block = lambda g: (g // blocks_pad) * blocks + jnp.minimum(g % blocks_pad, blocks - 1)
    mesh = plsc.VectorSubcoreMesh(core_axis_name="c", subcore_axis_name="s")

    @functools.partial(pl.kernel, out_type=jax.ShapeDtypeStruct((planes * n_ext, lanes), table.dtype), mesh=mesh,
                       scratch_types=[])
    def scatter_kernel(table_hbm, dest_hbm, out_hbm):
        def body(rows_vmem, dest_vmem):
            for k in range(TOP_K):
                pltpu.sync_copy(rows_vmem, out_hbm.at[dest_vmem.at[k]])

        pltpu.emit_pipeline(
            body,
            grid=(steps,),
            in_specs=[pl.BlockSpec((win, lanes), lambda g: (src_block(g), 0)),
                      pl.BlockSpec((TOP_K, win), lambda g: (g, 0))],
            out_specs=[],
            core_axis_name=("c", "s"),
            dimension_semantics=(pltpu.PARALLEL,),
        )(table_hbm, dest_hbm)

    return scatter_kernel(table.reshape(planes * n_tok, lanes), dest).reshape(planes, n_ext, lanes)


def _gather_packed(table, rows):
    planes, n_table, lanes = table.shape
    idx = jnp.concatenate([rows + c * n_table for c in range(planes)])
    out = _gather_rows(table.reshape(planes * n_table, lanes), idx)
    return out.reshape(planes, rows.shape[0], lanes)


def _expert_weight_copies(e, w1_hbm, w2_hbm, w1s_ref, w2s_ref, sem):
    return (pltpu.make_async_copy(w1_hbm.at[e], w1s_ref, sem.at[0]),
            pltpu.make_async_copy(w2_hbm.at[e], w2s_ref, sem.at[1]))


def _experts_kernel(te_ref, tr_ref, nx_ref, nv_ref, xs_ref, w1_hbm, b1_ref, w2_hbm, b2_ref, ys_ref,
                    w1s_ref, w2s_ref, w1b_ref, w2b_ref, sem):
    i = pl.program_id(0)
    e = te_ref[i]
    new_expert = (i == 0) | (e != te_ref[jnp.maximum(i - 1, 0)])
    copies = functools.partial(_expert_weight_copies, w1_hbm=w1_hbm, w2_hbm=w2_hbm, w1s_ref=w1s_ref,
                               w2s_ref=w2s_ref, sem=sem)

    @pl.when(i == 0)
    def _():
        for c in copies(e):
            c.start()

    @pl.when(new_expert)
    def _():
        for c in copies(e):
            c.wait()
        w1b_ref[...] = w1s_ref[...].astype(BF16)
        w2b_ref[...] = w2s_ref[...].astype(BF16)

        @pl.when(nx_ref[i] >= 0)
        def _():
            for c in copies(nx_ref[i]):
                c.start()

    def expert_pass(r0, n_rows, rows_valid):
        xs_v, ys_v = xs_ref.at[:, pl.ds(r0, n_rows)], ys_ref.at[:, pl.ds(r0, n_rows)]
        row = lax.broadcasted_iota(jnp.int32, (n_rows, D_MODEL), 0)
        x = jnp.where(row < rows_valid, _load_packed(xs_v, n_rows), 0.0).astype(BF16)
        hu = jnp.dot(x, w1b_ref[...], preferred_element_type=F32) + b1_ref[0]
        x_glu = jnp.minimum(hu[:, :D_FF], SWIGLU_LIMIT)
        x_lin = jnp.clip(hu[:, D_FF:], -SWIGLU_LIMIT, SWIGLU_LIMIT)
        act = x_glu * _sigmoid(SWIGLU_ALPHA * x_glu) * (x_lin + 1.0)
        _store_packed(ys_v, jnp.dot(act.astype(BF16), w2b_ref[...], preferred_element_type=F32) + b2_ref[0])

    def zero_rows(r0, n_rows):
        for c in range(PACK_ROWS):
            ys_ref[c, pl.ds(r0, n_rows), :] = jnp.zeros((n_rows, LANES), jnp.uint32)

    half = EXPERT_ROWS // 2
    rows_tile = jnp.where(i < nv_ref[0], tr_ref[i], 0)
    full_tile = rows_tile > SLOT_TILE - half

    @pl.when(full_tile)
    def _():
        expert_pass(0, SLOT_TILE, rows_tile)

    for h in range(SLOT_TILE // EXPERT_ROWS):
        r0 = h * EXPERT_ROWS
        rows_here = jnp.where(full_tile, -1, rows_tile - r0)

        @pl.when(rows_here > half)
        def _(r0=r0, rows_here=rows_here):
            expert_pass(r0, EXPERT_ROWS, rows_here)

        @pl.when((rows_here > 0) & (rows_here <= half))
        def _(r0=r0, rows_here=rows_here):
            expert_pass(r0, half, rows_here)
            zero_rows(r0 + half, half)

        @pl.when((rows_here <= 0) & jnp.logical_not(full_tile))
        def _(r0=r0):
            zero_rows(r0, EXPERT_ROWS)


def _experts(tile_expert, tile_rows, next_expert, n_valid, xs, w1, b1, w2, b2):
    n_tiles = tile_expert.shape[0]
    n_slots = n_tiles * SLOT_TILE
    grid_spec = pltpu.PrefetchScalarGridSpec(
        num_scalar_prefetch=4,
        grid=(n_tiles,),
        in_specs=[
            pl.BlockSpec((PACK_ROWS, SLOT_TILE, LANES), lambda i, te, tr, nx, nv: (0, i, 0)),
            pl.BlockSpec(memory_space=pl.ANY),
            pl.BlockSpec((1, 1, 2 * D_FF), lambda i, te, tr, nx, nv: (te[i], 0, 0)),
            pl.BlockSpec(memory_space=pl.ANY),
            pl.BlockSpec((1, 1, D_MODEL), lambda i, te, tr, nx, nv: (te[i], 0, 0)),
        ],
        out_specs=pl.BlockSpec((PACK_ROWS, SLOT_TILE, LANES), lambda i, te, tr, nx, nv: (0, i, 0)),
        scratch_shapes=[pltpu.VMEM((D_MODEL, 2 * D_FF), F32), pltpu.VMEM((D_FF, D_MODEL), F32),
                        pltpu.VMEM((D_MODEL, 2 * D_FF), BF16), pltpu.VMEM((D_FF, D_MODEL), BF16),
                        pltpu.SemaphoreType.DMA((2,))],
    )
    return pl.pallas_call(
        _experts_kernel,
        grid_spec=grid_spec,
        out_shape=jax.ShapeDtypeStruct((PACK_ROWS, n_slots, LANES), jnp.uint32),
        compiler_params=pltpu.CompilerParams(dimension_semantics=("arbitrary",), vmem_limit_bytes=VMEM_LIMIT),
        name="experts",
    )(tile_expert, tile_rows, next_expert, n_valid, xs, w1, b1.reshape(N_EXPERTS, 1, 2 * D_FF), w2,
      b2.reshape(N_EXPERTS, 1, D_MODEL))


def _combine_kernel(y4_ref, x1_ref, topw_ref, g2_ref, fw_ref, *rest, reps):
    o_ref = rest[-2] if reps else rest[-1]
    w = topw_ref[...]
    n = w.shape[0]
    ff = None
    for k in range(TOP_K):
        yk = w[:, k:k + 1] * _load_packed(y4_ref.at[:, k], n)
        ff = yk if ff is None else ff + yk
    if reps:
        _expand_rows(rest[-1], g2_ref[...], reps)
        g2 = _expanded(rest[-1], 0)
    else:
        g2 = g2_ref[0]
    x = x1_ref[...] + g2 * ff
    ms = jnp.mean(x * x, axis=-1, keepdims=True)
    o_ref[...] = x * lax.rsqrt(ms + NORM_EPS) * fw_ref[...]


def _combine(y4, y4_row0, x1, x1_row0, n_rows, topw, topw_row0, mod, rows_per_mod, fw, out_buf):
    tile = COMBINE_TILE
    y4_off, x1_off, tw_off = y4_row0 // tile, x1_row0 // tile, topw_row0 // tile
    reps = rows_per_mod if rows_per_mod < tile else 0
    scratch = []
    if reps:
        g2_spec = pl.BlockSpec((tile // reps, D_MODEL), lambda i: (i + x1_off, 5))
        scratch = [pltpu.VMEM((D_MODEL // LANES, tile, LANES), F32)]
    else:
        mod = mod.reshape(mod.shape[0], 1, mod.shape[1])
        g2_spec = pl.BlockSpec((1, 1, D_MODEL), lambda i: (((i + x1_off) * tile) // rows_per_mod, 0, 5))
    in_specs = [pl.BlockSpec((PACK_ROWS, TOP_K, tile, LANES), lambda i: (0, 0, i + y4_off, 0)),
                pl.BlockSpec((tile, D_MODEL), lambda i: (i + x1_off, 0)),
                pl.BlockSpec((tile, LANES), lambda i: (i + tw_off, 0)),
                g2_spec, _const_spec(fw.shape)]
    args = [y4, x1, topw, mod, fw]
    aliases = {}
    if out_buf is not None:
        in_specs.append(pl.BlockSpec(memory_space=pl.ANY))
        args.append(out_buf)
        aliases = {len(args) - 1: 0}
    return pl.pallas_call(
        functools.partial(_combine_kernel, reps=reps),
        grid=(n_rows // tile,),
        in_specs=in_specs,
        out_specs=pl.BlockSpec((tile, D_MODEL), lambda i: (i + x1_off, 0)),
        out_shape=jax.ShapeDtypeStruct(x1.shape, F32),
        input_output_aliases=aliases,
        scratch_shapes=scratch,
        compiler_params=pltpu.CompilerParams(dimension_semantics=("arbitrary",), vmem_limit_bytes=VMEM_LIMIT),
        name="combine",
    )(*args)


def _routing_tables(route, counts, n_slots):
    padded = ((counts + SLOT_TILE - 1) // SLOT_TILE) * SLOT_TILE
    pend = jnp.cumsum(padded)
    poff = pend - padded
    expert_kt, rank_kt = route[:TOP_K], route[TOP_K:]
    experts = jnp.arange(N_EXPERTS, dtype=jnp.int32)
    start_kt = jnp.sum((expert_kt[None] == experts[:, None, None]).astype(jnp.int32) * poff[:, None, None], axis=0)
    slot_kt = start_kt + rank_kt
    n_tiles = n_slots // SLOT_TILE
    n_valid = (pend[-1] // SLOT_TILE).astype(jnp.int32)
    tile_row = jnp.minimum(jnp.arange(n_tiles, dtype=jnp.int32), n_valid - 1) * SLOT_TILE
    in_later = (pend[None, :] <= tile_row[:, None]).astype(jnp.int32)
    tile_e = jnp.sum(in_later, axis=1).astype(jnp.int32)
    is_e = (experts[None, :] == tile_e[:, None]).astype(jnp.int32)
    used_end = jnp.sum(is_e * (poff + counts)[None, :], axis=1)
    tile_rows = jnp.clip(used_end - tile_row, 0, SLOT_TILE).astype(jnp.int32)
    later_used = (experts[None, :] > tile_e[:, None]) & (counts[None, :] > 0)
    next_e = jnp.min(jnp.where(later_used, experts[None, :], N_EXPERTS), axis=1)
    next_e = jnp.where(next_e < N_EXPERTS, next_e, -1).astype(jnp.int32)
    return slot_kt, tile_e, tile_rows, next_e, n_valid.reshape(1)


def _round_up(n, m):
    return ((n + m - 1) // m) * m


def kernel(x_prompt, x_sample, c_prompt, c_sample, state_ret, state_s5_re, state_s5_im, norm1_w, norm2_w, w_ada, b_ada, w_in, ret_norm_w, s5_lam_re, s5_lam_im, s5_log_dt, s5_b_re, s5_b_im, s5_c_re, s5_c_im, s5_d, w_glu, b_glu, s5_norm_w, w_out, w_router, b_router, w1, b1, w2, b2, final_w):
    bp, lp, _ = x_prompt.shape
    bs, ls, _ = x_sample.shape
    assert norm1_w.shape[0] == 1, "single-layer model"
    n_p, n_s = bp * lp, bs * ls
    n_tok = n_p + n_s

    mod = _ada(jnp.concatenate([c_prompt, c_sample], axis=0), w_ada[0], b_ada[0])
    mod_p, mod_s = mod[:bp], mod[bp:]

    lbr, lbi, bbr, bbi = _s5prep(s5_lam_re[0], s5_lam_im[0], s5_log_dt[0], s5_b_re[0], s5_b_im[0])
    bmat = jnp.concatenate([_block_diag(bbr), _block_diag(bbi)], axis=-1).astype(BF16)
    cre = _block_diag(jnp.transpose(s5_c_re[0], (0, 2, 1))).astype(BF16)
    cim = _block_diag(jnp.transpose(-s5_c_im[0], (0, 2, 1))).astype(BF16)
    wts = dict(
        n1w=norm1_w, w_in=w_in[0].astype(BF16), rnw=ret_norm_w, bmat=bmat, cre=cre, cim=cim,
        lbr=lbr.reshape(1, SSM_CH), lbi=lbi.reshape(1, SSM_CH), dsk=s5_d[0].reshape(1, SSM_WIDTH),
        w_glu=w_glu[0].astype(BF16), b_glu=b_glu, snw=s5_norm_w, w_out=w_out[0].astype(BF16),
    )

    zero_states = (jnp.zeros((bp, RET_HEADS, HEAD_DIM, HEAD_DIM), F32), jnp.zeros((bp, SSM_CH), F32),
                   jnp.zeros((bp, SSM_CH), F32))
    x1_p, ret_p, re_p, im_p = _mixer(x_prompt, mod_p, np.arange(lp, dtype=np.float32), zero_states, wts,
                                     prompt=True)
    sample_states = (state_ret[0], state_s5_re[0].reshape(bs, SSM_CH), state_s5_im[0].reshape(bs, SSM_CH))
    x1_s, ret_s, re_s, im_s = _mixer(x_sample.reshape(n_s, D_MODEL), mod_s,
                                     PAST_LEN + np.arange(ls, dtype=np.float32), sample_states, wts, prompt=False)

    x1_p_rows = x1_p.reshape(n_p, D_MODEL)
    h2, route, topw, counts = _router(x1_p_rows, mod_p, lp, x1_s, mod_s, ls, norm2_w, w_router[0], b_router)

    n_assign = n_tok * TOP_K
    gather_quantum = SC_GATHER_WINDOW * SC_WORKERS // PACK_ROWS
    assert n_assign % gather_quantum == 0
    n_slots = _round_up(_round_up(n_assign, SLOT_TILE) + N_EXPERTS * SLOT_TILE, gather_quantum)
    slot_kt, tile_e, tile_rows, next_e, n_valid = _routing_tables(route, counts[0, :N_EXPERTS].astype(jnp.int32),
                                                                  n_slots)
    xs = _dispatch_packed(h2, slot_kt, n_slots)
    ys = _experts(tile_e, tile_rows, next_e, n_valid, xs, w1[0], b1[0], w2[0], b2[0])
    fw = final_w.reshape(1, D_MODEL)
    y_p, y_s = None, None
    bounds = [r * (n_p // COMBINE_RANGES) for r in range(COMBINE_RANGES)] + [n_tok]
    for lo, hi in zip(bounds[:-1], bounds[1:]):
        y4 = _gather_packed(ys, slot_kt[:, lo:hi].reshape(-1)).reshape(PACK_ROWS, TOP_K, hi - lo, LANES)
        y_p = _combine(y4, 0, x1_p_rows, lo, min(hi, n_p) - lo, topw, lo, mod_p, lp, fw, y_p)
        if hi > n_p:
            y_s = _combine(y4, n_p - lo, x1_s, 0, n_s, topw, n_p, mod_s, ls, fw, None)

    g, p = SSM_GROUPS, SSM_STATE
    return (y_p.reshape(bp, lp, D_MODEL), y_s.reshape(bs, ls, D_MODEL),
            ret_p[None], re_p.reshape(1, bp, g, p), im_p.reshape(1, bp, g, p),
            ret_s[None], re_s.reshape(1, bs, g, p), im_s.reshape(1, bs, g, p))
```

```python
import functools
import math

import jax
import jax.numpy as jnp
import numpy as np
from jax import lax
from jax.experimental import pallas as pl
from jax.experimental.pallas import tpu as pltpu
from jax.experimental.pallas import tpu_sc as plsc

F32 = jnp.float32
BF16 = jnp.bfloat16

D_MODEL = 1024
PAST_LEN = 16384
RET_WIDTH = 512
RET_HEADS = 4
HEAD_DIM = 128
ROPE_BASE = 10000.0
SSM_WIDTH = 512
SSM_GROUP = 16
SSM_GROUPS = 32
SSM_STATE = 64
SSM_CH = SSM_GROUPS * SSM_STATE
N_EXPERTS = 32
TOP_K = 4
D_FF = 1024
SWIGLU_LIMIT = 7.0
SWIGLU_ALPHA = 1.702
NORM_EPS = 1e-6

LANES = 128
SUBLANES = 8
VMEM_LIMIT = 56 * 1024 * 1024

SEQ_PER_BLOCK = 8
SAMPLE_SEQ_PER_BLOCK = 16
SCAN_ELEMS = 8 * 1024
PROMPT_CHUNK = 64
S5_BLOCK_GROUPS = 8
N_S5_BLOCKS = SSM_GROUPS // S5_BLOCK_GROUPS
S5_BLOCK_IN = S5_BLOCK_GROUPS * SSM_GROUP
S5_BLOCK_CH = S5_BLOCK_GROUPS * SSM_STATE
ROUTER_TILE = 512
SLOT_TILE = 512
EXPERT_ROWS = 256
COMBINE_TILE = 512
COMBINE_RANGES = 2
SC_GATHER_WINDOW = 128
SC_WORKERS = 32


def _sigmoid(z):
    return 0.5 * (jnp.tanh(0.5 * z) + 1.0)


def _silu(x):
    return x * _sigmoid(x)


def _expand_rows(dst_ref, src, reps, first_chunk=0):
    n = src.shape[0]
    for c in range(src.shape[1] // LANES):
        piece = src[:, c * LANES:(c + 1) * LANES]
        for t in range(reps):
            dst_ref[first_chunk + c, pl.ds(t, n, stride=reps), :] = piece


def _expanded(ref, first_chunk, n_chunks=D_MODEL // LANES):
    return jnp.concatenate([ref[c] for c in range(first_chunk, first_chunk + n_chunks)], axis=1)


def _ada_kernel(c_ref, w_ref, b_ref, o_ref):
    sh, sm = _split_bf16(_silu(c_ref[...]))
    wh, wm = _split_bf16(w_ref[...])
    o_ref[...] = (jnp.dot(sh, wh, preferred_element_type=F32)
                  + (jnp.dot(sh, wm, preferred_element_type=F32) + jnp.dot(sm, wh, preferred_element_type=F32))
                  + b_ref[...])


def _ada(c_all, w_ada, b_ada):
    n_rows, n_out = c_all.shape[0], w_ada.shape[1]
    tn = 1536
    return pl.pallas_call(
        _ada_kernel,
        grid=(n_out // tn,),
        in_specs=[
            pl.BlockSpec((n_rows, D_MODEL), lambda j: (0, 0)),
            pl.BlockSpec((D_MODEL, tn), lambda j: (0, j)),
            pl.BlockSpec((1, tn), lambda j: (0, j)),
        ],
        out_specs=pl.BlockSpec((n_rows, tn), lambda j: (0, j)),
        out_shape=jax.ShapeDtypeStruct((n_rows, n_out), F32),
        compiler_params=pltpu.CompilerParams(dimension_semantics=("arbitrary",), vmem_limit_bytes=VMEM_LIMIT),
        name="ada",
    )(c_all, w_ada, b_ada.reshape(1, n_out))


def _s5prep_kernel(lre_ref, lim_ref, ldt_ref, bre_ref, bim_ref, lbr_ref, lbi_ref, bbr_ref, bbi_ref):
    lam_re, lam_im = lre_ref[...], lim_ref[...]
    dt = jnp.exp(ldt_ref[...])
    mag = jnp.exp(lam_re * dt)
    ang = lam_im * dt
    lb_re, lb_im = mag * jnp.cos(ang), mag * jnp.sin(ang)
    den = lam_re * lam_re + lam_im * lam_im
    f_re = ((lb_re - 1.0) * lam_re + lb_im * lam_im) / den
    f_im = (lb_im * lam_re - (lb_re - 1.0) * lam_im) / den
    lbr_ref[...] = lb_re
    lbi_ref[...] = lb_im
    b_re, b_im = bre_ref[...], bim_ref[...]
    bbr_ref[...] = f_re[:, None, :] * b_re - f_im[:, None, :] * b_im
    bbi_ref[...] = f_re[:, None, :] * b_im + f_im[:, None, :] * b_re


def _s5prep(lam_re, lam_im, log_dt, b_re, b_im):
    g, p = lam_re.shape
    bt_re = jnp.transpose(b_re, (0, 2, 1))
    bt_im = jnp.transpose(b_im, (0, 2, 1))
    return pl.pallas_call(
        _s5prep_kernel,
        out_shape=(
            jax.ShapeDtypeStruct((g, p), F32), jax.ShapeDtypeStruct((g, p), F32),
            jax.ShapeDtypeStruct((g, SSM_GROUP, p), F32), jax.ShapeDtypeStruct((g, SSM_GROUP, p), F32),
        ),
        name="s5prep",
    )(lam_re, lam_im, log_dt.reshape(g, 1), bt_re, bt_im)


def _block_diag(blocks):
    _, r, c = blocks.shape
    b4 = blocks.reshape(N_S5_BLOCKS, S5_BLOCK_GROUPS, r, c)
    eye = jnp.eye(S5_BLOCK_GROUPS, dtype=blocks.dtype)
    out = b4[:, :, :, None, :] * eye[None, :, None, :, None]
    return out.reshape(N_S5_BLOCKS, S5_BLOCK_GROUPS * r, S5_BLOCK_GROUPS * c)


def _mixer_kernel(x_ref, mod_ref, n1w_ref, win_ref, cos_ref, sin_ref, dmask_ref, cdec_ref, sdec_ref,
                  rnw_ref, bmat_ref, cre_ref, cim_ref, lbr_ref, lbi_ref, dsk_ref, wglu_ref, bglu_ref,
                  snw_ref, wout_ref, sret0_ref, sre0_ref, sim0_ref,
                  x1_ref, sret_ref, sre_ref, sim_ref,
                  hb_ref, z_ref, zu_ref, oy_ref, utb_ref, bur0_ref, bur1_ref, bui0_ref, bui1_ref, ytb_ref, yb_ref, modx_ref,
                  *, n_seq, chunk, tile_rows, carry, chunk_decay):
    rows = n_seq * chunk
    seq_per_tile = tile_rows // chunk
    n_tiles = rows // tile_rows
    per_row_mod = chunk % SUBLANES != 0
    dl = D_MODEL // LANES

    def load_states():
        sret_ref[...] = sret0_ref[...]
        sre_ref[...] = sre0_ref[...]
        sim_ref[...] = sim0_ref[...]

    if carry:
        pl.when(pl.program_id(0) == 0)(load_states)
    else:
        load_states()

    if per_row_mod:
        _expand_rows(modx_ref, mod_ref[...], chunk)

    n1w = n1w_ref[...]
    mod_rows = rows if per_row_mod else chunk
    for i in range(rows // mod_rows):
        r0 = i * mod_rows
        xb = _load_rows(x_ref, r0, mod_rows, chunk)
        if per_row_mod:
            sh, sc = _expanded(modx_ref, 0), _expanded(modx_ref, dl)
        else:
            sh = mod_ref[pl.ds(i, 1), pl.ds(0, D_MODEL)]
            sc = mod_ref[pl.ds(i, 1), pl.ds(D_MODEL, D_MODEL)]
        ms = jnp.mean(xb * xb, axis=-1, keepdims=True)
        hn = xb * lax.rsqrt(ms + NORM_EPS) * n1w
        hb_ref[pl.ds(r0, mod_rows), :] = (hn * (1.0 + sc) + sh).astype(BF16)
    ret_w = 4 * RET_WIDTH
    z_ref[...] = jnp.dot(hb_ref[...], win_ref[:, pl.ds(0, ret_w)], preferred_element_type=F32)
    zu = jnp.dot(hb_ref[...], win_ref[:, pl.ds(ret_w, SSM_WIDTH)], preferred_element_type=F32)
    pitch = zu_ref.shape[1] // n_seq
    for c in range(SSM_WIDTH // LANES):
        for b in range(n_seq if pitch != chunk else 1):
            nb = chunk if pitch != chunk else rows
            zu_ref[c, pl.ds(b * pitch, nb), :] = zu[b * chunk:b * chunk + nb, c * LANES:(c + 1) * LANES]

    cos = cos_ref[...]
    sin = sin_ref[...]
    scale = HEAD_DIM ** -0.5
    if seq_per_tile > 1:
        row_id = lax.broadcasted_iota(jnp.int32, (tile_rows, HEAD_DIM), 0)

    def rope(t):
        return t * cos + pltpu.roll(t, HEAD_DIM // 2, 1) * sin

    def ret_tile(ti, c):
        r0 = pl.multiple_of(ti * tile_rows, tile_rows)
        for h in range(RET_HEADS):
            c0 = h * HEAD_DIM
            q = rope(z_ref[pl.ds(r0, tile_rows), pl.ds(c0, HEAD_DIM)])
            k = rope(z_ref[pl.ds(r0, tile_rows), pl.ds(RET_WIDTH + c0, HEAD_DIM)]) * scale
            v = z_ref[pl.ds(r0, tile_rows), pl.ds(2 * RET_WIDTH + c0, HEAD_DIM)]
            g = z_ref[pl.ds(r0, tile_rows), pl.ds(3 * RET_WIDTH + c0, HEAD_DIM)]
            kd = k * sdec_ref[h]
            if tile_rows < HEAD_DIM:
                pad = jnp.zeros((HEAD_DIM - tile_rows, HEAD_DIM), F32)
                k, v, kd = (jnp.concatenate([t, pad], axis=0) for t in (k, v, kd))
                if seq_per_tile > 1:
                    row_kv = lax.broadcasted_iota(jnp.int32, (HEAD_DIM, HEAD_DIM), 0)
            elif seq_per_tile > 1:
                row_kv = row_id
            qb, kb, vb = q.astype(BF16), k.astype(BF16), v.astype(BF16)
            s = lax.dot_general(qb, kb, (((1,), (1,)), ((), ())), preferred_element_type=F32) * dmask_ref[h]
            o = jnp.dot(s.astype(BF16), vb, preferred_element_type=F32)
            cross = None
            for si in range(seq_per_tile):
                sidx = ti * seq_per_tile + si
                st = sret_ref[sidx, h]
                cr = jnp.dot(qb, st.astype(BF16), preferred_element_type=F32)
                if seq_per_tile > 1:
                    in_seq = (row_id >= si * chunk) & (row_id < (si + 1) * chunk)
                    cross = jnp.where(in_seq, cr, 0.0 if cross is None else cross)
                    kds = jnp.where((row_kv >= si * chunk) & (row_kv < (si + 1) * chunk), kd, 0.0)
                else:
                    cross, kds = cr, kd
                upd = lax.dot_general(kds.astype(BF16), vb, (((0,), (0,)), ((), ())), preferred_element_type=F32)
                sret_ref[sidx, h] = st * chunk_decay[h] + upd
            o = o + cross * cdec_ref[h]
            o = o * lax.rsqrt(jnp.mean(o * o, axis=-1, keepdims=True) + NORM_EPS)
            o = o * rnw_ref[:, pl.ds(c0, HEAD_DIM)] * _silu(g)
            oy_ref[pl.ds(r0, tile_rows), pl.ds(c0, HEAD_DIM)] = o
        return c

    lax.fori_loop(0, n_tiles, ret_tile, 0, unroll=True)

    for t in range(chunk):
        for c in range(SSM_WIDTH // LANES):
            utb_ref[pl.ds(t * n_seq, n_seq), pl.ds(c * LANES, LANES)] = zu_ref[c, pl.ds(t, n_seq, stride=pitch), :]
    half_ch = SSM_CH // 2
    blk_per_half = N_S5_BLOCKS // 2
    bur_refs, bui_refs = (bur0_ref, bur1_ref), (bui0_ref, bui1_ref)
    for blk in range(N_S5_BLOCKS):
        hf, lcols = blk // blk_per_half, pl.ds((blk % blk_per_half) * S5_BLOCK_CH, S5_BLOCK_CH)
        ub = utb_ref[:, pl.ds(blk * S5_BLOCK_IN, S5_BLOCK_IN)].astype(BF16)
        bu = jnp.dot(ub, bmat_ref[blk], preferred_element_type=F32)
        bur_refs[hf][:, lcols] = bu[:, :S5_BLOCK_CH]
        bui_refs[hf][:, lcols] = bu[:, S5_BLOCK_CH:]

    scan_w = min(half_ch, SCAN_ELEMS // n_seq)
    for hf in range(2):
        bur_ref, bui_ref = bur_refs[hf], bui_refs[hf]
        for p in range(half_ch // scan_w):
            cols = pl.ds(p * scan_w, scan_w)
            gcols = pl.ds(hf * half_ch + p * scan_w, scan_w)
            lbr = jnp.broadcast_to(lbr_ref[:, gcols], (n_seq, scan_w))
            lbi = jnp.broadcast_to(lbi_ref[:, gcols], (n_seq, scan_w))
            hr, hi = sre_ref[:, gcols], sim_ref[:, gcols]
            for t in range(chunk):
                rws = pl.ds(t * n_seq, n_seq)
                hr, hi = (lbr * hr - lbi * hi + bur_ref[rws, cols], lbr * hi + lbi * hr + bui_ref[rws, cols])
                bur_ref[rws, cols] = hr
                bui_ref[rws, cols] = hi
            sre_ref[:, gcols] = hr
            sim_ref[:, gcols] = hi

    for blk in range(N_S5_BLOCKS):
        hf, lcols = blk // blk_per_half, pl.ds((blk % blk_per_half) * S5_BLOCK_CH, S5_BLOCK_CH)
        yb = jnp.dot(bur_refs[hf][:, lcols].astype(BF16), cre_ref[blk], preferred_element_type=F32)
        yb = yb + jnp.dot(bui_refs[hf][:, lcols].astype(BF16), cim_ref[blk], preferred_element_type=F32)
        ucols = pl.ds(blk * S5_BLOCK_IN, S5_BLOCK_IN)
        ytb_ref[:, ucols] = yb + dsk_ref[:, ucols] * utb_ref[:, ucols]
    for t in range(chunk):
        for c in range(SSM_WIDTH // LANES):
            yb_ref[c, pl.ds(t, n_seq, stride=pitch), :] = ytb_ref[pl.ds(t * n_seq, n_seq), pl.ds(c * LANES, LANES)]

    def seq_major(c):
        if pitch == chunk:
            return yb_ref[c]
        return jnp.concatenate([yb_ref[c, pl.ds(b * pitch, chunk), :] for b in range(n_seq)], axis=0)

    y = jnp.concatenate([seq_major(c) for c in range(SSM_WIDTH // LANES)], axis=1)
    y = jax.nn.gelu(y, approximate=True)
    gate = jnp.dot(y.astype(BF16), wglu_ref[...], preferred_element_type=F32) + bglu_ref[...]
    y = y * _sigmoid(gate)
    y = y * lax.rsqrt(jnp.mean(y * y, axis=-1, keepdims=True) + NORM_EPS) * snw_ref[...]
    oy_ref[:, pl.ds(RET_WIDTH, SSM_WIDTH)] = y

    mix = jnp.dot(oy_ref[...].astype(BF16), wout_ref[...], preferred_element_type=F32)
    for i in range(rows // mod_rows):
        r0 = i * mod_rows
        if per_row_mod:
            g1 = _expanded(modx_ref, 2 * dl)
        else:
            g1 = mod_ref[pl.ds(i, 1), pl.ds(2 * D_MODEL, D_MODEL)]
        _store_rows(x1_ref, r0, mod_rows, chunk,
                    _load_rows(x_ref, r0, mod_rows, chunk) + g1 * mix[r0:r0 + mod_rows])


def _load_rows(ref, r0, n, chunk):
    if len(ref.shape) == 2:
        return ref[pl.ds(r0, n), :]
    assert n == chunk and r0 % chunk == 0
    return ref[r0 // chunk]


def _store_rows(ref, r0, n, chunk, val):
    if len(ref.shape) == 2:
        ref[pl.ds(r0, n), :] = val
    else:
        assert n == chunk and r0 % chunk == 0
        ref[r0 // chunk] = val


def _seq_pitch(chunk):
    return chunk + SUBLANES if chunk % SUBLANES == 0 else chunk


def _const_spec(shape):
    nd = len(shape)
    return pl.BlockSpec(shape, lambda j, _n=nd: (0,) * _n)


def _decay_tables(chunk, tile_rows):
    f32 = np.float32
    log_gamma = np.log1p(-np.exp2(f32(-5.0) - np.arange(RET_HEADS, dtype=f32))).astype(f32)
    r = np.arange(tile_rows)
    seq, loc = r // chunk, (r % chunk).astype(f32)
    rel = loc[:, None] - loc[None, :]
    ok = (seq[:, None] == seq[None, :]) & (rel >= 0)
    dmask = np.where(ok[None], np.exp(np.where(ok, rel, f32(0.0))[None] * log_gamma[:, None, None]), f32(0.0))
    if tile_rows < HEAD_DIM:
        dmask = np.pad(dmask, ((0, 0), (0, 0), (0, HEAD_DIM - tile_rows)))
    cdec = np.exp((loc[None, :] + f32(1.0)) * log_gamma[:, None])
    sdec = np.exp((f32(chunk) - f32(1.0) - loc)[None, :] * log_gamma[:, None])
    bcast = lambda t: np.ascontiguousarray(np.broadcast_to(t[:, :, None], (RET_HEADS, tile_rows, HEAD_DIM)))
    return dmask.astype(f32), bcast(cdec.astype(f32)), bcast(sdec.astype(f32))


def _rope_tables(pos):
    f32 = np.float32
    half = HEAD_DIM // 2
    inv_freq = (f32(ROPE_BASE) ** (-np.arange(half, dtype=f32) / f32(half))).astype(f32)
    ang = (pos.astype(f32)[:, None] * inv_freq[None, :]).astype(f32)
    cos, sin = np.cos(ang).astype(f32), np.sin(ang).astype(f32)
    return np.concatenate([cos, cos], axis=-1), np.concatenate([-sin, sin], axis=-1)


def _mixer(x, mod, pos, states, wts, *, prompt):
    n_seq = SEQ_PER_BLOCK if prompt else SAMPLE_SEQ_PER_BLOCK
    if prompt:
        n_total, seq_len, _ = x.shape
        assert n_total == n_seq
        chunk, tile_rows, n_steps = PROMPT_CHUNK, PROMPT_CHUNK, seq_len // PROMPT_CHUNK
        x_spec = pl.BlockSpec((n_seq, chunk, D_MODEL), lambda j: (0, j, 0))
        mod_spec = pl.BlockSpec((n_seq, 3 * D_MODEL), lambda j: (0, 0))
        tab_spec = pl.BlockSpec((chunk, HEAD_DIM), lambda j: (j, 0))
        seq_map = lambda j: 0
    else:
        chunk = pos.shape[0]
        tile_rows = SUBLANES
        n_total = x.shape[0] // chunk
        n_steps = n_total // n_seq
        x_spec = pl.BlockSpec((n_seq * chunk, D_MODEL), lambda j: (j, 0))
        mod_spec = pl.BlockSpec((n_seq, 3 * D_MODEL), lambda j: (j, 0))
        tab_spec = _const_spec((tile_rows, HEAD_DIM))
        seq_map = lambda j: j
    rows = n_seq * chunk
    cos, sin = _rope_tables(pos)
    if not prompt:
        reps = tile_rows // chunk
        cos, sin = np.tile(cos, (reps, 1)), np.tile(sin, (reps, 1))
    dmask, cdec, sdec = _decay_tables(chunk, tile_rows)
    chunk_decay = tuple(float(math.exp(chunk * math.log1p(-2.0 ** (-5.0 - h)))) for h in range(RET_HEADS))
    sret0, sre0, sim0 = states

    st_ret_spec = pl.BlockSpec((n_seq, RET_HEADS, HEAD_DIM, HEAD_DIM), lambda j: (seq_map(j), 0, 0, 0))
    st_s5_spec = pl.BlockSpec((n_seq, SSM_CH), lambda j: (seq_map(j), 0))
    consts = [dmask, cdec, sdec, wts["rnw"], wts["bmat"], wts["cre"], wts["cim"], wts["lbr"], wts["lbi"],
              wts["dsk"], wts["w_glu"], wts["b_glu"], wts["snw"], wts["w_out"]]
    args = [x, mod, wts["n1w"], wts["w_in"], cos, sin] + consts + [sret0, sre0, sim0]
    in_specs = ([x_spec, mod_spec, _const_spec(wts["n1w"].shape), _const_spec(wts["w_in"].shape), tab_spec, tab_spec]
                + [_const_spec(a.shape) for a in consts] + [st_ret_spec, st_s5_spec, st_s5_spec])

    kern = functools.partial(_mixer_kernel, n_seq=n_seq, chunk=chunk, tile_rows=tile_rows, carry=prompt,
                             chunk_decay=chunk_decay)
    out_shape = (
        jax.ShapeDtypeStruct(x.shape, F32),
        jax.ShapeDtypeStruct((n_total, RET_HEADS, HEAD_DIM, HEAD_DIM), F32),
        jax.ShapeDtypeStruct((n_total, SSM_CH), F32),
        jax.ShapeDtypeStruct((n_total, SSM_CH), F32),
    )
    scratch = [
        pltpu.VMEM((rows, D_MODEL), BF16),
        pltpu.VMEM((rows, 4 * RET_WIDTH), F32),
        pltpu.VMEM((SSM_WIDTH // LANES, n_seq * _seq_pitch(chunk), LANES), F32),
        pltpu.VMEM((rows, D_MODEL), F32),
        pltpu.VMEM((rows, SSM_WIDTH), F32),
        pltpu.VMEM((rows, SSM_CH // 2), F32),
        pltpu.VMEM((rows, SSM_CH // 2), F32),
        pltpu.VMEM((rows, SSM_CH // 2), F32),
        pltpu.VMEM((rows, SSM_CH // 2), F32),
        pltpu.VMEM((rows, SSM_WIDTH), F32),
        pltpu.VMEM((SSM_WIDTH // LANES, n_seq * _seq_pitch(chunk), LANES), F32),
        pltpu.VMEM((3 * D_MODEL // LANES, SUBLANES if prompt else rows, LANES), F32),
    ]
    return pl.pallas_call(
        kern,
        grid=(n_steps,),
        in_specs=in_specs,
        out_specs=(x_spec, st_ret_spec, st_s5_spec, st_s5_spec),
        out_shape=out_shape,
        scratch_shapes=scratch,
        compiler_params=pltpu.CompilerParams(dimension_semantics=("arbitrary",), vmem_limit_bytes=VMEM_LIMIT),
        name="mixer_prompt" if prompt else "mixer_sample",
    )(*args)


PACK_ROWS = D_MODEL // (2 * LANES)


def _store_packed(ref, x):
    half = D_MODEL // 2
    bits = lax.bitcast_convert_type(x.astype(BF16).astype(F32), jnp.uint32)
    words = bits[:, :half] | (bits[:, half:] >> 16)
    for c in range(PACK_ROWS):
        ref[c] = words[:, c * LANES:(c + 1) * LANES]


def _load_packed(ref, n, first_row=0, row_stride=1):
    hi, lo = [], []
    for c in range(PACK_ROWS):
        w = ref[c] if row_stride == 1 else ref[c, pl.ds(first_row, n, stride=row_stride), :]
        hi.append(lax.bitcast_convert_type(w & jnp.uint32(0xFFFF0000), F32))
        lo.append(lax.bitcast_convert_type(w << 16, F32))
    return jnp.concatenate(hi + lo, axis=1)


def _split_bf16(x):
    hi = x.astype(BF16)
    return hi, (x - hi.astype(F32)).astype(BF16)


def _route_tile(x, sh, sc, n2w_ref, wrh_ref, wrm_ref, br_ref, ltri_ref, count_ref, h2_ref, route_ref, topw_ref):
    ms = jnp.mean(x * x, axis=-1, keepdims=True)
    h2 = x * lax.rsqrt(ms + NORM_EPS) * n2w_ref[...] * (1.0 + sc) + sh
    _store_packed(h2_ref, h2)
    hh, hm = _split_bf16(h2)
    logits = (jnp.dot(hh, wrh_ref[...], preferred_element_type=F32)
              + (jnp.dot(hh, wrm_ref[...], preferred_element_type=F32)
                 + jnp.dot(hm, wrh_ref[...], preferred_element_type=F32))) + br_ref[...]
    lane = lax.broadcasted_iota(jnp.int32, logits.shape, 1)
    lane_f = lane.astype(F32)
    work = logits
    vals, idxs = [], []
    for _ in range(TOP_K):
        m = jnp.max(work, axis=-1, keepdims=True)
        idx = jnp.min(jnp.where(work == m, lane_f, float(LANES)), axis=-1, keepdims=True)
        vals.append(m)
        idxs.append(idx)
        work = jnp.where(lane_f == idx, -jnp.inf, work)
    shifted = jnp.full(logits.shape, -jnp.inf, F32)
    for k in range(TOP_K):
        shifted = jnp.where(lane == k, vals[k] - vals[0], shifted)
    expd = jnp.exp(shifted)
    topw_ref[...] = expd / jnp.sum(expd, axis=-1, keepdims=True)

    onehot = [(lane_f == idxs[k]).astype(F32) for k in range(TOP_K)]
    chosen = onehot[0] + onehot[1] + onehot[2] + onehot[3]
    before = jnp.dot(ltri_ref[...], chosen.astype(BF16), preferred_element_type=F32) + count_ref[...]
    info = jnp.zeros(logits.shape, jnp.int32)
    for k in range(TOP_K):
        rank = jnp.sum(onehot[k] * before, axis=-1, keepdims=True).astype(jnp.int32)
        info = jnp.where(lane == k, idxs[k].astype(jnp.int32), info)
        info = jnp.where(lane == TOP_K + k, rank, info)
    route_ref[...] = jnp.transpose(info)[:2 * TOP_K, :]
    count_ref[...] = count_ref[...] + jnp.sum(chosen, axis=0, keepdims=True)


def _router_kernel(xp_ref, shp_ref, scp_ref, xs_ref, shs_ref, scs_ref, n2w_ref, wrh_ref, wrm_ref, br_ref, ltri_ref,
                   h2_ref, route_ref, topw_ref, count_ref, modx_ref, *, n_prompt_tiles, sample_len):
    i = pl.program_id(0)
    rest = (n2w_ref, wrh_ref, wrm_ref, br_ref, ltri_ref, count_ref, h2_ref, route_ref, topw_ref)

    @pl.when(i == 0)
    def _():
        count_ref[...] = jnp.zeros(count_ref.shape, F32)

    @pl.when(i < n_prompt_tiles)
    def _():
        _route_tile(xp_ref[...], shp_ref[0], scp_ref[0], *rest)

    @pl.when(i >= n_prompt_tiles)
    def _():
        dl = D_MODEL // LANES
        _expand_rows(modx_ref, shs_ref[...], sample_len)
        _expand_rows(modx_ref, scs_ref[...], sample_len, first_chunk=dl)
        _route_tile(xs_ref[...], _expanded(modx_ref, 0), _expanded(modx_ref, dl), *rest)


def _router(xp_rows, mod_p, seq_len, xs_rows, mod_s, sample_len, n2w, w_router, b_router):
    tile = ROUTER_TILE
    n_p, n_s = xp_rows.shape[0], xs_rows.shape[0]
    tp, ts = n_p // tile, n_s // tile
    n_total = n_p + n_s
    mod_p = mod_p.reshape(mod_p.shape[0], 1, mod_p.shape[1])
    seq_of = lambda i: (jnp.minimum(i, tp - 1) * tile) // seq_len
    wr_pad = jnp.pad(w_router, ((0, 0), (0, LANES - N_EXPERTS)))
    wr_hi = wr_pad.astype(BF16)
    wr_mid = (wr_pad - wr_hi.astype(F32)).astype(BF16)
    br_pad = jnp.pad(b_router, ((0, 0), (0, LANES - N_EXPERTS)), constant_values=-1e30)
    ltri = jnp.asarray(np.tril(np.ones((tile, tile), np.float32), -1), BF16)
    clamp_p = lambda i: jnp.minimum(i, tp - 1)
    clamp_s = lambda i: jnp.maximum(i - tp, 0)
    return pl.pallas_call(
        functools.partial(_router_kernel, n_prompt_tiles=tp, sample_len=sample_len),
        grid=(tp + ts,),
        in_specs=[pl.BlockSpec((tile, D_MODEL), lambda i: (clamp_p(i), 0)),
                  pl.BlockSpec((1, 1, D_MODEL), lambda i: (seq_of(i), 0, 3)),
                  pl.BlockSpec((1, 1, D_MODEL), lambda i: (seq_of(i), 0, 4)),
                  pl.BlockSpec((tile, D_MODEL), lambda i: (clamp_s(i), 0)),
                  pl.BlockSpec((tile // sample_len, D_MODEL), lambda i: (clamp_s(i), 3)),
                  pl.BlockSpec((tile // sample_len, D_MODEL), lambda i: (clamp_s(i), 4)),
                  _const_spec(n2w.shape), _const_spec(wr_hi.shape), _const_spec(wr_mid.shape),
                  _const_spec(br_pad.shape), _const_spec(ltri.shape)],
        out_specs=(pl.BlockSpec((PACK_ROWS, tile, LANES), lambda i: (0, i, 0)),
                   pl.BlockSpec((2 * TOP_K, tile), lambda i: (0, i)),
                   pl.BlockSpec((tile, LANES), lambda i: (i, 0)),
                   pl.BlockSpec((1, LANES), lambda i: (0, 0))),
        out_shape=(jax.ShapeDtypeStruct((PACK_ROWS, n_total, LANES), jnp.uint32),
                   jax.ShapeDtypeStruct((2 * TOP_K, n_total), jnp.int32),
                   jax.ShapeDtypeStruct((n_total, LANES), F32),
                   jax.ShapeDtypeStruct((1, LANES), F32)),
        scratch_shapes=[pltpu.VMEM((2 * D_MODEL // LANES, tile, LANES), F32)],
        compiler_params=pltpu.CompilerParams(dimension_semantics=("arbitrary",), vmem_limit_bytes=VMEM_LIMIT),
        name="router",
    )(xp_rows, mod_p, mod_p, xs_rows, mod_s, mod_s, n2w, wr_hi, wr_mid, br_pad, ltri)


def _gather_rows(table, idx):
    n = idx.shape[0]
    steps = n // SC_GATHER_WINDOW
    assert n % SC_GATHER_WINDOW == 0 and steps % SC_WORKERS == 0
    mesh = plsc.VectorSubcoreMesh(core_axis_name="c", subcore_axis_name="s")

    @functools.partial(pl.kernel, out_type=jax.ShapeDtypeStruct((n, table.shape[1]), table.dtype), mesh=mesh,
                       scratch_types=[])
    def gather_kernel(table_hbm, idx_hbm, out_hbm):
        def body(idx_vmem, out_vmem):
            pltpu.sync_copy(table_hbm.at[idx_vmem.at[0]], out_vmem)

        pltpu.emit_pipeline(
            body,
            grid=(steps,),
            in_specs=[pl.BlockSpec((1, SC_GATHER_WINDOW), lambda i: (0, i))],
            out_specs=[pl.BlockSpec((SC_GATHER_WINDOW, table.shape[1]), lambda i: (i, 0))],
            core_axis_name=("c", "s"),
            dimension_semantics=(pltpu.PARALLEL,),
        )(idx_hbm, out_hbm)

    return gather_kernel(table, idx.reshape(1, n))


def _dispatch_packed(table, slot_kt, n_slots):
    planes, n_tok, lanes = table.shape
    win = SC_GATHER_WINDOW
    blocks = n_tok // win
    assert n_tok % win == 0
    blocks_pad = _round_up(blocks, SC_WORKERS // math.gcd(SC_WORKERS, planes))
    n_spare = (blocks_pad - blocks) * TOP_K * win
    n_ext = n_slots + n_spare
    dest = jnp.transpose(slot_kt.reshape(TOP_K, blocks, win), (1, 0, 2))
    spare = n_slots + jnp.arange(n_spare, dtype=jnp.int32).reshape(blocks_pad - blocks, TOP_K, win)
    dest = jnp.concatenate([dest, spare], axis=0)[None] + (jnp.arange(planes, dtype=jnp.int32) * n_ext)[:, None, None, None]
    dest = dest.reshape(planes * blocks_pad * TOP_K, win)
    steps = planes * blocks_pad
    src_block = lambda g: (g // blocks_pad) * blocks + jnp.minimum(g % blocks_pad, blocks - 1)
    mesh = plsc.VectorSubcoreMesh(core_axis_name="c", subcore_axis_name="s")

    @functools.partial(pl.kernel, out_type=jax.ShapeDtypeStruct((planes * n_ext, lanes), table.dtype), mesh=mesh,
                       scratch_types=[])
    def scatter_kernel(table_hbm, dest_hbm, out_hbm):
        def body(rows_vmem, dest_vmem):
            for k in range(TOP_K):
                pltpu.sync_copy(rows_vmem, out_hbm.at[dest_vmem.at[k]])

        pltpu.emit_pipeline(
            body,
            grid=(steps,),
            in_specs=[pl.BlockSpec((win, lanes), lambda g: (src_block(g), 0)),
                      pl.BlockSpec((TOP_K, win), lambda g: (g, 0))],
            out_specs=[],
            core_axis_name=("c", "s"),
            dimension_semantics=(pltpu.PARALLEL,),
        )(table_hbm, dest_hbm)

    return scatter_kernel(table.reshape(planes * n_tok, lanes), dest).reshape(planes, n_ext, lanes)


def _gather_packed(table, rows):
    planes, n_table, lanes = table.shape
    idx = jnp.concatenate([rows + c * n_table for c in range(planes)])
    out = _gather_rows(table.reshape(planes * n_table, lanes), idx)
    return out.reshape(planes, rows.shape[0], lanes)


def _expert_weight_copies(e, w1_hbm, w2_hbm, w1s_ref, w2s_ref, sem):
    return (pltpu.make_async_copy(w1_hbm.at[e], w1s_ref, sem.at[0]),
            pltpu.make_async_copy(w2_hbm.at[e], w2s_ref, sem.at[1]))


def _experts_kernel(te_ref, tr_ref, nx_ref, nv_ref, xs_ref, w1_hbm, b1_ref, w2_hbm, b2_ref, ys_ref,
                    w1s_ref, w2s_ref, w1b_ref, w2b_ref, sem):
    i = pl.program_id(0)
    e = te_ref[i]
    new_expert = (i == 0) | (e != te_ref[jnp.maximum(i - 1, 0)])
    copies = functools.partial(_expert_weight_copies, w1_hbm=w1_hbm, w2_hbm=w2_hbm, w1s_ref=w1s_ref,
                               w2s_ref=w2s_ref, sem=sem)

    @pl.when(i == 0)
    def _():
        for c in copies(e):
            c.start()

    @pl.when(new_expert)
    def _():
        for c in copies(e):
            c.wait()
        w1b_ref[...] = w1s_ref[...].astype(BF16)
        w2b_ref[...] = w2s_ref[...].astype(BF16)

        @pl.when(nx_ref[i] >= 0)
        def _():
            for c in copies(nx_ref[i]):
                c.start()

    def expert_pass(r0, n_rows, rows_valid):
        xs_v, ys_v = xs_ref.at[:, pl.ds(r0, n_rows)], ys_ref.at[:, pl.ds(r0, n_rows)]
        row = lax.broadcasted_iota(jnp.int32, (n_rows, D_MODEL), 0)
        x = jnp.where(row < rows_valid, _load_packed(xs_v, n_rows), 0.0).astype(BF16)
        hu = jnp.dot(x, w1b_ref[...], preferred_element_type=F32) + b1_ref[0]
        x_glu = jnp.minimum(hu[:, :D_FF], SWIGLU_LIMIT)
        x_lin = jnp.clip(hu[:, D_FF:], -SWIGLU_LIMIT, SWIGLU_LIMIT)
        act = x_glu * _sigmoid(SWIGLU_ALPHA * x_glu) * (x_lin + 1.0)
        _store_packed(ys_v, jnp.dot(act.astype(BF16), w2b_ref[...], preferred_element_type=F32) + b2_ref[0])

    def zero_rows(r0, n_rows):
        for c in range(PACK_ROWS):
            ys_ref[c, pl.ds(r0, n_rows), :] = jnp.zeros((n_rows, LANES), jnp.uint32)

    half = EXPERT_ROWS // 2
    rows_tile = jnp.where(i < nv_ref[0], tr_ref[i], 0)
    full_tile = rows_tile > SLOT_TILE - half

    @pl.when(full_tile)
    def _():
        expert_pass(0, SLOT_TILE, rows_tile)

    for h in range(SLOT_TILE // EXPERT_ROWS):
        r0 = h * EXPERT_ROWS
        rows_here = jnp.where(full_tile, -1, rows_tile - r0)

        @pl.when(rows_here > half)
        def _(r0=r0, rows_here=rows_here):
            expert_pass(r0, EXPERT_ROWS, rows_here)

        @pl.when((rows_here > 0) & (rows_here <= half))
        def _(r0=r0, rows_here=rows_here):
            expert_pass(r0, half, rows_here)
            zero_rows(r0 + half, half)

        @pl.when((rows_here <= 0) & jnp.logical_not(full_tile))
        def _(r0=r0):
            zero_rows(r0, EXPERT_ROWS)


def _experts(tile_expert, tile_rows, next_expert, n_valid, xs, w1, b1, w2, b2):
    n_tiles = tile_expert.shape[0]
    n_slots = n_tiles * SLOT_TILE
    grid_spec = pltpu.PrefetchScalarGridSpec(
        num_scalar_prefetch=4,
        grid=(n_tiles,),
        in_specs=[
            pl.BlockSpec((PACK_ROWS, SLOT_TILE, LANES), lambda i, te, tr, nx, nv: (0, i, 0)),
            pl.BlockSpec(memory_space=pl.ANY),
            pl.BlockSpec((1, 1, 2 * D_FF), lambda i, te, tr, nx, nv: (te[i], 0, 0)),
            pl.BlockSpec(memory_space=pl.ANY),
            pl.BlockSpec((1, 1, D_MODEL), lambda i, te, tr, nx, nv: (te[i], 0, 0)),
        ],
        out_specs=pl.BlockSpec((PACK_ROWS, SLOT_TILE, LANES), lambda i, te, tr, nx, nv: (0, i, 0)),
        scratch_shapes=[pltpu.VMEM((D_MODEL, 2 * D_FF), F32), pltpu.VMEM((D_FF, D_MODEL), F32),
                        pltpu.VMEM((D_MODEL, 2 * D_FF), BF16), pltpu.VMEM((D_FF, D_MODEL), BF16),
                        pltpu.SemaphoreType.DMA((2,))],
    )
    return pl.pallas_call(
        _experts_kernel,
        grid_spec=grid_spec,
        out_shape=jax.ShapeDtypeStruct((PACK_ROWS, n_slots, LANES), jnp.uint32),
        compiler_params=pltpu.CompilerParams(dimension_semantics=("arbitrary",), vmem_limit_bytes=VMEM_LIMIT),
        name="experts",
    )(tile_expert, tile_rows, next_expert, n_valid, xs, w1, b1.reshape(N_EXPERTS, 1, 2 * D_FF), w2,
      b2.reshape(N_EXPERTS, 1, D_MODEL))


def _combine_kernel(y4_ref, x1_ref, topw_ref, g2_ref, fw_ref, *rest, reps):
    o_ref = rest[-2] if reps else rest[-1]
    w = topw_ref[...]
    n = w.shape[0]
    ff = None
    for k in range(TOP_K):
        yk = w[:, k:k + 1] * _load_packed(y4_ref.at[:, k], n)
        ff = yk if ff is None else ff + yk
    if reps:
        _expand_rows(rest[-1], g2_ref[...], reps)
        g2 = _expanded(rest[-1], 0)
    else:
        g2 = g2_ref[0]
    x = x1_ref[...] + g2 * ff
    ms = jnp.mean(x * x, axis=-1, keepdims=True)
    o_ref[...] = x * lax.rsqrt(ms + NORM_EPS) * fw_ref[...]


def _combine(y4, y4_row0, x1, x1_row0, n_rows, topw, topw_row0, mod, rows_per_mod, fw, out_buf):
    tile = COMBINE_TILE
    y4_off, x1_off, tw_off = y4_row0 // tile, x1_row0 // tile, topw_row0 // tile
    reps = rows_per_mod if rows_per_mod < tile else 0
    scratch = []
    if reps:
        g2_spec = pl.BlockSpec((tile // reps, D_MODEL), lambda i: (i + x1_off, 5))
        scratch = [pltpu.VMEM((D_MODEL // LANES, tile, LANES), F32)]
    else:
        mod = mod.reshape(mod.shape[0], 1, mod.shape[1])
        g2_spec = pl.BlockSpec((1, 1, D_MODEL), lambda i: (((i + x1_off) * tile) // rows_per_mod, 0, 5))
    in_specs = [pl.BlockSpec((PACK_ROWS, TOP_K, tile, LANES), lambda i: (0, 0, i + y4_off, 0)),
                pl.BlockSpec((tile, D_MODEL), lambda i: (i + x1_off, 0)),
                pl.BlockSpec((tile, LANES), lambda i: (i + tw_off, 0)),
                g2_spec, _const_spec(fw.shape)]
    args = [y4, x1, topw, mod, fw]
    aliases = {}
    if out_buf is not None:
        in_specs.append(pl.BlockSpec(memory_space=pl.ANY))
        args.append(out_buf)
        aliases = {len(args) - 1: 0}
    return pl.pallas_call(
        functools.partial(_combine_kernel, reps=reps),
        grid=(n_rows // tile,),
        in_specs=in_specs,
        out_specs=pl.BlockSpec((tile, D_MODEL), lambda i: (i + x1_off, 0)),
        out_shape=jax.ShapeDtypeStruct(x1.shape, F32),
        input_output_aliases=aliases,
        scratch_shapes=scratch,
        compiler_params=pltpu.CompilerParams(dimension_semantics=("arbitrary",), vmem_limit_bytes=VMEM_LIMIT),
        name="combine",
    )(*args)


def _routing_tables(route, counts, n_slots):
    padded = ((counts + SLOT_TILE - 1) // SLOT_TILE) * SLOT_TILE
    pend = jnp.cumsum(padded)
    poff = pend - padded
    expert_kt, rank_kt = route[:TOP_K], route[TOP_K:]
    experts = jnp.arange(N_EXPERTS, dtype=jnp.int32)
    start_kt = jnp.sum((expert_kt[None] == experts[:, None, None]).astype(jnp.int32) * poff[:, None, None], axis=0)
    slot_kt = start_kt + rank_kt
    n_tiles = n_slots // SLOT_TILE
    n_valid = (pend[-1] // SLOT_TILE).astype(jnp.int32)
    tile_row = jnp.minimum(jnp.arange(n_tiles, dtype=jnp.int32), n_valid - 1) * SLOT_TILE
    in_later = (pend[None, :] <= tile_row[:, None]).astype(jnp.int32)
    tile_e = jnp.sum(in_later, axis=1).astype(jnp.int32)
    is_e = (experts[None, :] == tile_e[:, None]).astype(jnp.int32)
    used_end = jnp.sum(is_e * (poff + counts)[None, :], axis=1)
    tile_rows = jnp.clip(used_end - tile_row, 0, SLOT_TILE).astype(jnp.int32)
    later_used = (experts[None, :] > tile_e[:, None]) & (counts[None, :] > 0)
    next_e = jnp.min(jnp.where(later_used, experts[None, :], N_EXPERTS), axis=1)
    next_e = jnp.where(next_e < N_EXPERTS, next_e, -1).astype(jnp.int32)
    return slot_kt, tile_e, tile_rows, next_e, n_valid.reshape(1)


def _round_up(n, m):
    return ((n + m - 1) // m) * m


def kernel(x_prompt, x_sample, c_prompt, c_sample, state_ret, state_s5_re, state_s5_im, norm1_w, norm2_w, w_ada, b_ada, w_in, ret_norm_w, s5_lam_re, s5_lam_im, s5_log_dt, s5_b_re, s5_b_im, s5_c_re, s5_c_im, s5_d, w_glu, b_glu, s5_norm_w, w_out, w_router, b_router, w1, b1, w2, b2, final_w):
    bp, lp, _ = x_prompt.shape
    bs, ls, _ = x_sample.shape
    assert norm1_w.shape[0] == 1, "single-layer model"
    n_p, n_s = bp * lp, bs * ls
    n_tok = n_p + n_s

    mod = _ada(jnp.concatenate([c_prompt, c_sample], axis=0), w_ada[0], b_ada[0])
    mod_p, mod_s = mod[:bp], mod[bp:]

    lbr, lbi, bbr, bbi = _s5prep(s5_lam_re[0], s5_lam_im[0], s5_log_dt[0], s5_b_re[0], s5_b_im[0])
    bmat = jnp.concatenate([_block_diag(bbr), _block_diag(bbi)], axis=-1).astype(BF16)
    cre = _block_diag(jnp.transpose(s5_c_re[0], (0, 2, 1))).astype(BF16)
    cim = _block_diag(jnp.transpose(-s5_c_im[0], (0, 2, 1))).astype(BF16)
    wts = dict(
        n1w=norm1_w, w_in=w_in[0].astype(BF16), rnw=ret_norm_w, bmat=bmat, cre=cre, cim=cim,
        lbr=lbr.reshape(1, SSM_CH), lbi=lbi.reshape(1, SSM_CH), dsk=s5_d[0].reshape(1, SSM_WIDTH),
        w_glu=w_glu[0].astype(BF16), b_glu=b_glu, snw=s5_norm_w, w_out=w_out[0].astype(BF16),
    )

    zero_states = (jnp.zeros((bp, RET_HEADS, HEAD_DIM, HEAD_DIM), F32), jnp.zeros((bp, SSM_CH), F32),
                   jnp.zeros((bp, SSM_CH), F32))
    x1_p, ret_p, re_p, im_p = _mixer(x_prompt, mod_p, np.arange(lp, dtype=np.float32), zero_states, wts,
                                     prompt=True)
    sample_states = (state_ret[0], state_s5_re[0].reshape(bs, SSM_CH), state_s5_im[0].reshape(bs, SSM_CH))
    x1_s, ret_s, re_s, im_s = _mixer(x_sample.reshape(n_s, D_MODEL), mod_s,
                                     PAST_LEN + np.arange(ls, dtype=np.float32), sample_states, wts, prompt=False)

    x1_p_rows = x1_p.reshape(n_p, D_MODEL)
    h2, route, topw, counts = _router(x1_p_rows, mod_p, lp, x1_s, mod_s, ls, norm2_w, w_router[0], b_router)

    n_assign = n_tok * TOP_K
    gather_quantum = SC_GATHER_WINDOW * SC_WORKERS // PACK_ROWS
    assert n_assign % gather_quantum == 0
    n_slots = _round_up(_round_up(n_assign, SLOT_TILE) + N_EXPERTS * SLOT_TILE, gather_quantum)
    slot_kt, tile_e, tile_rows, next_e, n_valid = _routing_tables(route, counts[0, :N_EXPERTS].astype(jnp.int32),
                                                                  n_slots)
    xs = _dispatch_packed(h2, slot_kt, n_slots)
    ys = _experts(tile_e, tile_rows, next_e, n_valid, xs, w1[0], b1[0], w2[0], b2[0])
    fw = final_w.reshape(1, D_MODEL)
    y_p, y_s = None, None
    bounds = [r * (n_p // COMBINE_RANGES) for r in range(COMBINE_RANGES)] + [n_tok]
    for lo, hi in zip(bounds[:-1], bounds[1:]):
        y4 = _gather_packed(ys, slot_kt[:, lo:hi].reshape(-1)).reshape(PACK_ROWS, TOP_K, hi - lo, LANES)
        y_p = _combine(y4, 0, x1_p_rows, lo, min(hi, n_p) - lo, topw, lo, mod_p, lp, fw, y_p)
        if hi > n_p:
            y_s = _combine(y4, n_p - lo, x1_s, 0, n_s, topw, n_p, mod_s, ls, fw, None)

    g, p = SSM_GROUPS, SSM_STATE
    return (y_p.reshape(bp, lp, D_MODEL), y_s.reshape(bs, ls, D_MODEL),
            ret_p[None], re_p.reshape(1, bp, g, p), im_p.reshape(1, bp, g, p),
            ret_s[None], re_s.reshape(1, bs, g, p), im_s.reshape(1, bs, g, p))
```
